```python
import numpy as np
import jax
import jax.numpy as jnp
from jax import lax

D_MODEL = 1024
BATCH = 8
SEQ = 4096
DEPTH = 2

CONV_CH = D_MODEL // 2
CONV_WIDTH = 31
NSA_HEADS = 8
NSA_KV_HEADS = 2
HEAD_DIM = (D_MODEL // 2) // NSA_HEADS
CMP_BLOCK = 32
CMP_STRIDE = 16
CMP_HIDDEN = 256
SEL_BLOCK = 64
SEL_TOP_N = 16
WINDOW = 512
Q_BLOCK = 128
FORCE_SCORE = 1e4
SHORT_CONV_WIDTH = 3
MEM_LEN = 256
XA_HEADS = 4
XA_HEAD_DIM = D_MODEL // XA_HEADS
N_GROUPS = 4
EXPERTS_PER_GROUP = 8
N_EXPERTS = N_GROUPS * EXPERTS_PER_GROUP
EXPERT_HIDDEN = D_MODEL // 2
EXPERT_TOP_K = 2
DISPATCH_ROWS = 256
DN_ALPHA = (2 * DEPTH) ** 0.25
DN_BETA = (8 * DEPTH) ** -0.25
LN_EPS = 1e-5
NEG_INF = -1e30

KV_COLS = NSA_KV_HEADS * HEAD_DIM
EVEN_SPLIT_SIZES = (CONV_CH, CONV_CH, NSA_HEADS * HEAD_DIM, KV_COLS, KV_COLS, KV_COLS, KV_COLS, KV_COLS, KV_COLS, 3 * NSA_HEADS)
EVEN_IN_COLS = sum(EVEN_SPLIT_SIZES)

kernel_name = 'hybrid_conformer_nsa_shortconv_hmoe'


def layer_norm(x, g, b):
    xf = x.astype(jnp.float32)
    mu = jnp.mean(xf, -1, keepdims=True)
    var = jnp.mean(jnp.square(xf - mu), -1, keepdims=True)
    y = (xf - mu) * lax.rsqrt(var + LN_EPS)
    return (y * g.astype(jnp.float32) + b.astype(jnp.float32)).astype(x.dtype)


def causal_depthwise_conv(u, w):
    k = w.shape[0]
    return lax.conv_general_dilated(u, w[:, None, :].astype(u.dtype), window_strides=(1,), padding=((k - 1, 0),), dimension_numbers=('NWC', 'WIO', 'NWC'), feature_group_count=u.shape[-1])


def masked_softmax(s, mask):
    p = jax.nn.softmax(jnp.where(mask, s, NEG_INF), axis=-1)
    return jnp.where(mask, p, 0.0)


def compress_blocks(kv, pe, w1, w2):
    b, s, g, dh = kv.shape
    n_cmp = (s - CMP_BLOCK) // CMP_STRIDE + 1
    idx = np.arange(n_cmp)[:, None] * CMP_STRIDE + np.arange(CMP_BLOCK)[None, :]
    blocks = kv[:, idx] + pe[None, None, :, None, :].astype(kv.dtype)
    blocks = jnp.transpose(blocks, (0, 1, 3, 2, 4)).reshape(b, n_cmp, g, CMP_BLOCK * dh)
    return jax.nn.gelu(blocks @ w1) @ w2


def cmp_to_sel_matrix(n_cmp, n_sel):
    c0 = np.arange(n_cmp) * CMP_STRIDE
    s0 = np.arange(n_sel) * SEL_BLOCK
    ov = np.minimum(c0[:, None] + CMP_BLOCK, s0[None, :] + SEL_BLOCK) - np.maximum(c0[:, None], s0[None, :])
    return (np.clip(ov, 0, None) / CMP_BLOCK).astype(np.float32)


def nsa_attention(q, k_cmp, v_cmp, k_sel, v_sel, k_win, v_win, gates):
    b, s, h, dh = q.shape
    g = NSA_KV_HEADS
    hg = h // g
    n_cmp = k_cmp.shape[1]
    n_sel = s // SEL_BLOCK
    top_n = min(SEL_TOP_N, n_sel)
    scale = dh ** -0.5
    m_sel = jnp.asarray(cmp_to_sel_matrix(n_cmp, n_sel))
    cmp_end = jnp.arange(n_cmp) * CMP_STRIDE + CMP_BLOCK - 1
    blk_ids = jnp.arange(n_sel)
    ks_blk = k_sel.reshape(b, n_sel, SEL_BLOCK, g, dh).transpose(0, 3, 1, 2, 4)
    vs_blk = v_sel.reshape(b, n_sel, SEL_BLOCK, g, dh).transpose(0, 3, 1, 2, 4)
    zpad = jnp.zeros((b, WINDOW, g, dh), k_win.dtype)
    kw_pad = jnp.concatenate([zpad, k_win], axis=1)
    vw_pad = jnp.concatenate([zpad, v_win], axis=1)
    bi = jnp.arange(b)[:, None, None, None]
    gi = jnp.arange(g)[None, :, None, None]
    f32 = jnp.float32

    def query_block(c):
        q0 = c * Q_BLOCK
        t = q0 + jnp.arange(Q_BLOCK)
        qc = lax.dynamic_slice_in_dim(q, q0, Q_BLOCK, axis=1).reshape(b, Q_BLOCK, g, hg, dh)
        gc = lax.dynamic_slice_in_dim(gates, q0, Q_BLOCK, axis=1).reshape(b, Q_BLOCK, g, hg, 3)
        s_c = jnp.einsum('bqghd,bngd->bghqn', qc, k_cmp, preferred_element_type=f32) * scale
        p_c = masked_softmax(s_c, cmp_end[None, :] <= t[:, None])
        o_c = jnp.einsum('bghqn,bngd->bqghd', p_c.astype(v_cmp.dtype), v_cmp)
        imp = jnp.einsum('bghqn,nm->bgqm', p_c, m_sel)
        cur = t // SEL_BLOCK
        forced = (blk_ids[None, :] == 0) | (blk_ids[None, :] == cur[:, None]) | (blk_ids[None, :] == cur[:, None] - 1)
        score = jnp.where(forced, FORCE_SCORE, imp)
        score = jnp.where(blk_ids[None, :] * SEL_BLOCK <= t[:, None], score, NEG_INF)
        top_score, top_idx = lax.top_k(score, top_n)
        blk_ok = top_score > 0.5 * NEG_INF
        k_g = ks_blk[bi, gi, top_idx]
        v_g = vs_blk[bi, gi, top_idx]
        s_s = jnp.einsum('bqghd,bgqnld->bghqnl', qc, k_g, preferred_element_type=f32) * scale
        key_pos = top_idx[..., None] * SEL_BLOCK + jnp.arange(SEL_BLOCK)
        mask_s = (blk_ok[..., None] & (key_pos <= t[None, None, :, None, None]))[:, :, None]
        p_s = masked_softmax(s_s.reshape(b, g, hg, Q_BLOCK, top_n * SEL_BLOCK), mask_s.reshape(b, g, 1, Q_BLOCK, top_n * SEL_BLOCK))
        p_s = p_s.reshape(b, g, hg, Q_BLOCK, top_n, SEL_BLOCK).astype(v_g.dtype)
        o_s = jnp.einsum('bghqnl,bgqnld->bqghd', p_s, v_g)
        kw = lax.dynamic_slice_in_dim(kw_pad, q0, WINDOW + Q_BLOCK, axis=1)
        vw = lax.dynamic_slice_in_dim(vw_pad, q0, WINDOW + Q_BLOCK, axis=1)
        kpos = q0 - WINDOW + jnp.arange(WINDOW + Q_BLOCK)
        mask_w = (kpos[None, :] <= t[:, None]) & (kpos[None, :] > t[:, None] - WINDOW) & (kpos[None, :] >= 0)
        s_w = jnp.einsum('bqghd,bkgd->bghqk', qc, kw, preferred_element_type=f32) * scale
        p_w = masked_softmax(s_w, mask_w)
        o_w = jnp.einsum('bghqk,bkgd->bqghd', p_w.astype(vw.dtype), vw)
        o = gc[..., 0:1] * o_c + gc[..., 1:2] * o_s + gc[..., 2:3] * o_w
        return o.reshape(b, Q_BLOCK, h * dh)

    out = lax.map(query_block, jnp.arange(s // Q_BLOCK))
    return jnp.transpose(out, (1, 0, 2, 3)).reshape(b, s, h * dh)


def even_mixer(x, w_in, conv_w, conv_b, cn_g, cn_b, pe_k, w1_k, w2_k, pe_v, w1_v, w2_v, w_out):
    b, s, _ = x.shape
    split_pts = np.cumsum(EVEN_SPLIT_SIZES)[:-1].tolist()
    a_val, a_gate, q, kc, vc, ks, vs, kw, vw, gl = jnp.split(x @ w_in, split_pts, axis=-1)
    a = a_val * jax.nn.sigmoid(a_gate)
    a = causal_depthwise_conv(a, conv_w) + conv_b.astype(a.dtype)
    a = jax.nn.silu(layer_norm(a, cn_g, cn_b))
    kvs = lambda u: u.reshape(b, s, NSA_KV_HEADS, HEAD_DIM)
    k_cmp = compress_blocks(kvs(kc), pe_k, w1_k, w2_k)
    v_cmp = compress_blocks(kvs(vc), pe_v, w1_v, w2_v)
    gates = jax.nn.sigmoid(gl).reshape(b, s, NSA_HEADS, 3)
    o = nsa_attention(q.reshape(b, s, NSA_HEADS, HEAD_DIM), k_cmp, v_cmp, kvs(ks), kvs(vs), kvs(kw), kvs(vw), gates)
    return jnp.concatenate([a, o], axis=-1) @ w_out


def odd_mixer(x, w_in, conv_w, w_out):
    gate_b, gate_c, h = jnp.split(x @ w_in, 3, axis=-1)
    return (gate_b * causal_depthwise_conv(gate_c * h, conv_w)) @ w_out


def memory_cross_attention(x, mem_k, mem_v, wq, wo):
    b, s, d = x.shape
    q = (x @ wq).reshape(b, s, XA_HEADS, XA_HEAD_DIM)
    sc = jnp.einsum('bshd,bmhd->bhsm', q, mem_k, preferred_element_type=jnp.float32) * XA_HEAD_DIM ** -0.5
    p = jax.nn.softmax(sc, axis=-1).astype(x.dtype)
    return jnp.einsum('bhsm,bmhd->bshd', p, mem_v).reshape(b, s, d) @ wo


def hierarchical_moe(x, wg, bg, we, be, w1, w3, w2):
    b, s, d = x.shape
    n_tok = b * s
    xt = x.reshape(n_tok, d)
    xf = xt.astype(jnp.float32)
    g_logits = xf @ wg.astype(jnp.float32) + bg.astype(jnp.float32)
    g_star = jnp.argmax(g_logits, axis=-1)
    p_group = jnp.take_along_axis(jax.nn.softmax(g_logits, axis=-1), g_star[:, None], axis=-1)
    e_logits = jnp.einsum('td,dge->tge', xf, we.astype(jnp.float32)) + be.astype(jnp.float32)
    e_logits = jnp.take_along_axis(e_logits, g_star[:, None, None], axis=1)[:, 0]
    top_val, top_idx = lax.top_k(e_logits, EXPERT_TOP_K)
    gate = (p_group * jax.nn.softmax(top_val, axis=-1)).reshape(-1)
    expert = (g_star[:, None] * EXPERTS_PER_GROUP + top_idx).reshape(-1).astype(jnp.int32)
    token = jnp.repeat(jnp.arange(n_tok, dtype=jnp.int32), EXPERT_TOP_K)
    n_assign = n_tok * EXPERT_TOP_K
    n_blocks = n_assign // DISPATCH_ROWS + N_EXPERTS + 1
    order = jnp.argsort(expert)
    e_sorted = expert[order]
    tok_sorted = token[order]
    gate_sorted = gate[order]
    counts = jnp.bincount(expert, length=N_EXPERTS)
    start = jnp.cumsum(counts) - counts
    padded = (counts + DISPATCH_ROWS - 1) // DISPATCH_ROWS * DISPATCH_ROWS
    pad_end = jnp.cumsum(padded)
    pad_start = pad_end - padded
    dest = pad_start[e_sorted] + jnp.arange(n_assign) - start[e_sorted]
    x_disp = jnp.zeros((n_blocks * DISPATCH_ROWS, d), x.dtype).at[dest].set(xt[tok_sorted])
    block_expert = jnp.minimum(jnp.searchsorted(pad_end, jnp.arange(n_blocks) * DISPATCH_ROWS, side='right'), N_EXPERTS - 1)

    def expert_block(args):
        xb, e = args
        return (jax.nn.silu(xb @ w1[e]) * (xb @ w3[e])) @ w2[e]

    y_disp = lax.map(expert_block, (x_disp.reshape(n_blocks, DISPATCH_ROWS, d), block_expert))
    y = y_disp.reshape(-1, d)[dest] * gate_sorted[:, None].astype(x.dtype)
    return jax.ops.segment_sum(y, tok_sorted, num_segments=n_tok).reshape(b, s, d)


def setup_inputs(seed: int = 0) -> dict:
    key = jax.random.key(seed)
    keys = list(jax.random.split(key, 40))

    def nrm(shape, scale):
        return jax.random.normal(keys.pop(), shape, jnp.float32) * scale

    def gain(shape):
        return 1.0 + nrm(shape, 0.02)

    d = D_MODEL
    ne = (DEPTH + 1) // 2
    no = DEPTH // 2
    return {
        'x': nrm((BATCH, SEQ, d), 1.0),
        'mem': nrm((BATCH, MEM_LEN, d), 1.0),
        'mem_wk': nrm((d, d), d ** -0.5),
        'mem_wv': nrm((d, d), d ** -0.5),
        'ev_w_in': nrm((ne, d, EVEN_IN_COLS), d ** -0.5),
        'ev_conv_w': nrm((ne, CONV_WIDTH, CONV_CH), CONV_WIDTH ** -0.5),
        'ev_conv_b': nrm((ne, CONV_CH), 0.02),
        'ev_cnorm_g': gain((ne, CONV_CH)),
        'ev_cnorm_b': nrm((ne, CONV_CH), 0.02),
        'ev_cmp_pe_k': nrm((ne, CMP_BLOCK, HEAD_DIM), 0.02),
        'ev_cmp_w1_k': nrm((ne, CMP_BLOCK * HEAD_DIM, CMP_HIDDEN), (CMP_BLOCK * HEAD_DIM) ** -0.5),
        'ev_cmp_w2_k': nrm((ne, CMP_HIDDEN, HEAD_DIM), CMP_HIDDEN ** -0.5),
        'ev_cmp_pe_v': nrm((ne, CMP_BLOCK, HEAD_DIM), 0.02),
        'ev_cmp_w1_v': nrm((ne, CMP_BLOCK * HEAD_DIM, CMP_HIDDEN), (CMP_BLOCK * HEAD_DIM) ** -0.5),
        'ev_cmp_w2_v': nrm((ne, CMP_HIDDEN, HEAD_DIM), CMP_HIDDEN ** -0.5),
        'ev_w_out': nrm((ne, CONV_CH + NSA_HEADS * HEAD_DIM, d), (CONV_CH + NSA_HEADS * HEAD_DIM) ** -0.5 * DN_BETA),
        'od_w_in': nrm((no, d, 3 * d), d ** -0.5),
        'od_conv_w': nrm((no, SHORT_CONV_WIDTH, d), SHORT_CONV_WIDTH ** -0.5),
        'od_w_out': nrm((no, d, d), d ** -0.5 * DN_BETA),
        'ln_mix_g': gain((DEPTH, d)),
        'ln_mix_b': nrm((DEPTH, d), 0.02),
        'xa_wq': nrm((DEPTH, d, d), d ** -0.5),
        'xa_wo': nrm((DEPTH, d, d), d ** -0.5 * DN_BETA),
        'ln_xa_g': gain((DEPTH, d)),
        'ln_xa_b': nrm((DEPTH, d), 0.02),
        'moe_wg': nrm((DEPTH, d, N_GROUPS), d ** -0.5),
        'moe_bg': nrm((DEPTH, N_GROUPS), 0.01),
        'moe_we': nrm((DEPTH, d, N_GROUPS, EXPERTS_PER_GROUP), d ** -0.5),
        'moe_be': nrm((DEPTH, N_GROUPS, EXPERTS_PER_GROUP), 0.01),
        'moe_w1': nrm((DEPTH, N_EXPERTS, d, EXPERT_HIDDEN), d ** -0.5),
        'moe_w3': nrm((DEPTH, N_EXPERTS, d, EXPERT_HIDDEN), d ** -0.5),
        'moe_w2': nrm((DEPTH, N_EXPERTS, EXPERT_HIDDEN, d), EXPERT_HIDDEN ** -0.5 * DN_BETA),
        'ln_ffn_g': gain((DEPTH, d)),
        'ln_ffn_b': nrm((DEPTH, d), 0.02),
    }


def reference(x, mem, mem_wk, mem_wv, ev_w_in, ev_conv_w, ev_conv_b, ev_cnorm_g, ev_cnorm_b, ev_cmp_pe_k, ev_cmp_w1_k, ev_cmp_w2_k, ev_cmp_pe_v, ev_cmp_w1_v, ev_cmp_w2_v, ev_w_out, od_w_in, od_conv_w, od_w_out, ln_mix_g, ln_mix_b, xa_wq, xa_wo, ln_xa_g, ln_xa_b, moe_wg, moe_bg, moe_we, moe_be, moe_w1, moe_w3, moe_w2, ln_ffn_g, ln_ffn_b):
    b = x.shape[0]
    m_len = mem.shape[1]
    mem_k = (mem @ mem_wk).reshape(b, m_len, XA_HEADS, XA_HEAD_DIM)
    mem_v = (mem @ mem_wv).reshape(b, m_len, XA_HEADS, XA_HEAD_DIM)
    for layer in range(DEPTH):
        i = layer // 2
        if layer % 2 == 0:
            mix = even_mixer(x, ev_w_in[i], ev_conv_w[i], ev_conv_b[i], ev_cnorm_g[i], ev_cnorm_b[i], ev_cmp_pe_k[i], ev_cmp_w1_k[i], ev_cmp_w2_k[i], ev_cmp_pe_v[i], ev_cmp_w1_v[i], ev_cmp_w2_v[i], ev_w_out[i])
        else:
            mix = odd_mixer(x, od_w_in[i], od_conv_w[i], od_w_out[i])
        x = layer_norm(DN_ALPHA * x + mix, ln_mix_g[layer], ln_mix_b[layer])
        x = layer_norm(DN_ALPHA * x + memory_cross_attention(x, mem_k, mem_v, xa_wq[layer], xa_wo[layer]), ln_xa_g[layer], ln_xa_b[layer])
        x = layer_norm(DN_ALPHA * x + hierarchical_moe(x, moe_wg[layer], moe_bg[layer], moe_we[layer], moe_be[layer], moe_w1[layer], moe_w3[layer], moe_w2[layer]), ln_ffn_g[layer], ln_ffn_b[layer])
    return x
```

```python
import functools

import numpy as np
import jax
import jax.numpy as jnp
from jax import lax
from jax.experimental import pallas as pl
from jax.experimental.pallas import tpu as pltpu

D_MODEL = 1024
DEPTH = 2
CONV_CH = D_MODEL // 2
NSA_HEADS = 8
NSA_KV_HEADS = 2
HEAD_DIM = (D_MODEL // 2) // NSA_HEADS
CMP_BLOCK = 32
CMP_STRIDE = 16
SEL_BLOCK = 64
SEL_TOP_N = 16
WINDOW = 512
Q_BLOCK = 128
FORCE_SCORE = 1e4
XA_HEADS = 4
XA_HEAD_DIM = D_MODEL // XA_HEADS
N_GROUPS = 4
EXPERTS_PER_GROUP = 8
N_EXPERTS = N_GROUPS * EXPERTS_PER_GROUP
EXPERT_TOP_K = 2
DISPATCH_ROWS = 256
DN_ALPHA = (2 * DEPTH) ** 0.25
LN_EPS = 1e-5
NEG_INF = -1e30
KV_COLS = NSA_KV_HEADS * HEAD_DIM
EVEN_SPLIT_SIZES = (CONV_CH, CONV_CH, NSA_HEADS * HEAD_DIM, KV_COLS, KV_COLS, KV_COLS, KV_COLS, KV_COLS, KV_COLS, 3 * NSA_HEADS)


def _ln_kernel(x_ref, r_ref, g_ref, b_ref, o_ref):
    y = DN_ALPHA * x_ref[...] + r_ref[...]
    mu = jnp.mean(y, axis=-1, keepdims=True)
    yc = y - mu
    var = jnp.mean(yc * yc, axis=-1, keepdims=True)
    o_ref[...] = yc * lax.rsqrt(var + LN_EPS) * g_ref[...] + b_ref[...]


def _residual_ln(x, r, g, b, tm=512):
    shape = x.shape
    d = shape[-1]
    x2 = x.reshape(-1, d)
    r2 = r.reshape(-1, d)
    m = x2.shape[0]
    out = pl.pallas_call(
        _ln_kernel,
        grid=(m // tm,),
        in_specs=[pl.BlockSpec((tm, d), lambda i: (i, 0)), pl.BlockSpec((tm, d), lambda i: (i, 0)),
                  pl.BlockSpec((1, d), lambda i: (0, 0)), pl.BlockSpec((1, d), lambda i: (0, 0))],
        out_specs=pl.BlockSpec((tm, d), lambda i: (i, 0)),
        out_shape=jax.ShapeDtypeStruct((m, d), jnp.float32),
    )(x2, r2, g.reshape(1, d), b.reshape(1, d))
    return out.reshape(shape)


def _layer_norm(x, g, b):
    mu = jnp.mean(x, -1, keepdims=True)
    var = jnp.mean(jnp.square(x - mu), -1, keepdims=True)
    return (x - mu) * lax.rsqrt(var + LN_EPS) * g + b


def _causal_depthwise_conv(u, w):
    k = w.shape[0]
    return lax.conv_general_dilated(u, w[:, None, :], window_strides=(1,), padding=((k - 1, 0),), dimension_numbers=('NWC', 'WIO', 'NWC'), feature_group_count=u.shape[-1])


def _masked_softmax(s, mask):
    p = jax.nn.softmax(jnp.where(mask, s, NEG_INF), axis=-1)
    return jnp.where(mask, p, 0.0)


def _compress_blocks(kv, pe, w1, w2):
    b, s, g, dh = kv.shape
    n_cmp = (s - CMP_BLOCK) // CMP_STRIDE + 1
    idx = np.arange(n_cmp)[:, None] * CMP_STRIDE + np.arange(CMP_BLOCK)[None, :]
    blocks = kv[:, idx] + pe[None, None, :, None, :]
    blocks = jnp.transpose(blocks, (0, 1, 3, 2, 4)).reshape(b, n_cmp, g, CMP_BLOCK * dh)
    return jax.nn.gelu(blocks @ w1) @ w2


def _cmp_to_sel_matrix(n_cmp, n_sel):
    c0 = np.arange(n_cmp) * CMP_STRIDE
    s0 = np.arange(n_sel) * SEL_BLOCK
    ov = np.minimum(c0[:, None] + CMP_BLOCK, s0[None, :] + SEL_BLOCK) - np.maximum(c0[:, None], s0[None, :])
    return (np.clip(ov, 0, None) / CMP_BLOCK).astype(np.float32)


def _nsa_attention(q, k_cmp, v_cmp, k_sel, v_sel, k_win, v_win, gates):
    b, s, h, dh = q.shape
    g = NSA_KV_HEADS
    hg = h // g
    n_cmp = k_cmp.shape[1]
    n_sel = s // SEL_BLOCK
    top_n = min(SEL_TOP_N, n_sel)
    scale = dh ** -0.5
    m_sel = jnp.asarray(_cmp_to_sel_matrix(n_cmp, n_sel))
    cmp_end = jnp.arange(n_cmp) * CMP_STRIDE + CMP_BLOCK - 1
    blk_ids = jnp.arange(n_sel)
    kpos = jnp.arange(s)
    f32 = jnp.float32

    def query_block(c):
        q0 = c * Q_BLOCK
        t = q0 + jnp.arange(Q_BLOCK)
        qc = lax.dynamic_slice_in_dim(q, q0, Q_BLOCK, axis=1).reshape(b, Q_BLOCK, g, hg, dh)
        gc = lax.dynamic_slice_in_dim(gates, q0, Q_BLOCK, axis=1).reshape(b, Q_BLOCK, g, hg, 3)
        s_c = jnp.einsum('bqghd,bngd->bghqn', qc, k_cmp, preferred_element_type=f32) * scale
        p_c = _masked_softmax(s_c, cmp_end[None, :] <= t[:, None])
        o_c = jnp.einsum('bghqn,bngd->bqghd', p_c, v_cmp)
        imp = jnp.einsum('bghqn,nm->bgqm', p_c, m_sel)
        cur = t // SEL_BLOCK
        forced = (blk_ids[None, :] == 0) | (blk_ids[None, :] == cur[:, None]) | (blk_ids[None, :] == cur[:, None] - 1)
        score = jnp.where(forced, FORCE_SCORE, imp)
        score = jnp.where(blk_ids[None, :] * SEL_BLOCK <= t[:, None], score, NEG_INF)
        sa = score[..., :, None]
        sb = score[..., None, :]
        beats = (sb > sa) | ((sb == sa) & (blk_ids[None, :] < blk_ids[:, None]))
        rank = jnp.sum(beats, axis=-1)
        sel = (rank < top_n) & (score > 0.5 * NEG_INF)
        selk = jnp.repeat(sel, SEL_BLOCK, axis=-1)
        mask_s = (selk & (kpos[None, :] <= t[:, None]))[:, :, None]
        s_s = jnp.einsum('bqghd,bkgd->bghqk', qc, k_sel, preferred_element_type=f32) * scale
        p_s = _masked_softmax(s_s, mask_s)
        o_s = jnp.einsum('bghqk,bkgd->bqghd', p_s, v_sel)
        mask_w = (kpos[None, :] <= t[:, None]) & (kpos[None, :] > t[:, None] - WINDOW)
        s_w = jnp.einsum('bqghd,bkgd->bghqk', qc, k_win, preferred_element_type=f32) * scale
        p_w = _masked_softmax(s_w, mask_w)
        o_w = jnp.einsum('bghqk,bkgd->bqghd', p_w, v_win)
        o = gc[..., 0:1] * o_c + gc[..., 1:2] * o_s + gc[..., 2:3] * o_w
        return o.reshape(b, Q_BLOCK, h * dh)

    out = lax.map(query_block, jnp.arange(s // Q_BLOCK))
    return jnp.transpose(out, (1, 0, 2, 3)).reshape(b, s, h * dh)


def _even_mixer(x, w_in, conv_w, conv_b, cn_g, cn_b, pe_k, w1_k, w2_k, pe_v, w1_v, w2_v, w_out):
    b, s, _ = x.shape
    split_pts = np.cumsum(EVEN_SPLIT_SIZES)[:-1].tolist()
    a_val, a_gate, q, kc, vc, ks, vs, kw, vw, gl = jnp.split(x @ w_in, split_pts, axis=-1)
    a = a_val * jax.nn.sigmoid(a_gate)
    a = _causal_depthwise_conv(a, conv_w) + conv_b
    a = jax.nn.silu(_layer_norm(a, cn_g, cn_b))
    kvs = lambda u: u.reshape(b, s, NSA_KV_HEADS, HEAD_DIM)
    k_cmp = _compress_blocks(kvs(kc), pe_k, w1_k, w2_k)
    v_cmp = _compress_blocks(kvs(vc), pe_v, w1_v, w2_v)
    gates = jax.nn.sigmoid(gl).reshape(b, s, NSA_HEADS, 3)
    o = _nsa_attention(q.reshape(b, s, NSA_HEADS, HEAD_DIM), k_cmp, v_cmp, kvs(ks), kvs(vs), kvs(kw), kvs(vw), gates)
    return jnp.concatenate([a, o], axis=-1) @ w_out


def _odd_mixer(x, w_in, conv_w, w_out):
    gate_b, gate_c, h = jnp.split(x @ w_in, 3, axis=-1)
    return (gate_b * _causal_depthwise_conv(gate_c * h, conv_w)) @ w_out


def _memory_cross_attention(x, mem_k, mem_v, wq, wo):
    b, s, d = x.shape
    q = (x @ wq).reshape(b, s, XA_HEADS, XA_HEAD_DIM)
    sc = jnp.einsum('bshd,bmhd->bhsm', q, mem_k, preferred_element_type=jnp.float32) * XA_HEAD_DIM ** -0.5
    p = jax.nn.softmax(sc, axis=-1)
    return jnp.einsum('bhsm,bmhd->bshd', p, mem_v).reshape(b, s, d) @ wo


def _hierarchical_moe(x, wg, bg, we, be, w1, w3, w2):
    b, s, d = x.shape
    n_tok = b * s
    xt = x.reshape(n_tok, d)
    g_logits = xt @ wg + bg
    g_star = jnp.argmax(g_logits, axis=-1)
    p_group = jnp.take_along_axis(jax.nn.softmax(g_logits, axis=-1), g_star[:, None], axis=-1)
    e_logits = jnp.einsum('td,dge->tge', xt, we) + be
    e_logits = jnp.take_along_axis(e_logits, g_star[:, None, None], axis=1)[:, 0]
    top_val, top_idx = lax.top_k(e_logits, EXPERT_TOP_K)
    gate = (p_group * jax.nn.softmax(top_val, axis=-1)).reshape(-1)
    expert = (g_star[:, None] * EXPERTS_PER_GROUP + top_idx).reshape(-1).astype(jnp.int32)
    token = jnp.repeat(jnp.arange(n_tok, dtype=jnp.int32), EXPERT_TOP_K)
    n_assign = n_tok * EXPERT_TOP_K
    n_blocks = n_assign // DISPATCH_ROWS + N_EXPERTS + 1
    order = jnp.argsort(expert)
    e_sorted = expert[order]
    tok_sorted = token[order]
    gate_sorted = gate[order]
    counts = jnp.bincount(expert, length=N_EXPERTS)
    start = jnp.cumsum(counts) - counts
    padded = (counts + DISPATCH_ROWS - 1) // DISPATCH_ROWS * DISPATCH_ROWS
    pad_end = jnp.cumsum(padded)
    pad_start = pad_end - padded
    dest = pad_start[e_sorted] + jnp.arange(n_assign) - start[e_sorted]
    x_disp = jnp.zeros((n_blocks * DISPATCH_ROWS, d), x.dtype).at[dest].set(xt[tok_sorted])
    block_expert = jnp.minimum(jnp.searchsorted(pad_end, jnp.arange(n_blocks) * DISPATCH_ROWS, side='right'), N_EXPERTS - 1)

    def expert_block(args):
        xb, e = args
        return (jax.nn.silu(xb @ w1[e]) * (xb @ w3[e])) @ w2[e]

    y_disp = lax.map(expert_block, (x_disp.reshape(n_blocks, DISPATCH_ROWS, d), block_expert))
    y = y_disp.reshape(-1, d)[dest] * gate_sorted[:, None]
    return jax.ops.segment_sum(y, tok_sorted, num_segments=n_tok).reshape(b, s, d)


def kernel(x, mem, mem_wk, mem_wv, ev_w_in, ev_conv_w, ev_conv_b, ev_cnorm_g, ev_cnorm_b, ev_cmp_pe_k, ev_cmp_w1_k, ev_cmp_w2_k, ev_cmp_pe_v, ev_cmp_w1_v, ev_cmp_w2_v, ev_w_out, od_w_in, od_conv_w, od_w_out, ln_mix_g, ln_mix_b, xa_wq, xa_wo, ln_xa_g, ln_xa_b, moe_wg, moe_bg, moe_we, moe_be, moe_w1, moe_w3, moe_w2, ln_ffn_g, ln_ffn_b):
    b = x.shape[0]
    m_len = mem.shape[1]
    mem_k = (mem @ mem_wk).reshape(b, m_len, XA_HEADS, XA_HEAD_DIM)
    mem_v = (mem @ mem_wv).reshape(b, m_len, XA_HEADS, XA_HEAD_DIM)
    for layer in range(DEPTH):
        i = layer // 2
        if layer % 2 == 0:
            mix = _even_mixer(x, ev_w_in[i], ev_conv_w[i], ev_conv_b[i], ev_cnorm_g[i], ev_cnorm_b[i], ev_cmp_pe_k[i], ev_cmp_w1_k[i], ev_cmp_w2_k[i], ev_cmp_pe_v[i], ev_cmp_w1_v[i], ev_cmp_w2_v[i], ev_w_out[i])
        else:
            mix = _odd_mixer(x, od_w_in[i], od_conv_w[i], od_w_out[i])
        x = _residual_ln(x, mix, ln_mix_g[layer], ln_mix_b[layer])
        x = _residual_ln(x, _memory_cross_attention(x, mem_k, mem_v, xa_wq[layer], xa_wo[layer]), ln_xa_g[layer], ln_xa_b[layer])
        x = _residual_ln(x, _hierarchical_moe(x, moe_wg[layer], moe_bg[layer], moe_we[layer], moe_be[layer], moe_w1[layer], moe_w3[layer], moe_w2[layer]), ln_ffn_g[layer], ln_ffn_b[layer])
    return x
```

```python
import functools

import numpy as np
import jax
import jax.numpy as jnp
from jax import lax
from jax.experimental import pallas as pl
from jax.experimental.pallas import tpu as pltpu

D_MODEL = 1024
DEPTH = 2
CONV_CH = D_MODEL // 2
NSA_HEADS = 8
NSA_KV_HEADS = 2
HEAD_DIM = (D_MODEL // 2) // NSA_HEADS
CMP_BLOCK = 32
CMP_STRIDE = 16
SEL_BLOCK = 64
SEL_TOP_N = 16
WINDOW = 512
Q_BLOCK = 128
FORCE_SCORE = 1e4
XA_HEADS = 4
XA_HEAD_DIM = D_MODEL // XA_HEADS
N_GROUPS = 4
EXPERTS_PER_GROUP = 8
N_EXPERTS = N_GROUPS * EXPERTS_PER_GROUP
EXPERT_TOP_K = 2
DN_ALPHA = (2 * DEPTH) ** 0.25
LN_EPS = 1e-5
NEG_INF = -1e30
KV_COLS = NSA_KV_HEADS * HEAD_DIM
EVEN_SPLIT_SIZES = (CONV_CH, CONV_CH, NSA_HEADS * HEAD_DIM, KV_COLS, KV_COLS, KV_COLS, KV_COLS, KV_COLS, KV_COLS, 3 * NSA_HEADS)

LANES = 128
HEADS_PER_KV = NSA_HEADS // NSA_KV_HEADS
QL = Q_BLOCK * HEADS_PER_KV
SEL_TILE = 512
WIN_SPAN = WINDOW + Q_BLOCK
BLOCKS_PER_TILE = SEL_TILE // SEL_BLOCK
ROUTE_TILE = 512
EXPERT_TILE = 256
ROUTE_COLS = 8


def _ln_rows(y, g, b):
    mu = jnp.mean(y, axis=-1, keepdims=True)
    yc = y - mu
    var = jnp.mean(yc * yc, axis=-1, keepdims=True)
    return yc * lax.rsqrt(var + LN_EPS) * g + b


def _ln_kernel(x_ref, r_ref, g_ref, b_ref, o_ref):
    o_ref[...] = _ln_rows(DN_ALPHA * x_ref[...] + r_ref[...], g_ref[...], b_ref[...])


def _residual_ln(x, r, g, b, tm=512):
    shape = x.shape
    d = shape[-1]
    x2 = x.reshape(-1, d)
    r2 = r.reshape(-1, d)
    m = x2.shape[0]
    out = pl.pallas_call(
        _ln_kernel,
        name="residual_ln",
        grid=(m // tm,),
        in_specs=[pl.BlockSpec((tm, d), lambda i: (i, 0)), pl.BlockSpec((tm, d), lambda i: (i, 0)),
                  pl.BlockSpec((1, d), lambda i: (0, 0)), pl.BlockSpec((1, d), lambda i: (0, 0))],
        out_specs=pl.BlockSpec((tm, d), lambda i: (i, 0)),
        out_shape=jax.ShapeDtypeStruct((m, d), jnp.float32),
    )(x2, r2, g.reshape(1, d), b.reshape(1, d))
    return out.reshape(shape)


def _layer_norm(x, g, b):
    mu = jnp.mean(x, -1, keepdims=True)
    var = jnp.mean(jnp.square(x - mu), -1, keepdims=True)
    return (x - mu) * lax.rsqrt(var + LN_EPS) * g + b


def _causal_depthwise_conv(u, w):
    k = w.shape[0]
    return lax.conv_general_dilated(u, w[:, None, :], window_strides=(1,), padding=((k - 1, 0),), dimension_numbers=('NWC', 'WIO', 'NWC'), feature_group_count=u.shape[-1])


def _compress_blocks(kv, pe, w1, w2):
    b, s, g, dh = kv.shape
    n_cmp = (s - CMP_BLOCK) // CMP_STRIDE + 1
    idx = np.arange(n_cmp)[:, None] * CMP_STRIDE + np.arange(CMP_BLOCK)[None, :]
    blocks = kv[:, idx] + pe[None, None, :, None, :]
    blocks = jnp.transpose(blocks, (0, 1, 3, 2, 4)).reshape(b, n_cmp, g, CMP_BLOCK * dh)
    return jax.nn.gelu(blocks @ w1) @ w2


def _cmp_to_sel_matrix(n_cmp, n_sel):
    c0 = np.arange(n_cmp) * CMP_STRIDE
    s0 = np.arange(n_sel) * SEL_BLOCK
    ov = np.minimum(c0[:, None] + CMP_BLOCK, s0[None, :] + SEL_BLOCK) - np.maximum(c0[:, None], s0[None, :])
    return (np.clip(ov, 0, None) / CMP_BLOCK).astype(np.float32)


def _nsa_kernel(qT_ref, gT_ref, kc_ref, vcT_ref, mselT_ref, ks_ref, vsT_ref, kw_ref, vwT_ref, o_ref,
                score_ref, sel_ref, *, n_sel):
    c = pl.program_id(1)
    q0 = c * Q_BLOCK
    f32 = jnp.float32
    bf16 = jnp.bfloat16
    n_cmp_pad = kc_ref.shape[1]

    lane_q = lax.broadcasted_iota(jnp.int32, (1, QL), 1) % Q_BLOCK
    t_row = q0 + lane_q
    t_row_q = q0 + lax.broadcasted_iota(jnp.int32, (1, Q_BLOCK), 1)
    cur_q = t_row_q // SEL_BLOCK

    for g in range(NSA_KV_HEADS):
        pieces = []
        for hg in range(HEADS_PER_KV):
            h = g * HEADS_PER_KV + hg
            qh = qT_ref[0, h * HEAD_DIM:(h + 1) * HEAD_DIM, :] * jnp.asarray(HEAD_DIM ** -0.5, bf16)
            z = jnp.zeros_like(qh)
            pieces.append(jnp.concatenate([qh, z] if g == 0 else [z, qh], axis=0))
        qTp = jnp.concatenate(pieces, axis=1)
        rows = slice(g * HEAD_DIM, (g + 1) * HEAD_DIM)

        s_c = jnp.dot(kc_ref[0], qTp, preferred_element_type=f32)
        n_iota = lax.broadcasted_iota(jnp.int32, (n_cmp_pad, QL), 0)
        mask_c = (n_iota * CMP_STRIDE + (CMP_BLOCK - 1)) <= t_row
        s_c = jnp.where(mask_c, s_c, NEG_INF)
        m_c = jnp.max(s_c, axis=0, keepdims=True)
        p_c = jnp.where(mask_c, jnp.exp(s_c - m_c), 0.0)
        l_c = jnp.sum(p_c, axis=0, keepdims=True)
        p_c = p_c * jnp.where(l_c > 0.0, 1.0 / l_c, 0.0)
        p_cb = p_c.astype(bf16)
        o_c = jnp.dot(vcT_ref[0], p_cb, preferred_element_type=f32)[rows]
        imp4 = jnp.dot(mselT_ref[...], p_cb, preferred_element_type=f32)
        imp = imp4[:, 0:Q_BLOCK]
        for hg in range(1, HEADS_PER_KV):
            imp = imp + imp4[:, hg * Q_BLOCK:(hg + 1) * Q_BLOCK]

        m_iota = lax.broadcasted_iota(jnp.int32, (n_sel, Q_BLOCK), 0)
        forced = (m_iota == 0) | (m_iota == cur_q) | (m_iota == cur_q - 1)
        score = jnp.where(forced, FORCE_SCORE, imp)
        valid = m_iota <= cur_q
        score = jnp.where(valid, score, NEG_INF)
        score_ref[...] = score

        def rank_body(mp, rank):
            row = score_ref[pl.ds(mp, 1), :]
            beats = (row > score) | ((row == score) & (mp < m_iota))
            return rank + beats.astype(jnp.int32)

        n_comp = jnp.minimum(q0 // SEL_BLOCK + 2, n_sel)
        rank = lax.fori_loop(0, n_comp, rank_body, jnp.zeros((n_sel, Q_BLOCK), jnp.int32))
        top_n = min(SEL_TOP_N, n_sel)
        sel_ref[...] = ((rank < top_n) & valid).astype(f32)

        def sel_body(j, carry):
            m_i, l_i, acc = carry
            s = jnp.dot(ks_ref[0, j], qTp, preferred_element_type=f32)
            selrows = sel_ref[pl.ds(pl.multiple_of(j * BLOCKS_PER_TILE, BLOCKS_PER_TILE), BLOCKS_PER_TILE), :]
            blkmask = jnp.concatenate(
                [jnp.broadcast_to(selrows[r:r + 1, :], (SEL_BLOCK, Q_BLOCK)) for r in range(BLOCKS_PER_TILE)], axis=0)
            blkmask = jnp.concatenate([blkmask] * HEADS_PER_KV, axis=1)
            key = j * SEL_TILE + lax.broadcasted_iota(jnp.int32, (SEL_TILE, QL), 0)
            allowed = (blkmask > 0.5) & (key <= t_row)
            s = jnp.where(allowed, s, NEG_INF)
            m_new = jnp.maximum(m_i, jnp.max(s, axis=0, keepdims=True))
            alpha = jnp.exp(m_i - m_new)
            p = jnp.where(allowed, jnp.exp(s - m_new), 0.0)
            l_new = alpha * l_i + jnp.sum(p, axis=0, keepdims=True)
            pv = jnp.dot(vsT_ref[0, j], p.astype(bf16), preferred_element_type=f32)
            return m_new, l_new, alpha * acc + pv

        n_tiles = q0 // SEL_TILE + 1
        m_s, l_s, acc_s = lax.fori_loop(
            0, n_tiles, sel_body,
            (jnp.full((1, QL), NEG_INF, f32), jnp.zeros((1, QL), f32), jnp.zeros((KV_COLS, QL), f32)))
        o_s = acc_s[rows] * (1.0 / l_s)

        start = pl.multiple_of(jnp.maximum(q0 - WINDOW, 0), Q_BLOCK)
        s_w = jnp.dot(kw_ref[0, pl.ds(start, WIN_SPAN), :], qTp, preferred_element_type=f32)
        key_w = start + lax.broadcasted_iota(jnp.int32, (WIN_SPAN, QL), 0)
        mask_w = (key_w <= t_row) & (key_w > t_row - WINDOW)
        s_w = jnp.where(mask_w, s_w, NEG_INF)
        m_w = jnp.max(s_w, axis=0, keepdims=True)
        p_w = jnp.where(mask_w, jnp.exp(s_w - m_w), 0.0)
        l_w = jnp.sum(p_w, axis=0, keepdims=True)
        p_wb = p_w.astype(bf16)
        j0 = start // Q_BLOCK
        acc_w = jnp.zeros((KV_COLS, QL), f32)
        for i in range(WIN_SPAN // Q_BLOCK):
            acc_w = acc_w + jnp.dot(vwT_ref[0, j0 + i], p_wb[i * Q_BLOCK:(i + 1) * Q_BLOCK, :], preferred_element_type=f32)
        o_w = acc_w[rows] * (1.0 / l_w)

        for hg in range(HEADS_PER_KV):
            h = g * HEADS_PER_KV + hg
            lanes = slice(hg * Q_BLOCK, (hg + 1) * Q_BLOCK)
            gate = jax.nn.sigmoid(gT_ref[0, 3 * h:3 * h + 3, :])
            o_ref[0, h * HEAD_DIM:(h + 1) * HEAD_DIM, :] = (
                gate[0:1] * o_c[:, lanes] + gate[1:2] * o_s[:, lanes] + gate[2:3] * o_w[:, lanes])


def _nsa_attention(q, k_cmp, v_cmp, k_sel, v_sel, k_win, v_win, gate_logits):
    b, s, _ = q.shape
    bf16 = jnp.bfloat16
    n_sel = s // SEL_BLOCK
    n_cmp = k_cmp.shape[1]
    nc_pad = -(-n_cmp // LANES) * LANES
    qT = jnp.swapaxes(q.astype(bf16), 1, 2)
    gT = jnp.swapaxes(gate_logits, 1, 2)
    kc = jnp.pad(k_cmp.reshape(b, n_cmp, KV_COLS).astype(bf16), ((0, 0), (0, nc_pad - n_cmp), (0, 0)))
    vcT = jnp.swapaxes(jnp.pad(v_cmp.reshape(b, n_cmp, KV_COLS).astype(bf16), ((0, 0), (0, nc_pad - n_cmp), (0, 0))), 1, 2)
    mselT = jnp.asarray(np.pad(_cmp_to_sel_matrix(n_cmp, n_sel).T, ((0, 0), (0, nc_pad - n_cmp))), bf16)
    ks = k_sel.astype(bf16).reshape(b, s // SEL_TILE, SEL_TILE, KV_COLS)
    vsT = jnp.transpose(v_sel.astype(bf16).reshape(b, s // SEL_TILE, SEL_TILE, KV_COLS), (0, 1, 3, 2))
    kw = k_win.astype(bf16)
    vwT = jnp.transpose(v_win.astype(bf16).reshape(b, s // Q_BLOCK, Q_BLOCK, KV_COLS), (0, 1, 3, 2))
    hd = NSA_HEADS * HEAD_DIM
    outT = pl.pallas_call(
        functools.partial(_nsa_kernel, n_sel=n_sel),
        name="nsa_attention",
        grid=(b, s // Q_BLOCK),
        in_specs=[
            pl.BlockSpec((1, hd, Q_BLOCK), lambda i, c: (i, 0, c)),
            pl.BlockSpec((1, 3 * NSA_HEADS, Q_BLOCK), lambda i, c: (i, 0, c)),
            pl.BlockSpec((1, nc_pad, KV_COLS), lambda i, c: (i, 0, 0)),
            pl.BlockSpec((1, KV_COLS, nc_pad), lambda i, c: (i, 0, 0)),
            pl.BlockSpec((n_sel, nc_pad), lambda i, c: (0, 0)),
            pl.BlockSpec((1, s // SEL_TILE, SEL_TILE, KV_COLS), lambda i, c: (i, 0, 0, 0)),
            pl.BlockSpec((1, s // SEL_TILE, KV_COLS, SEL_TILE), lambda i, c: (i, 0, 0, 0)),
            pl.BlockSpec((1, s, KV_COLS), lambda i, c: (i, 0, 0)),
            pl.BlockSpec((1, s // Q_BLOCK, KV_COLS, Q_BLOCK), lambda i, c: (i, 0, 0, 0)),
        ],
        out_specs=pl.BlockSpec((1, hd, Q_BLOCK), lambda i, c: (i, 0, c)),
        out_shape=jax.ShapeDtypeStruct((b, hd, s), jnp.float32),
        scratch_shapes=[pltpu.VMEM((n_sel, Q_BLOCK), jnp.float32), pltpu.VMEM((n_sel, Q_BLOCK), jnp.float32)],
        compiler_params=pltpu.CompilerParams(dimension_semantics=("arbitrary", "arbitrary")),
    )(qT, gT, kc, vcT, mselT, ks, vsT, kw, vwT)
    return jnp.swapaxes(outT, 1, 2)


def _even_mixer(x, w_in, conv_w, conv_b, cn_g, cn_b, pe_k, w1_k, w2_k, pe_v, w1_v, w2_v, w_out):
    b, s, _ = x.shape
    split_pts = np.cumsum(EVEN_SPLIT_SIZES)[:-1].tolist()
    a_val, a_gate, q, kc, vc, ks, vs, kw, vw, gl = jnp.split(x @ w_in, split_pts, axis=-1)
    a = a_val * jax.nn.sigmoid(a_gate)
    a = _causal_depthwise_conv(a, conv_w) + conv_b
    a = jax.nn.silu(_layer_norm(a, cn_g, cn_b))
    kvs = lambda u: u.reshape(b, s, NSA_KV_HEADS, HEAD_DIM)
    k_cmp = _compress_blocks(kvs(kc), pe_k, w1_k, w2_k)
    v_cmp = _compress_blocks(kvs(vc), pe_v, w1_v, w2_v)
    o = _nsa_attention(q, k_cmp, v_cmp, ks, vs, kw, vw, gl)
    return jnp.concatenate([a, o], axis=-1) @ w_out


def _odd_mixer(x, w_in, conv_w, w_out):
    gate_b, gate_c, h = jnp.split(x @ w_in, 3, axis=-1)
    return (gate_b * _causal_depthwise_conv(gate_c * h, conv_w)) @ w_out


def _memory_cross_attention(x, mem_k, mem_v, wq, wo):
    b, s, d = x.shape
    q = (x @ wq).reshape(b, s, XA_HEADS, XA_HEAD_DIM)
    sc = jnp.einsum('bshd,bmhd->bhsm', q, mem_k, preferred_element_type=jnp.float32) * XA_HEAD_DIM ** -0.5
    p = jax.nn.softmax(sc, axis=-1)
    return jnp.einsum('bhsm,bmhd->bshd', p, mem_v).reshape(b, s, d) @ wo


def _router_kernel(x_ref, w_ref, b_ref, tri_ref, route_ref, cnt_ref, run_ref):
    i = pl.program_id(0)
    f32 = jnp.float32

    @pl.when(i == 0)
    def _():
        run_ref[...] = jnp.zeros_like(run_ref)

    tm = x_ref.shape[0]
    logits = jnp.dot(x_ref[...].astype(jnp.bfloat16), w_ref[...], preferred_element_type=f32) + b_ref[...]
    lane = lax.broadcasted_iota(jnp.int32, (tm, LANES), 1)
    is_g = lane < N_GROUPS
    gl = jnp.where(is_g, logits, NEG_INF)
    g_max = jnp.max(gl, axis=1, keepdims=True)
    g_star = jnp.min(jnp.where(gl == g_max, lane, LANES), axis=1, keepdims=True)
    p_group = 1.0 / jnp.sum(jnp.where(is_g, jnp.exp(gl - g_max), 0.0), axis=1, keepdims=True)
    lo = N_GROUPS + g_star * EXPERTS_PER_GROUP
    in_grp = (lane >= lo) & (lane < lo + EXPERTS_PER_GROUP)
    el = jnp.where(in_grp, logits, NEG_INF)
    v1 = jnp.max(el, axis=1, keepdims=True)
    i1 = jnp.min(jnp.where(el == v1, lane, LANES), axis=1, keepdims=True)
    el2 = jnp.where(lane == i1, NEG_INF, el)
    v2 = jnp.max(el2, axis=1, keepdims=True)
    i2 = jnp.min(jnp.where(el2 == v2, lane, LANES), axis=1, keepdims=True)
    e21 = jnp.exp(v2 - v1)
    gate1 = p_group * (1.0 / (1.0 + e21))
    gate2 = p_group * (e21 / (1.0 + e21))
    oh1 = (lane == i1).astype(f32)
    oh2 = (lane == i2).astype(f32)
    both = (oh1 + oh2).astype(jnp.bfloat16)
    before = jnp.dot(tri_ref[...], both, preferred_element_type=f32) + run_ref[...]
    rank1 = jnp.sum(oh1 * before, axis=1, keepdims=True)
    rank2 = jnp.sum(oh2 * before, axis=1, keepdims=True)
    run_ref[...] = run_ref[...] + jnp.sum(oh1 + oh2, axis=0, keepdims=True)
    cnt_ref[...] = run_ref[...]
    zero = jnp.zeros_like(gate1)
    route_ref[...] = jnp.concatenate(
        [gate1, gate2, (i1 - N_GROUPS).astype(f32), (i2 - N_GROUPS).astype(f32), rank1, rank2, zero, zero], axis=1)


def _moe_route(xt, wg, bg, we, be):
    t, d = xt.shape
    pad = LANES - N_GROUPS - N_EXPERTS
    w = jnp.concatenate([wg, we.reshape(d, N_EXPERTS), jnp.zeros((d, pad), wg.dtype)], axis=1)
    bias = jnp.concatenate([bg, be.reshape(N_EXPERTS), jnp.zeros((pad,), bg.dtype)])[None, :]
    tri = jnp.asarray(np.tril(np.ones((ROUTE_TILE, ROUTE_TILE), np.float32), -1), jnp.bfloat16)
    return pl.pallas_call(
        _router_kernel,
        name="moe_router",
        grid=(t // ROUTE_TILE,),
        in_specs=[pl.BlockSpec((ROUTE_TILE, d), lambda i: (i, 0)),
                  pl.BlockSpec((d, LANES), lambda i: (0, 0)),
                  pl.BlockSpec((1, LANES), lambda i: (0, 0)),
                  pl.BlockSpec((ROUTE_TILE, ROUTE_TILE), lambda i: (0, 0))],
        out_specs=[pl.BlockSpec((ROUTE_TILE, ROUTE_COLS), lambda i: (i, 0)),
                   pl.BlockSpec((1, LANES), lambda i: (0, 0))],
        out_shape=[jax.ShapeDtypeStruct((t, ROUTE_COLS), jnp.float32), jax.ShapeDtypeStruct((1, LANES), jnp.float32)],
        scratch_shapes=[pltpu.VMEM((1, LANES), jnp.float32)],
        compiler_params=pltpu.CompilerParams(dimension_semantics=("arbitrary",)),
    )(xt, w.astype(jnp.bfloat16), bias, tri)


def _dispatch_kernel(dest_ref, x_hbm, zero_hbm, xd_hbm, sem):
    del zero_hbm
    i = pl.program_id(0)
    n = pl.num_programs(0)
    tm = dest_ref.shape[2] // EXPERT_TOP_K

    def row_copy(r, k):
        return pltpu.make_async_copy(
            x_hbm.at[pl.ds(i * tm + r, 1)], xd_hbm.at[pl.ds(dest_ref[0, 0, k * tm + r], 1)], sem)

    def wait_one_tile():
        pltpu.make_async_copy(
            x_hbm.at[pl.ds(0, EXPERT_TOP_K * tm)], xd_hbm.at[pl.ds(0, EXPERT_TOP_K * tm)], sem).wait()

    def body(r, c):
        row_copy(r, 0).start()
        row_copy(r, 1).start()
        return c

    lax.fori_loop(0, tm, body, 0, unroll=8)

    @pl.when(i > 0)
    def _():
        wait_one_tile()

    @pl.when(i == n - 1)
    def _():
        wait_one_tile()


def _moe_dispatch(xt, dest2, n_rows):
    t, d = xt.shape
    return pl.pallas_call(
        _dispatch_kernel,
        name="moe_dispatch",
        grid=(t // ROUTE_TILE,),
        in_specs=[pl.BlockSpec((1, 1, EXPERT_TOP_K * ROUTE_TILE), lambda i: (i, 0, 0), memory_space=pltpu.SMEM),
                  pl.BlockSpec(memory_space=pl.ANY),
                  pl.BlockSpec(memory_space=pl.ANY)],
        out_specs=pl.BlockSpec(memory_space=pl.ANY),
        out_shape=jax.ShapeDtypeStruct((n_rows, d), xt.dtype),
        scratch_shapes=[pltpu.SemaphoreType.DMA(())],
        input_output_aliases={2: 0},
        compiler_params=pltpu.CompilerParams(dimension_semantics=("arbitrary",), has_side_effects=True),
    )(dest2, xt, jnp.zeros((n_rows, d), xt.dtype))


def _expert_kernel(te_ref, nu_ref, xd_ref, w1_ref, w3_ref, w2_ref, y_ref):
    del te_ref
    i = pl.program_id(0)

    @pl.when(i < nu_ref[0])
    def _():
        xb = xd_ref[...].astype(jnp.bfloat16)
        h1 = jnp.dot(xb, w1_ref[0], preferred_element_type=jnp.float32)
        h3 = jnp.dot(xb, w3_ref[0], preferred_element_type=jnp.float32)
        a = (h1 * jax.nn.sigmoid(h1) * h3).astype(jnp.bfloat16)
        y_ref[...] = jnp.dot(a, w2_ref[0], preferred_element_type=jnp.float32)

    @pl.when(i >= nu_ref[0])
    def _():
        y_ref[...] = jnp.zeros_like(y_ref)


def _moe_experts(x_disp, tile_expert, n_used, w1, w3, w2):
    n_rows, d = x_disp.shape
    n_tiles = n_rows // EXPERT_TILE
    hid = w1.shape[2]

    def row_map(i, te, nu):
        return (i, 0)

    def w_map(i, te, nu):
        return (te[i], 0, 0)

    grid_spec = pltpu.PrefetchScalarGridSpec(
        num_scalar_prefetch=2,
        grid=(n_tiles,),
        in_specs=[pl.BlockSpec((EXPERT_TILE, d), row_map),
                  pl.BlockSpec((1, d, hid), w_map),
                  pl.BlockSpec((1, d, hid), w_map),
                  pl.BlockSpec((1, hid, d), w_map)],
        out_specs=pl.BlockSpec((EXPERT_TILE, d), row_map),
    )
    return pl.pallas_call(
        _expert_kernel,
        name="moe_experts",
        grid_spec=grid_spec,
        out_shape=jax.ShapeDtypeStruct((n_rows, d), jnp.float32),
        compiler_params=pltpu.CompilerParams(dimension_semantics=("arbitrary",)),
    )(tile_expert, n_used, x_disp, w1, w3, w2)


def _combine_kernel(dest_ref, dest_next_ref, x_ref, route_ref, g_ref, b_ref, yd_hbm, o_ref, ybuf, sem):
    i = pl.program_id(0)
    n = pl.num_programs(0)
    tm = x_ref.shape[0]
    rows = EXPERT_TOP_K * tm

    def start_tile(dref, slot):
        def body(r, c):
            pltpu.make_async_copy(
                yd_hbm.at[pl.ds(dref[0, 0, r], 1)], ybuf.at[slot, pl.ds(r, 1)], sem.at[slot]).start()
            return c
        lax.fori_loop(0, rows, body, 0, unroll=8)

    slot = i % 2

    @pl.when(i == 0)
    def _():
        start_tile(dest_ref, 0)

    @pl.when(i + 1 < n)
    def _():
        start_tile(dest_next_ref, 1 - slot)

    pltpu.make_async_copy(yd_hbm.at[pl.ds(0, rows)], ybuf.at[slot], sem.at[slot]).wait()
    y1 = ybuf[slot, 0:tm, :]
    y2 = ybuf[slot, tm:rows, :]
    route = route_ref[...]
    y = DN_ALPHA * x_ref[...] + (y1 * route[:, 0:1] + y2 * route[:, 1:2])
    o_ref[...] = _ln_rows(y, g_ref[...], b_ref[...])


def _moe_combine_ln(xt, y_disp, dest2, route, ln_g, ln_b):
    t, d = xt.shape
    n = t // ROUTE_TILE
    rows = EXPERT_TOP_K * ROUTE_TILE
    return pl.pallas_call(
        _combine_kernel,
        name="moe_combine_ln",
        grid=(n,),
        in_specs=[pl.BlockSpec((1, 1, rows), lambda i: (i, 0, 0), memory_space=pltpu.SMEM),
                  pl.BlockSpec((1, 1, rows), lambda i: (jnp.minimum(i + 1, n - 1), 0, 0), memory_space=pltpu.SMEM),
                  pl.BlockSpec((ROUTE_TILE, d), lambda i: (i, 0)),
                  pl.BlockSpec((ROUTE_TILE, ROUTE_COLS), lambda i: (i, 0)),
                  pl.BlockSpec((1, d), lambda i: (0, 0)),
                  pl.BlockSpec((1, d), lambda i: (0, 0)),
                  pl.BlockSpec(memory_space=pl.ANY)],
        out_specs=pl.BlockSpec((ROUTE_TILE, d), lambda i: (i, 0)),
        out_shape=jax.ShapeDtypeStruct((t, d), jnp.float32),
        scratch_shapes=[pltpu.VMEM((2, rows, d), jnp.float32), pltpu.SemaphoreType.DMA((2,))],
        compiler_params=pltpu.CompilerParams(dimension_semantics=("arbitrary",), vmem_limit_bytes=48 * 1024 * 1024),
    )(dest2, dest2, xt, route, ln_g.reshape(1, d), ln_b.reshape(1, d), y_disp)


def _moe_sublayer(x, wg, bg, we, be, w1, w3, w2, ln_g, ln_b):
    b, s, d = x.shape
    t = b * s
    xt = x.reshape(t, d)
    route, cnt = _moe_route(xt, wg, bg, we, be)
    counts = cnt[0, N_GROUPS:N_GROUPS + N_EXPERTS].astype(jnp.int32)
    n_tiles = (t * EXPERT_TOP_K) // EXPERT_TILE + N_EXPERTS
    tiles_per = (counts + EXPERT_TILE - 1) // EXPERT_TILE
    tile_end = jnp.cumsum(tiles_per)
    pad_start = (tile_end - tiles_per) * EXPERT_TILE
    n_used = tile_end[-1:].astype(jnp.int32)
    tile_ids = jnp.arange(n_tiles)
    tile_expert = jnp.minimum(jnp.searchsorted(tile_end, tile_ids, side='right'), N_EXPERTS - 1).astype(jnp.int32)
    tile_expert = jnp.where(tile_ids < n_used[0], tile_expert, tile_expert[jnp.maximum(n_used[0] - 1, 0)])
    experts = route[:, 2:4].astype(jnp.int32)
    dest = pad_start[experts] + route[:, 4:6].astype(jnp.int32)
    dest2 = jnp.swapaxes(dest.reshape(t // ROUTE_TILE, ROUTE_TILE, EXPERT_TOP_K), 1, 2)
    dest2 = dest2.reshape(t // ROUTE_TILE, 1, EXPERT_TOP_K * ROUTE_TILE)
    x_disp = _moe_dispatch(xt, dest2, n_tiles * EXPERT_TILE)
    bf16 = jnp.bfloat16
    y_disp = _moe_experts(x_disp, tile_expert, n_used, w1.astype(bf16), w3.astype(bf16), w2.astype(bf16))
    return _moe_combine_ln(xt, y_disp, dest2, route, ln_g, ln_b).reshape(b, s, d)


def kernel(x, mem, mem_wk, mem_wv, ev_w_in, ev_conv_w, ev_conv_b, ev_cnorm_g, ev_cnorm_b, ev_cmp_pe_k, ev_cmp_w1_k, ev_cmp_w2_k, ev_cmp_pe_v, ev_cmp_w1_v, ev_cmp_w2_v, ev_w_out, od_w_in, od_conv_w, od_w_out, ln_mix_g, ln_mix_b, xa_wq, xa_wo, ln_xa_g, ln_xa_b, moe_wg, moe_bg, moe_we, moe_be, moe_w1, moe_w3, moe_w2, ln_ffn_g, ln_ffn_b):
    b = x.shape[0]
    m_len = mem.shape[1]
    mem_k = (mem @ mem_wk).reshape(b, m_len, XA_HEADS, XA_HEAD_DIM)
    mem_v = (mem @ mem_wv).reshape(b, m_len, XA_HEADS, XA_HEAD_DIM)
    for layer in range(DEPTH):
        i = layer // 2
        if layer % 2 == 0:
            mix = _even_mixer(x, ev_w_in[i], ev_conv_w[i], ev_conv_b[i], ev_cnorm_g[i], ev_cnorm_b[i], ev_cmp_pe_k[i], ev_cmp_w1_k[i], ev_cmp_w2_k[i], ev_cmp_pe_v[i], ev_cmp_w1_v[i], ev_cmp_w2_v[i], ev_w_out[i])
        else:
            mix = _odd_mixer(x, od_w_in[i], od_conv_w[i], od_w_out[i])
        x = _residual_ln(x, mix, ln_mix_g[layer], ln_mix_b[layer])
        x = _residual_ln(x, _memory_cross_attention(x, mem_k, mem_v, xa_wq[layer], xa_wo[layer]), ln_xa_g[layer], ln_xa_b[layer])
        x = _moe_sublayer(x, moe_wg[layer], moe_bg[layer], moe_we[layer], moe_be[layer], moe_w1[layer], moe_w3[layer], moe_w2[layer], ln_ffn_g[layer], ln_ffn_b[layer])
    return x
```

```python
import functools

import numpy as np
import jax
import jax.numpy as jnp
from jax import lax
from jax.experimental import pallas as pl
from jax.experimental.pallas import tpu as pltpu

D_MODEL = 1024
DEPTH = 2
CONV_CH = D_MODEL // 2
NSA_HEADS = 8
NSA_KV_HEADS = 2
HEAD_DIM = (D_MODEL // 2) // NSA_HEADS
CMP_BLOCK = 32
CMP_STRIDE = 16
SEL_BLOCK = 64
SEL_TOP_N = 16
WINDOW = 512
Q_BLOCK = 128
FORCE_SCORE = 1e4
XA_HEADS = 4
XA_HEAD_DIM = D_MODEL // XA_HEADS
N_GROUPS = 4
EXPERTS_PER_GROUP = 8
N_EXPERTS = N_GROUPS * EXPERTS_PER_GROUP
EXPERT_TOP_K = 2
DN_ALPHA = (2 * DEPTH) ** 0.25
LN_EPS = 1e-5
NEG_INF = -1e30
KV_COLS = NSA_KV_HEADS * HEAD_DIM
EVEN_SPLIT_SIZES = (CONV_CH, CONV_CH, NSA_HEADS * HEAD_DIM, KV_COLS, KV_COLS, KV_COLS, KV_COLS, KV_COLS, KV_COLS, 3 * NSA_HEADS)

LANES = 128
HEADS_PER_KV = NSA_HEADS // NSA_KV_HEADS
QL = Q_BLOCK * HEADS_PER_KV
SEL_TILE = 512
WIN_SPAN = WINDOW + Q_BLOCK
BLOCKS_PER_TILE = SEL_TILE // SEL_BLOCK
ROUTE_TILE = 512
EXPERT_TILE = 256
ROUTE_COLS = 8


def _ln_rows(y, g, b):
    mu = jnp.mean(y, axis=-1, keepdims=True)
    yc = y - mu
    var = jnp.mean(yc * yc, axis=-1, keepdims=True)
    return yc * lax.rsqrt(var + LN_EPS) * g + b


def _ln_kernel(x_ref, r_ref, g_ref, b_ref, o_ref):
    o_ref[...] = _ln_rows(DN_ALPHA * x_ref[...] + r_ref[...], g_ref[...], b_ref[...])


def _residual_ln(x, r, g, b, tm=512):
    shape = x.shape
    d = shape[-1]
    x2 = x.reshape(-1, d)
    r2 = r.reshape(-1, d)
    m = x2.shape[0]
    out = pl.pallas_call(
        _ln_kernel,
        name="residual_ln",
        grid=(m // tm,),
        in_specs=[pl.BlockSpec((tm, d), lambda i: (i, 0)), pl.BlockSpec((tm, d), lambda i: (i, 0)),
                  pl.BlockSpec((1, d), lambda i: (0, 0)), pl.BlockSpec((1, d), lambda i: (0, 0))],
        out_specs=pl.BlockSpec((tm, d), lambda i: (i, 0)),
        out_shape=jax.ShapeDtypeStruct((m, d), jnp.float32),
    )(x2, r2, g.reshape(1, d), b.reshape(1, d))
    return out.reshape(shape)


def _layer_norm(x, g, b):
    mu = jnp.mean(x, -1, keepdims=True)
    var = jnp.mean(jnp.square(x - mu), -1, keepdims=True)
    return (x - mu) * lax.rsqrt(var + LN_EPS) * g + b


def _causal_depthwise_conv(u, w):
    k = w.shape[0]
    return lax.conv_general_dilated(u, w[:, None, :], window_strides=(1,), padding=((k - 1, 0),), dimension_numbers=('NWC', 'WIO', 'NWC'), feature_group_count=u.shape[-1])


def _compress_blocks(kv, pe, w1, w2):
    b, s, g, dh = kv.shape
    n_cmp = (s - CMP_BLOCK) // CMP_STRIDE + 1
    idx = np.arange(n_cmp)[:, None] * CMP_STRIDE + np.arange(CMP_BLOCK)[None, :]
    blocks = kv[:, idx] + pe[None, None, :, None, :]
    blocks = jnp.transpose(blocks, (0, 1, 3, 2, 4)).reshape(b, n_cmp, g, CMP_BLOCK * dh)
    return jax.nn.gelu(blocks @ w1) @ w2


def _cmp_to_sel_matrix(n_cmp, n_sel):
    c0 = np.arange(n_cmp) * CMP_STRIDE
    s0 = np.arange(n_sel) * SEL_BLOCK
    ov = np.minimum(c0[:, None] + CMP_BLOCK, s0[None, :] + SEL_BLOCK) - np.maximum(c0[:, None], s0[None, :])
    return (np.clip(ov, 0, None) / CMP_BLOCK).astype(np.float32)


def _nsa_kernel(qT_ref, gT_ref, kc_ref, vcT_ref, mselT_ref, ks_ref, vsT_ref, kw_ref, vwT_ref, o_ref,
                score_ref, sel_ref, *, n_sel):
    c = pl.program_id(1)
    q0 = c * Q_BLOCK
    f32 = jnp.float32
    bf16 = jnp.bfloat16
    n_cmp_pad = kc_ref.shape[1]

    lane_q = lax.broadcasted_iota(jnp.int32, (1, QL), 1) % Q_BLOCK
    t_row = q0 + lane_q
    t_row_q = q0 + lax.broadcasted_iota(jnp.int32, (1, Q_BLOCK), 1)
    cur_q = t_row_q // SEL_BLOCK

    for g in range(NSA_KV_HEADS):
        pieces = []
        for hg in range(HEADS_PER_KV):
            h = g * HEADS_PER_KV + hg
            qh = qT_ref[0, h * HEAD_DIM:(h + 1) * HEAD_DIM, :] * jnp.asarray(HEAD_DIM ** -0.5, bf16)
            z = jnp.zeros_like(qh)
            pieces.append(jnp.concatenate([qh, z] if g == 0 else [z, qh], axis=0))
        qTp = jnp.concatenate(pieces, axis=1)
        rows = slice(g * HEAD_DIM, (g + 1) * HEAD_DIM)

        s_c = jnp.dot(kc_ref[0], qTp, preferred_element_type=f32)
        n_iota = lax.broadcasted_iota(jnp.int32, (n_cmp_pad, QL), 0)
        mask_c = (n_iota * CMP_STRIDE + (CMP_BLOCK - 1)) <= t_row
        s_c = jnp.where(mask_c, s_c, NEG_INF)
        m_c = jnp.max(s_c, axis=0, keepdims=True)
        p_c = jnp.where(mask_c, jnp.exp(s_c - m_c), 0.0)
        l_c = jnp.sum(p_c, axis=0, keepdims=True)
        p_c = p_c * jnp.where(l_c > 0.0, 1.0 / l_c, 0.0)
        p_cb = p_c.astype(bf16)
        o_c = jnp.dot(vcT_ref[0], p_cb, preferred_element_type=f32)[rows]
        imp4 = jnp.dot(mselT_ref[...], p_cb, preferred_element_type=f32)
        imp = imp4[:, 0:Q_BLOCK]
        for hg in range(1, HEADS_PER_KV):
            imp = imp + imp4[:, hg * Q_BLOCK:(hg + 1) * Q_BLOCK]

        m_iota = lax.broadcasted_iota(jnp.int32, (n_sel, Q_BLOCK), 0)
        forced = (m_iota == 0) | (m_iota == cur_q) | (m_iota == cur_q - 1)
        score = jnp.where(forced, FORCE_SCORE, imp)
        valid = m_iota <= cur_q
        score = jnp.where(valid, score, NEG_INF)
        score_ref[...] = score

        def rank_body(mp, rank):
            row = score_ref[pl.ds(mp, 1), :]
            beats = (row > score) | ((row == score) & (mp < m_iota))
            return rank + beats.astype(jnp.int32)

        n_comp = jnp.minimum(q0 // SEL_BLOCK + 2, n_sel)
        rank = lax.fori_loop(0, n_comp, rank_body, jnp.zeros((n_sel, Q_BLOCK), jnp.int32))
        top_n = min(SEL_TOP_N, n_sel)
        sel_ref[...] = ((rank < top_n) & valid).astype(f32)

        def sel_body(j, carry):
            m_i, l_i, acc = carry
            s = jnp.dot(ks_ref[0, j], qTp, preferred_element_type=f32)
            selrows = sel_ref[pl.ds(pl.multiple_of(j * BLOCKS_PER_TILE, BLOCKS_PER_TILE), BLOCKS_PER_TILE), :]
            blkmask = jnp.concatenate(
                [jnp.broadcast_to(selrows[r:r + 1, :], (SEL_BLOCK, Q_BLOCK)) for r in range(BLOCKS_PER_TILE)], axis=0)
            blkmask = jnp.concatenate([blkmask] * HEADS_PER_KV, axis=1)
            key = j * SEL_TILE + lax.broadcasted_iota(jnp.int32, (SEL_TILE, QL), 0)
            allowed = (blkmask > 0.5) & (key <= t_row)
            s = jnp.where(allowed, s, NEG_INF)
            m_new = jnp.maximum(m_i, jnp.max(s, axis=0, keepdims=True))
            alpha = jnp.exp(m_i - m_new)
            p = jnp.where(allowed, jnp.exp(s - m_new), 0.0)
            l_new = alpha * l_i + jnp.sum(p, axis=0, keepdims=True)
            pv = jnp.dot(vsT_ref[0, j], p.astype(bf16), preferred_element_type=f32)
            return m_new, l_new, alpha * acc + pv

        n_tiles = q0 // SEL_TILE + 1
        m_s, l_s, acc_s = lax.fori_loop(
            0, n_tiles, sel_body,
            (jnp.full((1, QL), NEG_INF, f32), jnp.zeros((1, QL), f32), jnp.zeros((KV_COLS, QL), f32)))
        o_s = acc_s[rows] * (1.0 / l_s)

        start = pl.multiple_of(jnp.maximum(q0 - WINDOW, 0), Q_BLOCK)
        s_w = jnp.dot(kw_ref[0, pl.ds(start, WIN_SPAN), :], qTp, preferred_element_type=f32)
        key_w = start + lax.broadcasted_iota(jnp.int32, (WIN_SPAN, QL), 0)
        mask_w = (key_w <= t_row) & (key_w > t_row - WINDOW)
        s_w = jnp.where(mask_w, s_w, NEG_INF)
        m_w = jnp.max(s_w, axis=0, keepdims=True)
        p_w = jnp.where(mask_w, jnp.exp(s_w - m_w), 0.0)
        l_w = jnp.sum(p_w, axis=0, keepdims=True)
        p_wb = p_w.astype(bf16)
        j0 = start // Q_BLOCK
        acc_w = jnp.zeros((KV_COLS, QL), f32)
        for i in range(WIN_SPAN // Q_BLOCK):
            acc_w = acc_w + jnp.dot(vwT_ref[0, j0 + i], p_wb[i * Q_BLOCK:(i + 1) * Q_BLOCK, :], preferred_element_type=f32)
        o_w = acc_w[rows] * (1.0 / l_w)

        for hg in range(HEADS_PER_KV):
            h = g * HEADS_PER_KV + hg
            lanes = slice(hg * Q_BLOCK, (hg + 1) * Q_BLOCK)
            gate = jax.nn.sigmoid(gT_ref[0, 3 * h:3 * h + 3, :])
            o_ref[0, h * HEAD_DIM:(h + 1) * HEAD_DIM, :] = (
                gate[0:1] * o_c[:, lanes] + gate[1:2] * o_s[:, lanes] + gate[2:3] * o_w[:, lanes])


def _nsa_attention(q, k_cmp, v_cmp, k_sel, v_sel, k_win, v_win, gate_logits):
    b, s, _ = q.shape
    bf16 = jnp.bfloat16
    n_sel = s // SEL_BLOCK
    n_cmp = k_cmp.shape[1]
    nc_pad = -(-n_cmp // LANES) * LANES
    qT = jnp.swapaxes(q.astype(bf16), 1, 2)
    gT = jnp.swapaxes(gate_logits, 1, 2)
    kc = jnp.pad(k_cmp.reshape(b, n_cmp, KV_COLS).astype(bf16), ((0, 0), (0, nc_pad - n_cmp), (0, 0)))
    vcT = jnp.swapaxes(jnp.pad(v_cmp.reshape(b, n_cmp, KV_COLS).astype(bf16), ((0, 0), (0, nc_pad - n_cmp), (0, 0))), 1, 2)
    mselT = jnp.asarray(np.pad(_cmp_to_sel_matrix(n_cmp, n_sel).T, ((0, 0), (0, nc_pad - n_cmp))), bf16)
    ks = k_sel.astype(bf16).reshape(b, s // SEL_TILE, SEL_TILE, KV_COLS)
    vsT = jnp.transpose(v_sel.astype(bf16).reshape(b, s // SEL_TILE, SEL_TILE, KV_COLS), (0, 1, 3, 2))
    kw = k_win.astype(bf16)
    vwT = jnp.transpose(v_win.astype(bf16).reshape(b, s // Q_BLOCK, Q_BLOCK, KV_COLS), (0, 1, 3, 2))
    hd = NSA_HEADS * HEAD_DIM
    outT = pl.pallas_call(
        functools.partial(_nsa_kernel, n_sel=n_sel),
        name="nsa_attention",
        grid=(b, s // Q_BLOCK),
        in_specs=[
            pl.BlockSpec((1, hd, Q_BLOCK), lambda i, c: (i, 0, c)),
            pl.BlockSpec((1, 3 * NSA_HEADS, Q_BLOCK), lambda i, c: (i, 0, c)),
            pl.BlockSpec((1, nc_pad, KV_COLS), lambda i, c: (i, 0, 0)),
            pl.BlockSpec((1, KV_COLS, nc_pad), lambda i, c: (i, 0, 0)),
            pl.BlockSpec((n_sel, nc_pad), lambda i, c: (0, 0)),
            pl.BlockSpec((1, s // SEL_TILE, SEL_TILE, KV_COLS), lambda i, c: (i, 0, 0, 0)),
            pl.BlockSpec((1, s // SEL_TILE, KV_COLS, SEL_TILE), lambda i, c: (i, 0, 0, 0)),
            pl.BlockSpec((1, s, KV_COLS), lambda i, c: (i, 0, 0)),
            pl.BlockSpec((1, s // Q_BLOCK, KV_COLS, Q_BLOCK), lambda i, c: (i, 0, 0, 0)),
        ],
        out_specs=pl.BlockSpec((1, hd, Q_BLOCK), lambda i, c: (i, 0, c)),
        out_shape=jax.ShapeDtypeStruct((b, hd, s), jnp.float32),
        scratch_shapes=[pltpu.VMEM((n_sel, Q_BLOCK), jnp.float32), pltpu.VMEM((n_sel, Q_BLOCK), jnp.float32)],
        compiler_params=pltpu.CompilerParams(dimension_semantics=("arbitrary", "arbitrary")),
    )(qT, gT, kc, vcT, mselT, ks, vsT, kw, vwT)
    return jnp.swapaxes(outT, 1, 2)


def _even_mixer(x, w_in, conv_w, conv_b, cn_g, cn_b, pe_k, w1_k, w2_k, pe_v, w1_v, w2_v, w_out):
    b, s, _ = x.shape
    split_pts = np.cumsum(EVEN_SPLIT_SIZES)[:-1].tolist()
    a_val, a_gate, q, kc, vc, ks, vs, kw, vw, gl = jnp.split(x @ w_in, split_pts, axis=-1)
    a = a_val * jax.nn.sigmoid(a_gate)
    a = _causal_depthwise_conv(a, conv_w) + conv_b
    a = jax.nn.silu(_layer_norm(a, cn_g, cn_b))
    kvs = lambda u: u.reshape(b, s, NSA_KV_HEADS, HEAD_DIM)
    k_cmp = _compress_blocks(kvs(kc), pe_k, w1_k, w2_k)
    v_cmp = _compress_blocks(kvs(vc), pe_v, w1_v, w2_v)
    o = _nsa_attention(q, k_cmp, v_cmp, ks, vs, kw, vw, gl)
    return jnp.concatenate([a, o], axis=-1) @ w_out


def _odd_mixer(x, w_in, conv_w, w_out):
    gate_b, gate_c, h = jnp.split(x @ w_in, 3, axis=-1)
    return (gate_b * _causal_depthwise_conv(gate_c * h, conv_w)) @ w_out


def _memory_cross_attention(x, mem_k, mem_v, wq, wo):
    b, s, d = x.shape
    q = (x @ wq).reshape(b, s, XA_HEADS, XA_HEAD_DIM)
    sc = jnp.einsum('bshd,bmhd->bhsm', q, mem_k, preferred_element_type=jnp.float32) * XA_HEAD_DIM ** -0.5
    p = jax.nn.softmax(sc, axis=-1)
    return jnp.einsum('bhsm,bmhd->bshd', p, mem_v).reshape(b, s, d) @ wo


def _router_kernel(x_ref, w_ref, b_ref, tri_ref, route_ref, cnt_ref, run_ref):
    i = pl.program_id(0)
    f32 = jnp.float32

    @pl.when(i == 0)
    def _():
        run_ref[...] = jnp.zeros_like(run_ref)

    tm = x_ref.shape[0]
    logits = jnp.dot(x_ref[...].astype(jnp.bfloat16), w_ref[...], preferred_element_type=f32) + b_ref[...]
    lane = lax.broadcasted_iota(jnp.int32, (tm, LANES), 1)
    is_g = lane < N_GROUPS
    gl = jnp.where(is_g, logits, NEG_INF)
    g_max = jnp.max(gl, axis=1, keepdims=True)
    g_star = jnp.min(jnp.where(gl == g_max, lane, LANES), axis=1, keepdims=True)
    p_group = 1.0 / jnp.sum(jnp.where(is_g, jnp.exp(gl - g_max), 0.0), axis=1, keepdims=True)
    lo = N_GROUPS + g_star * EXPERTS_PER_GROUP
    in_grp = (lane >= lo) & (lane < lo + EXPERTS_PER_GROUP)
    el = jnp.where(in_grp, logits, NEG_INF)
    v1 = jnp.max(el, axis=1, keepdims=True)
    i1 = jnp.min(jnp.where(el == v1, lane, LANES), axis=1, keepdims=True)
    el2 = jnp.where(lane == i1, NEG_INF, el)
    v2 = jnp.max(el2, axis=1, keepdims=True)
    i2 = jnp.min(jnp.where(el2 == v2, lane, LANES), axis=1, keepdims=True)
    e21 = jnp.exp(v2 - v1)
    gate1 = p_group * (1.0 / (1.0 + e21))
    gate2 = p_group * (e21 / (1.0 + e21))
    oh1 = (lane == i1).astype(f32)
    oh2 = (lane == i2).astype(f32)
    both = (oh1 + oh2).astype(jnp.bfloat16)
    before = jnp.dot(tri_ref[...], both, preferred_element_type=f32) + run_ref[...]
    rank1 = jnp.sum(oh1 * before, axis=1, keepdims=True)
    rank2 = jnp.sum(oh2 * before, axis=1, keepdims=True)
    run_ref[...] = run_ref[...] + jnp.sum(oh1 + oh2, axis=0, keepdims=True)
    cnt_ref[...] = run_ref[...]
    zero = jnp.zeros_like(gate1)
    route_ref[...] = jnp.concatenate(
        [gate1, gate2, (i1 - N_GROUPS).astype(f32), (i2 - N_GROUPS).astype(f32), rank1, rank2, zero, zero], axis=1)


def _moe_route(xt, wg, bg, we, be):
    t, d = xt.shape
    pad = LANES - N_GROUPS - N_EXPERTS
    w = jnp.concatenate([wg, we.reshape(d, N_EXPERTS), jnp.zeros((d, pad), wg.dtype)], axis=1)
    bias = jnp.concatenate([bg, be.reshape(N_EXPERTS), jnp.zeros((pad,), bg.dtype)])[None, :]
    tri = jnp.asarray(np.tril(np.ones((ROUTE_TILE, ROUTE_TILE), np.float32), -1), jnp.bfloat16)
    return pl.pallas_call(
        _router_kernel,
        name="moe_router",
        grid=(t // ROUTE_TILE,),
        in_specs=[pl.BlockSpec((ROUTE_TILE, d), lambda i: (i, 0)),
                  pl.BlockSpec((d, LANES), lambda i: (0, 0)),
                  pl.BlockSpec((1, LANES), lambda i: (0, 0)),
                  pl.BlockSpec((ROUTE_TILE, ROUTE_TILE), lambda i: (0, 0))],
        out_specs=[pl.BlockSpec((ROUTE_TILE, ROUTE_COLS), lambda i: (i, 0)),
                   pl.BlockSpec((1, LANES), lambda i: (0, 0))],
        out_shape=[jax.ShapeDtypeStruct((t, ROUTE_COLS), jnp.float32), jax.ShapeDtypeStruct((1, LANES), jnp.float32)],
        scratch_shapes=[pltpu.VMEM((1, LANES), jnp.float32)],
        compiler_params=pltpu.CompilerParams(dimension_semantics=("arbitrary",)),
    )(xt, w.astype(jnp.bfloat16), bias, tri)


def _dispatch_kernel(dest_ref, x_ref, zero_hbm, xd_hbm, sem):
    del zero_hbm
    tm = x_ref.shape[0]

    def body(r, c):
        for k in range(EXPERT_TOP_K):
            pltpu.make_async_copy(
                x_ref.at[pl.ds(r, 1)], xd_hbm.at[pl.ds(dest_ref[0, 0, k * tm + r], 1)], sem).start()
        return c

    lax.fori_loop(0, tm, body, 0, unroll=8)
    for k in range(EXPERT_TOP_K):
        pltpu.make_async_copy(x_ref, xd_hbm.at[pl.ds(0, tm)], sem).wait()


def _moe_dispatch(xt, dest2, n_rows):
    t, d = xt.shape
    return pl.pallas_call(
        _dispatch_kernel,
        name="moe_dispatch",
        grid=(t // ROUTE_TILE,),
        in_specs=[pl.BlockSpec((1, 1, EXPERT_TOP_K * ROUTE_TILE), lambda i: (i, 0, 0), memory_space=pltpu.SMEM),
                  pl.BlockSpec((ROUTE_TILE, d), lambda i: (i, 0)),
                  pl.BlockSpec(memory_space=pl.ANY)],
        out_specs=pl.BlockSpec(memory_space=pl.ANY),
        out_shape=jax.ShapeDtypeStruct((n_rows, d), xt.dtype),
        scratch_shapes=[pltpu.SemaphoreType.DMA(())],
        input_output_aliases={2: 0},
        compiler_params=pltpu.CompilerParams(dimension_semantics=("arbitrary",), has_side_effects=True),
    )(dest2, xt, jnp.zeros((n_rows, d), xt.dtype))


def _expert_kernel(te_ref, nu_ref, xd_ref, w1_ref, w3_ref, w2_ref, y_ref):
    del te_ref
    i = pl.program_id(0)

    @pl.when(i < nu_ref[0])
    def _():
        xb = xd_ref[...].astype(jnp.bfloat16)
        h1 = jnp.dot(xb, w1_ref[0], preferred_element_type=jnp.float32)
        h3 = jnp.dot(xb, w3_ref[0], preferred_element_type=jnp.float32)
        a = (h1 * jax.nn.sigmoid(h1) * h3).astype(jnp.bfloat16)
        y_ref[...] = jnp.dot(a, w2_ref[0], preferred_element_type=jnp.float32)

    @pl.when(i >= nu_ref[0])
    def _():
        y_ref[...] = jnp.zeros_like(y_ref)


def _moe_experts(x_disp, tile_expert, n_used, w1, w3, w2):
    n_rows, d = x_disp.shape
    n_tiles = n_rows // EXPERT_TILE
    hid = w1.shape[2]

    def row_map(i, te, nu):
        return (i, 0)

    def w_map(i, te, nu):
        return (te[i], 0, 0)

    grid_spec = pltpu.PrefetchScalarGridSpec(
        num_scalar_prefetch=2,
        grid=(n_tiles,),
        in_specs=[pl.BlockSpec((EXPERT_TILE, d), row_map),
                  pl.BlockSpec((1, d, hid), w_map),
                  pl.BlockSpec((1, d, hid), w_map),
                  pl.BlockSpec((1, hid, d), w_map)],
        out_specs=pl.BlockSpec((EXPERT_TILE, d), row_map),
    )
    return pl.pallas_call(
        _expert_kernel,
        name="moe_experts",
        grid_spec=grid_spec,
        out_shape=jax.ShapeDtypeStruct((n_rows, d), jnp.float32),
        compiler_params=pltpu.CompilerParams(dimension_semantics=("arbitrary",)),
    )(tile_expert, n_used, x_disp, w1, w3, w2)


def _combine_kernel(dest_ref, dest_next_ref, x_ref, route_ref, g_ref, b_ref, yd_hbm, o_ref, ybuf, sem):
    i = pl.program_id(0)
    n = pl.num_programs(0)
    tm = x_ref.shape[0]
    rows = EXPERT_TOP_K * tm

    def start_tile(dref, slot):
        def body(r, c):
            pltpu.make_async_copy(
                yd_hbm.at[pl.ds(dref[0, 0, r], 1)], ybuf.at[slot, pl.ds(r, 1)], sem.at[slot]).start()
            return c
        lax.fori_loop(0, rows, body, 0, unroll=8)

    slot = i % 2

    @pl.when(i == 0)
    def _():
        start_tile(dest_ref, 0)

    @pl.when(i + 1 < n)
    def _():
        start_tile(dest_next_ref, 1 - slot)

    pltpu.make_async_copy(yd_hbm.at[pl.ds(0, rows)], ybuf.at[slot], sem.at[slot]).wait()
    y1 = ybuf[slot, 0:tm, :]
    y2 = ybuf[slot, tm:rows, :]
    route = route_ref[...]
    y = DN_ALPHA * x_ref[...] + (y1 * route[:, 0:1] + y2 * route[:, 1:2])
    o_ref[...] = _ln_rows(y, g_ref[...], b_ref[...])


def _moe_combine_ln(xt, y_disp, dest2, route, ln_g, ln_b):
    t, d = xt.shape
    n = t // ROUTE_TILE
    rows = EXPERT_TOP_K * ROUTE_TILE
    return pl.pallas_call(
        _combine_kernel,
        name="moe_combine_ln",
        grid=(n,),
        in_specs=[pl.BlockSpec((1, 1, rows), lambda i: (i, 0, 0), memory_space=pltpu.SMEM),
                  pl.BlockSpec((1, 1, rows), lambda i: (jnp.minimum(i + 1, n - 1), 0, 0), memory_space=pltpu.SMEM),
                  pl.BlockSpec((ROUTE_TILE, d), lambda i: (i, 0)),
                  pl.BlockSpec((ROUTE_TILE, ROUTE_COLS), lambda i: (i, 0)),
                  pl.BlockSpec((1, d), lambda i: (0, 0)),
                  pl.BlockSpec((1, d), lambda i: (0, 0)),
                  pl.BlockSpec(memory_space=pl.ANY)],
        out_specs=pl.BlockSpec((ROUTE_TILE, d), lambda i: (i, 0)),
        out_shape=jax.ShapeDtypeStruct((t, d), jnp.float32),
        scratch_shapes=[pltpu.VMEM((2, rows, d), jnp.float32), pltpu.SemaphoreType.DMA((2,))],
        compiler_params=pltpu.CompilerParams(dimension_semantics=("arbitrary",), vmem_limit_bytes=48 * 1024 * 1024),
    )(dest2, dest2, xt, route, ln_g.reshape(1, d), ln_b.reshape(1, d), y_disp)


def _moe_sublayer(x, wg, bg, we, be, w1, w3, w2, ln_g, ln_b):
    b, s, d = x.shape
    t = b * s
    xt = x.reshape(t, d)
    route, cnt = _moe_route(xt, wg, bg, we, be)
    counts = cnt[0, N_GROUPS:N_GROUPS + N_EXPERTS].astype(jnp.int32)
    n_tiles = (t * EXPERT_TOP_K) // EXPERT_TILE + N_EXPERTS
    tiles_per = (counts + EXPERT_TILE - 1) // EXPERT_TILE
    tile_end = jnp.cumsum(tiles_per)
    pad_start = (tile_end - tiles_per) * EXPERT_TILE
    n_used = tile_end[-1:].astype(jnp.int32)
    tile_ids = jnp.arange(n_tiles)
    tile_expert = jnp.minimum(jnp.searchsorted(tile_end, tile_ids, side='right'), N_EXPERTS - 1).astype(jnp.int32)
    tile_expert = jnp.where(tile_ids < n_used[0], tile_expert, tile_expert[jnp.maximum(n_used[0] - 1, 0)])
    experts = route[:, 2:4].astype(jnp.int32)
    dest = pad_start[experts] + route[:, 4:6].astype(jnp.int32)
    dest2 = jnp.swapaxes(dest.reshape(t // ROUTE_TILE, ROUTE_TILE, EXPERT_TOP_K), 1, 2)
    dest2 = dest2.reshape(t // ROUTE_TILE, 1, EXPERT_TOP_K * ROUTE_TILE)
    x_disp = _moe_dispatch(xt, dest2, n_tiles * EXPERT_TILE)
    bf16 = jnp.bfloat16
    y_disp = _moe_experts(x_disp, tile_expert, n_used, w1.astype(bf16), w3.astype(bf16), w2.astype(bf16))
    return _moe_combine_ln(xt, y_disp, dest2, route, ln_g, ln_b).reshape(b, s, d)


def kernel(x, mem, mem_wk, mem_wv, ev_w_in, ev_conv_w, ev_conv_b, ev_cnorm_g, ev_cnorm_b, ev_cmp_pe_k, ev_cmp_w1_k, ev_cmp_w2_k, ev_cmp_pe_v, ev_cmp_w1_v, ev_cmp_w2_v, ev_w_out, od_w_in, od_conv_w, od_w_out, ln_mix_g, ln_mix_b, xa_wq, xa_wo, ln_xa_g, ln_xa_b, moe_wg, moe_bg, moe_we, moe_be, moe_w1, moe_w3, moe_w2, ln_ffn_g, ln_ffn_b):
    b = x.shape[0]
    m_len = mem.shape[1]
    mem_k = (mem @ mem_wk).reshape(b, m_len, XA_HEADS, XA_HEAD_DIM)
    mem_v = (mem @ mem_wv).reshape(b, m_len, XA_HEADS, XA_HEAD_DIM)
    for layer in range(DEPTH):
        i = layer // 2
        if layer % 2 == 0:
            mix = _even_mixer(x, ev_w_in[i], ev_conv_w[i], ev_conv_b[i], ev_cnorm_g[i], ev_cnorm_b[i], ev_cmp_pe_k[i], ev_cmp_w1_k[i], ev_cmp_w2_k[i], ev_cmp_pe_v[i], ev_cmp_w1_v[i], ev_cmp_w2_v[i], ev_w_out[i])
        else:
            mix = _odd_mixer(x, od_w_in[i], od_conv_w[i], od_w_out[i])
        x = _residual_ln(x, mix, ln_mix_g[layer], ln_mix_b[layer])
        x = _residual_ln(x, _memory_cross_attention(x, mem_k, mem_v, xa_wq[layer], xa_wo[layer]), ln_xa_g[layer], ln_xa_b[layer])
        x = _moe_sublayer(x, moe_wg[layer], moe_bg[layer], moe_we[layer], moe_be[layer], moe_w1[layer], moe_w3[layer], moe_w2[layer], ln_ffn_g[layer], ln_ffn_b[layer])
    return x
```

```python
import functools

import numpy as np
import jax
import jax.numpy as jnp
from jax import lax
from jax.experimental import pallas as pl
from jax.experimental.pallas import tpu as pltpu

D_MODEL = 1024
DEPTH = 2
CONV_CH = D_MODEL // 2
CONV_WIDTH = 31
NSA_HEADS = 8
NSA_KV_HEADS = 2
HEAD_DIM = (D_MODEL // 2) // NSA_HEADS
CMP_BLOCK = 32
CMP_STRIDE = 16
SEL_BLOCK = 64
SEL_TOP_N = 16
WINDOW = 512
Q_BLOCK = 128
FORCE_SCORE = 1e4
SHORT_CONV_WIDTH = 3
XA_HEADS = 4
XA_HEAD_DIM = D_MODEL // XA_HEADS
N_GROUPS = 4
EXPERTS_PER_GROUP = 8
N_EXPERTS = N_GROUPS * EXPERTS_PER_GROUP
EXPERT_TOP_K = 2
DN_ALPHA = (2 * DEPTH) ** 0.25
LN_EPS = 1e-5
NEG_INF = -1e30
KV_COLS = NSA_KV_HEADS * HEAD_DIM
QCOLS = NSA_HEADS * HEAD_DIM
GATE_ROWS = 3 * NSA_HEADS

LANES = 128
HEADS_PER_KV = NSA_HEADS // NSA_KV_HEADS
QL = Q_BLOCK * HEADS_PER_KV
SEL_TILE = 512
WIN_SPAN = WINDOW + Q_BLOCK
BLOCKS_PER_TILE = SEL_TILE // SEL_BLOCK
SEQ_TILE = 512
HALO = 32
ODD_HALO = 8
ROUTE_TILE = 512
EXPERT_TILE = 256
ROUTE_COLS = 8
VMEM_LIMIT = 56 * 1024 * 1024

_NT = (((1,), (1,)), ((), ()))


def _ln_rows(y, g, b):
    mu = jnp.mean(y, axis=-1, keepdims=True)
    yc = y - mu
    var = jnp.mean(yc * yc, axis=-1, keepdims=True)
    return yc * lax.rsqrt(var + LN_EPS) * g + b


def _even_in_kernel(x_ref, wa_ref, wkv_ref, wk2_ref, wqT_ref, wvT_ref, wgT_ref,
                    a_ref, kv_ref, ks_ref, kw_ref, qT_ref, vsT_ref, vwT_ref, gT_ref):
    f32 = jnp.float32
    bf16 = jnp.bfloat16
    xb = x_ref[0].astype(bf16)
    av = jnp.dot(xb, wa_ref[...], preferred_element_type=f32)
    a_ref[0] = av[:, :CONV_CH] * jax.nn.sigmoid(av[:, CONV_CH:])
    kv_ref[0] = jnp.dot(xb, wkv_ref[...], preferred_element_type=f32)
    k2 = jnp.dot(xb, wk2_ref[...], preferred_element_type=f32)
    ks_ref[0] = k2[:, :KV_COLS].astype(bf16)
    kw_ref[0] = k2[:, KV_COLS:].astype(bf16)
    qT_ref[0] = lax.dot_general(wqT_ref[...], xb, _NT, preferred_element_type=f32).astype(bf16)
    vT = lax.dot_general(wvT_ref[...], xb, _NT, preferred_element_type=f32).astype(bf16)
    vsT_ref[0, 0] = vT[:KV_COLS]
    for j in range(SEQ_TILE // Q_BLOCK):
        vwT_ref[0, j] = vT[KV_COLS:, j * Q_BLOCK:(j + 1) * Q_BLOCK]
    gT_ref[0] = lax.dot_general(wgT_ref[...], xb, _NT, preferred_element_type=f32)


def _even_in_proj(x, w_in):
    b, s, d = x.shape
    bf16 = jnp.bfloat16
    c = np.cumsum((0, CONV_CH, CONV_CH, QCOLS, KV_COLS, KV_COLS, KV_COLS, KV_COLS, KV_COLS, KV_COLS, GATE_ROWS))
    col = lambda i, j: w_in[:, c[i]:c[j]]
    wa = col(0, 2).astype(bf16)
    wq_t = col(2, 3).T.astype(bf16)
    wkv = col(3, 5).astype(bf16)
    wk2 = jnp.concatenate([col(5, 6), col(7, 8)], axis=1).astype(bf16)
    wv_t = jnp.concatenate([col(6, 7), col(8, 9)], axis=1).T.astype(bf16)
    wg_t = col(9, 10).T.astype(bf16)
    ts = SEQ_TILE
    full = lambda shape: pl.BlockSpec(shape, lambda i, j: (0,) * len(shape))
    return pl.pallas_call(
        _even_in_kernel,
        name="even_in_proj",
        grid=(b, s // ts),
        in_specs=[pl.BlockSpec((1, ts, d), lambda i, j: (i, j, 0)),
                  full(wa.shape), full(wkv.shape), full(wk2.shape), full(wq_t.shape), full(wv_t.shape), full(wg_t.shape)],
        out_specs=[pl.BlockSpec((1, ts, CONV_CH), lambda i, j: (i, j, 0)),
                   pl.BlockSpec((1, ts, 2 * KV_COLS), lambda i, j: (i, j, 0)),
                   pl.BlockSpec((1, ts, KV_COLS), lambda i, j: (i, j, 0)),
                   pl.BlockSpec((1, ts, KV_COLS), lambda i, j: (i, j, 0)),
                   pl.BlockSpec((1, QCOLS, ts), lambda i, j: (i, 0, j)),
                   pl.BlockSpec((1, 1, KV_COLS, ts), lambda i, j: (i, j, 0, 0)),
                   pl.BlockSpec((1, ts // Q_BLOCK, KV_COLS, Q_BLOCK), lambda i, j: (i, j, 0, 0)),
                   pl.BlockSpec((1, GATE_ROWS, ts), lambda i, j: (i, 0, j))],
        out_shape=[jax.ShapeDtypeStruct((b, s, CONV_CH), jnp.float32),
                   jax.ShapeDtypeStruct((b, s, 2 * KV_COLS), jnp.float32),
                   jax.ShapeDtypeStruct((b, s, KV_COLS), bf16),
                   jax.ShapeDtypeStruct((b, s, KV_COLS), bf16),
                   jax.ShapeDtypeStruct((b, QCOLS, s), bf16),
                   jax.ShapeDtypeStruct((b, s // SEL_TILE, KV_COLS, SEL_TILE), bf16),
                   jax.ShapeDtypeStruct((b, s // Q_BLOCK, KV_COLS, Q_BLOCK), bf16),
                   jax.ShapeDtypeStruct((b, GATE_ROWS, s), jnp.float32)],
        compiler_params=pltpu.CompilerParams(dimension_semantics=("arbitrary", "arbitrary"), vmem_limit_bytes=VMEM_LIMIT),
    )(x, wa, wkv, wk2, wq_t, wv_t, wg_t)


def _conv_kernel(cur_ref, halo_ref, w_ref, cb_ref, g_ref, b_ref, o_ref, ext_ref):
    j = pl.program_id(1)
    ts = cur_ref.shape[1]
    halo = halo_ref[0]
    ext_ref[0:HALO, :] = jnp.where(j > 0, halo, jnp.zeros_like(halo))
    ext_ref[HALO:HALO + ts, :] = cur_ref[0]
    acc = jnp.zeros((ts, CONV_CH), jnp.float32)
    for k in range(CONV_WIDTH):
        off = HALO - (CONV_WIDTH - 1) + k
        acc = acc + w_ref[k:k + 1, :] * ext_ref[off:off + ts, :]
    y = _ln_rows(acc + cb_ref[...], g_ref[...], b_ref[...])
    o_ref[0] = (y * jax.nn.sigmoid(y)).astype(o_ref.dtype)


def _conformer_conv(a, conv_w, conv_b, cn_g, cn_b):
    b, s, c = a.shape
    ts = SEQ_TILE
    per = ts // HALO
    row = lambda v: v.reshape(1, c)
    return pl.pallas_call(
        _conv_kernel,
        name="conformer_conv",
        grid=(b, s // ts),
        in_specs=[pl.BlockSpec((1, ts, c), lambda i, j: (i, j, 0)),
                  pl.BlockSpec((1, HALO, c), lambda i, j: (i, jnp.maximum(j * per - 1, 0), 0)),
                  pl.BlockSpec((CONV_WIDTH, c), lambda i, j: (0, 0)),
                  pl.BlockSpec((1, c), lambda i, j: (0, 0)),
                  pl.BlockSpec((1, c), lambda i, j: (0, 0)),
                  pl.BlockSpec((1, c), lambda i, j: (0, 0))],
        out_specs=pl.BlockSpec((1, ts, c), lambda i, j: (i, j, 0)),
        out_shape=jax.ShapeDtypeStruct((b, s, c), jnp.bfloat16),
        scratch_shapes=[pltpu.VMEM((HALO + ts, c), jnp.float32)],
        compiler_params=pltpu.CompilerParams(dimension_semantics=("arbitrary", "arbitrary")),
    )(a, a, conv_w, row(conv_b), row(cn_g), row(cn_b))


def _compress_kernel(r_ref, pe_ref, w1_ref, w2_ref, o_ref):
    f32 = jnp.float32
    bf16 = jnp.bfloat16
    r = r_ref[0, 0]
    half = r.shape[1]
    ha = jnp.dot((r + pe_ref[0, 0:1, :]).astype(bf16), w1_ref[0, :half, :], preferred_element_type=f32)
    hb = jnp.dot((r + pe_ref[0, 1:2, :]).astype(bf16), w1_ref[0, half:, :], preferred_element_type=f32)
    hb_next = jnp.concatenate([hb[1:], jnp.zeros_like(hb[0:1])], axis=0)
    hid = jax.nn.gelu(ha + hb_next).astype(bf16)
    o_ref[0, 0] = jnp.dot(hid, w2_ref[0], preferred_element_type=f32).astype(bf16)


def _compress_kv(kv_in, pe_k, w1_k, w2_k, pe_v, w1_v, w2_v):
    b, s, _ = kv_in.shape
    bf16 = jnp.bfloat16
    nch = s // CMP_STRIDE
    g = NSA_KV_HEADS
    r = kv_in.reshape(b, nch, CMP_STRIDE, 2 * g, HEAD_DIM)
    r = jnp.transpose(r, (0, 3, 1, 2, 4)).reshape(b, 2 * g, nch, CMP_STRIDE * HEAD_DIM)
    pe = jnp.stack([pe_k, pe_v]).reshape(2, 2, CMP_STRIDE * HEAD_DIM)
    w1 = jnp.stack([w1_k, w1_v]).astype(bf16)
    w2 = jnp.stack([w2_k, w2_v]).astype(bf16)
    out = pl.pallas_call(
        _compress_kernel,
        name="compress_kv",
        grid=(b, 2 * g),
        in_specs=[pl.BlockSpec((1, 1, nch, r.shape[3]), lambda i, j: (i, j, 0, 0)),
                  pl.BlockSpec((1, 2, pe.shape[2]), lambda i, j: (j // g, 0, 0)),
                  pl.BlockSpec((1,) + w1.shape[1:], lambda i, j: (j // g, 0, 0)),
                  pl.BlockSpec((1,) + w2.shape[1:], lambda i, j: (j // g, 0, 0))],
        out_specs=pl.BlockSpec((1, 1, nch, HEAD_DIM), lambda i, j: (i, j, 0, 0)),
        out_shape=jax.ShapeDtypeStruct((b, 2 * g, nch, HEAD_DIM), bf16),
        compiler_params=pltpu.CompilerParams(dimension_semantics=("arbitrary", "arbitrary")),
    )(r, pe, w1, w2)
    kc = jnp.transpose(out[:, :g], (0, 2, 1, 3)).reshape(b, nch, KV_COLS)
    vcT = jnp.transpose(out[:, g:], (0, 1, 3, 2)).reshape(b, KV_COLS, nch)
    return kc, vcT


def _proj_ln_kernel(a_ref, o_ref, x_ref, w_ref, g_ref, b_ref, y_ref):
    lhs = jnp.concatenate([a_ref[...], o_ref[...]], axis=1)
    mix = jnp.dot(lhs, w_ref[...], preferred_element_type=jnp.float32)
    y_ref[...] = _ln_rows(DN_ALPHA * x_ref[...] + mix, g_ref[...], b_ref[...])


def _proj_residual_ln(a, o, x, w, ln_g, ln_b):
    t, d = x.shape
    tm = SEQ_TILE
    return pl.pallas_call(
        _proj_ln_kernel,
        name="proj_residual_ln",
        grid=(t // tm,),
        in_specs=[pl.BlockSpec((tm, a.shape[1]), lambda i: (i, 0)),
                  pl.BlockSpec((tm, o.shape[1]), lambda i: (i, 0)),
                  pl.BlockSpec((tm, d), lambda i: (i, 0)),
                  pl.BlockSpec(w.shape, lambda i: (0, 0)),
                  pl.BlockSpec((1, d), lambda i: (0, 0)),
                  pl.BlockSpec((1, d), lambda i: (0, 0))],
        out_specs=pl.BlockSpec((tm, d), lambda i: (i, 0)),
        out_shape=jax.ShapeDtypeStruct((t, d), jnp.float32),
        compiler_params=pltpu.CompilerParams(dimension_semantics=("arbitrary",), vmem_limit_bytes=VMEM_LIMIT),
    )(a, o, x, w.astype(jnp.bfloat16), ln_g.reshape(1, d), ln_b.reshape(1, d))


def _mem_kv_kernel(m_ref, wk_ref, wv_ref, k_ref, v_ref):
    mb = m_ref[...].astype(jnp.bfloat16)
    k_ref[...] = jnp.dot(mb, wk_ref[...], preferred_element_type=jnp.float32).astype(jnp.bfloat16)
    v_ref[...] = jnp.dot(mb, wv_ref[...], preferred_element_type=jnp.float32).astype(jnp.bfloat16)


def _mem_kv(mem, wk, wv):
    b, m, d = mem.shape
    bf16 = jnp.bfloat16
    k, v = pl.pallas_call(
        _mem_kv_kernel,
        name="mem_kv",
        grid=(b,),
        in_specs=[pl.BlockSpec((m, d), lambda i: (i, 0)), pl.BlockSpec((d, d), lambda i: (0, 0)), pl.BlockSpec((d, d), lambda i: (0, 0))],
        out_specs=[pl.BlockSpec((m, d), lambda i: (i, 0)), pl.BlockSpec((m, d), lambda i: (i, 0))],
        out_shape=[jax.ShapeDtypeStruct((b * m, d), bf16), jax.ShapeDtypeStruct((b * m, d), bf16)],
        compiler_params=pltpu.CompilerParams(dimension_semantics=("arbitrary",), vmem_limit_bytes=VMEM_LIMIT),
    )(mem.reshape(b * m, d), wk.astype(bf16), wv.astype(bf16))
    return k.reshape(b, m, d), v.reshape(b, m, d)


def _xattn_kernel(x_ref, k_ref, v_ref, wq_ref, wo_ref, g_ref, b_ref, y_ref):
    f32 = jnp.float32
    bf16 = jnp.bfloat16
    x = x_ref[0]
    q = (jnp.dot(x.astype(bf16), wq_ref[...], preferred_element_type=f32).astype(bf16)
         * jnp.asarray(XA_HEAD_DIM ** -0.5, bf16))
    heads = []
    for h in range(XA_HEADS):
        cols = slice(h * XA_HEAD_DIM, (h + 1) * XA_HEAD_DIM)
        s = lax.dot_general(q[:, cols], k_ref[0, :, cols], _NT, preferred_element_type=f32)
        m = jnp.max(s, axis=1, keepdims=True)
        p = jnp.exp(s - m)
        p = p / jnp.sum(p, axis=1, keepdims=True)
        heads.append(jnp.dot(p.astype(bf16), v_ref[0, :, cols], preferred_element_type=f32).astype(bf16))
    att = jnp.concatenate(heads, axis=1)
    out = jnp.dot(att, wo_ref[...], preferred_element_type=f32)
    y_ref[0] = _ln_rows(DN_ALPHA * x + out, g_ref[...], b_ref[...])


def _xattn_sublayer(x, mem_k, mem_v, wq, wo, ln_g, ln_b):
    b, s, d = x.shape
    m = mem_k.shape[1]
    ts = SEQ_TILE
    bf16 = jnp.bfloat16
    return pl.pallas_call(
        _xattn_kernel,
        name="xattn_sublayer",
        grid=(b, s // ts),
        in_specs=[pl.BlockSpec((1, ts, d), lambda i, j: (i, j, 0)),
                  pl.BlockSpec((1, m, d), lambda i, j: (i, 0, 0)),
                  pl.BlockSpec((1, m, d), lambda i, j: (i, 0, 0)),
                  pl.BlockSpec((d, d), lambda i, j: (0, 0)),
                  pl.BlockSpec((d, d), lambda i, j: (0, 0)),
                  pl.BlockSpec((1, d), lambda i, j: (0, 0)),
                  pl.BlockSpec((1, d), lambda i, j: (0, 0))],
        out_specs=pl.BlockSpec((1, ts, d), lambda i, j: (i, j, 0)),
        out_shape=jax.ShapeDtypeStruct((b, s, d), jnp.float32),
        compiler_params=pltpu.CompilerParams(dimension_semantics=("arbitrary", "arbitrary"), vmem_limit_bytes=VMEM_LIMIT),
    )(x, mem_k, mem_v, wq.astype(bf16), wo.astype(bf16), ln_g.reshape(1, d), ln_b.reshape(1, d))


def _odd_kernel(x_ref, halo_ref, wb_ref, wc_ref, wh_ref, cw_ref, wo_ref, g_ref, b_ref, y_ref):
    f32 = jnp.float32
    bf16 = jnp.bfloat16
    j = pl.program_id(1)
    ts = x_ref.shape[1]
    x = x_ref[0]
    xe = jnp.concatenate([halo_ref[0], x], axis=0).astype(bf16)
    u = (jnp.dot(xe, wc_ref[...], preferred_element_type=f32) * jnp.dot(xe, wh_ref[...], preferred_element_type=f32))
    row = lax.broadcasted_iota(jnp.int32, (ODD_HALO + ts, 1), 0)
    u = jnp.where((row >= ODD_HALO) | (j > 0), u, 0.0)
    conv = jnp.zeros((ts, u.shape[1]), f32)
    for k in range(SHORT_CONV_WIDTH):
        off = ODD_HALO - (SHORT_CONV_WIDTH - 1) + k
        conv = conv + cw_ref[k:k + 1, :] * u[off:off + ts, :]
    gate_b = jnp.dot(xe[ODD_HALO:], wb_ref[...], preferred_element_type=f32)
    mix = jnp.dot((gate_b * conv).astype(bf16), wo_ref[...], preferred_element_type=f32)
    y_ref[0] = _ln_rows(DN_ALPHA * x + mix, g_ref[...], b_ref[...])


def _odd_mixer_sublayer(x, w_in, conv_w, w_out, ln_g, ln_b):
    b, s, d = x.shape
    ts = SEQ_TILE
    per = ts // ODD_HALO
    bf16 = jnp.bfloat16
    wb, wc, wh = (w_in[:, i * d:(i + 1) * d].astype(bf16) for i in range(3))
    full = lambda shape: pl.BlockSpec(shape, lambda i, j: (0,) * len(shape))
    return pl.pallas_call(
        _odd_kernel,
        name="odd_mixer_sublayer",
        grid=(b, s // ts),
        in_specs=[pl.BlockSpec((1, ts, d), lambda i, j: (i, j, 0)),
                  pl.BlockSpec((1, ODD_HALO, d), lambda i, j: (i, jnp.maximum(j * per - 1, 0), 0)),
                  full((d, d)), full((d, d)), full((d, d)), full(conv_w.shape), full((d, d)), full((1, d)), full((1, d))],
        out_specs=pl.BlockSpec((1, ts, d), lambda i, j: (i, j, 0)),
        out_shape=jax.ShapeDtypeStruct((b, s, d), jnp.float32),
        compiler_params=pltpu.CompilerParams(dimension_semantics=("arbitrary", "arbitrary"), vmem_limit_bytes=VMEM_LIMIT),
    )(x, x, wb, wc, wh, conv_w, w_out.astype(bf16), ln_g.reshape(1, d), ln_b.reshape(1, d))


def _cmp_to_sel_matrix(n_cmp, n_sel):
    c0 = np.arange(n_cmp) * CMP_STRIDE
    s0 = np.arange(n_sel) * SEL_BLOCK
    ov = np.minimum(c0[:, None] + CMP_BLOCK, s0[None, :] + SEL_BLOCK) - np.maximum(c0[:, None], s0[None, :])
    return (np.clip(ov, 0, None) / CMP_BLOCK).astype(np.float32)


def _nsa_kernel(qT_ref, gT_ref, kc_ref, vcT_ref, mselT_ref, ks_ref, vsT_ref, kw_ref, vwT_ref, o_ref,
                score_ref, sel_ref, *, n_sel):
    c = pl.program_id(1)
    q0 = c * Q_BLOCK
    f32 = jnp.float32
    bf16 = jnp.bfloat16
    n_cmp_pad = kc_ref.shape[1]

    lane_q = lax.broadcasted_iota(jnp.int32, (1, QL), 1) % Q_BLOCK
    t_row = q0 + lane_q
    t_row_q = q0 + lax.broadcasted_iota(jnp.int32, (1, Q_BLOCK), 1)
    cur_q = t_row_q // SEL_BLOCK

    for g in range(NSA_KV_HEADS):
        pieces = []
        for hg in range(HEADS_PER_KV):
            h = g * HEADS_PER_KV + hg
            qh = qT_ref[0, h * HEAD_DIM:(h + 1) * HEAD_DIM, :] * jnp.asarray(HEAD_DIM ** -0.5, bf16)
            z = jnp.zeros_like(qh)
            pieces.append(jnp.concatenate([qh, z] if g == 0 else [z, qh], axis=0))
        qTp = jnp.concatenate(pieces, axis=1)
        rows = slice(g * HEAD_DIM, (g + 1) * HEAD_DIM)

        s_c = jnp.dot(kc_ref[0], qTp, preferred_element_type=f32)
        n_iota = lax.broadcasted_iota(jnp.int32, (n_cmp_pad, QL), 0)
        mask_c = (n_iota * CMP_STRIDE + (CMP_BLOCK - 1)) <= t_row
        s_c = jnp.where(mask_c, s_c, NEG_INF)
        m_c = jnp.max(s_c, axis=0, keepdims=True)
        p_c = jnp.where(mask_c, jnp.exp(s_c - m_c), 0.0)
        l_c = jnp.sum(p_c, axis=0, keepdims=True)
        p_c = p_c * jnp.where(l_c > 0.0, 1.0 / l_c, 0.0)
        p_cb = p_c.astype(bf16)
        o_c = jnp.dot(vcT_ref[0], p_cb, preferred_element_type=f32)[rows]
        imp4 = jnp.dot(mselT_ref[...], p_cb, preferred_element_type=f32)
        imp = imp4[:, 0:Q_BLOCK]
        for hg in range(1, HEADS_PER_KV):
            imp = imp + imp4[:, hg * Q_BLOCK:(hg + 1) * Q_BLOCK]

        m_iota = lax.broadcasted_iota(jnp.int32, (n_sel, Q_BLOCK), 0)
        forced = (m_iota == 0) | (m_iota == cur_q) | (m_iota == cur_q - 1)
        score = jnp.where(forced, FORCE_SCORE, imp)
        valid = m_iota <= cur_q
        score = jnp.where(valid, score, NEG_INF)
        score_ref[...] = score

        def rank_body(mp, rank):
            row = score_ref[pl.ds(mp, 1), :]
            beats = (row > score) | ((row == score) & (mp < m_iota))
            return rank + beats.astype(jnp.int32)

        n_comp = jnp.minimum(q0 // SEL_BLOCK + 2, n_sel)
        rank = lax.fori_loop(0, n_comp, rank_body, jnp.zeros((n_sel, Q_BLOCK), jnp.int32))
        top_n = min(SEL_TOP_N, n_sel)
        sel_ref[...] = jnp.where((rank < top_n) & valid, 0.0, NEG_INF)

        def sel_tile(j, carry, causal):
            m_i, l_i, acc = carry
            s = jnp.dot(ks_ref[0, j], qTp, preferred_element_type=f32)
            selrows = sel_ref[pl.ds(pl.multiple_of(j * BLOCKS_PER_TILE, BLOCKS_PER_TILE), BLOCKS_PER_TILE), :]
            bias = jnp.concatenate(
                [jnp.broadcast_to(selrows[r:r + 1, :], (SEL_BLOCK, Q_BLOCK)) for r in range(BLOCKS_PER_TILE)], axis=0)
            if causal:
                key = j * SEL_TILE + lax.broadcasted_iota(jnp.int32, (SEL_TILE, Q_BLOCK), 0)
                bias = jnp.where(key <= t_row_q, bias, NEG_INF)
            s = s + jnp.concatenate([bias] * HEADS_PER_KV, axis=1)
            m_new = jnp.maximum(m_i, jnp.max(s, axis=0, keepdims=True))
            alpha = jnp.exp(m_i - m_new)
            p = jnp.exp(s - m_new)
            l_new = alpha * l_i + jnp.sum(p, axis=0, keepdims=True)
            pv = jnp.dot(vsT_ref[0, j], p.astype(bf16), preferred_element_type=f32)
            return m_new, l_new, alpha * acc + pv

        n_full = q0 // SEL_TILE
        carry = lax.fori_loop(
            0, n_full, functools.partial(sel_tile, causal=False),
            (jnp.full((1, QL), NEG_INF, f32), jnp.zeros((1, QL), f32), jnp.zeros((KV_COLS, QL), f32)))
        m_s, l_s, acc_s = sel_tile(n_full, carry, causal=True)
        o_s = acc_s[rows] * (1.0 / l_s)

        start = pl.multiple_of(jnp.maximum(q0 - WINDOW, 0), Q_BLOCK)
        s_w = jnp.dot(kw_ref[0, pl.ds(start, WIN_SPAN), :], qTp, preferred_element_type=f32)
        key_w = start + lax.broadcasted_iota(jnp.int32, (WIN_SPAN, Q_BLOCK), 0)
        bias_w = jnp.where(key_w <= t_row_q, 0.0, NEG_INF)
        bias_w = jnp.where(key_w > t_row_q - WINDOW, bias_w, NEG_INF)
        s_w = s_w + jnp.concatenate([bias_w] * HEADS_PER_KV, axis=1)
        m_w = jnp.max(s_w, axis=0, keepdims=True)
        p_w = jnp.exp(s_w - m_w)
        l_w = jnp.sum(p_w, axis=0, keepdims=True)
        p_wb = p_w.astype(bf16)
        j0 = start // Q_BLOCK
        acc_w = jnp.zeros((KV_COLS, QL), f32)
        for i in range(WIN_SPAN // Q_BLOCK):
            acc_w = acc_w + jnp.dot(vwT_ref[0, j0 + i], p_wb[i * Q_BLOCK:(i + 1) * Q_BLOCK, :], preferred_element_type=f32)
        o_w = acc_w[rows] * (1.0 / l_w)

        for pair in range(HEADS_PER_KV // 2):
            halves = []
            for hg in (2 * pair, 2 * pair + 1):
                h = g * HEADS_PER_KV + hg
                lanes = slice(hg * Q_BLOCK, (hg + 1) * Q_BLOCK)
                gate = jax.nn.sigmoid(gT_ref[0, 3 * h:3 * h + 3, :])
                halves.append(gate[0:1] * o_c[:, lanes] + gate[1:2] * o_s[:, lanes] + gate[2:3] * o_w[:, lanes])
            both = jnp.concatenate(halves, axis=0)
            col0 = (g * HEADS_PER_KV + 2 * pair) * HEAD_DIM
            o_ref[0, :, col0:col0 + 2 * HEAD_DIM] = both.T.astype(o_ref.dtype)


def _nsa_attention(qT, gT, kc, vcT, ks, vsT, kw, vwT):
    b, hd, s = qT.shape
    n_sel = s // SEL_BLOCK
    nc = kc.shape[1]
    n_cmp = (s - CMP_BLOCK) // CMP_STRIDE + 1
    mselT = jnp.asarray(np.pad(_cmp_to_sel_matrix(n_cmp, n_sel).T, ((0, 0), (0, nc - n_cmp))), jnp.bfloat16)
    ks4 = ks.reshape(b, s // SEL_TILE, SEL_TILE, KV_COLS)
    return pl.pallas_call(
        functools.partial(_nsa_kernel, n_sel=n_sel),
        name="nsa_attention",
        grid=(b, s // Q_BLOCK),
        in_specs=[
            pl.BlockSpec((1, hd, Q_BLOCK), lambda i, c: (i, 0, c)),
            pl.BlockSpec((1, 3 * NSA_HEADS, Q_BLOCK), lambda i, c: (i, 0, c)),
            pl.BlockSpec((1, nc, KV_COLS), lambda i, c: (i, 0, 0)),
            pl.BlockSpec((1, KV_COLS, nc), lambda i, c: (i, 0, 0)),
            pl.BlockSpec((n_sel, nc), lambda i, c: (0, 0)),
            pl.BlockSpec((1, s // SEL_TILE, SEL_TILE, KV_COLS), lambda i, c: (i, 0, 0, 0)),
            pl.BlockSpec((1, s // SEL_TILE, KV_COLS, SEL_TILE), lambda i, c: (i, 0, 0, 0)),
            pl.BlockSpec((1, s, KV_COLS), lambda i, c: (i, 0, 0)),
            pl.BlockSpec((1, s // Q_BLOCK, KV_COLS, Q_BLOCK), lambda i, c: (i, 0, 0, 0)),
        ],
        out_specs=pl.BlockSpec((1, Q_BLOCK, hd), lambda i, c: (i, c, 0)),
        out_shape=jax.ShapeDtypeStruct((b, s, hd), jnp.bfloat16),
        scratch_shapes=[pltpu.VMEM((n_sel, Q_BLOCK), jnp.float32), pltpu.VMEM((n_sel, Q_BLOCK), jnp.float32)],
        compiler_params=pltpu.CompilerParams(dimension_semantics=("arbitrary", "arbitrary")),
    )(qT, gT, kc, vcT, mselT, ks4, vsT, kw, vwT)


def _router_kernel(x_ref, w_ref, b_ref, tri_ref, route_ref, cnt_ref, run_ref):
    i = pl.program_id(0)
    f32 = jnp.float32

    @pl.when(i == 0)
    def _():
        run_ref[...] = jnp.zeros_like(run_ref)

    tm = x_ref.shape[0]
    logits = jnp.dot(x_ref[...].astype(jnp.bfloat16), w_ref[...], preferred_element_type=f32) + b_ref[...]
    lane = lax.broadcasted_iota(jnp.int32, (tm, LANES), 1)
    is_g = lane < N_GROUPS
    gl = jnp.where(is_g, logits, NEG_INF)
    g_max = jnp.max(gl, axis=1, keepdims=True)
    g_star = jnp.min(jnp.where(gl == g_max, lane, LANES), axis=1, keepdims=True)
    p_group = 1.0 / jnp.sum(jnp.where(is_g, jnp.exp(gl - g_max), 0.0), axis=1, keepdims=True)
    lo = N_GROUPS + g_star * EXPERTS_PER_GROUP
    in_grp = (lane >= lo) & (lane < lo + EXPERTS_PER_GROUP)
    el = jnp.where(in_grp, logits, NEG_INF)
    v1 = jnp.max(el, axis=1, keepdims=True)
    i1 = jnp.min(jnp.where(el == v1, lane, LANES), axis=1, keepdims=True)
    el2 = jnp.where(lane == i1, NEG_INF, el)
    v2 = jnp.max(el2, axis=1, keepdims=True)
    i2 = jnp.min(jnp.where(el2 == v2, lane, LANES), axis=1, keepdims=True)
    e21 = jnp.exp(v2 - v1)
    gate1 = p_group * (1.0 / (1.0 + e21))
    gate2 = p_group * (e21 / (1.0 + e21))
    oh1 = (lane == i1).astype(f32)
    oh2 = (lane == i2).astype(f32)
    both = (oh1 + oh2).astype(jnp.bfloat16)
    before = jnp.dot(tri_ref[...], both, preferred_element_type=f32) + run_ref[...]
    rank1 = jnp.sum(oh1 * before, axis=1, keepdims=True)
    rank2 = jnp.sum(oh2 * before, axis=1, keepdims=True)
    run_ref[...] = run_ref[...] + jnp.sum(oh1 + oh2, axis=0, keepdims=True)
    cnt_ref[...] = run_ref[...]
    zero = jnp.zeros_like(gate1)
    route_ref[...] = jnp.concatenate(
        [gate1, gate2, (i1 - N_GROUPS).astype(f32), (i2 - N_GROUPS).astype(f32), rank1, rank2, zero, zero], axis=1)


def _moe_route(xt, wg, bg, we, be):
    t, d = xt.shape
    pad = LANES - N_GROUPS - N_EXPERTS
    w = jnp.concatenate([wg, we.reshape(d, N_EXPERTS), jnp.zeros((d, pad), wg.dtype)], axis=1)
    bias = jnp.concatenate([bg, be.reshape(N_EXPERTS), jnp.zeros((pad,), bg.dtype)])[None, :]
    tri = jnp.asarray(np.tril(np.ones((ROUTE_TILE, ROUTE_TILE), np.float32), -1), jnp.bfloat16)
    return pl.pallas_call(
        _router_kernel,
        name="moe_router",
        grid=(t // ROUTE_TILE,),
        in_specs=[pl.BlockSpec((ROUTE_TILE, d), lambda i: (i, 0)),
                  pl.BlockSpec((d, LANES), lambda i: (0, 0)),
                  pl.BlockSpec((1, LANES), lambda i: (0, 0)),
                  pl.BlockSpec((ROUTE_TILE, ROUTE_TILE), lambda i: (0, 0))],
        out_specs=[pl.BlockSpec((ROUTE_TILE, ROUTE_COLS), lambda i: (i, 0)),
                   pl.BlockSpec((1, LANES), lambda i: (0, 0))],
        out_shape=[jax.ShapeDtypeStruct((t, ROUTE_COLS), jnp.float32), jax.ShapeDtypeStruct((1, LANES), jnp.float32)],
        scratch_shapes=[pltpu.VMEM((1, LANES), jnp.float32)],
        compiler_params=pltpu.CompilerParams(dimension_semantics=("arbitrary",)),
    )(xt, w.astype(jnp.bfloat16), bias, tri)


def _dispatch_kernel(dest_ref, x_ref, zero_hbm, xd_hbm, sem):
    del zero_hbm
    tm = x_ref.shape[0]

    def body(r, c):
        for k in range(EXPERT_TOP_K):
            pltpu.make_async_copy(
                x_ref.at[pl.ds(r, 1)], xd_hbm.at[pl.ds(dest_ref[0, 0, k * tm + r], 1)], sem).start()
        return c

    lax.fori_loop(0, tm, body, 0, unroll=8)
    for k in range(EXPERT_TOP_K):
        pltpu.make_async_copy(x_ref, xd_hbm.at[pl.ds(0, tm)], sem).wait()


def _moe_dispatch(xt, dest2, n_rows):
    t, d = xt.shape
    return pl.pallas_call(
        _dispatch_kernel,
        name="moe_dispatch",
        grid=(t // ROUTE_TILE,),
        in_specs=[pl.BlockSpec((1, 1, EXPERT_TOP_K * ROUTE_TILE), lambda i: (i, 0, 0), memory_space=pltpu.SMEM),
                  pl.BlockSpec((ROUTE_TILE, d), lambda i: (i, 0)),
                  pl.BlockSpec(memory_space=pl.ANY)],
        out_specs=pl.BlockSpec(memory_space=pl.ANY),
        out_shape=jax.ShapeDtypeStruct((n_rows, d), xt.dtype),
        scratch_shapes=[pltpu.SemaphoreType.DMA(())],
        input_output_aliases={2: 0},
        compiler_params=pltpu.CompilerParams(dimension_semantics=("arbitrary",), has_side_effects=True),
    )(dest2, xt, jnp.zeros((n_rows, d), xt.dtype))


def _expert_kernel(te_ref, nu_ref, xd_ref, w1_ref, w3_ref, w2_ref, y_ref):
    del te_ref
    i = pl.program_id(0)

    @pl.when(i < nu_ref[0])
    def _():
        xb = xd_ref[...].astype(jnp.bfloat16)
        h1 = jnp.dot(xb, w1_ref[0], preferred_element_type=jnp.float32)
        h3 = jnp.dot(xb, w3_ref[0], preferred_element_type=jnp.float32)
        a = (h1 * jax.nn.sigmoid(h1) * h3).astype(jnp.bfloat16)
        y_ref[...] = jnp.dot(a, w2_ref[0], preferred_element_type=jnp.float32)

    @pl.when(i >= nu_ref[0])
    def _():
        y_ref[...] = jnp.zeros_like(y_ref)


def _moe_experts(x_disp, tile_expert, n_used, w1, w3, w2):
    n_rows, d = x_disp.shape
    n_tiles = n_rows // EXPERT_TILE
    hid = w1.shape[2]

    def row_map(i, te, nu):
        return (i, 0)

    def w_map(i, te, nu):
        return (te[i], 0, 0)

    grid_spec = pltpu.PrefetchScalarGridSpec(
        num_scalar_prefetch=2,
        grid=(n_tiles,),
        in_specs=[pl.BlockSpec((EXPERT_TILE, d), row_map),
                  pl.BlockSpec((1, d, hid), w_map),
                  pl.BlockSpec((1, d, hid), w_map),
                  pl.BlockSpec((1, hid, d), w_map)],
        out_specs=pl.BlockSpec((EXPERT_TILE, d), row_map),
    )
    return pl.pallas_call(
        _expert_kernel,
        name="moe_experts",
        grid_spec=grid_spec,
        out_shape=jax.ShapeDtypeStruct((n_rows, d), jnp.float32),
        compiler_params=pltpu.CompilerParams(dimension_semantics=("arbitrary",)),
    )(tile_expert, n_used, x_disp, w1, w3, w2)


def _combine_kernel(dest_ref, dest_next_ref, x_ref, route_ref, g_ref, b_ref, yd_hbm, o_ref, ybuf, sem):
    i = pl.program_id(0)
    n = pl.num_programs(0)
    tm = x_ref.shape[0]
    rows = EXPERT_TOP_K * tm

    def start_tile(dref, slot):
        def body(r, c):
            pltpu.make_async_copy(
                yd_hbm.at[pl.ds(dref[0, 0, r], 1)], ybuf.at[slot, pl.ds(r, 1)], sem.at[slot]).start()
            return c
        lax.fori_loop(0, rows, body, 0, unroll=8)

    slot = i % 2

    @pl.when(i == 0)
    def _():
        start_tile(dest_ref, 0)

    @pl.when(i + 1 < n)
    def _():
        start_tile(dest_next_ref, 1 - slot)

    pltpu.make_async_copy(yd_hbm.at[pl.ds(0, rows)], ybuf.at[slot], sem.at[slot]).wait()
    y1 = ybuf[slot, 0:tm, :]
    y2 = ybuf[slot, tm:rows, :]
    route = route_ref[...]
    y = DN_ALPHA * x_ref[...] + (y1 * route[:, 0:1] + y2 * route[:, 1:2])
    o_ref[...] = _ln_rows(y, g_ref[...], b_ref[...])


def _moe_combine_ln(xt, y_disp, dest2, route, ln_g, ln_b):
    t, d = xt.shape
    n = t // ROUTE_TILE
    rows = EXPERT_TOP_K * ROUTE_TILE
    return pl.pallas_call(
        _combine_kernel,
        name="moe_combine_ln",
        grid=(n,),
        in_specs=[pl.BlockSpec((1, 1, rows), lambda i: (i, 0, 0), memory_space=pltpu.SMEM),
                  pl.BlockSpec((1, 1, rows), lambda i: (jnp.minimum(i + 1, n - 1), 0, 0), memory_space=pltpu.SMEM),
                  pl.BlockSpec((ROUTE_TILE, d), lambda i: (i, 0)),
                  pl.BlockSpec((ROUTE_TILE, ROUTE_COLS), lambda i: (i, 0)),
                  pl.BlockSpec((1, d), lambda i: (0, 0)),
                  pl.BlockSpec((1, d), lambda i: (0, 0)),
                  pl.BlockSpec(memory_space=pl.ANY)],
        out_specs=pl.BlockSpec((ROUTE_TILE, d), lambda i: (i, 0)),
        out_shape=jax.ShapeDtypeStruct((t, d), jnp.float32),
        scratch_shapes=[pltpu.VMEM((2, rows, d), jnp.float32), pltpu.SemaphoreType.DMA((2,))],
        compiler_params=pltpu.CompilerParams(dimension_semantics=("arbitrary",), vmem_limit_bytes=VMEM_LIMIT),
    )(dest2, dest2, xt, route, ln_g.reshape(1, d), ln_b.reshape(1, d), y_disp)


def _moe_sublayer(x, wg, bg, we, be, w1, w3, w2, ln_g, ln_b):
    b, s, d = x.shape
    t = b * s
    xt = x.reshape(t, d)
    route, cnt = _moe_route(xt, wg, bg, we, be)
    counts = cnt[0, N_GROUPS:N_GROUPS + N_EXPERTS].astype(jnp.int32)
    n_tiles = (t * EXPERT_TOP_K) // EXPERT_TILE + N_EXPERTS
    tiles_per = (counts + EXPERT_TILE - 1) // EXPERT_TILE
    tile_end = jnp.cumsum(tiles_per)
    pad_start = (tile_end - tiles_per) * EXPERT_TILE
    n_used = tile_end[-1:].astype(jnp.int32)
    tile_ids = jnp.arange(n_tiles)
    tile_expert = jnp.minimum(jnp.searchsorted(tile_end, tile_ids, side='right'), N_EXPERTS - 1).astype(jnp.int32)
    tile_expert = jnp.where(tile_ids < n_used[0], tile_expert, tile_expert[jnp.maximum(n_used[0] - 1, 0)])
    experts = route[:, 2:4].astype(jnp.int32)
    dest = pad_start[experts] + route[:, 4:6].astype(jnp.int32)
    dest2 = jnp.swapaxes(dest.reshape(t // ROUTE_TILE, ROUTE_TILE, EXPERT_TOP_K), 1, 2)
    dest2 = dest2.reshape(t // ROUTE_TILE, 1, EXPERT_TOP_K * ROUTE_TILE)
    x_disp = _moe_dispatch(xt, dest2, n_tiles * EXPERT_TILE)
    bf16 = jnp.bfloat16
    y_disp = _moe_experts(x_disp, tile_expert, n_used, w1.astype(bf16), w3.astype(bf16), w2.astype(bf16))
    return _moe_combine_ln(xt, y_disp, dest2, route, ln_g, ln_b).reshape(b, s, d)


def kernel(x, mem, mem_wk, mem_wv, ev_w_in, ev_conv_w, ev_conv_b, ev_cnorm_g, ev_cnorm_b, ev_cmp_pe_k, ev_cmp_w1_k, ev_cmp_w2_k, ev_cmp_pe_v, ev_cmp_w1_v, ev_cmp_w2_v, ev_w_out, od_w_in, od_conv_w, od_w_out, ln_mix_g, ln_mix_b, xa_wq, xa_wo, ln_xa_g, ln_xa_b, moe_wg, moe_bg, moe_we, moe_be, moe_w1, moe_w3, moe_w2, ln_ffn_g, ln_ffn_b):
    b, s, d = x.shape
    mem_k, mem_v = _mem_kv(mem, mem_wk, mem_wv)
    for layer in range(DEPTH):
        i = layer // 2
        if layer % 2 == 0:
            a, kv_in, k_sel, k_win, q_t, v_sel_t, v_win_t, gate_t = _even_in_proj(x, ev_w_in[i])
            a = _conformer_conv(a, ev_conv_w[i], ev_conv_b[i], ev_cnorm_g[i], ev_cnorm_b[i])
            k_cmp, v_cmp_t = _compress_kv(kv_in, ev_cmp_pe_k[i], ev_cmp_w1_k[i], ev_cmp_w2_k[i], ev_cmp_pe_v[i], ev_cmp_w1_v[i], ev_cmp_w2_v[i])
            o = _nsa_attention(q_t, gate_t, k_cmp, v_cmp_t, k_sel, v_sel_t, k_win, v_win_t)
            x = _proj_residual_ln(a.reshape(b * s, -1), o.reshape(b * s, -1), x.reshape(b * s, d), ev_w_out[i], ln_mix_g[layer], ln_mix_b[layer]).reshape(b, s, d)
        else:
            x = _odd_mixer_sublayer(x, od_w_in[i], od_conv_w[i], od_w_out[i], ln_mix_g[layer], ln_mix_b[layer])
        x = _xattn_sublayer(x, mem_k, mem_v, xa_wq[layer], xa_wo[layer], ln_xa_g[layer], ln_xa_b[layer])
        x = _moe_sublayer(x, moe_wg[layer], moe_bg[layer], moe_we[layer], moe_be[layer], moe_w1[layer], moe_w3[layer], moe_w2[layer], ln_ffn_g[layer], ln_ffn_b[layer])
    return x
```

```python
import functools

import numpy as np
import jax
import jax.numpy as jnp
from jax import lax
from jax.experimental import pallas as pl
from jax.experimental.pallas import tpu as pltpu

D_MODEL = 1024
DEPTH = 2
CONV_CH = D_MODEL // 2
CONV_WIDTH = 31
NSA_HEADS = 8
NSA_KV_HEADS = 2
HEAD_DIM = (D_MODEL // 2) // NSA_HEADS
CMP_BLOCK = 32
CMP_STRIDE = 16
SEL_BLOCK = 64
SEL_TOP_N = 16
WINDOW = 512
Q_BLOCK = 128
FORCE_SCORE = 1e4
SHORT_CONV_WIDTH = 3
XA_HEADS = 4
XA_HEAD_DIM = D_MODEL // XA_HEADS
N_GROUPS = 4
EXPERTS_PER_GROUP = 8
N_EXPERTS = N_GROUPS * EXPERTS_PER_GROUP
EXPERT_TOP_K = 2
DN_ALPHA = (2 * DEPTH) ** 0.25
LN_EPS = 1e-5
NEG_INF = -1e30
KV_COLS = NSA_KV_HEADS * HEAD_DIM
QCOLS = NSA_HEADS * HEAD_DIM
GATE_ROWS = 3 * NSA_HEADS

LANES = 128
SUBLANES = 8
HEADS_PER_KV = NSA_HEADS // NSA_KV_HEADS
QL = Q_BLOCK * HEADS_PER_KV
SEL_TILE = 512
WIN_SPAN = WINDOW + Q_BLOCK
BLOCKS_PER_TILE = SEL_TILE // SEL_BLOCK
SEQ_TILE = 512
HALO = 32
ODD_HALO = 8
ROUTE_TILE = 512
EXPERT_TILE = 256
ROUTE_ROWS = 8
VMEM_LIMIT = 56 * 1024 * 1024

_NT = (((1,), (1,)), ((), ()))


def _ln_rows(y, g, b):
    mu = jnp.mean(y, axis=-1, keepdims=True)
    yc = y - mu
    var = jnp.mean(yc * yc, axis=-1, keepdims=True)
    return yc * lax.rsqrt(var + LN_EPS) * g + b


def _even_in_kernel(x_ref, wa_ref, wkv_ref, wk2_ref, wqT_ref, wvT_ref, wgT_ref,
                    a_ref, kv_ref, ks_ref, kw_ref, qT_ref, vsT_ref, vwT_ref, gT_ref):
    f32 = jnp.float32
    bf16 = jnp.bfloat16
    xb = x_ref[0].astype(bf16)
    av = jnp.dot(xb, wa_ref[...], preferred_element_type=f32)
    a_ref[0] = av[:, :CONV_CH] * jax.nn.sigmoid(av[:, CONV_CH:])
    kv_ref[0] = jnp.dot(xb, wkv_ref[...], preferred_element_type=f32)
    k2 = jnp.dot(xb, wk2_ref[...], preferred_element_type=f32)
    ks_ref[0] = k2[:, :KV_COLS].astype(bf16)
    kw_ref[0] = k2[:, KV_COLS:].astype(bf16)
    qT_ref[0] = lax.dot_general(wqT_ref[...], xb, _NT, preferred_element_type=f32).astype(bf16)
    vT = lax.dot_general(wvT_ref[...], xb, _NT, preferred_element_type=f32).astype(bf16)
    vsT_ref[0, 0] = vT[:KV_COLS]
    for j in range(SEQ_TILE // Q_BLOCK):
        vwT_ref[0, j] = vT[KV_COLS:, j * Q_BLOCK:(j + 1) * Q_BLOCK]
    gT_ref[0] = lax.dot_general(wgT_ref[...], xb, _NT, preferred_element_type=f32)


def _even_in_proj(x, w_in):
    b, s, d = x.shape
    bf16 = jnp.bfloat16
    c = np.cumsum((0, CONV_CH, CONV_CH, QCOLS, KV_COLS, KV_COLS, KV_COLS, KV_COLS, KV_COLS, KV_COLS, GATE_ROWS))
    col = lambda i, j: w_in[:, c[i]:c[j]]
    wa = col(0, 2).astype(bf16)
    wq_t = col(2, 3).T.astype(bf16)
    wkv = col(3, 5).astype(bf16)
    wk2 = jnp.concatenate([col(5, 6), col(7, 8)], axis=1).astype(bf16)
    wv_t = jnp.concatenate([col(6, 7), col(8, 9)], axis=1).T.astype(bf16)
    wg_t = col(9, 10).T.astype(bf16)
    ts = SEQ_TILE
    full = lambda shape: pl.BlockSpec(shape, lambda i, j: (0,) * len(shape))
    return pl.pallas_call(
        _even_in_kernel,
        name="even_in_proj",
        grid=(b, s // ts),
        in_specs=[pl.BlockSpec((1, ts, d), lambda i, j: (i, j, 0)),
                  full(wa.shape), full(wkv.shape), full(wk2.shape), full(wq_t.shape), full(wv_t.shape), full(wg_t.shape)],
        out_specs=[pl.BlockSpec((1, ts, CONV_CH), lambda i, j: (i, j, 0)),
                   pl.BlockSpec((1, ts, 2 * KV_COLS), lambda i, j: (i, j, 0)),
                   pl.BlockSpec((1, ts, KV_COLS), lambda i, j: (i, j, 0)),
                   pl.BlockSpec((1, ts, KV_COLS), lambda i, j: (i, j, 0)),
                   pl.BlockSpec((1, QCOLS, ts), lambda i, j: (i, 0, j)),
                   pl.BlockSpec((1, 1, KV_COLS, ts), lambda i, j: (i, j, 0, 0)),
                   pl.BlockSpec((1, ts // Q_BLOCK, KV_COLS, Q_BLOCK), lambda i, j: (i, j, 0, 0)),
                   pl.BlockSpec((1, GATE_ROWS, ts), lambda i, j: (i, 0, j))],
        out_shape=[jax.ShapeDtypeStruct((b, s, CONV_CH), jnp.float32),
                   jax.ShapeDtypeStruct((b, s, 2 * KV_COLS), jnp.float32),
                   jax.ShapeDtypeStruct((b, s, KV_COLS), bf16),
                   jax.ShapeDtypeStruct((b, s, KV_COLS), bf16),
                   jax.ShapeDtypeStruct((b, QCOLS, s), bf16),
                   jax.ShapeDtypeStruct((b, s // SEL_TILE, KV_COLS, SEL_TILE), bf16),
                   jax.ShapeDtypeStruct((b, s // Q_BLOCK, KV_COLS, Q_BLOCK), bf16),
                   jax.ShapeDtypeStruct((b, GATE_ROWS, s), jnp.float32)],
        compiler_params=pltpu.CompilerParams(dimension_semantics=("arbitrary", "arbitrary"), vmem_limit_bytes=VMEM_LIMIT),
    )(x, wa, wkv, wk2, wq_t, wv_t, wg_t)


def _conv_kernel(cur_ref, halo_ref, w_ref, cb_ref, g_ref, b_ref, o_ref, ext_ref, win_ref):
    j = pl.program_id(1)
    ts = cur_ref.shape[1]
    halo = halo_ref[0]
    ext_ref[0:HALO, :] = jnp.where(j > 0, halo, jnp.zeros_like(halo))
    ext_ref[HALO:HALO + ts, :] = cur_ref[0]
    first = HALO - (CONV_WIDTH - 1)
    acc = jnp.zeros((ts, CONV_CH), jnp.float32)
    for p in range(SUBLANES):
        n_a = len(range(p, CONV_WIDTH, SUBLANES))
        rows = ts + SUBLANES * (n_a - 1)
        win_ref[0:rows, :] = ext_ref[first + p:first + p + rows, :]
        for a in range(n_a):
            k = SUBLANES * a + p
            acc = acc + w_ref[k:k + 1, :] * win_ref[SUBLANES * a:SUBLANES * a + ts, :]
    y = _ln_rows(acc + cb_ref[...], g_ref[...], b_ref[...])
    o_ref[0] = (y * jax.nn.sigmoid(y)).astype(o_ref.dtype)


def _conformer_conv(a, conv_w, conv_b, cn_g, cn_b):
    b, s, c = a.shape
    ts = SEQ_TILE
    per = ts // HALO
    row = lambda v: v.reshape(1, c)
    return pl.pallas_call(
        _conv_kernel,
        name="conformer_conv",
        grid=(b, s // ts),
        in_specs=[pl.BlockSpec((1, ts, c), lambda i, j: (i, j, 0)),
                  pl.BlockSpec((1, HALO, c), lambda i, j: (i, jnp.maximum(j * per - 1, 0), 0)),
                  pl.BlockSpec((CONV_WIDTH, c), lambda i, j: (0, 0)),
                  pl.BlockSpec((1, c), lambda i, j: (0, 0)),
                  pl.BlockSpec((1, c), lambda i, j: (0, 0)),
                  pl.BlockSpec((1, c), lambda i, j: (0, 0))],
        out_specs=pl.BlockSpec((1, ts, c), lambda i, j: (i, j, 0)),
        out_shape=jax.ShapeDtypeStruct((b, s, c), jnp.bfloat16),
        scratch_shapes=[pltpu.VMEM((HALO + ts, c), jnp.float32), pltpu.VMEM((HALO + ts, c), jnp.float32)],
        compiler_params=pltpu.CompilerParams(dimension_semantics=("arbitrary", "arbitrary")),
    )(a, a, conv_w, row(conv_b), row(cn_g), row(cn_b))


def _compress_kernel(r_ref, pe_ref, w1_ref, w2_ref, o_ref):
    f32 = jnp.float32
    bf16 = jnp.bfloat16
    r = r_ref[0, 0]
    half = r.shape[1]
    ha = jnp.dot((r + pe_ref[0, 0:1, :]).astype(bf16), w1_ref[0, :half, :], preferred_element_type=f32)
    hb = jnp.dot((r + pe_ref[0, 1:2, :]).astype(bf16), w1_ref[0, half:, :], preferred_element_type=f32)
    hb_next = jnp.concatenate([hb[1:], jnp.zeros_like(hb[0:1])], axis=0)
    hid = jax.nn.gelu(ha + hb_next).astype(bf16)
    o_ref[0, 0] = jnp.dot(hid, w2_ref[0], preferred_element_type=f32).astype(bf16)


def _compress_kv(kv_in, pe_k, w1_k, w2_k, pe_v, w1_v, w2_v):
    b, s, _ = kv_in.shape
    bf16 = jnp.bfloat16
    nch = s // CMP_STRIDE
    g = NSA_KV_HEADS
    r = kv_in.reshape(b, nch, CMP_STRIDE, 2 * g, HEAD_DIM)
    r = jnp.transpose(r, (0, 3, 1, 2, 4)).reshape(b, 2 * g, nch, CMP_STRIDE * HEAD_DIM)
    pe = jnp.stack([pe_k, pe_v]).reshape(2, 2, CMP_STRIDE * HEAD_DIM)
    w1 = jnp.stack([w1_k, w1_v]).astype(bf16)
    w2 = jnp.stack([w2_k, w2_v]).astype(bf16)
    out = pl.pallas_call(
        _compress_kernel,
        name="compress_kv",
        grid=(b, 2 * g),
        in_specs=[pl.BlockSpec((1, 1, nch, r.shape[3]), lambda i, j: (i, j, 0, 0)),
                  pl.BlockSpec((1, 2, pe.shape[2]), lambda i, j: (j // g, 0, 0)),
                  pl.BlockSpec((1,) + w1.shape[1:], lambda i, j: (j // g, 0, 0)),
                  pl.BlockSpec((1,) + w2.shape[1:], lambda i, j: (j // g, 0, 0))],
        out_specs=pl.BlockSpec((1, 1, nch, HEAD_DIM), lambda i, j: (i, j, 0, 0)),
        out_shape=jax.ShapeDtypeStruct((b, 2 * g, nch, HEAD_DIM), bf16),
        compiler_params=pltpu.CompilerParams(dimension_semantics=("arbitrary", "arbitrary")),
    )(r, pe, w1, w2)
    kc = jnp.transpose(out[:, :g], (0, 2, 1, 3)).reshape(b, nch, KV_COLS)
    vcT = jnp.transpose(out[:, g:], (0, 1, 3, 2)).reshape(b, KV_COLS, nch)
    return kc, vcT


def _proj_ln_kernel(a_ref, o_ref, x_ref, w_ref, g_ref, b_ref, y_ref):
    lhs = jnp.concatenate([a_ref[...], o_ref[...]], axis=1)
    mix = jnp.dot(lhs, w_ref[...], preferred_element_type=jnp.float32)
    y_ref[...] = _ln_rows(DN_ALPHA * x_ref[...] + mix, g_ref[...], b_ref[...])


def _proj_residual_ln(a, o, x, w, ln_g, ln_b):
    t, d = x.shape
    tm = SEQ_TILE
    return pl.pallas_call(
        _proj_ln_kernel,
        name="proj_residual_ln",
        grid=(t // tm,),
        in_specs=[pl.BlockSpec((tm, a.shape[1]), lambda i: (i, 0)),
                  pl.BlockSpec((tm, o.shape[1]), lambda i: (i, 0)),
                  pl.BlockSpec((tm, d), lambda i: (i, 0)),
                  pl.BlockSpec(w.shape, lambda i: (0, 0)),
                  pl.BlockSpec((1, d), lambda i: (0, 0)),
                  pl.BlockSpec((1, d), lambda i: (0, 0))],
        out_specs=pl.BlockSpec((tm, d), lambda i: (i, 0)),
        out_shape=jax.ShapeDtypeStruct((t, d), jnp.float32),
        compiler_params=pltpu.CompilerParams(dimension_semantics=("arbitrary",), vmem_limit_bytes=VMEM_LIMIT),
    )(a, o, x, w.astype(jnp.bfloat16), ln_g.reshape(1, d), ln_b.reshape(1, d))


def _mem_kv_kernel(m_ref, wk_ref, wv_ref, k_ref, v_ref):
    mb = m_ref[...].astype(jnp.bfloat16)
    k_ref[...] = jnp.dot(mb, wk_ref[...], preferred_element_type=jnp.float32).astype(jnp.bfloat16)
    v_ref[...] = jnp.dot(mb, wv_ref[...], preferred_element_type=jnp.float32).astype(jnp.bfloat16)


def _mem_kv(mem, wk, wv):
    b, m, d = mem.shape
    bf16 = jnp.bfloat16
    k, v = pl.pallas_call(
        _mem_kv_kernel,
        name="mem_kv",
        grid=(b,),
        in_specs=[pl.BlockSpec((m, d), lambda i: (i, 0)), pl.BlockSpec((d, d), lambda i: (0, 0)), pl.BlockSpec((d, d), lambda i: (0, 0))],
        out_specs=[pl.BlockSpec((m, d), lambda i: (i, 0)), pl.BlockSpec((m, d), lambda i: (i, 0))],
        out_shape=[jax.ShapeDtypeStruct((b * m, d), bf16), jax.ShapeDtypeStruct((b * m, d), bf16)],
        compiler_params=pltpu.CompilerParams(dimension_semantics=("arbitrary",), vmem_limit_bytes=VMEM_LIMIT),
    )(mem.reshape(b * m, d), wk.astype(bf16), wv.astype(bf16))
    return k.reshape(b, m, d), v.reshape(b, m, d)


def _xattn_kernel(x_ref, k_ref, v_ref, wq_ref, wo_ref, g_ref, b_ref, y_ref):
    f32 = jnp.float32
    bf16 = jnp.bfloat16
    x = x_ref[0]
    q = (jnp.dot(x.astype(bf16), wq_ref[...], preferred_element_type=f32).astype(bf16)
         * jnp.asarray(XA_HEAD_DIM ** -0.5, bf16))
    heads = []
    for h in range(XA_HEADS):
        cols = slice(h * XA_HEAD_DIM, (h + 1) * XA_HEAD_DIM)
        s = lax.dot_general(q[:, cols], k_ref[0, :, cols], _NT, preferred_element_type=f32)
        m = jnp.max(s, axis=1, keepdims=True)
        p = jnp.exp(s - m)
        p = p / jnp.sum(p, axis=1, keepdims=True)
        heads.append(jnp.dot(p.astype(bf16), v_ref[0, :, cols], preferred_element_type=f32).astype(bf16))
    att = jnp.concatenate(heads, axis=1)
    out = jnp.dot(att, wo_ref[...], preferred_element_type=f32)
    y_ref[0] = _ln_rows(DN_ALPHA * x + out, g_ref[...], b_ref[...])


def _xattn_sublayer(x, mem_k, mem_v, wq, wo, ln_g, ln_b):
    b, s, d = x.shape
    m = mem_k.shape[1]
    ts = SEQ_TILE
    bf16 = jnp.bfloat16
    return pl.pallas_call(
        _xattn_kernel,
        name="xattn_sublayer",
        grid=(b, s // ts),
        in_specs=[pl.BlockSpec((1, ts, d), lambda i, j: (i, j, 0)),
                  pl.BlockSpec((1, m, d), lambda i, j: (i, 0, 0)),
                  pl.BlockSpec((1, m, d), lambda i, j: (i, 0, 0)),
                  pl.BlockSpec((d, d), lambda i, j: (0, 0)),
                  pl.BlockSpec((d, d), lambda i, j: (0, 0)),
                  pl.BlockSpec((1, d), lambda i, j: (0, 0)),
                  pl.BlockSpec((1, d), lambda i, j: (0, 0))],
        out_specs=pl.BlockSpec((1, ts, d), lambda i, j: (i, j, 0)),
        out_shape=jax.ShapeDtypeStruct((b, s, d), jnp.float32),
        compiler_params=pltpu.CompilerParams(dimension_semantics=("arbitrary", "arbitrary"), vmem_limit_bytes=VMEM_LIMIT),
    )(x, mem_k, mem_v, wq.astype(bf16), wo.astype(bf16), ln_g.reshape(1, d), ln_b.reshape(1, d))


def _odd_kernel(x_ref, halo_ref, wb_ref, wc_ref, wh_ref, cw_ref, wo_ref, g_ref, b_ref, y_ref):
    f32 = jnp.float32
    bf16 = jnp.bfloat16
    j = pl.program_id(1)
    ts = x_ref.shape[1]
    x = x_ref[0]
    xe = jnp.concatenate([halo_ref[0], x], axis=0).astype(bf16)
    u = (jnp.dot(xe, wc_ref[...], preferred_element_type=f32) * jnp.dot(xe, wh_ref[...], preferred_element_type=f32))
    row = lax.broadcasted_iota(jnp.int32, (ODD_HALO + ts, 1), 0)
    u = jnp.where((row >= ODD_HALO) | (j > 0), u, 0.0)
    conv = jnp.zeros((ts, u.shape[1]), f32)
    for k in range(SHORT_CONV_WIDTH):
        off = ODD_HALO - (SHORT_CONV_WIDTH - 1) + k
        conv = conv + cw_ref[k:k + 1, :] * u[off:off + ts, :]
    gate_b = jnp.dot(xe[ODD_HALO:], wb_ref[...], preferred_element_type=f32)
    mix = jnp.dot((gate_b * conv).astype(bf16), wo_ref[...], preferred_element_type=f32)
    y_ref[0] = _ln_rows(DN_ALPHA * x + mix, g_ref[...], b_ref[...])


def _odd_mixer_sublayer(x, w_in, conv_w, w_out, ln_g, ln_b):
    b, s, d = x.shape
    ts = SEQ_TILE
    per = ts // ODD_HALO
    bf16 = jnp.bfloat16
    wb, wc, wh = (w_in[:, i * d:(i + 1) * d].astype(bf16) for i in range(3))
    full = lambda shape: pl.BlockSpec(shape, lambda i, j: (0,) * len(shape))
    return pl.pallas_call(
        _odd_kernel,
        name="odd_mixer_sublayer",
        grid=(b, s // ts),
        in_specs=[pl.BlockSpec((1, ts, d), lambda i, j: (i, j, 0)),
                  pl.BlockSpec((1, ODD_HALO, d), lambda i, j: (i, jnp.maximum(j * per - 1, 0), 0)),
                  full((d, d)), full((d, d)), full((d, d)), full(conv_w.shape), full((d, d)), full((1, d)), full((1, d))],
        out_specs=pl.BlockSpec((1, ts, d), lambda i, j: (i, j, 0)),
        out_shape=jax.ShapeDtypeStruct((b, s, d), jnp.float32),
        compiler_params=pltpu.CompilerParams(dimension_semantics=("arbitrary", "arbitrary"), vmem_limit_bytes=VMEM_LIMIT),
    )(x, x, wb, wc, wh, conv_w, w_out.astype(bf16), ln_g.reshape(1, d), ln_b.reshape(1, d))


def _cmp_to_sel_matrix(n_cmp, n_sel):
    c0 = np.arange(n_cmp) * CMP_STRIDE
    s0 = np.arange(n_sel) * SEL_BLOCK
    ov = np.minimum(c0[:, None] + CMP_BLOCK, s0[None, :] + SEL_BLOCK) - np.maximum(c0[:, None], s0[None, :])
    return (np.clip(ov, 0, None) / CMP_BLOCK).astype(np.float32)


def _nsa_kernel(qT_ref, gT_ref, kc_ref, vcT_ref, mselT_ref, ks_ref, vsT_ref, kw_ref, vwT_ref, o_ref,
                score_ref, sel_ref, *, n_sel):
    c = pl.program_id(1)
    q0 = c * Q_BLOCK
    f32 = jnp.float32
    bf16 = jnp.bfloat16
    n_cmp_pad = kc_ref.shape[1]

    lane_q = lax.broadcasted_iota(jnp.int32, (1, QL), 1) % Q_BLOCK
    t_row = q0 + lane_q
    t_row_q = q0 + lax.broadcasted_iota(jnp.int32, (1, Q_BLOCK), 1)
    cur_q = t_row_q // SEL_BLOCK

    for g in range(NSA_KV_HEADS):
        pieces = []
        for hg in range(HEADS_PER_KV):
            h = g * HEADS_PER_KV + hg
            qh = qT_ref[0, h * HEAD_DIM:(h + 1) * HEAD_DIM, :] * jnp.asarray(HEAD_DIM ** -0.5, bf16)
            z = jnp.zeros_like(qh)
            pieces.append(jnp.concatenate([qh, z] if g == 0 else [z, qh], axis=0))
        qTp = jnp.concatenate(pieces, axis=1)
        rows = slice(g * HEAD_DIM, (g + 1) * HEAD_DIM)

        s_c = jnp.dot(kc_ref[0], qTp, preferred_element_type=f32)
        n_iota = lax.broadcasted_iota(jnp.int32, (n_cmp_pad, QL), 0)
        mask_c = (n_iota * CMP_STRIDE + (CMP_BLOCK - 1)) <= t_row
        s_c = jnp.where(mask_c, s_c, NEG_INF)
        m_c = jnp.max(s_c, axis=0, keepdims=True)
        p_c = jnp.where(mask_c, jnp.exp(s_c - m_c), 0.0)
        l_c = jnp.sum(p_c, axis=0, keepdims=True)
        p_c = p_c * jnp.where(l_c > 0.0, 1.0 / l_c, 0.0)
        p_cb = p_c.astype(bf16)
        o_c = jnp.dot(vcT_ref[0], p_cb, preferred_element_type=f32)[rows]
        imp4 = jnp.dot(mselT_ref[...], p_cb, preferred_element_type=f32)
        imp = imp4[:, 0:Q_BLOCK]
        for hg in range(1, HEADS_PER_KV):
            imp = imp + imp4[:, hg * Q_BLOCK:(hg + 1) * Q_BLOCK]

        m_iota = lax.broadcasted_iota(jnp.int32, (n_sel, Q_BLOCK), 0)
        forced = (m_iota == 0) | (m_iota == cur_q) | (m_iota == cur_q - 1)
        score = jnp.where(forced, FORCE_SCORE, imp)
        valid = m_iota <= cur_q
        score = jnp.where(valid, score, NEG_INF)
        score_ref[...] = score

        def rank_body(mp, rank):
            row = score_ref[pl.ds(mp, 1), :]
            beats = (row > score) | ((row == score) & (mp < m_iota))
            return rank + beats.astype(jnp.int32)

        n_comp = jnp.minimum(q0 // SEL_BLOCK + 2, n_sel)
        rank = lax.fori_loop(0, n_comp, rank_body, jnp.zeros((n_sel, Q_BLOCK), jnp.int32))
        top_n = min(SEL_TOP_N, n_sel)
        sel_ref[...] = jnp.where((rank < top_n) & valid, 0.0, NEG_INF)

        def sel_tile(j, carry, causal):
            m_i, l_i, acc = carry
            s = jnp.dot(ks_ref[0, j], qTp, preferred_element_type=f32)
            selrows = sel_ref[pl.ds(pl.multiple_of(j * BLOCKS_PER_TILE, BLOCKS_PER_TILE), BLOCKS_PER_TILE), :]
            bias = jnp.concatenate(
                [jnp.broadcast_to(selrows[r:r + 1, :], (SEL_BLOCK, Q_BLOCK)) for r in range(BLOCKS_PER_TILE)], axis=0)
            if causal:
                key = j * SEL_TILE + lax.broadcasted_iota(jnp.int32, (SEL_TILE, Q_BLOCK), 0)
                bias = jnp.where(key <= t_row_q, bias, NEG_INF)
            s = s + jnp.concatenate([bias] * HEADS_PER_KV, axis=1)
            m_new = jnp.maximum(m_i, jnp.max(s, axis=0, keepdims=True))
            alpha = jnp.exp(m_i - m_new)
            p = jnp.exp(s - m_new)
            l_new = alpha * l_i + jnp.sum(p, axis=0, keepdims=True)
            pv = jnp.dot(vsT_ref[0, j], p.astype(bf16), preferred_element_type=f32)
            return m_new, l_new, alpha * acc + pv

        n_full = q0 // SEL_TILE
        carry = lax.fori_loop(
            0, n_full, functools.partial(sel_tile, causal=False),
            (jnp.full((1, QL), NEG_INF, f32), jnp.zeros((1, QL), f32), jnp.zeros((KV_COLS, QL), f32)))
        m_s, l_s, acc_s = sel_tile(n_full, carry, causal=True)
        o_s = acc_s[rows] * (1.0 / l_s)

        start = pl.multiple_of(jnp.maximum(q0 - WINDOW, 0), Q_BLOCK)
        s_w = jnp.dot(kw_ref[0, pl.ds(start, WIN_SPAN), :], qTp, preferred_element_type=f32)
        key_w = start + lax.broadcasted_iota(jnp.int32, (WIN_SPAN, Q_BLOCK), 0)
        bias_w = jnp.where(key_w <= t_row_q, 0.0, NEG_INF)
        bias_w = jnp.where(key_w > t_row_q - WINDOW, bias_w, NEG_INF)
        s_w = s_w + jnp.concatenate([bias_w] * HEADS_PER_KV, axis=1)
        m_w = jnp.max(s_w, axis=0, keepdims=True)
        p_w = jnp.exp(s_w - m_w)
        l_w = jnp.sum(p_w, axis=0, keepdims=True)
        p_wb = p_w.astype(bf16)
        j0 = start // Q_BLOCK
        acc_w = jnp.zeros((KV_COLS, QL), f32)
        for i in range(WIN_SPAN // Q_BLOCK):
            acc_w = acc_w + jnp.dot(vwT_ref[0, j0 + i], p_wb[i * Q_BLOCK:(i + 1) * Q_BLOCK, :], preferred_element_type=f32)
        o_w = acc_w[rows] * (1.0 / l_w)

        for pair in range(HEADS_PER_KV // 2):
            halves = []
            for hg in (2 * pair, 2 * pair + 1):
                h = g * HEADS_PER_KV + hg
                lanes = slice(hg * Q_BLOCK, (hg + 1) * Q_BLOCK)
                gate = jax.nn.sigmoid(gT_ref[0, 3 * h:3 * h + 3, :])
                halves.append(gate[0:1] * o_c[:, lanes] + gate[1:2] * o_s[:, lanes] + gate[2:3] * o_w[:, lanes])
            both = jnp.concatenate(halves, axis=0)
            col0 = (g * HEADS_PER_KV + 2 * pair) * HEAD_DIM
            o_ref[0, :, col0:col0 + 2 * HEAD_DIM] = both.T.astype(o_ref.dtype)


def _nsa_attention(qT, gT, kc, vcT, ks, vsT, kw, vwT):
    b, hd, s = qT.shape
    n_sel = s // SEL_BLOCK
    nc = kc.shape[1]
    n_cmp = (s - CMP_BLOCK) // CMP_STRIDE + 1
    mselT = jnp.asarray(np.pad(_cmp_to_sel_matrix(n_cmp, n_sel).T, ((0, 0), (0, nc - n_cmp))), jnp.bfloat16)
    ks4 = ks.reshape(b, s // SEL_TILE, SEL_TILE, KV_COLS)
    return pl.pallas_call(
        functools.partial(_nsa_kernel, n_sel=n_sel),
        name="nsa_attention",
        grid=(b, s // Q_BLOCK),
        in_specs=[
            pl.BlockSpec((1, hd, Q_BLOCK), lambda i, c: (i, 0, c)),
            pl.BlockSpec((1, 3 * NSA_HEADS, Q_BLOCK), lambda i, c: (i, 0, c)),
            pl.BlockSpec((1, nc, KV_COLS), lambda i, c: (i, 0, 0)),
            pl.BlockSpec((1, KV_COLS, nc), lambda i, c: (i, 0, 0)),
            pl.BlockSpec((n_sel, nc), lambda i, c: (0, 0)),
            pl.BlockSpec((1, s // SEL_TILE, SEL_TILE, KV_COLS), lambda i, c: (i, 0, 0, 0)),
            pl.BlockSpec((1, s // SEL_TILE, KV_COLS, SEL_TILE), lambda i, c: (i, 0, 0, 0)),
            pl.BlockSpec((1, s, KV_COLS), lambda i, c: (i, 0, 0)),
            pl.BlockSpec((1, s // Q_BLOCK, KV_COLS, Q_BLOCK), lambda i, c: (i, 0, 0, 0)),
        ],
        out_specs=pl.BlockSpec((1, Q_BLOCK, hd), lambda i, c: (i, c, 0)),
        out_shape=jax.ShapeDtypeStruct((b, s, hd), jnp.bfloat16),
        scratch_shapes=[pltpu.VMEM((n_sel, Q_BLOCK), jnp.float32), pltpu.VMEM((n_sel, Q_BLOCK), jnp.float32)],
        compiler_params=pltpu.CompilerParams(dimension_semantics=("arbitrary", "arbitrary")),
    )(qT, gT, kc, vcT, mselT, ks4, vsT, kw, vwT)


def _router_kernel(x_ref, wT_ref, b_ref, tri_ref, route_ref, cnt_ref, run_ref):
    i = pl.program_id(0)
    f32 = jnp.float32

    @pl.when(i == 0)
    def _():
        run_ref[...] = jnp.zeros_like(run_ref)

    tm = x_ref.shape[0]
    logits = lax.dot_general(wT_ref[...], x_ref[...].astype(jnp.bfloat16), _NT, preferred_element_type=f32) + b_ref[...]
    sub = lax.broadcasted_iota(jnp.int32, (LANES, tm), 0)
    is_g = sub < N_GROUPS
    gl = jnp.where(is_g, logits, NEG_INF)
    g_max = jnp.max(gl, axis=0, keepdims=True)
    g_star = jnp.min(jnp.where(gl == g_max, sub, LANES), axis=0, keepdims=True)
    p_group = 1.0 / jnp.sum(jnp.where(is_g, jnp.exp(gl - g_max), 0.0), axis=0, keepdims=True)
    lo = N_GROUPS + g_star * EXPERTS_PER_GROUP
    in_grp = (sub >= lo) & (sub < lo + EXPERTS_PER_GROUP)
    el = jnp.where(in_grp, logits, NEG_INF)
    v1 = jnp.max(el, axis=0, keepdims=True)
    i1 = jnp.min(jnp.where(el == v1, sub, LANES), axis=0, keepdims=True)
    el2 = jnp.where(sub == i1, NEG_INF, el)
    v2 = jnp.max(el2, axis=0, keepdims=True)
    i2 = jnp.min(jnp.where(el2 == v2, sub, LANES), axis=0, keepdims=True)
    e21 = jnp.exp(v2 - v1)
    gate1 = p_group * (1.0 / (1.0 + e21))
    gate2 = p_group * (e21 / (1.0 + e21))
    oh1 = (sub == i1).astype(f32)
    oh2 = (sub == i2).astype(f32)
    both = oh1 + oh2
    before = jnp.dot(both.astype(jnp.bfloat16), tri_ref[...], preferred_element_type=f32) + run_ref[...]
    rank1 = jnp.sum(oh1 * before, axis=0, keepdims=True)
    rank2 = jnp.sum(oh2 * before, axis=0, keepdims=True)
    run_ref[...] = run_ref[...] + jnp.sum(both, axis=1, keepdims=True)
    cnt_ref[...] = run_ref[...]
    zero = jnp.zeros_like(gate1)
    route_ref[...] = jnp.concatenate(
        [gate1, gate2, (i1 - N_GROUPS).astype(f32), (i2 - N_GROUPS).astype(f32), rank1, rank2, zero, zero], axis=0)


def _moe_route(xt, wg, bg, we, be):
    t, d = xt.shape
    pad = LANES - N_GROUPS - N_EXPERTS
    w_t = jnp.concatenate([wg, we.reshape(d, N_EXPERTS), jnp.zeros((d, pad), wg.dtype)], axis=1).T
    bias = jnp.concatenate([bg, be.reshape(N_EXPERTS), jnp.zeros((pad,), bg.dtype)])[:, None]
    tri = jnp.asarray(np.triu(np.ones((ROUTE_TILE, ROUTE_TILE), np.float32), 1), jnp.bfloat16)
    return pl.pallas_call(
        _router_kernel,
        name="moe_router",
        grid=(t // ROUTE_TILE,),
        in_specs=[pl.BlockSpec((ROUTE_TILE, d), lambda i: (i, 0)),
                  pl.BlockSpec((LANES, d), lambda i: (0, 0)),
                  pl.BlockSpec((LANES, 1), lambda i: (0, 0)),
                  pl.BlockSpec((ROUTE_TILE, ROUTE_TILE), lambda i: (0, 0))],
        out_specs=[pl.BlockSpec((ROUTE_ROWS, ROUTE_TILE), lambda i: (0, i)),
                   pl.BlockSpec((LANES, 1), lambda i: (0, 0))],
        out_shape=[jax.ShapeDtypeStruct((ROUTE_ROWS, t), jnp.float32), jax.ShapeDtypeStruct((LANES, 1), jnp.float32)],
        scratch_shapes=[pltpu.VMEM((LANES, 1), jnp.float32)],
        compiler_params=pltpu.CompilerParams(dimension_semantics=("arbitrary",)),
    )(xt, w_t.astype(jnp.bfloat16), bias, tri)


def _dispatch_kernel(dest_ref, x_ref, zero_hbm, xd_hbm, sem):
    del zero_hbm
    tm = x_ref.shape[0]

    def body(r, c):
        for k in range(EXPERT_TOP_K):
            pltpu.make_async_copy(
                x_ref.at[pl.ds(r, 1)], xd_hbm.at[pl.ds(dest_ref[0, 0, k * tm + r], 1)], sem).start()
        return c

    lax.fori_loop(0, tm, body, 0, unroll=8)
    for k in range(EXPERT_TOP_K):
        pltpu.make_async_copy(x_ref, xd_hbm.at[pl.ds(0, tm)], sem).wait()


def _moe_dispatch(xt, dest2, n_rows):
    t, d = xt.shape
    return pl.pallas_call(
        _dispatch_kernel,
        name="moe_dispatch",
        grid=(t // ROUTE_TILE,),
        in_specs=[pl.BlockSpec((1, 1, EXPERT_TOP_K * ROUTE_TILE), lambda i: (i, 0, 0), memory_space=pltpu.SMEM),
                  pl.BlockSpec((ROUTE_TILE, d), lambda i: (i, 0)),
                  pl.BlockSpec(memory_space=pl.ANY)],
        out_specs=pl.BlockSpec(memory_space=pl.ANY),
        out_shape=jax.ShapeDtypeStruct((n_rows, d), xt.dtype),
        scratch_shapes=[pltpu.SemaphoreType.DMA(())],
        input_output_aliases={2: 0},
        compiler_params=pltpu.CompilerParams(dimension_semantics=("arbitrary",), has_side_effects=True),
    )(dest2, xt, jnp.zeros((n_rows, d), xt.dtype))


def _expert_kernel(te_ref, nu_ref, xd_ref, w1_ref, w3_ref, w2_ref, y_ref, w1b, w3b, w2b):
    i = pl.program_id(0)
    used = i < nu_ref[0]
    new_expert = (i == 0) | (te_ref[i] != te_ref[jnp.maximum(i - 1, 0)])

    @pl.when(used & new_expert)
    def _():
        w1b[...] = w1_ref[0].astype(jnp.bfloat16)
        w3b[...] = w3_ref[0].astype(jnp.bfloat16)
        w2b[...] = w2_ref[0].astype(jnp.bfloat16)

    @pl.when(used)
    def _():
        xb = xd_ref[...].astype(jnp.bfloat16)
        h1 = jnp.dot(xb, w1b[...], preferred_element_type=jnp.float32)
        h3 = jnp.dot(xb, w3b[...], preferred_element_type=jnp.float32)
        a = (h1 * jax.nn.sigmoid(h1) * h3).astype(jnp.bfloat16)
        y_ref[...] = jnp.dot(a, w2b[...], preferred_element_type=jnp.float32)

    @pl.when(jnp.logical_not(used))
    def _():
        y_ref[...] = jnp.zeros_like(y_ref)


def _moe_experts(x_disp, tile_expert, n_used, w1, w3, w2):
    n_rows, d = x_disp.shape
    n_tiles = n_rows // EXPERT_TILE
    hid = w1.shape[2]

    def row_map(i, te, nu):
        return (i, 0)

    def w_map(i, te, nu):
        return (te[i], 0, 0)

    grid_spec = pltpu.PrefetchScalarGridSpec(
        num_scalar_prefetch=2,
        grid=(n_tiles,),
        in_specs=[pl.BlockSpec((EXPERT_TILE, d), row_map),
                  pl.BlockSpec((1, d, hid), w_map),
                  pl.BlockSpec((1, d, hid), w_map),
                  pl.BlockSpec((1, hid, d), w_map)],
        out_specs=pl.BlockSpec((EXPERT_TILE, d), row_map),
        scratch_shapes=[pltpu.VMEM((d, hid), jnp.bfloat16), pltpu.VMEM((d, hid), jnp.bfloat16),
                        pltpu.VMEM((hid, d), jnp.bfloat16)],
    )
    return pl.pallas_call(
        _expert_kernel,
        name="moe_experts",
        grid_spec=grid_spec,
        out_shape=jax.ShapeDtypeStruct((n_rows, d), jnp.float32),
        compiler_params=pltpu.CompilerParams(dimension_semantics=("arbitrary",), vmem_limit_bytes=VMEM_LIMIT),
    )(tile_expert, n_used, x_disp, w1, w3, w2)


def _combine_kernel(dest_ref, dest_next_ref, x_ref, route_ref, g_ref, b_ref, yd_hbm, o_ref, ybuf, sem):
    i = pl.program_id(0)
    n = pl.num_programs(0)
    tm = x_ref.shape[0]
    rows = EXPERT_TOP_K * tm

    def start_tile(dref, slot):
        def body(r, c):
            pltpu.make_async_copy(
                yd_hbm.at[pl.ds(dref[0, 0, r], 1)], ybuf.at[slot, pl.ds(r, 1)], sem.at[slot]).start()
            return c
        lax.fori_loop(0, rows, body, 0, unroll=8)

    slot = i % 2

    @pl.when(i == 0)
    def _():
        start_tile(dest_ref, 0)

    @pl.when(i + 1 < n)
    def _():
        start_tile(dest_next_ref, 1 - slot)

    pltpu.make_async_copy(yd_hbm.at[pl.ds(0, rows)], ybuf.at[slot], sem.at[slot]).wait()
    y1 = ybuf[slot, 0:tm, :]
    y2 = ybuf[slot, tm:rows, :]
    route = route_ref[...].T
    y = DN_ALPHA * x_ref[...] + (y1 * route[:, 0:1] + y2 * route[:, 1:2])
    o_ref[...] = _ln_rows(y, g_ref[...], b_ref[...])


def _moe_combine_ln(xt, y_disp, dest2, route, ln_g, ln_b):
    t, d = xt.shape
    n = t // ROUTE_TILE
    rows = EXPERT_TOP_K * ROUTE_TILE
    return pl.pallas_call(
        _combine_kernel,
        name="moe_combine_ln",
        grid=(n,),
        in_specs=[pl.BlockSpec((1, 1, rows), lambda i: (i, 0, 0), memory_space=pltpu.SMEM),
                  pl.BlockSpec((1, 1, rows), lambda i: (jnp.minimum(i + 1, n - 1), 0, 0), memory_space=pltpu.SMEM),
                  pl.BlockSpec((ROUTE_TILE, d), lambda i: (i, 0)),
                  pl.BlockSpec((ROUTE_ROWS, ROUTE_TILE), lambda i: (0, i)),
                  pl.BlockSpec((1, d), lambda i: (0, 0)),
                  pl.BlockSpec((1, d), lambda i: (0, 0)),
                  pl.BlockSpec(memory_space=pl.ANY)],
        out_specs=pl.BlockSpec((ROUTE_TILE, d), lambda i: (i, 0)),
        out_shape=jax.ShapeDtypeStruct((t, d), jnp.float32),
        scratch_shapes=[pltpu.VMEM((2, rows, d), jnp.float32), pltpu.SemaphoreType.DMA((2,))],
        compiler_params=pltpu.CompilerParams(dimension_semantics=("arbitrary",), vmem_limit_bytes=VMEM_LIMIT),
    )(dest2, dest2, xt, route, ln_g.reshape(1, d), ln_b.reshape(1, d), y_disp)


def _moe_sublayer(x, wg, bg, we, be, w1, w3, w2, ln_g, ln_b):
    b, s, d = x.shape
    t = b * s
    xt = x.reshape(t, d)
    route, cnt = _moe_route(xt, wg, bg, we, be)
    counts = cnt[N_GROUPS:N_GROUPS + N_EXPERTS, 0].astype(jnp.int32)
    n_tiles = (t * EXPERT_TOP_K) // EXPERT_TILE + N_EXPERTS
    tiles_per = (counts + EXPERT_TILE - 1) // EXPERT_TILE
    tile_end = jnp.cumsum(tiles_per)
    pad_start = (tile_end - tiles_per) * EXPERT_TILE
    n_used = tile_end[-1:].astype(jnp.int32)
    tile_ids = jnp.minimum(jnp.arange(n_tiles), n_used[0] - 1)
    tile_expert = jnp.sum(tile_ids[:, None] >= tile_end[None, :], axis=1).astype(jnp.int32)
    dest = pad_start[route[2:4].astype(jnp.int32)] + route[4:6].astype(jnp.int32)
    dest2 = jnp.swapaxes(dest.reshape(EXPERT_TOP_K, t // ROUTE_TILE, ROUTE_TILE), 0, 1)
    dest2 = dest2.reshape(t // ROUTE_TILE, 1, EXPERT_TOP_K * ROUTE_TILE)
    x_disp = _moe_dispatch(xt, dest2, n_tiles * EXPERT_TILE)
    y_disp = _moe_experts(x_disp, tile_expert, n_used, w1, w3, w2)
    return _moe_combine_ln(xt, y_disp, dest2, route, ln_g, ln_b).reshape(b, s, d)


def kernel(x, mem, mem_wk, mem_wv, ev_w_in, ev_conv_w, ev_conv_b, ev_cnorm_g, ev_cnorm_b, ev_cmp_pe_k, ev_cmp_w1_k, ev_cmp_w2_k, ev_cmp_pe_v, ev_cmp_w1_v, ev_cmp_w2_v, ev_w_out, od_w_in, od_conv_w, od_w_out, ln_mix_g, ln_mix_b, xa_wq, xa_wo, ln_xa_g, ln_xa_b, moe_wg, moe_bg, moe_we, moe_be, moe_w1, moe_w3, moe_w2, ln_ffn_g, ln_ffn_b):
    b, s, d = x.shape
    mem_k, mem_v = _mem_kv(mem, mem_wk, mem_wv)
    for layer in range(DEPTH):
        i = layer // 2
        if layer % 2 == 0:
            a, kv_in, k_sel, k_win, q_t, v_sel_t, v_win_t, gate_t = _even_in_proj(x, ev_w_in[i])
            a = _conformer_conv(a, ev_conv_w[i], ev_conv_b[i], ev_cnorm_g[i], ev_cnorm_b[i])
            k_cmp, v_cmp_t = _compress_kv(kv_in, ev_cmp_pe_k[i], ev_cmp_w1_k[i], ev_cmp_w2_k[i], ev_cmp_pe_v[i], ev_cmp_w1_v[i], ev_cmp_w2_v[i])
            o = _nsa_attention(q_t, gate_t, k_cmp, v_cmp_t, k_sel, v_sel_t, k_win, v_win_t)
            x = _proj_residual_ln(a.reshape(b * s, -1), o.reshape(b * s, -1), x.reshape(b * s, d), ev_w_out[i], ln_mix_g[layer], ln_mix_b[layer]).reshape(b, s, d)
        else:
            x = _odd_mixer_sublayer(x, od_w_in[i], od_conv_w[i], od_w_out[i], ln_mix_g[layer], ln_mix_b[layer])
        x = _xattn_sublayer(x, mem_k, mem_v, xa_wq[layer], xa_wo[layer], ln_xa_g[layer], ln_xa_b[layer])
        x = _moe_sublayer(x, moe_wg[layer], moe_bg[layer], moe_we[layer], moe_be[layer], moe_w1[layer], moe_w3[layer], moe_w2[layer], ln_ffn_g[layer], ln_ffn_b[layer])
    return x
```

```python
import functools

import numpy as np
import jax
import jax.numpy as jnp
from jax import lax
from jax.experimental import pallas as pl
from jax.experimental.pallas import tpu as pltpu

D_MODEL = 1024
DEPTH = 2
CONV_CH = D_MODEL // 2
CONV_WIDTH = 31
NSA_HEADS = 8
NSA_KV_HEADS = 2
HEAD_DIM = (D_MODEL // 2) // NSA_HEADS
CMP_BLOCK = 32
CMP_STRIDE = 16
SEL_BLOCK = 64
SEL_TOP_N = 16
WINDOW = 512
Q_BLOCK = 128
FORCE_SCORE = 1e4
SHORT_CONV_WIDTH = 3
XA_HEADS = 4
XA_HEAD_DIM = D_MODEL // XA_HEADS
N_GROUPS = 4
EXPERTS_PER_GROUP = 8
N_EXPERTS = N_GROUPS * EXPERTS_PER_GROUP
EXPERT_TOP_K = 2
DN_ALPHA = (2 * DEPTH) ** 0.25
LN_EPS = 1e-5
NEG_INF = -1e30
KV_COLS = NSA_KV_HEADS * HEAD_DIM
QCOLS = NSA_HEADS * HEAD_DIM
GATE_ROWS = 3 * NSA_HEADS

LANES = 128
SUBLANES = 8
HEADS_PER_KV = NSA_HEADS // NSA_KV_HEADS
QL = Q_BLOCK * HEADS_PER_KV
SEL_TILE = 512
WIN_SPAN = WINDOW + Q_BLOCK
BLOCKS_PER_TILE = SEL_TILE // SEL_BLOCK
SEQ_TILE = 512
HALO = 32
ODD_HALO = 8
ROUTE_TILE = 512
EXPERT_TILE = 256
ROUTE_ROWS = 8
VMEM_LIMIT = 56 * 1024 * 1024

_NT = (((1,), (1,)), ((), ()))


def _ln_rows(y, g, b):
    mu = jnp.mean(y, axis=-1, keepdims=True)
    yc = y - mu
    var = jnp.mean(yc * yc, axis=-1, keepdims=True)
    return yc * lax.rsqrt(var + LN_EPS) * g + b


def _even_in_kernel(x_ref, wa_ref, wkv_ref, wk2_ref, wqT_ref, wvT_ref, wgT_ref,
                    a_ref, kv_ref, ks_ref, kw_ref, qT_ref, vsT_ref, vwT_ref, gT_ref):
    f32 = jnp.float32
    bf16 = jnp.bfloat16
    xb = x_ref[0].astype(bf16)
    av = jnp.dot(xb, wa_ref[...], preferred_element_type=f32)
    a_ref[0] = av[:, :CONV_CH] * jax.nn.sigmoid(av[:, CONV_CH:])
    kv_ref[0] = jnp.dot(xb, wkv_ref[...], preferred_element_type=f32)
    k2 = jnp.dot(xb, wk2_ref[...], preferred_element_type=f32)
    ks_ref[0] = k2[:, :KV_COLS].astype(bf16)
    kw_ref[0] = k2[:, KV_COLS:].astype(bf16)
    qT_ref[0] = lax.dot_general(wqT_ref[...], xb, _NT, preferred_element_type=f32).astype(bf16)
    vT = lax.dot_general(wvT_ref[...], xb, _NT, preferred_element_type=f32).astype(bf16)
    vsT_ref[0, 0] = vT[:KV_COLS]
    for j in range(SEQ_TILE // Q_BLOCK):
        vwT_ref[0, j] = vT[KV_COLS:, j * Q_BLOCK:(j + 1) * Q_BLOCK]
    gT_ref[0] = lax.dot_general(wgT_ref[...], xb, _NT, preferred_element_type=f32)


def _even_in_proj(x, w_in):
    b, s, d = x.shape
    bf16 = jnp.bfloat16
    c = np.cumsum((0, CONV_CH, CONV_CH, QCOLS, KV_COLS, KV_COLS, KV_COLS, KV_COLS, KV_COLS, KV_COLS, GATE_ROWS))
    col = lambda i, j: w_in[:, c[i]:c[j]]
    wa = col(0, 2).astype(bf16)
    wq_t = col(2, 3).T.astype(bf16)
    wkv = col(3, 5).astype(bf16)
    wk2 = jnp.concatenate([col(5, 6), col(7, 8)], axis=1).astype(bf16)
    wv_t = jnp.concatenate([col(6, 7), col(8, 9)], axis=1).T.astype(bf16)
    wg_t = col(9, 10).T.astype(bf16)
    ts = SEQ_TILE
    full = lambda shape: pl.BlockSpec(shape, lambda i, j: (0,) * len(shape))
    return pl.pallas_call(
        _even_in_kernel,
        name="even_in_proj",
        grid=(b, s // ts),
        in_specs=[pl.BlockSpec((1, ts, d), lambda i, j: (i, j, 0)),
                  full(wa.shape), full(wkv.shape), full(wk2.shape), full(wq_t.shape), full(wv_t.shape), full(wg_t.shape)],
        out_specs=[pl.BlockSpec((1, ts, CONV_CH), lambda i, j: (i, j, 0)),
                   pl.BlockSpec((1, ts, 2 * KV_COLS), lambda i, j: (i, j, 0)),
                   pl.BlockSpec((1, ts, KV_COLS), lambda i, j: (i, j, 0)),
                   pl.BlockSpec((1, ts, KV_COLS), lambda i, j: (i, j, 0)),
                   pl.BlockSpec((1, QCOLS, ts), lambda i, j: (i, 0, j)),
                   pl.BlockSpec((1, 1, KV_COLS, ts), lambda i, j: (i, j, 0, 0)),
                   pl.BlockSpec((1, ts // Q_BLOCK, KV_COLS, Q_BLOCK), lambda i, j: (i, j, 0, 0)),
                   pl.BlockSpec((1, GATE_ROWS, ts), lambda i, j: (i, 0, j))],
        out_shape=[jax.ShapeDtypeStruct((b, s, CONV_CH), jnp.float32),
                   jax.ShapeDtypeStruct((b, s, 2 * KV_COLS), jnp.float32),
                   jax.ShapeDtypeStruct((b, s, KV_COLS), bf16),
                   jax.ShapeDtypeStruct((b, s, KV_COLS), bf16),
                   jax.ShapeDtypeStruct((b, QCOLS, s), bf16),
                   jax.ShapeDtypeStruct((b, s // SEL_TILE, KV_COLS, SEL_TILE), bf16),
                   jax.ShapeDtypeStruct((b, s // Q_BLOCK, KV_COLS, Q_BLOCK), bf16),
                   jax.ShapeDtypeStruct((b, GATE_ROWS, s), jnp.float32)],
        compiler_params=pltpu.CompilerParams(dimension_semantics=("arbitrary", "arbitrary"), vmem_limit_bytes=VMEM_LIMIT),
    )(x, wa, wkv, wk2, wq_t, wv_t, wg_t)


def _conv_kernel(cur_ref, halo_ref, w_ref, cb_ref, g_ref, b_ref, o_ref, ext_ref, win_ref):
    j = pl.program_id(1)
    ts = cur_ref.shape[1]
    halo = halo_ref[0]
    ext_ref[0:HALO, :] = jnp.where(j > 0, halo, jnp.zeros_like(halo))
    ext_ref[HALO:HALO + ts, :] = cur_ref[0]
    first = HALO - (CONV_WIDTH - 1)
    acc = jnp.zeros((ts, CONV_CH), jnp.float32)
    for p in range(SUBLANES):
        n_a = len(range(p, CONV_WIDTH, SUBLANES))
        rows = ts + SUBLANES * (n_a - 1)
        win_ref[0:rows, :] = ext_ref[first + p:first + p + rows, :]
        for a in range(n_a):
            k = SUBLANES * a + p
            acc = acc + w_ref[k:k + 1, :] * win_ref[SUBLANES * a:SUBLANES * a + ts, :]
    y = _ln_rows(acc + cb_ref[...], g_ref[...], b_ref[...])
    o_ref[0] = (y * jax.nn.sigmoid(y)).astype(o_ref.dtype)


def _conformer_conv(a, conv_w, conv_b, cn_g, cn_b):
    b, s, c = a.shape
    ts = SEQ_TILE
    per = ts // HALO
    row = lambda v: v.reshape(1, c)
    return pl.pallas_call(
        _conv_kernel,
        name="conformer_conv",
        grid=(b, s // ts),
        in_specs=[pl.BlockSpec((1, ts, c), lambda i, j: (i, j, 0)),
                  pl.BlockSpec((1, HALO, c), lambda i, j: (i, jnp.maximum(j * per - 1, 0), 0)),
                  pl.BlockSpec((CONV_WIDTH, c), lambda i, j: (0, 0)),
                  pl.BlockSpec((1, c), lambda i, j: (0, 0)),
                  pl.BlockSpec((1, c), lambda i, j: (0, 0)),
                  pl.BlockSpec((1, c), lambda i, j: (0, 0))],
        out_specs=pl.BlockSpec((1, ts, c), lambda i, j: (i, j, 0)),
        out_shape=jax.ShapeDtypeStruct((b, s, c), jnp.bfloat16),
        scratch_shapes=[pltpu.VMEM((HALO + ts, c), jnp.float32), pltpu.VMEM((HALO + ts, c), jnp.float32)],
        compiler_params=pltpu.CompilerParams(dimension_semantics=("arbitrary", "arbitrary")),
    )(a, a, conv_w, row(conv_b), row(cn_g), row(cn_b))


def _compress_kernel(r_ref, pe_ref, w1_ref, w2_ref, o_ref):
    f32 = jnp.float32
    bf16 = jnp.bfloat16
    r = r_ref[0, 0]
    half = r.shape[1]
    ha = jnp.dot((r + pe_ref[0, 0:1, :]).astype(bf16), w1_ref[0, :half, :], preferred_element_type=f32)
    hb = jnp.dot((r + pe_ref[0, 1:2, :]).astype(bf16), w1_ref[0, half:, :], preferred_element_type=f32)
    hb_next = jnp.concatenate([hb[1:], jnp.zeros_like(hb[0:1])], axis=0)
    hid = jax.nn.gelu(ha + hb_next).astype(bf16)
    o_ref[0, 0] = jnp.dot(hid, w2_ref[0], preferred_element_type=f32).astype(bf16)


def _compress_kv(kv_in, pe_k, w1_k, w2_k, pe_v, w1_v, w2_v):
    b, s, _ = kv_in.shape
    bf16 = jnp.bfloat16
    nch = s // CMP_STRIDE
    g = NSA_KV_HEADS
    r = kv_in.reshape(b, nch, CMP_STRIDE, 2 * g, HEAD_DIM)
    r = jnp.transpose(r, (0, 3, 1, 2, 4)).reshape(b, 2 * g, nch, CMP_STRIDE * HEAD_DIM)
    pe = jnp.stack([pe_k, pe_v]).reshape(2, 2, CMP_STRIDE * HEAD_DIM)
    w1 = jnp.stack([w1_k, w1_v]).astype(bf16)
    w2 = jnp.stack([w2_k, w2_v]).astype(bf16)
    out = pl.pallas_call(
        _compress_kernel,
        name="compress_kv",
        grid=(b, 2 * g),
        in_specs=[pl.BlockSpec((1, 1, nch, r.shape[3]), lambda i, j: (i, j, 0, 0)),
                  pl.BlockSpec((1, 2, pe.shape[2]), lambda i, j: (j // g, 0, 0)),
                  pl.BlockSpec((1,) + w1.shape[1:], lambda i, j: (j // g, 0, 0)),
                  pl.BlockSpec((1,) + w2.shape[1:], lambda i, j: (j // g, 0, 0))],
        out_specs=pl.BlockSpec((1, 1, nch, HEAD_DIM), lambda i, j: (i, j, 0, 0)),
        out_shape=jax.ShapeDtypeStruct((b, 2 * g, nch, HEAD_DIM), bf16),
        compiler_params=pltpu.CompilerParams(dimension_semantics=("arbitrary", "arbitrary")),
    )(r, pe, w1, w2)
    kc = jnp.transpose(out[:, :g], (0, 2, 1, 3)).reshape(b, nch, KV_COLS)
    vcT = jnp.transpose(out[:, g:], (0, 1, 3, 2)).reshape(b, KV_COLS, nch)
    return kc, vcT


def _proj_ln_kernel(a_ref, o_ref, x_ref, w_ref, g_ref, b_ref, y_ref):
    lhs = jnp.concatenate([a_ref[...], o_ref[...]], axis=1)
    mix = jnp.dot(lhs, w_ref[...], preferred_element_type=jnp.float32)
    y_ref[...] = _ln_rows(DN_ALPHA * x_ref[...] + mix, g_ref[...], b_ref[...])


def _proj_residual_ln(a, o, x, w, ln_g, ln_b):
    t, d = x.shape
    tm = SEQ_TILE
    return pl.pallas_call(
        _proj_ln_kernel,
        name="proj_residual_ln",
        grid=(t // tm,),
        in_specs=[pl.BlockSpec((tm, a.shape[1]), lambda i: (i, 0)),
                  pl.BlockSpec((tm, o.shape[1]), lambda i: (i, 0)),
                  pl.BlockSpec((tm, d), lambda i: (i, 0)),
                  pl.BlockSpec(w.shape, lambda i: (0, 0)),
                  pl.BlockSpec((1, d), lambda i: (0, 0)),
                  pl.BlockSpec((1, d), lambda i: (0, 0))],
        out_specs=pl.BlockSpec((tm, d), lambda i: (i, 0)),
        out_shape=jax.ShapeDtypeStruct((t, d), jnp.float32),
        compiler_params=pltpu.CompilerParams(dimension_semantics=("arbitrary",), vmem_limit_bytes=VMEM_LIMIT),
    )(a, o, x, w.astype(jnp.bfloat16), ln_g.reshape(1, d), ln_b.reshape(1, d))


def _mem_kv_kernel(m_ref, wk_ref, wv_ref, k_ref, v_ref):
    mb = m_ref[...].astype(jnp.bfloat16)
    k_ref[...] = jnp.dot(mb, wk_ref[...], preferred_element_type=jnp.float32).astype(jnp.bfloat16)
    v_ref[...] = jnp.dot(mb, wv_ref[...], preferred_element_type=jnp.float32).astype(jnp.bfloat16)


def _mem_kv(mem, wk, wv):
    b, m, d = mem.shape
    bf16 = jnp.bfloat16
    k, v = pl.pallas_call(
        _mem_kv_kernel,
        name="mem_kv",
        grid=(b,),
        in_specs=[pl.BlockSpec((m, d), lambda i: (i, 0)), pl.BlockSpec((d, d), lambda i: (0, 0)), pl.BlockSpec((d, d), lambda i: (0, 0))],
        out_specs=[pl.BlockSpec((m, d), lambda i: (i, 0)), pl.BlockSpec((m, d), lambda i: (i, 0))],
        out_shape=[jax.ShapeDtypeStruct((b * m, d), bf16), jax.ShapeDtypeStruct((b * m, d), bf16)],
        compiler_params=pltpu.CompilerParams(dimension_semantics=("arbitrary",), vmem_limit_bytes=VMEM_LIMIT),
    )(mem.reshape(b * m, d), wk.astype(bf16), wv.astype(bf16))
    return k.reshape(b, m, d), v.reshape(b, m, d)


def _xattn_kernel(x_ref, k_ref, v_ref, wq_ref, wo_ref, g_ref, b_ref, y_ref):
    f32 = jnp.float32
    bf16 = jnp.bfloat16
    x = x_ref[0]
    q = (jnp.dot(x.astype(bf16), wq_ref[...], preferred_element_type=f32).astype(bf16)
         * jnp.asarray(XA_HEAD_DIM ** -0.5, bf16))
    heads = []
    for h in range(XA_HEADS):
        cols = slice(h * XA_HEAD_DIM, (h + 1) * XA_HEAD_DIM)
        s = lax.dot_general(q[:, cols], k_ref[0, :, cols], _NT, preferred_element_type=f32)
        m = jnp.max(s, axis=1, keepdims=True)
        p = jnp.exp(s - m)
        p = p / jnp.sum(p, axis=1, keepdims=True)
        heads.append(jnp.dot(p.astype(bf16), v_ref[0, :, cols], preferred_element_type=f32).astype(bf16))
    att = jnp.concatenate(heads, axis=1)
    out = jnp.dot(att, wo_ref[...], preferred_element_type=f32)
    y_ref[0] = _ln_rows(DN_ALPHA * x + out, g_ref[...], b_ref[...])


def _xattn_sublayer(x, mem_k, mem_v, wq, wo, ln_g, ln_b):
    b, s, d = x.shape
    m = mem_k.shape[1]
    ts = SEQ_TILE
    bf16 = jnp.bfloat16
    return pl.pallas_call(
        _xattn_kernel,
        name="xattn_sublayer",
        grid=(b, s // ts),
        in_specs=[pl.BlockSpec((1, ts, d), lambda i, j: (i, j, 0)),
                  pl.BlockSpec((1, m, d), lambda i, j: (i, 0, 0)),
                  pl.BlockSpec((1, m, d), lambda i, j: (i, 0, 0)),
                  pl.BlockSpec((d, d), lambda i, j: (0, 0)),
                  pl.BlockSpec((d, d), lambda i, j: (0, 0)),
                  pl.BlockSpec((1, d), lambda i, j: (0, 0)),
                  pl.BlockSpec((1, d), lambda i, j: (0, 0))],
        out_specs=pl.BlockSpec((1, ts, d), lambda i, j: (i, j, 0)),
        out_shape=jax.ShapeDtypeStruct((b, s, d), jnp.float32),
        compiler_params=pltpu.CompilerParams(dimension_semantics=("arbitrary", "arbitrary"), vmem_limit_bytes=VMEM_LIMIT),
    )(x, mem_k, mem_v, wq.astype(bf16), wo.astype(bf16), ln_g.reshape(1, d), ln_b.reshape(1, d))


def _odd_kernel(x_ref, halo_ref, wb_ref, wc_ref, wh_ref, cw_ref, wo_ref, g_ref, b_ref, y_ref):
    f32 = jnp.float32
    bf16 = jnp.bfloat16
    j = pl.program_id(1)
    ts = x_ref.shape[1]
    x = x_ref[0]
    xe = jnp.concatenate([halo_ref[0], x], axis=0).astype(bf16)
    u = (jnp.dot(xe, wc_ref[...], preferred_element_type=f32) * jnp.dot(xe, wh_ref[...], preferred_element_type=f32))
    row = lax.broadcasted_iota(jnp.int32, (ODD_HALO + ts, 1), 0)
    u = jnp.where((row >= ODD_HALO) | (j > 0), u, 0.0)
    conv = jnp.zeros((ts, u.shape[1]), f32)
    for k in range(SHORT_CONV_WIDTH):
        off = ODD_HALO - (SHORT_CONV_WIDTH - 1) + k
        conv = conv + cw_ref[k:k + 1, :] * u[off:off + ts, :]
    gate_b = jnp.dot(xe[ODD_HALO:], wb_ref[...], preferred_element_type=f32)
    mix = jnp.dot((gate_b * conv).astype(bf16), wo_ref[...], preferred_element_type=f32)
    y_ref[0] = _ln_rows(DN_ALPHA * x + mix, g_ref[...], b_ref[...])


def _odd_mixer_sublayer(x, w_in, conv_w, w_out, ln_g, ln_b):
    b, s, d = x.shape
    ts = SEQ_TILE
    per = ts // ODD_HALO
    bf16 = jnp.bfloat16
    wb, wc, wh = (w_in[:, i * d:(i + 1) * d].astype(bf16) for i in range(3))
    full = lambda shape: pl.BlockSpec(shape, lambda i, j: (0,) * len(shape))
    return pl.pallas_call(
        _odd_kernel,
        name="odd_mixer_sublayer",
        grid=(b, s // ts),
        in_specs=[pl.BlockSpec((1, ts, d), lambda i, j: (i, j, 0)),
                  pl.BlockSpec((1, ODD_HALO, d), lambda i, j: (i, jnp.maximum(j * per - 1, 0), 0)),
                  full((d, d)), full((d, d)), full((d, d)), full(conv_w.shape), full((d, d)), full((1, d)), full((1, d))],
        out_specs=pl.BlockSpec((1, ts, d), lambda i, j: (i, j, 0)),
        out_shape=jax.ShapeDtypeStruct((b, s, d), jnp.float32),
        compiler_params=pltpu.CompilerParams(dimension_semantics=("arbitrary", "arbitrary"), vmem_limit_bytes=VMEM_LIMIT),
    )(x, x, wb, wc, wh, conv_w, w_out.astype(bf16), ln_g.reshape(1, d), ln_b.reshape(1, d))


def _cmp_to_sel_matrix(n_cmp, n_sel):
    c0 = np.arange(n_cmp) * CMP_STRIDE
    s0 = np.arange(n_sel) * SEL_BLOCK
    ov = np.minimum(c0[:, None] + CMP_BLOCK, s0[None, :] + SEL_BLOCK) - np.maximum(c0[:, None], s0[None, :])
    return (np.clip(ov, 0, None) / CMP_BLOCK).astype(np.float32)


def _nsa_kernel(qT_ref, gT_ref, kc_ref, vcT_ref, mselT_ref, ks_ref, vsT_ref, kw_ref, vwT_ref, o_ref,
                score_ref, sel_ref, *, n_sel):
    c = pl.program_id(1)
    q0 = c * Q_BLOCK
    f32 = jnp.float32
    bf16 = jnp.bfloat16
    n_cmp_pad = kc_ref.shape[1]

    lane_q = lax.broadcasted_iota(jnp.int32, (1, QL), 1) % Q_BLOCK
    t_row = q0 + lane_q
    t_row_q = q0 + lax.broadcasted_iota(jnp.int32, (1, Q_BLOCK), 1)
    cur_q = t_row_q // SEL_BLOCK

    top_n = min(SEL_TOP_N, n_sel)
    m_iota = lax.broadcasted_iota(jnp.int32, (n_sel, Q_BLOCK), 0)
    forced = (m_iota == 0) | (m_iota == cur_q) | (m_iota == cur_q - 1)
    valid = m_iota <= cur_q
    n_comp = jnp.minimum(q0 // SEL_BLOCK + 2, n_sel)

    q_g, o_c = [], []
    for g in range(NSA_KV_HEADS):
        pieces = []
        for hg in range(HEADS_PER_KV):
            h = g * HEADS_PER_KV + hg
            qh = qT_ref[0, h * HEAD_DIM:(h + 1) * HEAD_DIM, :] * jnp.asarray(HEAD_DIM ** -0.5, bf16)
            z = jnp.zeros_like(qh)
            pieces.append(jnp.concatenate([qh, z] if g == 0 else [z, qh], axis=0))
        qTp = jnp.concatenate(pieces, axis=1)
        q_g.append(qTp)
        rows = slice(g * HEAD_DIM, (g + 1) * HEAD_DIM)

        s_c = jnp.dot(kc_ref[0], qTp, preferred_element_type=f32)
        n_iota = lax.broadcasted_iota(jnp.int32, (n_cmp_pad, QL), 0)
        mask_c = (n_iota * CMP_STRIDE + (CMP_BLOCK - 1)) <= t_row
        s_c = jnp.where(mask_c, s_c, NEG_INF)
        m_c = jnp.max(s_c, axis=0, keepdims=True)
        p_c = jnp.where(mask_c, jnp.exp(s_c - m_c), 0.0)
        l_c = jnp.sum(p_c, axis=0, keepdims=True)
        p_c = p_c * jnp.where(l_c > 0.0, 1.0 / l_c, 0.0)
        p_cb = p_c.astype(bf16)
        o_c.append(jnp.dot(vcT_ref[0], p_cb, preferred_element_type=f32)[rows])
        imp4 = jnp.dot(mselT_ref[...], p_cb, preferred_element_type=f32)
        imp = imp4[:, 0:Q_BLOCK]
        for hg in range(1, HEADS_PER_KV):
            imp = imp + imp4[:, hg * Q_BLOCK:(hg + 1) * Q_BLOCK]

        score = jnp.where(valid, jnp.where(forced, FORCE_SCORE, imp), NEG_INF)
        score_ref[...] = score

        def rank_body(mp, rank, score=score):
            row = score_ref[pl.ds(mp, 1), :]
            beats = (row > score) | ((row == score) & (mp < m_iota))
            return rank + beats.astype(jnp.int32)

        rank = lax.fori_loop(0, n_comp, rank_body, jnp.zeros((n_sel, Q_BLOCK), jnp.int32))
        sel_ref[g] = jnp.where((rank < top_n) & valid, 0.0, NEG_INF)

    def sel_tile(j, carry, causal):
        k_tile = ks_ref[0, j]
        v_tile = vsT_ref[0, j]
        blk0 = pl.multiple_of(j * BLOCKS_PER_TILE, BLOCKS_PER_TILE)
        out = []
        for g in range(NSA_KV_HEADS):
            m_i, l_i, acc = carry[g]
            s = jnp.dot(k_tile, q_g[g], preferred_element_type=f32)
            selrows = sel_ref[g, pl.ds(blk0, BLOCKS_PER_TILE), :]
            bias = jnp.concatenate(
                [jnp.broadcast_to(selrows[r:r + 1, :], (SEL_BLOCK, Q_BLOCK)) for r in range(BLOCKS_PER_TILE)], axis=0)
            if causal:
                key = j * SEL_TILE + lax.broadcasted_iota(jnp.int32, (SEL_TILE, Q_BLOCK), 0)
                bias = jnp.where(key <= t_row_q, bias, NEG_INF)
            s = s + jnp.concatenate([bias] * HEADS_PER_KV, axis=1)
            m_new = jnp.maximum(m_i, jnp.max(s, axis=0, keepdims=True))
            alpha = jnp.exp(m_i - m_new)
            p = jnp.exp(s - m_new)
            l_new = alpha * l_i + jnp.sum(p, axis=0, keepdims=True)
            pv = jnp.dot(v_tile, p.astype(bf16), preferred_element_type=f32)
            out.append((m_new, l_new, alpha * acc + pv))
        return tuple(out)

    n_full = q0 // SEL_TILE
    init = tuple((jnp.full((1, QL), NEG_INF, f32), jnp.zeros((1, QL), f32), jnp.zeros((KV_COLS, QL), f32))
                 for _ in range(NSA_KV_HEADS))
    carry = lax.fori_loop(0, n_full, functools.partial(sel_tile, causal=False), init)
    carry = sel_tile(n_full, carry, causal=True)

    start = pl.multiple_of(jnp.maximum(q0 - WINDOW, 0), Q_BLOCK)
    j0 = start // Q_BLOCK
    k_win = kw_ref[0, pl.ds(start, WIN_SPAN), :]
    key_w = start + lax.broadcasted_iota(jnp.int32, (WIN_SPAN, Q_BLOCK), 0)
    bias_w = jnp.where(key_w <= t_row_q, 0.0, NEG_INF)
    bias_w = jnp.where(key_w > t_row_q - WINDOW, bias_w, NEG_INF)
    bias_w = jnp.concatenate([bias_w] * HEADS_PER_KV, axis=1)
    for g in range(NSA_KV_HEADS):
        rows = slice(g * HEAD_DIM, (g + 1) * HEAD_DIM)
        m_s, l_s, acc_s = carry[g]
        o_s = acc_s[rows] * (1.0 / l_s)
        s_w = jnp.dot(k_win, q_g[g], preferred_element_type=f32) + bias_w
        m_w = jnp.max(s_w, axis=0, keepdims=True)
        p_w = jnp.exp(s_w - m_w)
        l_w = jnp.sum(p_w, axis=0, keepdims=True)
        p_wb = p_w.astype(bf16)
        acc_w = jnp.zeros((KV_COLS, QL), f32)
        for i in range(WIN_SPAN // Q_BLOCK):
            acc_w = acc_w + jnp.dot(vwT_ref[0, j0 + i], p_wb[i * Q_BLOCK:(i + 1) * Q_BLOCK, :], preferred_element_type=f32)
        o_w = acc_w[rows] * (1.0 / l_w)

        for pair in range(HEADS_PER_KV // 2):
            halves = []
            for hg in (2 * pair, 2 * pair + 1):
                h = g * HEADS_PER_KV + hg
                lanes = slice(hg * Q_BLOCK, (hg + 1) * Q_BLOCK)
                gate = jax.nn.sigmoid(gT_ref[0, 3 * h:3 * h + 3, :])
                halves.append(gate[0:1] * o_c[g][:, lanes] + gate[1:2] * o_s[:, lanes] + gate[2:3] * o_w[:, lanes])
            both = jnp.concatenate(halves, axis=0)
            col0 = (g * HEADS_PER_KV + 2 * pair) * HEAD_DIM
            o_ref[0, :, col0:col0 + 2 * HEAD_DIM] = both.T.astype(o_ref.dtype)


def _nsa_attention(qT, gT, kc, vcT, ks, vsT, kw, vwT):
    b, hd, s = qT.shape
    n_sel = s // SEL_BLOCK
    nc = kc.shape[1]
    n_cmp = (s - CMP_BLOCK) // CMP_STRIDE + 1
    mselT = jnp.asarray(np.pad(_cmp_to_sel_matrix(n_cmp, n_sel).T, ((0, 0), (0, nc - n_cmp))), jnp.bfloat16)
    ks4 = ks.reshape(b, s // SEL_TILE, SEL_TILE, KV_COLS)
    return pl.pallas_call(
        functools.partial(_nsa_kernel, n_sel=n_sel),
        name="nsa_attention",
        grid=(b, s // Q_BLOCK),
        in_specs=[
            pl.BlockSpec((1, hd, Q_BLOCK), lambda i, c: (i, 0, c)),
            pl.BlockSpec((1, 3 * NSA_HEADS, Q_BLOCK), lambda i, c: (i, 0, c)),
            pl.BlockSpec((1, nc, KV_COLS), lambda i, c: (i, 0, 0)),
            pl.BlockSpec((1, KV_COLS, nc), lambda i, c: (i, 0, 0)),
            pl.BlockSpec((n_sel, nc), lambda i, c: (0, 0)),
            pl.BlockSpec((1, s // SEL_TILE, SEL_TILE, KV_COLS), lambda i, c: (i, 0, 0, 0)),
            pl.BlockSpec((1, s // SEL_TILE, KV_COLS, SEL_TILE), lambda i, c: (i, 0, 0, 0)),
            pl.BlockSpec((1, s, KV_COLS), lambda i, c: (i, 0, 0)),
            pl.BlockSpec((1, s // Q_BLOCK, KV_COLS, Q_BLOCK), lambda i, c: (i, 0, 0, 0)),
        ],
        out_specs=pl.BlockSpec((1, Q_BLOCK, hd), lambda i, c: (i, c, 0)),
        out_shape=jax.ShapeDtypeStruct((b, s, hd), jnp.bfloat16),
        scratch_shapes=[pltpu.VMEM((n_sel, Q_BLOCK), jnp.float32),
                        pltpu.VMEM((NSA_KV_HEADS, n_sel, Q_BLOCK), jnp.float32)],
        compiler_params=pltpu.CompilerParams(dimension_semantics=("arbitrary", "arbitrary")),
    )(qT, gT, kc, vcT, mselT, ks4, vsT, kw, vwT)


def _router_kernel(x_ref, wT_ref, b_ref, tri_ref, route_ref, cnt_ref, run_ref):
    i = pl.program_id(0)
    f32 = jnp.float32

    @pl.when(i == 0)
    def _():
        run_ref[...] = jnp.zeros_like(run_ref)

    tm = x_ref.shape[0]
    logits = lax.dot_general(wT_ref[...], x_ref[...].astype(jnp.bfloat16), _NT, preferred_element_type=f32) + b_ref[...]
    sub = lax.broadcasted_iota(jnp.int32, (LANES, tm), 0)
    is_g = sub < N_GROUPS
    gl = jnp.where(is_g, logits, NEG_INF)
    g_max = jnp.max(gl, axis=0, keepdims=True)
    g_star = jnp.min(jnp.where(gl == g_max, sub, LANES), axis=0, keepdims=True)
    p_group = 1.0 / jnp.sum(jnp.where(is_g, jnp.exp(gl - g_max), 0.0), axis=0, keepdims=True)
    lo = N_GROUPS + g_star * EXPERTS_PER_GROUP
    in_grp = (sub >= lo) & (sub < lo + EXPERTS_PER_GROUP)
    el = jnp.where(in_grp, logits, NEG_INF)
    v1 = jnp.max(el, axis=0, keepdims=True)
    i1 = jnp.min(jnp.where(el == v1, sub, LANES), axis=0, keepdims=True)
    el2 = jnp.where(sub == i1, NEG_INF, el)
    v2 = jnp.max(el2, axis=0, keepdims=True)
    i2 = jnp.min(jnp.where(el2 == v2, sub, LANES), axis=0, keepdims=True)
    e21 = jnp.exp(v2 - v1)
    gate1 = p_group * (1.0 / (1.0 + e21))
    gate2 = p_group * (e21 / (1.0 + e21))
    oh1 = (sub == i1).astype(f32)
    oh2 = (sub == i2).astype(f32)
    both = oh1 + oh2
    before = jnp.dot(both.astype(jnp.bfloat16), tri_ref[...], preferred_element_type=f32) + run_ref[...]
    rank1 = jnp.sum(oh1 * before, axis=0, keepdims=True)
    rank2 = jnp.sum(oh2 * before, axis=0, keepdims=True)
    run_ref[...] = run_ref[...] + jnp.sum(both, axis=1, keepdims=True)
    cnt_ref[...] = run_ref[...]
    zero = jnp.zeros_like(gate1)
    route_ref[...] = jnp.concatenate(
        [gate1, gate2, (i1 - N_GROUPS).astype(f32), (i2 - N_GROUPS).astype(f32), rank1, rank2, zero, zero], axis=0)


def _moe_route(xt, wg, bg, we, be):
    t, d = xt.shape
    pad = LANES - N_GROUPS - N_EXPERTS
    w_t = jnp.concatenate([wg, we.reshape(d, N_EXPERTS), jnp.zeros((d, pad), wg.dtype)], axis=1).T
    bias = jnp.concatenate([bg, be.reshape(N_EXPERTS), jnp.zeros((pad,), bg.dtype)])[:, None]
    tri = jnp.asarray(np.triu(np.ones((ROUTE_TILE, ROUTE_TILE), np.float32), 1), jnp.bfloat16)
    return pl.pallas_call(
        _router_kernel,
        name="moe_router",
        grid=(t // ROUTE_TILE,),
        in_specs=[pl.BlockSpec((ROUTE_TILE, d), lambda i: (i, 0)),
                  pl.BlockSpec((LANES, d), lambda i: (0, 0)),
                  pl.BlockSpec((LANES, 1), lambda i: (0, 0)),
                  pl.BlockSpec((ROUTE_TILE, ROUTE_TILE), lambda i: (0, 0))],
        out_specs=[pl.BlockSpec((ROUTE_ROWS, ROUTE_TILE), lambda i: (0, i)),
                   pl.BlockSpec((LANES, 1), lambda i: (0, 0))],
        out_shape=[jax.ShapeDtypeStruct((ROUTE_ROWS, t), jnp.float32), jax.ShapeDtypeStruct((LANES, 1), jnp.float32)],
        scratch_shapes=[pltpu.VMEM((LANES, 1), jnp.float32)],
        compiler_params=pltpu.CompilerParams(dimension_semantics=("arbitrary",)),
    )(xt, w_t.astype(jnp.bfloat16), bias, tri)


def _dispatch_kernel(dest_ref, x_ref, zero_hbm, xd_hbm, sem):
    del zero_hbm
    tm = x_ref.shape[0]

    def body(r, c):
        for k in range(EXPERT_TOP_K):
            pltpu.make_async_copy(
                x_ref.at[pl.ds(r, 1)], xd_hbm.at[pl.ds(dest_ref[0, 0, k * tm + r], 1)], sem).start()
        return c

    lax.fori_loop(0, tm, body, 0, unroll=8)
    for k in range(EXPERT_TOP_K):
        pltpu.make_async_copy(x_ref, xd_hbm.at[pl.ds(0, tm)], sem).wait()


def _moe_dispatch(xt, dest2, n_rows):
    t, d = xt.shape
    return pl.pallas_call(
        _dispatch_kernel,
        name="moe_dispatch",
        grid=(t // ROUTE_TILE,),
        in_specs=[pl.BlockSpec((1, 1, EXPERT_TOP_K * ROUTE_TILE), lambda i: (i, 0, 0), memory_space=pltpu.SMEM),
                  pl.BlockSpec((ROUTE_TILE, d), lambda i: (i, 0)),
                  pl.BlockSpec(memory_space=pl.ANY)],
        out_specs=pl.BlockSpec(memory_space=pl.ANY),
        out_shape=jax.ShapeDtypeStruct((n_rows, d), xt.dtype),
        scratch_shapes=[pltpu.SemaphoreType.DMA(())],
        input_output_aliases={2: 0},
        compiler_params=pltpu.CompilerParams(dimension_semantics=("arbitrary",), has_side_effects=True),
    )(dest2, xt, jnp.zeros((n_rows, d), xt.dtype))


def _expert_kernel(te_ref, nu_ref, xd_ref, w1_ref, w3_ref, w2_ref, y_ref, w1b, w3b, w2b):
    i = pl.program_id(0)
    used = i < nu_ref[0]
    new_expert = (i == 0) | (te_ref[i] != te_ref[jnp.maximum(i - 1, 0)])

    @pl.when(used & new_expert)
    def _():
        w1b[...] = w1_ref[0, 0].astype(jnp.bfloat16)
        w3b[...] = w3_ref[0, 0].astype(jnp.bfloat16)
        w2b[...] = w2_ref[0, 0].astype(jnp.bfloat16)

    @pl.when(used)
    def _():
        xb = xd_ref[...].astype(jnp.bfloat16)
        h1 = jnp.dot(xb, w1b[...], preferred_element_type=jnp.float32)
        h3 = jnp.dot(xb, w3b[...], preferred_element_type=jnp.float32)
        a = (h1 * jax.nn.sigmoid(h1) * h3).astype(jnp.bfloat16)
        y_ref[...] = jnp.dot(a, w2b[...], preferred_element_type=jnp.float32)

    @pl.when(jnp.logical_not(used))
    def _():
        y_ref[...] = jnp.zeros_like(y_ref)


def _moe_experts(x_disp, tile_expert, n_used, w1, w3, w2, layer):
    n_rows, d = x_disp.shape
    n_tiles = n_rows // EXPERT_TILE
    hid = w1.shape[3]

    def row_map(i, te, nu):
        return (i, 0)

    def w_map(i, te, nu):
        return (layer, te[i], 0, 0)

    grid_spec = pltpu.PrefetchScalarGridSpec(
        num_scalar_prefetch=2,
        grid=(n_tiles,),
        in_specs=[pl.BlockSpec((EXPERT_TILE, d), row_map),
                  pl.BlockSpec((1, 1, d, hid), w_map),
                  pl.BlockSpec((1, 1, d, hid), w_map),
                  pl.BlockSpec((1, 1, hid, d), w_map)],
        out_specs=pl.BlockSpec((EXPERT_TILE, d), row_map),
        scratch_shapes=[pltpu.VMEM((d, hid), jnp.bfloat16), pltpu.VMEM((d, hid), jnp.bfloat16),
                        pltpu.VMEM((hid, d), jnp.bfloat16)],
    )
    return pl.pallas_call(
        _expert_kernel,
        name="moe_experts",
        grid_spec=grid_spec,
        out_shape=jax.ShapeDtypeStruct((n_rows, d), jnp.float32),
        compiler_params=pltpu.CompilerParams(dimension_semantics=("arbitrary",), vmem_limit_bytes=VMEM_LIMIT),
    )(tile_expert, n_used, x_disp, w1, w3, w2)


def _combine_kernel(dest_ref, dest_next_ref, x_ref, route_ref, g_ref, b_ref, yd_hbm, o_ref, ybuf, sem):
    i = pl.program_id(0)
    n = pl.num_programs(0)
    tm = x_ref.shape[0]
    rows = EXPERT_TOP_K * tm

    def start_tile(dref, slot):
        def body(r, c):
            pltpu.make_async_copy(
                yd_hbm.at[pl.ds(dref[0, 0, r], 1)], ybuf.at[slot, pl.ds(r, 1)], sem.at[slot]).start()
            return c
        lax.fori_loop(0, rows, body, 0, unroll=8)

    slot = i % 2

    @pl.when(i == 0)
    def _():
        start_tile(dest_ref, 0)

    @pl.when(i + 1 < n)
    def _():
        start_tile(dest_next_ref, 1 - slot)

    pltpu.make_async_copy(yd_hbm.at[pl.ds(0, rows)], ybuf.at[slot], sem.at[slot]).wait()
    y1 = ybuf[slot, 0:tm, :]
    y2 = ybuf[slot, tm:rows, :]
    route = route_ref[...].T
    y = DN_ALPHA * x_ref[...] + (y1 * route[:, 0:1] + y2 * route[:, 1:2])
    o_ref[...] = _ln_rows(y, g_ref[...], b_ref[...])


def _moe_combine_ln(xt, y_disp, dest2, route, ln_g, ln_b):
    t, d = xt.shape
    n = t // ROUTE_TILE
    rows = EXPERT_TOP_K * ROUTE_TILE
    return pl.pallas_call(
        _combine_kernel,
        name="moe_combine_ln",
        grid=(n,),
        in_specs=[pl.BlockSpec((1, 1, rows), lambda i: (i, 0, 0), memory_space=pltpu.SMEM),
                  pl.BlockSpec((1, 1, rows), lambda i: (jnp.minimum(i + 1, n - 1), 0, 0), memory_space=pltpu.SMEM),
                  pl.BlockSpec((ROUTE_TILE, d), lambda i: (i, 0)),
                  pl.BlockSpec((ROUTE_ROWS, ROUTE_TILE), lambda i: (0, i)),
                  pl.BlockSpec((1, d), lambda i: (0, 0)),
                  pl.BlockSpec((1, d), lambda i: (0, 0)),
                  pl.BlockSpec(memory_space=pl.ANY)],
        out_specs=pl.BlockSpec((ROUTE_TILE, d), lambda i: (i, 0)),
        out_shape=jax.ShapeDtypeStruct((t, d), jnp.float32),
        scratch_shapes=[pltpu.VMEM((2, rows, d), jnp.float32), pltpu.SemaphoreType.DMA((2,))],
        compiler_params=pltpu.CompilerParams(dimension_semantics=("arbitrary",), vmem_limit_bytes=VMEM_LIMIT),
    )(dest2, dest2, xt, route, ln_g.reshape(1, d), ln_b.reshape(1, d), y_disp)


def _moe_sublayer(x, wg, bg, we, be, w1, w3, w2, layer, ln_g, ln_b):
    b, s, d = x.shape
    t = b * s
    xt = x.reshape(t, d)
    route, cnt = _moe_route(xt, wg, bg, we, be)
    counts = cnt[N_GROUPS:N_GROUPS + N_EXPERTS, 0].astype(jnp.int32)
    n_tiles = (t * EXPERT_TOP_K) // EXPERT_TILE + N_EXPERTS
    tiles_per = (counts + EXPERT_TILE - 1) // EXPERT_TILE
    tile_end = jnp.cumsum(tiles_per)
    pad_start = (tile_end - tiles_per) * EXPERT_TILE
    n_used = tile_end[-1:].astype(jnp.int32)
    tile_ids = jnp.minimum(jnp.arange(n_tiles), n_used[0] - 1)
    tile_expert = jnp.sum(tile_ids[:, None] >= tile_end[None, :], axis=1).astype(jnp.int32)
    experts = route[2:4].astype(jnp.int32)
    first_row = jnp.sum(jnp.where(experts[..., None] == jnp.arange(N_EXPERTS), pad_start, 0), axis=-1)
    dest = first_row + route[4:6].astype(jnp.int32)
    dest2 = jnp.swapaxes(dest.reshape(EXPERT_TOP_K, t // ROUTE_TILE, ROUTE_TILE), 0, 1)
    dest2 = dest2.reshape(t // ROUTE_TILE, 1, EXPERT_TOP_K * ROUTE_TILE)
    x_disp = _moe_dispatch(xt, dest2, n_tiles * EXPERT_TILE)
    y_disp = _moe_experts(x_disp, tile_expert, n_used, w1, w3, w2, layer)
    return _moe_combine_ln(xt, y_disp, dest2, route, ln_g, ln_b).reshape(b, s, d)


def kernel(x, mem, mem_wk, mem_wv, ev_w_in, ev_conv_w, ev_conv_b, ev_cnorm_g, ev_cnorm_b, ev_cmp_pe_k, ev_cmp_w1_k, ev_cmp_w2_k, ev_cmp_pe_v, ev_cmp_w1_v, ev_cmp_w2_v, ev_w_out, od_w_in, od_conv_w, od_w_out, ln_mix_g, ln_mix_b, xa_wq, xa_wo, ln_xa_g, ln_xa_b, moe_wg, moe_bg, moe_we, moe_be, moe_w1, moe_w3, moe_w2, ln_ffn_g, ln_ffn_b):
    b, s, d = x.shape
    mem_k, mem_v = _mem_kv(mem, mem_wk, mem_wv)
    for layer in range(DEPTH):
        i = layer // 2
        if layer % 2 == 0:
            a, kv_in, k_sel, k_win, q_t, v_sel_t, v_win_t, gate_t = _even_in_proj(x, ev_w_in[i])
            a = _conformer_conv(a, ev_conv_w[i], ev_conv_b[i], ev_cnorm_g[i], ev_cnorm_b[i])
            k_cmp, v_cmp_t = _compress_kv(kv_in, ev_cmp_pe_k[i], ev_cmp_w1_k[i], ev_cmp_w2_k[i], ev_cmp_pe_v[i], ev_cmp_w1_v[i], ev_cmp_w2_v[i])
            o = _nsa_attention(q_t, gate_t, k_cmp, v_cmp_t, k_sel, v_sel_t, k_win, v_win_t)
            x = _proj_residual_ln(a.reshape(b * s, -1), o.reshape(b * s, -1), x.reshape(b * s, d), ev_w_out[i], ln_mix_g[layer], ln_mix_b[layer]).reshape(b, s, d)
        else:
            x = _odd_mixer_sublayer(x, od_w_in[i], od_conv_w[i], od_w_out[i], ln_mix_g[layer], ln_mix_b[layer])
        x = _xattn_sublayer(x, mem_k, mem_v, xa_wq[layer], xa_wo[layer], ln_xa_g[layer], ln_xa_b[layer])
        x = _moe_sublayer(x, moe_wg[layer], moe_bg[layer], moe_we[layer], moe_be[layer], moe_w1, moe_w3, moe_w2, layer, ln_ffn_g[layer], ln_ffn_b[layer])
    return x
```

```python
import functools

import numpy as np
import jax
import jax.numpy as jnp
from jax import lax
from jax.experimental import pallas as pl
from jax.experimental.pallas import tpu as pltpu

D_MODEL = 1024
DEPTH = 2
CONV_CH = D_MODEL // 2
CONV_WIDTH = 31
NSA_HEADS = 8
NSA_KV_HEADS = 2
HEAD_DIM = (D_MODEL // 2) // NSA_HEADS
CMP_BLOCK = 32
CMP_STRIDE = 16
SEL_BLOCK = 64
SEL_TOP_N = 16
WINDOW = 512
Q_BLOCK = 128
FORCE_SCORE = 1e4
SHORT_CONV_WIDTH = 3
XA_HEADS = 4
XA_HEAD_DIM = D_MODEL // XA_HEADS
N_GROUPS = 4
EXPERTS_PER_GROUP = 8
N_EXPERTS = N_GROUPS * EXPERTS_PER_GROUP
EXPERT_TOP_K = 2
DN_ALPHA = (2 * DEPTH) ** 0.25
LN_EPS = 1e-5
NEG_INF = -1e30
KV_COLS = NSA_KV_HEADS * HEAD_DIM
QCOLS = NSA_HEADS * HEAD_DIM
GATE_ROWS = 3 * NSA_HEADS

LANES = 128
SUBLANES = 8
HEADS_PER_KV = NSA_HEADS // NSA_KV_HEADS
QL = Q_BLOCK * HEADS_PER_KV
SEL_TILE = 512
WIN_SPAN = WINDOW + Q_BLOCK
BLOCKS_PER_TILE = SEL_TILE // SEL_BLOCK
SEQ_TILE = 512
HALO = 32
ODD_HALO = 8
ROUTE_TILE = 512
EXPERT_TILE = 512
ROUTE_ROWS = 8
VMEM_LIMIT = 56 * 1024 * 1024

_NT = (((1,), (1,)), ((), ()))


def _ln_rows(y, g, b):
    mu = jnp.mean(y, axis=-1, keepdims=True)
    yc = y - mu
    var = jnp.mean(yc * yc, axis=-1, keepdims=True)
    return yc * lax.rsqrt(var + LN_EPS) * g + b


def _even_in_kernel(x_ref, wa_ref, wkv_ref, wk2_ref, wqT_ref, wvT_ref, wgT_ref,
                    a_ref, kv_ref, ks_ref, kw_ref, qT_ref, vsT_ref, vwT_ref, gT_ref):
    f32 = jnp.float32
    bf16 = jnp.bfloat16
    xb = x_ref[0].astype(bf16)
    av = jnp.dot(xb, wa_ref[...], preferred_element_type=f32)
    a_ref[0] = av[:, :CONV_CH] * jax.nn.sigmoid(av[:, CONV_CH:])
    kv_ref[0] = jnp.dot(xb, wkv_ref[...], preferred_element_type=f32)
    k2 = jnp.dot(xb, wk2_ref[...], preferred_element_type=f32)
    ks_ref[0] = k2[:, :KV_COLS].astype(bf16)
    kw_ref[0] = k2[:, KV_COLS:].astype(bf16)
    qT_ref[0] = lax.dot_general(wqT_ref[...], xb, _NT, preferred_element_type=f32).astype(bf16)
    vT = lax.dot_general(wvT_ref[...], xb, _NT, preferred_element_type=f32).astype(bf16)
    vsT_ref[0, 0] = vT[:KV_COLS]
    for j in range(SEQ_TILE // Q_BLOCK):
        vwT_ref[0, j] = vT[KV_COLS:, j * Q_BLOCK:(j + 1) * Q_BLOCK]
    gT_ref[0] = lax.dot_general(wgT_ref[...], xb, _NT, preferred_element_type=f32)


def _even_in_proj(x, w_in):
    b, s, d = x.shape
    bf16 = jnp.bfloat16
    c = np.cumsum((0, CONV_CH, CONV_CH, QCOLS, KV_COLS, KV_COLS, KV_COLS, KV_COLS, KV_COLS, KV_COLS, GATE_ROWS))
    col = lambda i, j: w_in[:, c[i]:c[j]]
    wa = col(0, 2).astype(bf16)
    wq_t = col(2, 3).T.astype(bf16)
    wkv = col(3, 5).astype(bf16)
    wk2 = jnp.concatenate([col(5, 6), col(7, 8)], axis=1).astype(bf16)
    wv_t = jnp.concatenate([col(6, 7), col(8, 9)], axis=1).T.astype(bf16)
    wg_t = col(9, 10).T.astype(bf16)
    ts = SEQ_TILE
    full = lambda shape: pl.BlockSpec(shape, lambda i, j: (0,) * len(shape))
    return pl.pallas_call(
        _even_in_kernel,
        name="even_in_proj",
        grid=(b, s // ts),
        in_specs=[pl.BlockSpec((1, ts, d), lambda i, j: (i, j, 0)),
                  full(wa.shape), full(wkv.shape), full(wk2.shape), full(wq_t.shape), full(wv_t.shape), full(wg_t.shape)],
        out_specs=[pl.BlockSpec((1, ts, CONV_CH), lambda i, j: (i, j, 0)),
                   pl.BlockSpec((1, ts, 2 * KV_COLS), lambda i, j: (i, j, 0)),
                   pl.BlockSpec((1, ts, KV_COLS), lambda i, j: (i, j, 0)),
                   pl.BlockSpec((1, ts, KV_COLS), lambda i, j: (i, j, 0)),
                   pl.BlockSpec((1, QCOLS, ts), lambda i, j: (i, 0, j)),
                   pl.BlockSpec((1, 1, KV_COLS, ts), lambda i, j: (i, j, 0, 0)),
                   pl.BlockSpec((1, ts // Q_BLOCK, KV_COLS, Q_BLOCK), lambda i, j: (i, j, 0, 0)),
                   pl.BlockSpec((1, GATE_ROWS, ts), lambda i, j: (i, 0, j))],
        out_shape=[jax.ShapeDtypeStruct((b, s, CONV_CH), jnp.float32),
                   jax.ShapeDtypeStruct((b, s, 2 * KV_COLS), jnp.float32),
                   jax.ShapeDtypeStruct((b, s, KV_COLS), bf16),
                   jax.ShapeDtypeStruct((b, s, KV_COLS), bf16),
                   jax.ShapeDtypeStruct((b, QCOLS, s), bf16),
                   jax.ShapeDtypeStruct((b, s // SEL_TILE, KV_COLS, SEL_TILE), bf16),
                   jax.ShapeDtypeStruct((b, s // Q_BLOCK, KV_COLS, Q_BLOCK), bf16),
                   jax.ShapeDtypeStruct((b, GATE_ROWS, s), jnp.float32)],
        compiler_params=pltpu.CompilerParams(dimension_semantics=("arbitrary", "arbitrary"), vmem_limit_bytes=VMEM_LIMIT),
    )(x, wa, wkv, wk2, wq_t, wv_t, wg_t)


def _conv_kernel(cur_ref, halo_ref, w_ref, cb_ref, g_ref, b_ref, o_ref, ext_ref, win_ref):
    j = pl.program_id(1)
    ts = cur_ref.shape[1]
    halo = halo_ref[0]
    ext_ref[0:HALO, :] = jnp.where(j > 0, halo, jnp.zeros_like(halo))
    ext_ref[HALO:HALO + ts, :] = cur_ref[0]
    first = HALO - (CONV_WIDTH - 1)
    acc = jnp.zeros((ts, CONV_CH), jnp.float32)
    for p in range(SUBLANES):
        n_a = len(range(p, CONV_WIDTH, SUBLANES))
        rows = ts + SUBLANES * (n_a - 1)
        win_ref[0:rows, :] = ext_ref[first + p:first + p + rows, :]
        for a in range(n_a):
            k = SUBLANES * a + p
            acc = acc + w_ref[k:k + 1, :] * win_ref[SUBLANES * a:SUBLANES * a + ts, :]
    y = _ln_rows(acc + cb_ref[...], g_ref[...], b_ref[...])
    o_ref[0] = (y * jax.nn.sigmoid(y)).astype(o_ref.dtype)


def _conformer_conv(a, conv_w, conv_b, cn_g, cn_b):
    b, s, c = a.shape
    ts = SEQ_TILE
    per = ts // HALO
    row = lambda v: v.reshape(1, c)
    return pl.pallas_call(
        _conv_kernel,
        name="conformer_conv",
        grid=(b, s // ts),
        in_specs=[pl.BlockSpec((1, ts, c), lambda i, j: (i, j, 0)),
                  pl.BlockSpec((1, HALO, c), lambda i, j: (i, jnp.maximum(j * per - 1, 0), 0)),
                  pl.BlockSpec((CONV_WIDTH, c), lambda i, j: (0, 0)),
                  pl.BlockSpec((1, c), lambda i, j: (0, 0)),
                  pl.BlockSpec((1, c), lambda i, j: (0, 0)),
                  pl.BlockSpec((1, c), lambda i, j: (0, 0))],
        out_specs=pl.BlockSpec((1, ts, c), lambda i, j: (i, j, 0)),
        out_shape=jax.ShapeDtypeStruct((b, s, c), jnp.bfloat16),
        scratch_shapes=[pltpu.VMEM((HALO + ts, c), jnp.float32), pltpu.VMEM((HALO + ts, c), jnp.float32)],
        compiler_params=pltpu.CompilerParams(dimension_semantics=("arbitrary", "arbitrary")),
    )(a, a, conv_w, row(conv_b), row(cn_g), row(cn_b))


def _compress_kernel(r_ref, pe_ref, w1_ref, w2_ref, o_ref):
    f32 = jnp.float32
    bf16 = jnp.bfloat16
    r = r_ref[0, 0]
    half = r.shape[1]
    ha = jnp.dot((r + pe_ref[0, 0:1, :]).astype(bf16), w1_ref[0, :half, :], preferred_element_type=f32)
    hb = jnp.dot((r + pe_ref[0, 1:2, :]).astype(bf16), w1_ref[0, half:, :], preferred_element_type=f32)
    hb_next = jnp.concatenate([hb[1:], jnp.zeros_like(hb[0:1])], axis=0)
    hid = jax.nn.gelu(ha + hb_next).astype(bf16)
    o_ref[0, 0] = jnp.dot(hid, w2_ref[0], preferred_element_type=f32).astype(bf16)


def _compress_kv(kv_in, pe_k, w1_k, w2_k, pe_v, w1_v, w2_v):
    b, s, _ = kv_in.shape
    bf16 = jnp.bfloat16
    nch = s // CMP_STRIDE
    g = NSA_KV_HEADS
    r = kv_in.reshape(b, nch, CMP_STRIDE, 2 * g, HEAD_DIM)
    r = jnp.transpose(r, (0, 3, 1, 2, 4)).reshape(b, 2 * g, nch, CMP_STRIDE * HEAD_DIM)
    pe = jnp.stack([pe_k, pe_v]).reshape(2, 2, CMP_STRIDE * HEAD_DIM)
    w1 = jnp.stack([w1_k, w1_v]).astype(bf16)
    w2 = jnp.stack([w2_k, w2_v]).astype(bf16)
    out = pl.pallas_call(
        _compress_kernel,
        name="compress_kv",
        grid=(b, 2 * g),
        in_specs=[pl.BlockSpec((1, 1, nch, r.shape[3]), lambda i, j: (i, j, 0, 0)),
                  pl.BlockSpec((1, 2, pe.shape[2]), lambda i, j: (j // g, 0, 0)),
                  pl.BlockSpec((1,) + w1.shape[1:], lambda i, j: (j // g, 0, 0)),
                  pl.BlockSpec((1,) + w2.shape[1:], lambda i, j: (j // g, 0, 0))],
        out_specs=pl.BlockSpec((1, 1, nch, HEAD_DIM), lambda i, j: (i, j, 0, 0)),
        out_shape=jax.ShapeDtypeStruct((b, 2 * g, nch, HEAD_DIM), bf16),
        compiler_params=pltpu.CompilerParams(dimension_semantics=("arbitrary", "arbitrary")),
    )(r, pe, w1, w2)
    kc = jnp.transpose(out[:, :g], (0, 2, 1, 3)).reshape(b, nch, KV_COLS)
    vcT = jnp.transpose(out[:, g:], (0, 1, 3, 2)).reshape(b, KV_COLS, nch)
    return kc, vcT


def _proj_ln_kernel(a_ref, o_ref, x_ref, w_ref, g_ref, b_ref, y_ref):
    lhs = jnp.concatenate([a_ref[...], o_ref[...]], axis=1)
    mix = jnp.dot(lhs, w_ref[...], preferred_element_type=jnp.float32)
    y_ref[...] = _ln_rows(DN_ALPHA * x_ref[...] + mix, g_ref[...], b_ref[...])


def _proj_residual_ln(a, o, x, w, ln_g, ln_b):
    t, d = x.shape
    tm = SEQ_TILE
    return pl.pallas_call(
        _proj_ln_kernel,
        name="proj_residual_ln",
        grid=(t // tm,),
        in_specs=[pl.BlockSpec((tm, a.shape[1]), lambda i: (i, 0)),
                  pl.BlockSpec((tm, o.shape[1]), lambda i: (i, 0)),
                  pl.BlockSpec((tm, d), lambda i: (i, 0)),
                  pl.BlockSpec(w.shape, lambda i: (0, 0)),
                  pl.BlockSpec((1, d), lambda i: (0, 0)),
                  pl.BlockSpec((1, d), lambda i: (0, 0))],
        out_specs=pl.BlockSpec((tm, d), lambda i: (i, 0)),
        out_shape=jax.ShapeDtypeStruct((t, d), jnp.float32),
        compiler_params=pltpu.CompilerParams(dimension_semantics=("arbitrary",), vmem_limit_bytes=VMEM_LIMIT),
    )(a, o, x, w.astype(jnp.bfloat16), ln_g.reshape(1, d), ln_b.reshape(1, d))


def _mem_kv_kernel(m_ref, wk_ref, wv_ref, k_ref, v_ref):
    mb = m_ref[...].astype(jnp.bfloat16)
    k_ref[...] = jnp.dot(mb, wk_ref[...], preferred_element_type=jnp.float32).astype(jnp.bfloat16)
    v_ref[...] = jnp.dot(mb, wv_ref[...], preferred_element_type=jnp.float32).astype(jnp.bfloat16)


def _mem_kv(mem, wk, wv):
    b, m, d = mem.shape
    bf16 = jnp.bfloat16
    k, v = pl.pallas_call(
        _mem_kv_kernel,
        name="mem_kv",
        grid=(b,),
        in_specs=[pl.BlockSpec((m, d), lambda i: (i, 0)), pl.BlockSpec((d, d), lambda i: (0, 0)), pl.BlockSpec((d, d), lambda i: (0, 0))],
        out_specs=[pl.BlockSpec((m, d), lambda i: (i, 0)), pl.BlockSpec((m, d), lambda i: (i, 0))],
        out_shape=[jax.ShapeDtypeStruct((b * m, d), bf16), jax.ShapeDtypeStruct((b * m, d), bf16)],
        compiler_params=pltpu.CompilerParams(dimension_semantics=("arbitrary",), vmem_limit_bytes=VMEM_LIMIT),
    )(mem.reshape(b * m, d), wk.astype(bf16), wv.astype(bf16))
    return k.reshape(b, m, d), v.reshape(b, m, d)


def _xattn_kernel(x_ref, k_ref, v_ref, wq_ref, wo_ref, g_ref, b_ref, y_ref):
    f32 = jnp.float32
    bf16 = jnp.bfloat16
    x = x_ref[0]
    q = (jnp.dot(x.astype(bf16), wq_ref[...], preferred_element_type=f32).astype(bf16)
         * jnp.asarray(XA_HEAD_DIM ** -0.5, bf16))
    heads = []
    for h in range(XA_HEADS):
        cols = slice(h * XA_HEAD_DIM, (h + 1) * XA_HEAD_DIM)
        s = lax.dot_general(q[:, cols], k_ref[0, :, cols], _NT, preferred_element_type=f32)
        m = jnp.max(s, axis=1, keepdims=True)
        p = jnp.exp(s - m)
        p = p / jnp.sum(p, axis=1, keepdims=True)
        heads.append(jnp.dot(p.astype(bf16), v_ref[0, :, cols], preferred_element_type=f32).astype(bf16))
    att = jnp.concatenate(heads, axis=1)
    out = jnp.dot(att, wo_ref[...], preferred_element_type=f32)
    y_ref[0] = _ln_rows(DN_ALPHA * x + out, g_ref[...], b_ref[...])


def _xattn_sublayer(x, mem_k, mem_v, wq, wo, ln_g, ln_b):
    b, s, d = x.shape
    m = mem_k.shape[1]
    ts = SEQ_TILE
    bf16 = jnp.bfloat16
    return pl.pallas_call(
        _xattn_kernel,
        name="xattn_sublayer",
        grid=(b, s // ts),
        in_specs=[pl.BlockSpec((1, ts, d), lambda i, j: (i, j, 0)),
                  pl.BlockSpec((1, m, d), lambda i, j: (i, 0, 0)),
                  pl.BlockSpec((1, m, d), lambda i, j: (i, 0, 0)),
                  pl.BlockSpec((d, d), lambda i, j: (0, 0)),
                  pl.BlockSpec((d, d), lambda i, j: (0, 0)),
                  pl.BlockSpec((1, d), lambda i, j: (0, 0)),
                  pl.BlockSpec((1, d), lambda i, j: (0, 0))],
        out_specs=pl.BlockSpec((1, ts, d), lambda i, j: (i, j, 0)),
        out_shape=jax.ShapeDtypeStruct((b, s, d), jnp.float32),
        compiler_params=pltpu.CompilerParams(dimension_semantics=("arbitrary", "arbitrary"), vmem_limit_bytes=VMEM_LIMIT),
    )(x, mem_k, mem_v, wq.astype(bf16), wo.astype(bf16), ln_g.reshape(1, d), ln_b.reshape(1, d))


def _odd_kernel(x_ref, halo_ref, wb_ref, wc_ref, wh_ref, cw_ref, wo_ref, g_ref, b_ref, y_ref):
    f32 = jnp.float32
    bf16 = jnp.bfloat16
    j = pl.program_id(1)
    ts = x_ref.shape[1]
    x = x_ref[0]
    xe = jnp.concatenate([halo_ref[0], x], axis=0).astype(bf16)
    u = (jnp.dot(xe, wc_ref[...], preferred_element_type=f32) * jnp.dot(xe, wh_ref[...], preferred_element_type=f32))
    row = lax.broadcasted_iota(jnp.int32, (ODD_HALO + ts, 1), 0)
    u = jnp.where((row >= ODD_HALO) | (j > 0), u, 0.0)
    conv = jnp.zeros((ts, u.shape[1]), f32)
    for k in range(SHORT_CONV_WIDTH):
        off = ODD_HALO - (SHORT_CONV_WIDTH - 1) + k
        conv = conv + cw_ref[k:k + 1, :] * u[off:off + ts, :]
    gate_b = jnp.dot(xe[ODD_HALO:], wb_ref[...], preferred_element_type=f32)
    mix = jnp.dot((gate_b * conv).astype(bf16), wo_ref[...], preferred_element_type=f32)
    y_ref[0] = _ln_rows(DN_ALPHA * x + mix, g_ref[...], b_ref[...])


def _odd_mixer_sublayer(x, w_in, conv_w, w_out, ln_g, ln_b):
    b, s, d = x.shape
    ts = SEQ_TILE
    per = ts // ODD_HALO
    bf16 = jnp.bfloat16
    wb, wc, wh = (w_in[:, i * d:(i + 1) * d].astype(bf16) for i in range(3))
    full = lambda shape: pl.BlockSpec(shape, lambda i, j: (0,) * len(shape))
    return pl.pallas_call(
        _odd_kernel,
        name="odd_mixer_sublayer",
        grid=(b, s // ts),
        in_specs=[pl.BlockSpec((1, ts, d), lambda i, j: (i, j, 0)),
                  pl.BlockSpec((1, ODD_HALO, d), lambda i, j: (i, jnp.maximum(j * per - 1, 0), 0)),
                  full((d, d)), full((d, d)), full((d, d)), full(conv_w.shape), full((d, d)), full((1, d)), full((1, d))],
        out_specs=pl.BlockSpec((1, ts, d), lambda i, j: (i, j, 0)),
        out_shape=jax.ShapeDtypeStruct((b, s, d), jnp.float32),
        compiler_params=pltpu.CompilerParams(dimension_semantics=("arbitrary", "arbitrary"), vmem_limit_bytes=VMEM_LIMIT),
    )(x, x, wb, wc, wh, conv_w, w_out.astype(bf16), ln_g.reshape(1, d), ln_b.reshape(1, d))


def _cmp_to_sel_matrix(n_cmp, n_sel):
    c0 = np.arange(n_cmp) * CMP_STRIDE
    s0 = np.arange(n_sel) * SEL_BLOCK
    ov = np.minimum(c0[:, None] + CMP_BLOCK, s0[None, :] + SEL_BLOCK) - np.maximum(c0[:, None], s0[None, :])
    return (np.clip(ov, 0, None) / CMP_BLOCK).astype(np.float32)


def _nsa_kernel(qT_ref, gT_ref, kc_ref, vcT_ref, mselT_ref, ks_ref, vsT_ref, kw_ref, vwT_ref, o_ref,
                score_ref, sel_ref, *, n_sel):
    c = pl.program_id(1)
    q0 = c * Q_BLOCK
    f32 = jnp.float32
    bf16 = jnp.bfloat16
    n_cmp_pad = kc_ref.shape[1]

    lane_q = lax.broadcasted_iota(jnp.int32, (1, QL), 1) % Q_BLOCK
    t_row = q0 + lane_q
    t_row_q = q0 + lax.broadcasted_iota(jnp.int32, (1, Q_BLOCK), 1)
    cur_q = t_row_q // SEL_BLOCK

    top_n = min(SEL_TOP_N, n_sel)
    m_iota = lax.broadcasted_iota(jnp.int32, (n_sel, Q_BLOCK), 0)
    forced = (m_iota == 0) | (m_iota == cur_q) | (m_iota == cur_q - 1)
    valid = m_iota <= cur_q
    n_comp = jnp.minimum(q0 // SEL_BLOCK + 2, n_sel)

    q_g, o_c = [], []
    for g in range(NSA_KV_HEADS):
        pieces = []
        for hg in range(HEADS_PER_KV):
            h = g * HEADS_PER_KV + hg
            qh = qT_ref[0, h * HEAD_DIM:(h + 1) * HEAD_DIM, :] * jnp.asarray(HEAD_DIM ** -0.5, bf16)
            z = jnp.zeros_like(qh)
            pieces.append(jnp.concatenate([qh, z] if g == 0 else [z, qh], axis=0))
        qTp = jnp.concatenate(pieces, axis=1)
        q_g.append(qTp)
        rows = slice(g * HEAD_DIM, (g + 1) * HEAD_DIM)

        s_c = jnp.dot(kc_ref[0], qTp, preferred_element_type=f32)
        n_iota = lax.broadcasted_iota(jnp.int32, (n_cmp_pad, QL), 0)
        mask_c = (n_iota * CMP_STRIDE + (CMP_BLOCK - 1)) <= t_row
        s_c = jnp.where(mask_c, s_c, NEG_INF)
        m_c = jnp.max(s_c, axis=0, keepdims=True)
        p_c = jnp.where(mask_c, jnp.exp(s_c - m_c), 0.0)
        l_c = jnp.sum(p_c, axis=0, keepdims=True)
        p_c = p_c * jnp.where(l_c > 0.0, 1.0 / l_c, 0.0)
        p_cb = p_c.astype(bf16)
        o_c.append(jnp.dot(vcT_ref[0], p_cb, preferred_element_type=f32)[rows])
        imp4 = jnp.dot(mselT_ref[...], p_cb, preferred_element_type=f32)
        imp = imp4[:, 0:Q_BLOCK]
        for hg in range(1, HEADS_PER_KV):
            imp = imp + imp4[:, hg * Q_BLOCK:(hg + 1) * Q_BLOCK]

        score = jnp.where(valid, jnp.where(forced, FORCE_SCORE, imp), NEG_INF)
        score_ref[...] = score

        def rank_body(mp, rank, score=score):
            row = score_ref[pl.ds(mp, 1), :]
            beats = (row > score) | ((row == score) & (mp < m_iota))
            return rank + beats.astype(jnp.int32)

        rank = lax.fori_loop(0, n_comp, rank_body, jnp.zeros((n_sel, Q_BLOCK), jnp.int32))
        sel_ref[g] = jnp.where((rank < top_n) & valid, 0.0, NEG_INF)

    def sel_tile(j, carry, causal):
        k_tile = ks_ref[0, j]
        v_tile = vsT_ref[0, j]
        blk0 = pl.multiple_of(j * BLOCKS_PER_TILE, BLOCKS_PER_TILE)
        out = []
        for g in range(NSA_KV_HEADS):
            m_i, l_i, acc = carry[g]
            s = jnp.dot(k_tile, q_g[g], preferred_element_type=f32)
            selrows = sel_ref[g, pl.ds(blk0, BLOCKS_PER_TILE), :]
            bias = jnp.concatenate(
                [jnp.broadcast_to(selrows[r:r + 1, :], (SEL_BLOCK, Q_BLOCK)) for r in range(BLOCKS_PER_TILE)], axis=0)
            if causal:
                key = j * SEL_TILE + lax.broadcasted_iota(jnp.int32, (SEL_TILE, Q_BLOCK), 0)
                bias = jnp.where(key <= t_row_q, bias, NEG_INF)
            s = s + jnp.concatenate([bias] * HEADS_PER_KV, axis=1)
            m_new = jnp.maximum(m_i, jnp.max(s, axis=0, keepdims=True))
            alpha = jnp.exp(m_i - m_new)
            p = jnp.exp(s - m_new)
            l_new = alpha * l_i + jnp.sum(p, axis=0, keepdims=True)
            pv = jnp.dot(v_tile, p.astype(bf16), preferred_element_type=f32)
            out.append((m_new, l_new, alpha * acc + pv))
        return tuple(out)

    n_full = q0 // SEL_TILE
    init = tuple((jnp.full((1, QL), NEG_INF, f32), jnp.zeros((1, QL), f32), jnp.zeros((KV_COLS, QL), f32))
                 for _ in range(NSA_KV_HEADS))
    carry = lax.fori_loop(0, n_full, functools.partial(sel_tile, causal=False), init)
    carry = sel_tile(n_full, carry, causal=True)

    start = pl.multiple_of(jnp.maximum(q0 - WINDOW, 0), Q_BLOCK)
    j0 = start // Q_BLOCK
    k_win = kw_ref[0, pl.ds(start, WIN_SPAN), :]
    key_w = start + lax.broadcasted_iota(jnp.int32, (WIN_SPAN, Q_BLOCK), 0)
    bias_w = jnp.where(key_w <= t_row_q, 0.0, NEG_INF)
    bias_w = jnp.where(key_w > t_row_q - WINDOW, bias_w, NEG_INF)
    bias_w = jnp.concatenate([bias_w] * HEADS_PER_KV, axis=1)
    for g in range(NSA_KV_HEADS):
        rows = slice(g * HEAD_DIM, (g + 1) * HEAD_DIM)
        m_s, l_s, acc_s = carry[g]
        o_s = acc_s[rows] * (1.0 / l_s)
        s_w = jnp.dot(k_win, q_g[g], preferred_element_type=f32) + bias_w
        m_w = jnp.max(s_w, axis=0, keepdims=True)
        p_w = jnp.exp(s_w - m_w)
        l_w = jnp.sum(p_w, axis=0, keepdims=True)
        p_wb = p_w.astype(bf16)
        acc_w = jnp.zeros((KV_COLS, QL), f32)
        for i in range(WIN_SPAN // Q_BLOCK):
            acc_w = acc_w + jnp.dot(vwT_ref[0, j0 + i], p_wb[i * Q_BLOCK:(i + 1) * Q_BLOCK, :], preferred_element_type=f32)
        o_w = acc_w[rows] * (1.0 / l_w)

        for pair in range(HEADS_PER_KV // 2):
            halves = []
            for hg in (2 * pair, 2 * pair + 1):
                h = g * HEADS_PER_KV + hg
                lanes = slice(hg * Q_BLOCK, (hg + 1) * Q_BLOCK)
                gate = jax.nn.sigmoid(gT_ref[0, 3 * h:3 * h + 3, :])
                halves.append(gate[0:1] * o_c[g][:, lanes] + gate[1:2] * o_s[:, lanes] + gate[2:3] * o_w[:, lanes])
            both = jnp.concatenate(halves, axis=0)
            col0 = (g * HEADS_PER_KV + 2 * pair) * HEAD_DIM
            o_ref[0, :, col0:col0 + 2 * HEAD_DIM] = both.T.astype(o_ref.dtype)


def _nsa_attention(qT, gT, kc, vcT, ks, vsT, kw, vwT):
    b, hd, s = qT.shape
    n_sel = s // SEL_BLOCK
    nc = kc.shape[1]
    n_cmp = (s - CMP_BLOCK) // CMP_STRIDE + 1
    mselT = jnp.asarray(np.pad(_cmp_to_sel_matrix(n_cmp, n_sel).T, ((0, 0), (0, nc - n_cmp))), jnp.bfloat16)
    ks4 = ks.reshape(b, s // SEL_TILE, SEL_TILE, KV_COLS)
    return pl.pallas_call(
        functools.partial(_nsa_kernel, n_sel=n_sel),
        name="nsa_attention",
        grid=(b, s // Q_BLOCK),
        in_specs=[
            pl.BlockSpec((1, hd, Q_BLOCK), lambda i, c: (i, 0, c)),
            pl.BlockSpec((1, 3 * NSA_HEADS, Q_BLOCK), lambda i, c: (i, 0, c)),
            pl.BlockSpec((1, nc, KV_COLS), lambda i, c: (i, 0, 0)),
            pl.BlockSpec((1, KV_COLS, nc), lambda i, c: (i, 0, 0)),
            pl.BlockSpec((n_sel, nc), lambda i, c: (0, 0)),
            pl.BlockSpec((1, s // SEL_TILE, SEL_TILE, KV_COLS), lambda i, c: (i, 0, 0, 0)),
            pl.BlockSpec((1, s // SEL_TILE, KV_COLS, SEL_TILE), lambda i, c: (i, 0, 0, 0)),
            pl.BlockSpec((1, s, KV_COLS), lambda i, c: (i, 0, 0)),
            pl.BlockSpec((1, s // Q_BLOCK, KV_COLS, Q_BLOCK), lambda i, c: (i, 0, 0, 0)),
        ],
        out_specs=pl.BlockSpec((1, Q_BLOCK, hd), lambda i, c: (i, c, 0)),
        out_shape=jax.ShapeDtypeStruct((b, s, hd), jnp.bfloat16),
        scratch_shapes=[pltpu.VMEM((n_sel, Q_BLOCK), jnp.float32),
                        pltpu.VMEM((NSA_KV_HEADS, n_sel, Q_BLOCK), jnp.float32)],
        compiler_params=pltpu.CompilerParams(dimension_semantics=("arbitrary", "arbitrary")),
    )(qT, gT, kc, vcT, mselT, ks4, vsT, kw, vwT)


def _router_kernel(x_ref, wT_ref, b_ref, tri_ref, route_ref, cnt_ref, run_ref):
    i = pl.program_id(0)
    f32 = jnp.float32

    @pl.when(i == 0)
    def _():
        run_ref[...] = jnp.zeros_like(run_ref)

    tm = x_ref.shape[0]
    logits = lax.dot_general(wT_ref[...], x_ref[...].astype(jnp.bfloat16), _NT, preferred_element_type=f32) + b_ref[...]
    sub = lax.broadcasted_iota(jnp.int32, (LANES, tm), 0)
    is_g = sub < N_GROUPS
    gl = jnp.where(is_g, logits, NEG_INF)
    g_max = jnp.max(gl, axis=0, keepdims=True)
    g_star = jnp.min(jnp.where(gl == g_max, sub, LANES), axis=0, keepdims=True)
    p_group = 1.0 / jnp.sum(jnp.where(is_g, jnp.exp(gl - g_max), 0.0), axis=0, keepdims=True)
    lo = N_GROUPS + g_star * EXPERTS_PER_GROUP
    in_grp = (sub >= lo) & (sub < lo + EXPERTS_PER_GROUP)
    el = jnp.where(in_grp, logits, NEG_INF)
    v1 = jnp.max(el, axis=0, keepdims=True)
    i1 = jnp.min(jnp.where(el == v1, sub, LANES), axis=0, keepdims=True)
    el2 = jnp.where(sub == i1, NEG_INF, el)
    v2 = jnp.max(el2, axis=0, keepdims=True)
    i2 = jnp.min(jnp.where(el2 == v2, sub, LANES), axis=0, keepdims=True)
    e21 = jnp.exp(v2 - v1)
    gate1 = p_group * (1.0 / (1.0 + e21))
    gate2 = p_group * (e21 / (1.0 + e21))
    oh1 = (sub == i1).astype(f32)
    oh2 = (sub == i2).astype(f32)
    both = oh1 + oh2
    before = jnp.dot(both.astype(jnp.bfloat16), tri_ref[...], preferred_element_type=f32) + run_ref[...]
    rank1 = jnp.sum(oh1 * before, axis=0, keepdims=True)
    rank2 = jnp.sum(oh2 * before, axis=0, keepdims=True)
    run_ref[...] = run_ref[...] + jnp.sum(both, axis=1, keepdims=True)
    cnt_ref[...] = run_ref[...]
    zero = jnp.zeros_like(gate1)
    route_ref[...] = jnp.concatenate(
        [gate1, gate2, (i1 - N_GROUPS).astype(f32), (i2 - N_GROUPS).astype(f32), rank1, rank2, zero, zero], axis=0)


def _moe_route(xt, wg, bg, we, be):
    t, d = xt.shape
    pad = LANES - N_GROUPS - N_EXPERTS
    w_t = jnp.concatenate([wg, we.reshape(d, N_EXPERTS), jnp.zeros((d, pad), wg.dtype)], axis=1).T
    bias = jnp.concatenate([bg, be.reshape(N_EXPERTS), jnp.zeros((pad,), bg.dtype)])[:, None]
    tri = jnp.asarray(np.triu(np.ones((ROUTE_TILE, ROUTE_TILE), np.float32), 1), jnp.bfloat16)
    return pl.pallas_call(
        _router_kernel,
        name="moe_router",
        grid=(t // ROUTE_TILE,),
        in_specs=[pl.BlockSpec((ROUTE_TILE, d), lambda i: (i, 0)),
                  pl.BlockSpec((LANES, d), lambda i: (0, 0)),
                  pl.BlockSpec((LANES, 1), lambda i: (0, 0)),
                  pl.BlockSpec((ROUTE_TILE, ROUTE_TILE), lambda i: (0, 0))],
        out_specs=[pl.BlockSpec((ROUTE_ROWS, ROUTE_TILE), lambda i: (0, i)),
                   pl.BlockSpec((LANES, 1), lambda i: (0, 0))],
        out_shape=[jax.ShapeDtypeStruct((ROUTE_ROWS, t), jnp.float32), jax.ShapeDtypeStruct((LANES, 1), jnp.float32)],
        scratch_shapes=[pltpu.VMEM((LANES, 1), jnp.float32)],
        compiler_params=pltpu.CompilerParams(dimension_semantics=("arbitrary",)),
    )(xt, w_t.astype(jnp.bfloat16), bias, tri)


def _dispatch_kernel(dest_ref, x_ref, zero_hbm, xd_hbm, sem):
    del zero_hbm
    tm = x_ref.shape[0]

    def body(r, c):
        for k in range(EXPERT_TOP_K):
            pltpu.make_async_copy(
                x_ref.at[pl.ds(r, 1)], xd_hbm.at[pl.ds(dest_ref[0, 0, k * tm + r], 1)], sem).start(priority=k)
        return c

    lax.fori_loop(0, tm, body, 0, unroll=8)
    for k in range(EXPERT_TOP_K):
        pltpu.make_async_copy(x_ref, xd_hbm.at[pl.ds(0, tm)], sem).wait()


def _moe_dispatch(xt, dest2, n_rows):
    t, d = xt.shape
    return pl.pallas_call(
        _dispatch_kernel,
        name="moe_dispatch",
        grid=(t // ROUTE_TILE,),
        in_specs=[pl.BlockSpec((1, 1, EXPERT_TOP_K * ROUTE_TILE), lambda i: (i, 0, 0), memory_space=pltpu.SMEM),
                  pl.BlockSpec((ROUTE_TILE, d), lambda i: (i, 0)),
                  pl.BlockSpec(memory_space=pl.ANY)],
        out_specs=pl.BlockSpec(memory_space=pl.ANY),
        out_shape=jax.ShapeDtypeStruct((n_rows, d), xt.dtype),
        scratch_shapes=[pltpu.SemaphoreType.DMA(())],
        input_output_aliases={2: 0},
        compiler_params=pltpu.CompilerParams(dimension_semantics=("arbitrary",), has_side_effects=True),
    )(dest2, xt, jnp.zeros((n_rows, d), xt.dtype))


def _expert_kernel(te_ref, nu_ref, xd_ref, w1_ref, w3_ref, w2_ref, y_ref, w1b, w3b, w2b):
    i = pl.program_id(0)
    used = i < nu_ref[0]
    new_expert = (i == 0) | (te_ref[i] != te_ref[jnp.maximum(i - 1, 0)])

    @pl.when(used & new_expert)
    def _():
        w1b[...] = w1_ref[0, 0].astype(jnp.bfloat16)
        w3b[...] = w3_ref[0, 0].astype(jnp.bfloat16)
        w2b[...] = w2_ref[0, 0].astype(jnp.bfloat16)

    @pl.when(used)
    def _():
        xb = xd_ref[...].astype(jnp.bfloat16)
        h1 = jnp.dot(xb, w1b[...], preferred_element_type=jnp.float32)
        h3 = jnp.dot(xb, w3b[...], preferred_element_type=jnp.float32)
        a = (h1 * jax.nn.sigmoid(h1) * h3).astype(jnp.bfloat16)
        y_ref[...] = jnp.dot(a, w2b[...], preferred_element_type=jnp.float32)

    @pl.when(jnp.logical_not(used))
    def _():
        y_ref[...] = jnp.zeros_like(y_ref)


def _moe_experts(x_disp, tile_expert, n_used, w1, w3, w2, layer):
    n_rows, d = x_disp.shape
    n_tiles = n_rows // EXPERT_TILE
    hid = w1.shape[3]

    def row_map(i, te, nu):
        return (i, 0)

    def w_map(i, te, nu):
        return (layer, te[i], 0, 0)

    grid_spec = pltpu.PrefetchScalarGridSpec(
        num_scalar_prefetch=2,
        grid=(n_tiles,),
        in_specs=[pl.BlockSpec((EXPERT_TILE, d), row_map),
                  pl.BlockSpec((1, 1, d, hid), w_map),
                  pl.BlockSpec((1, 1, d, hid), w_map),
                  pl.BlockSpec((1, 1, hid, d), w_map)],
        out_specs=pl.BlockSpec((EXPERT_TILE, d), row_map),
        scratch_shapes=[pltpu.VMEM((d, hid), jnp.bfloat16), pltpu.VMEM((d, hid), jnp.bfloat16),
                        pltpu.VMEM((hid, d), jnp.bfloat16)],
    )
    return pl.pallas_call(
        _expert_kernel,
        name="moe_experts",
        grid_spec=grid_spec,
        out_shape=jax.ShapeDtypeStruct((n_rows, d), jnp.float32),
        compiler_params=pltpu.CompilerParams(dimension_semantics=("arbitrary",), vmem_limit_bytes=VMEM_LIMIT),
    )(tile_expert, n_used, x_disp, w1, w3, w2)


def _combine_kernel(dest_ref, dest_next_ref, x_ref, route_ref, g_ref, b_ref, yd_hbm, o_ref, ybuf, sem):
    i = pl.program_id(0)
    n = pl.num_programs(0)
    tm = x_ref.shape[0]
    rows = EXPERT_TOP_K * tm

    def start_tile(dref, slot):
        def body(r, c):
            for k in range(EXPERT_TOP_K):
                row = k * tm + r
                pltpu.make_async_copy(
                    yd_hbm.at[pl.ds(dref[0, 0, row], 1)], ybuf.at[slot, pl.ds(row, 1)], sem.at[slot]).start(priority=k)
            return c
        lax.fori_loop(0, tm, body, 0, unroll=8)

    slot = i % 2

    @pl.when(i == 0)
    def _():
        start_tile(dest_ref, 0)

    @pl.when(i + 1 < n)
    def _():
        start_tile(dest_next_ref, 1 - slot)

    pltpu.make_async_copy(yd_hbm.at[pl.ds(0, rows)], ybuf.at[slot], sem.at[slot]).wait()
    y1 = ybuf[slot, 0:tm, :]
    y2 = ybuf[slot, tm:rows, :]
    route = route_ref[...].T
    y = DN_ALPHA * x_ref[...] + (y1 * route[:, 0:1] + y2 * route[:, 1:2])
    o_ref[...] = _ln_rows(y, g_ref[...], b_ref[...])


def _moe_combine_ln(xt, y_disp, dest2, route, ln_g, ln_b):
    t, d = xt.shape
    n = t // ROUTE_TILE
    rows = EXPERT_TOP_K * ROUTE_TILE
    return pl.pallas_call(
        _combine_kernel,
        name="moe_combine_ln",
        grid=(n,),
        in_specs=[pl.BlockSpec((1, 1, rows), lambda i: (i, 0, 0), memory_space=pltpu.SMEM),
                  pl.BlockSpec((1, 1, rows), lambda i: (jnp.minimum(i + 1, n - 1), 0, 0), memory_space=pltpu.SMEM),
                  pl.BlockSpec((ROUTE_TILE, d), lambda i: (i, 0)),
                  pl.BlockSpec((ROUTE_ROWS, ROUTE_TILE), lambda i: (0, i)),
                  pl.BlockSpec((1, d), lambda i: (0, 0)),
                  pl.BlockSpec((1, d), lambda i: (0, 0)),
                  pl.BlockSpec(memory_space=pl.ANY)],
        out_specs=pl.BlockSpec((ROUTE_TILE, d), lambda i: (i, 0)),
        out_shape=jax.ShapeDtypeStruct((t, d), jnp.float32),
        scratch_shapes=[pltpu.VMEM((2, rows, d), jnp.float32), pltpu.SemaphoreType.DMA((2,))],
        compiler_params=pltpu.CompilerParams(dimension_semantics=("arbitrary",), vmem_limit_bytes=VMEM_LIMIT),
    )(dest2, dest2, xt, route, ln_g.reshape(1, d), ln_b.reshape(1, d), y_disp)


def _moe_sublayer(x, wg, bg, we, be, w1, w3, w2, layer, ln_g, ln_b):
    b, s, d = x.shape
    t = b * s
    xt = x.reshape(t, d)
    route, cnt = _moe_route(xt, wg, bg, we, be)
    counts = cnt[N_GROUPS:N_GROUPS + N_EXPERTS, 0].astype(jnp.int32)
    n_tiles = (t * EXPERT_TOP_K) // EXPERT_TILE + N_EXPERTS
    tiles_per = (counts + EXPERT_TILE - 1) // EXPERT_TILE
    tile_end = jnp.cumsum(tiles_per)
    pad_start = (tile_end - tiles_per) * EXPERT_TILE
    n_used = tile_end[-1:].astype(jnp.int32)
    tile_ids = jnp.minimum(jnp.arange(n_tiles), n_used[0] - 1)
    tile_expert = jnp.sum(tile_ids[:, None] >= tile_end[None, :], axis=1).astype(jnp.int32)
    experts = route[2:4].astype(jnp.int32)
    first_row = jnp.sum(jnp.where(experts[..., None] == jnp.arange(N_EXPERTS), pad_start, 0), axis=-1)
    dest = first_row + route[4:6].astype(jnp.int32)
    dest2 = jnp.swapaxes(dest.reshape(EXPERT_TOP_K, t // ROUTE_TILE, ROUTE_TILE), 0, 1)
    dest2 = dest2.reshape(t // ROUTE_TILE, 1, EXPERT_TOP_K * ROUTE_TILE)
    x_disp = _moe_dispatch(xt, dest2, n_tiles * EXPERT_TILE)
    y_disp = _moe_experts(x_disp, tile_expert, n_used, w1, w3, w2, layer)
    return _moe_combine_ln(xt, y_disp, dest2, route, ln_g, ln_b).reshape(b, s, d)


def kernel(x, mem, mem_wk, mem_wv, ev_w_in, ev_conv_w, ev_conv_b, ev_cnorm_g, ev_cnorm_b, ev_cmp_pe_k, ev_cmp_w1_k, ev_cmp_w2_k, ev_cmp_pe_v, ev_cmp_w1_v, ev_cmp_w2_v, ev_w_out, od_w_in, od_conv_w, od_w_out, ln_mix_g, ln_mix_b, xa_wq, xa_wo, ln_xa_g, ln_xa_b, moe_wg, moe_bg, moe_we, moe_be, moe_w1, moe_w3, moe_w2, ln_ffn_g, ln_ffn_b):
    b, s, d = x.shape
    mem_k, mem_v = _mem_kv(mem, mem_wk, mem_wv)
    for layer in range(DEPTH):
        i = layer // 2
        if layer % 2 == 0:
            a, kv_in, k_sel, k_win, q_t, v_sel_t, v_win_t, gate_t = _even_in_proj(x, ev_w_in[i])
            a = _conformer_conv(a, ev_conv_w[i], ev_conv_b[i], ev_cnorm_g[i], ev_cnorm_b[i])
            k_cmp, v_cmp_t = _compress_kv(kv_in, ev_cmp_pe_k[i], ev_cmp_w1_k[i], ev_cmp_w2_k[i], ev_cmp_pe_v[i], ev_cmp_w1_v[i], ev_cmp_w2_v[i])
            o = _nsa_attention(q_t, gate_t, k_cmp, v_cmp_t, k_sel, v_sel_t, k_win, v_win_t)
            x = _proj_residual_ln(a.reshape(b * s, -1), o.reshape(b * s, -1), x.reshape(b * s, d), ev_w_out[i], ln_mix_g[layer], ln_mix_b[layer]).reshape(b, s, d)
        else:
            x = _odd_mixer_sublayer(x, od_w_in[i], od_conv_w[i], od_w_out[i], ln_mix_g[layer], ln_mix_b[layer])
        x = _xattn_sublayer(x, mem_k, mem_v, xa_wq[layer], xa_wo[layer], ln_xa_g[layer], ln_xa_b[layer])
        x = _moe_sublayer(x, moe_wg[layer], moe_bg[layer], moe_we[layer], moe_be[layer], moe_w1, moe_w3, moe_w2, layer, ln_ffn_g[layer], ln_ffn_b[layer])
    return x
```

```python
import functools

import numpy as np
import jax
import jax.numpy as jnp
from jax import lax
from jax.experimental import pallas as pl
from jax.experimental.pallas import tpu as pltpu

D_MODEL = 1024
DEPTH = 2
CONV_CH = D_MODEL // 2
CONV_WIDTH = 31
NSA_HEADS = 8
NSA_KV_HEADS = 2
HEAD_DIM = (D_MODEL // 2) // NSA_HEADS
CMP_BLOCK = 32
CMP_STRIDE = 16
SEL_BLOCK = 64
SEL_TOP_N = 16
WINDOW = 512
Q_BLOCK = 128
FORCE_SCORE = 1e4
SHORT_CONV_WIDTH = 3
XA_HEADS = 4
XA_HEAD_DIM = D_MODEL // XA_HEADS
N_GROUPS = 4
EXPERTS_PER_GROUP = 8
N_EXPERTS = N_GROUPS * EXPERTS_PER_GROUP
EXPERT_TOP_K = 2
DN_ALPHA = (2 * DEPTH) ** 0.25
LN_EPS = 1e-5
NEG_INF = -1e30
KV_COLS = NSA_KV_HEADS * HEAD_DIM
QCOLS = NSA_HEADS * HEAD_DIM
GATE_ROWS = 3 * NSA_HEADS

LANES = 128
SUBLANES = 8
HEADS_PER_KV = NSA_HEADS // NSA_KV_HEADS
QL = Q_BLOCK * HEADS_PER_KV
SEL_TILE = 1024
WIN_SPAN = WINDOW + Q_BLOCK
BLOCKS_PER_TILE = SEL_TILE // SEL_BLOCK
SEQ_TILE = 512
HALO = 32
ODD_HALO = 8
ROUTE_TILE = 512
EXPERT_TILE = 512
ROUTE_ROWS = 8
VMEM_LIMIT = 56 * 1024 * 1024

_NT = (((1,), (1,)), ((), ()))


def _ln_rows(y, g, b):
    mu = jnp.mean(y, axis=-1, keepdims=True)
    yc = y - mu
    var = jnp.mean(yc * yc, axis=-1, keepdims=True)
    return yc * lax.rsqrt(var + LN_EPS) * g + b


def _even_in_kernel(x_ref, wa_ref, wkv_ref, wk2_ref, wqT_ref, wvT_ref, wgT_ref,
                    a_ref, kv_ref, ks_ref, kw_ref, qT_ref, vsT_ref, vwT_ref, gT_ref):
    f32 = jnp.float32
    bf16 = jnp.bfloat16
    xb = x_ref[0].astype(bf16)
    av = jnp.dot(xb, wa_ref[...], preferred_element_type=f32)
    a_ref[0] = av[:, :CONV_CH] * jax.nn.sigmoid(av[:, CONV_CH:])
    kv_ref[0] = jnp.dot(xb, wkv_ref[...], preferred_element_type=f32)
    k2 = jnp.dot(xb, wk2_ref[...], preferred_element_type=f32)
    ks_ref[0] = k2[:, :KV_COLS].astype(bf16)
    kw_ref[0] = k2[:, KV_COLS:].astype(bf16)
    qT_ref[0] = lax.dot_general(wqT_ref[...], xb, _NT, preferred_element_type=f32).astype(bf16)
    vT = lax.dot_general(wvT_ref[...], xb, _NT, preferred_element_type=f32).astype(bf16)
    vsT_ref[0, 0] = vT[:KV_COLS]
    for j in range(SEQ_TILE // Q_BLOCK):
        vwT_ref[0, j] = vT[KV_COLS:, j * Q_BLOCK:(j + 1) * Q_BLOCK]
    gT_ref[0] = lax.dot_general(wgT_ref[...], xb, _NT, preferred_element_type=f32)


def _even_in_proj(x, w_in):
    b, s, d = x.shape
    bf16 = jnp.bfloat16
    c = np.cumsum((0, CONV_CH, CONV_CH, QCOLS, KV_COLS, KV_COLS, KV_COLS, KV_COLS, KV_COLS, KV_COLS, GATE_ROWS))
    col = lambda i, j: w_in[:, c[i]:c[j]]
    wa = col(0, 2).astype(bf16)
    wq_t = col(2, 3).T.astype(bf16)
    wkv = col(3, 5).astype(bf16)
    wk2 = jnp.concatenate([col(5, 6), col(7, 8)], axis=1).astype(bf16)
    wv_t = jnp.concatenate([col(6, 7), col(8, 9)], axis=1).T.astype(bf16)
    wg_t = col(9, 10).T.astype(bf16)
    ts = SEQ_TILE
    per_sel = SEL_TILE // ts
    full = lambda shape: pl.BlockSpec(shape, lambda i, j: (0,) * len(shape))
    return pl.pallas_call(
        _even_in_kernel,
        name="even_in_proj",
        grid=(b, s // ts),
        in_specs=[pl.BlockSpec((1, ts, d), lambda i, j: (i, j, 0)),
                  full(wa.shape), full(wkv.shape), full(wk2.shape), full(wq_t.shape), full(wv_t.shape), full(wg_t.shape)],
        out_specs=[pl.BlockSpec((1, ts, CONV_CH), lambda i, j: (i, j, 0)),
                   pl.BlockSpec((1, ts, 2 * KV_COLS), lambda i, j: (i, j, 0)),
                   pl.BlockSpec((1, ts, KV_COLS), lambda i, j: (i, j, 0)),
                   pl.BlockSpec((1, ts, KV_COLS), lambda i, j: (i, j, 0)),
                   pl.BlockSpec((1, QCOLS, ts), lambda i, j: (i, 0, j)),
                   pl.BlockSpec((1, 1, KV_COLS, ts), lambda i, j: (i, j // per_sel, 0, j % per_sel)),
                   pl.BlockSpec((1, ts // Q_BLOCK, KV_COLS, Q_BLOCK), lambda i, j: (i, j, 0, 0)),
                   pl.BlockSpec((1, GATE_ROWS, ts), lambda i, j: (i, 0, j))],
        out_shape=[jax.ShapeDtypeStruct((b, s, CONV_CH), jnp.float32),
                   jax.ShapeDtypeStruct((b, s, 2 * KV_COLS), jnp.float32),
                   jax.ShapeDtypeStruct((b, s, KV_COLS), bf16),
                   jax.ShapeDtypeStruct((b, s, KV_COLS), bf16),
                   jax.ShapeDtypeStruct((b, QCOLS, s), bf16),
                   jax.ShapeDtypeStruct((b, s // SEL_TILE, KV_COLS, SEL_TILE), bf16),
                   jax.ShapeDtypeStruct((b, s // Q_BLOCK, KV_COLS, Q_BLOCK), bf16),
                   jax.ShapeDtypeStruct((b, GATE_ROWS, s), jnp.float32)],
        compiler_params=pltpu.CompilerParams(dimension_semantics=("arbitrary", "arbitrary"), vmem_limit_bytes=VMEM_LIMIT),
    )(x, wa, wkv, wk2, wq_t, wv_t, wg_t)


def _conv_kernel(cur_ref, halo_ref, w_ref, cb_ref, g_ref, b_ref, o_ref, ext_ref, win_ref):
    j = pl.program_id(1)
    ts = cur_ref.shape[1]
    halo = halo_ref[0]
    ext_ref[0:HALO, :] = jnp.where(j > 0, halo, jnp.zeros_like(halo))
    ext_ref[HALO:HALO + ts, :] = cur_ref[0]
    first = HALO - (CONV_WIDTH - 1)
    acc = jnp.zeros((ts, CONV_CH), jnp.float32)
    for p in range(SUBLANES):
        n_a = len(range(p, CONV_WIDTH, SUBLANES))
        rows = ts + SUBLANES * (n_a - 1)
        win_ref[0:rows, :] = ext_ref[first + p:first + p + rows, :]
        for a in range(n_a):
            k = SUBLANES * a + p
            acc = acc + w_ref[k:k + 1, :] * win_ref[SUBLANES * a:SUBLANES * a + ts, :]
    y = _ln_rows(acc + cb_ref[...], g_ref[...], b_ref[...])
    o_ref[0] = (y * jax.nn.sigmoid(y)).astype(o_ref.dtype)


def _conformer_conv(a, conv_w, conv_b, cn_g, cn_b):
    b, s, c = a.shape
    ts = SEQ_TILE
    per = ts // HALO
    row = lambda v: v.reshape(1, c)
    return pl.pallas_call(
        _conv_kernel,
        name="conformer_conv",
        grid=(b, s // ts),
        in_specs=[pl.BlockSpec((1, ts, c), lambda i, j: (i, j, 0)),
                  pl.BlockSpec((1, HALO, c), lambda i, j: (i, jnp.maximum(j * per - 1, 0), 0)),
                  pl.BlockSpec((CONV_WIDTH, c), lambda i, j: (0, 0)),
                  pl.BlockSpec((1, c), lambda i, j: (0, 0)),
                  pl.BlockSpec((1, c), lambda i, j: (0, 0)),
                  pl.BlockSpec((1, c), lambda i, j: (0, 0))],
        out_specs=pl.BlockSpec((1, ts, c), lambda i, j: (i, j, 0)),
        out_shape=jax.ShapeDtypeStruct((b, s, c), jnp.bfloat16),
        scratch_shapes=[pltpu.VMEM((HALO + ts, c), jnp.float32), pltpu.VMEM((HALO + ts, c), jnp.float32)],
        compiler_params=pltpu.CompilerParams(dimension_semantics=("arbitrary", "arbitrary")),
    )(a, a, conv_w, row(conv_b), row(cn_g), row(cn_b))


def _compress_kernel(r_ref, pe_ref, w1_ref, w2_ref, o_ref):
    f32 = jnp.float32
    bf16 = jnp.bfloat16
    r = r_ref[0, 0]
    half = r.shape[1]
    ha = jnp.dot((r + pe_ref[0, 0:1, :]).astype(bf16), w1_ref[0, :half, :], preferred_element_type=f32)
    hb = jnp.dot((r + pe_ref[0, 1:2, :]).astype(bf16), w1_ref[0, half:, :], preferred_element_type=f32)
    hb_next = jnp.concatenate([hb[1:], jnp.zeros_like(hb[0:1])], axis=0)
    hid = jax.nn.gelu(ha + hb_next).astype(bf16)
    o_ref[0, 0] = jnp.dot(hid, w2_ref[0], preferred_element_type=f32).astype(bf16)


def _compress_kv(kv_in, pe_k, w1_k, w2_k, pe_v, w1_v, w2_v):
    b, s, _ = kv_in.shape
    bf16 = jnp.bfloat16
    nch = s // CMP_STRIDE
    g = NSA_KV_HEADS
    r = kv_in.reshape(b, nch, CMP_STRIDE, 2 * g, HEAD_DIM)
    r = jnp.transpose(r, (0, 3, 1, 2, 4)).reshape(b, 2 * g, nch, CMP_STRIDE * HEAD_DIM)
    pe = jnp.stack([pe_k, pe_v]).reshape(2, 2, CMP_STRIDE * HEAD_DIM)
    w1 = jnp.stack([w1_k, w1_v]).astype(bf16)
    w2 = jnp.stack([w2_k, w2_v]).astype(bf16)
    out = pl.pallas_call(
        _compress_kernel,
        name="compress_kv",
        grid=(b, 2 * g),
        in_specs=[pl.BlockSpec((1, 1, nch, r.shape[3]), lambda i, j: (i, j, 0, 0)),
                  pl.BlockSpec((1, 2, pe.shape[2]), lambda i, j: (j // g, 0, 0)),
                  pl.BlockSpec((1,) + w1.shape[1:], lambda i, j: (j // g, 0, 0)),
                  pl.BlockSpec((1,) + w2.shape[1:], lambda i, j: (j // g, 0, 0))],
        out_specs=pl.BlockSpec((1, 1, nch, HEAD_DIM), lambda i, j: (i, j, 0, 0)),
        out_shape=jax.ShapeDtypeStruct((b, 2 * g, nch, HEAD_DIM), bf16),
        compiler_params=pltpu.CompilerParams(dimension_semantics=("arbitrary", "arbitrary")),
    )(r, pe, w1, w2)
    kc = jnp.transpose(out[:, :g], (0, 2, 1, 3)).reshape(b, nch, KV_COLS)
    vcT = jnp.transpose(out[:, g:], (0, 1, 3, 2)).reshape(b, KV_COLS, nch)
    return kc, vcT


def _proj_ln_kernel(a_ref, o_ref, x_ref, w_ref, g_ref, b_ref, y_ref):
    lhs = jnp.concatenate([a_ref[...], o_ref[...]], axis=1)
    mix = jnp.dot(lhs, w_ref[...], preferred_element_type=jnp.float32)
    y_ref[...] = _ln_rows(DN_ALPHA * x_ref[...] + mix, g_ref[...], b_ref[...])


def _proj_residual_ln(a, o, x, w, ln_g, ln_b):
    t, d = x.shape
    tm = SEQ_TILE
    return pl.pallas_call(
        _proj_ln_kernel,
        name="proj_residual_ln",
        grid=(t // tm,),
        in_specs=[pl.BlockSpec((tm, a.shape[1]), lambda i: (i, 0)),
                  pl.BlockSpec((tm, o.shape[1]), lambda i: (i, 0)),
                  pl.BlockSpec((tm, d), lambda i: (i, 0)),
                  pl.BlockSpec(w.shape, lambda i: (0, 0)),
                  pl.BlockSpec((1, d), lambda i: (0, 0)),
                  pl.BlockSpec((1, d), lambda i: (0, 0))],
        out_specs=pl.BlockSpec((tm, d), lambda i: (i, 0)),
        out_shape=jax.ShapeDtypeStruct((t, d), jnp.float32),
        compiler_params=pltpu.CompilerParams(dimension_semantics=("arbitrary",), vmem_limit_bytes=VMEM_LIMIT),
    )(a, o, x, w.astype(jnp.bfloat16), ln_g.reshape(1, d), ln_b.reshape(1, d))


def _mem_kv_kernel(m_ref, wk_ref, wv_ref, k_ref, v_ref):
    mb = m_ref[...].astype(jnp.bfloat16)
    k_ref[...] = jnp.dot(mb, wk_ref[...], preferred_element_type=jnp.float32).astype(jnp.bfloat16)
    v_ref[...] = jnp.dot(mb, wv_ref[...], preferred_element_type=jnp.float32).astype(jnp.bfloat16)


def _mem_kv(mem, wk, wv):
    b, m, d = mem.shape
    bf16 = jnp.bfloat16
    k, v = pl.pallas_call(
        _mem_kv_kernel,
        name="mem_kv",
        grid=(b,),
        in_specs=[pl.BlockSpec((m, d), lambda i: (i, 0)), pl.BlockSpec((d, d), lambda i: (0, 0)), pl.BlockSpec((d, d), lambda i: (0, 0))],
        out_specs=[pl.BlockSpec((m, d), lambda i: (i, 0)), pl.BlockSpec((m, d), lambda i: (i, 0))],
        out_shape=[jax.ShapeDtypeStruct((b * m, d), bf16), jax.ShapeDtypeStruct((b * m, d), bf16)],
        compiler_params=pltpu.CompilerParams(dimension_semantics=("arbitrary",), vmem_limit_bytes=VMEM_LIMIT),
    )(mem.reshape(b * m, d), wk.astype(bf16), wv.astype(bf16))
    return k.reshape(b, m, d), v.reshape(b, m, d)


def _xattn_kernel(x_ref, k_ref, v_ref, wq_ref, wo_ref, g_ref, b_ref, y_ref):
    f32 = jnp.float32
    bf16 = jnp.bfloat16
    x = x_ref[0]
    q = (jnp.dot(x.astype(bf16), wq_ref[...], preferred_element_type=f32).astype(bf16)
         * jnp.asarray(XA_HEAD_DIM ** -0.5, bf16))
    heads = []
    for h in range(XA_HEADS):
        cols = slice(h * XA_HEAD_DIM, (h + 1) * XA_HEAD_DIM)
        s = lax.dot_general(q[:, cols], k_ref[0, :, cols], _NT, preferred_element_type=f32)
        m = jnp.max(s, axis=1, keepdims=True)
        p = jnp.exp(s - m)
        p = p / jnp.sum(p, axis=1, keepdims=True)
        heads.append(jnp.dot(p.astype(bf16), v_ref[0, :, cols], preferred_element_type=f32).astype(bf16))
    att = jnp.concatenate(heads, axis=1)
    out = jnp.dot(att, wo_ref[...], preferred_element_type=f32)
    y_ref[0] = _ln_rows(DN_ALPHA * x + out, g_ref[...], b_ref[...])


def _xattn_sublayer(x, mem_k, mem_v, wq, wo, ln_g, ln_b):
    b, s, d = x.shape
    m = mem_k.shape[1]
    ts = SEQ_TILE
    bf16 = jnp.bfloat16
    return pl.pallas_call(
        _xattn_kernel,
        name="xattn_sublayer",
        grid=(b, s // ts),
        in_specs=[pl.BlockSpec((1, ts, d), lambda i, j: (i, j, 0)),
                  pl.BlockSpec((1, m, d), lambda i, j: (i, 0, 0)),
                  pl.BlockSpec((1, m, d), lambda i, j: (i, 0, 0)),
                  pl.BlockSpec((d, d), lambda i, j: (0, 0)),
                  pl.BlockSpec((d, d), lambda i, j: (0, 0)),
                  pl.BlockSpec((1, d), lambda i, j: (0, 0)),
                  pl.BlockSpec((1, d), lambda i, j: (0, 0))],
        out_specs=pl.BlockSpec((1, ts, d), lambda i, j: (i, j, 0)),
        out_shape=jax.ShapeDtypeStruct((b, s, d), jnp.float32),
        compiler_params=pltpu.CompilerParams(dimension_semantics=("arbitrary", "arbitrary"), vmem_limit_bytes=VMEM_LIMIT),
    )(x, mem_k, mem_v, wq.astype(bf16), wo.astype(bf16), ln_g.reshape(1, d), ln_b.reshape(1, d))


def _odd_kernel(x_ref, halo_ref, wb_ref, wc_ref, wh_ref, cw_ref, wo_ref, g_ref, b_ref, y_ref):
    f32 = jnp.float32
    bf16 = jnp.bfloat16
    j = pl.program_id(1)
    ts = x_ref.shape[1]
    x = x_ref[0]
    xe = jnp.concatenate([halo_ref[0], x], axis=0).astype(bf16)
    u = (jnp.dot(xe, wc_ref[...], preferred_element_type=f32) * jnp.dot(xe, wh_ref[...], preferred_element_type=f32))
    row = lax.broadcasted_iota(jnp.int32, (ODD_HALO + ts, 1), 0)
    u = jnp.where((row >= ODD_HALO) | (j > 0), u, 0.0)
    conv = jnp.zeros((ts, u.shape[1]), f32)
    for k in range(SHORT_CONV_WIDTH):
        off = ODD_HALO - (SHORT_CONV_WIDTH - 1) + k
        conv = conv + cw_ref[k:k + 1, :] * u[off:off + ts, :]
    gate_b = jnp.dot(xe[ODD_HALO:], wb_ref[...], preferred_element_type=f32)
    mix = jnp.dot((gate_b * conv).astype(bf16), wo_ref[...], preferred_element_type=f32)
    y_ref[0] = _ln_rows(DN_ALPHA * x + mix, g_ref[...], b_ref[...])


def _odd_mixer_sublayer(x, w_in, conv_w, w_out, ln_g, ln_b):
    b, s, d = x.shape
    ts = SEQ_TILE
    per = ts // ODD_HALO
    bf16 = jnp.bfloat16
    wb, wc, wh = (w_in[:, i * d:(i + 1) * d].astype(bf16) for i in range(3))
    full = lambda shape: pl.BlockSpec(shape, lambda i, j: (0,) * len(shape))
    return pl.pallas_call(
        _odd_kernel,
        name="odd_mixer_sublayer",
        grid=(b, s // ts),
        in_specs=[pl.BlockSpec((1, ts, d), lambda i, j: (i, j, 0)),
                  pl.BlockSpec((1, ODD_HALO, d), lambda i, j: (i, jnp.maximum(j * per - 1, 0), 0)),
                  full((d, d)), full((d, d)), full((d, d)), full(conv_w.shape), full((d, d)), full((1, d)), full((1, d))],
        out_specs=pl.BlockSpec((1, ts, d), lambda i, j: (i, j, 0)),
        out_shape=jax.ShapeDtypeStruct((b, s, d), jnp.float32),
        compiler_params=pltpu.CompilerParams(dimension_semantics=("arbitrary", "arbitrary"), vmem_limit_bytes=VMEM_LIMIT),
    )(x, x, wb, wc, wh, conv_w, w_out.astype(bf16), ln_g.reshape(1, d), ln_b.reshape(1, d))


def _cmp_to_sel_matrix(n_cmp, n_sel):
    c0 = np.arange(n_cmp) * CMP_STRIDE
    s0 = np.arange(n_sel) * SEL_BLOCK
    ov = np.minimum(c0[:, None] + CMP_BLOCK, s0[None, :] + SEL_BLOCK) - np.maximum(c0[:, None], s0[None, :])
    return (np.clip(ov, 0, None) / CMP_BLOCK).astype(np.float32)


def _nsa_kernel(qT_ref, gT_ref, kc_ref, vcT_ref, mselT_ref, ks_ref, vsT_ref, kw_ref, vwT_ref, o_ref,
                score_ref, sel_ref, *, n_sel):
    c = pl.program_id(1)
    q0 = c * Q_BLOCK
    f32 = jnp.float32
    bf16 = jnp.bfloat16
    n_cmp_pad = kc_ref.shape[1]

    lane_q = lax.broadcasted_iota(jnp.int32, (1, QL), 1) % Q_BLOCK
    t_row = q0 + lane_q
    t_row_q = q0 + lax.broadcasted_iota(jnp.int32, (1, Q_BLOCK), 1)
    cur_q = t_row_q // SEL_BLOCK

    top_n = min(SEL_TOP_N, n_sel)
    m_iota = lax.broadcasted_iota(jnp.int32, (n_sel, Q_BLOCK), 0)
    forced = (m_iota == 0) | (m_iota == cur_q) | (m_iota == cur_q - 1)
    valid = m_iota <= cur_q
    n_comp = jnp.minimum(q0 // SEL_BLOCK + 2, n_sel)

    q_g, o_c = [], []
    for g in range(NSA_KV_HEADS):
        pieces = []
        for hg in range(HEADS_PER_KV):
            h = g * HEADS_PER_KV + hg
            qh = qT_ref[0, h * HEAD_DIM:(h + 1) * HEAD_DIM, :] * jnp.asarray(HEAD_DIM ** -0.5, bf16)
            z = jnp.zeros_like(qh)
            pieces.append(jnp.concatenate([qh, z] if g == 0 else [z, qh], axis=0))
        qTp = jnp.concatenate(pieces, axis=1)
        q_g.append(qTp)
        rows = slice(g * HEAD_DIM, (g + 1) * HEAD_DIM)

        s_c = jnp.dot(kc_ref[0], qTp, preferred_element_type=f32)
        n_iota = lax.broadcasted_iota(jnp.int32, (n_cmp_pad, QL), 0)
        mask_c = (n_iota * CMP_STRIDE + (CMP_BLOCK - 1)) <= t_row
        s_c = jnp.where(mask_c, s_c, NEG_INF)
        m_c = jnp.max(s_c, axis=0, keepdims=True)
        p_c = jnp.where(mask_c, jnp.exp(s_c - m_c), 0.0)
        l_c = jnp.sum(p_c, axis=0, keepdims=True)
        p_c = p_c * jnp.where(l_c > 0.0, 1.0 / l_c, 0.0)
        p_cb = p_c.astype(bf16)
        o_c.append(jnp.dot(vcT_ref[0], p_cb, preferred_element_type=f32)[rows])
        imp4 = jnp.dot(mselT_ref[...], p_cb, preferred_element_type=f32)
        imp = imp4[:, 0:Q_BLOCK]
        for hg in range(1, HEADS_PER_KV):
            imp = imp + imp4[:, hg * Q_BLOCK:(hg + 1) * Q_BLOCK]

        score = jnp.where(valid, jnp.where(forced, FORCE_SCORE, imp), NEG_INF)
        score_ref[...] = score

        def rank_body(mp, rank, score=score):
            row = score_ref[pl.ds(mp, 1), :]
            beats = (row > score) | ((row == score) & (mp < m_iota))
            return rank + beats.astype(jnp.int32)

        rank = lax.fori_loop(0, n_comp, rank_body, jnp.zeros((n_sel, Q_BLOCK), jnp.int32))
        sel_ref[g] = jnp.where((rank < top_n) & valid, 0.0, NEG_INF)

    def sel_tile(j, carry, causal):
        k_tile = ks_ref[0, j]
        v_tile = vsT_ref[0, j]
        blk0 = pl.multiple_of(j * BLOCKS_PER_TILE, BLOCKS_PER_TILE)
        out = []
        for g in range(NSA_KV_HEADS):
            m_i, l_i, acc = carry[g]
            s = jnp.dot(k_tile, q_g[g], preferred_element_type=f32)
            selrows = sel_ref[g, pl.ds(blk0, BLOCKS_PER_TILE), :]
            bias = jnp.concatenate(
                [jnp.broadcast_to(selrows[r:r + 1, :], (SEL_BLOCK, Q_BLOCK)) for r in range(BLOCKS_PER_TILE)], axis=0)
            if causal:
                key = j * SEL_TILE + lax.broadcasted_iota(jnp.int32, (SEL_TILE, Q_BLOCK), 0)
                bias = jnp.where(key <= t_row_q, bias, NEG_INF)
            s = s + jnp.concatenate([bias] * HEADS_PER_KV, axis=1)
            m_new = jnp.maximum(m_i, jnp.max(s, axis=0, keepdims=True))
            alpha = jnp.exp(m_i - m_new)
            p = jnp.exp(s - m_new)
            l_new = alpha * l_i + jnp.sum(p, axis=0, keepdims=True)
            pv = jnp.dot(v_tile, p.astype(bf16), preferred_element_type=f32)
            out.append((m_new, l_new, alpha * acc + pv))
        return tuple(out)

    n_full = q0 // SEL_TILE
    init = tuple((jnp.full((1, QL), NEG_INF, f32), jnp.zeros((1, QL), f32), jnp.zeros((KV_COLS, QL), f32))
                 for _ in range(NSA_KV_HEADS))
    carry = lax.fori_loop(0, n_full, functools.partial(sel_tile, causal=False), init)
    carry = sel_tile(n_full, carry, causal=True)

    start = pl.multiple_of(jnp.maximum(q0 - WINDOW, 0), Q_BLOCK)
    j0 = start // Q_BLOCK
    k_win = kw_ref[0, pl.ds(start, WIN_SPAN), :]
    key_w = start + lax.broadcasted_iota(jnp.int32, (WIN_SPAN, Q_BLOCK), 0)
    bias_w = jnp.where(key_w <= t_row_q, 0.0, NEG_INF)
    bias_w = jnp.where(key_w > t_row_q - WINDOW, bias_w, NEG_INF)
    bias_w = jnp.concatenate([bias_w] * HEADS_PER_KV, axis=1)
    for g in range(NSA_KV_HEADS):
        rows = slice(g * HEAD_DIM, (g + 1) * HEAD_DIM)
        m_s, l_s, acc_s = carry[g]
        o_s = acc_s[rows] * (1.0 / l_s)
        s_w = jnp.dot(k_win, q_g[g], preferred_element_type=f32) + bias_w
        m_w = jnp.max(s_w, axis=0, keepdims=True)
        p_w = jnp.exp(s_w - m_w)
        l_w = jnp.sum(p_w, axis=0, keepdims=True)
        p_wb = p_w.astype(bf16)
        acc_w = jnp.zeros((KV_COLS, QL), f32)
        for i in range(WIN_SPAN // Q_BLOCK):
            acc_w = acc_w + jnp.dot(vwT_ref[0, j0 + i], p_wb[i * Q_BLOCK:(i + 1) * Q_BLOCK, :], preferred_element_type=f32)
        o_w = acc_w[rows] * (1.0 / l_w)

        for pair in range(HEADS_PER_KV // 2):
            halves = []
            for hg in (2 * pair, 2 * pair + 1):
                h = g * HEADS_PER_KV + hg
                lanes = slice(hg * Q_BLOCK, (hg + 1) * Q_BLOCK)
                gate = jax.nn.sigmoid(gT_ref[0, 3 * h:3 * h + 3, :])
                halves.append(gate[0:1] * o_c[g][:, lanes] + gate[1:2] * o_s[:, lanes] + gate[2:3] * o_w[:, lanes])
            both = jnp.concatenate(halves, axis=0)
            col0 = (g * HEADS_PER_KV + 2 * pair) * HEAD_DIM
            o_ref[0, :, col0:col0 + 2 * HEAD_DIM] = both.T.astype(o_ref.dtype)


def _nsa_attention(qT, gT, kc, vcT, ks, vsT, kw, vwT):
    b, hd, s = qT.shape
    n_sel = s // SEL_BLOCK
    nc = kc.shape[1]
    n_cmp = (s - CMP_BLOCK) // CMP_STRIDE + 1
    mselT = jnp.asarray(np.pad(_cmp_to_sel_matrix(n_cmp, n_sel).T, ((0, 0), (0, nc - n_cmp))), jnp.bfloat16)
    ks4 = ks.reshape(b, s // SEL_TILE, SEL_TILE, KV_COLS)
    return pl.pallas_call(
        functools.partial(_nsa_kernel, n_sel=n_sel),
        name="nsa_attention",
        grid=(b, s // Q_BLOCK),
        in_specs=[
            pl.BlockSpec((1, hd, Q_BLOCK), lambda i, c: (i, 0, c)),
            pl.BlockSpec((1, 3 * NSA_HEADS, Q_BLOCK), lambda i, c: (i, 0, c)),
            pl.BlockSpec((1, nc, KV_COLS), lambda i, c: (i, 0, 0)),
            pl.BlockSpec((1, KV_COLS, nc), lambda i, c: (i, 0, 0)),
            pl.BlockSpec((n_sel, nc), lambda i, c: (0, 0)),
            pl.BlockSpec((1, s // SEL_TILE, SEL_TILE, KV_COLS), lambda i, c: (i, 0, 0, 0)),
            pl.BlockSpec((1, s // SEL_TILE, KV_COLS, SEL_TILE), lambda i, c: (i, 0, 0, 0)),
            pl.BlockSpec((1, s, KV_COLS), lambda i, c: (i, 0, 0)),
            pl.BlockSpec((1, s // Q_BLOCK, KV_COLS, Q_BLOCK), lambda i, c: (i, 0, 0, 0)),
        ],
        out_specs=pl.BlockSpec((1, Q_BLOCK, hd), lambda i, c: (i, c, 0)),
        out_shape=jax.ShapeDtypeStruct((b, s, hd), jnp.bfloat16),
        scratch_shapes=[pltpu.VMEM((n_sel, Q_BLOCK), jnp.float32),
                        pltpu.VMEM((NSA_KV_HEADS, n_sel, Q_BLOCK), jnp.float32)],
        compiler_params=pltpu.CompilerParams(dimension_semantics=("arbitrary", "arbitrary")),
    )(qT, gT, kc, vcT, mselT, ks4, vsT, kw, vwT)


def _router_kernel(x_ref, wT_ref, b_ref, tri_ref, route_ref, cnt_ref, run_ref):
    i = pl.program_id(0)
    f32 = jnp.float32

    @pl.when(i == 0)
    def _():
        run_ref[...] = jnp.zeros_like(run_ref)

    tm = x_ref.shape[0]
    logits = lax.dot_general(wT_ref[...], x_ref[...].astype(jnp.bfloat16), _NT, preferred_element_type=f32) + b_ref[...]
    sub = lax.broadcasted_iota(jnp.int32, (LANES, tm), 0)
    is_g = sub < N_GROUPS
    gl = jnp.where(is_g, logits, NEG_INF)
    g_max = jnp.max(gl, axis=0, keepdims=True)
    g_star = jnp.min(jnp.where(gl == g_max, sub, LANES), axis=0, keepdims=True)
    p_group = 1.0 / jnp.sum(jnp.where(is_g, jnp.exp(gl - g_max), 0.0), axis=0, keepdims=True)
    lo = N_GROUPS + g_star * EXPERTS_PER_GROUP
    in_grp = (sub >= lo) & (sub < lo + EXPERTS_PER_GROUP)
    el = jnp.where(in_grp, logits, NEG_INF)
    v1 = jnp.max(el, axis=0, keepdims=True)
    i1 = jnp.min(jnp.where(el == v1, sub, LANES), axis=0, keepdims=True)
    el2 = jnp.where(sub == i1, NEG_INF, el)
    v2 = jnp.max(el2, axis=0, keepdims=True)
    i2 = jnp.min(jnp.where(el2 == v2, sub, LANES), axis=0, keepdims=True)
    e21 = jnp.exp(v2 - v1)
    gate1 = p_group * (1.0 / (1.0 + e21))
    gate2 = p_group * (e21 / (1.0 + e21))
    oh1 = (sub == i1).astype(f32)
    oh2 = (sub == i2).astype(f32)
    both = oh1 + oh2
    before = jnp.dot(both.astype(jnp.bfloat16), tri_ref[...], preferred_element_type=f32) + run_ref[...]
    rank1 = jnp.sum(oh1 * before, axis=0, keepdims=True)
    rank2 = jnp.sum(oh2 * before, axis=0, keepdims=True)
    run_ref[...] = run_ref[...] + jnp.sum(both, axis=1, keepdims=True)
    cnt_ref[...] = run_ref[...]
    zero = jnp.zeros_like(gate1)
    route_ref[...] = jnp.concatenate(
        [gate1, gate2, (i1 - N_GROUPS).astype(f32), (i2 - N_GROUPS).astype(f32), rank1, rank2, zero, zero], axis=0)


def _moe_route(xt, wg, bg, we, be):
    t, d = xt.shape
    pad = LANES - N_GROUPS - N_EXPERTS
    w_t = jnp.concatenate([wg, we.reshape(d, N_EXPERTS), jnp.zeros((d, pad), wg.dtype)], axis=1).T
    bias = jnp.concatenate([bg, be.reshape(N_EXPERTS), jnp.zeros((pad,), bg.dtype)])[:, None]
    tri = jnp.asarray(np.triu(np.ones((ROUTE_TILE, ROUTE_TILE), np.float32), 1), jnp.bfloat16)
    return pl.pallas_call(
        _router_kernel,
        name="moe_router",
        grid=(t // ROUTE_TILE,),
        in_specs=[pl.BlockSpec((ROUTE_TILE, d), lambda i: (i, 0)),
                  pl.BlockSpec((LANES, d), lambda i: (0, 0)),
                  pl.BlockSpec((LANES, 1), lambda i: (0, 0)),
                  pl.BlockSpec((ROUTE_TILE, ROUTE_TILE), lambda i: (0, 0))],
        out_specs=[pl.BlockSpec((ROUTE_ROWS, ROUTE_TILE), lambda i: (0, i)),
                   pl.BlockSpec((LANES, 1), lambda i: (0, 0))],
        out_shape=[jax.ShapeDtypeStruct((ROUTE_ROWS, t), jnp.float32), jax.ShapeDtypeStruct((LANES, 1), jnp.float32)],
        scratch_shapes=[pltpu.VMEM((LANES, 1), jnp.float32)],
        compiler_params=pltpu.CompilerParams(dimension_semantics=("arbitrary",)),
    )(xt, w_t.astype(jnp.bfloat16), bias, tri)


def _dispatch_kernel(dest_ref, x_ref, zero_hbm, xd_hbm, sem):
    del zero_hbm
    tm = x_ref.shape[0]

    def body(r, c):
        for k in range(EXPERT_TOP_K):
            pltpu.make_async_copy(
                x_ref.at[pl.ds(r, 1)], xd_hbm.at[pl.ds(dest_ref[0, 0, k * tm + r], 1)], sem).start(priority=k)
        return c

    lax.fori_loop(0, tm, body, 0, unroll=8)
    for k in range(EXPERT_TOP_K):
        pltpu.make_async_copy(x_ref, xd_hbm.at[pl.ds(0, tm)], sem).wait()


def _moe_dispatch(xt, dest2, n_rows):
    t, d = xt.shape
    return pl.pallas_call(
        _dispatch_kernel,
        name="moe_dispatch",
        grid=(t // ROUTE_TILE,),
        in_specs=[pl.BlockSpec((1, 1, EXPERT_TOP_K * ROUTE_TILE), lambda i: (i, 0, 0), memory_space=pltpu.SMEM),
                  pl.BlockSpec((ROUTE_TILE, d), lambda i: (i, 0)),
                  pl.BlockSpec(memory_space=pl.ANY)],
        out_specs=pl.BlockSpec(memory_space=pl.ANY),
        out_shape=jax.ShapeDtypeStruct((n_rows, d), xt.dtype),
        scratch_shapes=[pltpu.SemaphoreType.DMA(())],
        input_output_aliases={2: 0},
        compiler_params=pltpu.CompilerParams(dimension_semantics=("arbitrary",), has_side_effects=True),
    )(dest2, xt, jnp.zeros((n_rows, d), xt.dtype))


def _expert_kernel(te_ref, nu_ref, xd_ref, w1_ref, w3_ref, w2_ref, y_ref, w1b, w3b, w2b):
    i = pl.program_id(0)
    used = i < nu_ref[0]
    new_expert = (i == 0) | (te_ref[i] != te_ref[jnp.maximum(i - 1, 0)])

    @pl.when(used & new_expert)
    def _():
        w1b[...] = w1_ref[0, 0].astype(jnp.bfloat16)
        w3b[...] = w3_ref[0, 0].astype(jnp.bfloat16)
        w2b[...] = w2_ref[0, 0].astype(jnp.bfloat16)

    @pl.when(used)
    def _():
        xb = xd_ref[...].astype(jnp.bfloat16)
        h1 = jnp.dot(xb, w1b[...], preferred_element_type=jnp.float32)
        h3 = jnp.dot(xb, w3b[...], preferred_element_type=jnp.float32)
        a = (h1 * jax.nn.sigmoid(h1) * h3).astype(jnp.bfloat16)
        y_ref[...] = jnp.dot(a, w2b[...], preferred_element_type=jnp.float32)

    @pl.when(jnp.logical_not(used))
    def _():
        y_ref[...] = jnp.zeros_like(y_ref)


def _moe_experts(x_disp, tile_expert, n_used, w1, w3, w2, layer):
    n_rows, d = x_disp.shape
    n_tiles = n_rows // EXPERT_TILE
    hid = w1.shape[3]

    def row_map(i, te, nu):
        return (i, 0)

    def w_map(i, te, nu):
        return (layer, te[i], 0, 0)

    grid_spec = pltpu.PrefetchScalarGridSpec(
        num_scalar_prefetch=2,
        grid=(n_tiles,),
        in_specs=[pl.BlockSpec((EXPERT_TILE, d), row_map),
                  pl.BlockSpec((1, 1, d, hid), w_map),
                  pl.BlockSpec((1, 1, d, hid), w_map),
                  pl.BlockSpec((1, 1, hid, d), w_map)],
        out_specs=pl.BlockSpec((EXPERT_TILE, d), row_map),
        scratch_shapes=[pltpu.VMEM((d, hid), jnp.bfloat16), pltpu.VMEM((d, hid), jnp.bfloat16),
                        pltpu.VMEM((hid, d), jnp.bfloat16)],
    )
    return pl.pallas_call(
        _expert_kernel,
        name="moe_experts",
        grid_spec=grid_spec,
        out_shape=jax.ShapeDtypeStruct((n_rows, d), jnp.float32),
        compiler_params=pltpu.CompilerParams(dimension_semantics=("arbitrary",), vmem_limit_bytes=VMEM_LIMIT),
    )(tile_expert, n_used, x_disp, w1, w3, w2)


def _combine_kernel(dest_ref, dest_next_ref, x_ref, route_ref, g_ref, b_ref, yd_hbm, o_ref, ybuf, sem):
    i = pl.program_id(0)
    n = pl.num_programs(0)
    tm = x_ref.shape[0]
    rows = EXPERT_TOP_K * tm

    def start_tile(dref, slot):
        def body(r, c):
            for k in range(EXPERT_TOP_K):
                row = k * tm + r
                pltpu.make_async_copy(
                    yd_hbm.at[pl.ds(dref[0, 0, row], 1)], ybuf.at[slot, pl.ds(row, 1)], sem.at[slot]).start(priority=k)
            return c
        lax.fori_loop(0, tm, body, 0, unroll=8)

    slot = i % 2

    @pl.when(i == 0)
    def _():
        start_tile(dest_ref, 0)

    @pl.when(i + 1 < n)
    def _():
        start_tile(dest_next_ref, 1 - slot)

    pltpu.make_async_copy(yd_hbm.at[pl.ds(0, rows)], ybuf.at[slot], sem.at[slot]).wait()
    y1 = ybuf[slot, 0:tm, :]
    y2 = ybuf[slot, tm:rows, :]
    route = route_ref[...].T
    y = DN_ALPHA * x_ref[...] + (y1 * route[:, 0:1] + y2 * route[:, 1:2])
    o_ref[...] = _ln_rows(y, g_ref[...], b_ref[...])


def _moe_combine_ln(xt, y_disp, dest2, route, ln_g, ln_b):
    t, d = xt.shape
    n = t // ROUTE_TILE
    rows = EXPERT_TOP_K * ROUTE_TILE
    return pl.pallas_call(
        _combine_kernel,
        name="moe_combine_ln",
        grid=(n,),
        in_specs=[pl.BlockSpec((1, 1, rows), lambda i: (i, 0, 0), memory_space=pltpu.SMEM),
                  pl.BlockSpec((1, 1, rows), lambda i: (jnp.minimum(i + 1, n - 1), 0, 0), memory_space=pltpu.SMEM),
                  pl.BlockSpec((ROUTE_TILE, d), lambda i: (i, 0)),
                  pl.BlockSpec((ROUTE_ROWS, ROUTE_TILE), lambda i: (0, i)),
                  pl.BlockSpec((1, d), lambda i: (0, 0)),
                  pl.BlockSpec((1, d), lambda i: (0, 0)),
                  pl.BlockSpec(memory_space=pl.ANY)],
        out_specs=pl.BlockSpec((ROUTE_TILE, d), lambda i: (i, 0)),
        out_shape=jax.ShapeDtypeStruct((t, d), jnp.float32),
        scratch_shapes=[pltpu.VMEM((2, rows, d), jnp.float32), pltpu.SemaphoreType.DMA((2,))],
        compiler_params=pltpu.CompilerParams(dimension_semantics=("arbitrary",), vmem_limit_bytes=VMEM_LIMIT),
    )(dest2, dest2, xt, route, ln_g.reshape(1, d), ln_b.reshape(1, d), y_disp)


def _moe_sublayer(x, wg, bg, we, be, w1, w3, w2, layer, ln_g, ln_b):
    b, s, d = x.shape
    t = b * s
    xt = x.reshape(t, d)
    route, cnt = _moe_route(xt, wg, bg, we, be)
    counts = cnt[N_GROUPS:N_GROUPS + N_EXPERTS, 0].astype(jnp.int32)
    n_tiles = (t * EXPERT_TOP_K) // EXPERT_TILE + N_EXPERTS
    tiles_per = (counts + EXPERT_TILE - 1) // EXPERT_TILE
    tile_end = jnp.cumsum(tiles_per)
    pad_start = (tile_end - tiles_per) * EXPERT_TILE
    n_used = tile_end[-1:].astype(jnp.int32)
    tile_ids = jnp.minimum(jnp.arange(n_tiles), n_used[0] - 1)
    tile_expert = jnp.sum(tile_ids[:, None] >= tile_end[None, :], axis=1).astype(jnp.int32)
    experts = route[2:4].astype(jnp.int32)
    first_row = jnp.sum(jnp.where(experts[..., None] == jnp.arange(N_EXPERTS), pad_start, 0), axis=-1)
    dest = first_row + route[4:6].astype(jnp.int32)
    dest2 = jnp.swapaxes(dest.reshape(EXPERT_TOP_K, t // ROUTE_TILE, ROUTE_TILE), 0, 1)
    dest2 = dest2.reshape(t // ROUTE_TILE, 1, EXPERT_TOP_K * ROUTE_TILE)
    x_disp = _moe_dispatch(xt, dest2, n_tiles * EXPERT_TILE)
    y_disp = _moe_experts(x_disp, tile_expert, n_used, w1, w3, w2, layer)
    return _moe_combine_ln(xt, y_disp, dest2, route, ln_g, ln_b).reshape(b, s, d)


def kernel(x, mem, mem_wk, mem_wv, ev_w_in, ev_conv_w, ev_conv_b, ev_cnorm_g, ev_cnorm_b, ev_cmp_pe_k, ev_cmp_w1_k, ev_cmp_w2_k, ev_cmp_pe_v, ev_cmp_w1_v, ev_cmp_w2_v, ev_w_out, od_w_in, od_conv_w, od_w_out, ln_mix_g, ln_mix_b, xa_wq, xa_wo, ln_xa_g, ln_xa_b, moe_wg, moe_bg, moe_we, moe_be, moe_w1, moe_w3, moe_w2, ln_ffn_g, ln_ffn_b):
    b, s, d = x.shape
    mem_k, mem_v = _mem_kv(mem, mem_wk, mem_wv)
    for layer in range(DEPTH):
        i = layer // 2
        if layer % 2 == 0:
            a, kv_in, k_sel, k_win, q_t, v_sel_t, v_win_t, gate_t = _even_in_proj(x, ev_w_in[i])
            a = _conformer_conv(a, ev_conv_w[i], ev_conv_b[i], ev_cnorm_g[i], ev_cnorm_b[i])
            k_cmp, v_cmp_t = _compress_kv(kv_in, ev_cmp_pe_k[i], ev_cmp_w1_k[i], ev_cmp_w2_k[i], ev_cmp_pe_v[i], ev_cmp_w1_v[i], ev_cmp_w2_v[i])
            o = _nsa_attention(q_t, gate_t, k_cmp, v_cmp_t, k_sel, v_sel_t, k_win, v_win_t)
            x = _proj_residual_ln(a.reshape(b * s, -1), o.reshape(b * s, -1), x.reshape(b * s, d), ev_w_out[i], ln_mix_g[layer], ln_mix_b[layer]).reshape(b, s, d)
        else:
            x = _odd_mixer_sublayer(x, od_w_in[i], od_conv_w[i], od_w_out[i], ln_mix_g[layer], ln_mix_b[layer])
        x = _xattn_sublayer(x, mem_k, mem_v, xa_wq[layer], xa_wo[layer], ln_xa_g[layer], ln_xa_b[layer])
        x = _moe_sublayer(x, moe_wg[layer], moe_bg[layer], moe_we[layer], moe_be[layer], moe_w1, moe_w3, moe_w2, layer, ln_ffn_g[layer], ln_ffn_b[layer])
    return x
```

```python
import functools

import numpy as np
import jax
import jax.numpy as jnp
from jax import lax
from jax.experimental import pallas as pl
from jax.experimental.pallas import tpu as pltpu

D_MODEL = 1024
DEPTH = 2
CONV_CH = D_MODEL // 2
CONV_WIDTH = 31
NSA_HEADS = 8
NSA_KV_HEADS = 2
HEAD_DIM = (D_MODEL // 2) // NSA_HEADS
CMP_BLOCK = 32
CMP_STRIDE = 16
SEL_BLOCK = 64
SEL_TOP_N = 16
WINDOW = 512
Q_BLOCK = 128
FORCE_SCORE = 1e4
SHORT_CONV_WIDTH = 3
XA_HEADS = 4
XA_HEAD_DIM = D_MODEL // XA_HEADS
N_GROUPS = 4
EXPERTS_PER_GROUP = 8
N_EXPERTS = N_GROUPS * EXPERTS_PER_GROUP
EXPERT_TOP_K = 2
DN_ALPHA = (2 * DEPTH) ** 0.25
LN_EPS = 1e-5
NEG_INF = -1e30
KV_COLS = NSA_KV_HEADS * HEAD_DIM
QCOLS = NSA_HEADS * HEAD_DIM
GATE_ROWS = 3 * NSA_HEADS

LANES = 128
SUBLANES = 8
HEADS_PER_KV = NSA_HEADS // NSA_KV_HEADS
QL = Q_BLOCK * HEADS_PER_KV
SEL_TILE = 1024
WIN_SPAN = WINDOW + Q_BLOCK
BLOCKS_PER_TILE = SEL_TILE // SEL_BLOCK
SEQ_TILE = 512
HALO = 32
ODD_HALO = 8
ROUTE_TILE = 512
EXPERT_TILE = 512
ROUTE_ROWS = 8
VMEM_LIMIT = 56 * 1024 * 1024

_NT = (((1,), (1,)), ((), ()))


def _ln_rows(y, g, b):
    mu = jnp.mean(y, axis=-1, keepdims=True)
    yc = y - mu
    var = jnp.mean(yc * yc, axis=-1, keepdims=True)
    return yc * lax.rsqrt(var + LN_EPS) * g + b


def _even_in_kernel(x_ref, wa_ref, wkv_ref, wk2_ref, wqT_ref, wvT_ref, wgT_ref,
                    a_ref, kv_ref, ks_ref, kw_ref, qT_ref, vsT_ref, vwT_ref, gT_ref):
    f32 = jnp.float32
    bf16 = jnp.bfloat16
    xb = x_ref[0].astype(bf16)
    av = jnp.dot(xb, wa_ref[...], preferred_element_type=f32)
    a_ref[0] = av[:, :CONV_CH] * jax.nn.sigmoid(av[:, CONV_CH:])
    kv_ref[0] = jnp.dot(xb, wkv_ref[...], preferred_element_type=f32)
    k2 = jnp.dot(xb, wk2_ref[...], preferred_element_type=f32)
    ks_ref[0] = k2[:, :KV_COLS].astype(bf16)
    kw_ref[0] = k2[:, KV_COLS:].astype(bf16)
    qT_ref[0] = lax.dot_general(wqT_ref[...], xb, _NT, preferred_element_type=f32).astype(bf16)
    vT = lax.dot_general(wvT_ref[...], xb, _NT, preferred_element_type=f32).astype(bf16)
    vsT_ref[0, 0] = vT[:KV_COLS]
    for j in range(SEQ_TILE // Q_BLOCK):
        vwT_ref[0, j] = vT[KV_COLS:, j * Q_BLOCK:(j + 1) * Q_BLOCK]
    gT_ref[0] = lax.dot_general(wgT_ref[...], xb, _NT, preferred_element_type=f32)


def _even_in_proj(x, w_in):
    b, s, d = x.shape
    bf16 = jnp.bfloat16
    c = np.cumsum((0, CONV_CH, CONV_CH, QCOLS, KV_COLS, KV_COLS, KV_COLS, KV_COLS, KV_COLS, KV_COLS, GATE_ROWS))
    col = lambda i, j: w_in[:, c[i]:c[j]]
    wa = col(0, 2).astype(bf16)
    wq_t = col(2, 3).T.astype(bf16)
    wkv = col(3, 5).astype(bf16)
    wk2 = jnp.concatenate([col(5, 6), col(7, 8)], axis=1).astype(bf16)
    wv_t = jnp.concatenate([col(6, 7), col(8, 9)], axis=1).T.astype(bf16)
    wg_t = col(9, 10).T.astype(bf16)
    ts = SEQ_TILE
    per_sel = SEL_TILE // ts
    full = lambda shape: pl.BlockSpec(shape, lambda i, j: (0,) * len(shape))
    return pl.pallas_call(
        _even_in_kernel,
        name="even_in_proj",
        grid=(b, s // ts),
        in_specs=[pl.BlockSpec((1, ts, d), lambda i, j: (i, j, 0)),
                  full(wa.shape), full(wkv.shape), full(wk2.shape), full(wq_t.shape), full(wv_t.shape), full(wg_t.shape)],
        out_specs=[pl.BlockSpec((1, ts, CONV_CH), lambda i, j: (i, j, 0)),
                   pl.BlockSpec((1, ts, 2 * KV_COLS), lambda i, j: (i, j, 0)),
                   pl.BlockSpec((1, ts, KV_COLS), lambda i, j: (i, j, 0)),
                   pl.BlockSpec((1, ts, KV_COLS), lambda i, j: (i, j, 0)),
                   pl.BlockSpec((1, QCOLS, ts), lambda i, j: (i, 0, j)),
                   pl.BlockSpec((1, 1, KV_COLS, ts), lambda i, j: (i, j // per_sel, 0, j % per_sel)),
                   pl.BlockSpec((1, ts // Q_BLOCK, KV_COLS, Q_BLOCK), lambda i, j: (i, j, 0, 0)),
                   pl.BlockSpec((1, GATE_ROWS, ts), lambda i, j: (i, 0, j))],
        out_shape=[jax.ShapeDtypeStruct((b, s, CONV_CH), jnp.float32),
                   jax.ShapeDtypeStruct((b, s, 2 * KV_COLS), jnp.float32),
                   jax.ShapeDtypeStruct((b, s, KV_COLS), bf16),
                   jax.ShapeDtypeStruct((b, s, KV_COLS), bf16),
                   jax.ShapeDtypeStruct((b, QCOLS, s), bf16),
                   jax.ShapeDtypeStruct((b, s // SEL_TILE, KV_COLS, SEL_TILE), bf16),
                   jax.ShapeDtypeStruct((b, s // Q_BLOCK, KV_COLS, Q_BLOCK), bf16),
                   jax.ShapeDtypeStruct((b, GATE_ROWS, s), jnp.float32)],
        compiler_params=pltpu.CompilerParams(dimension_semantics=("arbitrary", "arbitrary"), vmem_limit_bytes=VMEM_LIMIT),
    )(x, wa, wkv, wk2, wq_t, wv_t, wg_t)


def _conv_kernel(cur_ref, halo_ref, w_ref, cb_ref, g_ref, b_ref, o_ref, ext_ref, win_ref):
    j = pl.program_id(1)
    ts = cur_ref.shape[1]
    halo = halo_ref[0]
    ext_ref[0:HALO, :] = jnp.where(j > 0, halo, jnp.zeros_like(halo))
    ext_ref[HALO:HALO + ts, :] = cur_ref[0]
    first = HALO - (CONV_WIDTH - 1)
    acc = jnp.zeros((ts, CONV_CH), jnp.float32)
    for p in range(SUBLANES):
        n_a = len(range(p, CONV_WIDTH, SUBLANES))
        rows = ts + SUBLANES * (n_a - 1)
        win_ref[0:rows, :] = ext_ref[first + p:first + p + rows, :]
        for a in range(n_a):
            k = SUBLANES * a + p
            acc = acc + w_ref[k:k + 1, :] * win_ref[SUBLANES * a:SUBLANES * a + ts, :]
    y = _ln_rows(acc + cb_ref[...], g_ref[...], b_ref[...])
    o_ref[0] = (y * jax.nn.sigmoid(y)).astype(o_ref.dtype)


def _conformer_conv(a, conv_w, conv_b, cn_g, cn_b):
    b, s, c = a.shape
    ts = SEQ_TILE
    per = ts // HALO
    row = lambda v: v.reshape(1, c)
    return pl.pallas_call(
        _conv_kernel,
        name="conformer_conv",
        grid=(b, s // ts),
        in_specs=[pl.BlockSpec((1, ts, c), lambda i, j: (i, j, 0)),
                  pl.BlockSpec((1, HALO, c), lambda i, j: (i, jnp.maximum(j * per - 1, 0), 0)),
                  pl.BlockSpec((CONV_WIDTH, c), lambda i, j: (0, 0)),
                  pl.BlockSpec((1, c), lambda i, j: (0, 0)),
                  pl.BlockSpec((1, c), lambda i, j: (0, 0)),
                  pl.BlockSpec((1, c), lambda i, j: (0, 0))],
        out_specs=pl.BlockSpec((1, ts, c), lambda i, j: (i, j, 0)),
        out_shape=jax.ShapeDtypeStruct((b, s, c), jnp.bfloat16),
        scratch_shapes=[pltpu.VMEM((HALO + ts, c), jnp.float32), pltpu.VMEM((HALO + ts, c), jnp.float32)],
        compiler_params=pltpu.CompilerParams(dimension_semantics=("arbitrary", "arbitrary")),
    )(a, a, conv_w, row(conv_b), row(cn_g), row(cn_b))


def _compress_kernel(r_ref, pe_ref, w1_ref, w2_ref, o_ref):
    f32 = jnp.float32
    bf16 = jnp.bfloat16
    r = r_ref[0, 0]
    half = r.shape[1]
    ha = jnp.dot((r + pe_ref[0, 0:1, :]).astype(bf16), w1_ref[0, :half, :], preferred_element_type=f32)
    hb = jnp.dot((r + pe_ref[0, 1:2, :]).astype(bf16), w1_ref[0, half:, :], preferred_element_type=f32)
    hb_next = jnp.concatenate([hb[1:], jnp.zeros_like(hb[0:1])], axis=0)
    hid = jax.nn.gelu(ha + hb_next).astype(bf16)
    o_ref[0, 0] = jnp.dot(hid, w2_ref[0], preferred_element_type=f32).astype(bf16)


def _compress_kv(kv_in, pe_k, w1_k, w2_k, pe_v, w1_v, w2_v):
    b, s, _ = kv_in.shape
    bf16 = jnp.bfloat16
    nch = s // CMP_STRIDE
    g = NSA_KV_HEADS
    r = kv_in.reshape(b, nch, CMP_STRIDE, 2 * g, HEAD_DIM)
    r = jnp.transpose(r, (0, 3, 1, 2, 4)).reshape(b, 2 * g, nch, CMP_STRIDE * HEAD_DIM)
    pe = jnp.stack([pe_k, pe_v]).reshape(2, 2, CMP_STRIDE * HEAD_DIM)
    w1 = jnp.stack([w1_k, w1_v]).astype(bf16)
    w2 = jnp.stack([w2_k, w2_v]).astype(bf16)
    out = pl.pallas_call(
        _compress_kernel,
        name="compress_kv",
        grid=(b, 2 * g),
        in_specs=[pl.BlockSpec((1, 1, nch, r.shape[3]), lambda i, j: (i, j, 0, 0)),
                  pl.BlockSpec((1, 2, pe.shape[2]), lambda i, j: (j // g, 0, 0)),
                  pl.BlockSpec((1,) + w1.shape[1:], lambda i, j: (j // g, 0, 0)),
                  pl.BlockSpec((1,) + w2.shape[1:], lambda i, j: (j // g, 0, 0))],
        out_specs=pl.BlockSpec((1, 1, nch, HEAD_DIM), lambda i, j: (i, j, 0, 0)),
        out_shape=jax.ShapeDtypeStruct((b, 2 * g, nch, HEAD_DIM), bf16),
        compiler_params=pltpu.CompilerParams(dimension_semantics=("arbitrary", "arbitrary")),
    )(r, pe, w1, w2)
    kc = jnp.transpose(out[:, :g], (0, 2, 1, 3)).reshape(b, nch, KV_COLS)
    vcT = jnp.transpose(out[:, g:], (0, 1, 3, 2)).reshape(b, KV_COLS, nch)
    return kc, vcT


def _proj_ln_kernel(a_ref, o_ref, x_ref, w_ref, g_ref, b_ref, y_ref):
    lhs = jnp.concatenate([a_ref[...], o_ref[...]], axis=1)
    mix = jnp.dot(lhs, w_ref[...], preferred_element_type=jnp.float32)
    y_ref[...] = _ln_rows(DN_ALPHA * x_ref[...] + mix, g_ref[...], b_ref[...])


def _proj_residual_ln(a, o, x, w, ln_g, ln_b):
    t, d = x.shape
    tm = SEQ_TILE
    return pl.pallas_call(
        _proj_ln_kernel,
        name="proj_residual_ln",
        grid=(t // tm,),
        in_specs=[pl.BlockSpec((tm, a.shape[1]), lambda i: (i, 0)),
                  pl.BlockSpec((tm, o.shape[1]), lambda i: (i, 0)),
                  pl.BlockSpec((tm, d), lambda i: (i, 0)),
                  pl.BlockSpec(w.shape, lambda i: (0, 0)),
                  pl.BlockSpec((1, d), lambda i: (0, 0)),
                  pl.BlockSpec((1, d), lambda i: (0, 0))],
        out_specs=pl.BlockSpec((tm, d), lambda i: (i, 0)),
        out_shape=jax.ShapeDtypeStruct((t, d), jnp.float32),
        compiler_params=pltpu.CompilerParams(dimension_semantics=("arbitrary",), vmem_limit_bytes=VMEM_LIMIT),
    )(a, o, x, w.astype(jnp.bfloat16), ln_g.reshape(1, d), ln_b.reshape(1, d))


def _mem_kv_kernel(m_ref, wk_ref, wv_ref, k_ref, v_ref):
    mb = m_ref[...].astype(jnp.bfloat16)
    k_ref[...] = jnp.dot(mb, wk_ref[...], preferred_element_type=jnp.float32).astype(jnp.bfloat16)
    v_ref[...] = jnp.dot(mb, wv_ref[...], preferred_element_type=jnp.float32).astype(jnp.bfloat16)


def _mem_kv(mem, wk, wv):
    b, m, d = mem.shape
    bf16 = jnp.bfloat16
    k, v = pl.pallas_call(
        _mem_kv_kernel,
        name="mem_kv",
        grid=(b,),
        in_specs=[pl.BlockSpec((m, d), lambda i: (i, 0)), pl.BlockSpec((d, d), lambda i: (0, 0)), pl.BlockSpec((d, d), lambda i: (0, 0))],
        out_specs=[pl.BlockSpec((m, d), lambda i: (i, 0)), pl.BlockSpec((m, d), lambda i: (i, 0))],
        out_shape=[jax.ShapeDtypeStruct((b * m, d), bf16), jax.ShapeDtypeStruct((b * m, d), bf16)],
        compiler_params=pltpu.CompilerParams(dimension_semantics=("arbitrary",), vmem_limit_bytes=VMEM_LIMIT),
    )(mem.reshape(b * m, d), wk.astype(bf16), wv.astype(bf16))
    return k.reshape(b, m, d), v.reshape(b, m, d)


def _xattn_kernel(x_ref, k_ref, v_ref, wq_ref, wo_ref, g_ref, b_ref, y_ref):
    f32 = jnp.float32
    bf16 = jnp.bfloat16
    x = x_ref[0]
    q = (jnp.dot(x.astype(bf16), wq_ref[...], preferred_element_type=f32).astype(bf16)
         * jnp.asarray(XA_HEAD_DIM ** -0.5, bf16))
    heads = []
    for h in range(XA_HEADS):
        cols = slice(h * XA_HEAD_DIM, (h + 1) * XA_HEAD_DIM)
        s = lax.dot_general(q[:, cols], k_ref[0, :, cols], _NT, preferred_element_type=f32)
        m = jnp.max(s, axis=1, keepdims=True)
        p = jnp.exp(s - m)
        p = p / jnp.sum(p, axis=1, keepdims=True)
        heads.append(jnp.dot(p.astype(bf16), v_ref[0, :, cols], preferred_element_type=f32).astype(bf16))
    att = jnp.concatenate(heads, axis=1)
    out = jnp.dot(att, wo_ref[...], preferred_element_type=f32)
    y_ref[0] = _ln_rows(DN_ALPHA * x + out, g_ref[...], b_ref[...])


def _xattn_sublayer(x, mem_k, mem_v, wq, wo, ln_g, ln_b):
    b, s, d = x.shape
    m = mem_k.shape[1]
    ts = SEQ_TILE
    bf16 = jnp.bfloat16
    return pl.pallas_call(
        _xattn_kernel,
        name="xattn_sublayer",
        grid=(b, s // ts),
        in_specs=[pl.BlockSpec((1, ts, d), lambda i, j: (i, j, 0)),
                  pl.BlockSpec((1, m, d), lambda i, j: (i, 0, 0)),
                  pl.BlockSpec((1, m, d), lambda i, j: (i, 0, 0)),
                  pl.BlockSpec((d, d), lambda i, j: (0, 0)),
                  pl.BlockSpec((d, d), lambda i, j: (0, 0)),
                  pl.BlockSpec((1, d), lambda i, j: (0, 0)),
                  pl.BlockSpec((1, d), lambda i, j: (0, 0))],
        out_specs=pl.BlockSpec((1, ts, d), lambda i, j: (i, j, 0)),
        out_shape=jax.ShapeDtypeStruct((b, s, d), jnp.float32),
        compiler_params=pltpu.CompilerParams(dimension_semantics=("arbitrary", "arbitrary"), vmem_limit_bytes=VMEM_LIMIT),
    )(x, mem_k, mem_v, wq.astype(bf16), wo.astype(bf16), ln_g.reshape(1, d), ln_b.reshape(1, d))


def _odd_kernel(x_ref, halo_ref, wb_ref, wc_ref, wh_ref, cw_ref, wo_ref, g_ref, b_ref, y_ref):
    f32 = jnp.float32
    bf16 = jnp.bfloat16
    j = pl.program_id(1)
    ts = x_ref.shape[1]
    x = x_ref[0]
    xe = jnp.concatenate([halo_ref[0], x], axis=0).astype(bf16)
    u = (jnp.dot(xe, wc_ref[...], preferred_element_type=f32) * jnp.dot(xe, wh_ref[...], preferred_element_type=f32))
    row = lax.broadcasted_iota(jnp.int32, (ODD_HALO + ts, 1), 0)
    u = jnp.where((row >= ODD_HALO) | (j > 0), u, 0.0)
    conv = jnp.zeros((ts, u.shape[1]), f32)
    for k in range(SHORT_CONV_WIDTH):
        off = ODD_HALO - (SHORT_CONV_WIDTH - 1) + k
        conv = conv + cw_ref[k:k + 1, :] * u[off:off + ts, :]
    gate_b = jnp.dot(xe[ODD_HALO:], wb_ref[...], preferred_element_type=f32)
    mix = jnp.dot((gate_b * conv).astype(bf16), wo_ref[...], preferred_element_type=f32)
    y_ref[0] = _ln_rows(DN_ALPHA * x + mix, g_ref[...], b_ref[...])


def _odd_mixer_sublayer(x, w_in, conv_w, w_out, ln_g, ln_b):
    b, s, d = x.shape
    ts = SEQ_TILE
    per = ts // ODD_HALO
    bf16 = jnp.bfloat16
    wb, wc, wh = (w_in[:, i * d:(i + 1) * d].astype(bf16) for i in range(3))
    full = lambda shape: pl.BlockSpec(shape, lambda i, j: (0,) * len(shape))
    return pl.pallas_call(
        _odd_kernel,
        name="odd_mixer_sublayer",
        grid=(b, s // ts),
        in_specs=[pl.BlockSpec((1, ts, d), lambda i, j: (i, j, 0)),
                  pl.BlockSpec((1, ODD_HALO, d), lambda i, j: (i, jnp.maximum(j * per - 1, 0), 0)),
                  full((d, d)), full((d, d)), full((d, d)), full(conv_w.shape), full((d, d)), full((1, d)), full((1, d))],
        out_specs=pl.BlockSpec((1, ts, d), lambda i, j: (i, j, 0)),
        out_shape=jax.ShapeDtypeStruct((b, s, d), jnp.float32),
        compiler_params=pltpu.CompilerParams(dimension_semantics=("arbitrary", "arbitrary"), vmem_limit_bytes=VMEM_LIMIT),
    )(x, x, wb, wc, wh, conv_w, w_out.astype(bf16), ln_g.reshape(1, d), ln_b.reshape(1, d))


def _cmp_to_sel_matrix(n_cmp, n_sel):
    c0 = np.arange(n_cmp) * CMP_STRIDE
    s0 = np.arange(n_sel) * SEL_BLOCK
    ov = np.minimum(c0[:, None] + CMP_BLOCK, s0[None, :] + SEL_BLOCK) - np.maximum(c0[:, None], s0[None, :])
    return (np.clip(ov, 0, None) / CMP_BLOCK).astype(np.float32)


def _nsa_kernel(qT_ref, gT_ref, kc_ref, vcT_ref, mselT_ref, ks_ref, vsT_ref, kw_ref, vwT_ref, o_ref,
                score_ref, sel_ref, *, n_sel):
    c = pl.program_id(1)
    q0 = c * Q_BLOCK
    f32 = jnp.float32
    bf16 = jnp.bfloat16
    n_cmp_pad = kc_ref.shape[1]

    lane_q = lax.broadcasted_iota(jnp.int32, (1, QL), 1) % Q_BLOCK
    t_row = q0 + lane_q
    t_row_q = q0 + lax.broadcasted_iota(jnp.int32, (1, Q_BLOCK), 1)
    cur_q = t_row_q // SEL_BLOCK

    top_n = min(SEL_TOP_N, n_sel)
    m_iota = lax.broadcasted_iota(jnp.int32, (n_sel, Q_BLOCK), 0)
    forced = (m_iota == 0) | (m_iota == cur_q) | (m_iota == cur_q - 1)
    valid = m_iota <= cur_q
    n_comp = jnp.minimum(q0 // SEL_BLOCK + 2, n_sel)

    q_g, o_c = [], []
    for g in range(NSA_KV_HEADS):
        pieces = []
        for hg in range(HEADS_PER_KV):
            h = g * HEADS_PER_KV + hg
            qh = qT_ref[0, h * HEAD_DIM:(h + 1) * HEAD_DIM, :] * jnp.asarray(HEAD_DIM ** -0.5, bf16)
            z = jnp.zeros_like(qh)
            pieces.append(jnp.concatenate([qh, z] if g == 0 else [z, qh], axis=0))
        qTp = jnp.concatenate(pieces, axis=1)
        q_g.append(qTp)
        rows = slice(g * HEAD_DIM, (g + 1) * HEAD_DIM)

        s_c = jnp.dot(kc_ref[0], qTp, preferred_element_type=f32)
        n_iota = lax.broadcasted_iota(jnp.int32, (n_cmp_pad, QL), 0)
        mask_c = (n_iota * CMP_STRIDE + (CMP_BLOCK - 1)) <= t_row
        s_c = jnp.where(mask_c, s_c, NEG_INF)
        m_c = jnp.max(s_c, axis=0, keepdims=True)
        p_c = jnp.where(mask_c, jnp.exp(s_c - m_c), 0.0)
        l_c = jnp.sum(p_c, axis=0, keepdims=True)
        p_c = p_c * jnp.where(l_c > 0.0, 1.0 / l_c, 0.0)
        p_cb = p_c.astype(bf16)
        o_c.append(jnp.dot(vcT_ref[0], p_cb, preferred_element_type=f32)[rows])
        imp4 = jnp.dot(mselT_ref[...], p_cb, preferred_element_type=f32)
        imp = imp4[:, 0:Q_BLOCK]
        for hg in range(1, HEADS_PER_KV):
            imp = imp + imp4[:, hg * Q_BLOCK:(hg + 1) * Q_BLOCK]

        score = jnp.where(valid, jnp.where(forced, FORCE_SCORE, imp), NEG_INF)
        score_ref[...] = score

        def rank_body(mp, rank, score=score):
            row = score_ref[pl.ds(mp, 1), :]
            beats = (row > score) | ((row == score) & (mp < m_iota))
            return rank + beats.astype(jnp.int32)

        rank = lax.fori_loop(0, n_comp, rank_body, jnp.zeros((n_sel, Q_BLOCK), jnp.int32))
        sel_ref[g] = jnp.where((rank < top_n) & valid, 0.0, NEG_INF)

    def sel_tile(j, carry, causal):
        k_tile = ks_ref[0, j]
        v_tile = vsT_ref[0, j]
        blk0 = pl.multiple_of(j * BLOCKS_PER_TILE, BLOCKS_PER_TILE)
        out = []
        for g in range(NSA_KV_HEADS):
            m_i, l_i, acc = carry[g]
            s = jnp.dot(k_tile, q_g[g], preferred_element_type=f32)
            selrows = sel_ref[g, pl.ds(blk0, BLOCKS_PER_TILE), :]
            bias = jnp.concatenate(
                [jnp.broadcast_to(selrows[r:r + 1, :], (SEL_BLOCK, Q_BLOCK)) for r in range(BLOCKS_PER_TILE)], axis=0)
            if causal:
                key = j * SEL_TILE + lax.broadcasted_iota(jnp.int32, (SEL_TILE, Q_BLOCK), 0)
                bias = jnp.where(key <= t_row_q, bias, NEG_INF)
            s = s + jnp.concatenate([bias] * HEADS_PER_KV, axis=1)
            m_new = jnp.maximum(m_i, jnp.max(s, axis=0, keepdims=True))
            alpha = jnp.exp(m_i - m_new)
            p = jnp.exp(s - m_new)
            l_new = alpha * l_i + jnp.sum(p, axis=0, keepdims=True)
            pv = jnp.dot(v_tile, p.astype(bf16), preferred_element_type=f32)
            out.append((m_new, l_new, alpha * acc + pv))
        return tuple(out)

    n_full = q0 // SEL_TILE
    init = tuple((jnp.full((1, QL), NEG_INF, f32), jnp.zeros((1, QL), f32), jnp.zeros((KV_COLS, QL), f32))
                 for _ in range(NSA_KV_HEADS))
    carry = lax.fori_loop(0, n_full, functools.partial(sel_tile, causal=False), init)
    carry = sel_tile(n_full, carry, causal=True)

    start = pl.multiple_of(jnp.maximum(q0 - WINDOW, 0), Q_BLOCK)
    j0 = start // Q_BLOCK
    k_win = kw_ref[0, pl.ds(start, WIN_SPAN), :]
    key_w = start + lax.broadcasted_iota(jnp.int32, (WIN_SPAN, Q_BLOCK), 0)
    bias_w = jnp.where(key_w <= t_row_q, 0.0, NEG_INF)
    bias_w = jnp.where(key_w > t_row_q - WINDOW, bias_w, NEG_INF)
    bias_w = jnp.concatenate([bias_w] * HEADS_PER_KV, axis=1)
    for g in range(NSA_KV_HEADS):
        rows = slice(g * HEAD_DIM, (g + 1) * HEAD_DIM)
        m_s, l_s, acc_s = carry[g]
        o_s = acc_s[rows] * (1.0 / l_s)
        s_w = jnp.dot(k_win, q_g[g], preferred_element_type=f32) + bias_w
        m_w = jnp.max(s_w, axis=0, keepdims=True)
        p_w = jnp.exp(s_w - m_w)
        l_w = jnp.sum(p_w, axis=0, keepdims=True)
        p_wb = p_w.astype(bf16)
        acc_w = jnp.zeros((KV_COLS, QL), f32)
        for i in range(WIN_SPAN // Q_BLOCK):
            acc_w = acc_w + jnp.dot(vwT_ref[0, j0 + i], p_wb[i * Q_BLOCK:(i + 1) * Q_BLOCK, :], preferred_element_type=f32)
        o_w = acc_w[rows] * (1.0 / l_w)

        for pair in range(HEADS_PER_KV // 2):
            halves = []
            for hg in (2 * pair, 2 * pair + 1):
                h = g * HEADS_PER_KV + hg
                lanes = slice(hg * Q_BLOCK, (hg + 1) * Q_BLOCK)
                gate = jax.nn.sigmoid(gT_ref[0, 3 * h:3 * h + 3, :])
                halves.append(gate[0:1] * o_c[g][:, lanes] + gate[1:2] * o_s[:, lanes] + gate[2:3] * o_w[:, lanes])
            both = jnp.concatenate(halves, axis=0)
            col0 = (g * HEADS_PER_KV + 2 * pair) * HEAD_DIM
            o_ref[0, :, col0:col0 + 2 * HEAD_DIM] = both.T.astype(o_ref.dtype)


def _nsa_attention(qT, gT, kc, vcT, ks, vsT, kw, vwT):
    b, hd, s = qT.shape
    n_sel = s // SEL_BLOCK
    nc = kc.shape[1]
    n_cmp = (s - CMP_BLOCK) // CMP_STRIDE + 1
    mselT = jnp.asarray(np.pad(_cmp_to_sel_matrix(n_cmp, n_sel).T, ((0, 0), (0, nc - n_cmp))), jnp.bfloat16)
    ks4 = ks.reshape(b, s // SEL_TILE, SEL_TILE, KV_COLS)
    return pl.pallas_call(
        functools.partial(_nsa_kernel, n_sel=n_sel),
        name="nsa_attention",
        grid=(b, s // Q_BLOCK),
        in_specs=[
            pl.BlockSpec((1, hd, Q_BLOCK), lambda i, c: (i, 0, c)),
            pl.BlockSpec((1, 3 * NSA_HEADS, Q_BLOCK), lambda i, c: (i, 0, c)),
            pl.BlockSpec((1, nc, KV_COLS), lambda i, c: (i, 0, 0)),
            pl.BlockSpec((1, KV_COLS, nc), lambda i, c: (i, 0, 0)),
            pl.BlockSpec((n_sel, nc), lambda i, c: (0, 0)),
            pl.BlockSpec((1, s // SEL_TILE, SEL_TILE, KV_COLS), lambda i, c: (i, 0, 0, 0)),
            pl.BlockSpec((1, s // SEL_TILE, KV_COLS, SEL_TILE), lambda i, c: (i, 0, 0, 0)),
            pl.BlockSpec((1, s, KV_COLS), lambda i, c: (i, 0, 0)),
            pl.BlockSpec((1, s // Q_BLOCK, KV_COLS, Q_BLOCK), lambda i, c: (i, 0, 0, 0)),
        ],
        out_specs=pl.BlockSpec((1, Q_BLOCK, hd), lambda i, c: (i, c, 0)),
        out_shape=jax.ShapeDtypeStruct((b, s, hd), jnp.bfloat16),
        scratch_shapes=[pltpu.VMEM((n_sel, Q_BLOCK), jnp.float32),
                        pltpu.VMEM((NSA_KV_HEADS, n_sel, Q_BLOCK), jnp.float32)],
        compiler_params=pltpu.CompilerParams(dimension_semantics=("arbitrary", "arbitrary")),
    )(qT, gT, kc, vcT, mselT, ks4, vsT, kw, vwT)


def _router_kernel(x_ref, wT_ref, b_ref, tri_ref, route_ref, cnt_ref, run_ref):
    i = pl.program_id(0)
    f32 = jnp.float32

    @pl.when(i == 0)
    def _():
        run_ref[...] = jnp.zeros_like(run_ref)

    tm = x_ref.shape[0]
    logits = lax.dot_general(wT_ref[...], x_ref[...].astype(jnp.bfloat16), _NT, preferred_element_type=f32) + b_ref[...]
    sub = lax.broadcasted_iota(jnp.int32, (LANES, tm), 0)
    is_g = sub < N_GROUPS
    gl = jnp.where(is_g, logits, NEG_INF)
    g_max = jnp.max(gl, axis=0, keepdims=True)
    g_star = jnp.min(jnp.where(gl == g_max, sub, LANES), axis=0, keepdims=True)
    p_group = 1.0 / jnp.sum(jnp.where(is_g, jnp.exp(gl - g_max), 0.0), axis=0, keepdims=True)
    lo = N_GROUPS + g_star * EXPERTS_PER_GROUP
    in_grp = (sub >= lo) & (sub < lo + EXPERTS_PER_GROUP)
    el = jnp.where(in_grp, logits, NEG_INF)
    v1 = jnp.max(el, axis=0, keepdims=True)
    i1 = jnp.min(jnp.where(el == v1, sub, LANES), axis=0, keepdims=True)
    el2 = jnp.where(sub == i1, NEG_INF, el)
    v2 = jnp.max(el2, axis=0, keepdims=True)
    i2 = jnp.min(jnp.where(el2 == v2, sub, LANES), axis=0, keepdims=True)
    e21 = jnp.exp(v2 - v1)
    gate1 = p_group * (1.0 / (1.0 + e21))
    gate2 = p_group * (e21 / (1.0 + e21))
    oh1 = (sub == i1).astype(f32)
    oh2 = (sub == i2).astype(f32)
    both = oh1 + oh2
    before = jnp.dot(both.astype(jnp.bfloat16), tri_ref[...], preferred_element_type=f32) + run_ref[...]
    rank1 = jnp.sum(oh1 * before, axis=0, keepdims=True)
    rank2 = jnp.sum(oh2 * before, axis=0, keepdims=True)
    run_ref[...] = run_ref[...] + jnp.sum(both, axis=1, keepdims=True)
    cnt_ref[...] = run_ref[...]
    zero = jnp.zeros_like(gate1)
    route_ref[...] = jnp.concatenate(
        [gate1, gate2, (i1 - N_GROUPS).astype(f32), (i2 - N_GROUPS).astype(f32), rank1, rank2, zero, zero], axis=0)


def _moe_route(xt, wg, bg, we, be):
    t, d = xt.shape
    pad = LANES - N_GROUPS - N_EXPERTS
    w_t = jnp.concatenate([wg, we.reshape(d, N_EXPERTS), jnp.zeros((d, pad), wg.dtype)], axis=1).T
    bias = jnp.concatenate([bg, be.reshape(N_EXPERTS), jnp.zeros((pad,), bg.dtype)])[:, None]
    tri = jnp.asarray(np.triu(np.ones((ROUTE_TILE, ROUTE_TILE), np.float32), 1), jnp.bfloat16)
    return pl.pallas_call(
        _router_kernel,
        name="moe_router",
        grid=(t // ROUTE_TILE,),
        in_specs=[pl.BlockSpec((ROUTE_TILE, d), lambda i: (i, 0)),
                  pl.BlockSpec((LANES, d), lambda i: (0, 0)),
                  pl.BlockSpec((LANES, 1), lambda i: (0, 0)),
                  pl.BlockSpec((ROUTE_TILE, ROUTE_TILE), lambda i: (0, 0))],
        out_specs=[pl.BlockSpec((ROUTE_ROWS, ROUTE_TILE), lambda i: (0, i)),
                   pl.BlockSpec((LANES, 1), lambda i: (0, 0))],
        out_shape=[jax.ShapeDtypeStruct((ROUTE_ROWS, t), jnp.float32), jax.ShapeDtypeStruct((LANES, 1), jnp.float32)],
        scratch_shapes=[pltpu.VMEM((LANES, 1), jnp.float32)],
        compiler_params=pltpu.CompilerParams(dimension_semantics=("arbitrary",)),
    )(xt, w_t.astype(jnp.bfloat16), bias, tri)


def _gather_expert_kernel(te_ref, nu_ref, src_ref, src_next_ref, x_hbm, w1_ref, w3_ref, w2_ref, y_ref,
                          xbuf, sem, w1b, w3b, w2b):
    i = pl.program_id(0)
    n_used = nu_ref[0]
    used = i < n_used
    slot = i % 2
    tile = xbuf.shape[1]

    @pl.when(i == 0)
    def _():
        def body(r, c):
            pltpu.make_async_copy(x_hbm.at[pl.ds(src_ref[0, 0, r], 1)], xbuf.at[0, pl.ds(r, 1)], sem.at[0]).start()
            return c
        lax.fori_loop(0, tile, body, 0, unroll=8)

    @pl.when(i <= n_used)
    def _():
        pltpu.make_async_copy(x_hbm.at[pl.ds(0, tile)], xbuf.at[slot], sem.at[slot]).wait()

    new_expert = (i == 0) | (te_ref[i] != te_ref[jnp.maximum(i - 1, 0)])

    @pl.when(used & new_expert)
    def _():
        w1b[...] = w1_ref[0, 0].astype(jnp.bfloat16)
        w3b[...] = w3_ref[0, 0].astype(jnp.bfloat16)
        w2b[...] = w2_ref[0, 0].astype(jnp.bfloat16)

    @pl.when(used)
    def _():
        xb = xbuf[slot].astype(jnp.bfloat16)
        for r in range(tile):
            pltpu.make_async_copy(x_hbm.at[pl.ds(src_next_ref[0, 0, r], 1)], xbuf.at[1 - slot, pl.ds(r, 1)],
                                  sem.at[1 - slot]).start()
        h1 = jnp.dot(xb, w1b[...], preferred_element_type=jnp.float32)
        h3 = jnp.dot(xb, w3b[...], preferred_element_type=jnp.float32)
        a = (h1 * jax.nn.sigmoid(h1) * h3).astype(jnp.bfloat16)
        y_ref[...] = jnp.dot(a, w2b[...], preferred_element_type=jnp.float32)

    @pl.when(jnp.logical_not(used))
    def _():
        y_ref[...] = jnp.zeros_like(y_ref)


def _moe_gather_experts(xt, src_tok, tile_expert, n_used, w1, w3, w2, layer):
    t, d = xt.shape
    n_tiles = src_tok.shape[0]
    hid = w1.shape[3]

    def w_map(i, te, nu):
        return (layer, te[i], 0, 0)

    smem_tile = lambda index_map: pl.BlockSpec((1, 1, EXPERT_TILE), index_map, memory_space=pltpu.SMEM)
    grid_spec = pltpu.PrefetchScalarGridSpec(
        num_scalar_prefetch=2,
        grid=(n_tiles,),
        in_specs=[smem_tile(lambda i, te, nu: (i, 0, 0)),
                  smem_tile(lambda i, te, nu: (jnp.minimum(i + 1, n_tiles - 1), 0, 0)),
                  pl.BlockSpec(memory_space=pl.ANY),
                  pl.BlockSpec((1, 1, d, hid), w_map),
                  pl.BlockSpec((1, 1, d, hid), w_map),
                  pl.BlockSpec((1, 1, hid, d), w_map)],
        out_specs=pl.BlockSpec((EXPERT_TILE, d), lambda i, te, nu: (i, 0)),
        scratch_shapes=[pltpu.VMEM((2, EXPERT_TILE, d), jnp.float32), pltpu.SemaphoreType.DMA((2,)),
                        pltpu.VMEM((d, hid), jnp.bfloat16), pltpu.VMEM((d, hid), jnp.bfloat16),
                        pltpu.VMEM((hid, d), jnp.bfloat16)],
    )
    return pl.pallas_call(
        _gather_expert_kernel,
        name="moe_gather_experts",
        grid_spec=grid_spec,
        out_shape=jax.ShapeDtypeStruct((n_tiles * EXPERT_TILE, d), jnp.float32),
        compiler_params=pltpu.CompilerParams(dimension_semantics=("arbitrary",), vmem_limit_bytes=VMEM_LIMIT),
    )(tile_expert, n_used, src_tok, src_tok, xt, w1, w3, w2)


def _combine_kernel(dest_ref, dest_next_ref, x_ref, route_ref, g_ref, b_ref, yd_hbm, o_ref, ybuf, sem):
    i = pl.program_id(0)
    n = pl.num_programs(0)
    tm = x_ref.shape[0]
    rows = EXPERT_TOP_K * tm

    def start_tile(dref, slot):
        def body(r, c):
            for k in range(EXPERT_TOP_K):
                row = k * tm + r
                pltpu.make_async_copy(
                    yd_hbm.at[pl.ds(dref[0, 0, row], 1)], ybuf.at[slot, pl.ds(row, 1)], sem.at[slot]).start(priority=k)
            return c
        lax.fori_loop(0, tm, body, 0, unroll=8)

    slot = i % 2

    @pl.when(i == 0)
    def _():
        start_tile(dest_ref, 0)

    @pl.when(i + 1 < n)
    def _():
        start_tile(dest_next_ref, 1 - slot)

    pltpu.make_async_copy(yd_hbm.at[pl.ds(0, rows)], ybuf.at[slot], sem.at[slot]).wait()
    y1 = ybuf[slot, 0:tm, :]
    y2 = ybuf[slot, tm:rows, :]
    route = route_ref[...].T
    y = DN_ALPHA * x_ref[...] + (y1 * route[:, 0:1] + y2 * route[:, 1:2])
    o_ref[...] = _ln_rows(y, g_ref[...], b_ref[...])


def _moe_combine_ln(xt, y_disp, dest2, route, ln_g, ln_b):
    t, d = xt.shape
    n = t // ROUTE_TILE
    rows = EXPERT_TOP_K * ROUTE_TILE
    return pl.pallas_call(
        _combine_kernel,
        name="moe_combine_ln",
        grid=(n,),
        in_specs=[pl.BlockSpec((1, 1, rows), lambda i: (i, 0, 0), memory_space=pltpu.SMEM),
                  pl.BlockSpec((1, 1, rows), lambda i: (jnp.minimum(i + 1, n - 1), 0, 0), memory_space=pltpu.SMEM),
                  pl.BlockSpec((ROUTE_TILE, d), lambda i: (i, 0)),
                  pl.BlockSpec((ROUTE_ROWS, ROUTE_TILE), lambda i: (0, i)),
                  pl.BlockSpec((1, d), lambda i: (0, 0)),
                  pl.BlockSpec((1, d), lambda i: (0, 0)),
                  pl.BlockSpec(memory_space=pl.ANY)],
        out_specs=pl.BlockSpec((ROUTE_TILE, d), lambda i: (i, 0)),
        out_shape=jax.ShapeDtypeStruct((t, d), jnp.float32),
        scratch_shapes=[pltpu.VMEM((2, rows, d), jnp.float32), pltpu.SemaphoreType.DMA((2,))],
        compiler_params=pltpu.CompilerParams(dimension_semantics=("arbitrary",), vmem_limit_bytes=VMEM_LIMIT),
    )(dest2, dest2, xt, route, ln_g.reshape(1, d), ln_b.reshape(1, d), y_disp)


def _moe_sublayer(x, wg, bg, we, be, w1, w3, w2, layer, ln_g, ln_b):
    b, s, d = x.shape
    t = b * s
    xt = x.reshape(t, d)
    route, cnt = _moe_route(xt, wg, bg, we, be)
    counts = cnt[N_GROUPS:N_GROUPS + N_EXPERTS, 0].astype(jnp.int32)
    n_tiles = (t * EXPERT_TOP_K) // EXPERT_TILE + N_EXPERTS
    tiles_per = (counts + EXPERT_TILE - 1) // EXPERT_TILE
    tile_end = jnp.cumsum(tiles_per)
    pad_start = (tile_end - tiles_per) * EXPERT_TILE
    n_used = tile_end[-1:].astype(jnp.int32)
    tile_ids = jnp.minimum(jnp.arange(n_tiles), n_used[0] - 1)
    tile_expert = jnp.sum(tile_ids[:, None] >= tile_end[None, :], axis=1).astype(jnp.int32)
    experts = route[2:4].astype(jnp.int32)
    first_row = jnp.sum(jnp.where(experts[..., None] == jnp.arange(N_EXPERTS), pad_start, 0), axis=-1)
    dest = first_row + route[4:6].astype(jnp.int32)
    dest2 = jnp.swapaxes(dest.reshape(EXPERT_TOP_K, t // ROUTE_TILE, ROUTE_TILE), 0, 1)
    dest2 = dest2.reshape(t // ROUTE_TILE, 1, EXPERT_TOP_K * ROUTE_TILE)
    tok = jnp.broadcast_to(jnp.arange(t, dtype=jnp.int32), (EXPERT_TOP_K, t))
    src_tok = jnp.zeros((n_tiles * EXPERT_TILE,), jnp.int32).at[dest.reshape(-1)].set(tok.reshape(-1), unique_indices=True)
    y_disp = _moe_gather_experts(xt, src_tok.reshape(n_tiles, 1, EXPERT_TILE), tile_expert, n_used, w1, w3, w2, layer)
    return _moe_combine_ln(xt, y_disp, dest2, route, ln_g, ln_b).reshape(b, s, d)


def kernel(x, mem, mem_wk, mem_wv, ev_w_in, ev_conv_w, ev_conv_b, ev_cnorm_g, ev_cnorm_b, ev_cmp_pe_k, ev_cmp_w1_k, ev_cmp_w2_k, ev_cmp_pe_v, ev_cmp_w1_v, ev_cmp_w2_v, ev_w_out, od_w_in, od_conv_w, od_w_out, ln_mix_g, ln_mix_b, xa_wq, xa_wo, ln_xa_g, ln_xa_b, moe_wg, moe_bg, moe_we, moe_be, moe_w1, moe_w3, moe_w2, ln_ffn_g, ln_ffn_b):
    b, s, d = x.shape
    mem_k, mem_v = _mem_kv(mem, mem_wk, mem_wv)
    for layer in range(DEPTH):
        i = layer // 2
        if layer % 2 == 0:
            a, kv_in, k_sel, k_win, q_t, v_sel_t, v_win_t, gate_t = _even_in_proj(x, ev_w_in[i])
            a = _conformer_conv(a, ev_conv_w[i], ev_conv_b[i], ev_cnorm_g[i], ev_cnorm_b[i])
            k_cmp, v_cmp_t = _compress_kv(kv_in, ev_cmp_pe_k[i], ev_cmp_w1_k[i], ev_cmp_w2_k[i], ev_cmp_pe_v[i], ev_cmp_w1_v[i], ev_cmp_w2_v[i])
            o = _nsa_attention(q_t, gate_t, k_cmp, v_cmp_t, k_sel, v_sel_t, k_win, v_win_t)
            x = _proj_residual_ln(a.reshape(b * s, -1), o.reshape(b * s, -1), x.reshape(b * s, d), ev_w_out[i], ln_mix_g[layer], ln_mix_b[layer]).reshape(b, s, d)
        else:
            x = _odd_mixer_sublayer(x, od_w_in[i], od_conv_w[i], od_w_out[i], ln_mix_g[layer], ln_mix_b[layer])
        x = _xattn_sublayer(x, mem_k, mem_v, xa_wq[layer], xa_wo[layer], ln_xa_g[layer], ln_xa_b[layer])
        x = _moe_sublayer(x, moe_wg[layer], moe_bg[layer], moe_we[layer], moe_be[layer], moe_w1, moe_w3, moe_w2, layer, ln_ffn_g[layer], ln_ffn_b[layer])
    return x
```

```python
import functools

import numpy as np
import jax
import jax.numpy as jnp
from jax import lax
from jax.experimental import pallas as pl
from jax.experimental.pallas import tpu as pltpu

D_MODEL = 1024
DEPTH = 2
CONV_CH = D_MODEL // 2
CONV_WIDTH = 31
NSA_HEADS = 8
NSA_KV_HEADS = 2
HEAD_DIM = (D_MODEL // 2) // NSA_HEADS
CMP_BLOCK = 32
CMP_STRIDE = 16
SEL_BLOCK = 64
SEL_TOP_N = 16
WINDOW = 512
Q_BLOCK = 256
FORCE_SCORE = 1e4
SHORT_CONV_WIDTH = 3
XA_HEADS = 4
XA_HEAD_DIM = D_MODEL // XA_HEADS
N_GROUPS = 4
EXPERTS_PER_GROUP = 8
N_EXPERTS = N_GROUPS * EXPERTS_PER_GROUP
EXPERT_TOP_K = 2
DN_ALPHA = (2 * DEPTH) ** 0.25
LN_EPS = 1e-5
NEG_INF = -1e30
KV_COLS = NSA_KV_HEADS * HEAD_DIM
QCOLS = NSA_HEADS * HEAD_DIM
GATE_ROWS = 3 * NSA_HEADS

LANES = 128
SUBLANES = 8
HEADS_PER_KV = NSA_HEADS // NSA_KV_HEADS
QL = Q_BLOCK * HEADS_PER_KV
SEL_TILE = 1024
WIN_SPAN = WINDOW + Q_BLOCK
BLOCKS_PER_TILE = SEL_TILE // SEL_BLOCK
SEQ_TILE = 512
HALO = 32
ODD_HALO = 8
ROUTE_TILE = 512
EXPERT_TILE = 512
ROUTE_ROWS = 8
VMEM_LIMIT = 56 * 1024 * 1024

_NT = (((1,), (1,)), ((), ()))


def _ln_rows(y, g, b):
    mu = jnp.mean(y, axis=-1, keepdims=True)
    yc = y - mu
    var = jnp.mean(yc * yc, axis=-1, keepdims=True)
    return yc * lax.rsqrt(var + LN_EPS) * g + b


def _even_in_kernel(x_ref, wa_ref, wkv_ref, wk2_ref, wqT_ref, wvT_ref, wgT_ref,
                    a_ref, kv_ref, ks_ref, kw_ref, qT_ref, vsT_ref, vwT_ref, gT_ref):
    f32 = jnp.float32
    bf16 = jnp.bfloat16
    xb = x_ref[0].astype(bf16)
    av = jnp.dot(xb, wa_ref[...], preferred_element_type=f32)
    a_ref[0] = av[:, :CONV_CH] * jax.nn.sigmoid(av[:, CONV_CH:])
    kv_ref[0] = jnp.dot(xb, wkv_ref[...], preferred_element_type=f32)
    k2 = jnp.dot(xb, wk2_ref[...], preferred_element_type=f32)
    ks_ref[0] = k2[:, :KV_COLS].astype(bf16)
    kw_ref[0] = k2[:, KV_COLS:].astype(bf16)
    qT_ref[0] = lax.dot_general(wqT_ref[...], xb, _NT, preferred_element_type=f32).astype(bf16)
    vT = lax.dot_general(wvT_ref[...], xb, _NT, preferred_element_type=f32).astype(bf16)
    vsT_ref[0, 0] = vT[:KV_COLS]
    for j in range(SEQ_TILE // Q_BLOCK):
        vwT_ref[0, j] = vT[KV_COLS:, j * Q_BLOCK:(j + 1) * Q_BLOCK]
    gT_ref[0] = lax.dot_general(wgT_ref[...], xb, _NT, preferred_element_type=f32)


def _even_in_proj(x, w_in):
    b, s, d = x.shape
    bf16 = jnp.bfloat16
    c = np.cumsum((0, CONV_CH, CONV_CH, QCOLS, KV_COLS, KV_COLS, KV_COLS, KV_COLS, KV_COLS, KV_COLS, GATE_ROWS))
    col = lambda i, j: w_in[:, c[i]:c[j]]
    wa = col(0, 2).astype(bf16)
    wq_t = col(2, 3).T.astype(bf16)
    wkv = col(3, 5).astype(bf16)
    wk2 = jnp.concatenate([col(5, 6), col(7, 8)], axis=1).astype(bf16)
    wv_t = jnp.concatenate([col(6, 7), col(8, 9)], axis=1).T.astype(bf16)
    wg_t = col(9, 10).T.astype(bf16)
    ts = SEQ_TILE
    per_sel = SEL_TILE // ts
    full = lambda shape: pl.BlockSpec(shape, lambda i, j: (0,) * len(shape))
    return pl.pallas_call(
        _even_in_kernel,
        name="even_in_proj",
        grid=(b, s // ts),
        in_specs=[pl.BlockSpec((1, ts, d), lambda i, j: (i, j, 0)),
                  full(wa.shape), full(wkv.shape), full(wk2.shape), full(wq_t.shape), full(wv_t.shape), full(wg_t.shape)],
        out_specs=[pl.BlockSpec((1, ts, CONV_CH), lambda i, j: (i, j, 0)),
                   pl.BlockSpec((1, ts, 2 * KV_COLS), lambda i, j: (i, j, 0)),
                   pl.BlockSpec((1, ts, KV_COLS), lambda i, j: (i, j, 0)),
                   pl.BlockSpec((1, ts, KV_COLS), lambda i, j: (i, j, 0)),
                   pl.BlockSpec((1, QCOLS, ts), lambda i, j: (i, 0, j)),
                   pl.BlockSpec((1, 1, KV_COLS, ts), lambda i, j: (i, j // per_sel, 0, j % per_sel)),
                   pl.BlockSpec((1, ts // Q_BLOCK, KV_COLS, Q_BLOCK), lambda i, j: (i, j, 0, 0)),
                   pl.BlockSpec((1, GATE_ROWS, ts), lambda i, j: (i, 0, j))],
        out_shape=[jax.ShapeDtypeStruct((b, s, CONV_CH), jnp.float32),
                   jax.ShapeDtypeStruct((b, s, 2 * KV_COLS), jnp.float32),
                   jax.ShapeDtypeStruct((b, s, KV_COLS), bf16),
                   jax.ShapeDtypeStruct((b, s, KV_COLS), bf16),
                   jax.ShapeDtypeStruct((b, QCOLS, s), bf16),
                   jax.ShapeDtypeStruct((b, s // SEL_TILE, KV_COLS, SEL_TILE), bf16),
                   jax.ShapeDtypeStruct((b, s // Q_BLOCK, KV_COLS, Q_BLOCK), bf16),
                   jax.ShapeDtypeStruct((b, GATE_ROWS, s), jnp.float32)],
        compiler_params=pltpu.CompilerParams(dimension_semantics=("arbitrary", "arbitrary"), vmem_limit_bytes=VMEM_LIMIT),
    )(x, wa, wkv, wk2, wq_t, wv_t, wg_t)


def _conv_kernel(cur_ref, halo_ref, w_ref, cb_ref, g_ref, b_ref, o_ref, ext_ref, win_ref):
    j = pl.program_id(1)
    ts = cur_ref.shape[1]
    halo = halo_ref[0]
    ext_ref[0:HALO, :] = jnp.where(j > 0, halo, jnp.zeros_like(halo))
    ext_ref[HALO:HALO + ts, :] = cur_ref[0]
    first = HALO - (CONV_WIDTH - 1)
    acc = jnp.zeros((ts, CONV_CH), jnp.float32)
    for p in range(SUBLANES):
        n_a = len(range(p, CONV_WIDTH, SUBLANES))
        rows = ts + SUBLANES * (n_a - 1)
        win_ref[0:rows, :] = ext_ref[first + p:first + p + rows, :]
        for a in range(n_a):
            k = SUBLANES * a + p
            acc = acc + w_ref[k:k + 1, :] * win_ref[SUBLANES * a:SUBLANES * a + ts, :]
    y = _ln_rows(acc + cb_ref[...], g_ref[...], b_ref[...])
    o_ref[0] = (y * jax.nn.sigmoid(y)).astype(o_ref.dtype)


def _conformer_conv(a, conv_w, conv_b, cn_g, cn_b):
    b, s, c = a.shape
    ts = SEQ_TILE
    per = ts // HALO
    row = lambda v: v.reshape(1, c)
    return pl.pallas_call(
        _conv_kernel,
        name="conformer_conv",
        grid=(b, s // ts),
        in_specs=[pl.BlockSpec((1, ts, c), lambda i, j: (i, j, 0)),
                  pl.BlockSpec((1, HALO, c), lambda i, j: (i, jnp.maximum(j * per - 1, 0), 0)),
                  pl.BlockSpec((CONV_WIDTH, c), lambda i, j: (0, 0)),
                  pl.BlockSpec((1, c), lambda i, j: (0, 0)),
                  pl.BlockSpec((1, c), lambda i, j: (0, 0)),
                  pl.BlockSpec((1, c), lambda i, j: (0, 0))],
        out_specs=pl.BlockSpec((1, ts, c), lambda i, j: (i, j, 0)),
        out_shape=jax.ShapeDtypeStruct((b, s, c), jnp.bfloat16),
        scratch_shapes=[pltpu.VMEM((HALO + ts, c), jnp.float32), pltpu.VMEM((HALO + ts, c), jnp.float32)],
        compiler_params=pltpu.CompilerParams(dimension_semantics=("arbitrary", "arbitrary")),
    )(a, a, conv_w, row(conv_b), row(cn_g), row(cn_b))


def _compress_kernel(r_ref, pe_ref, w1_ref, w2_ref, o_ref):
    f32 = jnp.float32
    bf16 = jnp.bfloat16
    r = r_ref[0, 0]
    half = r.shape[1]
    ha = jnp.dot((r + pe_ref[0, 0:1, :]).astype(bf16), w1_ref[0, :half, :], preferred_element_type=f32)
    hb = jnp.dot((r + pe_ref[0, 1:2, :]).astype(bf16), w1_ref[0, half:, :], preferred_element_type=f32)
    hb_next = jnp.concatenate([hb[1:], jnp.zeros_like(hb[0:1])], axis=0)
    hid = jax.nn.gelu(ha + hb_next).astype(bf16)
    o_ref[0, 0] = jnp.dot(hid, w2_ref[0], preferred_element_type=f32).astype(bf16)


def _compress_kv(kv_in, pe_k, w1_k, w2_k, pe_v, w1_v, w2_v):
    b, s, _ = kv_in.shape
    bf16 = jnp.bfloat16
    nch = s // CMP_STRIDE
    g = NSA_KV_HEADS
    r = kv_in.reshape(b, nch, CMP_STRIDE, 2 * g, HEAD_DIM)
    r = jnp.transpose(r, (0, 3, 1, 2, 4)).reshape(b, 2 * g, nch, CMP_STRIDE * HEAD_DIM)
    pe = jnp.stack([pe_k, pe_v]).reshape(2, 2, CMP_STRIDE * HEAD_DIM)
    w1 = jnp.stack([w1_k, w1_v]).astype(bf16)
    w2 = jnp.stack([w2_k, w2_v]).astype(bf16)
    out = pl.pallas_call(
        _compress_kernel,
        name="compress_kv",
        grid=(b, 2 * g),
        in_specs=[pl.BlockSpec((1, 1, nch, r.shape[3]), lambda i, j: (i, j, 0, 0)),
                  pl.BlockSpec((1, 2, pe.shape[2]), lambda i, j: (j // g, 0, 0)),
                  pl.BlockSpec((1,) + w1.shape[1:], lambda i, j: (j // g, 0, 0)),
                  pl.BlockSpec((1,) + w2.shape[1:], lambda i, j: (j // g, 0, 0))],
        out_specs=pl.BlockSpec((1, 1, nch, HEAD_DIM), lambda i, j: (i, j, 0, 0)),
        out_shape=jax.ShapeDtypeStruct((b, 2 * g, nch, HEAD_DIM), bf16),
        compiler_params=pltpu.CompilerParams(dimension_semantics=("arbitrary", "arbitrary")),
    )(r, pe, w1, w2)
    kc = jnp.transpose(out[:, :g], (0, 2, 1, 3)).reshape(b, nch, KV_COLS)
    vcT = jnp.transpose(out[:, g:], (0, 1, 3, 2)).reshape(b, KV_COLS, nch)
    return kc, vcT


def _proj_ln_kernel(a_ref, o_ref, x_ref, w_ref, g_ref, b_ref, y_ref):
    lhs = jnp.concatenate([a_ref[...], o_ref[...]], axis=1)
    mix = jnp.dot(lhs, w_ref[...], preferred_element_type=jnp.float32)
    y_ref[...] = _ln_rows(DN_ALPHA * x_ref[...] + mix, g_ref[...], b_ref[...])


def _proj_residual_ln(a, o, x, w, ln_g, ln_b):
    t, d = x.shape
    tm = SEQ_TILE
    return pl.pallas_call(
        _proj_ln_kernel,
        name="proj_residual_ln",
        grid=(t // tm,),
        in_specs=[pl.BlockSpec((tm, a.shape[1]), lambda i: (i, 0)),
                  pl.BlockSpec((tm, o.shape[1]), lambda i: (i, 0)),
                  pl.BlockSpec((tm, d), lambda i: (i, 0)),
                  pl.BlockSpec(w.shape, lambda i: (0, 0)),
                  pl.BlockSpec((1, d), lambda i: (0, 0)),
                  pl.BlockSpec((1, d), lambda i: (0, 0))],
        out_specs=pl.BlockSpec((tm, d), lambda i: (i, 0)),
        out_shape=jax.ShapeDtypeStruct((t, d), jnp.float32),
        compiler_params=pltpu.CompilerParams(dimension_semantics=("arbitrary",), vmem_limit_bytes=VMEM_LIMIT),
    )(a, o, x, w.astype(jnp.bfloat16), ln_g.reshape(1, d), ln_b.reshape(1, d))


def _mem_kv_kernel(m_ref, wk_ref, wv_ref, k_ref, v_ref):
    mb = m_ref[...].astype(jnp.bfloat16)
    k_ref[...] = jnp.dot(mb, wk_ref[...], preferred_element_type=jnp.float32).astype(jnp.bfloat16)
    v_ref[...] = jnp.dot(mb, wv_ref[...], preferred_element_type=jnp.float32).astype(jnp.bfloat16)


def _mem_kv(mem, wk, wv):
    b, m, d = mem.shape
    bf16 = jnp.bfloat16
    k, v = pl.pallas_call(
        _mem_kv_kernel,
        name="mem_kv",
        grid=(b,),
        in_specs=[pl.BlockSpec((m, d), lambda i: (i, 0)), pl.BlockSpec((d, d), lambda i: (0, 0)), pl.BlockSpec((d, d), lambda i: (0, 0))],
        out_specs=[pl.BlockSpec((m, d), lambda i: (i, 0)), pl.BlockSpec((m, d), lambda i: (i, 0))],
        out_shape=[jax.ShapeDtypeStruct((b * m, d), bf16), jax.ShapeDtypeStruct((b * m, d), bf16)],
        compiler_params=pltpu.CompilerParams(dimension_semantics=("arbitrary",), vmem_limit_bytes=VMEM_LIMIT),
    )(mem.reshape(b * m, d), wk.astype(bf16), wv.astype(bf16))
    return k.reshape(b, m, d), v.reshape(b, m, d)


def _xattn_kernel(x_ref, k_ref, v_ref, wq_ref, wo_ref, g_ref, b_ref, y_ref):
    f32 = jnp.float32
    bf16 = jnp.bfloat16
    x = x_ref[0]
    q = (jnp.dot(x.astype(bf16), wq_ref[...], preferred_element_type=f32).astype(bf16)
         * jnp.asarray(XA_HEAD_DIM ** -0.5, bf16))
    heads = []
    for h in range(XA_HEADS):
        cols = slice(h * XA_HEAD_DIM, (h + 1) * XA_HEAD_DIM)
        s = lax.dot_general(q[:, cols], k_ref[0, :, cols], _NT, preferred_element_type=f32)
        m = jnp.max(s, axis=1, keepdims=True)
        p = jnp.exp(s - m)
        p = p / jnp.sum(p, axis=1, keepdims=True)
        heads.append(jnp.dot(p.astype(bf16), v_ref[0, :, cols], preferred_element_type=f32).astype(bf16))
    att = jnp.concatenate(heads, axis=1)
    out = jnp.dot(att, wo_ref[...], preferred_element_type=f32)
    y_ref[0] = _ln_rows(DN_ALPHA * x + out, g_ref[...], b_ref[...])


def _xattn_sublayer(x, mem_k, mem_v, wq, wo, ln_g, ln_b):
    b, s, d = x.shape
    m = mem_k.shape[1]
    ts = SEQ_TILE
    bf16 = jnp.bfloat16
    return pl.pallas_call(
        _xattn_kernel,
        name="xattn_sublayer",
        grid=(b, s // ts),
        in_specs=[pl.BlockSpec((1, ts, d), lambda i, j: (i, j, 0)),
                  pl.BlockSpec((1, m, d), lambda i, j: (i, 0, 0)),
                  pl.BlockSpec((1, m, d), lambda i, j: (i, 0, 0)),
                  pl.BlockSpec((d, d), lambda i, j: (0, 0)),
                  pl.BlockSpec((d, d), lambda i, j: (0, 0)),
                  pl.BlockSpec((1, d), lambda i, j: (0, 0)),
                  pl.BlockSpec((1, d), lambda i, j: (0, 0))],
        out_specs=pl.BlockSpec((1, ts, d), lambda i, j: (i, j, 0)),
        out_shape=jax.ShapeDtypeStruct((b, s, d), jnp.float32),
        compiler_params=pltpu.CompilerParams(dimension_semantics=("arbitrary", "arbitrary"), vmem_limit_bytes=VMEM_LIMIT),
    )(x, mem_k, mem_v, wq.astype(bf16), wo.astype(bf16), ln_g.reshape(1, d), ln_b.reshape(1, d))


def _odd_kernel(x_ref, halo_ref, wb_ref, wc_ref, wh_ref, cw_ref, wo_ref, g_ref, b_ref, y_ref):
    f32 = jnp.float32
    bf16 = jnp.bfloat16
    j = pl.program_id(1)
    ts = x_ref.shape[1]
    x = x_ref[0]
    xe = jnp.concatenate([halo_ref[0], x], axis=0).astype(bf16)
    u = (jnp.dot(xe, wc_ref[...], preferred_element_type=f32) * jnp.dot(xe, wh_ref[...], preferred_element_type=f32))
    row = lax.broadcasted_iota(jnp.int32, (ODD_HALO + ts, 1), 0)
    u = jnp.where((row >= ODD_HALO) | (j > 0), u, 0.0)
    conv = jnp.zeros((ts, u.shape[1]), f32)
    for k in range(SHORT_CONV_WIDTH):
        off = ODD_HALO - (SHORT_CONV_WIDTH - 1) + k
        conv = conv + cw_ref[k:k + 1, :] * u[off:off + ts, :]
    gate_b = jnp.dot(xe[ODD_HALO:], wb_ref[...], preferred_element_type=f32)
    mix = jnp.dot((gate_b * conv).astype(bf16), wo_ref[...], preferred_element_type=f32)
    y_ref[0] = _ln_rows(DN_ALPHA * x + mix, g_ref[...], b_ref[...])


def _odd_mixer_sublayer(x, w_in, conv_w, w_out, ln_g, ln_b):
    b, s, d = x.shape
    ts = SEQ_TILE
    per = ts // ODD_HALO
    bf16 = jnp.bfloat16
    wb, wc, wh = (w_in[:, i * d:(i + 1) * d].astype(bf16) for i in range(3))
    full = lambda shape: pl.BlockSpec(shape, lambda i, j: (0,) * len(shape))
    return pl.pallas_call(
        _odd_kernel,
        name="odd_mixer_sublayer",
        grid=(b, s // ts),
        in_specs=[pl.BlockSpec((1, ts, d), lambda i, j: (i, j, 0)),
                  pl.BlockSpec((1, ODD_HALO, d), lambda i, j: (i, jnp.maximum(j * per - 1, 0), 0)),
                  full((d, d)), full((d, d)), full((d, d)), full(conv_w.shape), full((d, d)), full((1, d)), full((1, d))],
        out_specs=pl.BlockSpec((1, ts, d), lambda i, j: (i, j, 0)),
        out_shape=jax.ShapeDtypeStruct((b, s, d), jnp.float32),
        compiler_params=pltpu.CompilerParams(dimension_semantics=("arbitrary", "arbitrary"), vmem_limit_bytes=VMEM_LIMIT),
    )(x, x, wb, wc, wh, conv_w, w_out.astype(bf16), ln_g.reshape(1, d), ln_b.reshape(1, d))


def _cmp_to_sel_matrix(n_cmp, n_sel):
    c0 = np.arange(n_cmp) * CMP_STRIDE
    s0 = np.arange(n_sel) * SEL_BLOCK
    ov = np.minimum(c0[:, None] + CMP_BLOCK, s0[None, :] + SEL_BLOCK) - np.maximum(c0[:, None], s0[None, :])
    return (np.clip(ov, 0, None) / CMP_BLOCK).astype(np.float32)


def _nsa_kernel(qT_ref, gT_ref, kc_ref, vcT_ref, mselT_ref, ks_ref, vsT_ref, kw_ref, vwT_ref, o_ref,
                score_ref, sel_ref, *, n_sel):
    c = pl.program_id(1)
    q0 = c * Q_BLOCK
    f32 = jnp.float32
    bf16 = jnp.bfloat16
    n_cmp_pad = kc_ref.shape[1]

    lane_q = lax.broadcasted_iota(jnp.int32, (1, QL), 1) % Q_BLOCK
    t_row = q0 + lane_q
    t_row_q = q0 + lax.broadcasted_iota(jnp.int32, (1, Q_BLOCK), 1)
    cur_q = t_row_q // SEL_BLOCK

    top_n = min(SEL_TOP_N, n_sel)
    m_iota = lax.broadcasted_iota(jnp.int32, (n_sel, Q_BLOCK), 0)
    forced = (m_iota == 0) | (m_iota == cur_q) | (m_iota == cur_q - 1)
    valid = m_iota <= cur_q
    n_comp = jnp.minimum(q0 // SEL_BLOCK + Q_BLOCK // SEL_BLOCK, n_sel)

    q_g, o_c = [], []
    for g in range(NSA_KV_HEADS):
        pieces = []
        for hg in range(HEADS_PER_KV):
            h = g * HEADS_PER_KV + hg
            qh = qT_ref[0, h * HEAD_DIM:(h + 1) * HEAD_DIM, :] * jnp.asarray(HEAD_DIM ** -0.5, bf16)
            z = jnp.zeros_like(qh)
            pieces.append(jnp.concatenate([qh, z] if g == 0 else [z, qh], axis=0))
        qTp = jnp.concatenate(pieces, axis=1)
        q_g.append(qTp)
        rows = slice(g * HEAD_DIM, (g + 1) * HEAD_DIM)

        s_c = jnp.dot(kc_ref[0], qTp, preferred_element_type=f32)
        n_iota = lax.broadcasted_iota(jnp.int32, (n_cmp_pad, QL), 0)
        mask_c = (n_iota * CMP_STRIDE + (CMP_BLOCK - 1)) <= t_row
        s_c = jnp.where(mask_c, s_c, NEG_INF)
        m_c = jnp.max(s_c, axis=0, keepdims=True)
        p_c = jnp.where(mask_c, jnp.exp(s_c - m_c), 0.0)
        l_c = jnp.sum(p_c, axis=0, keepdims=True)
        p_c = p_c * jnp.where(l_c > 0.0, 1.0 / l_c, 0.0)
        p_cb = p_c.astype(bf16)
        o_c.append(jnp.dot(vcT_ref[0], p_cb, preferred_element_type=f32)[rows])
        imp4 = jnp.dot(mselT_ref[...], p_cb, preferred_element_type=f32)
        imp = imp4[:, 0:Q_BLOCK]
        for hg in range(1, HEADS_PER_KV):
            imp = imp + imp4[:, hg * Q_BLOCK:(hg + 1) * Q_BLOCK]

        score = jnp.where(valid, jnp.where(forced, FORCE_SCORE, imp), NEG_INF)
        score_ref[...] = score

        def rank_body(i8, rank, score=score):
            base = pl.multiple_of(i8 * SUBLANES, SUBLANES)
            rows = score_ref[pl.ds(base, SUBLANES), :]
            for u in range(SUBLANES):
                row = rows[u:u + 1, :]
                beats = (row > score) | ((row == score) & (base + u < m_iota))
                rank = rank + beats.astype(jnp.int32)
            return rank

        rank = lax.fori_loop(0, (n_comp + SUBLANES - 1) // SUBLANES, rank_body,
                             jnp.zeros((n_sel, Q_BLOCK), jnp.int32))
        sel_ref[g] = jnp.where((rank < top_n) & valid, 0.0, NEG_INF)

    def sel_tile(j, carry, causal):
        k_tile = ks_ref[0, j]
        v_tile = vsT_ref[0, j]
        blk0 = pl.multiple_of(j * BLOCKS_PER_TILE, BLOCKS_PER_TILE)
        out = []
        for g in range(NSA_KV_HEADS):
            m_i, l_i, acc = carry[g]
            s = jnp.dot(k_tile, q_g[g], preferred_element_type=f32)
            selrows = sel_ref[g, pl.ds(blk0, BLOCKS_PER_TILE), :]
            bias = jnp.concatenate(
                [jnp.broadcast_to(selrows[r:r + 1, :], (SEL_BLOCK, Q_BLOCK)) for r in range(BLOCKS_PER_TILE)], axis=0)
            if causal:
                key = j * SEL_TILE + lax.broadcasted_iota(jnp.int32, (SEL_TILE, Q_BLOCK), 0)
                bias = jnp.where(key <= t_row_q, bias, NEG_INF)
            s = s + jnp.concatenate([bias] * HEADS_PER_KV, axis=1)
            m_new = jnp.maximum(m_i, jnp.max(s, axis=0, keepdims=True))
            alpha = jnp.exp(m_i - m_new)
            p = jnp.exp(s - m_new)
            l_new = alpha * l_i + jnp.sum(p, axis=0, keepdims=True)
            pv = jnp.dot(v_tile, p.astype(bf16), preferred_element_type=f32)
            out.append((m_new, l_new, alpha * acc + pv))
        return tuple(out)

    n_full = q0 // SEL_TILE
    init = tuple((jnp.full((1, QL), NEG_INF, f32), jnp.zeros((1, QL), f32), jnp.zeros((KV_COLS, QL), f32))
                 for _ in range(NSA_KV_HEADS))
    carry = lax.fori_loop(0, n_full, functools.partial(sel_tile, causal=False), init)
    carry = sel_tile(n_full, carry, causal=True)

    start = pl.multiple_of(jnp.maximum(q0 - WINDOW, 0), Q_BLOCK)
    j0 = start // Q_BLOCK
    k_win = kw_ref[0, pl.ds(start, WIN_SPAN), :]
    key_w = start + lax.broadcasted_iota(jnp.int32, (WIN_SPAN, Q_BLOCK), 0)
    bias_w = jnp.where(key_w <= t_row_q, 0.0, NEG_INF)
    bias_w = jnp.where(key_w > t_row_q - WINDOW, bias_w, NEG_INF)
    bias_w = jnp.concatenate([bias_w] * HEADS_PER_KV, axis=1)
    for g in range(NSA_KV_HEADS):
        rows = slice(g * HEAD_DIM, (g + 1) * HEAD_DIM)
        m_s, l_s, acc_s = carry[g]
        o_s = acc_s[rows] * (1.0 / l_s)
        s_w = jnp.dot(k_win, q_g[g], preferred_element_type=f32) + bias_w
        m_w = jnp.max(s_w, axis=0, keepdims=True)
        p_w = jnp.exp(s_w - m_w)
        l_w = jnp.sum(p_w, axis=0, keepdims=True)
        p_wb = p_w.astype(bf16)
        acc_w = jnp.zeros((KV_COLS, QL), f32)
        for i in range(WIN_SPAN // Q_BLOCK):
            acc_w = acc_w + jnp.dot(vwT_ref[0, j0 + i], p_wb[i * Q_BLOCK:(i + 1) * Q_BLOCK, :], preferred_element_type=f32)
        o_w = acc_w[rows] * (1.0 / l_w)

        for pair in range(HEADS_PER_KV // 2):
            halves = []
            for hg in (2 * pair, 2 * pair + 1):
                h = g * HEADS_PER_KV + hg
                lanes = slice(hg * Q_BLOCK, (hg + 1) * Q_BLOCK)
                gate = jax.nn.sigmoid(gT_ref[0, 3 * h:3 * h + 3, :])
                halves.append(gate[0:1] * o_c[g][:, lanes] + gate[1:2] * o_s[:, lanes] + gate[2:3] * o_w[:, lanes])
            both = jnp.concatenate(halves, axis=0)
            col0 = (g * HEADS_PER_KV + 2 * pair) * HEAD_DIM
            o_ref[0, :, col0:col0 + 2 * HEAD_DIM] = both.T.astype(o_ref.dtype)


def _nsa_attention(qT, gT, kc, vcT, ks, vsT, kw, vwT):
    b, hd, s = qT.shape
    n_sel = s // SEL_BLOCK
    nc = kc.shape[1]
    n_cmp = (s - CMP_BLOCK) // CMP_STRIDE + 1
    mselT = jnp.asarray(np.pad(_cmp_to_sel_matrix(n_cmp, n_sel).T, ((0, 0), (0, nc - n_cmp))), jnp.bfloat16)
    ks4 = ks.reshape(b, s // SEL_TILE, SEL_TILE, KV_COLS)
    return pl.pallas_call(
        functools.partial(_nsa_kernel, n_sel=n_sel),
        name="nsa_attention",
        grid=(b, s // Q_BLOCK),
        in_specs=[
            pl.BlockSpec((1, hd, Q_BLOCK), lambda i, c: (i, 0, c)),
            pl.BlockSpec((1, 3 * NSA_HEADS, Q_BLOCK), lambda i, c: (i, 0, c)),
            pl.BlockSpec((1, nc, KV_COLS), lambda i, c: (i, 0, 0)),
            pl.BlockSpec((1, KV_COLS, nc), lambda i, c: (i, 0, 0)),
            pl.BlockSpec((n_sel, nc), lambda i, c: (0, 0)),
            pl.BlockSpec((1, s // SEL_TILE, SEL_TILE, KV_COLS), lambda i, c: (i, 0, 0, 0)),
            pl.BlockSpec((1, s // SEL_TILE, KV_COLS, SEL_TILE), lambda i, c: (i, 0, 0, 0)),
            pl.BlockSpec((1, s, KV_COLS), lambda i, c: (i, 0, 0)),
            pl.BlockSpec((1, s // Q_BLOCK, KV_COLS, Q_BLOCK), lambda i, c: (i, 0, 0, 0)),
        ],
        out_specs=pl.BlockSpec((1, Q_BLOCK, hd), lambda i, c: (i, c, 0)),
        out_shape=jax.ShapeDtypeStruct((b, s, hd), jnp.bfloat16),
        scratch_shapes=[pltpu.VMEM((n_sel, Q_BLOCK), jnp.float32),
                        pltpu.VMEM((NSA_KV_HEADS, n_sel, Q_BLOCK), jnp.float32)],
        compiler_params=pltpu.CompilerParams(dimension_semantics=("arbitrary", "arbitrary")),
    )(qT, gT, kc, vcT, mselT, ks4, vsT, kw, vwT)


def _router_kernel(x_ref, wT_ref, b_ref, tri_ref, route_ref, cnt_ref, run_ref):
    i = pl.program_id(0)
    f32 = jnp.float32

    @pl.when(i == 0)
    def _():
        run_ref[...] = jnp.zeros_like(run_ref)

    tm = x_ref.shape[0]
    logits = lax.dot_general(wT_ref[...], x_ref[...].astype(jnp.bfloat16), _NT, preferred_element_type=f32) + b_ref[...]
    sub = lax.broadcasted_iota(jnp.int32, (LANES, tm), 0)
    is_g = sub < N_GROUPS
    gl = jnp.where(is_g, logits, NEG_INF)
    g_max = jnp.max(gl, axis=0, keepdims=True)
    g_star = jnp.min(jnp.where(gl == g_max, sub, LANES), axis=0, keepdims=True)
    p_group = 1.0 / jnp.sum(jnp.where(is_g, jnp.exp(gl - g_max), 0.0), axis=0, keepdims=True)
    lo = N_GROUPS + g_star * EXPERTS_PER_GROUP
    in_grp = (sub >= lo) & (sub < lo + EXPERTS_PER_GROUP)
    el = jnp.where(in_grp, logits, NEG_INF)
    v1 = jnp.max(el, axis=0, keepdims=True)
    i1 = jnp.min(jnp.where(el == v1, sub, LANES), axis=0, keepdims=True)
    el2 = jnp.where(sub == i1, NEG_INF, el)
    v2 = jnp.max(el2, axis=0, keepdims=True)
    i2 = jnp.min(jnp.where(el2 == v2, sub, LANES), axis=0, keepdims=True)
    e21 = jnp.exp(v2 - v1)
    gate1 = p_group * (1.0 / (1.0 + e21))
    gate2 = p_group * (e21 / (1.0 + e21))
    oh1 = (sub == i1).astype(f32)
    oh2 = (sub == i2).astype(f32)
    both = oh1 + oh2
    before = jnp.dot(both.astype(jnp.bfloat16), tri_ref[...], preferred_element_type=f32) + run_ref[...]
    rank1 = jnp.sum(oh1 * before, axis=0, keepdims=True)
    rank2 = jnp.sum(oh2 * before, axis=0, keepdims=True)
    run_ref[...] = run_ref[...] + jnp.sum(both, axis=1, keepdims=True)
    cnt_ref[...] = run_ref[...]
    zero = jnp.zeros_like(gate1)
    route_ref[...] = jnp.concatenate(
        [gate1, gate2, (i1 - N_GROUPS).astype(f32), (i2 - N_GROUPS).astype(f32), rank1, rank2, zero, zero], axis=0)


def _moe_route(xt, wg, bg, we, be):
    t, d = xt.shape
    pad = LANES - N_GROUPS - N_EXPERTS
    w_t = jnp.concatenate([wg, we.reshape(d, N_EXPERTS), jnp.zeros((d, pad), wg.dtype)], axis=1).T
    bias = jnp.concatenate([bg, be.reshape(N_EXPERTS), jnp.zeros((pad,), bg.dtype)])[:, None]
    tri = jnp.asarray(np.triu(np.ones((ROUTE_TILE, ROUTE_TILE), np.float32), 1), jnp.bfloat16)
    return pl.pallas_call(
        _router_kernel,
        name="moe_router",
        grid=(t // ROUTE_TILE,),
        in_specs=[pl.BlockSpec((ROUTE_TILE, d), lambda i: (i, 0)),
                  pl.BlockSpec((LANES, d), lambda i: (0, 0)),
                  pl.BlockSpec((LANES, 1), lambda i: (0, 0)),
                  pl.BlockSpec((ROUTE_TILE, ROUTE_TILE), lambda i: (0, 0))],
        out_specs=[pl.BlockSpec((ROUTE_ROWS, ROUTE_TILE), lambda i: (0, i)),
                   pl.BlockSpec((LANES, 1), lambda i: (0, 0))],
        out_shape=[jax.ShapeDtypeStruct((ROUTE_ROWS, t), jnp.float32), jax.ShapeDtypeStruct((LANES, 1), jnp.float32)],
        scratch_shapes=[pltpu.VMEM((LANES, 1), jnp.float32)],
        compiler_params=pltpu.CompilerParams(dimension_semantics=("arbitrary",)),
    )(xt, w_t.astype(jnp.bfloat16), bias, tri)


def _dispatch_kernel(dest_ref, x_ref, zero_hbm, xd_hbm, sem):
    del zero_hbm
    tm = x_ref.shape[0]

    def body(r, c):
        for k in range(EXPERT_TOP_K):
            pltpu.make_async_copy(
                x_ref.at[pl.ds(r, 1)], xd_hbm.at[pl.ds(dest_ref[0, 0, k * tm + r], 1)], sem).start(priority=k)
        return c

    lax.fori_loop(0, tm, body, 0, unroll=8)
    for k in range(EXPERT_TOP_K):
        pltpu.make_async_copy(x_ref, xd_hbm.at[pl.ds(0, tm)], sem).wait()


def _moe_dispatch(xt, dest2, n_rows):
    t, d = xt.shape
    return pl.pallas_call(
        _dispatch_kernel,
        name="moe_dispatch",
        grid=(t // ROUTE_TILE,),
        in_specs=[pl.BlockSpec((1, 1, EXPERT_TOP_K * ROUTE_TILE), lambda i: (i, 0, 0), memory_space=pltpu.SMEM),
                  pl.BlockSpec((ROUTE_TILE, d), lambda i: (i, 0)),
                  pl.BlockSpec(memory_space=pl.ANY)],
        out_specs=pl.BlockSpec(memory_space=pl.ANY),
        out_shape=jax.ShapeDtypeStruct((n_rows, d), xt.dtype),
        scratch_shapes=[pltpu.SemaphoreType.DMA(())],
        input_output_aliases={2: 0},
        compiler_params=pltpu.CompilerParams(dimension_semantics=("arbitrary",), has_side_effects=True),
    )(dest2, xt, jnp.zeros((n_rows, d), xt.dtype))


def _expert_kernel(te_ref, nu_ref, xd_ref, w1_ref, w3_ref, w2_ref, y_ref, w1b, w3b, w2b):
    i = pl.program_id(0)
    used = i < nu_ref[0]
    new_expert = (i == 0) | (te_ref[i] != te_ref[jnp.maximum(i - 1, 0)])

    @pl.when(used & new_expert)
    def _():
        w1b[...] = w1_ref[0, 0].astype(jnp.bfloat16)
        w3b[...] = w3_ref[0, 0].astype(jnp.bfloat16)
        w2b[...] = w2_ref[0, 0].astype(jnp.bfloat16)

    @pl.when(used)
    def _():
        xb = xd_ref[...].astype(jnp.bfloat16)
        h1 = jnp.dot(xb, w1b[...], preferred_element_type=jnp.float32)
        h3 = jnp.dot(xb, w3b[...], preferred_element_type=jnp.float32)
        a = (h1 * jax.nn.sigmoid(h1) * h3).astype(jnp.bfloat16)
        y_ref[...] = jnp.dot(a, w2b[...], preferred_element_type=jnp.float32)

    @pl.when(jnp.logical_not(used))
    def _():
        y_ref[...] = jnp.zeros_like(y_ref)


def _moe_experts(x_disp, tile_expert, n_used, w1, w3, w2, layer):
    n_rows, d = x_disp.shape
    n_tiles = n_rows // EXPERT_TILE
    hid = w1.shape[3]

    def row_map(i, te, nu):
        return (i, 0)

    def w_map(i, te, nu):
        return (layer, te[i], 0, 0)

    grid_spec = pltpu.PrefetchScalarGridSpec(
        num_scalar_prefetch=2,
        grid=(n_tiles,),
        in_specs=[pl.BlockSpec((EXPERT_TILE, d), row_map),
                  pl.BlockSpec((1, 1, d, hid), w_map),
                  pl.BlockSpec((1, 1, d, hid), w_map),
                  pl.BlockSpec((1, 1, hid, d), w_map)],
        out_specs=pl.BlockSpec((EXPERT_TILE, d), row_map),
        scratch_shapes=[pltpu.VMEM((d, hid), jnp.bfloat16), pltpu.VMEM((d, hid), jnp.bfloat16),
                        pltpu.VMEM((hid, d), jnp.bfloat16)],
    )
    return pl.pallas_call(
        _expert_kernel,
        name="moe_experts",
        grid_spec=grid_spec,
        out_shape=jax.ShapeDtypeStruct((n_rows, d), jnp.float32),
        compiler_params=pltpu.CompilerParams(dimension_semantics=("arbitrary",), vmem_limit_bytes=VMEM_LIMIT),
    )(tile_expert, n_used, x_disp, w1, w3, w2)


def _combine_kernel(dest_ref, dest_next_ref, x_ref, route_ref, g_ref, b_ref, yd_hbm, o_ref, ybuf, sem):
    i = pl.program_id(0)
    n = pl.num_programs(0)
    tm = x_ref.shape[0]
    rows = EXPERT_TOP_K * tm

    def start_tile(dref, slot):
        def body(r, c):
            for k in range(EXPERT_TOP_K):
                row = k * tm + r
                pltpu.make_async_copy(
                    yd_hbm.at[pl.ds(dref[0, 0, row], 1)], ybuf.at[slot, pl.ds(row, 1)], sem.at[slot]).start(priority=k)
            return c
        lax.fori_loop(0, tm, body, 0, unroll=8)

    slot = i % 2

    @pl.when(i == 0)
    def _():
        start_tile(dest_ref, 0)

    @pl.when(i + 1 < n)
    def _():
        start_tile(dest_next_ref, 1 - slot)

    pltpu.make_async_copy(yd_hbm.at[pl.ds(0, rows)], ybuf.at[slot], sem.at[slot]).wait()
    y1 = ybuf[slot, 0:tm, :]
    y2 = ybuf[slot, tm:rows, :]
    route = route_ref[...].T
    y = DN_ALPHA * x_ref[...] + (y1 * route[:, 0:1] + y2 * route[:, 1:2])
    o_ref[...] = _ln_rows(y, g_ref[...], b_ref[...])


def _moe_combine_ln(xt, y_disp, dest2, route, ln_g, ln_b):
    t, d = xt.shape
    n = t // ROUTE_TILE
    rows = EXPERT_TOP_K * ROUTE_TILE
    return pl.pallas_call(
        _combine_kernel,
        name="moe_combine_ln",
        grid=(n,),
        in_specs=[pl.BlockSpec((1, 1, rows), lambda i: (i, 0, 0), memory_space=pltpu.SMEM),
                  pl.BlockSpec((1, 1, rows), lambda i: (jnp.minimum(i + 1, n - 1), 0, 0), memory_space=pltpu.SMEM),
                  pl.BlockSpec((ROUTE_TILE, d), lambda i: (i, 0)),
                  pl.BlockSpec((ROUTE_ROWS, ROUTE_TILE), lambda i: (0, i)),
                  pl.BlockSpec((1, d), lambda i: (0, 0)),
                  pl.BlockSpec((1, d), lambda i: (0, 0)),
                  pl.BlockSpec(memory_space=pl.ANY)],
        out_specs=pl.BlockSpec((ROUTE_TILE, d), lambda i: (i, 0)),
        out_shape=jax.ShapeDtypeStruct((t, d), jnp.float32),
        scratch_shapes=[pltpu.VMEM((2, rows, d), jnp.float32), pltpu.SemaphoreType.DMA((2,))],
        compiler_params=pltpu.CompilerParams(dimension_semantics=("arbitrary",), vmem_limit_bytes=VMEM_LIMIT),
    )(dest2, dest2, xt, route, ln_g.reshape(1, d), ln_b.reshape(1, d), y_disp)


def _moe_sublayer(x, wg, bg, we, be, w1, w3, w2, layer, ln_g, ln_b):
    b, s, d = x.shape
    t = b * s
    xt = x.reshape(t, d)
    route, cnt = _moe_route(xt, wg, bg, we, be)
    counts = cnt[N_GROUPS:N_GROUPS + N_EXPERTS, 0].astype(jnp.int32)
    n_tiles = (t * EXPERT_TOP_K) // EXPERT_TILE + N_EXPERTS
    tiles_per = (counts + EXPERT_TILE - 1) // EXPERT_TILE
    tile_end = jnp.cumsum(tiles_per)
    pad_start = (tile_end - tiles_per) * EXPERT_TILE
    n_used = tile_end[-1:].astype(jnp.int32)
    tile_ids = jnp.minimum(jnp.arange(n_tiles), n_used[0] - 1)
    tile_expert = jnp.sum(tile_ids[:, None] >= tile_end[None, :], axis=1).astype(jnp.int32)
    experts = route[2:4].astype(jnp.int32)
    first_row = jnp.sum(jnp.where(experts[..., None] == jnp.arange(N_EXPERTS), pad_start, 0), axis=-1)
    dest = first_row + route[4:6].astype(jnp.int32)
    dest2 = jnp.swapaxes(dest.reshape(EXPERT_TOP_K, t // ROUTE_TILE, ROUTE_TILE), 0, 1)
    dest2 = dest2.reshape(t // ROUTE_TILE, 1, EXPERT_TOP_K * ROUTE_TILE)
    x_disp = _moe_dispatch(xt, dest2, n_tiles * EXPERT_TILE)
    y_disp = _moe_experts(x_disp, tile_expert, n_used, w1, w3, w2, layer)
    return _moe_combine_ln(xt, y_disp, dest2, route, ln_g, ln_b).reshape(b, s, d)


def kernel(x, mem, mem_wk, mem_wv, ev_w_in, ev_conv_w, ev_conv_b, ev_cnorm_g, ev_cnorm_b, ev_cmp_pe_k, ev_cmp_w1_k, ev_cmp_w2_k, ev_cmp_pe_v, ev_cmp_w1_v, ev_cmp_w2_v, ev_w_out, od_w_in, od_conv_w, od_w_out, ln_mix_g, ln_mix_b, xa_wq, xa_wo, ln_xa_g, ln_xa_b, moe_wg, moe_bg, moe_we, moe_be, moe_w1, moe_w3, moe_w2, ln_ffn_g, ln_ffn_b):
    b, s, d = x.shape
    mem_k, mem_v = _mem_kv(mem, mem_wk, mem_wv)
    for layer in range(DEPTH):
        i = layer // 2
        if layer % 2 == 0:
            a, kv_in, k_sel, k_win, q_t, v_sel_t, v_win_t, gate_t = _even_in_proj(x, ev_w_in[i])
            a = _conformer_conv(a, ev_conv_w[i], ev_conv_b[i], ev_cnorm_g[i], ev_cnorm_b[i])
            k_cmp, v_cmp_t = _compress_kv(kv_in, ev_cmp_pe_k[i], ev_cmp_w1_k[i], ev_cmp_w2_k[i], ev_cmp_pe_v[i], ev_cmp_w1_v[i], ev_cmp_w2_v[i])
            o = _nsa_attention(q_t, gate_t, k_cmp, v_cmp_t, k_sel, v_sel_t, k_win, v_win_t)
            x = _proj_residual_ln(a.reshape(b * s, -1), o.reshape(b * s, -1), x.reshape(b * s, d), ev_w_out[i], ln_mix_g[layer], ln_mix_b[layer]).reshape(b, s, d)
        else:
            x = _odd_mixer_sublayer(x, od_w_in[i], od_conv_w[i], od_w_out[i], ln_mix_g[layer], ln_mix_b[layer])
        x = _xattn_sublayer(x, mem_k, mem_v, xa_wq[layer], xa_wo[layer], ln_xa_g[layer], ln_xa_b[layer])
        x = _moe_sublayer(x, moe_wg[layer], moe_bg[layer], moe_we[layer], moe_be[layer], moe_w1, moe_w3, moe_w2, layer, ln_ffn_g[layer], ln_ffn_b[layer])
    return x
```

```python
import functools

import numpy as np
import jax
import jax.numpy as jnp
from jax import lax
from jax.experimental import pallas as pl
from jax.experimental.pallas import tpu as pltpu

D_MODEL = 1024
DEPTH = 2
CONV_CH = D_MODEL // 2
CONV_WIDTH = 31
NSA_HEADS = 8
NSA_KV_HEADS = 2
HEAD_DIM = (D_MODEL // 2) // NSA_HEADS
CMP_BLOCK = 32
CMP_STRIDE = 16
SEL_BLOCK = 64
SEL_TOP_N = 16
WINDOW = 512
Q_BLOCK = 256
FORCE_SCORE = 1e4
SHORT_CONV_WIDTH = 3
XA_HEADS = 4
XA_HEAD_DIM = D_MODEL // XA_HEADS
N_GROUPS = 4
EXPERTS_PER_GROUP = 8
N_EXPERTS = N_GROUPS * EXPERTS_PER_GROUP
EXPERT_TOP_K = 2
DN_ALPHA = (2 * DEPTH) ** 0.25
LN_EPS = 1e-5
NEG_INF = -1e30
KV_COLS = NSA_KV_HEADS * HEAD_DIM
QCOLS = NSA_HEADS * HEAD_DIM
GATE_ROWS = 3 * NSA_HEADS

LANES = 128
SUBLANES = 8
HEADS_PER_KV = NSA_HEADS // NSA_KV_HEADS
QL = Q_BLOCK * HEADS_PER_KV
SEL_TILE = 1024
WIN_SPAN = WINDOW + Q_BLOCK
BLOCKS_PER_TILE = SEL_TILE // SEL_BLOCK
SEQ_TILE = 1024
HALO = 32
ODD_HALO = 8
ROUTE_TILE = 512
EXPERT_TILE = 512
ROUTE_ROWS = 8
VMEM_LIMIT = 56 * 1024 * 1024

_NT = (((1,), (1,)), ((), ()))


def _ln_rows(y, g, b):
    mu = jnp.mean(y, axis=-1, keepdims=True)
    yc = y - mu
    var = jnp.mean(yc * yc, axis=-1, keepdims=True)
    return yc * lax.rsqrt(var + LN_EPS) * g + b


def _even_in_kernel(x_ref, wa_ref, wkv_ref, wk2_ref, wqT_ref, wvT_ref, wgT_ref,
                    a_ref, kv_ref, ks_ref, kw_ref, qT_ref, vsT_ref, vwT_ref, gT_ref):
    f32 = jnp.float32
    bf16 = jnp.bfloat16
    xb = x_ref[0].astype(bf16)
    av = jnp.dot(xb, wa_ref[...], preferred_element_type=f32)
    a_ref[0] = av[:, :CONV_CH] * jax.nn.sigmoid(av[:, CONV_CH:])
    kv_ref[0] = jnp.dot(xb, wkv_ref[...], preferred_element_type=f32)
    k2 = jnp.dot(xb, wk2_ref[...], preferred_element_type=f32)
    ks_ref[0] = k2[:, :KV_COLS].astype(bf16)
    kw_ref[0] = k2[:, KV_COLS:].astype(bf16)
    qT_ref[0] = lax.dot_general(wqT_ref[...], xb, _NT, preferred_element_type=f32).astype(bf16)
    vT = lax.dot_general(wvT_ref[...], xb, _NT, preferred_element_type=f32).astype(bf16)
    vsT_ref[0, 0] = vT[:KV_COLS]
    for j in range(SEQ_TILE // Q_BLOCK):
        vwT_ref[0, j] = vT[KV_COLS:, j * Q_BLOCK:(j + 1) * Q_BLOCK]
    gT_ref[0] = lax.dot_general(wgT_ref[...], xb, _NT, preferred_element_type=f32)


def _even_in_proj(x, w_in):
    b, s, d = x.shape
    bf16 = jnp.bfloat16
    c = np.cumsum((0, CONV_CH, CONV_CH, QCOLS, KV_COLS, KV_COLS, KV_COLS, KV_COLS, KV_COLS, KV_COLS, GATE_ROWS))
    col = lambda i, j: w_in[:, c[i]:c[j]]
    wa = col(0, 2).astype(bf16)
    wq_t = col(2, 3).T.astype(bf16)
    wkv = col(3, 5).astype(bf16)
    wk2 = jnp.concatenate([col(5, 6), col(7, 8)], axis=1).astype(bf16)
    wv_t = jnp.concatenate([col(6, 7), col(8, 9)], axis=1).T.astype(bf16)
    wg_t = col(9, 10).T.astype(bf16)
    ts = SEQ_TILE
    per_sel = SEL_TILE // ts
    full = lambda shape: pl.BlockSpec(shape, lambda i, j: (0,) * len(shape))
    return pl.pallas_call(
        _even_in_kernel,
        name="even_in_proj",
        grid=(b, s // ts),
        in_specs=[pl.BlockSpec((1, ts, d), lambda i, j: (i, j, 0)),
                  full(wa.shape), full(wkv.shape), full(wk2.shape), full(wq_t.shape), full(wv_t.shape), full(wg_t.shape)],
        out_specs=[pl.BlockSpec((1, ts, CONV_CH), lambda i, j: (i, j, 0)),
                   pl.BlockSpec((1, ts, 2 * KV_COLS), lambda i, j: (i, j, 0)),
                   pl.BlockSpec((1, ts, KV_COLS), lambda i, j: (i, j, 0)),
                   pl.BlockSpec((1, ts, KV_COLS), lambda i, j: (i, j, 0)),
                   pl.BlockSpec((1, QCOLS, ts), lambda i, j: (i, 0, j)),
                   pl.BlockSpec((1, 1, KV_COLS, ts), lambda i, j: (i, j // per_sel, 0, j % per_sel)),
                   pl.BlockSpec((1, ts // Q_BLOCK, KV_COLS, Q_BLOCK), lambda i, j: (i, j, 0, 0)),
                   pl.BlockSpec((1, GATE_ROWS, ts), lambda i, j: (i, 0, j))],
        out_shape=[jax.ShapeDtypeStruct((b, s, CONV_CH), jnp.float32),
                   jax.ShapeDtypeStruct((b, s, 2 * KV_COLS), jnp.float32),
                   jax.ShapeDtypeStruct((b, s, KV_COLS), bf16),
                   jax.ShapeDtypeStruct((b, s, KV_COLS), bf16),
                   jax.ShapeDtypeStruct((b, QCOLS, s), bf16),
                   jax.ShapeDtypeStruct((b, s // SEL_TILE, KV_COLS, SEL_TILE), bf16),
                   jax.ShapeDtypeStruct((b, s // Q_BLOCK, KV_COLS, Q_BLOCK), bf16),
                   jax.ShapeDtypeStruct((b, GATE_ROWS, s), jnp.float32)],
        compiler_params=pltpu.CompilerParams(dimension_semantics=("arbitrary", "arbitrary"), vmem_limit_bytes=VMEM_LIMIT),
    )(x, wa, wkv, wk2, wq_t, wv_t, wg_t)


def _conv_kernel(cur_ref, halo_ref, w_ref, cb_ref, g_ref, b_ref, o_ref, ext_ref, win_ref):
    j = pl.program_id(1)
    ts = cur_ref.shape[1]
    halo = halo_ref[0]
    ext_ref[0:HALO, :] = jnp.where(j > 0, halo, jnp.zeros_like(halo))
    ext_ref[HALO:HALO + ts, :] = cur_ref[0]
    first = HALO - (CONV_WIDTH - 1)
    acc = jnp.zeros((ts, CONV_CH), jnp.float32)
    for p in range(SUBLANES):
        n_a = len(range(p, CONV_WIDTH, SUBLANES))
        rows = ts + SUBLANES * (n_a - 1)
        win_ref[0:rows, :] = ext_ref[first + p:first + p + rows, :]
        for a in range(n_a):
            k = SUBLANES * a + p
            acc = acc + w_ref[k:k + 1, :] * win_ref[SUBLANES * a:SUBLANES * a + ts, :]
    y = _ln_rows(acc + cb_ref[...], g_ref[...], b_ref[...])
    o_ref[0] = (y * jax.nn.sigmoid(y)).astype(o_ref.dtype)


def _conformer_conv(a, conv_w, conv_b, cn_g, cn_b):
    b, s, c = a.shape
    ts = SEQ_TILE
    per = ts // HALO
    row = lambda v: v.reshape(1, c)
    return pl.pallas_call(
        _conv_kernel,
        name="conformer_conv",
        grid=(b, s // ts),
        in_specs=[pl.BlockSpec((1, ts, c), lambda i, j: (i, j, 0)),
                  pl.BlockSpec((1, HALO, c), lambda i, j: (i, jnp.maximum(j * per - 1, 0), 0)),
                  pl.BlockSpec((CONV_WIDTH, c), lambda i, j: (0, 0)),
                  pl.BlockSpec((1, c), lambda i, j: (0, 0)),
                  pl.BlockSpec((1, c), lambda i, j: (0, 0)),
                  pl.BlockSpec((1, c), lambda i, j: (0, 0))],
        out_specs=pl.BlockSpec((1, ts, c), lambda i, j: (i, j, 0)),
        out_shape=jax.ShapeDtypeStruct((b, s, c), jnp.bfloat16),
        scratch_shapes=[pltpu.VMEM((HALO + ts, c), jnp.float32), pltpu.VMEM((HALO + ts, c), jnp.float32)],
        compiler_params=pltpu.CompilerParams(dimension_semantics=("arbitrary", "arbitrary")),
    )(a, a, conv_w, row(conv_b), row(cn_g), row(cn_b))


def _compress_kernel(r_ref, pe_ref, w1_ref, w2_ref, o_ref):
    f32 = jnp.float32
    bf16 = jnp.bfloat16
    r = r_ref[0, 0]
    half = r.shape[1]
    ha = jnp.dot((r + pe_ref[0, 0:1, :]).astype(bf16), w1_ref[0, :half, :], preferred_element_type=f32)
    hb = jnp.dot((r + pe_ref[0, 1:2, :]).astype(bf16), w1_ref[0, half:, :], preferred_element_type=f32)
    hb_next = jnp.concatenate([hb[1:], jnp.zeros_like(hb[0:1])], axis=0)
    hid = jax.nn.gelu(ha + hb_next).astype(bf16)
    o_ref[0, 0] = jnp.dot(hid, w2_ref[0], preferred_element_type=f32).astype(bf16)


def _compress_kv(kv_in, pe_k, w1_k, w2_k, pe_v, w1_v, w2_v):
    b, s, _ = kv_in.shape
    bf16 = jnp.bfloat16
    nch = s // CMP_STRIDE
    g = NSA_KV_HEADS
    r = kv_in.reshape(b, nch, CMP_STRIDE, 2 * g, HEAD_DIM)
    r = jnp.transpose(r, (0, 3, 1, 2, 4)).reshape(b, 2 * g, nch, CMP_STRIDE * HEAD_DIM)
    pe = jnp.stack([pe_k, pe_v]).reshape(2, 2, CMP_STRIDE * HEAD_DIM)
    w1 = jnp.stack([w1_k, w1_v]).astype(bf16)
    w2 = jnp.stack([w2_k, w2_v]).astype(bf16)
    out = pl.pallas_call(
        _compress_kernel,
        name="compress_kv",
        grid=(b, 2 * g),
        in_specs=[pl.BlockSpec((1, 1, nch, r.shape[3]), lambda i, j: (i, j, 0, 0)),
                  pl.BlockSpec((1, 2, pe.shape[2]), lambda i, j: (j // g, 0, 0)),
                  pl.BlockSpec((1,) + w1.shape[1:], lambda i, j: (j // g, 0, 0)),
                  pl.BlockSpec((1,) + w2.shape[1:], lambda i, j: (j // g, 0, 0))],
        out_specs=pl.BlockSpec((1, 1, nch, HEAD_DIM), lambda i, j: (i, j, 0, 0)),
        out_shape=jax.ShapeDtypeStruct((b, 2 * g, nch, HEAD_DIM), bf16),
        compiler_params=pltpu.CompilerParams(dimension_semantics=("arbitrary", "arbitrary")),
    )(r, pe, w1, w2)
    kc = jnp.transpose(out[:, :g], (0, 2, 1, 3)).reshape(b, nch, KV_COLS)
    vcT = jnp.transpose(out[:, g:], (0, 1, 3, 2)).reshape(b, KV_COLS, nch)
    return kc, vcT


def _proj_ln_kernel(a_ref, o_ref, x_ref, w_ref, g_ref, b_ref, y_ref):
    lhs = jnp.concatenate([a_ref[...], o_ref[...]], axis=1)
    mix = jnp.dot(lhs, w_ref[...], preferred_element_type=jnp.float32)
    y_ref[...] = _ln_rows(DN_ALPHA * x_ref[...] + mix, g_ref[...], b_ref[...])


def _proj_residual_ln(a, o, x, w, ln_g, ln_b):
    t, d = x.shape
    tm = SEQ_TILE
    return pl.pallas_call(
        _proj_ln_kernel,
        name="proj_residual_ln",
        grid=(t // tm,),
        in_specs=[pl.BlockSpec((tm, a.shape[1]), lambda i: (i, 0)),
                  pl.BlockSpec((tm, o.shape[1]), lambda i: (i, 0)),
                  pl.BlockSpec((tm, d), lambda i: (i, 0)),
                  pl.BlockSpec(w.shape, lambda i: (0, 0)),
                  pl.BlockSpec((1, d), lambda i: (0, 0)),
                  pl.BlockSpec((1, d), lambda i: (0, 0))],
        out_specs=pl.BlockSpec((tm, d), lambda i: (i, 0)),
        out_shape=jax.ShapeDtypeStruct((t, d), jnp.float32),
        compiler_params=pltpu.CompilerParams(dimension_semantics=("arbitrary",), vmem_limit_bytes=VMEM_LIMIT),
    )(a, o, x, w.astype(jnp.bfloat16), ln_g.reshape(1, d), ln_b.reshape(1, d))


def _mem_kv_kernel(m_ref, wk_ref, wv_ref, k_ref, v_ref):
    mb = m_ref[...].astype(jnp.bfloat16)
    k_ref[...] = jnp.dot(mb, wk_ref[...], preferred_element_type=jnp.float32).astype(jnp.bfloat16)
    v_ref[...] = jnp.dot(mb, wv_ref[...], preferred_element_type=jnp.float32).astype(jnp.bfloat16)


def _mem_kv(mem, wk, wv):
    b, m, d = mem.shape
    bf16 = jnp.bfloat16
    k, v = pl.pallas_call(
        _mem_kv_kernel,
        name="mem_kv",
        grid=(b,),
        in_specs=[pl.BlockSpec((m, d), lambda i: (i, 0)), pl.BlockSpec((d, d), lambda i: (0, 0)), pl.BlockSpec((d, d), lambda i: (0, 0))],
        out_specs=[pl.BlockSpec((m, d), lambda i: (i, 0)), pl.BlockSpec((m, d), lambda i: (i, 0))],
        out_shape=[jax.ShapeDtypeStruct((b * m, d), bf16), jax.ShapeDtypeStruct((b * m, d), bf16)],
        compiler_params=pltpu.CompilerParams(dimension_semantics=("arbitrary",), vmem_limit_bytes=VMEM_LIMIT),
    )(mem.reshape(b * m, d), wk.astype(bf16), wv.astype(bf16))
    return k.reshape(b, m, d), v.reshape(b, m, d)


def _xattn_kernel(x_ref, k_ref, v_ref, wq_ref, wo_ref, g_ref, b_ref, y_ref):
    f32 = jnp.float32
    bf16 = jnp.bfloat16
    x = x_ref[0]
    q = (jnp.dot(x.astype(bf16), wq_ref[...], preferred_element_type=f32).astype(bf16)
         * jnp.asarray(XA_HEAD_DIM ** -0.5, bf16))
    heads = []
    for h in range(XA_HEADS):
        cols = slice(h * XA_HEAD_DIM, (h + 1) * XA_HEAD_DIM)
        s = lax.dot_general(q[:, cols], k_ref[0, :, cols], _NT, preferred_element_type=f32)
        m = jnp.max(s, axis=1, keepdims=True)
        p = jnp.exp(s - m)
        p = p / jnp.sum(p, axis=1, keepdims=True)
        heads.append(jnp.dot(p.astype(bf16), v_ref[0, :, cols], preferred_element_type=f32).astype(bf16))
    att = jnp.concatenate(heads, axis=1)
    out = jnp.dot(att, wo_ref[...], preferred_element_type=f32)
    y_ref[0] = _ln_rows(DN_ALPHA * x + out, g_ref[...], b_ref[...])


def _xattn_sublayer(x, mem_k, mem_v, wq, wo, ln_g, ln_b):
    b, s, d = x.shape
    m = mem_k.shape[1]
    ts = SEQ_TILE
    bf16 = jnp.bfloat16
    return pl.pallas_call(
        _xattn_kernel,
        name="xattn_sublayer",
        grid=(b, s // ts),
        in_specs=[pl.BlockSpec((1, ts, d), lambda i, j: (i, j, 0)),
                  pl.BlockSpec((1, m, d), lambda i, j: (i, 0, 0)),
                  pl.BlockSpec((1, m, d), lambda i, j: (i, 0, 0)),
                  pl.BlockSpec((d, d), lambda i, j: (0, 0)),
                  pl.BlockSpec((d, d), lambda i, j: (0, 0)),
                  pl.BlockSpec((1, d), lambda i, j: (0, 0)),
                  pl.BlockSpec((1, d), lambda i, j: (0, 0))],
        out_specs=pl.BlockSpec((1, ts, d), lambda i, j: (i, j, 0)),
        out_shape=jax.ShapeDtypeStruct((b, s, d), jnp.float32),
        compiler_params=pltpu.CompilerParams(dimension_semantics=("arbitrary", "arbitrary"), vmem_limit_bytes=VMEM_LIMIT),
    )(x, mem_k, mem_v, wq.astype(bf16), wo.astype(bf16), ln_g.reshape(1, d), ln_b.reshape(1, d))


def _odd_kernel(x_ref, halo_ref, wb_ref, wc_ref, wh_ref, cw_ref, wo_ref, g_ref, b_ref, y_ref):
    f32 = jnp.float32
    bf16 = jnp.bfloat16
    j = pl.program_id(1)
    ts = x_ref.shape[1]
    x = x_ref[0]
    xe = jnp.concatenate([halo_ref[0], x], axis=0).astype(bf16)
    u = (jnp.dot(xe, wc_ref[...], preferred_element_type=f32) * jnp.dot(xe, wh_ref[...], preferred_element_type=f32))
    row = lax.broadcasted_iota(jnp.int32, (ODD_HALO + ts, 1), 0)
    u = jnp.where((row >= ODD_HALO) | (j > 0), u, 0.0)
    conv = jnp.zeros((ts, u.shape[1]), f32)
    for k in range(SHORT_CONV_WIDTH):
        off = ODD_HALO - (SHORT_CONV_WIDTH - 1) + k
        conv = conv + cw_ref[k:k + 1, :] * u[off:off + ts, :]
    gate_b = jnp.dot(xe[ODD_HALO:], wb_ref[...], preferred_element_type=f32)
    mix = jnp.dot((gate_b * conv).astype(bf16), wo_ref[...], preferred_element_type=f32)
    y_ref[0] = _ln_rows(DN_ALPHA * x + mix, g_ref[...], b_ref[...])


def _odd_mixer_sublayer(x, w_in, conv_w, w_out, ln_g, ln_b):
    b, s, d = x.shape
    ts = SEQ_TILE
    per = ts // ODD_HALO
    bf16 = jnp.bfloat16
    wb, wc, wh = (w_in[:, i * d:(i + 1) * d].astype(bf16) for i in range(3))
    full = lambda shape: pl.BlockSpec(shape, lambda i, j: (0,) * len(shape))
    return pl.pallas_call(
        _odd_kernel,
        name="odd_mixer_sublayer",
        grid=(b, s // ts),
        in_specs=[pl.BlockSpec((1, ts, d), lambda i, j: (i, j, 0)),
                  pl.BlockSpec((1, ODD_HALO, d), lambda i, j: (i, jnp.maximum(j * per - 1, 0), 0)),
                  full((d, d)), full((d, d)), full((d, d)), full(conv_w.shape), full((d, d)), full((1, d)), full((1, d))],
        out_specs=pl.BlockSpec((1, ts, d), lambda i, j: (i, j, 0)),
        out_shape=jax.ShapeDtypeStruct((b, s, d), jnp.float32),
        compiler_params=pltpu.CompilerParams(dimension_semantics=("arbitrary", "arbitrary"), vmem_limit_bytes=VMEM_LIMIT),
    )(x, x, wb, wc, wh, conv_w, w_out.astype(bf16), ln_g.reshape(1, d), ln_b.reshape(1, d))


def _cmp_to_sel_matrix(n_cmp, n_sel):
    c0 = np.arange(n_cmp) * CMP_STRIDE
    s0 = np.arange(n_sel) * SEL_BLOCK
    ov = np.minimum(c0[:, None] + CMP_BLOCK, s0[None, :] + SEL_BLOCK) - np.maximum(c0[:, None], s0[None, :])
    return (np.clip(ov, 0, None) / CMP_BLOCK).astype(np.float32)


def _nsa_kernel(qT_ref, gT_ref, kc_ref, vcT_ref, mselT_ref, ks_ref, vsT_ref, kw_ref, vwT_ref, o_ref,
                score_ref, sel_ref, *, n_sel):
    c = pl.program_id(1)
    q0 = c * Q_BLOCK
    f32 = jnp.float32
    bf16 = jnp.bfloat16
    n_cmp_pad = kc_ref.shape[1]

    lane_q = lax.broadcasted_iota(jnp.int32, (1, QL), 1) % Q_BLOCK
    t_row = q0 + lane_q
    t_row_q = q0 + lax.broadcasted_iota(jnp.int32, (1, Q_BLOCK), 1)
    cur_q = t_row_q // SEL_BLOCK

    top_n = min(SEL_TOP_N, n_sel)
    m_iota = lax.broadcasted_iota(jnp.int32, (n_sel, Q_BLOCK), 0)
    forced = (m_iota == 0) | (m_iota == cur_q) | (m_iota == cur_q - 1)
    valid = m_iota <= cur_q
    n_comp = jnp.minimum(q0 // SEL_BLOCK + Q_BLOCK // SEL_BLOCK, n_sel)

    q_g, o_c = [], []
    for g in range(NSA_KV_HEADS):
        pieces = []
        for hg in range(HEADS_PER_KV):
            h = g * HEADS_PER_KV + hg
            qh = qT_ref[0, h * HEAD_DIM:(h + 1) * HEAD_DIM, :] * jnp.asarray(HEAD_DIM ** -0.5, bf16)
            z = jnp.zeros_like(qh)
            pieces.append(jnp.concatenate([qh, z] if g == 0 else [z, qh], axis=0))
        qTp = jnp.concatenate(pieces, axis=1)
        q_g.append(qTp)
        rows = slice(g * HEAD_DIM, (g + 1) * HEAD_DIM)

        s_c = jnp.dot(kc_ref[0], qTp, preferred_element_type=f32)
        n_iota = lax.broadcasted_iota(jnp.int32, (n_cmp_pad, QL), 0)
        mask_c = (n_iota * CMP_STRIDE + (CMP_BLOCK - 1)) <= t_row
        s_c = jnp.where(mask_c, s_c, NEG_INF)
        m_c = jnp.max(s_c, axis=0, keepdims=True)
        p_c = jnp.where(mask_c, jnp.exp(s_c - m_c), 0.0)
        l_c = jnp.sum(p_c, axis=0, keepdims=True)
        p_c = p_c * jnp.where(l_c > 0.0, 1.0 / l_c, 0.0)
        p_cb = p_c.astype(bf16)
        o_c.append(jnp.dot(vcT_ref[0], p_cb, preferred_element_type=f32)[rows])
        imp4 = jnp.dot(mselT_ref[...], p_cb, preferred_element_type=f32)
        imp = imp4[:, 0:Q_BLOCK]
        for hg in range(1, HEADS_PER_KV):
            imp = imp + imp4[:, hg * Q_BLOCK:(hg + 1) * Q_BLOCK]

        score = jnp.where(valid, jnp.where(forced, FORCE_SCORE, imp), NEG_INF)
        score_ref[...] = score

        def rank_body(i8, rank, score=score):
            base = pl.multiple_of(i8 * SUBLANES, SUBLANES)
            rows = score_ref[pl.ds(base, SUBLANES), :]
            for u in range(SUBLANES):
                row = rows[u:u + 1, :]
                beats = (row > score) | ((row == score) & (base + u < m_iota))
                rank = rank + beats.astype(jnp.int32)
            return rank

        rank = lax.fori_loop(0, (n_comp + SUBLANES - 1) // SUBLANES, rank_body,
                             jnp.zeros((n_sel, Q_BLOCK), jnp.int32))
        sel_ref[g] = jnp.where((rank < top_n) & valid, 0.0, NEG_INF)

    def sel_tile(j, carry, causal):
        k_tile = ks_ref[0, j]
        v_tile = vsT_ref[0, j]
        blk0 = pl.multiple_of(j * BLOCKS_PER_TILE, BLOCKS_PER_TILE)
        out = []
        for g in range(NSA_KV_HEADS):
            m_i, l_i, acc = carry[g]
            s = jnp.dot(k_tile, q_g[g], preferred_element_type=f32)
            selrows = sel_ref[g, pl.ds(blk0, BLOCKS_PER_TILE), :]
            bias = jnp.concatenate(
                [jnp.broadcast_to(selrows[r:r + 1, :], (SEL_BLOCK, Q_BLOCK)) for r in range(BLOCKS_PER_TILE)], axis=0)
            if causal:
                key = j * SEL_TILE + lax.broadcasted_iota(jnp.int32, (SEL_TILE, Q_BLOCK), 0)
                bias = jnp.where(key <= t_row_q, bias, NEG_INF)
            s = s + jnp.concatenate([bias] * HEADS_PER_KV, axis=1)
            m_new = jnp.maximum(m_i, jnp.max(s, axis=0, keepdims=True))
            alpha = jnp.exp(m_i - m_new)
            p = jnp.exp(s - m_new)
            l_new = alpha * l_i + jnp.sum(p, axis=0, keepdims=True)
            pv = jnp.dot(v_tile, p.astype(bf16), preferred_element_type=f32)
            out.append((m_new, l_new, alpha * acc + pv))
        return tuple(out)

    n_full = q0 // SEL_TILE
    init = tuple((jnp.full((1, QL), NEG_INF, f32), jnp.zeros((1, QL), f32), jnp.zeros((KV_COLS, QL), f32))
                 for _ in range(NSA_KV_HEADS))
    carry = lax.fori_loop(0, n_full, functools.partial(sel_tile, causal=False), init)
    carry = sel_tile(n_full, carry, causal=True)

    start = pl.multiple_of(jnp.maximum(q0 - WINDOW, 0), Q_BLOCK)
    j0 = start // Q_BLOCK
    k_win = kw_ref[0, pl.ds(start, WIN_SPAN), :]
    key_w = start + lax.broadcasted_iota(jnp.int32, (WIN_SPAN, Q_BLOCK), 0)
    bias_w = jnp.where(key_w <= t_row_q, 0.0, NEG_INF)
    bias_w = jnp.where(key_w > t_row_q - WINDOW, bias_w, NEG_INF)
    bias_w = jnp.concatenate([bias_w] * HEADS_PER_KV, axis=1)
    for g in range(NSA_KV_HEADS):
        rows = slice(g * HEAD_DIM, (g + 1) * HEAD_DIM)
        m_s, l_s, acc_s = carry[g]
        o_s = acc_s[rows] * (1.0 / l_s)
        s_w = jnp.dot(k_win, q_g[g], preferred_element_type=f32) + bias_w
        m_w = jnp.max(s_w, axis=0, keepdims=True)
        p_w = jnp.exp(s_w - m_w)
        l_w = jnp.sum(p_w, axis=0, keepdims=True)
        p_wb = p_w.astype(bf16)
        acc_w = jnp.zeros((KV_COLS, QL), f32)
        for i in range(WIN_SPAN // Q_BLOCK):
            acc_w = acc_w + jnp.dot(vwT_ref[0, j0 + i], p_wb[i * Q_BLOCK:(i + 1) * Q_BLOCK, :], preferred_element_type=f32)
        o_w = acc_w[rows] * (1.0 / l_w)

        for pair in range(HEADS_PER_KV // 2):
            halves = []
            for hg in (2 * pair, 2 * pair + 1):
                h = g * HEADS_PER_KV + hg
                lanes = slice(hg * Q_BLOCK, (hg + 1) * Q_BLOCK)
                gate = jax.nn.sigmoid(gT_ref[0, 3 * h:3 * h + 3, :])
                halves.append(gate[0:1] * o_c[g][:, lanes] + gate[1:2] * o_s[:, lanes] + gate[2:3] * o_w[:, lanes])
            both = jnp.concatenate(halves, axis=0)
            col0 = (g * HEADS_PER_KV + 2 * pair) * HEAD_DIM
            o_ref[0, :, col0:col0 + 2 * HEAD_DIM] = both.T.astype(o_ref.dtype)


def _nsa_attention(qT, gT, kc, vcT, ks, vsT, kw, vwT):
    b, hd, s = qT.shape
    n_sel = s // SEL_BLOCK
    nc = kc.shape[1]
    n_cmp = (s - CMP_BLOCK) // CMP_STRIDE + 1
    mselT = jnp.asarray(np.pad(_cmp_to_sel_matrix(n_cmp, n_sel).T, ((0, 0), (0, nc - n_cmp))), jnp.bfloat16)
    ks4 = ks.reshape(b, s // SEL_TILE, SEL_TILE, KV_COLS)
    return pl.pallas_call(
        functools.partial(_nsa_kernel, n_sel=n_sel),
        name="nsa_attention",
        grid=(b, s // Q_BLOCK),
        in_specs=[
            pl.BlockSpec((1, hd, Q_BLOCK), lambda i, c: (i, 0, c)),
            pl.BlockSpec((1, 3 * NSA_HEADS, Q_BLOCK), lambda i, c: (i, 0, c)),
            pl.BlockSpec((1, nc, KV_COLS), lambda i, c: (i, 0, 0)),
            pl.BlockSpec((1, KV_COLS, nc), lambda i, c: (i, 0, 0)),
            pl.BlockSpec((n_sel, nc), lambda i, c: (0, 0)),
            pl.BlockSpec((1, s // SEL_TILE, SEL_TILE, KV_COLS), lambda i, c: (i, 0, 0, 0)),
            pl.BlockSpec((1, s // SEL_TILE, KV_COLS, SEL_TILE), lambda i, c: (i, 0, 0, 0)),
            pl.BlockSpec((1, s, KV_COLS), lambda i, c: (i, 0, 0)),
            pl.BlockSpec((1, s // Q_BLOCK, KV_COLS, Q_BLOCK), lambda i, c: (i, 0, 0, 0)),
        ],
        out_specs=pl.BlockSpec((1, Q_BLOCK, hd), lambda i, c: (i, c, 0)),
        out_shape=jax.ShapeDtypeStruct((b, s, hd), jnp.bfloat16),
        scratch_shapes=[pltpu.VMEM((n_sel, Q_BLOCK), jnp.float32),
                        pltpu.VMEM((NSA_KV_HEADS, n_sel, Q_BLOCK), jnp.float32)],
        compiler_params=pltpu.CompilerParams(dimension_semantics=("arbitrary", "arbitrary")),
    )(qT, gT, kc, vcT, mselT, ks4, vsT, kw, vwT)


def _router_kernel(x_ref, wT_ref, b_ref, tri_ref, route_ref, cnt_ref, run_ref):
    i = pl.program_id(0)
    f32 = jnp.float32

    @pl.when(i == 0)
    def _():
        run_ref[...] = jnp.zeros_like(run_ref)

    tm = x_ref.shape[0]
    logits = lax.dot_general(wT_ref[...], x_ref[...].astype(jnp.bfloat16), _NT, preferred_element_type=f32) + b_ref[...]
    sub = lax.broadcasted_iota(jnp.int32, (LANES, tm), 0)
    is_g = sub < N_GROUPS
    gl = jnp.where(is_g, logits, NEG_INF)
    g_max = jnp.max(gl, axis=0, keepdims=True)
    g_star = jnp.min(jnp.where(gl == g_max, sub, LANES), axis=0, keepdims=True)
    p_group = 1.0 / jnp.sum(jnp.where(is_g, jnp.exp(gl - g_max), 0.0), axis=0, keepdims=True)
    lo = N_GROUPS + g_star * EXPERTS_PER_GROUP
    in_grp = (sub >= lo) & (sub < lo + EXPERTS_PER_GROUP)
    el = jnp.where(in_grp, logits, NEG_INF)
    v1 = jnp.max(el, axis=0, keepdims=True)
    i1 = jnp.min(jnp.where(el == v1, sub, LANES), axis=0, keepdims=True)
    el2 = jnp.where(sub == i1, NEG_INF, el)
    v2 = jnp.max(el2, axis=0, keepdims=True)
    i2 = jnp.min(jnp.where(el2 == v2, sub, LANES), axis=0, keepdims=True)
    e21 = jnp.exp(v2 - v1)
    gate1 = p_group * (1.0 / (1.0 + e21))
    gate2 = p_group * (e21 / (1.0 + e21))
    oh1 = (sub == i1).astype(f32)
    oh2 = (sub == i2).astype(f32)
    both = oh1 + oh2
    before = jnp.dot(both.astype(jnp.bfloat16), tri_ref[...], preferred_element_type=f32) + run_ref[...]
    rank1 = jnp.sum(oh1 * before, axis=0, keepdims=True)
    rank2 = jnp.sum(oh2 * before, axis=0, keepdims=True)
    run_ref[...] = run_ref[...] + jnp.sum(both, axis=1, keepdims=True)
    cnt_ref[...] = run_ref[...]
    zero = jnp.zeros_like(gate1)
    route_ref[...] = jnp.concatenate(
        [gate1, gate2, (i1 - N_GROUPS).astype(f32), (i2 - N_GROUPS).astype(f32), rank1, rank2, zero, zero], axis=0)


def _moe_route(xt, wg, bg, we, be):
    t, d = xt.shape
    pad = LANES - N_GROUPS - N_EXPERTS
    w_t = jnp.concatenate([wg, we.reshape(d, N_EXPERTS), jnp.zeros((d, pad), wg.dtype)], axis=1).T
    bias = jnp.concatenate([bg, be.reshape(N_EXPERTS), jnp.zeros((pad,), bg.dtype)])[:, None]
    tri = jnp.asarray(np.triu(np.ones((ROUTE_TILE, ROUTE_TILE), np.float32), 1), jnp.bfloat16)
    return pl.pallas_call(
        _router_kernel,
        name="moe_router",
        grid=(t // ROUTE_TILE,),
        in_specs=[pl.BlockSpec((ROUTE_TILE, d), lambda i: (i, 0)),
                  pl.BlockSpec((LANES, d), lambda i: (0, 0)),
                  pl.BlockSpec((LANES, 1), lambda i: (0, 0)),
                  pl.BlockSpec((ROUTE_TILE, ROUTE_TILE), lambda i: (0, 0))],
        out_specs=[pl.BlockSpec((ROUTE_ROWS, ROUTE_TILE), lambda i: (0, i)),
                   pl.BlockSpec((LANES, 1), lambda i: (0, 0))],
        out_shape=[jax.ShapeDtypeStruct((ROUTE_ROWS, t), jnp.float32), jax.ShapeDtypeStruct((LANES, 1), jnp.float32)],
        scratch_shapes=[pltpu.VMEM((LANES, 1), jnp.float32)],
        compiler_params=pltpu.CompilerParams(dimension_semantics=("arbitrary",)),
    )(xt, w_t.astype(jnp.bfloat16), bias, tri)


def _dispatch_kernel(dest_ref, x_ref, zero_hbm, xd_hbm, sem):
    del zero_hbm
    tm = x_ref.shape[0]

    def body(r, c):
        for k in range(EXPERT_TOP_K):
            pltpu.make_async_copy(
                x_ref.at[pl.ds(r, 1)], xd_hbm.at[pl.ds(dest_ref[0, 0, k * tm + r], 1)], sem).start(priority=k)
        return c

    lax.fori_loop(0, tm, body, 0, unroll=8)
    for k in range(EXPERT_TOP_K):
        pltpu.make_async_copy(x_ref, xd_hbm.at[pl.ds(0, tm)], sem).wait()


def _moe_dispatch(xt, dest2, n_rows):
    t, d = xt.shape
    return pl.pallas_call(
        _dispatch_kernel,
        name="moe_dispatch",
        grid=(t // ROUTE_TILE,),
        in_specs=[pl.BlockSpec((1, 1, EXPERT_TOP_K * ROUTE_TILE), lambda i: (i, 0, 0), memory_space=pltpu.SMEM),
                  pl.BlockSpec((ROUTE_TILE, d), lambda i: (i, 0)),
                  pl.BlockSpec(memory_space=pl.ANY)],
        out_specs=pl.BlockSpec(memory_space=pl.ANY),
        out_shape=jax.ShapeDtypeStruct((n_rows, d), xt.dtype),
        scratch_shapes=[pltpu.SemaphoreType.DMA(())],
        input_output_aliases={2: 0},
        compiler_params=pltpu.CompilerParams(dimension_semantics=("arbitrary",), has_side_effects=True),
    )(dest2, xt, jnp.zeros((n_rows, d), xt.dtype))


def _expert_kernel(te_ref, nu_ref, xd_ref, w1_ref, w3_ref, w2_ref, y_ref, w1b, w3b, w2b):
    i = pl.program_id(0)
    used = i < nu_ref[0]
    new_expert = (i == 0) | (te_ref[i] != te_ref[jnp.maximum(i - 1, 0)])

    @pl.when(used & new_expert)
    def _():
        w1b[...] = w1_ref[0, 0].astype(jnp.bfloat16)
        w3b[...] = w3_ref[0, 0].astype(jnp.bfloat16)
        w2b[...] = w2_ref[0, 0].astype(jnp.bfloat16)

    @pl.when(used)
    def _():
        xb = xd_ref[...].astype(jnp.bfloat16)
        h1 = jnp.dot(xb, w1b[...], preferred_element_type=jnp.float32)
        h3 = jnp.dot(xb, w3b[...], preferred_element_type=jnp.float32)
        a = (h1 * jax.nn.sigmoid(h1) * h3).astype(jnp.bfloat16)
        y_ref[...] = jnp.dot(a, w2b[...], preferred_element_type=jnp.float32)

    @pl.when(jnp.logical_not(used))
    def _():
        y_ref[...] = jnp.zeros_like(y_ref)


def _moe_experts(x_disp, tile_expert, n_used, w1, w3, w2, layer):
    n_rows, d = x_disp.shape
    n_tiles = n_rows // EXPERT_TILE
    hid = w1.shape[3]

    def row_map(i, te, nu):
        return (i, 0)

    def w_map(i, te, nu):
        return (layer, te[i], 0, 0)

    grid_spec = pltpu.PrefetchScalarGridSpec(
        num_scalar_prefetch=2,
        grid=(n_tiles,),
        in_specs=[pl.BlockSpec((EXPERT_TILE, d), row_map),
                  pl.BlockSpec((1, 1, d, hid), w_map),
                  pl.BlockSpec((1, 1, d, hid), w_map),
                  pl.BlockSpec((1, 1, hid, d), w_map)],
        out_specs=pl.BlockSpec((EXPERT_TILE, d), row_map),
        scratch_shapes=[pltpu.VMEM((d, hid), jnp.bfloat16), pltpu.VMEM((d, hid), jnp.bfloat16),
                        pltpu.VMEM((hid, d), jnp.bfloat16)],
    )
    return pl.pallas_call(
        _expert_kernel,
        name="moe_experts",
        grid_spec=grid_spec,
        out_shape=jax.ShapeDtypeStruct((n_rows, d), jnp.float32),
        compiler_params=pltpu.CompilerParams(dimension_semantics=("arbitrary",), vmem_limit_bytes=VMEM_LIMIT),
    )(tile_expert, n_used, x_disp, w1, w3, w2)


def _combine_kernel(dest_ref, dest_next_ref, x_ref, route_ref, g_ref, b_ref, yd_hbm, o_ref, ybuf, sem):
    i = pl.program_id(0)
    n = pl.num_programs(0)
    tm = x_ref.shape[0]
    rows = EXPERT_TOP_K * tm

    def start_tile(dref, slot):
        def body(r, c):
            for k in range(EXPERT_TOP_K):
                row = k * tm + r
                pltpu.make_async_copy(
                    yd_hbm.at[pl.ds(dref[0, 0, row], 1)], ybuf.at[slot, pl.ds(row, 1)], sem.at[slot]).start(priority=k)
            return c
        lax.fori_loop(0, tm, body, 0, unroll=8)

    slot = i % 2

    @pl.when(i == 0)
    def _():
        start_tile(dest_ref, 0)

    @pl.when(i + 1 < n)
    def _():
        start_tile(dest_next_ref, 1 - slot)

    pltpu.make_async_copy(yd_hbm.at[pl.ds(0, rows)], ybuf.at[slot], sem.at[slot]).wait()
    y1 = ybuf[slot, 0:tm, :]
    y2 = ybuf[slot, tm:rows, :]
    route = route_ref[...].T
    y = DN_ALPHA * x_ref[...] + (y1 * route[:, 0:1] + y2 * route[:, 1:2])
    o_ref[...] = _ln_rows(y, g_ref[...], b_ref[...])


def _moe_combine_ln(xt, y_disp, dest2, route, ln_g, ln_b):
    t, d = xt.shape
    n = t // ROUTE_TILE
    rows = EXPERT_TOP_K * ROUTE_TILE
    return pl.pallas_call(
        _combine_kernel,
        name="moe_combine_ln",
        grid=(n,),
        in_specs=[pl.BlockSpec((1, 1, rows), lambda i: (i, 0, 0), memory_space=pltpu.SMEM),
                  pl.BlockSpec((1, 1, rows), lambda i: (jnp.minimum(i + 1, n - 1), 0, 0), memory_space=pltpu.SMEM),
                  pl.BlockSpec((ROUTE_TILE, d), lambda i: (i, 0)),
                  pl.BlockSpec((ROUTE_ROWS, ROUTE_TILE), lambda i: (0, i)),
                  pl.BlockSpec((1, d), lambda i: (0, 0)),
                  pl.BlockSpec((1, d), lambda i: (0, 0)),
                  pl.BlockSpec(memory_space=pl.ANY)],
        out_specs=pl.BlockSpec((ROUTE_TILE, d), lambda i: (i, 0)),
        out_shape=jax.ShapeDtypeStruct((t, d), jnp.float32),
        scratch_shapes=[pltpu.VMEM((2, rows, d), jnp.float32), pltpu.SemaphoreType.DMA((2,))],
        compiler_params=pltpu.CompilerParams(dimension_semantics=("arbitrary",), vmem_limit_bytes=VMEM_LIMIT),
    )(dest2, dest2, xt, route, ln_g.reshape(1, d), ln_b.reshape(1, d), y_disp)


def _moe_sublayer(x, wg, bg, we, be, w1, w3, w2, layer, ln_g, ln_b):
    b, s, d = x.shape
    t = b * s
    xt = x.reshape(t, d)
    route, cnt = _moe_route(xt, wg, bg, we, be)
    counts = cnt[N_GROUPS:N_GROUPS + N_EXPERTS, 0].astype(jnp.int32)
    n_tiles = (t * EXPERT_TOP_K) // EXPERT_TILE + N_EXPERTS
    tiles_per = (counts + EXPERT_TILE - 1) // EXPERT_TILE
    tile_end = jnp.cumsum(tiles_per)
    pad_start = (tile_end - tiles_per) * EXPERT_TILE
    n_used = tile_end[-1:].astype(jnp.int32)
    tile_ids = jnp.minimum(jnp.arange(n_tiles), n_used[0] - 1)
    tile_expert = jnp.sum(tile_ids[:, None] >= tile_end[None, :], axis=1).astype(jnp.int32)
    experts = route[2:4].astype(jnp.int32)
    first_row = jnp.sum(jnp.where(experts[..., None] == jnp.arange(N_EXPERTS), pad_start, 0), axis=-1)
    dest = first_row + route[4:6].astype(jnp.int32)
    dest2 = jnp.swapaxes(dest.reshape(EXPERT_TOP_K, t // ROUTE_TILE, ROUTE_TILE), 0, 1)
    dest2 = dest2.reshape(t // ROUTE_TILE, 1, EXPERT_TOP_K * ROUTE_TILE)
    x_disp = _moe_dispatch(xt, dest2, n_tiles * EXPERT_TILE)
    y_disp = _moe_experts(x_disp, tile_expert, n_used, w1, w3, w2, layer)
    return _moe_combine_ln(xt, y_disp, dest2, route, ln_g, ln_b).reshape(b, s, d)


def kernel(x, mem, mem_wk, mem_wv, ev_w_in, ev_conv_w, ev_conv_b, ev_cnorm_g, ev_cnorm_b, ev_cmp_pe_k, ev_cmp_w1_k, ev_cmp_w2_k, ev_cmp_pe_v, ev_cmp_w1_v, ev_cmp_w2_v, ev_w_out, od_w_in, od_conv_w, od_w_out, ln_mix_g, ln_mix_b, xa_wq, xa_wo, ln_xa_g, ln_xa_b, moe_wg, moe_bg, moe_we, moe_be, moe_w1, moe_w3, moe_w2, ln_ffn_g, ln_ffn_b):
    b, s, d = x.shape
    mem_k, mem_v = _mem_kv(mem, mem_wk, mem_wv)
    for layer in range(DEPTH):
        i = layer // 2
        if layer % 2 == 0:
            a, kv_in, k_sel, k_win, q_t, v_sel_t, v_win_t, gate_t = _even_in_proj(x, ev_w_in[i])
            a = _conformer_conv(a, ev_conv_w[i], ev_conv_b[i], ev_cnorm_g[i], ev_cnorm_b[i])
            k_cmp, v_cmp_t = _compress_kv(kv_in, ev_cmp_pe_k[i], ev_cmp_w1_k[i], ev_cmp_w2_k[i], ev_cmp_pe_v[i], ev_cmp_w1_v[i], ev_cmp_w2_v[i])
            o = _nsa_attention(q_t, gate_t, k_cmp, v_cmp_t, k_sel, v_sel_t, k_win, v_win_t)
            x = _proj_residual_ln(a.reshape(b * s, -1), o.reshape(b * s, -1), x.reshape(b * s, d), ev_w_out[i], ln_mix_g[layer], ln_mix_b[layer]).reshape(b, s, d)
        else:
            x = _odd_mixer_sublayer(x, od_w_in[i], od_conv_w[i], od_w_out[i], ln_mix_g[layer], ln_mix_b[layer])
        x = _xattn_sublayer(x, mem_k, mem_v, xa_wq[layer], xa_wo[layer], ln_xa_g[layer], ln_xa_b[layer])
        x = _moe_sublayer(x, moe_wg[layer], moe_bg[layer], moe_we[layer], moe_be[layer], moe_w1, moe_w3, moe_w2, layer, ln_ffn_g[layer], ln_ffn_b[layer])
    return x
```

```python
import functools

import numpy as np
import jax
import jax.numpy as jnp
from jax import lax
from jax.experimental import pallas as pl
from jax.experimental.pallas import tpu as pltpu

D_MODEL = 1024
DEPTH = 2
CONV_CH = D_MODEL // 2
CONV_WIDTH = 31
NSA_HEADS = 8
NSA_KV_HEADS = 2
HEAD_DIM = (D_MODEL // 2) // NSA_HEADS
CMP_BLOCK = 32
CMP_STRIDE = 16
SEL_BLOCK = 64
SEL_TOP_N = 16
WINDOW = 512
Q_BLOCK = 256
FORCE_SCORE = 1e4
SHORT_CONV_WIDTH = 3
XA_HEADS = 4
XA_HEAD_DIM = D_MODEL // XA_HEADS
N_GROUPS = 4
EXPERTS_PER_GROUP = 8
N_EXPERTS = N_GROUPS * EXPERTS_PER_GROUP
EXPERT_TOP_K = 2
DN_ALPHA = (2 * DEPTH) ** 0.25
LN_EPS = 1e-5
NEG_INF = -1e30
KV_COLS = NSA_KV_HEADS * HEAD_DIM
QCOLS = NSA_HEADS * HEAD_DIM
GATE_ROWS = 3 * NSA_HEADS

LANES = 128
SUBLANES = 8
HEADS_PER_KV = NSA_HEADS // NSA_KV_HEADS
QL = Q_BLOCK * HEADS_PER_KV
SEL_TILE = 1024
WIN_SPAN = WINDOW + Q_BLOCK
BLOCKS_PER_TILE = SEL_TILE // SEL_BLOCK
SEQ_TILE = 1024
HALO = 32
ODD_HALO = 8
ROUTE_TILE = 512
EXPERT_TILE = 512
ROUTE_ROWS = 8
VMEM_LIMIT = 56 * 1024 * 1024

_NT = (((1,), (1,)), ((), ()))


def _ln_rows(y, g, b):
    mu = jnp.mean(y, axis=-1, keepdims=True)
    yc = y - mu
    var = jnp.mean(yc * yc, axis=-1, keepdims=True)
    return yc * lax.rsqrt(var + LN_EPS) * g + b


def _even_in_kernel(x_ref, wa_ref, wkv_ref, wk2_ref, wqT_ref, wvT_ref, wgT_ref,
                    a_ref, kv_ref, ks_ref, kw_ref, qT_ref, vsT_ref, vwT_ref, gT_ref):
    f32 = jnp.float32
    bf16 = jnp.bfloat16
    xb = x_ref[0].astype(bf16)
    av = jnp.dot(xb, wa_ref[...], preferred_element_type=f32)
    a_ref[0] = av[:, :CONV_CH] * jax.nn.sigmoid(av[:, CONV_CH:])
    kv_ref[0] = jnp.dot(xb, wkv_ref[...], preferred_element_type=f32)
    k2 = jnp.dot(xb, wk2_ref[...], preferred_element_type=f32)
    ks_ref[0] = k2[:, :KV_COLS].astype(bf16)
    kw_ref[0] = k2[:, KV_COLS:].astype(bf16)
    qT_ref[0] = lax.dot_general(wqT_ref[...], xb, _NT, preferred_element_type=f32).astype(bf16)
    vT = lax.dot_general(wvT_ref[...], xb, _NT, preferred_element_type=f32).astype(bf16)
    vsT_ref[0, 0] = vT[:KV_COLS]
    for j in range(SEQ_TILE // Q_BLOCK):
        vwT_ref[0, j] = vT[KV_COLS:, j * Q_BLOCK:(j + 1) * Q_BLOCK]
    gT_ref[0] = lax.dot_general(wgT_ref[...], xb, _NT, preferred_element_type=f32)


def _even_in_proj(x, w_in):
    b, s, d = x.shape
    bf16 = jnp.bfloat16
    c = np.cumsum((0, CONV_CH, CONV_CH, QCOLS, KV_COLS, KV_COLS, KV_COLS, KV_COLS, KV_COLS, KV_COLS, GATE_ROWS))
    col = lambda i, j: w_in[:, c[i]:c[j]]
    wa = col(0, 2).astype(bf16)
    wq_t = col(2, 3).T.astype(bf16)
    wkv = col(3, 5).astype(bf16)
    wk2 = jnp.concatenate([col(5, 6), col(7, 8)], axis=1).astype(bf16)
    wv_t = jnp.concatenate([col(6, 7), col(8, 9)], axis=1).T.astype(bf16)
    wg_t = col(9, 10).T.astype(bf16)
    ts = SEQ_TILE
    per_sel = SEL_TILE // ts
    full = lambda shape: pl.BlockSpec(shape, lambda i, j: (0,) * len(shape))
    return pl.pallas_call(
        _even_in_kernel,
        name="even_in_proj",
        grid=(b, s // ts),
        in_specs=[pl.BlockSpec((1, ts, d), lambda i, j: (i, j, 0)),
                  full(wa.shape), full(wkv.shape), full(wk2.shape), full(wq_t.shape), full(wv_t.shape), full(wg_t.shape)],
        out_specs=[pl.BlockSpec((1, ts, CONV_CH), lambda i, j: (i, j, 0)),
                   pl.BlockSpec((1, ts, 2 * KV_COLS), lambda i, j: (i, j, 0)),
                   pl.BlockSpec((1, ts, KV_COLS), lambda i, j: (i, j, 0)),
                   pl.BlockSpec((1, ts, KV_COLS), lambda i, j: (i, j, 0)),
                   pl.BlockSpec((1, QCOLS, ts), lambda i, j: (i, 0, j)),
                   pl.BlockSpec((1, 1, KV_COLS, ts), lambda i, j: (i, j // per_sel, 0, j % per_sel)),
                   pl.BlockSpec((1, ts // Q_BLOCK, KV_COLS, Q_BLOCK), lambda i, j: (i, j, 0, 0)),
                   pl.BlockSpec((1, GATE_ROWS, ts), lambda i, j: (i, 0, j))],
        out_shape=[jax.ShapeDtypeStruct((b, s, CONV_CH), jnp.float32),
                   jax.ShapeDtypeStruct((b, s, 2 * KV_COLS), jnp.float32),
                   jax.ShapeDtypeStruct((b, s, KV_COLS), bf16),
                   jax.ShapeDtypeStruct((b, s, KV_COLS), bf16),
                   jax.ShapeDtypeStruct((b, QCOLS, s), bf16),
                   jax.ShapeDtypeStruct((b, s // SEL_TILE, KV_COLS, SEL_TILE), bf16),
                   jax.ShapeDtypeStruct((b, s // Q_BLOCK, KV_COLS, Q_BLOCK), bf16),
                   jax.ShapeDtypeStruct((b, GATE_ROWS, s), jnp.float32)],
        compiler_params=pltpu.CompilerParams(dimension_semantics=("arbitrary", "arbitrary"), vmem_limit_bytes=VMEM_LIMIT),
    )(x, wa, wkv, wk2, wq_t, wv_t, wg_t)


def _conv_kernel(cur_ref, halo_ref, w_ref, cb_ref, g_ref, b_ref, o_ref, ext_ref, win_ref):
    j = pl.program_id(1)
    ts = cur_ref.shape[1]
    halo = halo_ref[0]
    ext_ref[0:HALO, :] = jnp.where(j > 0, halo, jnp.zeros_like(halo))
    ext_ref[HALO:HALO + ts, :] = cur_ref[0]
    first = HALO - (CONV_WIDTH - 1)
    acc = jnp.zeros((ts, CONV_CH), jnp.float32)
    for p in range(SUBLANES):
        n_a = len(range(p, CONV_WIDTH, SUBLANES))
        rows = ts + SUBLANES * (n_a - 1)
        win_ref[0:rows, :] = ext_ref[first + p:first + p + rows, :]
        for a in range(n_a):
            k = SUBLANES * a + p
            acc = acc + w_ref[k:k + 1, :] * win_ref[SUBLANES * a:SUBLANES * a + ts, :]
    y = _ln_rows(acc + cb_ref[...], g_ref[...], b_ref[...])
    o_ref[0] = (y * jax.nn.sigmoid(y)).astype(o_ref.dtype)


def _conformer_conv(a, conv_w, conv_b, cn_g, cn_b):
    b, s, c = a.shape
    ts = SEQ_TILE
    per = ts // HALO
    row = lambda v: v.reshape(1, c)
    return pl.pallas_call(
        _conv_kernel,
        name="conformer_conv",
        grid=(b, s // ts),
        in_specs=[pl.BlockSpec((1, ts, c), lambda i, j: (i, j, 0)),
                  pl.BlockSpec((1, HALO, c), lambda i, j: (i, jnp.maximum(j * per - 1, 0), 0)),
                  pl.BlockSpec((CONV_WIDTH, c), lambda i, j: (0, 0)),
                  pl.BlockSpec((1, c), lambda i, j: (0, 0)),
                  pl.BlockSpec((1, c), lambda i, j: (0, 0)),
                  pl.BlockSpec((1, c), lambda i, j: (0, 0))],
        out_specs=pl.BlockSpec((1, ts, c), lambda i, j: (i, j, 0)),
        out_shape=jax.ShapeDtypeStruct((b, s, c), jnp.bfloat16),
        scratch_shapes=[pltpu.VMEM((HALO + ts, c), jnp.float32), pltpu.VMEM((HALO + ts, c), jnp.float32)],
        compiler_params=pltpu.CompilerParams(dimension_semantics=("arbitrary", "arbitrary")),
    )(a, a, conv_w, row(conv_b), row(cn_g), row(cn_b))


def _compress_kernel(r_ref, pe_ref, w1_ref, w2_ref, w2T_ref, o_ref, oT_ref):
    f32 = jnp.float32
    bf16 = jnp.bfloat16
    nch = o_ref.shape[1]
    hw = w1_ref.shape[3]
    first = jnp.zeros((nch, hw), f32)
    second = jnp.zeros((nch, hw), f32)
    for l in range(CMP_STRIDE):
        tok = r_ref[pl.ds(l, nch, stride=CMP_STRIDE), :]
        first = first + jnp.dot((tok + pe_ref[0, l:l + 1, :]).astype(bf16), w1_ref[0, l], preferred_element_type=f32)
        second = second + jnp.dot((tok + pe_ref[0, CMP_STRIDE + l:CMP_STRIDE + l + 1, :]).astype(bf16),
                                  w1_ref[0, CMP_STRIDE + l], preferred_element_type=f32)
    second_next = jnp.concatenate([second[1:], jnp.zeros_like(second[0:1])], axis=0)
    hid = jax.nn.gelu(first + second_next).astype(bf16)
    o_ref[0] = jnp.dot(hid, w2_ref[0], preferred_element_type=f32).astype(bf16)
    oT_ref[0] = lax.dot_general(w2T_ref[0], hid, _NT, preferred_element_type=f32).astype(bf16)


def _compress_kv(kv_in, pe_k, w1_k, w2_k, pe_v, w1_v, w2_v):
    b, s, _ = kv_in.shape
    bf16 = jnp.bfloat16
    nch = s // CMP_STRIDE
    g = NSA_KV_HEADS
    hidden = w1_k.shape[1]

    def expand(pe, w1, w2):
        w1r = w1.reshape(CMP_BLOCK, HEAD_DIM, hidden)
        w1e = jnp.zeros((CMP_BLOCK, g, HEAD_DIM, g, hidden), w1.dtype)
        w2e = jnp.zeros((g, hidden, g, HEAD_DIM), w2.dtype)
        for gi in range(g):
            w1e = w1e.at[:, gi, :, gi, :].set(w1r)
            w2e = w2e.at[gi, :, gi, :].set(w2)
        return jnp.tile(pe, (1, g)), w1e.reshape(CMP_BLOCK, KV_COLS, g * hidden), w2e.reshape(g * hidden, KV_COLS)

    pk, w1k, w2k = expand(pe_k, w1_k, w2_k)
    pv, w1v, w2v = expand(pe_v, w1_v, w2_v)
    pe = jnp.stack([pk, pv])
    w1 = jnp.stack([w1k, w1v]).astype(bf16)
    w2 = jnp.stack([w2k, w2v]).astype(bf16)
    w2t = jnp.swapaxes(w2, 1, 2)
    o, o_t = pl.pallas_call(
        _compress_kernel,
        name="compress_kv",
        grid=(b, 2),
        in_specs=[pl.BlockSpec((s, KV_COLS), lambda i, j: (i, j)),
                  pl.BlockSpec((1,) + pe.shape[1:], lambda i, j: (j, 0, 0)),
                  pl.BlockSpec((1,) + w1.shape[1:], lambda i, j: (j, 0, 0, 0)),
                  pl.BlockSpec((1,) + w2.shape[1:], lambda i, j: (j, 0, 0)),
                  pl.BlockSpec((1,) + w2t.shape[1:], lambda i, j: (j, 0, 0))],
        out_specs=[pl.BlockSpec((1, nch, KV_COLS), lambda i, j: (2 * i + j, 0, 0)),
                   pl.BlockSpec((1, KV_COLS, nch), lambda i, j: (2 * i + j, 0, 0))],
        out_shape=[jax.ShapeDtypeStruct((b * 2, nch, KV_COLS), bf16), jax.ShapeDtypeStruct((b * 2, KV_COLS, nch), bf16)],
        compiler_params=pltpu.CompilerParams(dimension_semantics=("arbitrary", "arbitrary"), vmem_limit_bytes=VMEM_LIMIT),
    )(kv_in.reshape(b * s, 2 * KV_COLS), pe, w1, w2, w2t)
    return o.reshape(b, 2, nch, KV_COLS)[:, 0], o_t.reshape(b, 2, KV_COLS, nch)[:, 1]


def _proj_ln_kernel(a_ref, o_ref, x_ref, w_ref, g_ref, b_ref, y_ref):
    lhs = jnp.concatenate([a_ref[...], o_ref[...]], axis=1)
    mix = jnp.dot(lhs, w_ref[...], preferred_element_type=jnp.float32)
    y_ref[...] = _ln_rows(DN_ALPHA * x_ref[...] + mix, g_ref[...], b_ref[...])


def _proj_residual_ln(a, o, x, w, ln_g, ln_b):
    t, d = x.shape
    tm = SEQ_TILE
    return pl.pallas_call(
        _proj_ln_kernel,
        name="proj_residual_ln",
        grid=(t // tm,),
        in_specs=[pl.BlockSpec((tm, a.shape[1]), lambda i: (i, 0)),
                  pl.BlockSpec((tm, o.shape[1]), lambda i: (i, 0)),
                  pl.BlockSpec((tm, d), lambda i: (i, 0)),
                  pl.BlockSpec(w.shape, lambda i: (0, 0)),
                  pl.BlockSpec((1, d), lambda i: (0, 0)),
                  pl.BlockSpec((1, d), lambda i: (0, 0))],
        out_specs=pl.BlockSpec((tm, d), lambda i: (i, 0)),
        out_shape=jax.ShapeDtypeStruct((t, d), jnp.float32),
        compiler_params=pltpu.CompilerParams(dimension_semantics=("arbitrary",), vmem_limit_bytes=VMEM_LIMIT),
    )(a, o, x, w.astype(jnp.bfloat16), ln_g.reshape(1, d), ln_b.reshape(1, d))


def _mem_kv_kernel(m_ref, wk_ref, wv_ref, k_ref, v_ref):
    mb = m_ref[...].astype(jnp.bfloat16)
    k_ref[...] = jnp.dot(mb, wk_ref[...], preferred_element_type=jnp.float32).astype(jnp.bfloat16)
    v_ref[...] = jnp.dot(mb, wv_ref[...], preferred_element_type=jnp.float32).astype(jnp.bfloat16)


def _mem_kv(mem, wk, wv):
    b, m, d = mem.shape
    bf16 = jnp.bfloat16
    k, v = pl.pallas_call(
        _mem_kv_kernel,
        name="mem_kv",
        grid=(b,),
        in_specs=[pl.BlockSpec((m, d), lambda i: (i, 0)), pl.BlockSpec((d, d), lambda i: (0, 0)), pl.BlockSpec((d, d), lambda i: (0, 0))],
        out_specs=[pl.BlockSpec((m, d), lambda i: (i, 0)), pl.BlockSpec((m, d), lambda i: (i, 0))],
        out_shape=[jax.ShapeDtypeStruct((b * m, d), bf16), jax.ShapeDtypeStruct((b * m, d), bf16)],
        compiler_params=pltpu.CompilerParams(dimension_semantics=("arbitrary",), vmem_limit_bytes=VMEM_LIMIT),
    )(mem.reshape(b * m, d), wk.astype(bf16), wv.astype(bf16))
    return k.reshape(b, m, d), v.reshape(b, m, d)


def _xattn_kernel(x_ref, k_ref, v_ref, wq_ref, wo_ref, g_ref, b_ref, y_ref):
    f32 = jnp.float32
    bf16 = jnp.bfloat16
    x = x_ref[0]
    q = (jnp.dot(x.astype(bf16), wq_ref[...], preferred_element_type=f32).astype(bf16)
         * jnp.asarray(XA_HEAD_DIM ** -0.5, bf16))
    heads = []
    for h in range(XA_HEADS):
        cols = slice(h * XA_HEAD_DIM, (h + 1) * XA_HEAD_DIM)
        s = lax.dot_general(q[:, cols], k_ref[0, :, cols], _NT, preferred_element_type=f32)
        m = jnp.max(s, axis=1, keepdims=True)
        p = jnp.exp(s - m)
        p = p / jnp.sum(p, axis=1, keepdims=True)
        heads.append(jnp.dot(p.astype(bf16), v_ref[0, :, cols], preferred_element_type=f32).astype(bf16))
    att = jnp.concatenate(heads, axis=1)
    out = jnp.dot(att, wo_ref[...], preferred_element_type=f32)
    y_ref[0] = _ln_rows(DN_ALPHA * x + out, g_ref[...], b_ref[...])


def _xattn_sublayer(x, mem_k, mem_v, wq, wo, ln_g, ln_b):
    b, s, d = x.shape
    m = mem_k.shape[1]
    ts = SEQ_TILE
    bf16 = jnp.bfloat16
    return pl.pallas_call(
        _xattn_kernel,
        name="xattn_sublayer",
        grid=(b, s // ts),
        in_specs=[pl.BlockSpec((1, ts, d), lambda i, j: (i, j, 0)),
                  pl.BlockSpec((1, m, d), lambda i, j: (i, 0, 0)),
                  pl.BlockSpec((1, m, d), lambda i, j: (i, 0, 0)),
                  pl.BlockSpec((d, d), lambda i, j: (0, 0)),
                  pl.BlockSpec((d, d), lambda i, j: (0, 0)),
                  pl.BlockSpec((1, d), lambda i, j: (0, 0)),
                  pl.BlockSpec((1, d), lambda i, j: (0, 0))],
        out_specs=pl.BlockSpec((1, ts, d), lambda i, j: (i, j, 0)),
        out_shape=jax.ShapeDtypeStruct((b, s, d), jnp.float32),
        compiler_params=pltpu.CompilerParams(dimension_semantics=("arbitrary", "arbitrary"), vmem_limit_bytes=VMEM_LIMIT),
    )(x, mem_k, mem_v, wq.astype(bf16), wo.astype(bf16), ln_g.reshape(1, d), ln_b.reshape(1, d))


def _odd_kernel(x_ref, halo_ref, wb_ref, wc_ref, wh_ref, cw_ref, wo_ref, g_ref, b_ref, y_ref):
    f32 = jnp.float32
    bf16 = jnp.bfloat16
    j = pl.program_id(1)
    ts = x_ref.shape[1]
    x = x_ref[0]
    xe = jnp.concatenate([halo_ref[0], x], axis=0).astype(bf16)
    u = (jnp.dot(xe, wc_ref[...], preferred_element_type=f32) * jnp.dot(xe, wh_ref[...], preferred_element_type=f32))
    row = lax.broadcasted_iota(jnp.int32, (ODD_HALO + ts, 1), 0)
    u = jnp.where((row >= ODD_HALO) | (j > 0), u, 0.0)
    conv = jnp.zeros((ts, u.shape[1]), f32)
    for k in range(SHORT_CONV_WIDTH):
        off = ODD_HALO - (SHORT_CONV_WIDTH - 1) + k
        conv = conv + cw_ref[k:k + 1, :] * u[off:off + ts, :]
    gate_b = jnp.dot(xe[ODD_HALO:], wb_ref[...], preferred_element_type=f32)
    mix = jnp.dot((gate_b * conv).astype(bf16), wo_ref[...], preferred_element_type=f32)
    y_ref[0] = _ln_rows(DN_ALPHA * x + mix, g_ref[...], b_ref[...])


def _odd_mixer_sublayer(x, w_in, conv_w, w_out, ln_g, ln_b):
    b, s, d = x.shape
    ts = SEQ_TILE
    per = ts // ODD_HALO
    bf16 = jnp.bfloat16
    wb, wc, wh = (w_in[:, i * d:(i + 1) * d].astype(bf16) for i in range(3))
    full = lambda shape: pl.BlockSpec(shape, lambda i, j: (0,) * len(shape))
    return pl.pallas_call(
        _odd_kernel,
        name="odd_mixer_sublayer",
        grid=(b, s // ts),
        in_specs=[pl.BlockSpec((1, ts, d), lambda i, j: (i, j, 0)),
                  pl.BlockSpec((1, ODD_HALO, d), lambda i, j: (i, jnp.maximum(j * per - 1, 0), 0)),
                  full((d, d)), full((d, d)), full((d, d)), full(conv_w.shape), full((d, d)), full((1, d)), full((1, d))],
        out_specs=pl.BlockSpec((1, ts, d), lambda i, j: (i, j, 0)),
        out_shape=jax.ShapeDtypeStruct((b, s, d), jnp.float32),
        compiler_params=pltpu.CompilerParams(dimension_semantics=("arbitrary", "arbitrary"), vmem_limit_bytes=VMEM_LIMIT),
    )(x, x, wb, wc, wh, conv_w, w_out.astype(bf16), ln_g.reshape(1, d), ln_b.reshape(1, d))


def _cmp_to_sel_matrix(n_cmp, n_sel):
    c0 = np.arange(n_cmp) * CMP_STRIDE
    s0 = np.arange(n_sel) * SEL_BLOCK
    ov = np.minimum(c0[:, None] + CMP_BLOCK, s0[None, :] + SEL_BLOCK) - np.maximum(c0[:, None], s0[None, :])
    return (np.clip(ov, 0, None) / CMP_BLOCK).astype(np.float32)


def _nsa_kernel(qT_ref, gT_ref, kc_ref, vcT_ref, mselT_ref, ks_ref, vsT_ref, kw_ref, vwT_ref, o_ref,
                score_ref, sel_ref, *, n_sel):
    c = pl.program_id(1)
    q0 = c * Q_BLOCK
    f32 = jnp.float32
    bf16 = jnp.bfloat16
    n_cmp_pad = kc_ref.shape[1]

    lane_q = lax.broadcasted_iota(jnp.int32, (1, QL), 1) % Q_BLOCK
    t_row = q0 + lane_q
    t_row_q = q0 + lax.broadcasted_iota(jnp.int32, (1, Q_BLOCK), 1)
    cur_q = t_row_q // SEL_BLOCK

    top_n = min(SEL_TOP_N, n_sel)
    m_iota = lax.broadcasted_iota(jnp.int32, (n_sel, Q_BLOCK), 0)
    forced = (m_iota == 0) | (m_iota == cur_q) | (m_iota == cur_q - 1)
    valid = m_iota <= cur_q
    n_comp = jnp.minimum(q0 // SEL_BLOCK + Q_BLOCK // SEL_BLOCK, n_sel)

    q_g, o_c = [], []
    for g in range(NSA_KV_HEADS):
        pieces = []
        for hg in range(HEADS_PER_KV):
            h = g * HEADS_PER_KV + hg
            qh = qT_ref[0, h * HEAD_DIM:(h + 1) * HEAD_DIM, :] * jnp.asarray(HEAD_DIM ** -0.5, bf16)
            z = jnp.zeros_like(qh)
            pieces.append(jnp.concatenate([qh, z] if g == 0 else [z, qh], axis=0))
        qTp = jnp.concatenate(pieces, axis=1)
        q_g.append(qTp)
        rows = slice(g * HEAD_DIM, (g + 1) * HEAD_DIM)

        s_c = jnp.dot(kc_ref[0], qTp, preferred_element_type=f32)
        n_iota = lax.broadcasted_iota(jnp.int32, (n_cmp_pad, QL), 0)
        mask_c = (n_iota * CMP_STRIDE + (CMP_BLOCK - 1)) <= t_row
        s_c = jnp.where(mask_c, s_c, NEG_INF)
        m_c = jnp.max(s_c, axis=0, keepdims=True)
        p_c = jnp.where(mask_c, jnp.exp(s_c - m_c), 0.0)
        l_c = jnp.sum(p_c, axis=0, keepdims=True)
        p_c = p_c * jnp.where(l_c > 0.0, 1.0 / l_c, 0.0)
        p_cb = p_c.astype(bf16)
        o_c.append(jnp.dot(vcT_ref[0], p_cb, preferred_element_type=f32)[rows])
        imp4 = jnp.dot(mselT_ref[...], p_cb, preferred_element_type=f32)
        imp = imp4[:, 0:Q_BLOCK]
        for hg in range(1, HEADS_PER_KV):
            imp = imp + imp4[:, hg * Q_BLOCK:(hg + 1) * Q_BLOCK]

        score = jnp.where(valid, jnp.where(forced, FORCE_SCORE, imp), NEG_INF)
        score_ref[...] = score

        def rank_body(i8, rank, score=score):
            base = pl.multiple_of(i8 * SUBLANES, SUBLANES)
            rows = score_ref[pl.ds(base, SUBLANES), :]
            for u in range(SUBLANES):
                row = rows[u:u + 1, :]
                beats = (row > score) | ((row == score) & (base + u < m_iota))
                rank = rank + beats.astype(jnp.int32)
            return rank

        rank = lax.fori_loop(0, (n_comp + SUBLANES - 1) // SUBLANES, rank_body,
                             jnp.zeros((n_sel, Q_BLOCK), jnp.int32))
        sel_ref[g] = jnp.where((rank < top_n) & valid, 0.0, NEG_INF)

    def sel_tile(j, carry, causal):
        k_tile = ks_ref[0, j]
        v_tile = vsT_ref[0, j]
        blk0 = pl.multiple_of(j * BLOCKS_PER_TILE, BLOCKS_PER_TILE)
        out = []
        for g in range(NSA_KV_HEADS):
            m_i, l_i, acc = carry[g]
            s = jnp.dot(k_tile, q_g[g], preferred_element_type=f32)
            selrows = sel_ref[g, pl.ds(blk0, BLOCKS_PER_TILE), :]
            bias = jnp.concatenate(
                [jnp.broadcast_to(selrows[r:r + 1, :], (SEL_BLOCK, Q_BLOCK)) for r in range(BLOCKS_PER_TILE)], axis=0)
            if causal:
                key = j * SEL_TILE + lax.broadcasted_iota(jnp.int32, (SEL_TILE, Q_BLOCK), 0)
                bias = jnp.where(key <= t_row_q, bias, NEG_INF)
            s = s + jnp.concatenate([bias] * HEADS_PER_KV, axis=1)
            m_new = jnp.maximum(m_i, jnp.max(s, axis=0, keepdims=True))
            alpha = jnp.exp(m_i - m_new)
            p = jnp.exp(s - m_new)
            l_new = alpha * l_i + jnp.sum(p, axis=0, keepdims=True)
            pv = jnp.dot(v_tile, p.astype(bf16), preferred_element_type=f32)
            out.append((m_new, l_new, alpha * acc + pv))
        return tuple(out)

    n_full = q0 // SEL_TILE
    init = tuple((jnp.full((1, QL), NEG_INF, f32), jnp.zeros((1, QL), f32), jnp.zeros((KV_COLS, QL), f32))
                 for _ in range(NSA_KV_HEADS))
    carry = lax.fori_loop(0, n_full, functools.partial(sel_tile, causal=False), init)
    carry = sel_tile(n_full, carry, causal=True)

    start = pl.multiple_of(jnp.maximum(q0 - WINDOW, 0), Q_BLOCK)
    j0 = start // Q_BLOCK
    k_win = kw_ref[0, pl.ds(start, WIN_SPAN), :]
    key_w = start + lax.broadcasted_iota(jnp.int32, (WIN_SPAN, Q_BLOCK), 0)
    bias_w = jnp.where(key_w <= t_row_q, 0.0, NEG_INF)
    bias_w = jnp.where(key_w > t_row_q - WINDOW, bias_w, NEG_INF)
    bias_w = jnp.concatenate([bias_w] * HEADS_PER_KV, axis=1)
    for g in range(NSA_KV_HEADS):
        rows = slice(g * HEAD_DIM, (g + 1) * HEAD_DIM)
        m_s, l_s, acc_s = carry[g]
        o_s = acc_s[rows] * (1.0 / l_s)
        s_w = jnp.dot(k_win, q_g[g], preferred_element_type=f32) + bias_w
        m_w = jnp.max(s_w, axis=0, keepdims=True)
        p_w = jnp.exp(s_w - m_w)
        l_w = jnp.sum(p_w, axis=0, keepdims=True)
        p_wb = p_w.astype(bf16)
        acc_w = jnp.zeros((KV_COLS, QL), f32)
        for i in range(WIN_SPAN // Q_BLOCK):
            acc_w = acc_w + jnp.dot(vwT_ref[0, j0 + i], p_wb[i * Q_BLOCK:(i + 1) * Q_BLOCK, :], preferred_element_type=f32)
        o_w = acc_w[rows] * (1.0 / l_w)

        for pair in range(HEADS_PER_KV // 2):
            halves = []
            for hg in (2 * pair, 2 * pair + 1):
                h = g * HEADS_PER_KV + hg
                lanes = slice(hg * Q_BLOCK, (hg + 1) * Q_BLOCK)
                gate = jax.nn.sigmoid(gT_ref[0, 3 * h:3 * h + 3, :])
                halves.append(gate[0:1] * o_c[g][:, lanes] + gate[1:2] * o_s[:, lanes] + gate[2:3] * o_w[:, lanes])
            both = jnp.concatenate(halves, axis=0)
            col0 = (g * HEADS_PER_KV + 2 * pair) * HEAD_DIM
            o_ref[0, :, col0:col0 + 2 * HEAD_DIM] = both.T.astype(o_ref.dtype)


def _nsa_attention(qT, gT, kc, vcT, ks, vsT, kw, vwT):
    b, hd, s = qT.shape
    n_sel = s // SEL_BLOCK
    nc = kc.shape[1]
    n_cmp = (s - CMP_BLOCK) // CMP_STRIDE + 1
    mselT = jnp.asarray(np.pad(_cmp_to_sel_matrix(n_cmp, n_sel).T, ((0, 0), (0, nc - n_cmp))), jnp.bfloat16)
    ks4 = ks.reshape(b, s // SEL_TILE, SEL_TILE, KV_COLS)
    return pl.pallas_call(
        functools.partial(_nsa_kernel, n_sel=n_sel),
        name="nsa_attention",
        grid=(b, s // Q_BLOCK),
        in_specs=[
            pl.BlockSpec((1, hd, Q_BLOCK), lambda i, c: (i, 0, c)),
            pl.BlockSpec((1, 3 * NSA_HEADS, Q_BLOCK), lambda i, c: (i, 0, c)),
            pl.BlockSpec((1, nc, KV_COLS), lambda i, c: (i, 0, 0)),
            pl.BlockSpec((1, KV_COLS, nc), lambda i, c: (i, 0, 0)),
            pl.BlockSpec((n_sel, nc), lambda i, c: (0, 0)),
            pl.BlockSpec((1, s // SEL_TILE, SEL_TILE, KV_COLS), lambda i, c: (i, 0, 0, 0)),
            pl.BlockSpec((1, s // SEL_TILE, KV_COLS, SEL_TILE), lambda i, c: (i, 0, 0, 0)),
            pl.BlockSpec((1, s, KV_COLS), lambda i, c: (i, 0, 0)),
            pl.BlockSpec((1, s // Q_BLOCK, KV_COLS, Q_BLOCK), lambda i, c: (i, 0, 0, 0)),
        ],
        out_specs=pl.BlockSpec((1, Q_BLOCK, hd), lambda i, c: (i, c, 0)),
        out_shape=jax.ShapeDtypeStruct((b, s, hd), jnp.bfloat16),
        scratch_shapes=[pltpu.VMEM((n_sel, Q_BLOCK), jnp.float32),
                        pltpu.VMEM((NSA_KV_HEADS, n_sel, Q_BLOCK), jnp.float32)],
        compiler_params=pltpu.CompilerParams(dimension_semantics=("arbitrary", "arbitrary")),
    )(qT, gT, kc, vcT, mselT, ks4, vsT, kw, vwT)


def _router_kernel(x_ref, wT_ref, b_ref, tri_ref, route_ref, cnt_ref, run_ref):
    i = pl.program_id(0)
    f32 = jnp.float32

    @pl.when(i == 0)
    def _():
        run_ref[...] = jnp.zeros_like(run_ref)

    tm = x_ref.shape[0]
    logits = lax.dot_general(wT_ref[...], x_ref[...].astype(jnp.bfloat16), _NT, preferred_element_type=f32) + b_ref[...]
    sub = lax.broadcasted_iota(jnp.int32, (LANES, tm), 0)
    is_g = sub < N_GROUPS
    gl = jnp.where(is_g, logits, NEG_INF)
    g_max = jnp.max(gl, axis=0, keepdims=True)
    g_star = jnp.min(jnp.where(gl == g_max, sub, LANES), axis=0, keepdims=True)
    p_group = 1.0 / jnp.sum(jnp.where(is_g, jnp.exp(gl - g_max), 0.0), axis=0, keepdims=True)
    lo = N_GROUPS + g_star * EXPERTS_PER_GROUP
    in_grp = (sub >= lo) & (sub < lo + EXPERTS_PER_GROUP)
    el = jnp.where(in_grp, logits, NEG_INF)
    v1 = jnp.max(el, axis=0, keepdims=True)
    i1 = jnp.min(jnp.where(el == v1, sub, LANES), axis=0, keepdims=True)
    el2 = jnp.where(sub == i1, NEG_INF, el)
    v2 = jnp.max(el2, axis=0, keepdims=True)
    i2 = jnp.min(jnp.where(el2 == v2, sub, LANES), axis=0, keepdims=True)
    e21 = jnp.exp(v2 - v1)
    gate1 = p_group * (1.0 / (1.0 + e21))
    gate2 = p_group * (e21 / (1.0 + e21))
    oh1 = (sub == i1).astype(f32)
    oh2 = (sub == i2).astype(f32)
    both = oh1 + oh2
    before = jnp.dot(both.astype(jnp.bfloat16), tri_ref[...], preferred_element_type=f32) + run_ref[...]
    rank1 = jnp.sum(oh1 * before, axis=0, keepdims=True)
    rank2 = jnp.sum(oh2 * before, axis=0, keepdims=True)
    run_ref[...] = run_ref[...] + jnp.sum(both, axis=1, keepdims=True)
    cnt_ref[...] = run_ref[...]
    zero = jnp.zeros_like(gate1)
    route_ref[...] = jnp.concatenate(
        [gate1, gate2, (i1 - N_GROUPS).astype(f32), (i2 - N_GROUPS).astype(f32), rank1, rank2, zero, zero], axis=0)


def _moe_route(xt, wg, bg, we, be):
    t, d = xt.shape
    pad = LANES - N_GROUPS - N_EXPERTS
    w_t = jnp.concatenate([wg, we.reshape(d, N_EXPERTS), jnp.zeros((d, pad), wg.dtype)], axis=1).T
    bias = jnp.concatenate([bg, be.reshape(N_EXPERTS), jnp.zeros((pad,), bg.dtype)])[:, None]
    tri = jnp.asarray(np.triu(np.ones((ROUTE_TILE, ROUTE_TILE), np.float32), 1), jnp.bfloat16)
    return pl.pallas_call(
        _router_kernel,
        name="moe_router",
        grid=(t // ROUTE_TILE,),
        in_specs=[pl.BlockSpec((ROUTE_TILE, d), lambda i: (i, 0)),
                  pl.BlockSpec((LANES, d), lambda i: (0, 0)),
                  pl.BlockSpec((LANES, 1), lambda i: (0, 0)),
                  pl.BlockSpec((ROUTE_TILE, ROUTE_TILE), lambda i: (0, 0))],
        out_specs=[pl.BlockSpec((ROUTE_ROWS, ROUTE_TILE), lambda i: (0, i)),
                   pl.BlockSpec((LANES, 1), lambda i: (0, 0))],
        out_shape=[jax.ShapeDtypeStruct((ROUTE_ROWS, t), jnp.float32), jax.ShapeDtypeStruct((LANES, 1), jnp.float32)],
        scratch_shapes=[pltpu.VMEM((LANES, 1), jnp.float32)],
        compiler_params=pltpu.CompilerParams(dimension_semantics=("arbitrary",)),
    )(xt, w_t.astype(jnp.bfloat16), bias, tri)


def _dispatch_kernel(dest_ref, x_ref, zero_hbm, xd_hbm, sem):
    del zero_hbm
    tm = x_ref.shape[0]

    def body(r, c):
        for k in range(EXPERT_TOP_K):
            pltpu.make_async_copy(
                x_ref.at[pl.ds(r, 1)], xd_hbm.at[pl.ds(dest_ref[0, 0, k * tm + r], 1)], sem).start(priority=k)
        return c

    lax.fori_loop(0, tm, body, 0, unroll=8)
    for k in range(EXPERT_TOP_K):
        pltpu.make_async_copy(x_ref, xd_hbm.at[pl.ds(0, tm)], sem).wait()


def _moe_dispatch(xt, dest2, n_rows):
    t, d = xt.shape
    return pl.pallas_call(
        _dispatch_kernel,
        name="moe_dispatch",
        grid=(t // ROUTE_TILE,),
        in_specs=[pl.BlockSpec((1, 1, EXPERT_TOP_K * ROUTE_TILE), lambda i: (i, 0, 0), memory_space=pltpu.SMEM),
                  pl.BlockSpec((ROUTE_TILE, d), lambda i: (i, 0)),
                  pl.BlockSpec(memory_space=pl.ANY)],
        out_specs=pl.BlockSpec(memory_space=pl.ANY),
        out_shape=jax.ShapeDtypeStruct((n_rows, d), xt.dtype),
        scratch_shapes=[pltpu.SemaphoreType.DMA(())],
        input_output_aliases={2: 0},
        compiler_params=pltpu.CompilerParams(dimension_semantics=("arbitrary",), has_side_effects=True),
    )(dest2, xt, jnp.zeros((n_rows, d), xt.dtype))


def _expert_kernel(te_ref, nu_ref, xd_ref, w1_ref, w3_ref, w2_ref, y_ref, w1b, w3b, w2b):
    i = pl.program_id(0)
    used = i < nu_ref[0]
    new_expert = (i == 0) | (te_ref[i] != te_ref[jnp.maximum(i - 1, 0)])

    @pl.when(used & new_expert)
    def _():
        w1b[...] = w1_ref[0, 0].astype(jnp.bfloat16)
        w3b[...] = w3_ref[0, 0].astype(jnp.bfloat16)
        w2b[...] = w2_ref[0, 0].astype(jnp.bfloat16)

    @pl.when(used)
    def _():
        xb = xd_ref[...].astype(jnp.bfloat16)
        h1 = jnp.dot(xb, w1b[...], preferred_element_type=jnp.float32)
        h3 = jnp.dot(xb, w3b[...], preferred_element_type=jnp.float32)
        a = (h1 * jax.nn.sigmoid(h1) * h3).astype(jnp.bfloat16)
        y_ref[...] = jnp.dot(a, w2b[...], preferred_element_type=jnp.float32)

    @pl.when(jnp.logical_not(used))
    def _():
        y_ref[...] = jnp.zeros_like(y_ref)


def _moe_experts(x_disp, tile_expert, n_used, w1, w3, w2, layer):
    n_rows, d = x_disp.shape
    n_tiles = n_rows // EXPERT_TILE
    hid = w1.shape[3]

    def row_map(i, te, nu):
        return (i, 0)

    def w_map(i, te, nu):
        return (layer, te[i], 0, 0)

    grid_spec = pltpu.PrefetchScalarGridSpec(
        num_scalar_prefetch=2,
        grid=(n_tiles,),
        in_specs=[pl.BlockSpec((EXPERT_TILE, d), row_map),
                  pl.BlockSpec((1, 1, d, hid), w_map),
                  pl.BlockSpec((1, 1, d, hid), w_map),
                  pl.BlockSpec((1, 1, hid, d), w_map)],
        out_specs=pl.BlockSpec((EXPERT_TILE, d), row_map),
        scratch_shapes=[pltpu.VMEM((d, hid), jnp.bfloat16), pltpu.VMEM((d, hid), jnp.bfloat16),
                        pltpu.VMEM((hid, d), jnp.bfloat16)],
    )
    return pl.pallas_call(
        _expert_kernel,
        name="moe_experts",
        grid_spec=grid_spec,
        out_shape=jax.ShapeDtypeStruct((n_rows, d), jnp.float32),
        compiler_params=pltpu.CompilerParams(dimension_semantics=("arbitrary",), vmem_limit_bytes=VMEM_LIMIT),
    )(tile_expert, n_used, x_disp, w1, w3, w2)


def _combine_kernel(dest_ref, dest_next_ref, x_ref, route_ref, g_ref, b_ref, yd_hbm, o_ref, ybuf, sem):
    i = pl.program_id(0)
    n = pl.num_programs(0)
    tm = x_ref.shape[0]
    rows = EXPERT_TOP_K * tm

    def start_tile(dref, slot):
        def body(r, c):
            for k in range(EXPERT_TOP_K):
                row = k * tm + r
                pltpu.make_async_copy(
                    yd_hbm.at[pl.ds(dref[0, 0, row], 1)], ybuf.at[slot, pl.ds(row, 1)], sem.at[slot]).start(priority=k)
            return c
        lax.fori_loop(0, tm, body, 0, unroll=8)

    slot = i % 2

    @pl.when(i == 0)
    def _():
        start_tile(dest_ref, 0)

    @pl.when(i + 1 < n)
    def _():
        start_tile(dest_next_ref, 1 - slot)

    pltpu.make_async_copy(yd_hbm.at[pl.ds(0, rows)], ybuf.at[slot], sem.at[slot]).wait()
    y1 = ybuf[slot, 0:tm, :]
    y2 = ybuf[slot, tm:rows, :]
    route = route_ref[...].T
    y = DN_ALPHA * x_ref[...] + (y1 * route[:, 0:1] + y2 * route[:, 1:2])
    o_ref[...] = _ln_rows(y, g_ref[...], b_ref[...])


def _moe_combine_ln(xt, y_disp, dest2, route, ln_g, ln_b):
    t, d = xt.shape
    n = t // ROUTE_TILE
    rows = EXPERT_TOP_K * ROUTE_TILE
    return pl.pallas_call(
        _combine_kernel,
        name="moe_combine_ln",
        grid=(n,),
        in_specs=[pl.BlockSpec((1, 1, rows), lambda i: (i, 0, 0), memory_space=pltpu.SMEM),
                  pl.BlockSpec((1, 1, rows), lambda i: (jnp.minimum(i + 1, n - 1), 0, 0), memory_space=pltpu.SMEM),
                  pl.BlockSpec((ROUTE_TILE, d), lambda i: (i, 0)),
                  pl.BlockSpec((ROUTE_ROWS, ROUTE_TILE), lambda i: (0, i)),
                  pl.BlockSpec((1, d), lambda i: (0, 0)),
                  pl.BlockSpec((1, d), lambda i: (0, 0)),
                  pl.BlockSpec(memory_space=pl.ANY)],
        out_specs=pl.BlockSpec((ROUTE_TILE, d), lambda i: (i, 0)),
        out_shape=jax.ShapeDtypeStruct((t, d), jnp.float32),
        scratch_shapes=[pltpu.VMEM((2, rows, d), jnp.float32), pltpu.SemaphoreType.DMA((2,))],
        compiler_params=pltpu.CompilerParams(dimension_semantics=("arbitrary",), vmem_limit_bytes=VMEM_LIMIT),
    )(dest2, dest2, xt, route, ln_g.reshape(1, d), ln_b.reshape(1, d), y_disp)


def _moe_sublayer(x, wg, bg, we, be, w1, w3, w2, layer, ln_g, ln_b):
    b, s, d = x.shape
    t = b * s
    xt = x.reshape(t, d)
    route, cnt = _moe_route(xt, wg, bg, we, be)
    counts = cnt[N_GROUPS:N_GROUPS + N_EXPERTS, 0].astype(jnp.int32)
    n_tiles = (t * EXPERT_TOP_K) // EXPERT_TILE + N_EXPERTS
    tiles_per = (counts + EXPERT_TILE - 1) // EXPERT_TILE
    tile_end = jnp.cumsum(tiles_per)
    pad_start = (tile_end - tiles_per) * EXPERT_TILE
    n_used = tile_end[-1:].astype(jnp.int32)
    tile_ids = jnp.minimum(jnp.arange(n_tiles), n_used[0] - 1)
    tile_expert = jnp.sum(tile_ids[:, None] >= tile_end[None, :], axis=1).astype(jnp.int32)
    experts = route[2:4].astype(jnp.int32)
    first_row = jnp.sum(jnp.where(experts[..., None] == jnp.arange(N_EXPERTS), pad_start, 0), axis=-1)
    dest = first_row + route[4:6].astype(jnp.int32)
    dest2 = jnp.swapaxes(dest.reshape(EXPERT_TOP_K, t // ROUTE_TILE, ROUTE_TILE), 0, 1)
    dest2 = dest2.reshape(t // ROUTE_TILE, 1, EXPERT_TOP_K * ROUTE_TILE)
    x_disp = _moe_dispatch(xt, dest2, n_tiles * EXPERT_TILE)
    y_disp = _moe_experts(x_disp, tile_expert, n_used, w1, w3, w2, layer)
    return _moe_combine_ln(xt, y_disp, dest2, route, ln_g, ln_b).reshape(b, s, d)


def kernel(x, mem, mem_wk, mem_wv, ev_w_in, ev_conv_w, ev_conv_b, ev_cnorm_g, ev_cnorm_b, ev_cmp_pe_k, ev_cmp_w1_k, ev_cmp_w2_k, ev_cmp_pe_v, ev_cmp_w1_v, ev_cmp_w2_v, ev_w_out, od_w_in, od_conv_w, od_w_out, ln_mix_g, ln_mix_b, xa_wq, xa_wo, ln_xa_g, ln_xa_b, moe_wg, moe_bg, moe_we, moe_be, moe_w1, moe_w3, moe_w2, ln_ffn_g, ln_ffn_b):
    b, s, d = x.shape
    mem_k, mem_v = _mem_kv(mem, mem_wk, mem_wv)
    for layer in range(DEPTH):
        i = layer // 2
        if layer % 2 == 0:
            a, kv_in, k_sel, k_win, q_t, v_sel_t, v_win_t, gate_t = _even_in_proj(x, ev_w_in[i])
            a = _conformer_conv(a, ev_conv_w[i], ev_conv_b[i], ev_cnorm_g[i], ev_cnorm_b[i])
            k_cmp, v_cmp_t = _compress_kv(kv_in, ev_cmp_pe_k[i], ev_cmp_w1_k[i], ev_cmp_w2_k[i], ev_cmp_pe_v[i], ev_cmp_w1_v[i], ev_cmp_w2_v[i])
            o = _nsa_attention(q_t, gate_t, k_cmp, v_cmp_t, k_sel, v_sel_t, k_win, v_win_t)
            x = _proj_residual_ln(a.reshape(b * s, -1), o.reshape(b * s, -1), x.reshape(b * s, d), ev_w_out[i], ln_mix_g[layer], ln_mix_b[layer]).reshape(b, s, d)
        else:
            x = _odd_mixer_sublayer(x, od_w_in[i], od_conv_w[i], od_w_out[i], ln_mix_g[layer], ln_mix_b[layer])
        x = _xattn_sublayer(x, mem_k, mem_v, xa_wq[layer], xa_wo[layer], ln_xa_g[layer], ln_xa_b[layer])
        x = _moe_sublayer(x, moe_wg[layer], moe_bg[layer], moe_we[layer], moe_be[layer], moe_w1, moe_w3, moe_w2, layer, ln_ffn_g[layer], ln_ffn_b[layer])
    return x
```

```python
import functools

import numpy as np
import jax
import jax.numpy as jnp
from jax import lax
from jax.experimental import pallas as pl
from jax.experimental.pallas import tpu as pltpu

D_MODEL = 1024
DEPTH = 2
CONV_CH = D_MODEL // 2
CONV_WIDTH = 31
NSA_HEADS = 8
NSA_KV_HEADS = 2
HEAD_DIM = (D_MODEL // 2) // NSA_HEADS
CMP_BLOCK = 32
CMP_STRIDE = 16
SEL_BLOCK = 64
SEL_TOP_N = 16
WINDOW = 512
Q_BLOCK = 256
FORCE_SCORE = 1e4
SHORT_CONV_WIDTH = 3
XA_HEADS = 4
XA_HEAD_DIM = D_MODEL // XA_HEADS
N_GROUPS = 4
EXPERTS_PER_GROUP = 8
N_EXPERTS = N_GROUPS * EXPERTS_PER_GROUP
EXPERT_TOP_K = 2
DN_ALPHA = (2 * DEPTH) ** 0.25
LN_EPS = 1e-5
NEG_INF = -1e30
KV_COLS = NSA_KV_HEADS * HEAD_DIM
QCOLS = NSA_HEADS * HEAD_DIM
GATE_ROWS = 3 * NSA_HEADS

LANES = 128
SUBLANES = 8
HEADS_PER_KV = NSA_HEADS // NSA_KV_HEADS
QL = Q_BLOCK * HEADS_PER_KV
SEL_TILE = 1024
WIN_SPAN = WINDOW + Q_BLOCK
BLOCKS_PER_TILE = SEL_TILE // SEL_BLOCK
SEQ_TILE = 1024
HALO = 32
ODD_HALO = 8
ROUTE_TILE = 512
EXPERT_TILE = 512
ROUTE_ROWS = 8
VMEM_LIMIT = 56 * 1024 * 1024

_NT = (((1,), (1,)), ((), ()))


def _ln_rows(y, g, b):
    mu = jnp.mean(y, axis=-1, keepdims=True)
    yc = y - mu
    var = jnp.mean(yc * yc, axis=-1, keepdims=True)
    return yc * lax.rsqrt(var + LN_EPS) * g + b


def _even_in_kernel(x_ref, wa_ref, wkv_ref, wk2_ref, wqT_ref, wvT_ref, wgT_ref,
                    a_ref, kv_ref, ks_ref, kw_ref, qT_ref, vsT_ref, vwT_ref, gT_ref):
    f32 = jnp.float32
    bf16 = jnp.bfloat16
    xb = x_ref[0].astype(bf16)
    av = jnp.dot(xb, wa_ref[...], preferred_element_type=f32)
    a_ref[0] = av[:, :CONV_CH] * jax.nn.sigmoid(av[:, CONV_CH:])
    kv_ref[0] = jnp.dot(xb, wkv_ref[...], preferred_element_type=f32)
    k2 = jnp.dot(xb, wk2_ref[...], preferred_element_type=f32)
    ks_ref[0] = k2[:, :KV_COLS].astype(bf16)
    kw_ref[0] = k2[:, KV_COLS:].astype(bf16)
    qT_ref[0] = lax.dot_general(wqT_ref[...], xb, _NT, preferred_element_type=f32).astype(bf16)
    vT = lax.dot_general(wvT_ref[...], xb, _NT, preferred_element_type=f32).astype(bf16)
    vsT_ref[0, 0] = vT[:KV_COLS]
    for j in range(SEQ_TILE // Q_BLOCK):
        vwT_ref[0, j] = vT[KV_COLS:, j * Q_BLOCK:(j + 1) * Q_BLOCK]
    gT_ref[0] = lax.dot_general(wgT_ref[...], xb, _NT, preferred_element_type=f32)


def _even_in_proj(x, w_in):
    b, s, d = x.shape
    bf16 = jnp.bfloat16
    c = np.cumsum((0, CONV_CH, CONV_CH, QCOLS, KV_COLS, KV_COLS, KV_COLS, KV_COLS, KV_COLS, KV_COLS, GATE_ROWS))
    col = lambda i, j: w_in[:, c[i]:c[j]]
    wa = col(0, 2).astype(bf16)
    wq_t = col(2, 3).T.astype(bf16)
    wkv = col(3, 5).astype(bf16)
    wk2 = jnp.concatenate([col(5, 6), col(7, 8)], axis=1).astype(bf16)
    wv_t = jnp.concatenate([col(6, 7), col(8, 9)], axis=1).T.astype(bf16)
    wg_t = col(9, 10).T.astype(bf16)
    ts = SEQ_TILE
    per_sel = SEL_TILE // ts
    full = lambda shape: pl.BlockSpec(shape, lambda i, j: (0,) * len(shape))
    return pl.pallas_call(
        _even_in_kernel,
        name="even_in_proj",
        grid=(b, s // ts),
        in_specs=[pl.BlockSpec((1, ts, d), lambda i, j: (i, j, 0)),
                  full(wa.shape), full(wkv.shape), full(wk2.shape), full(wq_t.shape), full(wv_t.shape), full(wg_t.shape)],
        out_specs=[pl.BlockSpec((1, ts, CONV_CH), lambda i, j: (i, j, 0)),
                   pl.BlockSpec((1, ts, 2 * KV_COLS), lambda i, j: (i, j, 0)),
                   pl.BlockSpec((1, ts, KV_COLS), lambda i, j: (i, j, 0)),
                   pl.BlockSpec((1, ts, KV_COLS), lambda i, j: (i, j, 0)),
                   pl.BlockSpec((1, QCOLS, ts), lambda i, j: (i, 0, j)),
                   pl.BlockSpec((1, 1, KV_COLS, ts), lambda i, j: (i, j // per_sel, 0, j % per_sel)),
                   pl.BlockSpec((1, ts // Q_BLOCK, KV_COLS, Q_BLOCK), lambda i, j: (i, j, 0, 0)),
                   pl.BlockSpec((1, GATE_ROWS, ts), lambda i, j: (i, 0, j))],
        out_shape=[jax.ShapeDtypeStruct((b, s, CONV_CH), jnp.float32),
                   jax.ShapeDtypeStruct((b, s, 2 * KV_COLS), jnp.float32),
                   jax.ShapeDtypeStruct((b, s, KV_COLS), bf16),
                   jax.ShapeDtypeStruct((b, s, KV_COLS), bf16),
                   jax.ShapeDtypeStruct((b, QCOLS, s), bf16),
                   jax.ShapeDtypeStruct((b, s // SEL_TILE, KV_COLS, SEL_TILE), bf16),
                   jax.ShapeDtypeStruct((b, s // Q_BLOCK, KV_COLS, Q_BLOCK), bf16),
                   jax.ShapeDtypeStruct((b, GATE_ROWS, s), jnp.float32)],
        compiler_params=pltpu.CompilerParams(dimension_semantics=("arbitrary", "arbitrary"), vmem_limit_bytes=VMEM_LIMIT),
    )(x, wa, wkv, wk2, wq_t, wv_t, wg_t)


def _conv_kernel(cur_ref, halo_ref, w_ref, cb_ref, g_ref, b_ref, o_ref, ext_ref, win_ref):
    j = pl.program_id(1)
    ts = cur_ref.shape[1]
    halo = halo_ref[0]
    ext_ref[0:HALO, :] = jnp.where(j > 0, halo, jnp.zeros_like(halo))
    ext_ref[HALO:HALO + ts, :] = cur_ref[0]
    first = HALO - (CONV_WIDTH - 1)
    acc = jnp.zeros((ts, CONV_CH), jnp.float32)
    for p in range(SUBLANES):
        n_a = len(range(p, CONV_WIDTH, SUBLANES))
        rows = ts + SUBLANES * (n_a - 1)
        win_ref[0:rows, :] = ext_ref[first + p:first + p + rows, :]
        for a in range(n_a):
            k = SUBLANES * a + p
            acc = acc + w_ref[k:k + 1, :] * win_ref[SUBLANES * a:SUBLANES * a + ts, :]
    y = _ln_rows(acc + cb_ref[...], g_ref[...], b_ref[...])
    o_ref[0] = (y * jax.nn.sigmoid(y)).astype(o_ref.dtype)


def _conformer_conv(a, conv_w, conv_b, cn_g, cn_b):
    b, s, c = a.shape
    ts = SEQ_TILE
    per = ts // HALO
    row = lambda v: v.reshape(1, c)
    return pl.pallas_call(
        _conv_kernel,
        name="conformer_conv",
        grid=(b, s // ts),
        in_specs=[pl.BlockSpec((1, ts, c), lambda i, j: (i, j, 0)),
                  pl.BlockSpec((1, HALO, c), lambda i, j: (i, jnp.maximum(j * per - 1, 0), 0)),
                  pl.BlockSpec((CONV_WIDTH, c), lambda i, j: (0, 0)),
                  pl.BlockSpec((1, c), lambda i, j: (0, 0)),
                  pl.BlockSpec((1, c), lambda i, j: (0, 0)),
                  pl.BlockSpec((1, c), lambda i, j: (0, 0))],
        out_specs=pl.BlockSpec((1, ts, c), lambda i, j: (i, j, 0)),
        out_shape=jax.ShapeDtypeStruct((b, s, c), jnp.bfloat16),
        scratch_shapes=[pltpu.VMEM((HALO + ts, c), jnp.float32), pltpu.VMEM((HALO + ts, c), jnp.float32)],
        compiler_params=pltpu.CompilerParams(dimension_semantics=("arbitrary", "arbitrary")),
    )(a, a, conv_w, row(conv_b), row(cn_g), row(cn_b))


def _compress_kernel(r_ref, pe_ref, w1_ref, w2_ref, w2T_ref, o_ref, oT_ref):
    f32 = jnp.float32
    bf16 = jnp.bfloat16
    nch = o_ref.shape[1]
    hw = w1_ref.shape[3]
    first = jnp.zeros((nch, hw), f32)
    second = jnp.zeros((nch, hw), f32)
    for l in range(CMP_STRIDE):
        tok = r_ref[pl.ds(l, nch, stride=CMP_STRIDE), :]
        first = first + jnp.dot((tok + pe_ref[0, l:l + 1, :]).astype(bf16), w1_ref[0, l], preferred_element_type=f32)
        second = second + jnp.dot((tok + pe_ref[0, CMP_STRIDE + l:CMP_STRIDE + l + 1, :]).astype(bf16),
                                  w1_ref[0, CMP_STRIDE + l], preferred_element_type=f32)
    second_next = jnp.concatenate([second[1:], jnp.zeros_like(second[0:1])], axis=0)
    hid = jax.nn.gelu(first + second_next).astype(bf16)
    o_ref[0] = jnp.dot(hid, w2_ref[0], preferred_element_type=f32).astype(bf16)
    oT_ref[0] = lax.dot_general(w2T_ref[0], hid, _NT, preferred_element_type=f32).astype(bf16)


def _compress_kv(kv_in, pe_k, w1_k, w2_k, pe_v, w1_v, w2_v):
    b, s, _ = kv_in.shape
    bf16 = jnp.bfloat16
    nch = s // CMP_STRIDE
    g = NSA_KV_HEADS
    hidden = w1_k.shape[1]

    def expand(pe, w1, w2):
        same = jnp.eye(g, dtype=bool)
        w1r = w1.reshape(CMP_BLOCK, HEAD_DIM, hidden)
        w1e = jnp.where(same[None, :, None, :, None], w1r[:, None, :, None, :], 0.0)
        w2e = jnp.where(same[:, None, :, None], w2[None, :, None, :], 0.0)
        return jnp.tile(pe, (1, g)), w1e.reshape(CMP_BLOCK, KV_COLS, g * hidden), w2e.reshape(g * hidden, KV_COLS)

    pk, w1k, w2k = expand(pe_k, w1_k, w2_k)
    pv, w1v, w2v = expand(pe_v, w1_v, w2_v)
    pe = jnp.stack([pk, pv])
    w1 = jnp.stack([w1k, w1v]).astype(bf16)
    w2 = jnp.stack([w2k, w2v]).astype(bf16)
    w2t = jnp.swapaxes(w2, 1, 2)
    o, o_t = pl.pallas_call(
        _compress_kernel,
        name="compress_kv",
        grid=(b, 2),
        in_specs=[pl.BlockSpec((s, KV_COLS), lambda i, j: (i, j)),
                  pl.BlockSpec((1,) + pe.shape[1:], lambda i, j: (j, 0, 0)),
                  pl.BlockSpec((1,) + w1.shape[1:], lambda i, j: (j, 0, 0, 0)),
                  pl.BlockSpec((1,) + w2.shape[1:], lambda i, j: (j, 0, 0)),
                  pl.BlockSpec((1,) + w2t.shape[1:], lambda i, j: (j, 0, 0))],
        out_specs=[pl.BlockSpec((1, nch, KV_COLS), lambda i, j: (2 * i + j, 0, 0)),
                   pl.BlockSpec((1, KV_COLS, nch), lambda i, j: (2 * i + j, 0, 0))],
        out_shape=[jax.ShapeDtypeStruct((b * 2, nch, KV_COLS), bf16), jax.ShapeDtypeStruct((b * 2, KV_COLS, nch), bf16)],
        compiler_params=pltpu.CompilerParams(dimension_semantics=("arbitrary", "arbitrary"), vmem_limit_bytes=VMEM_LIMIT),
    )(kv_in.reshape(b * s, 2 * KV_COLS), pe, w1, w2, w2t)
    return o.reshape(b, 2, nch, KV_COLS)[:, 0], o_t.reshape(b, 2, KV_COLS, nch)[:, 1]


def _proj_ln_kernel(a_ref, o_ref, x_ref, w_ref, g_ref, b_ref, y_ref):
    lhs = jnp.concatenate([a_ref[...], o_ref[...]], axis=1)
    mix = jnp.dot(lhs, w_ref[...], preferred_element_type=jnp.float32)
    y_ref[...] = _ln_rows(DN_ALPHA * x_ref[...] + mix, g_ref[...], b_ref[...])


def _proj_residual_ln(a, o, x, w, ln_g, ln_b):
    t, d = x.shape
    tm = SEQ_TILE
    return pl.pallas_call(
        _proj_ln_kernel,
        name="proj_residual_ln",
        grid=(t // tm,),
        in_specs=[pl.BlockSpec((tm, a.shape[1]), lambda i: (i, 0)),
                  pl.BlockSpec((tm, o.shape[1]), lambda i: (i, 0)),
                  pl.BlockSpec((tm, d), lambda i: (i, 0)),
                  pl.BlockSpec(w.shape, lambda i: (0, 0)),
                  pl.BlockSpec((1, d), lambda i: (0, 0)),
                  pl.BlockSpec((1, d), lambda i: (0, 0))],
        out_specs=pl.BlockSpec((tm, d), lambda i: (i, 0)),
        out_shape=jax.ShapeDtypeStruct((t, d), jnp.float32),
        compiler_params=pltpu.CompilerParams(dimension_semantics=("arbitrary",), vmem_limit_bytes=VMEM_LIMIT),
    )(a, o, x, w.astype(jnp.bfloat16), ln_g.reshape(1, d), ln_b.reshape(1, d))


def _mem_kv_kernel(m_ref, wk_ref, wv_ref, k_ref, v_ref):
    mb = m_ref[...].astype(jnp.bfloat16)
    k_ref[...] = jnp.dot(mb, wk_ref[...], preferred_element_type=jnp.float32).astype(jnp.bfloat16)
    v_ref[...] = jnp.dot(mb, wv_ref[...], preferred_element_type=jnp.float32).astype(jnp.bfloat16)


def _mem_kv(mem, wk, wv):
    b, m, d = mem.shape
    bf16 = jnp.bfloat16
    k, v = pl.pallas_call(
        _mem_kv_kernel,
        name="mem_kv",
        grid=(b,),
        in_specs=[pl.BlockSpec((m, d), lambda i: (i, 0)), pl.BlockSpec((d, d), lambda i: (0, 0)), pl.BlockSpec((d, d), lambda i: (0, 0))],
        out_specs=[pl.BlockSpec((m, d), lambda i: (i, 0)), pl.BlockSpec((m, d), lambda i: (i, 0))],
        out_shape=[jax.ShapeDtypeStruct((b * m, d), bf16), jax.ShapeDtypeStruct((b * m, d), bf16)],
        compiler_params=pltpu.CompilerParams(dimension_semantics=("arbitrary",), vmem_limit_bytes=VMEM_LIMIT),
    )(mem.reshape(b * m, d), wk.astype(bf16), wv.astype(bf16))
    return k.reshape(b, m, d), v.reshape(b, m, d)


def _xattn_kernel(x_ref, k_ref, v_ref, wq_ref, wo_ref, g_ref, b_ref, rw_ref, rb_ref, tri_ref,
                  y_ref, route_ref, cnt_ref, run_ref):
    f32 = jnp.float32
    bf16 = jnp.bfloat16
    x = x_ref[0]
    q = (jnp.dot(x.astype(bf16), wq_ref[...], preferred_element_type=f32).astype(bf16)
         * jnp.asarray(XA_HEAD_DIM ** -0.5, bf16))
    heads = []
    for h in range(XA_HEADS):
        cols = slice(h * XA_HEAD_DIM, (h + 1) * XA_HEAD_DIM)
        s = lax.dot_general(q[:, cols], k_ref[0, :, cols], _NT, preferred_element_type=f32)
        m = jnp.max(s, axis=1, keepdims=True)
        p = jnp.exp(s - m)
        p = p / jnp.sum(p, axis=1, keepdims=True)
        heads.append(jnp.dot(p.astype(bf16), v_ref[0, :, cols], preferred_element_type=f32).astype(bf16))
    att = jnp.concatenate(heads, axis=1)
    out = jnp.dot(att, wo_ref[...], preferred_element_type=f32)
    y = _ln_rows(DN_ALPHA * x + out, g_ref[...], b_ref[...])
    y_ref[0] = y
    first_step = (pl.program_id(0) == 0) & (pl.program_id(1) == 0)
    _route_tile(y.astype(bf16), first_step, rw_ref, rb_ref, tri_ref, route_ref, cnt_ref, run_ref)


def _xattn_router_sublayer(x, mem_k, mem_v, wq, wo, ln_g, ln_b, wg, bg, we, be):
    b, s, d = x.shape
    m = mem_k.shape[1]
    ts = SEQ_TILE
    per_seq = s // ts
    bf16 = jnp.bfloat16
    rw, rb, tri = _router_operands(wg, bg, we, be, ts)
    const = lambda shape: pl.BlockSpec(shape, lambda i, j: (0,) * len(shape))
    return pl.pallas_call(
        _xattn_kernel,
        name="xattn_router_sublayer",
        grid=(b, per_seq),
        in_specs=[pl.BlockSpec((1, ts, d), lambda i, j: (i, j, 0)),
                  pl.BlockSpec((1, m, d), lambda i, j: (i, 0, 0)),
                  pl.BlockSpec((1, m, d), lambda i, j: (i, 0, 0)),
                  const((d, d)), const((d, d)), const((1, d)), const((1, d)),
                  const(rw.shape), const(rb.shape), const(tri.shape)],
        out_specs=[pl.BlockSpec((1, ts, d), lambda i, j: (i, j, 0)),
                   pl.BlockSpec((ROUTE_ROWS, ts), lambda i, j: (0, i * per_seq + j)),
                   const((LANES, 1))],
        out_shape=[jax.ShapeDtypeStruct((b, s, d), jnp.float32),
                   jax.ShapeDtypeStruct((ROUTE_ROWS, b * s), jnp.float32),
                   jax.ShapeDtypeStruct((LANES, 1), jnp.float32)],
        scratch_shapes=[pltpu.VMEM((LANES, 1), jnp.float32)],
        compiler_params=pltpu.CompilerParams(dimension_semantics=("arbitrary", "arbitrary"), vmem_limit_bytes=VMEM_LIMIT),
    )(x, mem_k, mem_v, wq.astype(bf16), wo.astype(bf16), ln_g.reshape(1, d), ln_b.reshape(1, d), rw, rb, tri)


def _odd_kernel(x_ref, halo_ref, wb_ref, wc_ref, wh_ref, cw_ref, wo_ref, g_ref, b_ref, y_ref):
    f32 = jnp.float32
    bf16 = jnp.bfloat16
    j = pl.program_id(1)
    ts = x_ref.shape[1]
    x = x_ref[0]
    xe = jnp.concatenate([halo_ref[0], x], axis=0).astype(bf16)
    u = (jnp.dot(xe, wc_ref[...], preferred_element_type=f32) * jnp.dot(xe, wh_ref[...], preferred_element_type=f32))
    row = lax.broadcasted_iota(jnp.int32, (ODD_HALO + ts, 1), 0)
    u = jnp.where((row >= ODD_HALO) | (j > 0), u, 0.0)
    conv = jnp.zeros((ts, u.shape[1]), f32)
    for k in range(SHORT_CONV_WIDTH):
        off = ODD_HALO - (SHORT_CONV_WIDTH - 1) + k
        conv = conv + cw_ref[k:k + 1, :] * u[off:off + ts, :]
    gate_b = jnp.dot(xe[ODD_HALO:], wb_ref[...], preferred_element_type=f32)
    mix = jnp.dot((gate_b * conv).astype(bf16), wo_ref[...], preferred_element_type=f32)
    y_ref[0] = _ln_rows(DN_ALPHA * x + mix, g_ref[...], b_ref[...])


def _odd_mixer_sublayer(x, w_in, conv_w, w_out, ln_g, ln_b):
    b, s, d = x.shape
    ts = SEQ_TILE
    per = ts // ODD_HALO
    bf16 = jnp.bfloat16
    wb, wc, wh = (w_in[:, i * d:(i + 1) * d].astype(bf16) for i in range(3))
    full = lambda shape: pl.BlockSpec(shape, lambda i, j: (0,) * len(shape))
    return pl.pallas_call(
        _odd_kernel,
        name="odd_mixer_sublayer",
        grid=(b, s // ts),
        in_specs=[pl.BlockSpec((1, ts, d), lambda i, j: (i, j, 0)),
                  pl.BlockSpec((1, ODD_HALO, d), lambda i, j: (i, jnp.maximum(j * per - 1, 0), 0)),
                  full((d, d)), full((d, d)), full((d, d)), full(conv_w.shape), full((d, d)), full((1, d)), full((1, d))],
        out_specs=pl.BlockSpec((1, ts, d), lambda i, j: (i, j, 0)),
        out_shape=jax.ShapeDtypeStruct((b, s, d), jnp.float32),
        compiler_params=pltpu.CompilerParams(dimension_semantics=("arbitrary", "arbitrary"), vmem_limit_bytes=VMEM_LIMIT),
    )(x, x, wb, wc, wh, conv_w, w_out.astype(bf16), ln_g.reshape(1, d), ln_b.reshape(1, d))


def _cmp_to_sel_matrix(n_cmp, n_sel):
    c0 = np.arange(n_cmp) * CMP_STRIDE
    s0 = np.arange(n_sel) * SEL_BLOCK
    ov = np.minimum(c0[:, None] + CMP_BLOCK, s0[None, :] + SEL_BLOCK) - np.maximum(c0[:, None], s0[None, :])
    return (np.clip(ov, 0, None) / CMP_BLOCK).astype(np.float32)


def _nsa_kernel(qT_ref, gT_ref, kc_ref, vcT_ref, mselT_ref, ks_ref, vsT_ref, kw_ref, vwT_ref, o_ref,
                score_ref, sel_ref, *, n_sel):
    c = pl.program_id(1)
    q0 = c * Q_BLOCK
    f32 = jnp.float32
    bf16 = jnp.bfloat16
    n_cmp_pad = kc_ref.shape[1]

    lane_q = lax.broadcasted_iota(jnp.int32, (1, QL), 1) % Q_BLOCK
    t_row = q0 + lane_q
    t_row_q = q0 + lax.broadcasted_iota(jnp.int32, (1, Q_BLOCK), 1)
    cur_q = t_row_q // SEL_BLOCK

    top_n = min(SEL_TOP_N, n_sel)
    m_iota = lax.broadcasted_iota(jnp.int32, (n_sel, Q_BLOCK), 0)
    forced = (m_iota == 0) | (m_iota == cur_q) | (m_iota == cur_q - 1)
    valid = m_iota <= cur_q
    n_comp = jnp.minimum(q0 // SEL_BLOCK + Q_BLOCK // SEL_BLOCK, n_sel)

    q_g, o_c = [], []
    for g in range(NSA_KV_HEADS):
        pieces = []
        for hg in range(HEADS_PER_KV):
            h = g * HEADS_PER_KV + hg
            qh = qT_ref[0, h * HEAD_DIM:(h + 1) * HEAD_DIM, :] * jnp.asarray(HEAD_DIM ** -0.5, bf16)
            z = jnp.zeros_like(qh)
            pieces.append(jnp.concatenate([qh, z] if g == 0 else [z, qh], axis=0))
        qTp = jnp.concatenate(pieces, axis=1)
        q_g.append(qTp)
        rows = slice(g * HEAD_DIM, (g + 1) * HEAD_DIM)

        s_c = jnp.dot(kc_ref[0], qTp, preferred_element_type=f32)
        n_iota = lax.broadcasted_iota(jnp.int32, (n_cmp_pad, QL), 0)
        mask_c = (n_iota * CMP_STRIDE + (CMP_BLOCK - 1)) <= t_row
        s_c = jnp.where(mask_c, s_c, NEG_INF)
        m_c = jnp.max(s_c, axis=0, keepdims=True)
        p_c = jnp.where(mask_c, jnp.exp(s_c - m_c), 0.0)
        l_c = jnp.sum(p_c, axis=0, keepdims=True)
        p_c = p_c * jnp.where(l_c > 0.0, 1.0 / l_c, 0.0)
        p_cb = p_c.astype(bf16)
        o_c.append(jnp.dot(vcT_ref[0], p_cb, preferred_element_type=f32)[rows])
        imp4 = jnp.dot(mselT_ref[...], p_cb, preferred_element_type=f32)
        imp = imp4[:, 0:Q_BLOCK]
        for hg in range(1, HEADS_PER_KV):
            imp = imp + imp4[:, hg * Q_BLOCK:(hg + 1) * Q_BLOCK]

        score = jnp.where(valid, jnp.where(forced, FORCE_SCORE, imp), NEG_INF)
        score_ref[...] = score

        def rank_body(i8, rank, score=score):
            base = pl.multiple_of(i8 * SUBLANES, SUBLANES)
            rows = score_ref[pl.ds(base, SUBLANES), :]
            for u in range(SUBLANES):
                row = rows[u:u + 1, :]
                beats = (row > score) | ((row == score) & (base + u < m_iota))
                rank = rank + beats.astype(jnp.int32)
            return rank

        rank = lax.fori_loop(0, (n_comp + SUBLANES - 1) // SUBLANES, rank_body,
                             jnp.zeros((n_sel, Q_BLOCK), jnp.int32))
        sel_ref[g] = jnp.where((rank < top_n) & valid, 0.0, NEG_INF)

    def sel_tile(j, carry, causal):
        k_tile = ks_ref[0, j]
        v_tile = vsT_ref[0, j]
        blk0 = pl.multiple_of(j * BLOCKS_PER_TILE, BLOCKS_PER_TILE)
        out = []
        for g in range(NSA_KV_HEADS):
            m_i, l_i, acc = carry[g]
            s = jnp.dot(k_tile, q_g[g], preferred_element_type=f32)
            selrows = sel_ref[g, pl.ds(blk0, BLOCKS_PER_TILE), :]
            bias = jnp.concatenate(
                [jnp.broadcast_to(selrows[r:r + 1, :], (SEL_BLOCK, Q_BLOCK)) for r in range(BLOCKS_PER_TILE)], axis=0)
            if causal:
                key = j * SEL_TILE + lax.broadcasted_iota(jnp.int32, (SEL_TILE, Q_BLOCK), 0)
                bias = jnp.where(key <= t_row_q, bias, NEG_INF)
            s = s + jnp.concatenate([bias] * HEADS_PER_KV, axis=1)
            m_new = jnp.maximum(m_i, jnp.max(s, axis=0, keepdims=True))
            alpha = jnp.exp(m_i - m_new)
            p = jnp.exp(s - m_new)
            l_new = alpha * l_i + jnp.sum(p, axis=0, keepdims=True)
            pv = jnp.dot(v_tile, p.astype(bf16), preferred_element_type=f32)
            out.append((m_new, l_new, alpha * acc + pv))
        return tuple(out)

    n_full = q0 // SEL_TILE
    init = tuple((jnp.full((1, QL), NEG_INF, f32), jnp.zeros((1, QL), f32), jnp.zeros((KV_COLS, QL), f32))
                 for _ in range(NSA_KV_HEADS))
    carry = lax.fori_loop(0, n_full, functools.partial(sel_tile, causal=False), init)
    carry = sel_tile(n_full, carry, causal=True)

    start = pl.multiple_of(jnp.maximum(q0 - WINDOW, 0), Q_BLOCK)
    j0 = start // Q_BLOCK
    k_win = kw_ref[0, pl.ds(start, WIN_SPAN), :]
    key_w = start + lax.broadcasted_iota(jnp.int32, (WIN_SPAN, Q_BLOCK), 0)
    bias_w = jnp.where(key_w <= t_row_q, 0.0, NEG_INF)
    bias_w = jnp.where(key_w > t_row_q - WINDOW, bias_w, NEG_INF)
    bias_w = jnp.concatenate([bias_w] * HEADS_PER_KV, axis=1)
    for g in range(NSA_KV_HEADS):
        rows = slice(g * HEAD_DIM, (g + 1) * HEAD_DIM)
        m_s, l_s, acc_s = carry[g]
        o_s = acc_s[rows] * (1.0 / l_s)
        s_w = jnp.dot(k_win, q_g[g], preferred_element_type=f32) + bias_w
        m_w = jnp.max(s_w, axis=0, keepdims=True)
        p_w = jnp.exp(s_w - m_w)
        l_w = jnp.sum(p_w, axis=0, keepdims=True)
        p_wb = p_w.astype(bf16)
        acc_w = jnp.zeros((KV_COLS, QL), f32)
        for i in range(WIN_SPAN // Q_BLOCK):
            acc_w = acc_w + jnp.dot(vwT_ref[0, j0 + i], p_wb[i * Q_BLOCK:(i + 1) * Q_BLOCK, :], preferred_element_type=f32)
        o_w = acc_w[rows] * (1.0 / l_w)

        for pair in range(HEADS_PER_KV // 2):
            halves = []
            for hg in (2 * pair, 2 * pair + 1):
                h = g * HEADS_PER_KV + hg
                lanes = slice(hg * Q_BLOCK, (hg + 1) * Q_BLOCK)
                gate = jax.nn.sigmoid(gT_ref[0, 3 * h:3 * h + 3, :])
                halves.append(gate[0:1] * o_c[g][:, lanes] + gate[1:2] * o_s[:, lanes] + gate[2:3] * o_w[:, lanes])
            both = jnp.concatenate(halves, axis=0)
            col0 = (g * HEADS_PER_KV + 2 * pair) * HEAD_DIM
            o_ref[0, :, col0:col0 + 2 * HEAD_DIM] = both.T.astype(o_ref.dtype)


def _nsa_attention(qT, gT, kc, vcT, ks, vsT, kw, vwT):
    b, hd, s = qT.shape
    n_sel = s // SEL_BLOCK
    nc = kc.shape[1]
    n_cmp = (s - CMP_BLOCK) // CMP_STRIDE + 1
    mselT = jnp.asarray(np.pad(_cmp_to_sel_matrix(n_cmp, n_sel).T, ((0, 0), (0, nc - n_cmp))), jnp.bfloat16)
    ks4 = ks.reshape(b, s // SEL_TILE, SEL_TILE, KV_COLS)
    return pl.pallas_call(
        functools.partial(_nsa_kernel, n_sel=n_sel),
        name="nsa_attention",
        grid=(b, s // Q_BLOCK),
        in_specs=[
            pl.BlockSpec((1, hd, Q_BLOCK), lambda i, c: (i, 0, c)),
            pl.BlockSpec((1, 3 * NSA_HEADS, Q_BLOCK), lambda i, c: (i, 0, c)),
            pl.BlockSpec((1, nc, KV_COLS), lambda i, c: (i, 0, 0)),
            pl.BlockSpec((1, KV_COLS, nc), lambda i, c: (i, 0, 0)),
            pl.BlockSpec((n_sel, nc), lambda i, c: (0, 0)),
            pl.BlockSpec((1, s // SEL_TILE, SEL_TILE, KV_COLS), lambda i, c: (i, 0, 0, 0)),
            pl.BlockSpec((1, s // SEL_TILE, KV_COLS, SEL_TILE), lambda i, c: (i, 0, 0, 0)),
            pl.BlockSpec((1, s, KV_COLS), lambda i, c: (i, 0, 0)),
            pl.BlockSpec((1, s // Q_BLOCK, KV_COLS, Q_BLOCK), lambda i, c: (i, 0, 0, 0)),
        ],
        out_specs=pl.BlockSpec((1, Q_BLOCK, hd), lambda i, c: (i, c, 0)),
        out_shape=jax.ShapeDtypeStruct((b, s, hd), jnp.bfloat16),
        scratch_shapes=[pltpu.VMEM((n_sel, Q_BLOCK), jnp.float32),
                        pltpu.VMEM((NSA_KV_HEADS, n_sel, Q_BLOCK), jnp.float32)],
        compiler_params=pltpu.CompilerParams(dimension_semantics=("arbitrary", "arbitrary")),
    )(qT, gT, kc, vcT, mselT, ks4, vsT, kw, vwT)


def _route_tile(xb, first_step, wT_ref, b_ref, tri_ref, route_ref, cnt_ref, run_ref):
    f32 = jnp.float32

    @pl.when(first_step)
    def _():
        run_ref[...] = jnp.zeros_like(run_ref)

    tm = xb.shape[0]
    logits = lax.dot_general(wT_ref[...], xb, _NT, preferred_element_type=f32) + b_ref[...]
    sub = lax.broadcasted_iota(jnp.int32, (LANES, tm), 0)
    is_g = sub < N_GROUPS
    gl = jnp.where(is_g, logits, NEG_INF)
    g_max = jnp.max(gl, axis=0, keepdims=True)
    g_star = jnp.min(jnp.where(gl == g_max, sub, LANES), axis=0, keepdims=True)
    p_group = 1.0 / jnp.sum(jnp.where(is_g, jnp.exp(gl - g_max), 0.0), axis=0, keepdims=True)
    lo = N_GROUPS + g_star * EXPERTS_PER_GROUP
    in_grp = (sub >= lo) & (sub < lo + EXPERTS_PER_GROUP)
    el = jnp.where(in_grp, logits, NEG_INF)
    v1 = jnp.max(el, axis=0, keepdims=True)
    i1 = jnp.min(jnp.where(el == v1, sub, LANES), axis=0, keepdims=True)
    el2 = jnp.where(sub == i1, NEG_INF, el)
    v2 = jnp.max(el2, axis=0, keepdims=True)
    i2 = jnp.min(jnp.where(el2 == v2, sub, LANES), axis=0, keepdims=True)
    e21 = jnp.exp(v2 - v1)
    gate1 = p_group * (1.0 / (1.0 + e21))
    gate2 = p_group * (e21 / (1.0 + e21))
    oh1 = (sub == i1).astype(f32)
    oh2 = (sub == i2).astype(f32)
    both = oh1 + oh2
    before = jnp.dot(both.astype(jnp.bfloat16), tri_ref[...], preferred_element_type=f32) + run_ref[...]
    rank1 = jnp.sum(oh1 * before, axis=0, keepdims=True)
    rank2 = jnp.sum(oh2 * before, axis=0, keepdims=True)
    run_ref[...] = run_ref[...] + jnp.sum(both, axis=1, keepdims=True)
    cnt_ref[...] = run_ref[...]
    zero = jnp.zeros_like(gate1)
    route_ref[...] = jnp.concatenate(
        [gate1, gate2, (i1 - N_GROUPS).astype(f32), (i2 - N_GROUPS).astype(f32), rank1, rank2, zero, zero], axis=0)


def _router_operands(wg, bg, we, be, tile):
    d = wg.shape[0]
    pad = LANES - N_GROUPS - N_EXPERTS
    w_t = jnp.concatenate([wg, we.reshape(d, N_EXPERTS), jnp.zeros((d, pad), wg.dtype)], axis=1).T
    bias = jnp.concatenate([bg, be.reshape(N_EXPERTS), jnp.zeros((pad,), bg.dtype)])[:, None]
    tri = jnp.asarray(np.triu(np.ones((tile, tile), np.float32), 1), jnp.bfloat16)
    return w_t.astype(jnp.bfloat16), bias, tri


def _dispatch_kernel(dest_ref, x_ref, zero_hbm, xd_hbm, sem):
    del zero_hbm
    tm = x_ref.shape[0]

    def body(r, c):
        for k in range(EXPERT_TOP_K):
            pltpu.make_async_copy(
                x_ref.at[pl.ds(r, 1)], xd_hbm.at[pl.ds(dest_ref[0, 0, k * tm + r], 1)], sem).start(priority=k)
        return c

    lax.fori_loop(0, tm, body, 0, unroll=8)
    for k in range(EXPERT_TOP_K):
        pltpu.make_async_copy(x_ref, xd_hbm.at[pl.ds(0, tm)], sem).wait()


def _moe_dispatch(xt, dest2, n_rows):
    t, d = xt.shape
    return pl.pallas_call(
        _dispatch_kernel,
        name="moe_dispatch",
        grid=(t // ROUTE_TILE,),
        in_specs=[pl.BlockSpec((1, 1, EXPERT_TOP_K * ROUTE_TILE), lambda i: (i, 0, 0), memory_space=pltpu.SMEM),
                  pl.BlockSpec((ROUTE_TILE, d), lambda i: (i, 0)),
                  pl.BlockSpec(memory_space=pl.ANY)],
        out_specs=pl.BlockSpec(memory_space=pl.ANY),
        out_shape=jax.ShapeDtypeStruct((n_rows, d), xt.dtype),
        scratch_shapes=[pltpu.SemaphoreType.DMA(())],
        input_output_aliases={2: 0},
        compiler_params=pltpu.CompilerParams(dimension_semantics=("arbitrary",), has_side_effects=True),
    )(dest2, xt, jnp.zeros((n_rows, d), xt.dtype))


def _expert_kernel(te_ref, nu_ref, xd_ref, w1_ref, w3_ref, w2_ref, y_ref, w1b, w3b, w2b):
    i = pl.program_id(0)
    used = i < nu_ref[0]
    new_expert = (i == 0) | (te_ref[i] != te_ref[jnp.maximum(i - 1, 0)])

    @pl.when(used & new_expert)
    def _():
        w1b[...] = w1_ref[0, 0].astype(jnp.bfloat16)
        w3b[...] = w3_ref[0, 0].astype(jnp.bfloat16)
        w2b[...] = w2_ref[0, 0].astype(jnp.bfloat16)

    @pl.when(used)
    def _():
        xb = xd_ref[...].astype(jnp.bfloat16)
        h1 = jnp.dot(xb, w1b[...], preferred_element_type=jnp.float32)
        h3 = jnp.dot(xb, w3b[...], preferred_element_type=jnp.float32)
        a = (h1 * jax.nn.sigmoid(h1) * h3).astype(jnp.bfloat16)
        y_ref[...] = jnp.dot(a, w2b[...], preferred_element_type=jnp.float32)

    @pl.when(jnp.logical_not(used))
    def _():
        y_ref[...] = jnp.zeros_like(y_ref)


def _moe_experts(x_disp, tile_expert, n_used, w1, w3, w2, layer):
    n_rows, d = x_disp.shape
    n_tiles = n_rows // EXPERT_TILE
    hid = w1.shape[3]

    def row_map(i, te, nu):
        return (i, 0)

    def w_map(i, te, nu):
        return (layer, te[i], 0, 0)

    grid_spec = pltpu.PrefetchScalarGridSpec(
        num_scalar_prefetch=2,
        grid=(n_tiles,),
        in_specs=[pl.BlockSpec((EXPERT_TILE, d), row_map),
                  pl.BlockSpec((1, 1, d, hid), w_map),
                  pl.BlockSpec((1, 1, d, hid), w_map),
                  pl.BlockSpec((1, 1, hid, d), w_map)],
        out_specs=pl.BlockSpec((EXPERT_TILE, d), row_map),
        scratch_shapes=[pltpu.VMEM((d, hid), jnp.bfloat16), pltpu.VMEM((d, hid), jnp.bfloat16),
                        pltpu.VMEM((hid, d), jnp.bfloat16)],
    )
    return pl.pallas_call(
        _expert_kernel,
        name="moe_experts",
        grid_spec=grid_spec,
        out_shape=jax.ShapeDtypeStruct((n_rows, d), jnp.float32),
        compiler_params=pltpu.CompilerParams(dimension_semantics=("arbitrary",), vmem_limit_bytes=VMEM_LIMIT),
    )(tile_expert, n_used, x_disp, w1, w3, w2)


def _combine_kernel(dest_ref, dest_next_ref, x_ref, route_ref, g_ref, b_ref, yd_hbm, o_ref, ybuf, sem):
    i = pl.program_id(0)
    n = pl.num_programs(0)
    tm = x_ref.shape[0]
    rows = EXPERT_TOP_K * tm

    def start_tile(dref, slot):
        def body(r, c):
            for k in range(EXPERT_TOP_K):
                row = k * tm + r
                pltpu.make_async_copy(
                    yd_hbm.at[pl.ds(dref[0, 0, row], 1)], ybuf.at[slot, pl.ds(row, 1)], sem.at[slot]).start(priority=k)
            return c
        lax.fori_loop(0, tm, body, 0, unroll=8)

    slot = i % 2

    @pl.when(i == 0)
    def _():
        start_tile(dest_ref, 0)

    @pl.when(i + 1 < n)
    def _():
        start_tile(dest_next_ref, 1 - slot)

    pltpu.make_async_copy(yd_hbm.at[pl.ds(0, rows)], ybuf.at[slot], sem.at[slot]).wait()
    y1 = ybuf[slot, 0:tm, :]
    y2 = ybuf[slot, tm:rows, :]
    route = route_ref[...].T
    y = DN_ALPHA * x_ref[...] + (y1 * route[:, 0:1] + y2 * route[:, 1:2])
    o_ref[...] = _ln_rows(y, g_ref[...], b_ref[...])


def _moe_combine_ln(xt, y_disp, dest2, route, ln_g, ln_b):
    t, d = xt.shape
    n = t // ROUTE_TILE
    rows = EXPERT_TOP_K * ROUTE_TILE
    return pl.pallas_call(
        _combine_kernel,
        name="moe_combine_ln",
        grid=(n,),
        in_specs=[pl.BlockSpec((1, 1, rows), lambda i: (i, 0, 0), memory_space=pltpu.SMEM),
                  pl.BlockSpec((1, 1, rows), lambda i: (jnp.minimum(i + 1, n - 1), 0, 0), memory_space=pltpu.SMEM),
                  pl.BlockSpec((ROUTE_TILE, d), lambda i: (i, 0)),
                  pl.BlockSpec((ROUTE_ROWS, ROUTE_TILE), lambda i: (0, i)),
                  pl.BlockSpec((1, d), lambda i: (0, 0)),
                  pl.BlockSpec((1, d), lambda i: (0, 0)),
                  pl.BlockSpec(memory_space=pl.ANY)],
        out_specs=pl.BlockSpec((ROUTE_TILE, d), lambda i: (i, 0)),
        out_shape=jax.ShapeDtypeStruct((t, d), jnp.float32),
        scratch_shapes=[pltpu.VMEM((2, rows, d), jnp.float32), pltpu.SemaphoreType.DMA((2,))],
        compiler_params=pltpu.CompilerParams(dimension_semantics=("arbitrary",), vmem_limit_bytes=VMEM_LIMIT),
    )(dest2, dest2, xt, route, ln_g.reshape(1, d), ln_b.reshape(1, d), y_disp)


def _moe_sublayer(x, route, cnt, w1, w3, w2, layer, ln_g, ln_b):
    b, s, d = x.shape
    t = b * s
    xt = x.reshape(t, d)
    counts = cnt[N_GROUPS:N_GROUPS + N_EXPERTS, 0].astype(jnp.int32)
    n_tiles = (t * EXPERT_TOP_K) // EXPERT_TILE + N_EXPERTS
    tiles_per = (counts + EXPERT_TILE - 1) // EXPERT_TILE
    tile_end = jnp.cumsum(tiles_per)
    pad_start = (tile_end - tiles_per) * EXPERT_TILE
    n_used = tile_end[-1:].astype(jnp.int32)
    tile_ids = jnp.minimum(jnp.arange(n_tiles), n_used[0] - 1)
    tile_expert = jnp.sum(tile_ids[:, None] >= tile_end[None, :], axis=1).astype(jnp.int32)
    experts = route[2:4].astype(jnp.int32)
    first_row = jnp.sum(jnp.where(experts[..., None] == jnp.arange(N_EXPERTS), pad_start, 0), axis=-1)
    dest = first_row + route[4:6].astype(jnp.int32)
    dest2 = jnp.swapaxes(dest.reshape(EXPERT_TOP_K, t // ROUTE_TILE, ROUTE_TILE), 0, 1)
    dest2 = dest2.reshape(t // ROUTE_TILE, 1, EXPERT_TOP_K * ROUTE_TILE)
    x_disp = _moe_dispatch(xt, dest2, n_tiles * EXPERT_TILE)
    y_disp = _moe_experts(x_disp, tile_expert, n_used, w1, w3, w2, layer)
    return _moe_combine_ln(xt, y_disp, dest2, route, ln_g, ln_b).reshape(b, s, d)


def kernel(x, mem, mem_wk, mem_wv, ev_w_in, ev_conv_w, ev_conv_b, ev_cnorm_g, ev_cnorm_b, ev_cmp_pe_k, ev_cmp_w1_k, ev_cmp_w2_k, ev_cmp_pe_v, ev_cmp_w1_v, ev_cmp_w2_v, ev_w_out, od_w_in, od_conv_w, od_w_out, ln_mix_g, ln_mix_b, xa_wq, xa_wo, ln_xa_g, ln_xa_b, moe_wg, moe_bg, moe_we, moe_be, moe_w1, moe_w3, moe_w2, ln_ffn_g, ln_ffn_b):
    b, s, d = x.shape
    mem_k, mem_v = _mem_kv(mem, mem_wk, mem_wv)
    for layer in range(DEPTH):
        i = layer // 2
        if layer % 2 == 0:
            a, kv_in, k_sel, k_win, q_t, v_sel_t, v_win_t, gate_t = _even_in_proj(x, ev_w_in[i])
            a = _conformer_conv(a, ev_conv_w[i], ev_conv_b[i], ev_cnorm_g[i], ev_cnorm_b[i])
            k_cmp, v_cmp_t = _compress_kv(kv_in, ev_cmp_pe_k[i], ev_cmp_w1_k[i], ev_cmp_w2_k[i], ev_cmp_pe_v[i], ev_cmp_w1_v[i], ev_cmp_w2_v[i])
            o = _nsa_attention(q_t, gate_t, k_cmp, v_cmp_t, k_sel, v_sel_t, k_win, v_win_t)
            x = _proj_residual_ln(a.reshape(b * s, -1), o.reshape(b * s, -1), x.reshape(b * s, d), ev_w_out[i], ln_mix_g[layer], ln_mix_b[layer]).reshape(b, s, d)
        else:
            x = _odd_mixer_sublayer(x, od_w_in[i], od_conv_w[i], od_w_out[i], ln_mix_g[layer], ln_mix_b[layer])
        x, route, cnt = _xattn_router_sublayer(x, mem_k, mem_v, xa_wq[layer], xa_wo[layer], ln_xa_g[layer], ln_xa_b[layer],
                                               moe_wg[layer], moe_bg[layer], moe_we[layer], moe_be[layer])
        x = _moe_sublayer(x, route, cnt, moe_w1, moe_w3, moe_w2, layer, ln_ffn_g[layer], ln_ffn_b[layer])
    return x
```

```python
import functools

import numpy as np
import jax
import jax.numpy as jnp
from jax import lax
from jax.experimental import pallas as pl
from jax.experimental.pallas import tpu as pltpu

D_MODEL = 1024
DEPTH = 2
CONV_CH = D_MODEL // 2
CONV_WIDTH = 31
NSA_HEADS = 8
NSA_KV_HEADS = 2
HEAD_DIM = (D_MODEL // 2) // NSA_HEADS
CMP_BLOCK = 32
CMP_STRIDE = 16
SEL_BLOCK = 64
SEL_TOP_N = 16
WINDOW = 512
Q_BLOCK = 256
FORCE_SCORE = 1e4
SHORT_CONV_WIDTH = 3
XA_HEADS = 4
XA_HEAD_DIM = D_MODEL // XA_HEADS
N_GROUPS = 4
EXPERTS_PER_GROUP = 8
N_EXPERTS = N_GROUPS * EXPERTS_PER_GROUP
EXPERT_TOP_K = 2
DN_ALPHA = (2 * DEPTH) ** 0.25
LN_EPS = 1e-5
NEG_INF = -1e30
KV_COLS = NSA_KV_HEADS * HEAD_DIM
QCOLS = NSA_HEADS * HEAD_DIM
GATE_ROWS = 3 * NSA_HEADS

LANES = 128
SUBLANES = 8
HEADS_PER_KV = NSA_HEADS // NSA_KV_HEADS
QL = Q_BLOCK * HEADS_PER_KV
SEL_TILE = 1024
WIN_SPAN = WINDOW + Q_BLOCK
BLOCKS_PER_TILE = SEL_TILE // SEL_BLOCK
SEQ_TILE = 1024
HALO = 32
ODD_HALO = 8
ROUTE_TILE = 512
EXPERT_TILE = 512
ROUTE_ROWS = 8
VMEM_LIMIT = 56 * 1024 * 1024

_NT = (((1,), (1,)), ((), ()))


def _ln_rows(y, g, b):
    mu = jnp.mean(y, axis=-1, keepdims=True)
    yc = y - mu
    var = jnp.mean(yc * yc, axis=-1, keepdims=True)
    return yc * lax.rsqrt(var + LN_EPS) * g + b


def _even_in_kernel(x_ref, wa_ref, wkv_ref, wk2_ref, wqT_ref, wvT_ref, wgT_ref,
                    a_ref, kv_ref, ks_ref, kw_ref, qT_ref, vsT_ref, vwT_ref, gT_ref):
    f32 = jnp.float32
    bf16 = jnp.bfloat16
    xb = x_ref[0].astype(bf16)
    av = jnp.dot(xb, wa_ref[...], preferred_element_type=f32)
    a_ref[0] = av[:, :CONV_CH] * jax.nn.sigmoid(av[:, CONV_CH:])
    kv_ref[0] = jnp.dot(xb, wkv_ref[...], preferred_element_type=f32)
    k2 = jnp.dot(xb, wk2_ref[...], preferred_element_type=f32)
    ks_ref[0] = k2[:, :KV_COLS].astype(bf16)
    kw_ref[0] = k2[:, KV_COLS:].astype(bf16)
    qT_ref[0] = lax.dot_general(wqT_ref[...], xb, _NT, preferred_element_type=f32).astype(bf16)
    vT = lax.dot_general(wvT_ref[...], xb, _NT, preferred_element_type=f32).astype(bf16)
    vsT_ref[0, 0] = vT[:KV_COLS]
    for j in range(SEQ_TILE // Q_BLOCK):
        vwT_ref[0, j] = vT[KV_COLS:, j * Q_BLOCK:(j + 1) * Q_BLOCK]
    gT_ref[0] = lax.dot_general(wgT_ref[...], xb, _NT, preferred_element_type=f32)


def _even_in_proj(x, w_in):
    b, s, d = x.shape
    bf16 = jnp.bfloat16
    c = np.cumsum((0, CONV_CH, CONV_CH, QCOLS, KV_COLS, KV_COLS, KV_COLS, KV_COLS, KV_COLS, KV_COLS, GATE_ROWS))
    col = lambda i, j: w_in[:, c[i]:c[j]]
    wa = col(0, 2).astype(bf16)
    wq_t = col(2, 3).T.astype(bf16)
    wkv = col(3, 5).astype(bf16)
    wk2 = jnp.concatenate([col(5, 6), col(7, 8)], axis=1).astype(bf16)
    wv_t = jnp.concatenate([col(6, 7), col(8, 9)], axis=1).T.astype(bf16)
    wg_t = col(9, 10).T.astype(bf16)
    ts = SEQ_TILE
    per_sel = SEL_TILE // ts
    full = lambda shape: pl.BlockSpec(shape, lambda i, j: (0,) * len(shape))
    return pl.pallas_call(
        _even_in_kernel,
        name="even_in_proj",
        grid=(b, s // ts),
        in_specs=[pl.BlockSpec((1, ts, d), lambda i, j: (i, j, 0)),
                  full(wa.shape), full(wkv.shape), full(wk2.shape), full(wq_t.shape), full(wv_t.shape), full(wg_t.shape)],
        out_specs=[pl.BlockSpec((1, ts, CONV_CH), lambda i, j: (i, j, 0)),
                   pl.BlockSpec((1, ts, 2 * KV_COLS), lambda i, j: (i, j, 0)),
                   pl.BlockSpec((1, ts, KV_COLS), lambda i, j: (i, j, 0)),
                   pl.BlockSpec((1, ts, KV_COLS), lambda i, j: (i, j, 0)),
                   pl.BlockSpec((1, QCOLS, ts), lambda i, j: (i, 0, j)),
                   pl.BlockSpec((1, 1, KV_COLS, ts), lambda i, j: (i, j // per_sel, 0, j % per_sel)),
                   pl.BlockSpec((1, ts // Q_BLOCK, KV_COLS, Q_BLOCK), lambda i, j: (i, j, 0, 0)),
                   pl.BlockSpec((1, GATE_ROWS, ts), lambda i, j: (i, 0, j))],
        out_shape=[jax.ShapeDtypeStruct((b, s, CONV_CH), jnp.float32),
                   jax.ShapeDtypeStruct((b, s, 2 * KV_COLS), jnp.float32),
                   jax.ShapeDtypeStruct((b, s, KV_COLS), bf16),
                   jax.ShapeDtypeStruct((b, s, KV_COLS), bf16),
                   jax.ShapeDtypeStruct((b, QCOLS, s), bf16),
                   jax.ShapeDtypeStruct((b, s // SEL_TILE, KV_COLS, SEL_TILE), bf16),
                   jax.ShapeDtypeStruct((b, s // Q_BLOCK, KV_COLS, Q_BLOCK), bf16),
                   jax.ShapeDtypeStruct((b, GATE_ROWS, s), jnp.float32)],
        compiler_params=pltpu.CompilerParams(dimension_semantics=("arbitrary", "arbitrary"), vmem_limit_bytes=VMEM_LIMIT),
    )(x, wa, wkv, wk2, wq_t, wv_t, wg_t)


def _conv_kernel(cur_ref, halo_ref, w_ref, cb_ref, g_ref, b_ref, o_ref, ext_ref, win_ref):
    j = pl.program_id(1)
    ts = cur_ref.shape[1]
    halo = halo_ref[0]
    ext_ref[0:HALO, :] = jnp.where(j > 0, halo, jnp.zeros_like(halo))
    ext_ref[HALO:HALO + ts, :] = cur_ref[0]
    first = HALO - (CONV_WIDTH - 1)
    acc = jnp.zeros((ts, CONV_CH), jnp.float32)
    for p in range(SUBLANES):
        n_a = len(range(p, CONV_WIDTH, SUBLANES))
        rows = ts + SUBLANES * (n_a - 1)
        win_ref[0:rows, :] = ext_ref[first + p:first + p + rows, :]
        for a in range(n_a):
            k = SUBLANES * a + p
            acc = acc + w_ref[k:k + 1, :] * win_ref[SUBLANES * a:SUBLANES * a + ts, :]
    y = _ln_rows(acc + cb_ref[...], g_ref[...], b_ref[...])
    o_ref[0] = (y * jax.nn.sigmoid(y)).astype(o_ref.dtype)


def _conformer_conv(a, conv_w, conv_b, cn_g, cn_b):
    b, s, c = a.shape
    ts = SEQ_TILE
    per = ts // HALO
    row = lambda v: v.reshape(1, c)
    return pl.pallas_call(
        _conv_kernel,
        name="conformer_conv",
        grid=(b, s // ts),
        in_specs=[pl.BlockSpec((1, ts, c), lambda i, j: (i, j, 0)),
                  pl.BlockSpec((1, HALO, c), lambda i, j: (i, jnp.maximum(j * per - 1, 0), 0)),
                  pl.BlockSpec((CONV_WIDTH, c), lambda i, j: (0, 0)),
                  pl.BlockSpec((1, c), lambda i, j: (0, 0)),
                  pl.BlockSpec((1, c), lambda i, j: (0, 0)),
                  pl.BlockSpec((1, c), lambda i, j: (0, 0))],
        out_specs=pl.BlockSpec((1, ts, c), lambda i, j: (i, j, 0)),
        out_shape=jax.ShapeDtypeStruct((b, s, c), jnp.bfloat16),
        scratch_shapes=[pltpu.VMEM((HALO + ts, c), jnp.float32), pltpu.VMEM((HALO + ts, c), jnp.float32)],
        compiler_params=pltpu.CompilerParams(dimension_semantics=("arbitrary", "arbitrary")),
    )(a, a, conv_w, row(conv_b), row(cn_g), row(cn_b))


def _compress_kernel(r_ref, pe_ref, w1_ref, w2_ref, w2T_ref, o_ref, oT_ref):
    f32 = jnp.float32
    bf16 = jnp.bfloat16
    nch = o_ref.shape[1]
    hw = w1_ref.shape[3]
    first = jnp.zeros((nch, hw), f32)
    second = jnp.zeros((nch, hw), f32)
    for l in range(CMP_STRIDE):
        tok = r_ref[pl.ds(l, nch, stride=CMP_STRIDE), :]
        first = first + jnp.dot((tok + pe_ref[0, l:l + 1, :]).astype(bf16), w1_ref[0, l], preferred_element_type=f32)
        second = second + jnp.dot((tok + pe_ref[0, CMP_STRIDE + l:CMP_STRIDE + l + 1, :]).astype(bf16),
                                  w1_ref[0, CMP_STRIDE + l], preferred_element_type=f32)
    second_next = jnp.concatenate([second[1:], jnp.zeros_like(second[0:1])], axis=0)
    hid = jax.nn.gelu(first + second_next).astype(bf16)
    o_ref[0] = jnp.dot(hid, w2_ref[0], preferred_element_type=f32).astype(bf16)
    oT_ref[0] = lax.dot_general(w2T_ref[0], hid, _NT, preferred_element_type=f32).astype(bf16)


def _compress_kv(kv_in, pe_k, w1_k, w2_k, pe_v, w1_v, w2_v):
    b, s, _ = kv_in.shape
    bf16 = jnp.bfloat16
    nch = s // CMP_STRIDE
    g = NSA_KV_HEADS
    hidden = w1_k.shape[1]

    def expand(pe, w1, w2):
        same = jnp.eye(g, dtype=bool)
        w1r = w1.reshape(CMP_BLOCK, HEAD_DIM, hidden)
        w1e = jnp.where(same[None, :, None, :, None], w1r[:, None, :, None, :], 0.0)
        w2e = jnp.where(same[:, None, :, None], w2[None, :, None, :], 0.0)
        return jnp.tile(pe, (1, g)), w1e.reshape(CMP_BLOCK, KV_COLS, g * hidden), w2e.reshape(g * hidden, KV_COLS)

    pk, w1k, w2k = expand(pe_k, w1_k, w2_k)
    pv, w1v, w2v = expand(pe_v, w1_v, w2_v)
    pe = jnp.stack([pk, pv])
    w1 = jnp.stack([w1k, w1v]).astype(bf16)
    w2 = jnp.stack([w2k, w2v]).astype(bf16)
    w2t = jnp.swapaxes(w2, 1, 2)
    o, o_t = pl.pallas_call(
        _compress_kernel,
        name="compress_kv",
        grid=(b, 2),
        in_specs=[pl.BlockSpec((s, KV_COLS), lambda i, j: (i, j)),
                  pl.BlockSpec((1,) + pe.shape[1:], lambda i, j: (j, 0, 0)),
                  pl.BlockSpec((1,) + w1.shape[1:], lambda i, j: (j, 0, 0, 0)),
                  pl.BlockSpec((1,) + w2.shape[1:], lambda i, j: (j, 0, 0)),
                  pl.BlockSpec((1,) + w2t.shape[1:], lambda i, j: (j, 0, 0))],
        out_specs=[pl.BlockSpec((1, nch, KV_COLS), lambda i, j: (2 * i + j, 0, 0)),
                   pl.BlockSpec((1, KV_COLS, nch), lambda i, j: (2 * i + j, 0, 0))],
        out_shape=[jax.ShapeDtypeStruct((b * 2, nch, KV_COLS), bf16), jax.ShapeDtypeStruct((b * 2, KV_COLS, nch), bf16)],
        compiler_params=pltpu.CompilerParams(dimension_semantics=("arbitrary", "arbitrary"), vmem_limit_bytes=VMEM_LIMIT),
    )(kv_in.reshape(b * s, 2 * KV_COLS), pe, w1, w2, w2t)
    return o.reshape(b, 2, nch, KV_COLS)[:, 0], o_t.reshape(b, 2, KV_COLS, nch)[:, 1]


def _proj_ln_kernel(a_ref, o_ref, x_ref, w_ref, g_ref, b_ref, y_ref):
    lhs = jnp.concatenate([a_ref[...], o_ref[...]], axis=1)
    mix = jnp.dot(lhs, w_ref[...], preferred_element_type=jnp.float32)
    y_ref[...] = _ln_rows(DN_ALPHA * x_ref[...] + mix, g_ref[...], b_ref[...])


def _proj_residual_ln(a, o, x, w, ln_g, ln_b):
    t, d = x.shape
    tm = SEQ_TILE
    return pl.pallas_call(
        _proj_ln_kernel,
        name="proj_residual_ln",
        grid=(t // tm,),
        in_specs=[pl.BlockSpec((tm, a.shape[1]), lambda i: (i, 0)),
                  pl.BlockSpec((tm, o.shape[1]), lambda i: (i, 0)),
                  pl.BlockSpec((tm, d), lambda i: (i, 0)),
                  pl.BlockSpec(w.shape, lambda i: (0, 0)),
                  pl.BlockSpec((1, d), lambda i: (0, 0)),
                  pl.BlockSpec((1, d), lambda i: (0, 0))],
        out_specs=pl.BlockSpec((tm, d), lambda i: (i, 0)),
        out_shape=jax.ShapeDtypeStruct((t, d), jnp.float32),
        compiler_params=pltpu.CompilerParams(dimension_semantics=("arbitrary",), vmem_limit_bytes=VMEM_LIMIT),
    )(a, o, x, w.astype(jnp.bfloat16), ln_g.reshape(1, d), ln_b.reshape(1, d))


def _mem_kv_kernel(m_ref, wk_ref, wv_ref, k_ref, v_ref):
    mb = m_ref[...].astype(jnp.bfloat16)
    k_ref[...] = jnp.dot(mb, wk_ref[...], preferred_element_type=jnp.float32).astype(jnp.bfloat16)
    v_ref[...] = jnp.dot(mb, wv_ref[...], preferred_element_type=jnp.float32).astype(jnp.bfloat16)


def _mem_kv(mem, wk, wv):
    b, m, d = mem.shape
    bf16 = jnp.bfloat16
    k, v = pl.pallas_call(
        _mem_kv_kernel,
        name="mem_kv",
        grid=(b,),
        in_specs=[pl.BlockSpec((m, d), lambda i: (i, 0)), pl.BlockSpec((d, d), lambda i: (0, 0)), pl.BlockSpec((d, d), lambda i: (0, 0))],
        out_specs=[pl.BlockSpec((m, d), lambda i: (i, 0)), pl.BlockSpec((m, d), lambda i: (i, 0))],
        out_shape=[jax.ShapeDtypeStruct((b * m, d), bf16), jax.ShapeDtypeStruct((b * m, d), bf16)],
        compiler_params=pltpu.CompilerParams(dimension_semantics=("arbitrary",), vmem_limit_bytes=VMEM_LIMIT),
    )(mem.reshape(b * m, d), wk.astype(bf16), wv.astype(bf16))
    return k.reshape(b, m, d), v.reshape(b, m, d)


def _xattn_kernel(x_ref, k_ref, v_ref, wq_ref, wo_ref, g_ref, b_ref, rw_ref, rb_ref, tri_ref,
                  y_ref, route_ref, cnt_ref, run_ref):
    f32 = jnp.float32
    bf16 = jnp.bfloat16
    x = x_ref[0]
    q = (jnp.dot(x.astype(bf16), wq_ref[...], preferred_element_type=f32).astype(bf16)
         * jnp.asarray(XA_HEAD_DIM ** -0.5, bf16))
    heads = []
    for h in range(XA_HEADS):
        cols = slice(h * XA_HEAD_DIM, (h + 1) * XA_HEAD_DIM)
        s = lax.dot_general(q[:, cols], k_ref[0, :, cols], _NT, preferred_element_type=f32)
        m = jnp.max(s, axis=1, keepdims=True)
        p = jnp.exp(s - m)
        p = p / jnp.sum(p, axis=1, keepdims=True)
        heads.append(jnp.dot(p.astype(bf16), v_ref[0, :, cols], preferred_element_type=f32).astype(bf16))
    att = jnp.concatenate(heads, axis=1)
    out = jnp.dot(att, wo_ref[...], preferred_element_type=f32)
    y = _ln_rows(DN_ALPHA * x + out, g_ref[...], b_ref[...])
    y_ref[0] = y
    first_step = (pl.program_id(0) == 0) & (pl.program_id(1) == 0)
    _route_tile(y.astype(bf16), first_step, rw_ref, rb_ref, tri_ref, route_ref, cnt_ref, run_ref)


def _xattn_router_sublayer(x, mem_k, mem_v, wq, wo, ln_g, ln_b, wg, bg, we, be):
    b, s, d = x.shape
    m = mem_k.shape[1]
    ts = SEQ_TILE
    per_seq = s // ts
    bf16 = jnp.bfloat16
    rw, rb, tri = _router_operands(wg, bg, we, be, ts)
    const = lambda shape: pl.BlockSpec(shape, lambda i, j: (0,) * len(shape))
    return pl.pallas_call(
        _xattn_kernel,
        name="xattn_router_sublayer",
        grid=(b, per_seq),
        in_specs=[pl.BlockSpec((1, ts, d), lambda i, j: (i, j, 0)),
                  pl.BlockSpec((1, m, d), lambda i, j: (i, 0, 0)),
                  pl.BlockSpec((1, m, d), lambda i, j: (i, 0, 0)),
                  const((d, d)), const((d, d)), const((1, d)), const((1, d)),
                  const(rw.shape), const(rb.shape), const(tri.shape)],
        out_specs=[pl.BlockSpec((1, ts, d), lambda i, j: (i, j, 0)),
                   pl.BlockSpec((ROUTE_ROWS, ts), lambda i, j: (0, i * per_seq + j)),
                   const((LANES, 1))],
        out_shape=[jax.ShapeDtypeStruct((b, s, d), jnp.float32),
                   jax.ShapeDtypeStruct((ROUTE_ROWS, b * s), jnp.float32),
                   jax.ShapeDtypeStruct((LANES, 1), jnp.float32)],
        scratch_shapes=[pltpu.VMEM((LANES, 1), jnp.float32)],
        compiler_params=pltpu.CompilerParams(dimension_semantics=("arbitrary", "arbitrary"), vmem_limit_bytes=VMEM_LIMIT),
    )(x, mem_k, mem_v, wq.astype(bf16), wo.astype(bf16), ln_g.reshape(1, d), ln_b.reshape(1, d), rw, rb, tri)


def _odd_kernel(x_ref, halo_ref, wb_ref, wc_ref, wh_ref, cw_ref, wo_ref, g_ref, b_ref, y_ref):
    f32 = jnp.float32
    bf16 = jnp.bfloat16
    j = pl.program_id(1)
    ts = x_ref.shape[1]
    x = x_ref[0]
    xe = jnp.concatenate([halo_ref[0], x], axis=0).astype(bf16)
    u = (jnp.dot(xe, wc_ref[...], preferred_element_type=f32) * jnp.dot(xe, wh_ref[...], preferred_element_type=f32))
    row = lax.broadcasted_iota(jnp.int32, (ODD_HALO + ts, 1), 0)
    u = jnp.where((row >= ODD_HALO) | (j > 0), u, 0.0)
    conv = jnp.zeros((ts, u.shape[1]), f32)
    for k in range(SHORT_CONV_WIDTH):
        off = ODD_HALO - (SHORT_CONV_WIDTH - 1) + k
        conv = conv + cw_ref[k:k + 1, :] * u[off:off + ts, :]
    gate_b = jnp.dot(xe[ODD_HALO:], wb_ref[...], preferred_element_type=f32)
    mix = jnp.dot((gate_b * conv).astype(bf16), wo_ref[...], preferred_element_type=f32)
    y_ref[0] = _ln_rows(DN_ALPHA * x + mix, g_ref[...], b_ref[...])


def _odd_mixer_sublayer(x, w_in, conv_w, w_out, ln_g, ln_b):
    b, s, d = x.shape
    ts = SEQ_TILE
    per = ts // ODD_HALO
    bf16 = jnp.bfloat16
    wb, wc, wh = (w_in[:, i * d:(i + 1) * d].astype(bf16) for i in range(3))
    full = lambda shape: pl.BlockSpec(shape, lambda i, j: (0,) * len(shape))
    return pl.pallas_call(
        _odd_kernel,
        name="odd_mixer_sublayer",
        grid=(b, s // ts),
        in_specs=[pl.BlockSpec((1, ts, d), lambda i, j: (i, j, 0)),
                  pl.BlockSpec((1, ODD_HALO, d), lambda i, j: (i, jnp.maximum(j * per - 1, 0), 0)),
                  full((d, d)), full((d, d)), full((d, d)), full(conv_w.shape), full((d, d)), full((1, d)), full((1, d))],
        out_specs=pl.BlockSpec((1, ts, d), lambda i, j: (i, j, 0)),
        out_shape=jax.ShapeDtypeStruct((b, s, d), jnp.float32),
        compiler_params=pltpu.CompilerParams(dimension_semantics=("arbitrary", "arbitrary"), vmem_limit_bytes=VMEM_LIMIT),
    )(x, x, wb, wc, wh, conv_w, w_out.astype(bf16), ln_g.reshape(1, d), ln_b.reshape(1, d))


def _cmp_to_sel_matrix(n_cmp, n_sel):
    c0 = np.arange(n_cmp) * CMP_STRIDE
    s0 = np.arange(n_sel) * SEL_BLOCK
    ov = np.minimum(c0[:, None] + CMP_BLOCK, s0[None, :] + SEL_BLOCK) - np.maximum(c0[:, None], s0[None, :])
    return (np.clip(ov, 0, None) / CMP_BLOCK).astype(np.float32)


def _nsa_kernel(qT_ref, gT_ref, kc_ref, vcT_ref, mselT_ref, ks_ref, vsT_ref, kw_ref, vwT_ref, o_ref,
                score_ref, sel_ref, *, n_sel):
    c = pl.program_id(1)
    q0 = c * Q_BLOCK
    f32 = jnp.float32
    bf16 = jnp.bfloat16
    n_cmp_pad = kc_ref.shape[1]

    lane_q = lax.broadcasted_iota(jnp.int32, (1, QL), 1) % Q_BLOCK
    t_row = q0 + lane_q
    t_row_q = q0 + lax.broadcasted_iota(jnp.int32, (1, Q_BLOCK), 1)
    cur_q = t_row_q // SEL_BLOCK

    top_n = min(SEL_TOP_N, n_sel)
    m_iota = lax.broadcasted_iota(jnp.int32, (n_sel, Q_BLOCK), 0)
    forced = (m_iota == 0) | (m_iota == cur_q) | (m_iota == cur_q - 1)
    valid = m_iota <= cur_q
    n_comp = jnp.minimum(q0 // SEL_BLOCK + Q_BLOCK // SEL_BLOCK, n_sel)

    q_g, o_c = [], []
    for g in range(NSA_KV_HEADS):
        pieces = []
        for hg in range(HEADS_PER_KV):
            h = g * HEADS_PER_KV + hg
            qh = qT_ref[0, h * HEAD_DIM:(h + 1) * HEAD_DIM, :] * jnp.asarray(HEAD_DIM ** -0.5, bf16)
            z = jnp.zeros_like(qh)
            pieces.append(jnp.concatenate([qh, z] if g == 0 else [z, qh], axis=0))
        qTp = jnp.concatenate(pieces, axis=1)
        q_g.append(qTp)
        rows = slice(g * HEAD_DIM, (g + 1) * HEAD_DIM)

        s_c = jnp.dot(kc_ref[0], qTp, preferred_element_type=f32)
        n_iota = lax.broadcasted_iota(jnp.int32, (n_cmp_pad, QL), 0)
        mask_c = (n_iota * CMP_STRIDE + (CMP_BLOCK - 1)) <= t_row
        s_c = jnp.where(mask_c, s_c, NEG_INF)
        m_c = jnp.max(s_c, axis=0, keepdims=True)
        p_c = jnp.where(mask_c, jnp.exp(s_c - m_c), 0.0)
        l_c = jnp.sum(p_c, axis=0, keepdims=True)
        p_c = p_c * jnp.where(l_c > 0.0, 1.0 / l_c, 0.0)
        p_cb = p_c.astype(bf16)
        o_c.append(jnp.dot(vcT_ref[0], p_cb, preferred_element_type=f32)[rows])
        imp4 = jnp.dot(mselT_ref[...], p_cb, preferred_element_type=f32)
        imp = imp4[:, 0:Q_BLOCK]
        for hg in range(1, HEADS_PER_KV):
            imp = imp + imp4[:, hg * Q_BLOCK:(hg + 1) * Q_BLOCK]

        score = jnp.where(valid, jnp.where(forced, FORCE_SCORE, imp), NEG_INF)
        score_ref[...] = score

        def rank_body(i8, rank, score=score):
            base = pl.multiple_of(i8 * SUBLANES, SUBLANES)
            rows = score_ref[pl.ds(base, SUBLANES), :]
            for u in range(SUBLANES):
                row = rows[u:u + 1, :]
                beats = (row > score) | ((row == score) & (base + u < m_iota))
                rank = rank + beats.astype(jnp.int32)
            return rank

        rank = lax.fori_loop(0, (n_comp + SUBLANES - 1) // SUBLANES, rank_body,
                             jnp.zeros((n_sel, Q_BLOCK), jnp.int32))
        sel_ref[g] = jnp.where((rank < top_n) & valid, 0.0, NEG_INF)

    def sel_tile(j, carry, causal):
        k_tile = ks_ref[0, j]
        v_tile = vsT_ref[0, j]
        blk0 = pl.multiple_of(j * BLOCKS_PER_TILE, BLOCKS_PER_TILE)
        out = []
        for g in range(NSA_KV_HEADS):
            m_i, l_i, acc = carry[g]
            s = jnp.dot(k_tile, q_g[g], preferred_element_type=f32)
            selrows = sel_ref[g, pl.ds(blk0, BLOCKS_PER_TILE), :]
            bias = jnp.concatenate(
                [jnp.broadcast_to(selrows[r:r + 1, :], (SEL_BLOCK, Q_BLOCK)) for r in range(BLOCKS_PER_TILE)], axis=0)
            if causal:
                key = j * SEL_TILE + lax.broadcasted_iota(jnp.int32, (SEL_TILE, Q_BLOCK), 0)
                bias = jnp.where(key <= t_row_q, bias, NEG_INF)
            s = s + jnp.concatenate([bias] * HEADS_PER_KV, axis=1)
            m_new = jnp.maximum(m_i, jnp.max(s, axis=0, keepdims=True))
            alpha = jnp.exp(m_i - m_new)
            p = jnp.exp(s - m_new)
            l_new = alpha * l_i + jnp.sum(p, axis=0, keepdims=True)
            pv = jnp.dot(v_tile, p.astype(bf16), preferred_element_type=f32)
            out.append((m_new, l_new, alpha * acc + pv))
        return tuple(out)

    n_full = q0 // SEL_TILE
    init = tuple((jnp.full((1, QL), NEG_INF, f32), jnp.zeros((1, QL), f32), jnp.zeros((KV_COLS, QL), f32))
                 for _ in range(NSA_KV_HEADS))
    carry = lax.fori_loop(0, n_full, functools.partial(sel_tile, causal=False), init)
    carry = sel_tile(n_full, carry, causal=True)

    start = pl.multiple_of(jnp.maximum(q0 - WINDOW, 0), Q_BLOCK)
    j0 = start // Q_BLOCK
    k_win = kw_ref[0, pl.ds(start, WIN_SPAN), :]
    key_w = start + lax.broadcasted_iota(jnp.int32, (WIN_SPAN, Q_BLOCK), 0)
    bias_w = jnp.where(key_w <= t_row_q, 0.0, NEG_INF)
    bias_w = jnp.where(key_w > t_row_q - WINDOW, bias_w, NEG_INF)
    bias_w = jnp.concatenate([bias_w] * HEADS_PER_KV, axis=1)
    for g in range(NSA_KV_HEADS):
        rows = slice(g * HEAD_DIM, (g + 1) * HEAD_DIM)
        m_s, l_s, acc_s = carry[g]
        o_s = acc_s[rows] * (1.0 / l_s)
        s_w = jnp.dot(k_win, q_g[g], preferred_element_type=f32) + bias_w
        m_w = jnp.max(s_w, axis=0, keepdims=True)
        p_w = jnp.exp(s_w - m_w)
        l_w = jnp.sum(p_w, axis=0, keepdims=True)
        p_wb = p_w.astype(bf16)
        acc_w = jnp.zeros((KV_COLS, QL), f32)
        for i in range(WIN_SPAN // Q_BLOCK):
            acc_w = acc_w + jnp.dot(vwT_ref[0, j0 + i], p_wb[i * Q_BLOCK:(i + 1) * Q_BLOCK, :], preferred_element_type=f32)
        o_w = acc_w[rows] * (1.0 / l_w)

        for pair in range(HEADS_PER_KV // 2):
            halves = []
            for hg in (2 * pair, 2 * pair + 1):
                h = g * HEADS_PER_KV + hg
                lanes = slice(hg * Q_BLOCK, (hg + 1) * Q_BLOCK)
                gate = jax.nn.sigmoid(gT_ref[0, 3 * h:3 * h + 3, :])
                halves.append(gate[0:1] * o_c[g][:, lanes] + gate[1:2] * o_s[:, lanes] + gate[2:3] * o_w[:, lanes])
            both = jnp.concatenate(halves, axis=0)
            col0 = (g * HEADS_PER_KV + 2 * pair) * HEAD_DIM
            o_ref[0, :, col0:col0 + 2 * HEAD_DIM] = both.T.astype(o_ref.dtype)


def _nsa_attention(qT, gT, kc, vcT, ks, vsT, kw, vwT):
    b, hd, s = qT.shape
    n_sel = s // SEL_BLOCK
    nc = kc.shape[1]
    n_cmp = (s - CMP_BLOCK) // CMP_STRIDE + 1
    mselT = jnp.asarray(np.pad(_cmp_to_sel_matrix(n_cmp, n_sel).T, ((0, 0), (0, nc - n_cmp))), jnp.bfloat16)
    ks4 = ks.reshape(b, s // SEL_TILE, SEL_TILE, KV_COLS)
    return pl.pallas_call(
        functools.partial(_nsa_kernel, n_sel=n_sel),
        name="nsa_attention",
        grid=(b, s // Q_BLOCK),
        in_specs=[
            pl.BlockSpec((1, hd, Q_BLOCK), lambda i, c: (i, 0, c)),
            pl.BlockSpec((1, 3 * NSA_HEADS, Q_BLOCK), lambda i, c: (i, 0, c)),
            pl.BlockSpec((1, nc, KV_COLS), lambda i, c: (i, 0, 0)),
            pl.BlockSpec((1, KV_COLS, nc), lambda i, c: (i, 0, 0)),
            pl.BlockSpec((n_sel, nc), lambda i, c: (0, 0)),
            pl.BlockSpec((1, s // SEL_TILE, SEL_TILE, KV_COLS), lambda i, c: (i, 0, 0, 0)),
            pl.BlockSpec((1, s // SEL_TILE, KV_COLS, SEL_TILE), lambda i, c: (i, 0, 0, 0)),
            pl.BlockSpec((1, s, KV_COLS), lambda i, c: (i, 0, 0)),
            pl.BlockSpec((1, s // Q_BLOCK, KV_COLS, Q_BLOCK), lambda i, c: (i, 0, 0, 0)),
        ],
        out_specs=pl.BlockSpec((1, Q_BLOCK, hd), lambda i, c: (i, c, 0)),
        out_shape=jax.ShapeDtypeStruct((b, s, hd), jnp.bfloat16),
        scratch_shapes=[pltpu.VMEM((n_sel, Q_BLOCK), jnp.float32),
                        pltpu.VMEM((NSA_KV_HEADS, n_sel, Q_BLOCK), jnp.float32)],
        compiler_params=pltpu.CompilerParams(dimension_semantics=("arbitrary", "arbitrary")),
    )(qT, gT, kc, vcT, mselT, ks4, vsT, kw, vwT)


def _route_tile(xb, first_step, wT_ref, b_ref, tri_ref, route_ref, cnt_ref, run_ref):
    f32 = jnp.float32

    @pl.when(first_step)
    def _():
        run_ref[...] = jnp.zeros_like(run_ref)

    tm = xb.shape[0]
    logits = lax.dot_general(wT_ref[...], xb, _NT, preferred_element_type=f32) + b_ref[...]
    sub = lax.broadcasted_iota(jnp.int32, (LANES, tm), 0)
    is_g = sub < N_GROUPS
    gl = jnp.where(is_g, logits, NEG_INF)
    g_max = jnp.max(gl, axis=0, keepdims=True)
    g_star = jnp.min(jnp.where(gl == g_max, sub, LANES), axis=0, keepdims=True)
    p_group = 1.0 / jnp.sum(jnp.where(is_g, jnp.exp(gl - g_max), 0.0), axis=0, keepdims=True)
    lo = N_GROUPS + g_star * EXPERTS_PER_GROUP
    in_grp = (sub >= lo) & (sub < lo + EXPERTS_PER_GROUP)
    el = jnp.where(in_grp, logits, NEG_INF)
    v1 = jnp.max(el, axis=0, keepdims=True)
    i1 = jnp.min(jnp.where(el == v1, sub, LANES), axis=0, keepdims=True)
    el2 = jnp.where(sub == i1, NEG_INF, el)
    v2 = jnp.max(el2, axis=0, keepdims=True)
    i2 = jnp.min(jnp.where(el2 == v2, sub, LANES), axis=0, keepdims=True)
    e21 = jnp.exp(v2 - v1)
    gate1 = p_group * (1.0 / (1.0 + e21))
    gate2 = p_group * (e21 / (1.0 + e21))
    oh1 = (sub == i1).astype(f32)
    oh2 = (sub == i2).astype(f32)
    both = oh1 + oh2
    before = jnp.dot(both.astype(jnp.bfloat16), tri_ref[...], preferred_element_type=f32) + run_ref[...]
    rank1 = jnp.sum(oh1 * before, axis=0, keepdims=True)
    rank2 = jnp.sum(oh2 * before, axis=0, keepdims=True)
    run_ref[...] = run_ref[...] + jnp.sum(both, axis=1, keepdims=True)
    cnt_ref[...] = run_ref[...]
    zero = jnp.zeros_like(gate1)
    route_ref[...] = jnp.concatenate(
        [gate1, gate2, (i1 - N_GROUPS).astype(f32), (i2 - N_GROUPS).astype(f32), rank1, rank2, zero, zero], axis=0)


def _router_operands(wg, bg, we, be, tile):
    d = wg.shape[0]
    pad = LANES - N_GROUPS - N_EXPERTS
    w_t = jnp.concatenate([wg, we.reshape(d, N_EXPERTS), jnp.zeros((d, pad), wg.dtype)], axis=1).T
    bias = jnp.concatenate([bg, be.reshape(N_EXPERTS), jnp.zeros((pad,), bg.dtype)])[:, None]
    tri = jnp.asarray(np.triu(np.ones((tile, tile), np.float32), 1), jnp.bfloat16)
    return w_t.astype(jnp.bfloat16), bias, tri


def _dispatch_kernel(zt_ref, dest_ref, x_ref, xd_hbm, zbuf, sem, zsem):
    tm = x_ref.shape[0]

    @pl.when(pl.program_id(0) == 0)
    def _():
        zbuf[...] = jnp.zeros_like(zbuf)

        def zero_copy(k):
            start = pl.multiple_of(jnp.maximum(zt_ref[k], 0) * EXPERT_TILE, EXPERT_TILE)
            return pltpu.make_async_copy(zbuf, xd_hbm.at[pl.ds(start, EXPERT_TILE)], zsem)

        def start_body(k, c):
            @pl.when(zt_ref[k] >= 0)
            def _():
                zero_copy(k).start()
            return c

        def wait_body(k, c):
            @pl.when(zt_ref[k] >= 0)
            def _():
                zero_copy(k).wait()
            return c

        lax.fori_loop(0, zt_ref.shape[0], start_body, 0)
        lax.fori_loop(0, zt_ref.shape[0], wait_body, 0)

    def body(r, c):
        for k in range(EXPERT_TOP_K):
            pltpu.make_async_copy(
                x_ref.at[pl.ds(r, 1)], xd_hbm.at[pl.ds(dest_ref[0, 0, k * tm + r], 1)], sem).start(priority=k)
        return c

    lax.fori_loop(0, tm, body, 0, unroll=8)
    for k in range(EXPERT_TOP_K):
        pltpu.make_async_copy(x_ref, xd_hbm.at[pl.ds(0, tm)], sem).wait()


def _moe_dispatch(xt, dest2, zero_tiles, n_rows):
    t, d = xt.shape
    grid_spec = pltpu.PrefetchScalarGridSpec(
        num_scalar_prefetch=1,
        grid=(t // ROUTE_TILE,),
        in_specs=[pl.BlockSpec((1, 1, EXPERT_TOP_K * ROUTE_TILE), lambda i, zt: (i, 0, 0), memory_space=pltpu.SMEM),
                  pl.BlockSpec((ROUTE_TILE, d), lambda i, zt: (i, 0))],
        out_specs=pl.BlockSpec(memory_space=pl.ANY),
        scratch_shapes=[pltpu.VMEM((EXPERT_TILE, d), xt.dtype), pltpu.SemaphoreType.DMA(()), pltpu.SemaphoreType.DMA(())],
    )
    return pl.pallas_call(
        _dispatch_kernel,
        name="moe_dispatch",
        grid_spec=grid_spec,
        out_shape=jax.ShapeDtypeStruct((n_rows, d), xt.dtype),
        compiler_params=pltpu.CompilerParams(dimension_semantics=("arbitrary",), has_side_effects=True),
    )(zero_tiles, dest2, xt)


def _expert_kernel(te_ref, nu_ref, xd_ref, w1_ref, w3_ref, w2_ref, y_ref, w1b, w3b, w2b):
    i = pl.program_id(0)
    used = i < nu_ref[0]
    new_expert = (i == 0) | (te_ref[i] != te_ref[jnp.maximum(i - 1, 0)])

    @pl.when(used & new_expert)
    def _():
        w1b[...] = w1_ref[0, 0].astype(jnp.bfloat16)
        w3b[...] = w3_ref[0, 0].astype(jnp.bfloat16)
        w2b[...] = w2_ref[0, 0].astype(jnp.bfloat16)

    @pl.when(used)
    def _():
        xb = xd_ref[...].astype(jnp.bfloat16)
        h1 = jnp.dot(xb, w1b[...], preferred_element_type=jnp.float32)
        h3 = jnp.dot(xb, w3b[...], preferred_element_type=jnp.float32)
        a = (h1 * jax.nn.sigmoid(h1) * h3).astype(jnp.bfloat16)
        y_ref[...] = jnp.dot(a, w2b[...], preferred_element_type=jnp.float32)

    @pl.when(jnp.logical_not(used))
    def _():
        y_ref[...] = jnp.zeros_like(y_ref)


def _moe_experts(x_disp, tile_expert, n_used, w1, w3, w2, layer):
    n_rows, d = x_disp.shape
    n_tiles = n_rows // EXPERT_TILE
    hid = w1.shape[3]

    def row_map(i, te, nu):
        return (i, 0)

    def w_map(i, te, nu):
        return (layer, te[i], 0, 0)

    grid_spec = pltpu.PrefetchScalarGridSpec(
        num_scalar_prefetch=2,
        grid=(n_tiles,),
        in_specs=[pl.BlockSpec((EXPERT_TILE, d), row_map),
                  pl.BlockSpec((1, 1, d, hid), w_map),
                  pl.BlockSpec((1, 1, d, hid), w_map),
                  pl.BlockSpec((1, 1, hid, d), w_map)],
        out_specs=pl.BlockSpec((EXPERT_TILE, d), row_map),
        scratch_shapes=[pltpu.VMEM((d, hid), jnp.bfloat16), pltpu.VMEM((d, hid), jnp.bfloat16),
                        pltpu.VMEM((hid, d), jnp.bfloat16)],
    )
    return pl.pallas_call(
        _expert_kernel,
        name="moe_experts",
        grid_spec=grid_spec,
        out_shape=jax.ShapeDtypeStruct((n_rows, d), jnp.float32),
        compiler_params=pltpu.CompilerParams(dimension_semantics=("arbitrary",), vmem_limit_bytes=VMEM_LIMIT),
    )(tile_expert, n_used, x_disp, w1, w3, w2)


def _combine_kernel(dest_ref, dest_next_ref, x_ref, route_ref, g_ref, b_ref, yd_hbm, o_ref, ybuf, sem):
    i = pl.program_id(0)
    n = pl.num_programs(0)
    tm = x_ref.shape[0]
    rows = EXPERT_TOP_K * tm

    def start_tile(dref, slot):
        def body(r, c):
            for k in range(EXPERT_TOP_K):
                row = k * tm + r
                pltpu.make_async_copy(
                    yd_hbm.at[pl.ds(dref[0, 0, row], 1)], ybuf.at[slot, pl.ds(row, 1)], sem.at[slot]).start(priority=k)
            return c
        lax.fori_loop(0, tm, body, 0, unroll=8)

    slot = i % 2

    @pl.when(i == 0)
    def _():
        start_tile(dest_ref, 0)

    @pl.when(i + 1 < n)
    def _():
        start_tile(dest_next_ref, 1 - slot)

    pltpu.make_async_copy(yd_hbm.at[pl.ds(0, rows)], ybuf.at[slot], sem.at[slot]).wait()
    y1 = ybuf[slot, 0:tm, :]
    y2 = ybuf[slot, tm:rows, :]
    route = route_ref[...].T
    y = DN_ALPHA * x_ref[...] + (y1 * route[:, 0:1] + y2 * route[:, 1:2])
    o_ref[...] = _ln_rows(y, g_ref[...], b_ref[...])


def _moe_combine_ln(xt, y_disp, dest2, route, ln_g, ln_b):
    t, d = xt.shape
    n = t // ROUTE_TILE
    rows = EXPERT_TOP_K * ROUTE_TILE
    return pl.pallas_call(
        _combine_kernel,
        name="moe_combine_ln",
        grid=(n,),
        in_specs=[pl.BlockSpec((1, 1, rows), lambda i: (i, 0, 0), memory_space=pltpu.SMEM),
                  pl.BlockSpec((1, 1, rows), lambda i: (jnp.minimum(i + 1, n - 1), 0, 0), memory_space=pltpu.SMEM),
                  pl.BlockSpec((ROUTE_TILE, d), lambda i: (i, 0)),
                  pl.BlockSpec((ROUTE_ROWS, ROUTE_TILE), lambda i: (0, i)),
                  pl.BlockSpec((1, d), lambda i: (0, 0)),
                  pl.BlockSpec((1, d), lambda i: (0, 0)),
                  pl.BlockSpec(memory_space=pl.ANY)],
        out_specs=pl.BlockSpec((ROUTE_TILE, d), lambda i: (i, 0)),
        out_shape=jax.ShapeDtypeStruct((t, d), jnp.float32),
        scratch_shapes=[pltpu.VMEM((2, rows, d), jnp.float32), pltpu.SemaphoreType.DMA((2,))],
        compiler_params=pltpu.CompilerParams(dimension_semantics=("arbitrary",), vmem_limit_bytes=VMEM_LIMIT),
    )(dest2, dest2, xt, route, ln_g.reshape(1, d), ln_b.reshape(1, d), y_disp)


def _moe_sublayer(x, route, cnt, w1, w3, w2, layer, ln_g, ln_b):
    b, s, d = x.shape
    t = b * s
    xt = x.reshape(t, d)
    counts = cnt[N_GROUPS:N_GROUPS + N_EXPERTS, 0].astype(jnp.int32)
    n_tiles = (t * EXPERT_TOP_K) // EXPERT_TILE + N_EXPERTS
    tiles_per = (counts + EXPERT_TILE - 1) // EXPERT_TILE
    tile_end = jnp.cumsum(tiles_per)
    pad_start = (tile_end - tiles_per) * EXPERT_TILE
    n_used = tile_end[-1:].astype(jnp.int32)
    tile_ids = jnp.minimum(jnp.arange(n_tiles), n_used[0] - 1)
    tile_expert = jnp.sum(tile_ids[:, None] >= tile_end[None, :], axis=1).astype(jnp.int32)
    experts = route[2:4].astype(jnp.int32)
    first_row = jnp.sum(jnp.where(experts[..., None] == jnp.arange(N_EXPERTS), pad_start, 0), axis=-1)
    dest = first_row + route[4:6].astype(jnp.int32)
    dest2 = jnp.swapaxes(dest.reshape(EXPERT_TOP_K, t // ROUTE_TILE, ROUTE_TILE), 0, 1)
    dest2 = dest2.reshape(t // ROUTE_TILE, 1, EXPERT_TOP_K * ROUTE_TILE)
    last_tiles = jnp.where(tiles_per > 0, tile_end - 1, -1)
    tail_tiles = n_used[0] + jnp.arange(N_EXPERTS)
    tail_tiles = jnp.where(tail_tiles < n_tiles, tail_tiles, -1)
    zero_tiles = jnp.concatenate([last_tiles, tail_tiles]).astype(jnp.int32)
    x_disp = _moe_dispatch(xt, dest2, zero_tiles, n_tiles * EXPERT_TILE)
    y_disp = _moe_experts(x_disp, tile_expert, n_used, w1, w3, w2, layer)
    return _moe_combine_ln(xt, y_disp, dest2, route, ln_g, ln_b).reshape(b, s, d)


def kernel(x, mem, mem_wk, mem_wv, ev_w_in, ev_conv_w, ev_conv_b, ev_cnorm_g, ev_cnorm_b, ev_cmp_pe_k, ev_cmp_w1_k, ev_cmp_w2_k, ev_cmp_pe_v, ev_cmp_w1_v, ev_cmp_w2_v, ev_w_out, od_w_in, od_conv_w, od_w_out, ln_mix_g, ln_mix_b, xa_wq, xa_wo, ln_xa_g, ln_xa_b, moe_wg, moe_bg, moe_we, moe_be, moe_w1, moe_w3, moe_w2, ln_ffn_g, ln_ffn_b):
    b, s, d = x.shape
    mem_k, mem_v = _mem_kv(mem, mem_wk, mem_wv)
    for layer in range(DEPTH):
        i = layer // 2
        if layer % 2 == 0:
            a, kv_in, k_sel, k_win, q_t, v_sel_t, v_win_t, gate_t = _even_in_proj(x, ev_w_in[i])
            a = _conformer_conv(a, ev_conv_w[i], ev_conv_b[i], ev_cnorm_g[i], ev_cnorm_b[i])
            k_cmp, v_cmp_t = _compress_kv(kv_in, ev_cmp_pe_k[i], ev_cmp_w1_k[i], ev_cmp_w2_k[i], ev_cmp_pe_v[i], ev_cmp_w1_v[i], ev_cmp_w2_v[i])
            o = _nsa_attention(q_t, gate_t, k_cmp, v_cmp_t, k_sel, v_sel_t, k_win, v_win_t)
            x = _proj_residual_ln(a.reshape(b * s, -1), o.reshape(b * s, -1), x.reshape(b * s, d), ev_w_out[i], ln_mix_g[layer], ln_mix_b[layer]).reshape(b, s, d)
        else:
            x = _odd_mixer_sublayer(x, od_w_in[i], od_conv_w[i], od_w_out[i], ln_mix_g[layer], ln_mix_b[layer])
        x, route, cnt = _xattn_router_sublayer(x, mem_k, mem_v, xa_wq[layer], xa_wo[layer], ln_xa_g[layer], ln_xa_b[layer],
                                               moe_wg[layer], moe_bg[layer], moe_we[layer], moe_be[layer])
        x = _moe_sublayer(x, route, cnt, moe_w1, moe_w3, moe_w2, layer, ln_ffn_g[layer], ln_ffn_b[layer])
    return x
```

```python
import functools

import numpy as np
import jax
import jax.numpy as jnp
from jax import lax
from jax.experimental import pallas as pl
from jax.experimental.pallas import tpu as pltpu

D_MODEL = 1024
DEPTH = 2
CONV_CH = D_MODEL // 2
CONV_WIDTH = 31
NSA_HEADS = 8
NSA_KV_HEADS = 2
HEAD_DIM = (D_MODEL // 2) // NSA_HEADS
CMP_BLOCK = 32
CMP_STRIDE = 16
SEL_BLOCK = 64
SEL_TOP_N = 16
WINDOW = 512
Q_BLOCK = 256
FORCE_SCORE = 1e4
SHORT_CONV_WIDTH = 3
XA_HEADS = 4
XA_HEAD_DIM = D_MODEL // XA_HEADS
N_GROUPS = 4
EXPERTS_PER_GROUP = 8
N_EXPERTS = N_GROUPS * EXPERTS_PER_GROUP
EXPERT_TOP_K = 2
DN_ALPHA = (2 * DEPTH) ** 0.25
LN_EPS = 1e-5
NEG_INF = -1e30
KV_COLS = NSA_KV_HEADS * HEAD_DIM
QCOLS = NSA_HEADS * HEAD_DIM
GATE_ROWS = 3 * NSA_HEADS

LANES = 128
SUBLANES = 8
HEADS_PER_KV = NSA_HEADS // NSA_KV_HEADS
QL = Q_BLOCK * HEADS_PER_KV
SEL_TILE = 1024
WIN_SPAN = WINDOW + Q_BLOCK
BLOCKS_PER_TILE = SEL_TILE // SEL_BLOCK
SEQ_TILE = 1024
HALO = 32
ODD_HALO = 8
ROUTE_TILE = 512
EXPERT_TILE = 512
ROUTE_ROWS = 8
VMEM_LIMIT = 56 * 1024 * 1024

_NT = (((1,), (1,)), ((), ()))


def _ln_rows(y, g, b):
    mu = jnp.mean(y, axis=-1, keepdims=True)
    yc = y - mu
    var = jnp.mean(yc * yc, axis=-1, keepdims=True)
    return yc * lax.rsqrt(var + LN_EPS) * g + b


def _even_in_kernel(x_ref, wa_ref, wkv_ref, wk2_ref, wqT_ref, wvT_ref, wgT_ref,
                    a_ref, kv_ref, ks_ref, kw_ref, qT_ref, vsT_ref, vwT_ref, gT_ref):
    f32 = jnp.float32
    bf16 = jnp.bfloat16
    xb = x_ref[0].astype(bf16)
    av = jnp.dot(xb, wa_ref[...], preferred_element_type=f32)
    a_ref[0] = av[:, :CONV_CH] * jax.nn.sigmoid(av[:, CONV_CH:])
    kv_ref[0] = jnp.dot(xb, wkv_ref[...], preferred_element_type=f32)
    k2 = jnp.dot(xb, wk2_ref[...], preferred_element_type=f32)
    ks_ref[0] = k2[:, :KV_COLS].astype(bf16)
    kw_ref[0] = k2[:, KV_COLS:].astype(bf16)
    qT_ref[0] = lax.dot_general(wqT_ref[...], xb, _NT, preferred_element_type=f32).astype(bf16)
    vT = lax.dot_general(wvT_ref[...], xb, _NT, preferred_element_type=f32).astype(bf16)
    vsT_ref[0, 0] = vT[:KV_COLS]
    for j in range(SEQ_TILE // Q_BLOCK):
        vwT_ref[0, j] = vT[KV_COLS:, j * Q_BLOCK:(j + 1) * Q_BLOCK]
    gT_ref[0] = lax.dot_general(wgT_ref[...], xb, _NT, preferred_element_type=f32)


def _even_in_proj(x, w_in):
    b, s, d = x.shape
    bf16 = jnp.bfloat16
    c = np.cumsum((0, CONV_CH, CONV_CH, QCOLS, KV_COLS, KV_COLS, KV_COLS, KV_COLS, KV_COLS, KV_COLS, GATE_ROWS))
    col = lambda i, j: w_in[:, c[i]:c[j]]
    wa = col(0, 2).astype(bf16)
    wq_t = col(2, 3).T.astype(bf16)
    wkv = col(3, 5).astype(bf16)
    wk2 = jnp.concatenate([col(5, 6), col(7, 8)], axis=1).astype(bf16)
    wv_t = jnp.concatenate([col(6, 7), col(8, 9)], axis=1).T.astype(bf16)
    wg_t = col(9, 10).T.astype(bf16)
    ts = SEQ_TILE
    per_sel = SEL_TILE // ts
    full = lambda shape: pl.BlockSpec(shape, lambda i, j: (0,) * len(shape))
    return pl.pallas_call(
        _even_in_kernel,
        name="even_in_proj",
        grid=(b, s // ts),
        in_specs=[pl.BlockSpec((1, ts, d), lambda i, j: (i, j, 0)),
                  full(wa.shape), full(wkv.shape), full(wk2.shape), full(wq_t.shape), full(wv_t.shape), full(wg_t.shape)],
        out_specs=[pl.BlockSpec((1, ts, CONV_CH), lambda i, j: (i, j, 0)),
                   pl.BlockSpec((1, ts, 2 * KV_COLS), lambda i, j: (i, j, 0)),
                   pl.BlockSpec((1, ts, KV_COLS), lambda i, j: (i, j, 0)),
                   pl.BlockSpec((1, ts, KV_COLS), lambda i, j: (i, j, 0)),
                   pl.BlockSpec((1, QCOLS, ts), lambda i, j: (i, 0, j)),
                   pl.BlockSpec((1, 1, KV_COLS, ts), lambda i, j: (i, j // per_sel, 0, j % per_sel)),
                   pl.BlockSpec((1, ts // Q_BLOCK, KV_COLS, Q_BLOCK), lambda i, j: (i, j, 0, 0)),
                   pl.BlockSpec((1, GATE_ROWS, ts), lambda i, j: (i, 0, j))],
        out_shape=[jax.ShapeDtypeStruct((b, s, CONV_CH), jnp.float32),
                   jax.ShapeDtypeStruct((b, s, 2 * KV_COLS), jnp.float32),
                   jax.ShapeDtypeStruct((b, s, KV_COLS), bf16),
                   jax.ShapeDtypeStruct((b, s, KV_COLS), bf16),
                   jax.ShapeDtypeStruct((b, QCOLS, s), bf16),
                   jax.ShapeDtypeStruct((b, s // SEL_TILE, KV_COLS, SEL_TILE), bf16),
                   jax.ShapeDtypeStruct((b, s // Q_BLOCK, KV_COLS, Q_BLOCK), bf16),
                   jax.ShapeDtypeStruct((b, GATE_ROWS, s), jnp.float32)],
        compiler_params=pltpu.CompilerParams(dimension_semantics=("arbitrary", "arbitrary"), vmem_limit_bytes=VMEM_LIMIT),
    )(x, wa, wkv, wk2, wq_t, wv_t, wg_t)


def _conv_kernel(cur_ref, halo_ref, w_ref, cb_ref, g_ref, b_ref, o_ref, ext_ref, win_ref):
    j = pl.program_id(1)
    ts = cur_ref.shape[1]
    halo = halo_ref[0]
    ext_ref[0:HALO, :] = jnp.where(j > 0, halo, jnp.zeros_like(halo))
    ext_ref[HALO:HALO + ts, :] = cur_ref[0]
    first = HALO - (CONV_WIDTH - 1)
    acc = jnp.zeros((ts, CONV_CH), jnp.float32)
    for p in range(SUBLANES):
        n_a = len(range(p, CONV_WIDTH, SUBLANES))
        rows = ts + SUBLANES * (n_a - 1)
        win_ref[0:rows, :] = ext_ref[first + p:first + p + rows, :]
        for a in range(n_a):
            k = SUBLANES * a + p
            acc = acc + w_ref[k:k + 1, :] * win_ref[SUBLANES * a:SUBLANES * a + ts, :]
    y = _ln_rows(acc + cb_ref[...], g_ref[...], b_ref[...])
    o_ref[0] = (y * jax.nn.sigmoid(y)).astype(o_ref.dtype)


def _conformer_conv(a, conv_w, conv_b, cn_g, cn_b):
    b, s, c = a.shape
    ts = SEQ_TILE
    per = ts // HALO
    row = lambda v: v.reshape(1, c)
    return pl.pallas_call(
        _conv_kernel,
        name="conformer_conv",
        grid=(b, s // ts),
        in_specs=[pl.BlockSpec((1, ts, c), lambda i, j: (i, j, 0)),
                  pl.BlockSpec((1, HALO, c), lambda i, j: (i, jnp.maximum(j * per - 1, 0), 0)),
                  pl.BlockSpec((CONV_WIDTH, c), lambda i, j: (0, 0)),
                  pl.BlockSpec((1, c), lambda i, j: (0, 0)),
                  pl.BlockSpec((1, c), lambda i, j: (0, 0)),
                  pl.BlockSpec((1, c), lambda i, j: (0, 0))],
        out_specs=pl.BlockSpec((1, ts, c), lambda i, j: (i, j, 0)),
        out_shape=jax.ShapeDtypeStruct((b, s, c), jnp.bfloat16),
        scratch_shapes=[pltpu.VMEM((HALO + ts, c), jnp.float32), pltpu.VMEM((HALO + ts, c), jnp.float32)],
        compiler_params=pltpu.CompilerParams(dimension_semantics=("arbitrary", "arbitrary")),
    )(a, a, conv_w, row(conv_b), row(cn_g), row(cn_b))


def _compress_kernel(r_ref, pe_ref, w1_ref, w2_ref, w2T_ref, o_ref, oT_ref):
    f32 = jnp.float32
    bf16 = jnp.bfloat16
    nch = o_ref.shape[1]
    hw = w1_ref.shape[3]
    first = jnp.zeros((nch, hw), f32)
    second = jnp.zeros((nch, hw), f32)
    for l in range(CMP_STRIDE):
        tok = r_ref[pl.ds(l, nch, stride=CMP_STRIDE), :]
        first = first + jnp.dot((tok + pe_ref[0, l:l + 1, :]).astype(bf16), w1_ref[0, l], preferred_element_type=f32)
        second = second + jnp.dot((tok + pe_ref[0, CMP_STRIDE + l:CMP_STRIDE + l + 1, :]).astype(bf16),
                                  w1_ref[0, CMP_STRIDE + l], preferred_element_type=f32)
    second_next = jnp.concatenate([second[1:], jnp.zeros_like(second[0:1])], axis=0)
    hid = jax.nn.gelu(first + second_next).astype(bf16)
    o_ref[0] = jnp.dot(hid, w2_ref[0], preferred_element_type=f32).astype(bf16)
    oT_ref[0] = lax.dot_general(w2T_ref[0], hid, _NT, preferred_element_type=f32).astype(bf16)


def _compress_kv(kv_in, pe_k, w1_k, w2_k, pe_v, w1_v, w2_v):
    b, s, _ = kv_in.shape
    bf16 = jnp.bfloat16
    nch = s // CMP_STRIDE
    g = NSA_KV_HEADS
    hidden = w1_k.shape[1]

    def expand(pe, w1, w2):
        same = jnp.eye(g, dtype=bool)
        w1r = w1.reshape(CMP_BLOCK, HEAD_DIM, hidden)
        w1e = jnp.where(same[None, :, None, :, None], w1r[:, None, :, None, :], 0.0)
        w2e = jnp.where(same[:, None, :, None], w2[None, :, None, :], 0.0)
        return jnp.tile(pe, (1, g)), w1e.reshape(CMP_BLOCK, KV_COLS, g * hidden), w2e.reshape(g * hidden, KV_COLS)

    pk, w1k, w2k = expand(pe_k, w1_k, w2_k)
    pv, w1v, w2v = expand(pe_v, w1_v, w2_v)
    pe = jnp.stack([pk, pv])
    w1 = jnp.stack([w1k, w1v]).astype(bf16)
    w2 = jnp.stack([w2k, w2v]).astype(bf16)
    w2t = jnp.swapaxes(w2, 1, 2)
    o, o_t = pl.pallas_call(
        _compress_kernel,
        name="compress_kv",
        grid=(b, 2),
        in_specs=[pl.BlockSpec((s, KV_COLS), lambda i, j: (i, j)),
                  pl.BlockSpec((1,) + pe.shape[1:], lambda i, j: (j, 0, 0)),
                  pl.BlockSpec((1,) + w1.shape[1:], lambda i, j: (j, 0, 0, 0)),
                  pl.BlockSpec((1,) + w2.shape[1:], lambda i, j: (j, 0, 0)),
                  pl.BlockSpec((1,) + w2t.shape[1:], lambda i, j: (j, 0, 0))],
        out_specs=[pl.BlockSpec((1, nch, KV_COLS), lambda i, j: (2 * i + j, 0, 0)),
                   pl.BlockSpec((1, KV_COLS, nch), lambda i, j: (2 * i + j, 0, 0))],
        out_shape=[jax.ShapeDtypeStruct((b * 2, nch, KV_COLS), bf16), jax.ShapeDtypeStruct((b * 2, KV_COLS, nch), bf16)],
        compiler_params=pltpu.CompilerParams(dimension_semantics=("arbitrary", "arbitrary"), vmem_limit_bytes=VMEM_LIMIT),
    )(kv_in.reshape(b * s, 2 * KV_COLS), pe, w1, w2, w2t)
    return o.reshape(b, 2, nch, KV_COLS)[:, 0], o_t.reshape(b, 2, KV_COLS, nch)[:, 1]


def _proj_ln_kernel(a_ref, o_ref, x_ref, w_ref, g_ref, b_ref, y_ref):
    lhs = jnp.concatenate([a_ref[...], o_ref[...]], axis=1)
    mix = jnp.dot(lhs, w_ref[...], preferred_element_type=jnp.float32)
    y_ref[...] = _ln_rows(DN_ALPHA * x_ref[...] + mix, g_ref[...], b_ref[...])


def _proj_residual_ln(a, o, x, w, ln_g, ln_b):
    t, d = x.shape
    tm = SEQ_TILE
    return pl.pallas_call(
        _proj_ln_kernel,
        name="proj_residual_ln",
        grid=(t // tm,),
        in_specs=[pl.BlockSpec((tm, a.shape[1]), lambda i: (i, 0)),
                  pl.BlockSpec((tm, o.shape[1]), lambda i: (i, 0)),
                  pl.BlockSpec((tm, d), lambda i: (i, 0)),
                  pl.BlockSpec(w.shape, lambda i: (0, 0)),
                  pl.BlockSpec((1, d), lambda i: (0, 0)),
                  pl.BlockSpec((1, d), lambda i: (0, 0))],
        out_specs=pl.BlockSpec((tm, d), lambda i: (i, 0)),
        out_shape=jax.ShapeDtypeStruct((t, d), jnp.float32),
        compiler_params=pltpu.CompilerParams(dimension_semantics=("arbitrary",), vmem_limit_bytes=VMEM_LIMIT),
    )(a, o, x, w.astype(jnp.bfloat16), ln_g.reshape(1, d), ln_b.reshape(1, d))


def _mem_kv_kernel(m_ref, wk_ref, wv_ref, k_ref, v_ref):
    mb = m_ref[...].astype(jnp.bfloat16)
    k_ref[...] = jnp.dot(mb, wk_ref[...], preferred_element_type=jnp.float32).astype(jnp.bfloat16)
    v_ref[...] = jnp.dot(mb, wv_ref[...], preferred_element_type=jnp.float32).astype(jnp.bfloat16)


def _mem_kv(mem, wk, wv):
    b, m, d = mem.shape
    bf16 = jnp.bfloat16
    k, v = pl.pallas_call(
        _mem_kv_kernel,
        name="mem_kv",
        grid=(b,),
        in_specs=[pl.BlockSpec((m, d), lambda i: (i, 0)), pl.BlockSpec((d, d), lambda i: (0, 0)), pl.BlockSpec((d, d), lambda i: (0, 0))],
        out_specs=[pl.BlockSpec((m, d), lambda i: (i, 0)), pl.BlockSpec((m, d), lambda i: (i, 0))],
        out_shape=[jax.ShapeDtypeStruct((b * m, d), bf16), jax.ShapeDtypeStruct((b * m, d), bf16)],
        compiler_params=pltpu.CompilerParams(dimension_semantics=("arbitrary",), vmem_limit_bytes=VMEM_LIMIT),
    )(mem.reshape(b * m, d), wk.astype(bf16), wv.astype(bf16))
    return k.reshape(b, m, d), v.reshape(b, m, d)


def _xattn_kernel(x_ref, k_ref, v_ref, wq_ref, wo_ref, g_ref, b_ref, rw_ref, rb_ref, tri_ref,
                  y_ref, route_ref, cnt_ref, run_ref):
    f32 = jnp.float32
    bf16 = jnp.bfloat16
    x = x_ref[0]
    q = (jnp.dot(x.astype(bf16), wq_ref[...], preferred_element_type=f32).astype(bf16)
         * jnp.asarray(XA_HEAD_DIM ** -0.5, bf16))
    heads = []
    for h in range(XA_HEADS):
        cols = slice(h * XA_HEAD_DIM, (h + 1) * XA_HEAD_DIM)
        s = lax.dot_general(q[:, cols], k_ref[0, :, cols], _NT, preferred_element_type=f32)
        m = jnp.max(s, axis=1, keepdims=True)
        p = jnp.exp(s - m)
        p = p / jnp.sum(p, axis=1, keepdims=True)
        heads.append(jnp.dot(p.astype(bf16), v_ref[0, :, cols], preferred_element_type=f32).astype(bf16))
    att = jnp.concatenate(heads, axis=1)
    out = jnp.dot(att, wo_ref[...], preferred_element_type=f32)
    y = _ln_rows(DN_ALPHA * x + out, g_ref[...], b_ref[...])
    y_ref[0] = y
    first_step = (pl.program_id(0) == 0) & (pl.program_id(1) == 0)
    _route_tile(y.astype(bf16), first_step, rw_ref, rb_ref, tri_ref, route_ref, cnt_ref, run_ref)


def _xattn_router_sublayer(x, mem_k, mem_v, wq, wo, ln_g, ln_b, wg, bg, we, be):
    b, s, d = x.shape
    m = mem_k.shape[1]
    ts = SEQ_TILE
    per_seq = s // ts
    bf16 = jnp.bfloat16
    rw, rb, tri = _router_operands(wg, bg, we, be, ts)
    const = lambda shape: pl.BlockSpec(shape, lambda i, j: (0,) * len(shape))
    return pl.pallas_call(
        _xattn_kernel,
        name="xattn_router_sublayer",
        grid=(b, per_seq),
        in_specs=[pl.BlockSpec((1, ts, d), lambda i, j: (i, j, 0)),
                  pl.BlockSpec((1, m, d), lambda i, j: (i, 0, 0)),
                  pl.BlockSpec((1, m, d), lambda i, j: (i, 0, 0)),
                  const((d, d)), const((d, d)), const((1, d)), const((1, d)),
                  const(rw.shape), const(rb.shape), const(tri.shape)],
        out_specs=[pl.BlockSpec((1, ts, d), lambda i, j: (i, j, 0)),
                   pl.BlockSpec((ROUTE_ROWS, ts), lambda i, j: (0, i * per_seq + j)),
                   const((LANES, 1))],
        out_shape=[jax.ShapeDtypeStruct((b, s, d), jnp.float32),
                   jax.ShapeDtypeStruct((ROUTE_ROWS, b * s), jnp.float32),
                   jax.ShapeDtypeStruct((LANES, 1), jnp.float32)],
        scratch_shapes=[pltpu.VMEM((LANES, 1), jnp.float32)],
        compiler_params=pltpu.CompilerParams(dimension_semantics=("arbitrary", "arbitrary"), vmem_limit_bytes=VMEM_LIMIT),
    )(x, mem_k, mem_v, wq.astype(bf16), wo.astype(bf16), ln_g.reshape(1, d), ln_b.reshape(1, d), rw, rb, tri)


def _odd_kernel(x_ref, halo_ref, wb_ref, wc_ref, wh_ref, cw_ref, wo_ref, g_ref, b_ref, y_ref):
    f32 = jnp.float32
    bf16 = jnp.bfloat16
    j = pl.program_id(1)
    ts = x_ref.shape[1]
    x = x_ref[0]
    xe = jnp.concatenate([halo_ref[0], x], axis=0).astype(bf16)
    u = (jnp.dot(xe, wc_ref[...], preferred_element_type=f32) * jnp.dot(xe, wh_ref[...], preferred_element_type=f32))
    row = lax.broadcasted_iota(jnp.int32, (ODD_HALO + ts, 1), 0)
    u = jnp.where((row >= ODD_HALO) | (j > 0), u, 0.0)
    conv = jnp.zeros((ts, u.shape[1]), f32)
    for k in range(SHORT_CONV_WIDTH):
        off = ODD_HALO - (SHORT_CONV_WIDTH - 1) + k
        conv = conv + cw_ref[k:k + 1, :] * u[off:off + ts, :]
    gate_b = jnp.dot(xe[ODD_HALO:], wb_ref[...], preferred_element_type=f32)
    mix = jnp.dot((gate_b * conv).astype(bf16), wo_ref[...], preferred_element_type=f32)
    y_ref[0] = _ln_rows(DN_ALPHA * x + mix, g_ref[...], b_ref[...])


def _odd_mixer_sublayer(x, w_in, conv_w, w_out, ln_g, ln_b):
    b, s, d = x.shape
    ts = SEQ_TILE
    per = ts // ODD_HALO
    bf16 = jnp.bfloat16
    wb, wc, wh = (w_in[:, i * d:(i + 1) * d].astype(bf16) for i in range(3))
    full = lambda shape: pl.BlockSpec(shape, lambda i, j: (0,) * len(shape))
    return pl.pallas_call(
        _odd_kernel,
        name="odd_mixer_sublayer",
        grid=(b, s // ts),
        in_specs=[pl.BlockSpec((1, ts, d), lambda i, j: (i, j, 0)),
                  pl.BlockSpec((1, ODD_HALO, d), lambda i, j: (i, jnp.maximum(j * per - 1, 0), 0)),
                  full((d, d)), full((d, d)), full((d, d)), full(conv_w.shape), full((d, d)), full((1, d)), full((1, d))],
        out_specs=pl.BlockSpec((1, ts, d), lambda i, j: (i, j, 0)),
        out_shape=jax.ShapeDtypeStruct((b, s, d), jnp.float32),
        compiler_params=pltpu.CompilerParams(dimension_semantics=("arbitrary", "arbitrary"), vmem_limit_bytes=VMEM_LIMIT),
    )(x, x, wb, wc, wh, conv_w, w_out.astype(bf16), ln_g.reshape(1, d), ln_b.reshape(1, d))


def _cmp_to_sel_matrix(n_cmp, n_sel):
    c0 = np.arange(n_cmp) * CMP_STRIDE
    s0 = np.arange(n_sel) * SEL_BLOCK
    ov = np.minimum(c0[:, None] + CMP_BLOCK, s0[None, :] + SEL_BLOCK) - np.maximum(c0[:, None], s0[None, :])
    return (np.clip(ov, 0, None) / CMP_BLOCK).astype(np.float32)


def _nsa_kernel(qT_ref, gT_ref, kc_ref, vcT_ref, mselT_ref, ks_ref, vsT_ref, kw_ref, vwT_ref, o_ref,
                score_ref, sel_ref, *, n_sel):
    c = pl.program_id(1)
    q0 = c * Q_BLOCK
    f32 = jnp.float32
    bf16 = jnp.bfloat16
    n_cmp_pad = kc_ref.shape[1]

    lane_q = lax.broadcasted_iota(jnp.int32, (1, QL), 1) % Q_BLOCK
    t_row = q0 + lane_q
    t_row_q = q0 + lax.broadcasted_iota(jnp.int32, (1, Q_BLOCK), 1)
    cur_q = t_row_q // SEL_BLOCK

    top_n = min(SEL_TOP_N, n_sel)
    m_iota = lax.broadcasted_iota(jnp.int32, (n_sel, Q_BLOCK), 0)
    forced = (m_iota == 0) | (m_iota == cur_q) | (m_iota == cur_q - 1)
    valid = m_iota <= cur_q
    n_comp = jnp.minimum(q0 // SEL_BLOCK + Q_BLOCK // SEL_BLOCK, n_sel)

    q_g, o_c = [], []
    for g in range(NSA_KV_HEADS):
        pieces = []
        for hg in range(HEADS_PER_KV):
            h = g * HEADS_PER_KV + hg
            qh = qT_ref[0, h * HEAD_DIM:(h + 1) * HEAD_DIM, :] * jnp.asarray(HEAD_DIM ** -0.5, bf16)
            z = jnp.zeros_like(qh)
            pieces.append(jnp.concatenate([qh, z] if g == 0 else [z, qh], axis=0))
        qTp = jnp.concatenate(pieces, axis=1)
        q_g.append(qTp)
        rows = slice(g * HEAD_DIM, (g + 1) * HEAD_DIM)

        s_c = jnp.dot(kc_ref[0], qTp, preferred_element_type=f32)
        n_iota = lax.broadcasted_iota(jnp.int32, (n_cmp_pad, QL), 0)
        mask_c = (n_iota * CMP_STRIDE + (CMP_BLOCK - 1)) <= t_row
        s_c = jnp.where(mask_c, s_c, NEG_INF)
        m_c = jnp.max(s_c, axis=0, keepdims=True)
        p_c = jnp.where(mask_c, jnp.exp(s_c - m_c), 0.0)
        l_c = jnp.sum(p_c, axis=0, keepdims=True)
        p_c = p_c * jnp.where(l_c > 0.0, 1.0 / l_c, 0.0)
        p_cb = p_c.astype(bf16)
        o_c.append(jnp.dot(vcT_ref[0], p_cb, preferred_element_type=f32)[rows])
        imp4 = jnp.dot(mselT_ref[...], p_cb, preferred_element_type=f32)
        imp = imp4[:, 0:Q_BLOCK]
        for hg in range(1, HEADS_PER_KV):
            imp = imp + imp4[:, hg * Q_BLOCK:(hg + 1) * Q_BLOCK]

        score = jnp.where(valid, jnp.where(forced, FORCE_SCORE, imp), NEG_INF)
        score_ref[...] = score

        def rank_body(i8, rank, score=score):
            base = pl.multiple_of(i8 * SUBLANES, SUBLANES)
            rows = score_ref[pl.ds(base, SUBLANES), :]
            for u in range(SUBLANES):
                row = rows[u:u + 1, :]
                beats = (row > score) | ((row == score) & (base + u < m_iota))
                rank = rank + beats.astype(jnp.int32)
            return rank

        rank = lax.fori_loop(0, (n_comp + SUBLANES - 1) // SUBLANES, rank_body,
                             jnp.zeros((n_sel, Q_BLOCK), jnp.int32))
        sel_ref[g] = jnp.where((rank < top_n) & valid, 0.0, NEG_INF)

    def sel_tile(j, carry, causal, size=SEL_TILE):
        k_tile = ks_ref[0, j, 0:size, :]
        v_tile = vsT_ref[0, j, :, 0:size]
        blk0 = pl.multiple_of(j * BLOCKS_PER_TILE, BLOCKS_PER_TILE)
        out = []
        for g in range(NSA_KV_HEADS):
            m_i, l_i, acc = carry[g]
            s = jnp.dot(k_tile, q_g[g], preferred_element_type=f32)
            selrows = sel_ref[g, pl.ds(blk0, BLOCKS_PER_TILE), :]
            bias = jnp.concatenate(
                [jnp.broadcast_to(selrows[r:r + 1, :], (SEL_BLOCK, Q_BLOCK)) for r in range(size // SEL_BLOCK)], axis=0)
            if causal:
                key = j * SEL_TILE + lax.broadcasted_iota(jnp.int32, (size, Q_BLOCK), 0)
                bias = jnp.where(key <= t_row_q, bias, NEG_INF)
            s = s + jnp.concatenate([bias] * HEADS_PER_KV, axis=1)
            m_new = jnp.maximum(m_i, jnp.max(s, axis=0, keepdims=True))
            alpha = jnp.exp(m_i - m_new)
            p = jnp.exp(s - m_new)
            l_new = alpha * l_i + jnp.sum(p, axis=0, keepdims=True)
            pv = jnp.dot(v_tile, p.astype(bf16), preferred_element_type=f32)
            out.append((m_new, l_new, alpha * acc + pv))
        return tuple(out)

    n_full = q0 // SEL_TILE
    init = tuple((jnp.full((1, QL), NEG_INF, f32), jnp.zeros((1, QL), f32), jnp.zeros((KV_COLS, QL), f32))
                 for _ in range(NSA_KV_HEADS))
    carry = lax.fori_loop(0, n_full, functools.partial(sel_tile, causal=False), init)
    carry = lax.cond(
        q0 % SEL_TILE + Q_BLOCK <= SEL_TILE // 2,
        functools.partial(sel_tile, n_full, causal=True, size=SEL_TILE // 2),
        functools.partial(sel_tile, n_full, causal=True, size=SEL_TILE),
        carry)

    start = pl.multiple_of(jnp.maximum(q0 - WINDOW, 0), Q_BLOCK)
    j0 = start // Q_BLOCK
    k_win = kw_ref[0, pl.ds(start, WIN_SPAN), :]
    key_w = start + lax.broadcasted_iota(jnp.int32, (WIN_SPAN, Q_BLOCK), 0)
    bias_w = jnp.where(key_w <= t_row_q, 0.0, NEG_INF)
    bias_w = jnp.where(key_w > t_row_q - WINDOW, bias_w, NEG_INF)
    bias_w = jnp.concatenate([bias_w] * HEADS_PER_KV, axis=1)
    for g in range(NSA_KV_HEADS):
        rows = slice(g * HEAD_DIM, (g + 1) * HEAD_DIM)
        m_s, l_s, acc_s = carry[g]
        o_s = acc_s[rows] * (1.0 / l_s)
        s_w = jnp.dot(k_win, q_g[g], preferred_element_type=f32) + bias_w
        m_w = jnp.max(s_w, axis=0, keepdims=True)
        p_w = jnp.exp(s_w - m_w)
        l_w = jnp.sum(p_w, axis=0, keepdims=True)
        p_wb = p_w.astype(bf16)
        acc_w = jnp.zeros((KV_COLS, QL), f32)
        for i in range(WIN_SPAN // Q_BLOCK):
            acc_w = acc_w + jnp.dot(vwT_ref[0, j0 + i], p_wb[i * Q_BLOCK:(i + 1) * Q_BLOCK, :], preferred_element_type=f32)
        o_w = acc_w[rows] * (1.0 / l_w)

        for pair in range(HEADS_PER_KV // 2):
            halves = []
            for hg in (2 * pair, 2 * pair + 1):
                h = g * HEADS_PER_KV + hg
                lanes = slice(hg * Q_BLOCK, (hg + 1) * Q_BLOCK)
                gate = jax.nn.sigmoid(gT_ref[0, 3 * h:3 * h + 3, :])
                halves.append(gate[0:1] * o_c[g][:, lanes] + gate[1:2] * o_s[:, lanes] + gate[2:3] * o_w[:, lanes])
            both = jnp.concatenate(halves, axis=0)
            col0 = (g * HEADS_PER_KV + 2 * pair) * HEAD_DIM
            o_ref[0, :, col0:col0 + 2 * HEAD_DIM] = both.T.astype(o_ref.dtype)


def _nsa_attention(qT, gT, kc, vcT, ks, vsT, kw, vwT):
    b, hd, s = qT.shape
    n_sel = s // SEL_BLOCK
    nc = kc.shape[1]
    n_cmp = (s - CMP_BLOCK) // CMP_STRIDE + 1
    mselT = jnp.asarray(np.pad(_cmp_to_sel_matrix(n_cmp, n_sel).T, ((0, 0), (0, nc - n_cmp))), jnp.bfloat16)
    ks4 = ks.reshape(b, s // SEL_TILE, SEL_TILE, KV_COLS)
    return pl.pallas_call(
        functools.partial(_nsa_kernel, n_sel=n_sel),
        name="nsa_attention",
        grid=(b, s // Q_BLOCK),
        in_specs=[
            pl.BlockSpec((1, hd, Q_BLOCK), lambda i, c: (i, 0, c)),
            pl.BlockSpec((1, 3 * NSA_HEADS, Q_BLOCK), lambda i, c: (i, 0, c)),
            pl.BlockSpec((1, nc, KV_COLS), lambda i, c: (i, 0, 0)),
            pl.BlockSpec((1, KV_COLS, nc), lambda i, c: (i, 0, 0)),
            pl.BlockSpec((n_sel, nc), lambda i, c: (0, 0)),
            pl.BlockSpec((1, s // SEL_TILE, SEL_TILE, KV_COLS), lambda i, c: (i, 0, 0, 0)),
            pl.BlockSpec((1, s // SEL_TILE, KV_COLS, SEL_TILE), lambda i, c: (i, 0, 0, 0)),
            pl.BlockSpec((1, s, KV_COLS), lambda i, c: (i, 0, 0)),
            pl.BlockSpec((1, s // Q_BLOCK, KV_COLS, Q_BLOCK), lambda i, c: (i, 0, 0, 0)),
        ],
        out_specs=pl.BlockSpec((1, Q_BLOCK, hd), lambda i, c: (i, c, 0)),
        out_shape=jax.ShapeDtypeStruct((b, s, hd), jnp.bfloat16),
        scratch_shapes=[pltpu.VMEM((n_sel, Q_BLOCK), jnp.float32),
                        pltpu.VMEM((NSA_KV_HEADS, n_sel, Q_BLOCK), jnp.float32)],
        compiler_params=pltpu.CompilerParams(dimension_semantics=("arbitrary", "arbitrary")),
    )(qT, gT, kc, vcT, mselT, ks4, vsT, kw, vwT)


def _route_tile(xb, first_step, wT_ref, b_ref, tri_ref, route_ref, cnt_ref, run_ref):
    f32 = jnp.float32

    @pl.when(first_step)
    def _():
        run_ref[...] = jnp.zeros_like(run_ref)

    tm = xb.shape[0]
    logits = lax.dot_general(wT_ref[...], xb, _NT, preferred_element_type=f32) + b_ref[...]
    sub = lax.broadcasted_iota(jnp.int32, (LANES, tm), 0)
    is_g = sub < N_GROUPS
    gl = jnp.where(is_g, logits, NEG_INF)
    g_max = jnp.max(gl, axis=0, keepdims=True)
    g_star = jnp.min(jnp.where(gl == g_max, sub, LANES), axis=0, keepdims=True)
    p_group = 1.0 / jnp.sum(jnp.where(is_g, jnp.exp(gl - g_max), 0.0), axis=0, keepdims=True)
    lo = N_GROUPS + g_star * EXPERTS_PER_GROUP
    in_grp = (sub >= lo) & (sub < lo + EXPERTS_PER_GROUP)
    el = jnp.where(in_grp, logits, NEG_INF)
    v1 = jnp.max(el, axis=0, keepdims=True)
    i1 = jnp.min(jnp.where(el == v1, sub, LANES), axis=0, keepdims=True)
    el2 = jnp.where(sub == i1, NEG_INF, el)
    v2 = jnp.max(el2, axis=0, keepdims=True)
    i2 = jnp.min(jnp.where(el2 == v2, sub, LANES), axis=0, keepdims=True)
    e21 = jnp.exp(v2 - v1)
    gate1 = p_group * (1.0 / (1.0 + e21))
    gate2 = p_group * (e21 / (1.0 + e21))
    oh1 = (sub == i1).astype(f32)
    oh2 = (sub == i2).astype(f32)
    both = oh1 + oh2
    before = jnp.dot(both.astype(jnp.bfloat16), tri_ref[...], preferred_element_type=f32) + run_ref[...]
    rank1 = jnp.sum(oh1 * before, axis=0, keepdims=True)
    rank2 = jnp.sum(oh2 * before, axis=0, keepdims=True)
    run_ref[...] = run_ref[...] + jnp.sum(both, axis=1, keepdims=True)
    cnt_ref[...] = run_ref[...]
    zero = jnp.zeros_like(gate1)
    route_ref[...] = jnp.concatenate(
        [gate1, gate2, (i1 - N_GROUPS).astype(f32), (i2 - N_GROUPS).astype(f32), rank1, rank2, zero, zero], axis=0)


def _router_operands(wg, bg, we, be, tile):
    d = wg.shape[0]
    pad = LANES - N_GROUPS - N_EXPERTS
    w_t = jnp.concatenate([wg, we.reshape(d, N_EXPERTS), jnp.zeros((d, pad), wg.dtype)], axis=1).T
    bias = jnp.concatenate([bg, be.reshape(N_EXPERTS), jnp.zeros((pad,), bg.dtype)])[:, None]
    tri = jnp.asarray(np.triu(np.ones((tile, tile), np.float32), 1), jnp.bfloat16)
    return w_t.astype(jnp.bfloat16), bias, tri


def _dispatch_kernel(zt_ref, dest_ref, x_ref, xd_hbm, zbuf, sem, zsem):
    tm = x_ref.shape[0]

    @pl.when(pl.program_id(0) == 0)
    def _():
        zbuf[...] = jnp.zeros_like(zbuf)

        def zero_copy(k):
            start = pl.multiple_of(jnp.maximum(zt_ref[k], 0) * EXPERT_TILE, EXPERT_TILE)
            return pltpu.make_async_copy(zbuf, xd_hbm.at[pl.ds(start, EXPERT_TILE)], zsem)

        def start_body(k, c):
            @pl.when(zt_ref[k] >= 0)
            def _():
                zero_copy(k).start()
            return c

        def wait_body(k, c):
            @pl.when(zt_ref[k] >= 0)
            def _():
                zero_copy(k).wait()
            return c

        lax.fori_loop(0, zt_ref.shape[0], start_body, 0)
        lax.fori_loop(0, zt_ref.shape[0], wait_body, 0)

    def body(r, c):
        for k in range(EXPERT_TOP_K):
            pltpu.make_async_copy(
                x_ref.at[pl.ds(r, 1)], xd_hbm.at[pl.ds(dest_ref[0, 0, k * tm + r], 1)], sem).start(priority=k)
        return c

    lax.fori_loop(0, tm, body, 0, unroll=8)
    for k in range(EXPERT_TOP_K):
        pltpu.make_async_copy(x_ref, xd_hbm.at[pl.ds(0, tm)], sem).wait()


def _moe_dispatch(xt, dest2, zero_tiles, n_rows):
    t, d = xt.shape
    grid_spec = pltpu.PrefetchScalarGridSpec(
        num_scalar_prefetch=1,
        grid=(t // ROUTE_TILE,),
        in_specs=[pl.BlockSpec((1, 1, EXPERT_TOP_K * ROUTE_TILE), lambda i, zt: (i, 0, 0), memory_space=pltpu.SMEM),
                  pl.BlockSpec((ROUTE_TILE, d), lambda i, zt: (i, 0))],
        out_specs=pl.BlockSpec(memory_space=pl.ANY),
        scratch_shapes=[pltpu.VMEM((EXPERT_TILE, d), xt.dtype), pltpu.SemaphoreType.DMA(()), pltpu.SemaphoreType.DMA(())],
    )
    return pl.pallas_call(
        _dispatch_kernel,
        name="moe_dispatch",
        grid_spec=grid_spec,
        out_shape=jax.ShapeDtypeStruct((n_rows, d), xt.dtype),
        compiler_params=pltpu.CompilerParams(dimension_semantics=("arbitrary",), has_side_effects=True),
    )(zero_tiles, dest2, xt)


def _expert_kernel(te_ref, nu_ref, xd_ref, w1_ref, w3_ref, w2_ref, y_ref, w1b, w3b, w2b):
    i = pl.program_id(0)
    used = i < nu_ref[0]
    new_expert = (i == 0) | (te_ref[i] != te_ref[jnp.maximum(i - 1, 0)])

    @pl.when(used & new_expert)
    def _():
        w1b[...] = w1_ref[0, 0].astype(jnp.bfloat16)
        w3b[...] = w3_ref[0, 0].astype(jnp.bfloat16)
        w2b[...] = w2_ref[0, 0].astype(jnp.bfloat16)

    @pl.when(used)
    def _():
        xb = xd_ref[...].astype(jnp.bfloat16)
        h1 = jnp.dot(xb, w1b[...], preferred_element_type=jnp.float32)
        h3 = jnp.dot(xb, w3b[...], preferred_element_type=jnp.float32)
        a = (h1 * jax.nn.sigmoid(h1) * h3).astype(jnp.bfloat16)
        y_ref[...] = jnp.dot(a, w2b[...], preferred_element_type=jnp.float32)

    @pl.when(jnp.logical_not(used))
    def _():
        y_ref[...] = jnp.zeros_like(y_ref)


def _moe_experts(x_disp, tile_expert, n_used, w1, w3, w2, layer):
    n_rows, d = x_disp.shape
    n_tiles = n_rows // EXPERT_TILE
    hid = w1.shape[3]

    def row_map(i, te, nu):
        return (i, 0)

    def w_map(i, te, nu):
        return (layer, te[i], 0, 0)

    grid_spec = pltpu.PrefetchScalarGridSpec(
        num_scalar_prefetch=2,
        grid=(n_tiles,),
        in_specs=[pl.BlockSpec((EXPERT_TILE, d), row_map),
                  pl.BlockSpec((1, 1, d, hid), w_map),
                  pl.BlockSpec((1, 1, d, hid), w_map),
                  pl.BlockSpec((1, 1, hid, d), w_map)],
        out_specs=pl.BlockSpec((EXPERT_TILE, d), row_map),
        scratch_shapes=[pltpu.VMEM((d, hid), jnp.bfloat16), pltpu.VMEM((d, hid), jnp.bfloat16),
                        pltpu.VMEM((hid, d), jnp.bfloat16)],
    )
    return pl.pallas_call(
        _expert_kernel,
        name="moe_experts",
        grid_spec=grid_spec,
        out_shape=jax.ShapeDtypeStruct((n_rows, d), jnp.float32),
        compiler_params=pltpu.CompilerParams(dimension_semantics=("arbitrary",), vmem_limit_bytes=VMEM_LIMIT),
    )(tile_expert, n_used, x_disp, w1, w3, w2)


def _combine_kernel(dest_ref, dest_next_ref, x_ref, route_ref, g_ref, b_ref, yd_hbm, o_ref, ybuf, sem):
    i = pl.program_id(0)
    n = pl.num_programs(0)
    tm = x_ref.shape[0]
    rows = EXPERT_TOP_K * tm

    def start_tile(dref, slot):
        def body(r, c):
            for k in range(EXPERT_TOP_K):
                row = k * tm + r
                pltpu.make_async_copy(
                    yd_hbm.at[pl.ds(dref[0, 0, row], 1)], ybuf.at[slot, pl.ds(row, 1)], sem.at[slot]).start(priority=k)
            return c
        lax.fori_loop(0, tm, body, 0, unroll=8)

    slot = i % 2

    @pl.when(i == 0)
    def _():
        start_tile(dest_ref, 0)

    @pl.when(i + 1 < n)
    def _():
        start_tile(dest_next_ref, 1 - slot)

    pltpu.make_async_copy(yd_hbm.at[pl.ds(0, rows)], ybuf.at[slot], sem.at[slot]).wait()
    y1 = ybuf[slot, 0:tm, :]
    y2 = ybuf[slot, tm:rows, :]
    route = route_ref[...].T
    y = DN_ALPHA * x_ref[...] + (y1 * route[:, 0:1] + y2 * route[:, 1:2])
    o_ref[...] = _ln_rows(y, g_ref[...], b_ref[...])


def _moe_combine_ln(xt, y_disp, dest2, route, ln_g, ln_b):
    t, d = xt.shape
    n = t // ROUTE_TILE
    rows = EXPERT_TOP_K * ROUTE_TILE
    return pl.pallas_call(
        _combine_kernel,
        name="moe_combine_ln",
        grid=(n,),
        in_specs=[pl.BlockSpec((1, 1, rows), lambda i: (i, 0, 0), memory_space=pltpu.SMEM),
                  pl.BlockSpec((1, 1, rows), lambda i: (jnp.minimum(i + 1, n - 1), 0, 0), memory_space=pltpu.SMEM),
                  pl.BlockSpec((ROUTE_TILE, d), lambda i: (i, 0)),
                  pl.BlockSpec((ROUTE_ROWS, ROUTE_TILE), lambda i: (0, i)),
                  pl.BlockSpec((1, d), lambda i: (0, 0)),
                  pl.BlockSpec((1, d), lambda i: (0, 0)),
                  pl.BlockSpec(memory_space=pl.ANY)],
        out_specs=pl.BlockSpec((ROUTE_TILE, d), lambda i: (i, 0)),
        out_shape=jax.ShapeDtypeStruct((t, d), jnp.float32),
        scratch_shapes=[pltpu.VMEM((2, rows, d), jnp.float32), pltpu.SemaphoreType.DMA((2,))],
        compiler_params=pltpu.CompilerParams(dimension_semantics=("arbitrary",), vmem_limit_bytes=VMEM_LIMIT),
    )(dest2, dest2, xt, route, ln_g.reshape(1, d), ln_b.reshape(1, d), y_disp)


def _moe_sublayer(x, route, cnt, w1, w3, w2, layer, ln_g, ln_b):
    b, s, d = x.shape
    t = b * s
    xt = x.reshape(t, d)
    counts = cnt[N_GROUPS:N_GROUPS + N_EXPERTS, 0].astype(jnp.int32)
    n_tiles = (t * EXPERT_TOP_K) // EXPERT_TILE + N_EXPERTS
    tiles_per = (counts + EXPERT_TILE - 1) // EXPERT_TILE
    tile_end = jnp.cumsum(tiles_per)
    pad_start = (tile_end - tiles_per) * EXPERT_TILE
    n_used = tile_end[-1:].astype(jnp.int32)
    tile_ids = jnp.minimum(jnp.arange(n_tiles), n_used[0] - 1)
    tile_expert = jnp.sum(tile_ids[:, None] >= tile_end[None, :], axis=1).astype(jnp.int32)
    experts = route[2:4].astype(jnp.int32)
    first_row = jnp.sum(jnp.where(experts[..., None] == jnp.arange(N_EXPERTS), pad_start, 0), axis=-1)
    dest = first_row + route[4:6].astype(jnp.int32)
    dest2 = jnp.swapaxes(dest.reshape(EXPERT_TOP_K, t // ROUTE_TILE, ROUTE_TILE), 0, 1)
    dest2 = dest2.reshape(t // ROUTE_TILE, 1, EXPERT_TOP_K * ROUTE_TILE)
    last_tiles = jnp.where(tiles_per > 0, tile_end - 1, -1)
    tail_tiles = n_used[0] + jnp.arange(N_EXPERTS)
    tail_tiles = jnp.where(tail_tiles < n_tiles, tail_tiles, -1)
    zero_tiles = jnp.concatenate([last_tiles, tail_tiles]).astype(jnp.int32)
    x_disp = _moe_dispatch(xt, dest2, zero_tiles, n_tiles * EXPERT_TILE)
    y_disp = _moe_experts(x_disp, tile_expert, n_used, w1, w3, w2, layer)
    return _moe_combine_ln(xt, y_disp, dest2, route, ln_g, ln_b).reshape(b, s, d)


def kernel(x, mem, mem_wk, mem_wv, ev_w_in, ev_conv_w, ev_conv_b, ev_cnorm_g, ev_cnorm_b, ev_cmp_pe_k, ev_cmp_w1_k, ev_cmp_w2_k, ev_cmp_pe_v, ev_cmp_w1_v, ev_cmp_w2_v, ev_w_out, od_w_in, od_conv_w, od_w_out, ln_mix_g, ln_mix_b, xa_wq, xa_wo, ln_xa_g, ln_xa_b, moe_wg, moe_bg, moe_we, moe_be, moe_w1, moe_w3, moe_w2, ln_ffn_g, ln_ffn_b):
    b, s, d = x.shape
    mem_k, mem_v = _mem_kv(mem, mem_wk, mem_wv)
    for layer in range(DEPTH):
        i = layer // 2
        if layer % 2 == 0:
            a, kv_in, k_sel, k_win, q_t, v_sel_t, v_win_t, gate_t = _even_in_proj(x, ev_w_in[i])
            a = _conformer_conv(a, ev_conv_w[i], ev_conv_b[i], ev_cnorm_g[i], ev_cnorm_b[i])
            k_cmp, v_cmp_t = _compress_kv(kv_in, ev_cmp_pe_k[i], ev_cmp_w1_k[i], ev_cmp_w2_k[i], ev_cmp_pe_v[i], ev_cmp_w1_v[i], ev_cmp_w2_v[i])
            o = _nsa_attention(q_t, gate_t, k_cmp, v_cmp_t, k_sel, v_sel_t, k_win, v_win_t)
            x = _proj_residual_ln(a.reshape(b * s, -1), o.reshape(b * s, -1), x.reshape(b * s, d), ev_w_out[i], ln_mix_g[layer], ln_mix_b[layer]).reshape(b, s, d)
        else:
            x = _odd_mixer_sublayer(x, od_w_in[i], od_conv_w[i], od_w_out[i], ln_mix_g[layer], ln_mix_b[layer])
        x, route, cnt = _xattn_router_sublayer(x, mem_k, mem_v, xa_wq[layer], xa_wo[layer], ln_xa_g[layer], ln_xa_b[layer],
                                               moe_wg[layer], moe_bg[layer], moe_we[layer], moe_be[layer])
        x = _moe_sublayer(x, route, cnt, moe_w1, moe_w3, moe_w2, layer, ln_ffn_g[layer], ln_ffn_b[layer])
    return x
```

```python
import functools

import numpy as np
import jax
import jax.numpy as jnp
from jax import lax
from jax.experimental import pallas as pl
from jax.experimental.pallas import tpu as pltpu

D_MODEL = 1024
DEPTH = 2
CONV_CH = D_MODEL // 2
CONV_WIDTH = 31
NSA_HEADS = 8
NSA_KV_HEADS = 2
HEAD_DIM = (D_MODEL // 2) // NSA_HEADS
CMP_BLOCK = 32
CMP_STRIDE = 16
SEL_BLOCK = 64
SEL_TOP_N = 16
WINDOW = 512
Q_BLOCK = 256
FORCE_SCORE = 1e4
SHORT_CONV_WIDTH = 3
XA_HEADS = 4
XA_HEAD_DIM = D_MODEL // XA_HEADS
N_GROUPS = 4
EXPERTS_PER_GROUP = 8
N_EXPERTS = N_GROUPS * EXPERTS_PER_GROUP
EXPERT_TOP_K = 2
DN_ALPHA = (2 * DEPTH) ** 0.25
LN_EPS = 1e-5
NEG_INF = -1e30
KV_COLS = NSA_KV_HEADS * HEAD_DIM
QCOLS = NSA_HEADS * HEAD_DIM
GATE_ROWS = 3 * NSA_HEADS

LANES = 128
SUBLANES = 8
HEADS_PER_KV = NSA_HEADS // NSA_KV_HEADS
QL = Q_BLOCK * HEADS_PER_KV
SEL_TILE = 1024
WIN_SPAN = WINDOW + Q_BLOCK
BLOCKS_PER_TILE = SEL_TILE // SEL_BLOCK
SEQ_TILE = 1024
HALO = 32
ODD_HALO = 8
ROUTE_TILE = 512
EXPERT_TILE = 512
ROUTE_ROWS = 8
VMEM_LIMIT = 56 * 1024 * 1024

_NT = (((1,), (1,)), ((), ()))


def _ln_rows(y, g, b):
    mu = jnp.mean(y, axis=-1, keepdims=True)
    yc = y - mu
    var = jnp.mean(yc * yc, axis=-1, keepdims=True)
    return yc * lax.rsqrt(var + LN_EPS) * g + b


def _even_in_kernel(x_ref, wa_ref, wkv_ref, wk2_ref, wqT_ref, wvT_ref, wgT_ref,
                    a_ref, kv_ref, ks_ref, kw_ref, qT_ref, vsT_ref, vwT_ref, gT_ref):
    f32 = jnp.float32
    bf16 = jnp.bfloat16
    xb = x_ref[0].astype(bf16)
    av = jnp.dot(xb, wa_ref[...], preferred_element_type=f32)
    a_ref[0] = av[:, :CONV_CH] * jax.nn.sigmoid(av[:, CONV_CH:])
    kv_ref[0] = jnp.dot(xb, wkv_ref[...], preferred_element_type=f32)
    k2 = jnp.dot(xb, wk2_ref[...], preferred_element_type=f32)
    ks_ref[0] = k2[:, :KV_COLS].astype(bf16)
    kw_ref[0] = k2[:, KV_COLS:].astype(bf16)
    qT_ref[0] = lax.dot_general(wqT_ref[...], xb, _NT, preferred_element_type=f32).astype(bf16)
    vT = lax.dot_general(wvT_ref[...], xb, _NT, preferred_element_type=f32).astype(bf16)
    vsT_ref[0, 0] = vT[:KV_COLS]
    for j in range(SEQ_TILE // Q_BLOCK):
        vwT_ref[0, j] = vT[KV_COLS:, j * Q_BLOCK:(j + 1) * Q_BLOCK]
    gT_ref[0] = lax.dot_general(wgT_ref[...], xb, _NT, preferred_element_type=f32)


def _even_in_proj(x, w_in):
    b, s, d = x.shape
    bf16 = jnp.bfloat16
    c = np.cumsum((0, CONV_CH, CONV_CH, QCOLS, KV_COLS, KV_COLS, KV_COLS, KV_COLS, KV_COLS, KV_COLS, GATE_ROWS))
    col = lambda i, j: w_in[:, c[i]:c[j]]
    wa = col(0, 2).astype(bf16)
    wq_t = col(2, 3).T.astype(bf16)
    wkv = col(3, 5).astype(bf16)
    wk2 = jnp.concatenate([col(5, 6), col(7, 8)], axis=1).astype(bf16)
    wv_t = jnp.concatenate([col(6, 7), col(8, 9)], axis=1).T.astype(bf16)
    wg_t = col(9, 10).T.astype(bf16)
    ts = SEQ_TILE
    per_sel = SEL_TILE // ts
    full = lambda shape: pl.BlockSpec(shape, lambda i, j: (0,) * len(shape))
    return pl.pallas_call(
        _even_in_kernel,
        name="even_in_proj",
        grid=(b, s // ts),
        in_specs=[pl.BlockSpec((1, ts, d), lambda i, j: (i, j, 0)),
                  full(wa.shape), full(wkv.shape), full(wk2.shape), full(wq_t.shape), full(wv_t.shape), full(wg_t.shape)],
        out_specs=[pl.BlockSpec((1, ts, CONV_CH), lambda i, j: (i, j, 0)),
                   pl.BlockSpec((1, ts, 2 * KV_COLS), lambda i, j: (i, j, 0)),
                   pl.BlockSpec((1, ts, KV_COLS), lambda i, j: (i, j, 0)),
                   pl.BlockSpec((1, ts, KV_COLS), lambda i, j: (i, j, 0)),
                   pl.BlockSpec((1, QCOLS, ts), lambda i, j: (i, 0, j)),
                   pl.BlockSpec((1, 1, KV_COLS, ts), lambda i, j: (i, j // per_sel, 0, j % per_sel)),
                   pl.BlockSpec((1, ts // Q_BLOCK, KV_COLS, Q_BLOCK), lambda i, j: (i, j, 0, 0)),
                   pl.BlockSpec((1, GATE_ROWS, ts), lambda i, j: (i, 0, j))],
        out_shape=[jax.ShapeDtypeStruct((b, s, CONV_CH), jnp.float32),
                   jax.ShapeDtypeStruct((b, s, 2 * KV_COLS), jnp.float32),
                   jax.ShapeDtypeStruct((b, s, KV_COLS), bf16),
                   jax.ShapeDtypeStruct((b, s, KV_COLS), bf16),
                   jax.ShapeDtypeStruct((b, QCOLS, s), bf16),
                   jax.ShapeDtypeStruct((b, s // SEL_TILE, KV_COLS, SEL_TILE), bf16),
                   jax.ShapeDtypeStruct((b, s // Q_BLOCK, KV_COLS, Q_BLOCK), bf16),
                   jax.ShapeDtypeStruct((b, GATE_ROWS, s), jnp.float32)],
        compiler_params=pltpu.CompilerParams(dimension_semantics=("arbitrary", "arbitrary"), vmem_limit_bytes=VMEM_LIMIT),
    )(x, wa, wkv, wk2, wq_t, wv_t, wg_t)


def _conv_kernel(cur_ref, halo_ref, w_ref, cb_ref, g_ref, b_ref, o_ref, ext_ref, win_ref):
    j = pl.program_id(1)
    ts = cur_ref.shape[1]
    halo = halo_ref[0]
    ext_ref[0:HALO, :] = jnp.where(j > 0, halo, jnp.zeros_like(halo))
    ext_ref[HALO:HALO + ts, :] = cur_ref[0]
    first = HALO - (CONV_WIDTH - 1)
    acc = jnp.zeros((ts, CONV_CH), jnp.float32)
    for p in range(SUBLANES):
        n_a = len(range(p, CONV_WIDTH, SUBLANES))
        rows = ts + SUBLANES * (n_a - 1)
        win_ref[0:rows, :] = ext_ref[first + p:first + p + rows, :]
        for a in range(n_a):
            k = SUBLANES * a + p
            acc = acc + w_ref[k:k + 1, :] * win_ref[SUBLANES * a:SUBLANES * a + ts, :]
    y = _ln_rows(acc + cb_ref[...], g_ref[...], b_ref[...])
    o_ref[0] = (y * jax.nn.sigmoid(y)).astype(o_ref.dtype)


def _conformer_conv(a, conv_w, conv_b, cn_g, cn_b):
    b, s, c = a.shape
    ts = SEQ_TILE
    per = ts // HALO
    row = lambda v: v.reshape(1, c)
    return pl.pallas_call(
        _conv_kernel,
        name="conformer_conv",
        grid=(b, s // ts),
        in_specs=[pl.BlockSpec((1, ts, c), lambda i, j: (i, j, 0)),
                  pl.BlockSpec((1, HALO, c), lambda i, j: (i, jnp.maximum(j * per - 1, 0), 0)),
                  pl.BlockSpec((CONV_WIDTH, c), lambda i, j: (0, 0)),
                  pl.BlockSpec((1, c), lambda i, j: (0, 0)),
                  pl.BlockSpec((1, c), lambda i, j: (0, 0)),
                  pl.BlockSpec((1, c), lambda i, j: (0, 0))],
        out_specs=pl.BlockSpec((1, ts, c), lambda i, j: (i, j, 0)),
        out_shape=jax.ShapeDtypeStruct((b, s, c), jnp.bfloat16),
        scratch_shapes=[pltpu.VMEM((HALO + ts, c), jnp.float32), pltpu.VMEM((HALO + ts, c), jnp.float32)],
        compiler_params=pltpu.CompilerParams(dimension_semantics=("arbitrary", "arbitrary")),
    )(a, a, conv_w, row(conv_b), row(cn_g), row(cn_b))


def _compress_kernel(r_ref, pe_ref, w1_ref, w2_ref, w2T_ref, o_ref, oT_ref):
    f32 = jnp.float32
    bf16 = jnp.bfloat16
    nch = o_ref.shape[1]
    hw = w1_ref.shape[3]
    first = jnp.zeros((nch, hw), f32)
    second = jnp.zeros((nch, hw), f32)
    for l in range(CMP_STRIDE):
        tok = r_ref[pl.ds(l, nch, stride=CMP_STRIDE), :]
        first = first + jnp.dot((tok + pe_ref[0, l:l + 1, :]).astype(bf16), w1_ref[0, l], preferred_element_type=f32)
        second = second + jnp.dot((tok + pe_ref[0, CMP_STRIDE + l:CMP_STRIDE + l + 1, :]).astype(bf16),
                                  w1_ref[0, CMP_STRIDE + l], preferred_element_type=f32)
    second_next = jnp.concatenate([second[1:], jnp.zeros_like(second[0:1])], axis=0)
    hid = jax.nn.gelu(first + second_next).astype(bf16)
    o_ref[0] = jnp.dot(hid, w2_ref[0], preferred_element_type=f32).astype(bf16)
    oT_ref[0] = lax.dot_general(w2T_ref[0], hid, _NT, preferred_element_type=f32).astype(bf16)


def _compress_kv(kv_in, pe_k, w1_k, w2_k, pe_v, w1_v, w2_v):
    b, s, _ = kv_in.shape
    bf16 = jnp.bfloat16
    nch = s // CMP_STRIDE
    g = NSA_KV_HEADS
    hidden = w1_k.shape[1]

    def expand(pe, w1, w2):
        same = jnp.eye(g, dtype=bool)
        w1r = w1.reshape(CMP_BLOCK, HEAD_DIM, hidden)
        w1e = jnp.where(same[None, :, None, :, None], w1r[:, None, :, None, :], 0.0)
        w2e = jnp.where(same[:, None, :, None], w2[None, :, None, :], 0.0)
        return jnp.tile(pe, (1, g)), w1e.reshape(CMP_BLOCK, KV_COLS, g * hidden), w2e.reshape(g * hidden, KV_COLS)

    pk, w1k, w2k = expand(pe_k, w1_k, w2_k)
    pv, w1v, w2v = expand(pe_v, w1_v, w2_v)
    pe = jnp.stack([pk, pv])
    w1 = jnp.stack([w1k, w1v]).astype(bf16)
    w2 = jnp.stack([w2k, w2v]).astype(bf16)
    w2t = jnp.swapaxes(w2, 1, 2)
    o, o_t = pl.pallas_call(
        _compress_kernel,
        name="compress_kv",
        grid=(b, 2),
        in_specs=[pl.BlockSpec((s, KV_COLS), lambda i, j: (i, j)),
                  pl.BlockSpec((1,) + pe.shape[1:], lambda i, j: (j, 0, 0)),
                  pl.BlockSpec((1,) + w1.shape[1:], lambda i, j: (j, 0, 0, 0)),
                  pl.BlockSpec((1,) + w2.shape[1:], lambda i, j: (j, 0, 0)),
                  pl.BlockSpec((1,) + w2t.shape[1:], lambda i, j: (j, 0, 0))],
        out_specs=[pl.BlockSpec((1, nch, KV_COLS), lambda i, j: (2 * i + j, 0, 0)),
                   pl.BlockSpec((1, KV_COLS, nch), lambda i, j: (2 * i + j, 0, 0))],
        out_shape=[jax.ShapeDtypeStruct((b * 2, nch, KV_COLS), bf16), jax.ShapeDtypeStruct((b * 2, KV_COLS, nch), bf16)],
        compiler_params=pltpu.CompilerParams(dimension_semantics=("arbitrary", "arbitrary"), vmem_limit_bytes=VMEM_LIMIT),
    )(kv_in.reshape(b * s, 2 * KV_COLS), pe, w1, w2, w2t)
    return o.reshape(b, 2, nch, KV_COLS)[:, 0], o_t.reshape(b, 2, KV_COLS, nch)[:, 1]


def _proj_ln_kernel(a_ref, o_ref, x_ref, w_ref, g_ref, b_ref, y_ref):
    lhs = jnp.concatenate([a_ref[...], o_ref[...]], axis=1)
    mix = jnp.dot(lhs, w_ref[...], preferred_element_type=jnp.float32)
    y_ref[...] = _ln_rows(DN_ALPHA * x_ref[...] + mix, g_ref[...], b_ref[...])


def _proj_residual_ln(a, o, x, w, ln_g, ln_b):
    t, d = x.shape
    tm = SEQ_TILE
    return pl.pallas_call(
        _proj_ln_kernel,
        name="proj_residual_ln",
        grid=(t // tm,),
        in_specs=[pl.BlockSpec((tm, a.shape[1]), lambda i: (i, 0)),
                  pl.BlockSpec((tm, o.shape[1]), lambda i: (i, 0)),
                  pl.BlockSpec((tm, d), lambda i: (i, 0)),
                  pl.BlockSpec(w.shape, lambda i: (0, 0)),
                  pl.BlockSpec((1, d), lambda i: (0, 0)),
                  pl.BlockSpec((1, d), lambda i: (0, 0))],
        out_specs=pl.BlockSpec((tm, d), lambda i: (i, 0)),
        out_shape=jax.ShapeDtypeStruct((t, d), jnp.float32),
        compiler_params=pltpu.CompilerParams(dimension_semantics=("arbitrary",), vmem_limit_bytes=VMEM_LIMIT),
    )(a, o, x, w.astype(jnp.bfloat16), ln_g.reshape(1, d), ln_b.reshape(1, d))


def _mem_kv_kernel(m_ref, wk_ref, wv_ref, k_ref, v_ref):
    mb = m_ref[...].astype(jnp.bfloat16)
    k_ref[...] = jnp.dot(mb, wk_ref[...], preferred_element_type=jnp.float32).astype(jnp.bfloat16)
    v_ref[...] = jnp.dot(mb, wv_ref[...], preferred_element_type=jnp.float32).astype(jnp.bfloat16)


def _mem_kv(mem, wk, wv):
    b, m, d = mem.shape
    bf16 = jnp.bfloat16
    k, v = pl.pallas_call(
        _mem_kv_kernel,
        name="mem_kv",
        grid=(b,),
        in_specs=[pl.BlockSpec((m, d), lambda i: (i, 0)), pl.BlockSpec((d, d), lambda i: (0, 0)), pl.BlockSpec((d, d), lambda i: (0, 0))],
        out_specs=[pl.BlockSpec((m, d), lambda i: (i, 0)), pl.BlockSpec((m, d), lambda i: (i, 0))],
        out_shape=[jax.ShapeDtypeStruct((b * m, d), bf16), jax.ShapeDtypeStruct((b * m, d), bf16)],
        compiler_params=pltpu.CompilerParams(dimension_semantics=("arbitrary",), vmem_limit_bytes=VMEM_LIMIT),
    )(mem.reshape(b * m, d), wk.astype(bf16), wv.astype(bf16))
    return k.reshape(b, m, d), v.reshape(b, m, d)


def _xattn_kernel(x_ref, k_ref, v_ref, wq_ref, wo_ref, g_ref, b_ref, rw_ref, rb_ref, tri_ref,
                  y_ref, route_ref, cnt_ref, run_ref):
    f32 = jnp.float32
    bf16 = jnp.bfloat16
    x = x_ref[0]
    q = (jnp.dot(x.astype(bf16), wq_ref[...], preferred_element_type=f32).astype(bf16)
         * jnp.asarray(XA_HEAD_DIM ** -0.5, bf16))
    heads = []
    for h in range(XA_HEADS):
        cols = slice(h * XA_HEAD_DIM, (h + 1) * XA_HEAD_DIM)
        s = lax.dot_general(q[:, cols], k_ref[0, :, cols], _NT, preferred_element_type=f32)
        m = jnp.max(s, axis=1, keepdims=True)
        p = jnp.exp(s - m)
        p = p / jnp.sum(p, axis=1, keepdims=True)
        heads.append(jnp.dot(p.astype(bf16), v_ref[0, :, cols], preferred_element_type=f32).astype(bf16))
    att = jnp.concatenate(heads, axis=1)
    out = jnp.dot(att, wo_ref[...], preferred_element_type=f32)
    y = _ln_rows(DN_ALPHA * x + out, g_ref[...], b_ref[...])
    y_ref[0] = y
    first_step = (pl.program_id(0) == 0) & (pl.program_id(1) == 0)
    _route_tile(y.astype(bf16), first_step, rw_ref, rb_ref, tri_ref, route_ref, cnt_ref, run_ref)


def _xattn_router_sublayer(x, mem_k, mem_v, wq, wo, ln_g, ln_b, wg, bg, we, be):
    b, s, d = x.shape
    m = mem_k.shape[1]
    ts = SEQ_TILE
    per_seq = s // ts
    bf16 = jnp.bfloat16
    rw, rb, tri = _router_operands(wg, bg, we, be, ts)
    const = lambda shape: pl.BlockSpec(shape, lambda i, j: (0,) * len(shape))
    return pl.pallas_call(
        _xattn_kernel,
        name="xattn_router_sublayer",
        grid=(b, per_seq),
        in_specs=[pl.BlockSpec((1, ts, d), lambda i, j: (i, j, 0)),
                  pl.BlockSpec((1, m, d), lambda i, j: (i, 0, 0)),
                  pl.BlockSpec((1, m, d), lambda i, j: (i, 0, 0)),
                  const((d, d)), const((d, d)), const((1, d)), const((1, d)),
                  const(rw.shape), const(rb.shape), const(tri.shape)],
        out_specs=[pl.BlockSpec((1, ts, d), lambda i, j: (i, j, 0)),
                   pl.BlockSpec((ROUTE_ROWS, ts), lambda i, j: (0, i * per_seq + j)),
                   const((LANES, 1))],
        out_shape=[jax.ShapeDtypeStruct((b, s, d), jnp.float32),
                   jax.ShapeDtypeStruct((ROUTE_ROWS, b * s), jnp.float32),
                   jax.ShapeDtypeStruct((LANES, 1), jnp.float32)],
        scratch_shapes=[pltpu.VMEM((LANES, 1), jnp.float32)],
        compiler_params=pltpu.CompilerParams(dimension_semantics=("arbitrary", "arbitrary"), vmem_limit_bytes=VMEM_LIMIT),
    )(x, mem_k, mem_v, wq.astype(bf16), wo.astype(bf16), ln_g.reshape(1, d), ln_b.reshape(1, d), rw, rb, tri)


def _odd_kernel(x_ref, halo_ref, wb_ref, wc_ref, wh_ref, cw_ref, wo_ref, g_ref, b_ref, y_ref):
    f32 = jnp.float32
    bf16 = jnp.bfloat16
    j = pl.program_id(1)
    ts = x_ref.shape[1]
    x = x_ref[0]
    xe = jnp.concatenate([halo_ref[0], x], axis=0).astype(bf16)
    u = (jnp.dot(xe, wc_ref[...], preferred_element_type=f32) * jnp.dot(xe, wh_ref[...], preferred_element_type=f32))
    row = lax.broadcasted_iota(jnp.int32, (ODD_HALO + ts, 1), 0)
    u = jnp.where((row >= ODD_HALO) | (j > 0), u, 0.0)
    conv = jnp.zeros((ts, u.shape[1]), f32)
    for k in range(SHORT_CONV_WIDTH):
        off = ODD_HALO - (SHORT_CONV_WIDTH - 1) + k
        conv = conv + cw_ref[k:k + 1, :] * u[off:off + ts, :]
    gate_b = jnp.dot(xe[ODD_HALO:], wb_ref[...], preferred_element_type=f32)
    mix = jnp.dot((gate_b * conv).astype(bf16), wo_ref[...], preferred_element_type=f32)
    y_ref[0] = _ln_rows(DN_ALPHA * x + mix, g_ref[...], b_ref[...])


def _odd_mixer_sublayer(x, w_in, conv_w, w_out, ln_g, ln_b):
    b, s, d = x.shape
    ts = SEQ_TILE
    per = ts // ODD_HALO
    bf16 = jnp.bfloat16
    wb, wc, wh = (w_in[:, i * d:(i + 1) * d].astype(bf16) for i in range(3))
    full = lambda shape: pl.BlockSpec(shape, lambda i, j: (0,) * len(shape))
    return pl.pallas_call(
        _odd_kernel,
        name="odd_mixer_sublayer",
        grid=(b, s // ts),
        in_specs=[pl.BlockSpec((1, ts, d), lambda i, j: (i, j, 0)),
                  pl.BlockSpec((1, ODD_HALO, d), lambda i, j: (i, jnp.maximum(j * per - 1, 0), 0)),
                  full((d, d)), full((d, d)), full((d, d)), full(conv_w.shape), full((d, d)), full((1, d)), full((1, d))],
        out_specs=pl.BlockSpec((1, ts, d), lambda i, j: (i, j, 0)),
        out_shape=jax.ShapeDtypeStruct((b, s, d), jnp.float32),
        compiler_params=pltpu.CompilerParams(dimension_semantics=("arbitrary", "arbitrary"), vmem_limit_bytes=VMEM_LIMIT),
    )(x, x, wb, wc, wh, conv_w, w_out.astype(bf16), ln_g.reshape(1, d), ln_b.reshape(1, d))


def _cmp_to_sel_matrix(n_cmp, n_sel):
    c0 = np.arange(n_cmp) * CMP_STRIDE
    s0 = np.arange(n_sel) * SEL_BLOCK
    ov = np.minimum(c0[:, None] + CMP_BLOCK, s0[None, :] + SEL_BLOCK) - np.maximum(c0[:, None], s0[None, :])
    return (np.clip(ov, 0, None) / CMP_BLOCK).astype(np.float32)


def _nsa_kernel(qT_ref, gT_ref, kc_ref, vcT_ref, mselT_ref, ks_ref, vsT_ref, kw_ref, vwT_ref, o_ref,
                score_ref, sel_ref, *, n_sel):
    c = pl.program_id(1)
    q0 = c * Q_BLOCK
    f32 = jnp.float32
    bf16 = jnp.bfloat16
    n_cmp_pad = kc_ref.shape[1]

    lane_q = lax.broadcasted_iota(jnp.int32, (1, QL), 1) % Q_BLOCK
    t_row = q0 + lane_q
    t_row_q = q0 + lax.broadcasted_iota(jnp.int32, (1, Q_BLOCK), 1)
    cur_q = t_row_q // SEL_BLOCK

    top_n = min(SEL_TOP_N, n_sel)
    m_iota = lax.broadcasted_iota(jnp.int32, (n_sel, Q_BLOCK), 0)
    forced = (m_iota == 0) | (m_iota == cur_q) | (m_iota == cur_q - 1)
    valid = m_iota <= cur_q
    n_comp = jnp.minimum(q0 // SEL_BLOCK + Q_BLOCK // SEL_BLOCK, n_sel)

    q_g, o_c = [], []
    for g in range(NSA_KV_HEADS):
        pieces = []
        for hg in range(HEADS_PER_KV):
            h = g * HEADS_PER_KV + hg
            qh = qT_ref[0, h * HEAD_DIM:(h + 1) * HEAD_DIM, :] * jnp.asarray(HEAD_DIM ** -0.5, bf16)
            z = jnp.zeros_like(qh)
            pieces.append(jnp.concatenate([qh, z] if g == 0 else [z, qh], axis=0))
        qTp = jnp.concatenate(pieces, axis=1)
        q_g.append(qTp)
        rows = slice(g * HEAD_DIM, (g + 1) * HEAD_DIM)

        s_c = jnp.dot(kc_ref[0], qTp, preferred_element_type=f32)
        n_iota = lax.broadcasted_iota(jnp.int32, (n_cmp_pad, QL), 0)
        mask_c = (n_iota * CMP_STRIDE + (CMP_BLOCK - 1)) <= t_row
        s_c = jnp.where(mask_c, s_c, NEG_INF)
        m_c = jnp.max(s_c, axis=0, keepdims=True)
        p_c = jnp.where(mask_c, jnp.exp(s_c - m_c), 0.0)
        l_c = jnp.sum(p_c, axis=0, keepdims=True)
        p_c = p_c * jnp.where(l_c > 0.0, 1.0 / l_c, 0.0)
        p_cb = p_c.astype(bf16)
        o_c.append(jnp.dot(vcT_ref[0], p_cb, preferred_element_type=f32)[rows])
        imp4 = jnp.dot(mselT_ref[...], p_cb, preferred_element_type=f32)
        imp = imp4[:, 0:Q_BLOCK]
        for hg in range(1, HEADS_PER_KV):
            imp = imp + imp4[:, hg * Q_BLOCK:(hg + 1) * Q_BLOCK]

        score = jnp.where(valid, jnp.where(forced, FORCE_SCORE, imp), NEG_INF)
        score_ref[...] = score

        def rank_body(i8, rank, score=score):
            base = pl.multiple_of(i8 * SUBLANES, SUBLANES)
            rows = score_ref[pl.ds(base, SUBLANES), :]
            for u in range(SUBLANES):
                row = rows[u:u + 1, :]
                beats = (row > score) | ((row == score) & (base + u < m_iota))
                rank = rank + beats.astype(jnp.int32)
            return rank

        rank = lax.fori_loop(0, (n_comp + SUBLANES - 1) // SUBLANES, rank_body,
                             jnp.zeros((n_sel, Q_BLOCK), jnp.int32))
        sel_ref[g] = jnp.where((rank < top_n) & valid, 0.0, NEG_INF)

    def sel_tile(j, carry, causal, size=SEL_TILE):
        k_tile = ks_ref[0, j, 0:size, :]
        v_tile = vsT_ref[0, j, :, 0:size]
        blk0 = pl.multiple_of(j * BLOCKS_PER_TILE, BLOCKS_PER_TILE)
        out = []
        for g in range(NSA_KV_HEADS):
            m_i, l_i, acc = carry[g]
            s = jnp.dot(k_tile, q_g[g], preferred_element_type=f32)
            selrows = sel_ref[g, pl.ds(blk0, BLOCKS_PER_TILE), :]
            bias = jnp.concatenate(
                [jnp.broadcast_to(selrows[r:r + 1, :], (SEL_BLOCK, Q_BLOCK)) for r in range(size // SEL_BLOCK)], axis=0)
            if causal:
                key = j * SEL_TILE + lax.broadcasted_iota(jnp.int32, (size, Q_BLOCK), 0)
                bias = jnp.where(key <= t_row_q, bias, NEG_INF)
            s = s + jnp.concatenate([bias] * HEADS_PER_KV, axis=1)
            m_new = jnp.maximum(m_i, jnp.max(s, axis=0, keepdims=True))
            alpha = jnp.exp(m_i - m_new)
            p = jnp.exp(s - m_new)
            l_new = alpha * l_i + jnp.sum(p, axis=0, keepdims=True)
            pv = jnp.dot(v_tile, p.astype(bf16), preferred_element_type=f32)
            out.append((m_new, l_new, alpha * acc + pv))
        return tuple(out)

    n_full = q0 // SEL_TILE
    init = tuple((jnp.full((1, QL), NEG_INF, f32), jnp.zeros((1, QL), f32), jnp.zeros((KV_COLS, QL), f32))
                 for _ in range(NSA_KV_HEADS))
    carry = lax.fori_loop(0, n_full, functools.partial(sel_tile, causal=False), init)
    carry = lax.cond(
        q0 % SEL_TILE + Q_BLOCK <= SEL_TILE // 2,
        functools.partial(sel_tile, n_full, causal=True, size=SEL_TILE // 2),
        functools.partial(sel_tile, n_full, causal=True, size=SEL_TILE),
        carry)

    start = pl.multiple_of(jnp.maximum(q0 - WINDOW, 0), Q_BLOCK)
    j0 = start // Q_BLOCK
    k_win = kw_ref[0, pl.ds(start, WIN_SPAN), :]
    key_w = start + lax.broadcasted_iota(jnp.int32, (WIN_SPAN, Q_BLOCK), 0)
    bias_w = jnp.where(key_w <= t_row_q, 0.0, NEG_INF)
    bias_w = jnp.where(key_w > t_row_q - WINDOW, bias_w, NEG_INF)
    bias_w = jnp.concatenate([bias_w] * HEADS_PER_KV, axis=1)
    for g in range(NSA_KV_HEADS):
        rows = slice(g * HEAD_DIM, (g + 1) * HEAD_DIM)
        m_s, l_s, acc_s = carry[g]
        o_s = acc_s[rows] * (1.0 / l_s)
        s_w = jnp.dot(k_win, q_g[g], preferred_element_type=f32) + bias_w
        m_w = jnp.max(s_w, axis=0, keepdims=True)
        p_w = jnp.exp(s_w - m_w)
        l_w = jnp.sum(p_w, axis=0, keepdims=True)
        p_wb = p_w.astype(bf16)
        acc_w = jnp.zeros((KV_COLS, QL), f32)
        for i in range(WIN_SPAN // Q_BLOCK):
            acc_w = acc_w + jnp.dot(vwT_ref[0, j0 + i], p_wb[i * Q_BLOCK:(i + 1) * Q_BLOCK, :], preferred_element_type=f32)
        o_w = acc_w[rows] * (1.0 / l_w)

        for pair in range(HEADS_PER_KV // 2):
            halves = []
            for hg in (2 * pair, 2 * pair + 1):
                h = g * HEADS_PER_KV + hg
                lanes = slice(hg * Q_BLOCK, (hg + 1) * Q_BLOCK)
                gate = jax.nn.sigmoid(gT_ref[0, 3 * h:3 * h + 3, :])
                halves.append(gate[0:1] * o_c[g][:, lanes] + gate[1:2] * o_s[:, lanes] + gate[2:3] * o_w[:, lanes])
            both = jnp.concatenate(halves, axis=0)
            col0 = (g * HEADS_PER_KV + 2 * pair) * HEAD_DIM
            o_ref[0, :, col0:col0 + 2 * HEAD_DIM] = both.T.astype(o_ref.dtype)


def _nsa_attention(qT, gT, kc, vcT, ks, vsT, kw, vwT):
    b, hd, s = qT.shape
    n_sel = s // SEL_BLOCK
    nc = kc.shape[1]
    n_cmp = (s - CMP_BLOCK) // CMP_STRIDE + 1
    mselT = jnp.asarray(np.pad(_cmp_to_sel_matrix(n_cmp, n_sel).T, ((0, 0), (0, nc - n_cmp))), jnp.bfloat16)
    ks4 = ks.reshape(b, s // SEL_TILE, SEL_TILE, KV_COLS)
    return pl.pallas_call(
        functools.partial(_nsa_kernel, n_sel=n_sel),
        name="nsa_attention",
        grid=(b, s // Q_BLOCK),
        in_specs=[
            pl.BlockSpec((1, hd, Q_BLOCK), lambda i, c: (i, 0, c)),
            pl.BlockSpec((1, 3 * NSA_HEADS, Q_BLOCK), lambda i, c: (i, 0, c)),
            pl.BlockSpec((1, nc, KV_COLS), lambda i, c: (i, 0, 0)),
            pl.BlockSpec((1, KV_COLS, nc), lambda i, c: (i, 0, 0)),
            pl.BlockSpec((n_sel, nc), lambda i, c: (0, 0)),
            pl.BlockSpec((1, s // SEL_TILE, SEL_TILE, KV_COLS), lambda i, c: (i, 0, 0, 0)),
            pl.BlockSpec((1, s // SEL_TILE, KV_COLS, SEL_TILE), lambda i, c: (i, 0, 0, 0)),
            pl.BlockSpec((1, s, KV_COLS), lambda i, c: (i, 0, 0)),
            pl.BlockSpec((1, s // Q_BLOCK, KV_COLS, Q_BLOCK), lambda i, c: (i, 0, 0, 0)),
        ],
        out_specs=pl.BlockSpec((1, Q_BLOCK, hd), lambda i, c: (i, c, 0)),
        out_shape=jax.ShapeDtypeStruct((b, s, hd), jnp.bfloat16),
        scratch_shapes=[pltpu.VMEM((n_sel, Q_BLOCK), jnp.float32),
                        pltpu.VMEM((NSA_KV_HEADS, n_sel, Q_BLOCK), jnp.float32)],
        compiler_params=pltpu.CompilerParams(dimension_semantics=("arbitrary", "arbitrary")),
    )(qT, gT, kc, vcT, mselT, ks4, vsT, kw, vwT)


def _route_tile(xb, first_step, wT_ref, b_ref, tri_ref, route_ref, cnt_ref, run_ref):
    f32 = jnp.float32

    @pl.when(first_step)
    def _():
        run_ref[...] = jnp.zeros_like(run_ref)

    tm = xb.shape[0]
    logits = lax.dot_general(wT_ref[...], xb, _NT, preferred_element_type=f32) + b_ref[...]
    sub = lax.broadcasted_iota(jnp.int32, (LANES, tm), 0)
    is_g = sub < N_GROUPS
    gl = jnp.where(is_g, logits, NEG_INF)
    g_max = jnp.max(gl, axis=0, keepdims=True)
    g_star = jnp.min(jnp.where(gl == g_max, sub, LANES), axis=0, keepdims=True)
    p_group = 1.0 / jnp.sum(jnp.where(is_g, jnp.exp(gl - g_max), 0.0), axis=0, keepdims=True)
    lo = N_GROUPS + g_star * EXPERTS_PER_GROUP
    in_grp = (sub >= lo) & (sub < lo + EXPERTS_PER_GROUP)
    el = jnp.where(in_grp, logits, NEG_INF)
    v1 = jnp.max(el, axis=0, keepdims=True)
    i1 = jnp.min(jnp.where(el == v1, sub, LANES), axis=0, keepdims=True)
    el2 = jnp.where(sub == i1, NEG_INF, el)
    v2 = jnp.max(el2, axis=0, keepdims=True)
    i2 = jnp.min(jnp.where(el2 == v2, sub, LANES), axis=0, keepdims=True)
    e21 = jnp.exp(v2 - v1)
    gate1 = p_group * (1.0 / (1.0 + e21))
    gate2 = p_group * (e21 / (1.0 + e21))
    oh1 = (sub == i1).astype(f32)
    oh2 = (sub == i2).astype(f32)
    both = oh1 + oh2
    before = jnp.dot(both.astype(jnp.bfloat16), tri_ref[...], preferred_element_type=f32) + run_ref[...]
    rank1 = jnp.sum(oh1 * before, axis=0, keepdims=True)
    rank2 = jnp.sum(oh2 * before, axis=0, keepdims=True)
    run_ref[...] = run_ref[...] + jnp.sum(both, axis=1, keepdims=True)
    cnt_ref[...] = run_ref[...]
    zero = jnp.zeros_like(gate1)
    route_ref[...] = jnp.concatenate(
        [gate1, gate2, (i1 - N_GROUPS).astype(f32), (i2 - N_GROUPS).astype(f32), rank1, rank2, zero, zero], axis=0)


def _router_operands(wg, bg, we, be, tile):
    d = wg.shape[0]
    pad = LANES - N_GROUPS - N_EXPERTS
    w_t = jnp.concatenate([wg, we.reshape(d, N_EXPERTS), jnp.zeros((d, pad), wg.dtype)], axis=1).T
    bias = jnp.concatenate([bg, be.reshape(N_EXPERTS), jnp.zeros((pad,), bg.dtype)])[:, None]
    tri = jnp.asarray(np.triu(np.ones((tile, tile), np.float32), 1), jnp.bfloat16)
    return w_t.astype(jnp.bfloat16), bias, tri


def _dispatch_kernel(zt_ref, dest_ref, x_ref, xd_hbm, zbuf, ring, sem, zsem):
    tm = x_ref.shape[0]

    @pl.when(pl.program_id(0) == 0)
    def _():
        zbuf[...] = jnp.zeros_like(zbuf)

        def zero_copy(k):
            start = pl.multiple_of(jnp.maximum(zt_ref[k], 0) * EXPERT_TILE, EXPERT_TILE)
            return pltpu.make_async_copy(zbuf, xd_hbm.at[pl.ds(start, EXPERT_TILE)], zsem)

        def start_body(k, c):
            @pl.when(zt_ref[k] >= 0)
            def _():
                zero_copy(k).start()
            return c

        def wait_body(k, c):
            @pl.when(zt_ref[k] >= 0)
            def _():
                zero_copy(k).wait()
            return c

        lax.fori_loop(0, zt_ref.shape[0], start_body, 0)
        lax.fori_loop(0, zt_ref.shape[0], wait_body, 0)

    i = pl.program_id(0)
    n = pl.num_programs(0)
    slot = i % 2

    def wait_tile(of_slot):
        for k in range(EXPERT_TOP_K):
            pltpu.make_async_copy(ring.at[of_slot], xd_hbm.at[pl.ds(0, tm)], sem.at[of_slot]).wait()

    @pl.when(i >= 2)
    def _():
        wait_tile(slot)

    ring[slot] = x_ref[...]

    def body(r, c):
        for k in range(EXPERT_TOP_K):
            pltpu.make_async_copy(
                ring.at[slot, pl.ds(r, 1)], xd_hbm.at[pl.ds(dest_ref[0, 0, k * tm + r], 1)], sem.at[slot]).start(priority=k)
        return c

    lax.fori_loop(0, tm, body, 0, unroll=8)

    @pl.when(i == n - 1)
    def _():
        wait_tile(slot)

        @pl.when(n > 1)
        def _():
            wait_tile(1 - slot)


def _moe_dispatch(xt, dest2, zero_tiles, n_rows):
    t, d = xt.shape
    grid_spec = pltpu.PrefetchScalarGridSpec(
        num_scalar_prefetch=1,
        grid=(t // ROUTE_TILE,),
        in_specs=[pl.BlockSpec((1, 1, EXPERT_TOP_K * ROUTE_TILE), lambda i, zt: (i, 0, 0), memory_space=pltpu.SMEM),
                  pl.BlockSpec((ROUTE_TILE, d), lambda i, zt: (i, 0))],
        out_specs=pl.BlockSpec(memory_space=pl.ANY),
        scratch_shapes=[pltpu.VMEM((EXPERT_TILE, d), xt.dtype), pltpu.VMEM((2, ROUTE_TILE, d), xt.dtype),
                        pltpu.SemaphoreType.DMA((2,)), pltpu.SemaphoreType.DMA(())],
    )
    return pl.pallas_call(
        _dispatch_kernel,
        name="moe_dispatch",
        grid_spec=grid_spec,
        out_shape=jax.ShapeDtypeStruct((n_rows, d), xt.dtype),
        compiler_params=pltpu.CompilerParams(dimension_semantics=("arbitrary",), has_side_effects=True),
    )(zero_tiles, dest2, xt)


def _expert_kernel(te_ref, nu_ref, xd_ref, w1_ref, w3_ref, w2_ref, y_ref, w1b, w3b, w2b):
    i = pl.program_id(0)
    used = i < nu_ref[0]
    new_expert = (i == 0) | (te_ref[i] != te_ref[jnp.maximum(i - 1, 0)])

    @pl.when(used & new_expert)
    def _():
        w1b[...] = w1_ref[0, 0].astype(jnp.bfloat16)
        w3b[...] = w3_ref[0, 0].astype(jnp.bfloat16)
        w2b[...] = w2_ref[0, 0].astype(jnp.bfloat16)

    @pl.when(used)
    def _():
        xb = xd_ref[...].astype(jnp.bfloat16)
        h1 = jnp.dot(xb, w1b[...], preferred_element_type=jnp.float32)
        h3 = jnp.dot(xb, w3b[...], preferred_element_type=jnp.float32)
        a = (h1 * jax.nn.sigmoid(h1) * h3).astype(jnp.bfloat16)
        y_ref[...] = jnp.dot(a, w2b[...], preferred_element_type=jnp.float32)

    @pl.when(jnp.logical_not(used))
    def _():
        y_ref[...] = jnp.zeros_like(y_ref)


def _moe_experts(x_disp, tile_expert, n_used, w1, w3, w2, layer):
    n_rows, d = x_disp.shape
    n_tiles = n_rows // EXPERT_TILE
    hid = w1.shape[3]

    def row_map(i, te, nu):
        return (i, 0)

    def w_map(i, te, nu):
        return (layer, te[i], 0, 0)

    grid_spec = pltpu.PrefetchScalarGridSpec(
        num_scalar_prefetch=2,
        grid=(n_tiles,),
        in_specs=[pl.BlockSpec((EXPERT_TILE, d), row_map),
                  pl.BlockSpec((1, 1, d, hid), w_map),
                  pl.BlockSpec((1, 1, d, hid), w_map),
                  pl.BlockSpec((1, 1, hid, d), w_map)],
        out_specs=pl.BlockSpec((EXPERT_TILE, d), row_map),
        scratch_shapes=[pltpu.VMEM((d, hid), jnp.bfloat16), pltpu.VMEM((d, hid), jnp.bfloat16),
                        pltpu.VMEM((hid, d), jnp.bfloat16)],
    )
    return pl.pallas_call(
        _expert_kernel,
        name="moe_experts",
        grid_spec=grid_spec,
        out_shape=jax.ShapeDtypeStruct((n_rows, d), jnp.float32),
        compiler_params=pltpu.CompilerParams(dimension_semantics=("arbitrary",), vmem_limit_bytes=VMEM_LIMIT),
    )(tile_expert, n_used, x_disp, w1, w3, w2)


def _combine_kernel(dest_ref, dest_next_ref, x_ref, route_ref, g_ref, b_ref, yd_hbm, o_ref, ybuf, sem):
    i = pl.program_id(0)
    n = pl.num_programs(0)
    tm = x_ref.shape[0]
    rows = EXPERT_TOP_K * tm

    def start_tile(dref, slot):
        def body(r, c):
            for k in range(EXPERT_TOP_K):
                row = k * tm + r
                pltpu.make_async_copy(
                    yd_hbm.at[pl.ds(dref[0, 0, row], 1)], ybuf.at[slot, pl.ds(row, 1)], sem.at[slot]).start(priority=k)
            return c
        lax.fori_loop(0, tm, body, 0, unroll=8)

    slot = i % 2

    @pl.when(i == 0)
    def _():
        start_tile(dest_ref, 0)

    @pl.when(i + 1 < n)
    def _():
        start_tile(dest_next_ref, 1 - slot)

    pltpu.make_async_copy(yd_hbm.at[pl.ds(0, rows)], ybuf.at[slot], sem.at[slot]).wait()
    y1 = ybuf[slot, 0:tm, :]
    y2 = ybuf[slot, tm:rows, :]
    route = route_ref[...].T
    y = DN_ALPHA * x_ref[...] + (y1 * route[:, 0:1] + y2 * route[:, 1:2])
    o_ref[...] = _ln_rows(y, g_ref[...], b_ref[...])


def _moe_combine_ln(xt, y_disp, dest2, route, ln_g, ln_b):
    t, d = xt.shape
    n = t // ROUTE_TILE
    rows = EXPERT_TOP_K * ROUTE_TILE
    return pl.pallas_call(
        _combine_kernel,
        name="moe_combine_ln",
        grid=(n,),
        in_specs=[pl.BlockSpec((1, 1, rows), lambda i: (i, 0, 0), memory_space=pltpu.SMEM),
                  pl.BlockSpec((1, 1, rows), lambda i: (jnp.minimum(i + 1, n - 1), 0, 0), memory_space=pltpu.SMEM),
                  pl.BlockSpec((ROUTE_TILE, d), lambda i: (i, 0)),
                  pl.BlockSpec((ROUTE_ROWS, ROUTE_TILE), lambda i: (0, i)),
                  pl.BlockSpec((1, d), lambda i: (0, 0)),
                  pl.BlockSpec((1, d), lambda i: (0, 0)),
                  pl.BlockSpec(memory_space=pl.ANY)],
        out_specs=pl.BlockSpec((ROUTE_TILE, d), lambda i: (i, 0)),
        out_shape=jax.ShapeDtypeStruct((t, d), jnp.float32),
        scratch_shapes=[pltpu.VMEM((2, rows, d), jnp.float32), pltpu.SemaphoreType.DMA((2,))],
        compiler_params=pltpu.CompilerParams(dimension_semantics=("arbitrary",), vmem_limit_bytes=VMEM_LIMIT),
    )(dest2, dest2, xt, route, ln_g.reshape(1, d), ln_b.reshape(1, d), y_disp)


def _moe_sublayer(x, route, cnt, w1, w3, w2, layer, ln_g, ln_b):
    b, s, d = x.shape
    t = b * s
    xt = x.reshape(t, d)
    counts = cnt[N_GROUPS:N_GROUPS + N_EXPERTS, 0].astype(jnp.int32)
    n_tiles = (t * EXPERT_TOP_K) // EXPERT_TILE + N_EXPERTS
    tiles_per = (counts + EXPERT_TILE - 1) // EXPERT_TILE
    tile_end = jnp.cumsum(tiles_per)
    pad_start = (tile_end - tiles_per) * EXPERT_TILE
    n_used = tile_end[-1:].astype(jnp.int32)
    tile_ids = jnp.minimum(jnp.arange(n_tiles), n_used[0] - 1)
    tile_expert = jnp.sum(tile_ids[:, None] >= tile_end[None, :], axis=1).astype(jnp.int32)
    experts = route[2:4].astype(jnp.int32)
    first_row = jnp.sum(jnp.where(experts[..., None] == jnp.arange(N_EXPERTS), pad_start, 0), axis=-1)
    dest = first_row + route[4:6].astype(jnp.int32)
    dest2 = jnp.swapaxes(dest.reshape(EXPERT_TOP_K, t // ROUTE_TILE, ROUTE_TILE), 0, 1)
    dest2 = dest2.reshape(t // ROUTE_TILE, 1, EXPERT_TOP_K * ROUTE_TILE)
    last_tiles = jnp.where(tiles_per > 0, tile_end - 1, -1)
    tail_tiles = n_used[0] + jnp.arange(N_EXPERTS)
    tail_tiles = jnp.where(tail_tiles < n_tiles, tail_tiles, -1)
    zero_tiles = jnp.concatenate([last_tiles, tail_tiles]).astype(jnp.int32)
    x_disp = _moe_dispatch(xt, dest2, zero_tiles, n_tiles * EXPERT_TILE)
    y_disp = _moe_experts(x_disp, tile_expert, n_used, w1, w3, w2, layer)
    return _moe_combine_ln(xt, y_disp, dest2, route, ln_g, ln_b).reshape(b, s, d)


def kernel(x, mem, mem_wk, mem_wv, ev_w_in, ev_conv_w, ev_conv_b, ev_cnorm_g, ev_cnorm_b, ev_cmp_pe_k, ev_cmp_w1_k, ev_cmp_w2_k, ev_cmp_pe_v, ev_cmp_w1_v, ev_cmp_w2_v, ev_w_out, od_w_in, od_conv_w, od_w_out, ln_mix_g, ln_mix_b, xa_wq, xa_wo, ln_xa_g, ln_xa_b, moe_wg, moe_bg, moe_we, moe_be, moe_w1, moe_w3, moe_w2, ln_ffn_g, ln_ffn_b):
    b, s, d = x.shape
    mem_k, mem_v = _mem_kv(mem, mem_wk, mem_wv)
    for layer in range(DEPTH):
        i = layer // 2
        if layer % 2 == 0:
            a, kv_in, k_sel, k_win, q_t, v_sel_t, v_win_t, gate_t = _even_in_proj(x, ev_w_in[i])
            a = _conformer_conv(a, ev_conv_w[i], ev_conv_b[i], ev_cnorm_g[i], ev_cnorm_b[i])
            k_cmp, v_cmp_t = _compress_kv(kv_in, ev_cmp_pe_k[i], ev_cmp_w1_k[i], ev_cmp_w2_k[i], ev_cmp_pe_v[i], ev_cmp_w1_v[i], ev_cmp_w2_v[i])
            o = _nsa_attention(q_t, gate_t, k_cmp, v_cmp_t, k_sel, v_sel_t, k_win, v_win_t)
            x = _proj_residual_ln(a.reshape(b * s, -1), o.reshape(b * s, -1), x.reshape(b * s, d), ev_w_out[i], ln_mix_g[layer], ln_mix_b[layer]).reshape(b, s, d)
        else:
            x = _odd_mixer_sublayer(x, od_w_in[i], od_conv_w[i], od_w_out[i], ln_mix_g[layer], ln_mix_b[layer])
        x, route, cnt = _xattn_router_sublayer(x, mem_k, mem_v, xa_wq[layer], xa_wo[layer], ln_xa_g[layer], ln_xa_b[layer],
                                               moe_wg[layer], moe_bg[layer], moe_we[layer], moe_be[layer])
        x = _moe_sublayer(x, route, cnt, moe_w1, moe_w3, moe_w2, layer, ln_ffn_g[layer], ln_ffn_b[layer])
    return x
```

```python
import functools

import numpy as np
import jax
import jax.numpy as jnp
from jax import lax
from jax.experimental import pallas as pl
from jax.experimental.pallas import tpu as pltpu

D_MODEL = 1024
DEPTH = 2
CONV_CH = D_MODEL // 2
CONV_WIDTH = 31
NSA_HEADS = 8
NSA_KV_HEADS = 2
HEAD_DIM = (D_MODEL // 2) // NSA_HEADS
CMP_BLOCK = 32
CMP_STRIDE = 16
SEL_BLOCK = 64
SEL_TOP_N = 16
WINDOW = 512
Q_BLOCK = 256
FORCE_SCORE = 1e4
SHORT_CONV_WIDTH = 3
XA_HEADS = 4
XA_HEAD_DIM = D_MODEL // XA_HEADS
N_GROUPS = 4
EXPERTS_PER_GROUP = 8
N_EXPERTS = N_GROUPS * EXPERTS_PER_GROUP
EXPERT_TOP_K = 2
DN_ALPHA = (2 * DEPTH) ** 0.25
LN_EPS = 1e-5
NEG_INF = -1e30
KV_COLS = NSA_KV_HEADS * HEAD_DIM
QCOLS = NSA_HEADS * HEAD_DIM
GATE_ROWS = 3 * NSA_HEADS

LANES = 128
SUBLANES = 8
HEADS_PER_KV = NSA_HEADS // NSA_KV_HEADS
QL = Q_BLOCK * HEADS_PER_KV
SEL_TILE = 1024
WIN_SPAN = WINDOW + Q_BLOCK
BLOCKS_PER_TILE = SEL_TILE // SEL_BLOCK
SEQ_TILE = 1024
HALO = 32
ODD_HALO = 8
ROUTE_TILE = 512
EXPERT_TILE = 512
ROUTE_ROWS = 8
VMEM_LIMIT = 56 * 1024 * 1024

_NT = (((1,), (1,)), ((), ()))


def _ln_rows(y, g, b):
    mu = jnp.mean(y, axis=-1, keepdims=True)
    yc = y - mu
    var = jnp.mean(yc * yc, axis=-1, keepdims=True)
    return yc * lax.rsqrt(var + LN_EPS) * g + b


def _even_in_kernel(x_ref, wa_ref, wkv_ref, wk2_ref, wqT_ref, wvT_ref, wgT_ref,
                    a_ref, kv_ref, ks_ref, kw_ref, qT_ref, vsT_ref, vwT_ref, gT_ref):
    f32 = jnp.float32
    bf16 = jnp.bfloat16
    xb = x_ref[0].astype(bf16)
    av = jnp.dot(xb, wa_ref[...], preferred_element_type=f32)
    a_ref[0] = av[:, :CONV_CH] * jax.nn.sigmoid(av[:, CONV_CH:])
    kv_ref[0] = jnp.dot(xb, wkv_ref[...], preferred_element_type=f32)
    k2 = jnp.dot(xb, wk2_ref[...], preferred_element_type=f32)
    ks_ref[0] = k2[:, :KV_COLS].astype(bf16)
    kw_ref[0] = k2[:, KV_COLS:].astype(bf16)
    qT_ref[0] = lax.dot_general(wqT_ref[...], xb, _NT, preferred_element_type=f32).astype(bf16)
    vT = lax.dot_general(wvT_ref[...], xb, _NT, preferred_element_type=f32).astype(bf16)
    vsT_ref[0, 0] = vT[:KV_COLS]
    for j in range(SEQ_TILE // Q_BLOCK):
        vwT_ref[0, j] = vT[KV_COLS:, j * Q_BLOCK:(j + 1) * Q_BLOCK]
    gT_ref[0] = lax.dot_general(wgT_ref[...], xb, _NT, preferred_element_type=f32)


def _even_in_proj(x, w_in):
    b, s, d = x.shape
    bf16 = jnp.bfloat16
    c = np.cumsum((0, CONV_CH, CONV_CH, QCOLS, KV_COLS, KV_COLS, KV_COLS, KV_COLS, KV_COLS, KV_COLS, GATE_ROWS))
    col = lambda i, j: w_in[:, c[i]:c[j]]
    wa = col(0, 2).astype(bf16)
    wq_t = col(2, 3).T.astype(bf16)
    wkv = col(3, 5).astype(bf16)
    wk2 = jnp.concatenate([col(5, 6), col(7, 8)], axis=1).astype(bf16)
    wv_t = jnp.concatenate([col(6, 7), col(8, 9)], axis=1).T.astype(bf16)
    wg_t = col(9, 10).T.astype(bf16)
    ts = SEQ_TILE
    per_sel = SEL_TILE // ts
    full = lambda shape: pl.BlockSpec(shape, lambda i, j: (0,) * len(shape))
    return pl.pallas_call(
        _even_in_kernel,
        name="even_in_proj",
        grid=(b, s // ts),
        in_specs=[pl.BlockSpec((1, ts, d), lambda i, j: (i, j, 0)),
                  full(wa.shape), full(wkv.shape), full(wk2.shape), full(wq_t.shape), full(wv_t.shape), full(wg_t.shape)],
        out_specs=[pl.BlockSpec((1, ts, CONV_CH), lambda i, j: (i, j, 0)),
                   pl.BlockSpec((1, ts, 2 * KV_COLS), lambda i, j: (i, j, 0)),
                   pl.BlockSpec((1, ts, KV_COLS), lambda i, j: (i, j, 0)),
                   pl.BlockSpec((1, ts, KV_COLS), lambda i, j: (i, j, 0)),
                   pl.BlockSpec((1, QCOLS, ts), lambda i, j: (i, 0, j)),
                   pl.BlockSpec((1, 1, KV_COLS, ts), lambda i, j: (i, j // per_sel, 0, j % per_sel)),
                   pl.BlockSpec((1, ts // Q_BLOCK, KV_COLS, Q_BLOCK), lambda i, j: (i, j, 0, 0)),
                   pl.BlockSpec((1, GATE_ROWS, ts), lambda i, j: (i, 0, j))],
        out_shape=[jax.ShapeDtypeStruct((b, s, CONV_CH), jnp.float32),
                   jax.ShapeDtypeStruct((b, s, 2 * KV_COLS), jnp.float32),
                   jax.ShapeDtypeStruct((b, s, KV_COLS), bf16),
                   jax.ShapeDtypeStruct((b, s, KV_COLS), bf16),
                   jax.ShapeDtypeStruct((b, QCOLS, s), bf16),
                   jax.ShapeDtypeStruct((b, s // SEL_TILE, KV_COLS, SEL_TILE), bf16),
                   jax.ShapeDtypeStruct((b, s // Q_BLOCK, KV_COLS, Q_BLOCK), bf16),
                   jax.ShapeDtypeStruct((b, GATE_ROWS, s), jnp.float32)],
        compiler_params=pltpu.CompilerParams(dimension_semantics=("arbitrary", "arbitrary"), vmem_limit_bytes=VMEM_LIMIT),
    )(x, wa, wkv, wk2, wq_t, wv_t, wg_t)


def _conv_kernel(cur_ref, halo_ref, w_ref, cb_ref, g_ref, b_ref, o_ref, ext_ref, win_ref):
    j = pl.program_id(1)
    ts = cur_ref.shape[1]
    halo = halo_ref[0]
    ext_ref[0:HALO, :] = jnp.where(j > 0, halo, jnp.zeros_like(halo))
    ext_ref[HALO:HALO + ts, :] = cur_ref[0]
    first = HALO - (CONV_WIDTH - 1)
    acc = jnp.zeros((ts, CONV_CH), jnp.float32)
    for p in range(SUBLANES):
        n_a = len(range(p, CONV_WIDTH, SUBLANES))
        rows = ts + SUBLANES * (n_a - 1)
        win_ref[0:rows, :] = ext_ref[first + p:first + p + rows, :]
        for a in range(n_a):
            k = SUBLANES * a + p
            acc = acc + w_ref[k:k + 1, :] * win_ref[SUBLANES * a:SUBLANES * a + ts, :]
    y = _ln_rows(acc + cb_ref[...], g_ref[...], b_ref[...])
    o_ref[0] = (y * jax.nn.sigmoid(y)).astype(o_ref.dtype)


def _conformer_conv(a, conv_w, conv_b, cn_g, cn_b):
    b, s, c = a.shape
    ts = SEQ_TILE
    per = ts // HALO
    row = lambda v: v.reshape(1, c)
    return pl.pallas_call(
        _conv_kernel,
        name="conformer_conv",
        grid=(b, s // ts),
        in_specs=[pl.BlockSpec((1, ts, c), lambda i, j: (i, j, 0)),
                  pl.BlockSpec((1, HALO, c), lambda i, j: (i, jnp.maximum(j * per - 1, 0), 0)),
                  pl.BlockSpec((CONV_WIDTH, c), lambda i, j: (0, 0)),
                  pl.BlockSpec((1, c), lambda i, j: (0, 0)),
                  pl.BlockSpec((1, c), lambda i, j: (0, 0)),
                  pl.BlockSpec((1, c), lambda i, j: (0, 0))],
        out_specs=pl.BlockSpec((1, ts, c), lambda i, j: (i, j, 0)),
        out_shape=jax.ShapeDtypeStruct((b, s, c), jnp.bfloat16),
        scratch_shapes=[pltpu.VMEM((HALO + ts, c), jnp.float32), pltpu.VMEM((HALO + ts, c), jnp.float32)],
        compiler_params=pltpu.CompilerParams(dimension_semantics=("arbitrary", "arbitrary")),
    )(a, a, conv_w, row(conv_b), row(cn_g), row(cn_b))


def _compress_kernel(r_ref, pe_ref, w1_ref, w2_ref, w2T_ref, o_ref, oT_ref):
    f32 = jnp.float32
    bf16 = jnp.bfloat16
    nch = o_ref.shape[1]
    hw = w1_ref.shape[3]
    first = jnp.zeros((nch, hw), f32)
    second = jnp.zeros((nch, hw), f32)
    for l in range(CMP_STRIDE):
        tok = r_ref[pl.ds(l, nch, stride=CMP_STRIDE), :]
        first = first + jnp.dot((tok + pe_ref[0, l:l + 1, :]).astype(bf16), w1_ref[0, l], preferred_element_type=f32)
        second = second + jnp.dot((tok + pe_ref[0, CMP_STRIDE + l:CMP_STRIDE + l + 1, :]).astype(bf16),
                                  w1_ref[0, CMP_STRIDE + l], preferred_element_type=f32)
    second_next = jnp.concatenate([second[1:], jnp.zeros_like(second[0:1])], axis=0)
    hid = jax.nn.gelu(first + second_next).astype(bf16)
    o_ref[0] = jnp.dot(hid, w2_ref[0], preferred_element_type=f32).astype(bf16)
    oT_ref[0] = lax.dot_general(w2T_ref[0], hid, _NT, preferred_element_type=f32).astype(bf16)


def _compress_kv(kv_in, pe_k, w1_k, w2_k, pe_v, w1_v, w2_v):
    b, s, _ = kv_in.shape
    bf16 = jnp.bfloat16
    nch = s // CMP_STRIDE
    g = NSA_KV_HEADS
    hidden = w1_k.shape[1]

    def expand(pe, w1, w2):
        same = jnp.eye(g, dtype=bool)
        w1r = w1.reshape(CMP_BLOCK, HEAD_DIM, hidden)
        w1e = jnp.where(same[None, :, None, :, None], w1r[:, None, :, None, :], 0.0)
        w2e = jnp.where(same[:, None, :, None], w2[None, :, None, :], 0.0)
        return jnp.tile(pe, (1, g)), w1e.reshape(CMP_BLOCK, KV_COLS, g * hidden), w2e.reshape(g * hidden, KV_COLS)

    pk, w1k, w2k = expand(pe_k, w1_k, w2_k)
    pv, w1v, w2v = expand(pe_v, w1_v, w2_v)
    pe = jnp.stack([pk, pv])
    w1 = jnp.stack([w1k, w1v]).astype(bf16)
    w2 = jnp.stack([w2k, w2v]).astype(bf16)
    w2t = jnp.swapaxes(w2, 1, 2)
    o, o_t = pl.pallas_call(
        _compress_kernel,
        name="compress_kv",
        grid=(b, 2),
        in_specs=[pl.BlockSpec((s, KV_COLS), lambda i, j: (i, j)),
                  pl.BlockSpec((1,) + pe.shape[1:], lambda i, j: (j, 0, 0)),
                  pl.BlockSpec((1,) + w1.shape[1:], lambda i, j: (j, 0, 0, 0)),
                  pl.BlockSpec((1,) + w2.shape[1:], lambda i, j: (j, 0, 0)),
                  pl.BlockSpec((1,) + w2t.shape[1:], lambda i, j: (j, 0, 0))],
        out_specs=[pl.BlockSpec((1, nch, KV_COLS), lambda i, j: (2 * i + j, 0, 0)),
                   pl.BlockSpec((1, KV_COLS, nch), lambda i, j: (2 * i + j, 0, 0))],
        out_shape=[jax.ShapeDtypeStruct((b * 2, nch, KV_COLS), bf16), jax.ShapeDtypeStruct((b * 2, KV_COLS, nch), bf16)],
        compiler_params=pltpu.CompilerParams(dimension_semantics=("arbitrary", "arbitrary"), vmem_limit_bytes=VMEM_LIMIT),
    )(kv_in.reshape(b * s, 2 * KV_COLS), pe, w1, w2, w2t)
    return o.reshape(b, 2, nch, KV_COLS)[:, 0], o_t.reshape(b, 2, KV_COLS, nch)[:, 1]


def _proj_ln_kernel(a_ref, o_ref, x_ref, w_ref, g_ref, b_ref, y_ref):
    lhs = jnp.concatenate([a_ref[...], o_ref[...]], axis=1)
    mix = jnp.dot(lhs, w_ref[...], preferred_element_type=jnp.float32)
    y_ref[...] = _ln_rows(DN_ALPHA * x_ref[...] + mix, g_ref[...], b_ref[...])


def _proj_residual_ln(a, o, x, w, ln_g, ln_b):
    t, d = x.shape
    tm = SEQ_TILE
    return pl.pallas_call(
        _proj_ln_kernel,
        name="proj_residual_ln",
        grid=(t // tm,),
        in_specs=[pl.BlockSpec((tm, a.shape[1]), lambda i: (i, 0)),
                  pl.BlockSpec((tm, o.shape[1]), lambda i: (i, 0)),
                  pl.BlockSpec((tm, d), lambda i: (i, 0)),
                  pl.BlockSpec(w.shape, lambda i: (0, 0)),
                  pl.BlockSpec((1, d), lambda i: (0, 0)),
                  pl.BlockSpec((1, d), lambda i: (0, 0))],
        out_specs=pl.BlockSpec((tm, d), lambda i: (i, 0)),
        out_shape=jax.ShapeDtypeStruct((t, d), jnp.float32),
        compiler_params=pltpu.CompilerParams(dimension_semantics=("arbitrary",), vmem_limit_bytes=VMEM_LIMIT),
    )(a, o, x, w.astype(jnp.bfloat16), ln_g.reshape(1, d), ln_b.reshape(1, d))


def _mem_kv_kernel(m_ref, wk_ref, wv_ref, k_ref, v_ref):
    mb = m_ref[...].astype(jnp.bfloat16)
    k_ref[...] = jnp.dot(mb, wk_ref[...], preferred_element_type=jnp.float32).astype(jnp.bfloat16)
    v_ref[...] = jnp.dot(mb, wv_ref[...], preferred_element_type=jnp.float32).astype(jnp.bfloat16)


def _mem_kv(mem, wk, wv):
    b, m, d = mem.shape
    bf16 = jnp.bfloat16
    k, v = pl.pallas_call(
        _mem_kv_kernel,
        name="mem_kv",
        grid=(b,),
        in_specs=[pl.BlockSpec((m, d), lambda i: (i, 0)), pl.BlockSpec((d, d), lambda i: (0, 0)), pl.BlockSpec((d, d), lambda i: (0, 0))],
        out_specs=[pl.BlockSpec((m, d), lambda i: (i, 0)), pl.BlockSpec((m, d), lambda i: (i, 0))],
        out_shape=[jax.ShapeDtypeStruct((b * m, d), bf16), jax.ShapeDtypeStruct((b * m, d), bf16)],
        compiler_params=pltpu.CompilerParams(dimension_semantics=("arbitrary",), vmem_limit_bytes=VMEM_LIMIT),
    )(mem.reshape(b * m, d), wk.astype(bf16), wv.astype(bf16))
    return k.reshape(b, m, d), v.reshape(b, m, d)


def _xattn_kernel(x_ref, k_ref, v_ref, wq_ref, wo_ref, g_ref, b_ref, rw_ref, rb_ref, tri_ref,
                  y_ref, route_ref, cnt_ref, run_ref):
    f32 = jnp.float32
    bf16 = jnp.bfloat16
    x = x_ref[0]
    q = (jnp.dot(x.astype(bf16), wq_ref[...], preferred_element_type=f32).astype(bf16)
         * jnp.asarray(XA_HEAD_DIM ** -0.5, bf16))
    heads = []
    for h in range(XA_HEADS):
        cols = slice(h * XA_HEAD_DIM, (h + 1) * XA_HEAD_DIM)
        s = lax.dot_general(q[:, cols], k_ref[0, :, cols], _NT, preferred_element_type=f32)
        m = jnp.max(s, axis=1, keepdims=True)
        p = jnp.exp(s - m)
        p = p / jnp.sum(p, axis=1, keepdims=True)
        heads.append(jnp.dot(p.astype(bf16), v_ref[0, :, cols], preferred_element_type=f32).astype(bf16))
    att = jnp.concatenate(heads, axis=1)
    out = jnp.dot(att, wo_ref[...], preferred_element_type=f32)
    y = _ln_rows(DN_ALPHA * x + out, g_ref[...], b_ref[...])
    y_ref[0] = y
    first_step = (pl.program_id(0) == 0) & (pl.program_id(1) == 0)
    _route_tile(y.astype(bf16), first_step, rw_ref, rb_ref, tri_ref, route_ref, cnt_ref, run_ref)


def _xattn_router_sublayer(x, mem_k, mem_v, wq, wo, ln_g, ln_b, wg, bg, we, be):
    b, s, d = x.shape
    m = mem_k.shape[1]
    ts = SEQ_TILE
    per_seq = s // ts
    bf16 = jnp.bfloat16
    rw, rb, tri = _router_operands(wg, bg, we, be, ts)
    const = lambda shape: pl.BlockSpec(shape, lambda i, j: (0,) * len(shape))
    return pl.pallas_call(
        _xattn_kernel,
        name="xattn_router_sublayer",
        grid=(b, per_seq),
        in_specs=[pl.BlockSpec((1, ts, d), lambda i, j: (i, j, 0)),
                  pl.BlockSpec((1, m, d), lambda i, j: (i, 0, 0)),
                  pl.BlockSpec((1, m, d), lambda i, j: (i, 0, 0)),
                  const((d, d)), const((d, d)), const((1, d)), const((1, d)),
                  const(rw.shape), const(rb.shape), const(tri.shape)],
        out_specs=[pl.BlockSpec((1, ts, d), lambda i, j: (i, j, 0)),
                   pl.BlockSpec((ROUTE_ROWS, ts), lambda i, j: (0, i * per_seq + j)),
                   const((LANES, 1))],
        out_shape=[jax.ShapeDtypeStruct((b, s, d), jnp.float32),
                   jax.ShapeDtypeStruct((ROUTE_ROWS, b * s), jnp.float32),
                   jax.ShapeDtypeStruct((LANES, 1), jnp.float32)],
        scratch_shapes=[pltpu.VMEM((LANES, 1), jnp.float32)],
        compiler_params=pltpu.CompilerParams(dimension_semantics=("arbitrary", "arbitrary"), vmem_limit_bytes=VMEM_LIMIT),
    )(x, mem_k, mem_v, wq.astype(bf16), wo.astype(bf16), ln_g.reshape(1, d), ln_b.reshape(1, d), rw, rb, tri)


def _odd_kernel(x_ref, halo_ref, wb_ref, wc_ref, wh_ref, cw_ref, wo_ref, g_ref, b_ref, y_ref):
    f32 = jnp.float32
    bf16 = jnp.bfloat16
    j = pl.program_id(1)
    ts = x_ref.shape[1]
    x = x_ref[0]
    xe = jnp.concatenate([halo_ref[0], x], axis=0).astype(bf16)
    u = (jnp.dot(xe, wc_ref[...], preferred_element_type=f32) * jnp.dot(xe, wh_ref[...], preferred_element_type=f32))
    row = lax.broadcasted_iota(jnp.int32, (ODD_HALO + ts, 1), 0)
    u = jnp.where((row >= ODD_HALO) | (j > 0), u, 0.0)
    conv = jnp.zeros((ts, u.shape[1]), f32)
    for k in range(SHORT_CONV_WIDTH):
        off = ODD_HALO - (SHORT_CONV_WIDTH - 1) + k
        conv = conv + cw_ref[k:k + 1, :] * u[off:off + ts, :]
    gate_b = jnp.dot(xe[ODD_HALO:], wb_ref[...], preferred_element_type=f32)
    mix = jnp.dot((gate_b * conv).astype(bf16), wo_ref[...], preferred_element_type=f32)
    y_ref[0] = _ln_rows(DN_ALPHA * x + mix, g_ref[...], b_ref[...])


def _odd_mixer_sublayer(x, w_in, conv_w, w_out, ln_g, ln_b):
    b, s, d = x.shape
    ts = SEQ_TILE
    per = ts // ODD_HALO
    bf16 = jnp.bfloat16
    wb, wc, wh = (w_in[:, i * d:(i + 1) * d].astype(bf16) for i in range(3))
    full = lambda shape: pl.BlockSpec(shape, lambda i, j: (0,) * len(shape))
    return pl.pallas_call(
        _odd_kernel,
        name="odd_mixer_sublayer",
        grid=(b, s // ts),
        in_specs=[pl.BlockSpec((1, ts, d), lambda i, j: (i, j, 0)),
                  pl.BlockSpec((1, ODD_HALO, d), lambda i, j: (i, jnp.maximum(j * per - 1, 0), 0)),
                  full((d, d)), full((d, d)), full((d, d)), full(conv_w.shape), full((d, d)), full((1, d)), full((1, d))],
        out_specs=pl.BlockSpec((1, ts, d), lambda i, j: (i, j, 0)),
        out_shape=jax.ShapeDtypeStruct((b, s, d), jnp.float32),
        compiler_params=pltpu.CompilerParams(dimension_semantics=("arbitrary", "arbitrary"), vmem_limit_bytes=VMEM_LIMIT),
    )(x, x, wb, wc, wh, conv_w, w_out.astype(bf16), ln_g.reshape(1, d), ln_b.reshape(1, d))


def _cmp_to_sel_matrix(n_cmp, n_sel):
    c0 = np.arange(n_cmp) * CMP_STRIDE
    s0 = np.arange(n_sel) * SEL_BLOCK
    ov = np.minimum(c0[:, None] + CMP_BLOCK, s0[None, :] + SEL_BLOCK) - np.maximum(c0[:, None], s0[None, :])
    return (np.clip(ov, 0, None) / CMP_BLOCK).astype(np.float32)


def _nsa_kernel(qT_ref, gT_ref, kc_ref, vcT_ref, mselT_ref, ks_ref, vsT_ref, kw_ref, vwT_ref, o_ref,
                score_ref, sel_ref, *, n_sel):
    c = pl.program_id(1)
    q0 = c * Q_BLOCK
    f32 = jnp.float32
    bf16 = jnp.bfloat16
    n_cmp_pad = kc_ref.shape[1]

    lane_q = lax.broadcasted_iota(jnp.int32, (1, QL), 1) % Q_BLOCK
    t_row = q0 + lane_q
    t_row_q = q0 + lax.broadcasted_iota(jnp.int32, (1, Q_BLOCK), 1)
    cur_q = t_row_q // SEL_BLOCK

    top_n = min(SEL_TOP_N, n_sel)
    m_iota = lax.broadcasted_iota(jnp.int32, (n_sel, Q_BLOCK), 0)
    forced = (m_iota == 0) | (m_iota == cur_q) | (m_iota == cur_q - 1)
    valid = m_iota <= cur_q
    n_comp = jnp.minimum(q0 // SEL_BLOCK + Q_BLOCK // SEL_BLOCK, n_sel)

    q_g, o_c = [], []
    for g in range(NSA_KV_HEADS):
        pieces = []
        for hg in range(HEADS_PER_KV):
            h = g * HEADS_PER_KV + hg
            qh = qT_ref[0, h * HEAD_DIM:(h + 1) * HEAD_DIM, :] * jnp.asarray(HEAD_DIM ** -0.5, bf16)
            z = jnp.zeros_like(qh)
            pieces.append(jnp.concatenate([qh, z] if g == 0 else [z, qh], axis=0))
        qTp = jnp.concatenate(pieces, axis=1)
        q_g.append(qTp)
        rows = slice(g * HEAD_DIM, (g + 1) * HEAD_DIM)

        s_c = jnp.dot(kc_ref[0], qTp, preferred_element_type=f32)
        n_iota = lax.broadcasted_iota(jnp.int32, (n_cmp_pad, QL), 0)
        mask_c = (n_iota * CMP_STRIDE + (CMP_BLOCK - 1)) <= t_row
        s_c = jnp.where(mask_c, s_c, NEG_INF)
        m_c = jnp.max(s_c, axis=0, keepdims=True)
        p_c = jnp.where(mask_c, jnp.exp(s_c - m_c), 0.0)
        l_c = jnp.sum(p_c, axis=0, keepdims=True)
        p_c = p_c * jnp.where(l_c > 0.0, 1.0 / l_c, 0.0)
        p_cb = p_c.astype(bf16)
        o_c.append(jnp.dot(vcT_ref[0], p_cb, preferred_element_type=f32)[rows])
        imp4 = jnp.dot(mselT_ref[...], p_cb, preferred_element_type=f32)
        imp = imp4[:, 0:Q_BLOCK]
        for hg in range(1, HEADS_PER_KV):
            imp = imp + imp4[:, hg * Q_BLOCK:(hg + 1) * Q_BLOCK]

        score = jnp.where(valid, jnp.where(forced, FORCE_SCORE, imp), NEG_INF)
        score_ref[...] = score

        def rank_body(i8, rank, score=score):
            base = pl.multiple_of(i8 * SUBLANES, SUBLANES)
            rows = score_ref[pl.ds(base, SUBLANES), :]
            for u in range(SUBLANES):
                row = rows[u:u + 1, :]
                beats = (row > score) | ((row == score) & (base + u < m_iota))
                rank = rank + beats.astype(jnp.int32)
            return rank

        rank = lax.fori_loop(0, (n_comp + SUBLANES - 1) // SUBLANES, rank_body,
                             jnp.zeros((n_sel, Q_BLOCK), jnp.int32))
        sel_ref[g] = jnp.where((rank < top_n) & valid, 0.0, NEG_INF)

    def sel_tile(j, carry, causal, size=SEL_TILE):
        k_tile = ks_ref[0, j, 0:size, :]
        v_tile = vsT_ref[0, j, :, 0:size]
        blk0 = pl.multiple_of(j * BLOCKS_PER_TILE, BLOCKS_PER_TILE)
        out = []
        for g in range(NSA_KV_HEADS):
            m_i, l_i, acc = carry[g]
            s = jnp.dot(k_tile, q_g[g], preferred_element_type=f32)
            selrows = sel_ref[g, pl.ds(blk0, BLOCKS_PER_TILE), :]
            bias = jnp.concatenate(
                [jnp.broadcast_to(selrows[r:r + 1, :], (SEL_BLOCK, Q_BLOCK)) for r in range(size // SEL_BLOCK)], axis=0)
            if causal:
                key = j * SEL_TILE + lax.broadcasted_iota(jnp.int32, (size, Q_BLOCK), 0)
                bias = jnp.where(key <= t_row_q, bias, NEG_INF)
            s = s + jnp.concatenate([bias] * HEADS_PER_KV, axis=1)
            m_new = jnp.maximum(m_i, jnp.max(s, axis=0, keepdims=True))
            alpha = jnp.exp(m_i - m_new)
            p = jnp.exp(s - m_new)
            l_new = alpha * l_i + jnp.sum(p, axis=0, keepdims=True)
            pv = jnp.dot(v_tile, p.astype(bf16), preferred_element_type=f32)
            out.append((m_new, l_new, alpha * acc + pv))
        return tuple(out)

    n_full = q0 // SEL_TILE
    init = tuple((jnp.full((1, QL), NEG_INF, f32), jnp.zeros((1, QL), f32), jnp.zeros((KV_COLS, QL), f32))
                 for _ in range(NSA_KV_HEADS))
    carry = lax.fori_loop(0, n_full, functools.partial(sel_tile, causal=False), init)
    carry = lax.cond(
        q0 % SEL_TILE + Q_BLOCK <= SEL_TILE // 2,
        functools.partial(sel_tile, n_full, causal=True, size=SEL_TILE // 2),
        functools.partial(sel_tile, n_full, causal=True, size=SEL_TILE),
        carry)

    start = pl.multiple_of(jnp.maximum(q0 - WINDOW, 0), Q_BLOCK)
    j0 = start // Q_BLOCK
    k_win = kw_ref[0, pl.ds(start, WIN_SPAN), :]
    key_w = start + lax.broadcasted_iota(jnp.int32, (WIN_SPAN, Q_BLOCK), 0)
    bias_w = jnp.where(key_w <= t_row_q, 0.0, NEG_INF)
    bias_w = jnp.where(key_w > t_row_q - WINDOW, bias_w, NEG_INF)
    bias_w = jnp.concatenate([bias_w] * HEADS_PER_KV, axis=1)
    for g in range(NSA_KV_HEADS):
        rows = slice(g * HEAD_DIM, (g + 1) * HEAD_DIM)
        m_s, l_s, acc_s = carry[g]
        o_s = acc_s[rows] * (1.0 / l_s)
        s_w = jnp.dot(k_win, q_g[g], preferred_element_type=f32) + bias_w
        m_w = jnp.max(s_w, axis=0, keepdims=True)
        p_w = jnp.exp(s_w - m_w)
        l_w = jnp.sum(p_w, axis=0, keepdims=True)
        p_wb = p_w.astype(bf16)
        acc_w = jnp.zeros((KV_COLS, QL), f32)
        for i in range(WIN_SPAN // Q_BLOCK):
            acc_w = acc_w + jnp.dot(vwT_ref[0, j0 + i], p_wb[i * Q_BLOCK:(i + 1) * Q_BLOCK, :], preferred_element_type=f32)
        o_w = acc_w[rows] * (1.0 / l_w)

        for pair in range(HEADS_PER_KV // 2):
            halves = []
            for hg in (2 * pair, 2 * pair + 1):
                h = g * HEADS_PER_KV + hg
                lanes = slice(hg * Q_BLOCK, (hg + 1) * Q_BLOCK)
                gate = jax.nn.sigmoid(gT_ref[0, 3 * h:3 * h + 3, :])
                halves.append(gate[0:1] * o_c[g][:, lanes] + gate[1:2] * o_s[:, lanes] + gate[2:3] * o_w[:, lanes])
            both = jnp.concatenate(halves, axis=0)
            col0 = (g * HEADS_PER_KV + 2 * pair) * HEAD_DIM
            o_ref[0, :, col0:col0 + 2 * HEAD_DIM] = both.T.astype(o_ref.dtype)


def _nsa_attention(qT, gT, kc, vcT, ks, vsT, kw, vwT):
    b, hd, s = qT.shape
    n_sel = s // SEL_BLOCK
    nc = kc.shape[1]
    n_cmp = (s - CMP_BLOCK) // CMP_STRIDE + 1
    mselT = jnp.asarray(np.pad(_cmp_to_sel_matrix(n_cmp, n_sel).T, ((0, 0), (0, nc - n_cmp))), jnp.bfloat16)
    ks4 = ks.reshape(b, s // SEL_TILE, SEL_TILE, KV_COLS)
    return pl.pallas_call(
        functools.partial(_nsa_kernel, n_sel=n_sel),
        name="nsa_attention",
        grid=(b, s // Q_BLOCK),
        in_specs=[
            pl.BlockSpec((1, hd, Q_BLOCK), lambda i, c: (i, 0, c)),
            pl.BlockSpec((1, 3 * NSA_HEADS, Q_BLOCK), lambda i, c: (i, 0, c)),
            pl.BlockSpec((1, nc, KV_COLS), lambda i, c: (i, 0, 0)),
            pl.BlockSpec((1, KV_COLS, nc), lambda i, c: (i, 0, 0)),
            pl.BlockSpec((n_sel, nc), lambda i, c: (0, 0)),
            pl.BlockSpec((1, s // SEL_TILE, SEL_TILE, KV_COLS), lambda i, c: (i, 0, 0, 0)),
            pl.BlockSpec((1, s // SEL_TILE, KV_COLS, SEL_TILE), lambda i, c: (i, 0, 0, 0)),
            pl.BlockSpec((1, s, KV_COLS), lambda i, c: (i, 0, 0)),
            pl.BlockSpec((1, s // Q_BLOCK, KV_COLS, Q_BLOCK), lambda i, c: (i, 0, 0, 0)),
        ],
        out_specs=pl.BlockSpec((1, Q_BLOCK, hd), lambda i, c: (i, c, 0)),
        out_shape=jax.ShapeDtypeStruct((b, s, hd), jnp.bfloat16),
        scratch_shapes=[pltpu.VMEM((n_sel, Q_BLOCK), jnp.float32),
                        pltpu.VMEM((NSA_KV_HEADS, n_sel, Q_BLOCK), jnp.float32)],
        compiler_params=pltpu.CompilerParams(dimension_semantics=("arbitrary", "arbitrary")),
    )(qT, gT, kc, vcT, mselT, ks4, vsT, kw, vwT)


def _route_tile(xb, first_step, wT_ref, b_ref, tri_ref, route_ref, cnt_ref, run_ref):
    f32 = jnp.float32

    @pl.when(first_step)
    def _():
        run_ref[...] = jnp.zeros_like(run_ref)

    tm = xb.shape[0]
    logits = lax.dot_general(wT_ref[...], xb, _NT, preferred_element_type=f32) + b_ref[...]
    sub = lax.broadcasted_iota(jnp.int32, (LANES, tm), 0)
    is_g = sub < N_GROUPS
    gl = jnp.where(is_g, logits, NEG_INF)
    g_max = jnp.max(gl, axis=0, keepdims=True)
    g_star = jnp.min(jnp.where(gl == g_max, sub, LANES), axis=0, keepdims=True)
    p_group = 1.0 / jnp.sum(jnp.where(is_g, jnp.exp(gl - g_max), 0.0), axis=0, keepdims=True)
    lo = N_GROUPS + g_star * EXPERTS_PER_GROUP
    in_grp = (sub >= lo) & (sub < lo + EXPERTS_PER_GROUP)
    el = jnp.where(in_grp, logits, NEG_INF)
    v1 = jnp.max(el, axis=0, keepdims=True)
    i1 = jnp.min(jnp.where(el == v1, sub, LANES), axis=0, keepdims=True)
    el2 = jnp.where(sub == i1, NEG_INF, el)
    v2 = jnp.max(el2, axis=0, keepdims=True)
    i2 = jnp.min(jnp.where(el2 == v2, sub, LANES), axis=0, keepdims=True)
    e21 = jnp.exp(v2 - v1)
    gate1 = p_group * (1.0 / (1.0 + e21))
    gate2 = p_group * (e21 / (1.0 + e21))
    oh1 = (sub == i1).astype(f32)
    oh2 = (sub == i2).astype(f32)
    both = oh1 + oh2
    before = jnp.dot(both.astype(jnp.bfloat16), tri_ref[...], preferred_element_type=f32) + run_ref[...]
    rank1 = jnp.sum(oh1 * before, axis=0, keepdims=True)
    rank2 = jnp.sum(oh2 * before, axis=0, keepdims=True)
    run_ref[...] = run_ref[...] + jnp.sum(both, axis=1, keepdims=True)
    cnt_ref[...] = run_ref[...]
    zero = jnp.zeros_like(gate1)
    route_ref[...] = jnp.concatenate(
        [gate1, gate2, (i1 - N_GROUPS).astype(f32), (i2 - N_GROUPS).astype(f32), rank1, rank2, zero, zero], axis=0)


def _router_operands(wg, bg, we, be, tile):
    d = wg.shape[0]
    pad = LANES - N_GROUPS - N_EXPERTS
    w_t = jnp.concatenate([wg, we.reshape(d, N_EXPERTS), jnp.zeros((d, pad), wg.dtype)], axis=1).T
    bias = jnp.concatenate([bg, be.reshape(N_EXPERTS), jnp.zeros((pad,), bg.dtype)])[:, None]
    tri = jnp.asarray(np.triu(np.ones((tile, tile), np.float32), 1), jnp.bfloat16)
    return w_t.astype(jnp.bfloat16), bias, tri


def _dispatch_kernel(zt_ref, dest_ref, x_ref, xd_hbm, zbuf, ring, sem, zsem):
    tm = x_ref.shape[0]

    @pl.when(pl.program_id(0) == 0)
    def _():
        zbuf[...] = jnp.zeros_like(zbuf)

        def zero_copy(k):
            start = pl.multiple_of(jnp.maximum(zt_ref[k], 0) * EXPERT_TILE, EXPERT_TILE)
            return pltpu.make_async_copy(zbuf, xd_hbm.at[pl.ds(start, EXPERT_TILE)], zsem)

        def start_body(k, c):
            @pl.when(zt_ref[k] >= 0)
            def _():
                zero_copy(k).start()
            return c

        def wait_body(k, c):
            @pl.when(zt_ref[k] >= 0)
            def _():
                zero_copy(k).wait()
            return c

        lax.fori_loop(0, zt_ref.shape[0], start_body, 0)
        lax.fori_loop(0, zt_ref.shape[0], wait_body, 0)

    i = pl.program_id(0)
    n = pl.num_programs(0)
    slot = i % 2

    def wait_tile(of_slot):
        for k in range(EXPERT_TOP_K):
            pltpu.make_async_copy(ring.at[of_slot], xd_hbm.at[pl.ds(0, tm)], sem.at[of_slot]).wait()

    @pl.when(i >= 2)
    def _():
        wait_tile(slot)

    ring[slot] = x_ref[...].reshape(ring.shape[1:])

    def body(r, c):
        for k in range(EXPERT_TOP_K):
            pltpu.make_async_copy(
                ring.at[slot, r], xd_hbm.at[dest_ref[0, 0, k * tm + r]], sem.at[slot]).start(priority=k)
        return c

    lax.fori_loop(0, tm, body, 0, unroll=8)

    @pl.when(i == n - 1)
    def _():
        wait_tile(slot)

        @pl.when(n > 1)
        def _():
            wait_tile(1 - slot)


def _moe_dispatch(xt, dest2, zero_tiles, n_rows):
    t, d = xt.shape
    row = (SUBLANES, d // SUBLANES)
    grid_spec = pltpu.PrefetchScalarGridSpec(
        num_scalar_prefetch=1,
        grid=(t // ROUTE_TILE,),
        in_specs=[pl.BlockSpec((1, 1, EXPERT_TOP_K * ROUTE_TILE), lambda i, zt: (i, 0, 0), memory_space=pltpu.SMEM),
                  pl.BlockSpec((ROUTE_TILE, d), lambda i, zt: (i, 0))],
        out_specs=pl.BlockSpec(memory_space=pl.ANY),
        scratch_shapes=[pltpu.VMEM((EXPERT_TILE,) + row, xt.dtype), pltpu.VMEM((2, ROUTE_TILE) + row, xt.dtype),
                        pltpu.SemaphoreType.DMA((2,)), pltpu.SemaphoreType.DMA(())],
    )
    return pl.pallas_call(
        _dispatch_kernel,
        name="moe_dispatch",
        grid_spec=grid_spec,
        out_shape=jax.ShapeDtypeStruct((n_rows,) + row, xt.dtype),
        compiler_params=pltpu.CompilerParams(dimension_semantics=("arbitrary",), has_side_effects=True),
    )(zero_tiles, dest2, xt)


def _expert_kernel(te_ref, nu_ref, xd_ref, w1_ref, w3_ref, w2_ref, y_ref, w1b, w3b, w2b):
    i = pl.program_id(0)
    used = i < nu_ref[0]
    new_expert = (i == 0) | (te_ref[i] != te_ref[jnp.maximum(i - 1, 0)])

    @pl.when(used & new_expert)
    def _():
        w1b[...] = w1_ref[0, 0].astype(jnp.bfloat16)
        w3b[...] = w3_ref[0, 0].astype(jnp.bfloat16)
        w2b[...] = w2_ref[0, 0].astype(jnp.bfloat16)

    @pl.when(used)
    def _():
        xb = xd_ref[...].reshape(xd_ref.shape[0], w1b.shape[0]).astype(jnp.bfloat16)
        h1 = jnp.dot(xb, w1b[...], preferred_element_type=jnp.float32)
        h3 = jnp.dot(xb, w3b[...], preferred_element_type=jnp.float32)
        a = (h1 * jax.nn.sigmoid(h1) * h3).astype(jnp.bfloat16)
        y_ref[...] = jnp.dot(a, w2b[...], preferred_element_type=jnp.float32).reshape(y_ref.shape)

    @pl.when(jnp.logical_not(used))
    def _():
        y_ref[...] = jnp.zeros_like(y_ref)


def _moe_experts(x_disp, tile_expert, n_used, w1, w3, w2, layer):
    n_rows = x_disp.shape[0]
    row = x_disp.shape[1:]
    d = row[0] * row[1]
    n_tiles = n_rows // EXPERT_TILE
    hid = w1.shape[3]

    def row_map(i, te, nu):
        return (i, 0, 0)

    def w_map(i, te, nu):
        return (layer, te[i], 0, 0)

    grid_spec = pltpu.PrefetchScalarGridSpec(
        num_scalar_prefetch=2,
        grid=(n_tiles,),
        in_specs=[pl.BlockSpec((EXPERT_TILE,) + row, row_map),
                  pl.BlockSpec((1, 1, d, hid), w_map),
                  pl.BlockSpec((1, 1, d, hid), w_map),
                  pl.BlockSpec((1, 1, hid, d), w_map)],
        out_specs=pl.BlockSpec((EXPERT_TILE,) + row, row_map),
        scratch_shapes=[pltpu.VMEM((d, hid), jnp.bfloat16), pltpu.VMEM((d, hid), jnp.bfloat16),
                        pltpu.VMEM((hid, d), jnp.bfloat16)],
    )
    return pl.pallas_call(
        _expert_kernel,
        name="moe_experts",
        grid_spec=grid_spec,
        out_shape=jax.ShapeDtypeStruct((n_rows,) + row, jnp.float32),
        compiler_params=pltpu.CompilerParams(dimension_semantics=("arbitrary",), vmem_limit_bytes=VMEM_LIMIT),
    )(tile_expert, n_used, x_disp, w1, w3, w2)


def _combine_kernel(dest_ref, dest_next_ref, x_ref, route_ref, g_ref, b_ref, yd_hbm, o_ref, ybuf, sem):
    i = pl.program_id(0)
    n = pl.num_programs(0)
    tm, d = x_ref.shape
    rows = EXPERT_TOP_K * tm

    def start_tile(dref, slot):
        def body(r, c):
            for k in range(EXPERT_TOP_K):
                row = k * tm + r
                pltpu.make_async_copy(yd_hbm.at[dref[0, 0, row]], ybuf.at[slot, row], sem.at[slot]).start(priority=k)
            return c
        lax.fori_loop(0, tm, body, 0, unroll=8)

    slot = i % 2

    @pl.when(i == 0)
    def _():
        start_tile(dest_ref, 0)

    @pl.when(i + 1 < n)
    def _():
        start_tile(dest_next_ref, 1 - slot)

    pltpu.make_async_copy(yd_hbm.at[pl.ds(0, rows)], ybuf.at[slot], sem.at[slot]).wait()
    y1 = ybuf[slot, 0:tm].reshape(tm, d)
    y2 = ybuf[slot, tm:rows].reshape(tm, d)
    route = route_ref[...].T
    y = DN_ALPHA * x_ref[...] + (y1 * route[:, 0:1] + y2 * route[:, 1:2])
    o_ref[...] = _ln_rows(y, g_ref[...], b_ref[...])


def _moe_combine_ln(xt, y_disp, dest2, route, ln_g, ln_b):
    t, d = xt.shape
    n = t // ROUTE_TILE
    rows = EXPERT_TOP_K * ROUTE_TILE
    return pl.pallas_call(
        _combine_kernel,
        name="moe_combine_ln",
        grid=(n,),
        in_specs=[pl.BlockSpec((1, 1, rows), lambda i: (i, 0, 0), memory_space=pltpu.SMEM),
                  pl.BlockSpec((1, 1, rows), lambda i: (jnp.minimum(i + 1, n - 1), 0, 0), memory_space=pltpu.SMEM),
                  pl.BlockSpec((ROUTE_TILE, d), lambda i: (i, 0)),
                  pl.BlockSpec((ROUTE_ROWS, ROUTE_TILE), lambda i: (0, i)),
                  pl.BlockSpec((1, d), lambda i: (0, 0)),
                  pl.BlockSpec((1, d), lambda i: (0, 0)),
                  pl.BlockSpec(memory_space=pl.ANY)],
        out_specs=pl.BlockSpec((ROUTE_TILE, d), lambda i: (i, 0)),
        out_shape=jax.ShapeDtypeStruct((t, d), jnp.float32),
        scratch_shapes=[pltpu.VMEM((2, rows) + y_disp.shape[1:], jnp.float32), pltpu.SemaphoreType.DMA((2,))],
        compiler_params=pltpu.CompilerParams(dimension_semantics=("arbitrary",), vmem_limit_bytes=VMEM_LIMIT),
    )(dest2, dest2, xt, route, ln_g.reshape(1, d), ln_b.reshape(1, d), y_disp)


def _moe_sublayer(x, route, cnt, w1, w3, w2, layer, ln_g, ln_b):
    b, s, d = x.shape
    t = b * s
    xt = x.reshape(t, d)
    counts = cnt[N_GROUPS:N_GROUPS + N_EXPERTS, 0].astype(jnp.int32)
    n_tiles = (t * EXPERT_TOP_K) // EXPERT_TILE + N_EXPERTS
    tiles_per = (counts + EXPERT_TILE - 1) // EXPERT_TILE
    tile_end = jnp.cumsum(tiles_per)
    pad_start = (tile_end - tiles_per) * EXPERT_TILE
    n_used = tile_end[-1:].astype(jnp.int32)
    tile_ids = jnp.minimum(jnp.arange(n_tiles), n_used[0] - 1)
    tile_expert = jnp.sum(tile_ids[:, None] >= tile_end[None, :], axis=1).astype(jnp.int32)
    experts = route[2:4].astype(jnp.int32)
    first_row = jnp.sum(jnp.where(experts[..., None] == jnp.arange(N_EXPERTS), pad_start, 0), axis=-1)
    dest = first_row + route[4:6].astype(jnp.int32)
    dest2 = jnp.swapaxes(dest.reshape(EXPERT_TOP_K, t // ROUTE_TILE, ROUTE_TILE), 0, 1)
    dest2 = dest2.reshape(t // ROUTE_TILE, 1, EXPERT_TOP_K * ROUTE_TILE)
    last_tiles = jnp.where(tiles_per > 0, tile_end - 1, -1)
    tail_tiles = n_used[0] + jnp.arange(N_EXPERTS)
    tail_tiles = jnp.where(tail_tiles < n_tiles, tail_tiles, -1)
    zero_tiles = jnp.concatenate([last_tiles, tail_tiles]).astype(jnp.int32)
    x_disp = _moe_dispatch(xt, dest2, zero_tiles, n_tiles * EXPERT_TILE)
    y_disp = _moe_experts(x_disp, tile_expert, n_used, w1, w3, w2, layer)
    return _moe_combine_ln(xt, y_disp, dest2, route, ln_g, ln_b).reshape(b, s, d)


def kernel(x, mem, mem_wk, mem_wv, ev_w_in, ev_conv_w, ev_conv_b, ev_cnorm_g, ev_cnorm_b, ev_cmp_pe_k, ev_cmp_w1_k, ev_cmp_w2_k, ev_cmp_pe_v, ev_cmp_w1_v, ev_cmp_w2_v, ev_w_out, od_w_in, od_conv_w, od_w_out, ln_mix_g, ln_mix_b, xa_wq, xa_wo, ln_xa_g, ln_xa_b, moe_wg, moe_bg, moe_we, moe_be, moe_w1, moe_w3, moe_w2, ln_ffn_g, ln_ffn_b):
    b, s, d = x.shape
    mem_k, mem_v = _mem_kv(mem, mem_wk, mem_wv)
    for layer in range(DEPTH):
        i = layer // 2
        if layer % 2 == 0:
            a, kv_in, k_sel, k_win, q_t, v_sel_t, v_win_t, gate_t = _even_in_proj(x, ev_w_in[i])
            a = _conformer_conv(a, ev_conv_w[i], ev_conv_b[i], ev_cnorm_g[i], ev_cnorm_b[i])
            k_cmp, v_cmp_t = _compress_kv(kv_in, ev_cmp_pe_k[i], ev_cmp_w1_k[i], ev_cmp_w2_k[i], ev_cmp_pe_v[i], ev_cmp_w1_v[i], ev_cmp_w2_v[i])
            o = _nsa_attention(q_t, gate_t, k_cmp, v_cmp_t, k_sel, v_sel_t, k_win, v_win_t)
            x = _proj_residual_ln(a.reshape(b * s, -1), o.reshape(b * s, -1), x.reshape(b * s, d), ev_w_out[i], ln_mix_g[layer], ln_mix_b[layer]).reshape(b, s, d)
        else:
            x = _odd_mixer_sublayer(x, od_w_in[i], od_conv_w[i], od_w_out[i], ln_mix_g[layer], ln_mix_b[layer])
        x, route, cnt = _xattn_router_sublayer(x, mem_k, mem_v, xa_wq[layer], xa_wo[layer], ln_xa_g[layer], ln_xa_b[layer],
                                               moe_wg[layer], moe_bg[layer], moe_we[layer], moe_be[layer])
        x = _moe_sublayer(x, route, cnt, moe_w1, moe_w3, moe_w2, layer, ln_ffn_g[layer], ln_ffn_b[layer])
    return x
```

```python
import functools

import numpy as np
import jax
import jax.numpy as jnp
from jax import lax
from jax.experimental import pallas as pl
from jax.experimental.pallas import tpu as pltpu

D_MODEL = 1024
DEPTH = 2
CONV_CH = D_MODEL // 2
CONV_WIDTH = 31
NSA_HEADS = 8
NSA_KV_HEADS = 2
HEAD_DIM = (D_MODEL // 2) // NSA_HEADS
CMP_BLOCK = 32
CMP_STRIDE = 16
SEL_BLOCK = 64
SEL_TOP_N = 16
WINDOW = 512
Q_BLOCK = 256
FORCE_SCORE = 1e4
SHORT_CONV_WIDTH = 3
XA_HEADS = 4
XA_HEAD_DIM = D_MODEL // XA_HEADS
N_GROUPS = 4
EXPERTS_PER_GROUP = 8
N_EXPERTS = N_GROUPS * EXPERTS_PER_GROUP
EXPERT_TOP_K = 2
DN_ALPHA = (2 * DEPTH) ** 0.25
LN_EPS = 1e-5
NEG_INF = -1e30
KV_COLS = NSA_KV_HEADS * HEAD_DIM
QCOLS = NSA_HEADS * HEAD_DIM
GATE_ROWS = 3 * NSA_HEADS

LANES = 128
SUBLANES = 8
HEADS_PER_KV = NSA_HEADS // NSA_KV_HEADS
QL = Q_BLOCK * HEADS_PER_KV
SEL_TILE = 1024
WIN_SPAN = WINDOW + Q_BLOCK
BLOCKS_PER_TILE = SEL_TILE // SEL_BLOCK
SEQ_TILE = 1024
HALO = 32
ODD_HALO = 8
ROUTE_TILE = 512
EXPERT_TILE = 512
COMBINE_CHUNK = 128
ROUTE_ROWS = 8
VMEM_LIMIT = 56 * 1024 * 1024

_NT = (((1,), (1,)), ((), ()))


def _ln_rows(y, g, b):
    mu = jnp.mean(y, axis=-1, keepdims=True)
    yc = y - mu
    var = jnp.mean(yc * yc, axis=-1, keepdims=True)
    return yc * lax.rsqrt(var + LN_EPS) * g + b


def _even_in_kernel(x_ref, wa_ref, wkv_ref, wk2_ref, wqT_ref, wvT_ref, wgT_ref,
                    a_ref, kv_ref, ks_ref, kw_ref, qT_ref, vsT_ref, vwT_ref, gT_ref):
    f32 = jnp.float32
    bf16 = jnp.bfloat16
    xb = x_ref[0].astype(bf16)
    av = jnp.dot(xb, wa_ref[...], preferred_element_type=f32)
    a_ref[0] = av[:, :CONV_CH] * jax.nn.sigmoid(av[:, CONV_CH:])
    kv_ref[0] = jnp.dot(xb, wkv_ref[...], preferred_element_type=f32)
    k2 = jnp.dot(xb, wk2_ref[...], preferred_element_type=f32)
    ks_ref[0] = k2[:, :KV_COLS].astype(bf16)
    kw_ref[0] = k2[:, KV_COLS:].astype(bf16)
    qT_ref[0] = lax.dot_general(wqT_ref[...], xb, _NT, preferred_element_type=f32).astype(bf16)
    vT = lax.dot_general(wvT_ref[...], xb, _NT, preferred_element_type=f32).astype(bf16)
    vsT_ref[0, 0] = vT[:KV_COLS]
    for j in range(SEQ_TILE // Q_BLOCK):
        vwT_ref[0, j] = vT[KV_COLS:, j * Q_BLOCK:(j + 1) * Q_BLOCK]
    gT_ref[0] = lax.dot_general(wgT_ref[...], xb, _NT, preferred_element_type=f32)


def _even_in_proj(x, w_in):
    b, s, d = x.shape
    bf16 = jnp.bfloat16
    c = np.cumsum((0, CONV_CH, CONV_CH, QCOLS, KV_COLS, KV_COLS, KV_COLS, KV_COLS, KV_COLS, KV_COLS, GATE_ROWS))
    col = lambda i, j: w_in[:, c[i]:c[j]]
    wa = col(0, 2).astype(bf16)
    wq_t = col(2, 3).T.astype(bf16)
    wkv = col(3, 5).astype(bf16)
    wk2 = jnp.concatenate([col(5, 6), col(7, 8)], axis=1).astype(bf16)
    wv_t = jnp.concatenate([col(6, 7), col(8, 9)], axis=1).T.astype(bf16)
    wg_t = col(9, 10).T.astype(bf16)
    ts = SEQ_TILE
    per_sel = SEL_TILE // ts
    full = lambda shape: pl.BlockSpec(shape, lambda i, j: (0,) * len(shape))
    return pl.pallas_call(
        _even_in_kernel,
        name="even_in_proj",
        grid=(b, s // ts),
        in_specs=[pl.BlockSpec((1, ts, d), lambda i, j: (i, j, 0)),
                  full(wa.shape), full(wkv.shape), full(wk2.shape), full(wq_t.shape), full(wv_t.shape), full(wg_t.shape)],
        out_specs=[pl.BlockSpec((1, ts, CONV_CH), lambda i, j: (i, j, 0)),
                   pl.BlockSpec((1, ts, 2 * KV_COLS), lambda i, j: (i, j, 0)),
                   pl.BlockSpec((1, ts, KV_COLS), lambda i, j: (i, j, 0)),
                   pl.BlockSpec((1, ts, KV_COLS), lambda i, j: (i, j, 0)),
                   pl.BlockSpec((1, QCOLS, ts), lambda i, j: (i, 0, j)),
                   pl.BlockSpec((1, 1, KV_COLS, ts), lambda i, j: (i, j // per_sel, 0, j % per_sel)),
                   pl.BlockSpec((1, ts // Q_BLOCK, KV_COLS, Q_BLOCK), lambda i, j: (i, j, 0, 0)),
                   pl.BlockSpec((1, GATE_ROWS, ts), lambda i, j: (i, 0, j))],
        out_shape=[jax.ShapeDtypeStruct((b, s, CONV_CH), jnp.float32),
                   jax.ShapeDtypeStruct((b, s, 2 * KV_COLS), jnp.float32),
                   jax.ShapeDtypeStruct((b, s, KV_COLS), bf16),
                   jax.ShapeDtypeStruct((b, s, KV_COLS), bf16),
                   jax.ShapeDtypeStruct((b, QCOLS, s), bf16),
                   jax.ShapeDtypeStruct((b, s // SEL_TILE, KV_COLS, SEL_TILE), bf16),
                   jax.ShapeDtypeStruct((b, s // Q_BLOCK, KV_COLS, Q_BLOCK), bf16),
                   jax.ShapeDtypeStruct((b, GATE_ROWS, s), jnp.float32)],
        compiler_params=pltpu.CompilerParams(dimension_semantics=("arbitrary", "arbitrary"), vmem_limit_bytes=VMEM_LIMIT),
    )(x, wa, wkv, wk2, wq_t, wv_t, wg_t)


def _conv_kernel(cur_ref, halo_ref, w_ref, cb_ref, g_ref, b_ref, o_ref, ext_ref, win_ref):
    j = pl.program_id(1)
    ts = cur_ref.shape[1]
    halo = halo_ref[0]
    ext_ref[0:HALO, :] = jnp.where(j > 0, halo, jnp.zeros_like(halo))
    ext_ref[HALO:HALO + ts, :] = cur_ref[0]
    first = HALO - (CONV_WIDTH - 1)
    acc = jnp.zeros((ts, CONV_CH), jnp.float32)
    for p in range(SUBLANES):
        n_a = len(range(p, CONV_WIDTH, SUBLANES))
        rows = ts + SUBLANES * (n_a - 1)
        win_ref[0:rows, :] = ext_ref[first + p:first + p + rows, :]
        for a in range(n_a):
            k = SUBLANES * a + p
            acc = acc + w_ref[k:k + 1, :] * win_ref[SUBLANES * a:SUBLANES * a + ts, :]
    y = _ln_rows(acc + cb_ref[...], g_ref[...], b_ref[...])
    o_ref[0] = (y * jax.nn.sigmoid(y)).astype(o_ref.dtype)


def _conformer_conv(a, conv_w, conv_b, cn_g, cn_b):
    b, s, c = a.shape
    ts = SEQ_TILE
    per = ts // HALO
    row = lambda v: v.reshape(1, c)
    return pl.pallas_call(
        _conv_kernel,
        name="conformer_conv",
        grid=(b, s // ts),
        in_specs=[pl.BlockSpec((1, ts, c), lambda i, j: (i, j, 0)),
                  pl.BlockSpec((1, HALO, c), lambda i, j: (i, jnp.maximum(j * per - 1, 0), 0)),
                  pl.BlockSpec((CONV_WIDTH, c), lambda i, j: (0, 0)),
                  pl.BlockSpec((1, c), lambda i, j: (0, 0)),
                  pl.BlockSpec((1, c), lambda i, j: (0, 0)),
                  pl.BlockSpec((1, c), lambda i, j: (0, 0))],
        out_specs=pl.BlockSpec((1, ts, c), lambda i, j: (i, j, 0)),
        out_shape=jax.ShapeDtypeStruct((b, s, c), jnp.bfloat16),
        scratch_shapes=[pltpu.VMEM((HALO + ts, c), jnp.float32), pltpu.VMEM((HALO + ts, c), jnp.float32)],
        compiler_params=pltpu.CompilerParams(dimension_semantics=("arbitrary", "arbitrary")),
    )(a, a, conv_w, row(conv_b), row(cn_g), row(cn_b))


def _compress_kernel(r_ref, pe_ref, w1_ref, w2_ref, w2T_ref, o_ref, oT_ref):
    f32 = jnp.float32
    bf16 = jnp.bfloat16
    nch = o_ref.shape[1]
    hw = w1_ref.shape[3]
    first = jnp.zeros((nch, hw), f32)
    second = jnp.zeros((nch, hw), f32)
    for l in range(CMP_STRIDE):
        tok = r_ref[pl.ds(l, nch, stride=CMP_STRIDE), :]
        first = first + jnp.dot((tok + pe_ref[0, l:l + 1, :]).astype(bf16), w1_ref[0, l], preferred_element_type=f32)
        second = second + jnp.dot((tok + pe_ref[0, CMP_STRIDE + l:CMP_STRIDE + l + 1, :]).astype(bf16),
                                  w1_ref[0, CMP_STRIDE + l], preferred_element_type=f32)
    second_next = jnp.concatenate([second[1:], jnp.zeros_like(second[0:1])], axis=0)
    hid = jax.nn.gelu(first + second_next).astype(bf16)
    o_ref[0] = jnp.dot(hid, w2_ref[0], preferred_element_type=f32).astype(bf16)
    oT_ref[0] = lax.dot_general(w2T_ref[0], hid, _NT, preferred_element_type=f32).astype(bf16)


def _compress_kv(kv_in, pe_k, w1_k, w2_k, pe_v, w1_v, w2_v):
    b, s, _ = kv_in.shape
    bf16 = jnp.bfloat16
    nch = s // CMP_STRIDE
    g = NSA_KV_HEADS
    hidden = w1_k.shape[1]

    def expand(pe, w1, w2):
        same = jnp.eye(g, dtype=bool)
        w1r = w1.reshape(CMP_BLOCK, HEAD_DIM, hidden)
        w1e = jnp.where(same[None, :, None, :, None], w1r[:, None, :, None, :], 0.0)
        w2e = jnp.where(same[:, None, :, None], w2[None, :, None, :], 0.0)
        return jnp.tile(pe, (1, g)), w1e.reshape(CMP_BLOCK, KV_COLS, g * hidden), w2e.reshape(g * hidden, KV_COLS)

    pk, w1k, w2k = expand(pe_k, w1_k, w2_k)
    pv, w1v, w2v = expand(pe_v, w1_v, w2_v)
    pe = jnp.stack([pk, pv])
    w1 = jnp.stack([w1k, w1v]).astype(bf16)
    w2 = jnp.stack([w2k, w2v]).astype(bf16)
    w2t = jnp.swapaxes(w2, 1, 2)
    o, o_t = pl.pallas_call(
        _compress_kernel,
        name="compress_kv",
        grid=(b, 2),
        in_specs=[pl.BlockSpec((s, KV_COLS), lambda i, j: (i, j)),
                  pl.BlockSpec((1,) + pe.shape[1:], lambda i, j: (j, 0, 0)),
                  pl.BlockSpec((1,) + w1.shape[1:], lambda i, j: (j, 0, 0, 0)),
                  pl.BlockSpec((1,) + w2.shape[1:], lambda i, j: (j, 0, 0)),
                  pl.BlockSpec((1,) + w2t.shape[1:], lambda i, j: (j, 0, 0))],
        out_specs=[pl.BlockSpec((1, nch, KV_COLS), lambda i, j: (2 * i + j, 0, 0)),
                   pl.BlockSpec((1, KV_COLS, nch), lambda i, j: (2 * i + j, 0, 0))],
        out_shape=[jax.ShapeDtypeStruct((b * 2, nch, KV_COLS), bf16), jax.ShapeDtypeStruct((b * 2, KV_COLS, nch), bf16)],
        compiler_params=pltpu.CompilerParams(dimension_semantics=("arbitrary", "arbitrary"), vmem_limit_bytes=VMEM_LIMIT),
    )(kv_in.reshape(b * s, 2 * KV_COLS), pe, w1, w2, w2t)
    return o.reshape(b, 2, nch, KV_COLS)[:, 0], o_t.reshape(b, 2, KV_COLS, nch)[:, 1]


def _proj_ln_kernel(a_ref, o_ref, x_ref, w_ref, g_ref, b_ref, y_ref):
    lhs = jnp.concatenate([a_ref[...], o_ref[...]], axis=1)
    mix = jnp.dot(lhs, w_ref[...], preferred_element_type=jnp.float32)
    y_ref[...] = _ln_rows(DN_ALPHA * x_ref[...] + mix, g_ref[...], b_ref[...])


def _proj_residual_ln(a, o, x, w, ln_g, ln_b):
    t, d = x.shape
    tm = SEQ_TILE
    return pl.pallas_call(
        _proj_ln_kernel,
        name="proj_residual_ln",
        grid=(t // tm,),
        in_specs=[pl.BlockSpec((tm, a.shape[1]), lambda i: (i, 0)),
                  pl.BlockSpec((tm, o.shape[1]), lambda i: (i, 0)),
                  pl.BlockSpec((tm, d), lambda i: (i, 0)),
                  pl.BlockSpec(w.shape, lambda i: (0, 0)),
                  pl.BlockSpec((1, d), lambda i: (0, 0)),
                  pl.BlockSpec((1, d), lambda i: (0, 0))],
        out_specs=pl.BlockSpec((tm, d), lambda i: (i, 0)),
        out_shape=jax.ShapeDtypeStruct((t, d), jnp.float32),
        compiler_params=pltpu.CompilerParams(dimension_semantics=("arbitrary",), vmem_limit_bytes=VMEM_LIMIT),
    )(a, o, x, w.astype(jnp.bfloat16), ln_g.reshape(1, d), ln_b.reshape(1, d))


def _mem_kv_kernel(m_ref, wk_ref, wv_ref, k_ref, v_ref):
    mb = m_ref[...].astype(jnp.bfloat16)
    k_ref[...] = jnp.dot(mb, wk_ref[...], preferred_element_type=jnp.float32).astype(jnp.bfloat16)
    v_ref[...] = jnp.dot(mb, wv_ref[...], preferred_element_type=jnp.float32).astype(jnp.bfloat16)


def _mem_kv(mem, wk, wv):
    b, m, d = mem.shape
    bf16 = jnp.bfloat16
    k, v = pl.pallas_call(
        _mem_kv_kernel,
        name="mem_kv",
        grid=(b,),
        in_specs=[pl.BlockSpec((m, d), lambda i: (i, 0)), pl.BlockSpec((d, d), lambda i: (0, 0)), pl.BlockSpec((d, d), lambda i: (0, 0))],
        out_specs=[pl.BlockSpec((m, d), lambda i: (i, 0)), pl.BlockSpec((m, d), lambda i: (i, 0))],
        out_shape=[jax.ShapeDtypeStruct((b * m, d), bf16), jax.ShapeDtypeStruct((b * m, d), bf16)],
        compiler_params=pltpu.CompilerParams(dimension_semantics=("arbitrary",), vmem_limit_bytes=VMEM_LIMIT),
    )(mem.reshape(b * m, d), wk.astype(bf16), wv.astype(bf16))
    return k.reshape(b, m, d), v.reshape(b, m, d)


def _xattn_kernel(x_ref, k_ref, v_ref, wq_ref, wo_ref, g_ref, b_ref, rw_ref, rb_ref, tri_ref,
                  y_ref, route_ref, cnt_ref, run_ref):
    f32 = jnp.float32
    bf16 = jnp.bfloat16
    x = x_ref[0]
    q = (jnp.dot(x.astype(bf16), wq_ref[...], preferred_element_type=f32).astype(bf16)
         * jnp.asarray(XA_HEAD_DIM ** -0.5, bf16))
    heads = []
    for h in range(XA_HEADS):
        cols = slice(h * XA_HEAD_DIM, (h + 1) * XA_HEAD_DIM)
        s = lax.dot_general(q[:, cols], k_ref[0, :, cols], _NT, preferred_element_type=f32)
        m = jnp.max(s, axis=1, keepdims=True)
        p = jnp.exp(s - m)
        p = p / jnp.sum(p, axis=1, keepdims=True)
        heads.append(jnp.dot(p.astype(bf16), v_ref[0, :, cols], preferred_element_type=f32).astype(bf16))
    att = jnp.concatenate(heads, axis=1)
    out = jnp.dot(att, wo_ref[...], preferred_element_type=f32)
    y = _ln_rows(DN_ALPHA * x + out, g_ref[...], b_ref[...])
    y_ref[0] = y
    first_step = (pl.program_id(0) == 0) & (pl.program_id(1) == 0)
    _route_tile(y.astype(bf16), first_step, rw_ref, rb_ref, tri_ref, route_ref, cnt_ref, run_ref)


def _xattn_router_sublayer(x, mem_k, mem_v, wq, wo, ln_g, ln_b, wg, bg, we, be):
    b, s, d = x.shape
    m = mem_k.shape[1]
    ts = SEQ_TILE
    per_seq = s // ts
    bf16 = jnp.bfloat16
    rw, rb, tri = _router_operands(wg, bg, we, be, ts)
    const = lambda shape: pl.BlockSpec(shape, lambda i, j: (0,) * len(shape))
    return pl.pallas_call(
        _xattn_kernel,
        name="xattn_router_sublayer",
        grid=(b, per_seq),
        in_specs=[pl.BlockSpec((1, ts, d), lambda i, j: (i, j, 0)),
                  pl.BlockSpec((1, m, d), lambda i, j: (i, 0, 0)),
                  pl.BlockSpec((1, m, d), lambda i, j: (i, 0, 0)),
                  const((d, d)), const((d, d)), const((1, d)), const((1, d)),
                  const(rw.shape), const(rb.shape), const(tri.shape)],
        out_specs=[pl.BlockSpec((1, ts, d), lambda i, j: (i, j, 0)),
                   pl.BlockSpec((ROUTE_ROWS, ts), lambda i, j: (0, i * per_seq + j)),
                   const((LANES, 1))],
        out_shape=[jax.ShapeDtypeStruct((b, s, d), jnp.float32),
                   jax.ShapeDtypeStruct((ROUTE_ROWS, b * s), jnp.float32),
                   jax.ShapeDtypeStruct((LANES, 1), jnp.float32)],
        scratch_shapes=[pltpu.VMEM((LANES, 1), jnp.float32)],
        compiler_params=pltpu.CompilerParams(dimension_semantics=("arbitrary", "arbitrary"), vmem_limit_bytes=VMEM_LIMIT),
    )(x, mem_k, mem_v, wq.astype(bf16), wo.astype(bf16), ln_g.reshape(1, d), ln_b.reshape(1, d), rw, rb, tri)


def _odd_kernel(x_ref, halo_ref, wb_ref, wc_ref, wh_ref, cw_ref, wo_ref, g_ref, b_ref, y_ref):
    f32 = jnp.float32
    bf16 = jnp.bfloat16
    j = pl.program_id(1)
    ts = x_ref.shape[1]
    x = x_ref[0]
    xe = jnp.concatenate([halo_ref[0], x], axis=0).astype(bf16)
    u = (jnp.dot(xe, wc_ref[...], preferred_element_type=f32) * jnp.dot(xe, wh_ref[...], preferred_element_type=f32))
    row = lax.broadcasted_iota(jnp.int32, (ODD_HALO + ts, 1), 0)
    u = jnp.where((row >= ODD_HALO) | (j > 0), u, 0.0)
    conv = jnp.zeros((ts, u.shape[1]), f32)
    for k in range(SHORT_CONV_WIDTH):
        off = ODD_HALO - (SHORT_CONV_WIDTH - 1) + k
        conv = conv + cw_ref[k:k + 1, :] * u[off:off + ts, :]
    gate_b = jnp.dot(xe[ODD_HALO:], wb_ref[...], preferred_element_type=f32)
    mix = jnp.dot((gate_b * conv).astype(bf16), wo_ref[...], preferred_element_type=f32)
    y_ref[0] = _ln_rows(DN_ALPHA * x + mix, g_ref[...], b_ref[...])


def _odd_mixer_sublayer(x, w_in, conv_w, w_out, ln_g, ln_b):
    b, s, d = x.shape
    ts = SEQ_TILE
    per = ts // ODD_HALO
    bf16 = jnp.bfloat16
    wb, wc, wh = (w_in[:, i * d:(i + 1) * d].astype(bf16) for i in range(3))
    full = lambda shape: pl.BlockSpec(shape, lambda i, j: (0,) * len(shape))
    return pl.pallas_call(
        _odd_kernel,
        name="odd_mixer_sublayer",
        grid=(b, s // ts),
        in_specs=[pl.BlockSpec((1, ts, d), lambda i, j: (i, j, 0)),
                  pl.BlockSpec((1, ODD_HALO, d), lambda i, j: (i, jnp.maximum(j * per - 1, 0), 0)),
                  full((d, d)), full((d, d)), full((d, d)), full(conv_w.shape), full((d, d)), full((1, d)), full((1, d))],
        out_specs=pl.BlockSpec((1, ts, d), lambda i, j: (i, j, 0)),
        out_shape=jax.ShapeDtypeStruct((b, s, d), jnp.float32),
        compiler_params=pltpu.CompilerParams(dimension_semantics=("arbitrary", "arbitrary"), vmem_limit_bytes=VMEM_LIMIT),
    )(x, x, wb, wc, wh, conv_w, w_out.astype(bf16), ln_g.reshape(1, d), ln_b.reshape(1, d))


def _cmp_to_sel_matrix(n_cmp, n_sel):
    c0 = np.arange(n_cmp) * CMP_STRIDE
    s0 = np.arange(n_sel) * SEL_BLOCK
    ov = np.minimum(c0[:, None] + CMP_BLOCK, s0[None, :] + SEL_BLOCK) - np.maximum(c0[:, None], s0[None, :])
    return (np.clip(ov, 0, None) / CMP_BLOCK).astype(np.float32)


def _nsa_kernel(qT_ref, gT_ref, kc_ref, vcT_ref, mselT_ref, ks_ref, vsT_ref, kw_ref, vwT_ref, o_ref,
                score_ref, sel_ref, *, n_sel):
    c = pl.program_id(1)
    q0 = c * Q_BLOCK
    f32 = jnp.float32
    bf16 = jnp.bfloat16
    n_cmp_pad = kc_ref.shape[1]

    lane_q = lax.broadcasted_iota(jnp.int32, (1, QL), 1) % Q_BLOCK
    t_row = q0 + lane_q
    t_row_q = q0 + lax.broadcasted_iota(jnp.int32, (1, Q_BLOCK), 1)
    cur_q = t_row_q // SEL_BLOCK

    top_n = min(SEL_TOP_N, n_sel)
    m_iota = lax.broadcasted_iota(jnp.int32, (n_sel, Q_BLOCK), 0)
    forced = (m_iota == 0) | (m_iota == cur_q) | (m_iota == cur_q - 1)
    valid = m_iota <= cur_q
    n_comp = jnp.minimum(q0 // SEL_BLOCK + Q_BLOCK // SEL_BLOCK, n_sel)

    q_g, o_c = [], []
    for g in range(NSA_KV_HEADS):
        pieces = []
        for hg in range(HEADS_PER_KV):
            h = g * HEADS_PER_KV + hg
            qh = qT_ref[0, h * HEAD_DIM:(h + 1) * HEAD_DIM, :] * jnp.asarray(HEAD_DIM ** -0.5, bf16)
            z = jnp.zeros_like(qh)
            pieces.append(jnp.concatenate([qh, z] if g == 0 else [z, qh], axis=0))
        qTp = jnp.concatenate(pieces, axis=1)
        q_g.append(qTp)
        rows = slice(g * HEAD_DIM, (g + 1) * HEAD_DIM)

        s_c = jnp.dot(kc_ref[0], qTp, preferred_element_type=f32)
        n_iota = lax.broadcasted_iota(jnp.int32, (n_cmp_pad, QL), 0)
        mask_c = (n_iota * CMP_STRIDE + (CMP_BLOCK - 1)) <= t_row
        s_c = jnp.where(mask_c, s_c, NEG_INF)
        m_c = jnp.max(s_c, axis=0, keepdims=True)
        p_c = jnp.where(mask_c, jnp.exp(s_c - m_c), 0.0)
        l_c = jnp.sum(p_c, axis=0, keepdims=True)
        p_c = p_c * jnp.where(l_c > 0.0, 1.0 / l_c, 0.0)
        p_cb = p_c.astype(bf16)
        o_c.append(jnp.dot(vcT_ref[0], p_cb, preferred_element_type=f32)[rows])
        imp4 = jnp.dot(mselT_ref[...], p_cb, preferred_element_type=f32)
        imp = imp4[:, 0:Q_BLOCK]
        for hg in range(1, HEADS_PER_KV):
            imp = imp + imp4[:, hg * Q_BLOCK:(hg + 1) * Q_BLOCK]

        score = jnp.where(valid, jnp.where(forced, FORCE_SCORE, imp), NEG_INF)
        score_ref[...] = score

        def rank_body(i8, rank, score=score):
            base = pl.multiple_of(i8 * SUBLANES, SUBLANES)
            rows = score_ref[pl.ds(base, SUBLANES), :]
            for u in range(SUBLANES):
                row = rows[u:u + 1, :]
                beats = (row > score) | ((row == score) & (base + u < m_iota))
                rank = rank + beats.astype(jnp.int32)
            return rank

        rank = lax.fori_loop(0, (n_comp + SUBLANES - 1) // SUBLANES, rank_body,
                             jnp.zeros((n_sel, Q_BLOCK), jnp.int32))
        sel_ref[g] = jnp.where((rank < top_n) & valid, 0.0, NEG_INF)

    def sel_tile(j, carry, causal, size=SEL_TILE):
        k_tile = ks_ref[0, j, 0:size, :]
        v_tile = vsT_ref[0, j, :, 0:size]
        blk0 = pl.multiple_of(j * BLOCKS_PER_TILE, BLOCKS_PER_TILE)
        out = []
        for g in range(NSA_KV_HEADS):
            m_i, l_i, acc = carry[g]
            s = jnp.dot(k_tile, q_g[g], preferred_element_type=f32)
            selrows = sel_ref[g, pl.ds(blk0, BLOCKS_PER_TILE), :]
            bias = jnp.concatenate(
                [jnp.broadcast_to(selrows[r:r + 1, :], (SEL_BLOCK, Q_BLOCK)) for r in range(size // SEL_BLOCK)], axis=0)
            if causal:
                key = j * SEL_TILE + lax.broadcasted_iota(jnp.int32, (size, Q_BLOCK), 0)
                bias = jnp.where(key <= t_row_q, bias, NEG_INF)
            s = s + jnp.concatenate([bias] * HEADS_PER_KV, axis=1)
            m_new = jnp.maximum(m_i, jnp.max(s, axis=0, keepdims=True))
            alpha = jnp.exp(m_i - m_new)
            p = jnp.exp(s - m_new)
            l_new = alpha * l_i + jnp.sum(p, axis=0, keepdims=True)
            pv = jnp.dot(v_tile, p.astype(bf16), preferred_element_type=f32)
            out.append((m_new, l_new, alpha * acc + pv))
        return tuple(out)

    n_full = q0 // SEL_TILE
    init = tuple((jnp.full((1, QL), NEG_INF, f32), jnp.zeros((1, QL), f32), jnp.zeros((KV_COLS, QL), f32))
                 for _ in range(NSA_KV_HEADS))
    carry = lax.fori_loop(0, n_full, functools.partial(sel_tile, causal=False), init)
    carry = lax.cond(
        q0 % SEL_TILE + Q_BLOCK <= SEL_TILE // 2,
        functools.partial(sel_tile, n_full, causal=True, size=SEL_TILE // 2),
        functools.partial(sel_tile, n_full, causal=True, size=SEL_TILE),
        carry)

    start = pl.multiple_of(jnp.maximum(q0 - WINDOW, 0), Q_BLOCK)
    j0 = start // Q_BLOCK
    k_win = kw_ref[0, pl.ds(start, WIN_SPAN), :]
    key_w = start + lax.broadcasted_iota(jnp.int32, (WIN_SPAN, Q_BLOCK), 0)
    bias_w = jnp.where(key_w <= t_row_q, 0.0, NEG_INF)
    bias_w = jnp.where(key_w > t_row_q - WINDOW, bias_w, NEG_INF)
    bias_w = jnp.concatenate([bias_w] * HEADS_PER_KV, axis=1)
    for g in range(NSA_KV_HEADS):
        rows = slice(g * HEAD_DIM, (g + 1) * HEAD_DIM)
        m_s, l_s, acc_s = carry[g]
        o_s = acc_s[rows] * (1.0 / l_s)
        s_w = jnp.dot(k_win, q_g[g], preferred_element_type=f32) + bias_w
        m_w = jnp.max(s_w, axis=0, keepdims=True)
        p_w = jnp.exp(s_w - m_w)
        l_w = jnp.sum(p_w, axis=0, keepdims=True)
        p_wb = p_w.astype(bf16)
        acc_w = jnp.zeros((KV_COLS, QL), f32)
        for i in range(WIN_SPAN // Q_BLOCK):
            acc_w = acc_w + jnp.dot(vwT_ref[0, j0 + i], p_wb[i * Q_BLOCK:(i + 1) * Q_BLOCK, :], preferred_element_type=f32)
        o_w = acc_w[rows] * (1.0 / l_w)

        for pair in range(HEADS_PER_KV // 2):
            halves = []
            for hg in (2 * pair, 2 * pair + 1):
                h = g * HEADS_PER_KV + hg
                lanes = slice(hg * Q_BLOCK, (hg + 1) * Q_BLOCK)
                gate = jax.nn.sigmoid(gT_ref[0, 3 * h:3 * h + 3, :])
                halves.append(gate[0:1] * o_c[g][:, lanes] + gate[1:2] * o_s[:, lanes] + gate[2:3] * o_w[:, lanes])
            both = jnp.concatenate(halves, axis=0)
            col0 = (g * HEADS_PER_KV + 2 * pair) * HEAD_DIM
            o_ref[0, :, col0:col0 + 2 * HEAD_DIM] = both.T.astype(o_ref.dtype)


def _nsa_attention(qT, gT, kc, vcT, ks, vsT, kw, vwT):
    b, hd, s = qT.shape
    n_sel = s // SEL_BLOCK
    nc = kc.shape[1]
    n_cmp = (s - CMP_BLOCK) // CMP_STRIDE + 1
    mselT = jnp.asarray(np.pad(_cmp_to_sel_matrix(n_cmp, n_sel).T, ((0, 0), (0, nc - n_cmp))), jnp.bfloat16)
    ks4 = ks.reshape(b, s // SEL_TILE, SEL_TILE, KV_COLS)
    return pl.pallas_call(
        functools.partial(_nsa_kernel, n_sel=n_sel),
        name="nsa_attention",
        grid=(b, s // Q_BLOCK),
        in_specs=[
            pl.BlockSpec((1, hd, Q_BLOCK), lambda i, c: (i, 0, c)),
            pl.BlockSpec((1, 3 * NSA_HEADS, Q_BLOCK), lambda i, c: (i, 0, c)),
            pl.BlockSpec((1, nc, KV_COLS), lambda i, c: (i, 0, 0)),
            pl.BlockSpec((1, KV_COLS, nc), lambda i, c: (i, 0, 0)),
            pl.BlockSpec((n_sel, nc), lambda i, c: (0, 0)),
            pl.BlockSpec((1, s // SEL_TILE, SEL_TILE, KV_COLS), lambda i, c: (i, 0, 0, 0)),
            pl.BlockSpec((1, s // SEL_TILE, KV_COLS, SEL_TILE), lambda i, c: (i, 0, 0, 0)),
            pl.BlockSpec((1, s, KV_COLS), lambda i, c: (i, 0, 0)),
            pl.BlockSpec((1, s // Q_BLOCK, KV_COLS, Q_BLOCK), lambda i, c: (i, 0, 0, 0)),
        ],
        out_specs=pl.BlockSpec((1, Q_BLOCK, hd), lambda i, c: (i, c, 0)),
        out_shape=jax.ShapeDtypeStruct((b, s, hd), jnp.bfloat16),
        scratch_shapes=[pltpu.VMEM((n_sel, Q_BLOCK), jnp.float32),
                        pltpu.VMEM((NSA_KV_HEADS, n_sel, Q_BLOCK), jnp.float32)],
        compiler_params=pltpu.CompilerParams(dimension_semantics=("arbitrary", "arbitrary")),
    )(qT, gT, kc, vcT, mselT, ks4, vsT, kw, vwT)


def _route_tile(xb, first_step, wT_ref, b_ref, tri_ref, route_ref, cnt_ref, run_ref):
    f32 = jnp.float32

    @pl.when(first_step)
    def _():
        run_ref[...] = jnp.zeros_like(run_ref)

    tm = xb.shape[0]
    logits = lax.dot_general(wT_ref[...], xb, _NT, preferred_element_type=f32) + b_ref[...]
    sub = lax.broadcasted_iota(jnp.int32, (LANES, tm), 0)
    is_g = sub < N_GROUPS
    gl = jnp.where(is_g, logits, NEG_INF)
    g_max = jnp.max(gl, axis=0, keepdims=True)
    g_star = jnp.min(jnp.where(gl == g_max, sub, LANES), axis=0, keepdims=True)
    p_group = 1.0 / jnp.sum(jnp.where(is_g, jnp.exp(gl - g_max), 0.0), axis=0, keepdims=True)
    lo = N_GROUPS + g_star * EXPERTS_PER_GROUP
    in_grp = (sub >= lo) & (sub < lo + EXPERTS_PER_GROUP)
    el = jnp.where(in_grp, logits, NEG_INF)
    v1 = jnp.max(el, axis=0, keepdims=True)
    i1 = jnp.min(jnp.where(el == v1, sub, LANES), axis=0, keepdims=True)
    el2 = jnp.where(sub == i1, NEG_INF, el)
    v2 = jnp.max(el2, axis=0, keepdims=True)
    i2 = jnp.min(jnp.where(el2 == v2, sub, LANES), axis=0, keepdims=True)
    e21 = jnp.exp(v2 - v1)
    gate1 = p_group * (1.0 / (1.0 + e21))
    gate2 = p_group * (e21 / (1.0 + e21))
    oh1 = (sub == i1).astype(f32)
    oh2 = (sub == i2).astype(f32)
    both = oh1 + oh2
    before = jnp.dot(both.astype(jnp.bfloat16), tri_ref[...], preferred_element_type=f32) + run_ref[...]
    rank1 = jnp.sum(oh1 * before, axis=0, keepdims=True)
    rank2 = jnp.sum(oh2 * before, axis=0, keepdims=True)
    run_ref[...] = run_ref[...] + jnp.sum(both, axis=1, keepdims=True)
    cnt_ref[...] = run_ref[...]
    zero = jnp.zeros_like(gate1)
    route_ref[...] = jnp.concatenate(
        [gate1, gate2, (i1 - N_GROUPS).astype(f32), (i2 - N_GROUPS).astype(f32), rank1, rank2, zero, zero], axis=0)


def _router_operands(wg, bg, we, be, tile):
    d = wg.shape[0]
    pad = LANES - N_GROUPS - N_EXPERTS
    w_t = jnp.concatenate([wg, we.reshape(d, N_EXPERTS), jnp.zeros((d, pad), wg.dtype)], axis=1).T
    bias = jnp.concatenate([bg, be.reshape(N_EXPERTS), jnp.zeros((pad,), bg.dtype)])[:, None]
    tri = jnp.asarray(np.triu(np.ones((tile, tile), np.float32), 1), jnp.bfloat16)
    return w_t.astype(jnp.bfloat16), bias, tri


def _dispatch_kernel(zt_ref, dest_ref, x_ref, xd_hbm, zbuf, ring, sem, zsem):
    tm = x_ref.shape[0]

    @pl.when(pl.program_id(0) == 0)
    def _():
        zbuf[...] = jnp.zeros_like(zbuf)

        def zero_copy(k):
            start = pl.multiple_of(jnp.maximum(zt_ref[k], 0) * EXPERT_TILE, EXPERT_TILE)
            return pltpu.make_async_copy(zbuf, xd_hbm.at[pl.ds(start, EXPERT_TILE)], zsem)

        def start_body(k, c):
            @pl.when(zt_ref[k] >= 0)
            def _():
                zero_copy(k).start()
            return c

        def wait_body(k, c):
            @pl.when(zt_ref[k] >= 0)
            def _():
                zero_copy(k).wait()
            return c

        lax.fori_loop(0, zt_ref.shape[0], start_body, 0)
        lax.fori_loop(0, zt_ref.shape[0], wait_body, 0)

    i = pl.program_id(0)
    n = pl.num_programs(0)
    slot = i % 2

    def wait_tile(of_slot):
        for k in range(EXPERT_TOP_K):
            pltpu.make_async_copy(ring.at[of_slot], xd_hbm.at[pl.ds(0, tm)], sem.at[of_slot]).wait()

    @pl.when(i >= 2)
    def _():
        wait_tile(slot)

    ring[slot] = x_ref[...].reshape(ring.shape[1:])

    def body(r, c):
        for k in range(EXPERT_TOP_K):
            pltpu.make_async_copy(
                ring.at[slot, r], xd_hbm.at[dest_ref[0, 0, k * tm + r]], sem.at[slot]).start(priority=k)
        return c

    lax.fori_loop(0, tm, body, 0, unroll=8)

    @pl.when(i == n - 1)
    def _():
        wait_tile(slot)

        @pl.when(n > 1)
        def _():
            wait_tile(1 - slot)


def _moe_dispatch(xt, dest2, zero_tiles, n_rows):
    t, d = xt.shape
    row = (SUBLANES, d // SUBLANES)
    grid_spec = pltpu.PrefetchScalarGridSpec(
        num_scalar_prefetch=1,
        grid=(t // ROUTE_TILE,),
        in_specs=[pl.BlockSpec((1, 1, EXPERT_TOP_K * ROUTE_TILE), lambda i, zt: (i, 0, 0), memory_space=pltpu.SMEM),
                  pl.BlockSpec((ROUTE_TILE, d), lambda i, zt: (i, 0))],
        out_specs=pl.BlockSpec(memory_space=pl.ANY),
        scratch_shapes=[pltpu.VMEM((EXPERT_TILE,) + row, xt.dtype), pltpu.VMEM((2, ROUTE_TILE) + row, xt.dtype),
                        pltpu.SemaphoreType.DMA((2,)), pltpu.SemaphoreType.DMA(())],
    )
    return pl.pallas_call(
        _dispatch_kernel,
        name="moe_dispatch",
        grid_spec=grid_spec,
        out_shape=jax.ShapeDtypeStruct((n_rows,) + row, xt.dtype),
        compiler_params=pltpu.CompilerParams(dimension_semantics=("arbitrary",), has_side_effects=True),
    )(zero_tiles, dest2, xt)


def _expert_kernel(te_ref, nu_ref, xd_ref, w1_ref, w3_ref, w2_ref, y_ref, w1b, w3b, w2b):
    i = pl.program_id(0)
    used = i < nu_ref[0]
    new_expert = (i == 0) | (te_ref[i] != te_ref[jnp.maximum(i - 1, 0)])

    @pl.when(used & new_expert)
    def _():
        w1b[...] = w1_ref[0, 0].astype(jnp.bfloat16)
        w3b[...] = w3_ref[0, 0].astype(jnp.bfloat16)
        w2b[...] = w2_ref[0, 0].astype(jnp.bfloat16)

    @pl.when(used)
    def _():
        xb = xd_ref[...].reshape(xd_ref.shape[0], w1b.shape[0]).astype(jnp.bfloat16)
        h1 = jnp.dot(xb, w1b[...], preferred_element_type=jnp.float32)
        h3 = jnp.dot(xb, w3b[...], preferred_element_type=jnp.float32)
        a = (h1 * jax.nn.sigmoid(h1) * h3).astype(jnp.bfloat16)
        y_ref[...] = jnp.dot(a, w2b[...], preferred_element_type=jnp.float32).reshape(y_ref.shape)

    @pl.when(jnp.logical_not(used))
    def _():
        y_ref[...] = jnp.zeros_like(y_ref)


def _moe_experts(x_disp, tile_expert, n_used, w1, w3, w2, layer):
    n_rows = x_disp.shape[0]
    row = x_disp.shape[1:]
    d = row[0] * row[1]
    n_tiles = n_rows // EXPERT_TILE
    hid = w1.shape[3]

    def row_map(i, te, nu):
        return (i, 0, 0)

    def w_map(i, te, nu):
        return (layer, te[i], 0, 0)

    grid_spec = pltpu.PrefetchScalarGridSpec(
        num_scalar_prefetch=2,
        grid=(n_tiles,),
        in_specs=[pl.BlockSpec((EXPERT_TILE,) + row, row_map),
                  pl.BlockSpec((1, 1, d, hid), w_map),
                  pl.BlockSpec((1, 1, d, hid), w_map),
                  pl.BlockSpec((1, 1, hid, d), w_map)],
        out_specs=pl.BlockSpec((EXPERT_TILE,) + row, row_map),
        scratch_shapes=[pltpu.VMEM((d, hid), jnp.bfloat16), pltpu.VMEM((d, hid), jnp.bfloat16),
                        pltpu.VMEM((hid, d), jnp.bfloat16)],
    )
    return pl.pallas_call(
        _expert_kernel,
        name="moe_experts",
        grid_spec=grid_spec,
        out_shape=jax.ShapeDtypeStruct((n_rows,) + row, jnp.float32),
        compiler_params=pltpu.CompilerParams(dimension_semantics=("arbitrary",), vmem_limit_bytes=VMEM_LIMIT),
    )(tile_expert, n_used, x_disp, w1, w3, w2)


def _combine_kernel(dest_ref, dest_next_ref, x_ref, route_ref, g_ref, b_ref, yd_hbm, o_ref, ybuf, sem):
    i = pl.program_id(0)
    n = pl.num_programs(0)
    tm, d = x_ref.shape
    rows = EXPERT_TOP_K * tm
    slot = i % 2

    def row_copy(dref, row, to_slot):
        return pltpu.make_async_copy(yd_hbm.at[dref[0, 0, row]], ybuf.at[to_slot, row], sem.at[to_slot])

    def wait_tile(of_slot):
        pltpu.make_async_copy(yd_hbm.at[pl.ds(0, rows)], ybuf.at[of_slot], sem.at[of_slot]).wait()

    @pl.when(i == 0)
    def _():
        def body(r, c):
            for k in range(EXPERT_TOP_K):
                row_copy(dest_ref, k * tm + r, 0).start(priority=k)
            return c
        lax.fori_loop(0, tm, body, 0, unroll=8)

    wait_tile(slot)

    def chunk(c, carry):
        r0 = pl.multiple_of(c * COMBINE_CHUNK, COMBINE_CHUNK)
        y1 = ybuf[slot, pl.ds(r0, COMBINE_CHUNK)].reshape(COMBINE_CHUNK, d)
        y2 = ybuf[slot, pl.ds(tm + r0, COMBINE_CHUNK)].reshape(COMBINE_CHUNK, d)
        xv = x_ref[pl.ds(r0, COMBINE_CHUNK), :]
        gate = route_ref[:, pl.ds(r0, COMBINE_CHUNK)].T
        for u in range(COMBINE_CHUNK):
            for k in range(EXPERT_TOP_K):
                row_copy(dest_next_ref, k * tm + r0 + u, 1 - slot).start(priority=k)
        y = DN_ALPHA * xv + (y1 * gate[:, 0:1] + y2 * gate[:, 1:2])
        o_ref[pl.ds(r0, COMBINE_CHUNK), :] = _ln_rows(y, g_ref[...], b_ref[...])
        return carry

    lax.fori_loop(0, tm // COMBINE_CHUNK, chunk, 0)

    @pl.when(i == n - 1)
    def _():
        wait_tile(1 - slot)


def _moe_combine_ln(xt, y_disp, dest2, route, ln_g, ln_b):
    t, d = xt.shape
    n = t // ROUTE_TILE
    rows = EXPERT_TOP_K * ROUTE_TILE
    return pl.pallas_call(
        _combine_kernel,
        name="moe_combine_ln",
        grid=(n,),
        in_specs=[pl.BlockSpec((1, 1, rows), lambda i: (i, 0, 0), memory_space=pltpu.SMEM),
                  pl.BlockSpec((1, 1, rows), lambda i: (jnp.minimum(i + 1, n - 1), 0, 0), memory_space=pltpu.SMEM),
                  pl.BlockSpec((ROUTE_TILE, d), lambda i: (i, 0)),
                  pl.BlockSpec((ROUTE_ROWS, ROUTE_TILE), lambda i: (0, i)),
                  pl.BlockSpec((1, d), lambda i: (0, 0)),
                  pl.BlockSpec((1, d), lambda i: (0, 0)),
                  pl.BlockSpec(memory_space=pl.ANY)],
        out_specs=pl.BlockSpec((ROUTE_TILE, d), lambda i: (i, 0)),
        out_shape=jax.ShapeDtypeStruct((t, d), jnp.float32),
        scratch_shapes=[pltpu.VMEM((2, rows) + y_disp.shape[1:], jnp.float32), pltpu.SemaphoreType.DMA((2,))],
        compiler_params=pltpu.CompilerParams(dimension_semantics=("arbitrary",), vmem_limit_bytes=VMEM_LIMIT),
    )(dest2, dest2, xt, route, ln_g.reshape(1, d), ln_b.reshape(1, d), y_disp)


def _moe_sublayer(x, route, cnt, w1, w3, w2, layer, ln_g, ln_b):
    b, s, d = x.shape
    t = b * s
    xt = x.reshape(t, d)
    counts = cnt[N_GROUPS:N_GROUPS + N_EXPERTS, 0].astype(jnp.int32)
    n_tiles = (t * EXPERT_TOP_K) // EXPERT_TILE + N_EXPERTS
    tiles_per = (counts + EXPERT_TILE - 1) // EXPERT_TILE
    tile_end = jnp.cumsum(tiles_per)
    pad_start = (tile_end - tiles_per) * EXPERT_TILE
    n_used = tile_end[-1:].astype(jnp.int32)
    tile_ids = jnp.minimum(jnp.arange(n_tiles), n_used[0] - 1)
    tile_expert = jnp.sum(tile_ids[:, None] >= tile_end[None, :], axis=1).astype(jnp.int32)
    experts = route[2:4].astype(jnp.int32)
    first_row = jnp.sum(jnp.where(experts[..., None] == jnp.arange(N_EXPERTS), pad_start, 0), axis=-1)
    dest = first_row + route[4:6].astype(jnp.int32)
    dest2 = jnp.swapaxes(dest.reshape(EXPERT_TOP_K, t // ROUTE_TILE, ROUTE_TILE), 0, 1)
    dest2 = dest2.reshape(t // ROUTE_TILE, 1, EXPERT_TOP_K * ROUTE_TILE)
    last_tiles = jnp.where(tiles_per > 0, tile_end - 1, -1)
    tail_tiles = n_used[0] + jnp.arange(N_EXPERTS)
    tail_tiles = jnp.where(tail_tiles < n_tiles, tail_tiles, -1)
    zero_tiles = jnp.concatenate([last_tiles, tail_tiles]).astype(jnp.int32)
    x_disp = _moe_dispatch(xt, dest2, zero_tiles, n_tiles * EXPERT_TILE)
    y_disp = _moe_experts(x_disp, tile_expert, n_used, w1, w3, w2, layer)
    return _moe_combine_ln(xt, y_disp, dest2, route, ln_g, ln_b).reshape(b, s, d)


def kernel(x, mem, mem_wk, mem_wv, ev_w_in, ev_conv_w, ev_conv_b, ev_cnorm_g, ev_cnorm_b, ev_cmp_pe_k, ev_cmp_w1_k, ev_cmp_w2_k, ev_cmp_pe_v, ev_cmp_w1_v, ev_cmp_w2_v, ev_w_out, od_w_in, od_conv_w, od_w_out, ln_mix_g, ln_mix_b, xa_wq, xa_wo, ln_xa_g, ln_xa_b, moe_wg, moe_bg, moe_we, moe_be, moe_w1, moe_w3, moe_w2, ln_ffn_g, ln_ffn_b):
    b, s, d = x.shape
    mem_k, mem_v = _mem_kv(mem, mem_wk, mem_wv)
    for layer in range(DEPTH):
        i = layer // 2
        if layer % 2 == 0:
            a, kv_in, k_sel, k_win, q_t, v_sel_t, v_win_t, gate_t = _even_in_proj(x, ev_w_in[i])
            a = _conformer_conv(a, ev_conv_w[i], ev_conv_b[i], ev_cnorm_g[i], ev_cnorm_b[i])
            k_cmp, v_cmp_t = _compress_kv(kv_in, ev_cmp_pe_k[i], ev_cmp_w1_k[i], ev_cmp_w2_k[i], ev_cmp_pe_v[i], ev_cmp_w1_v[i], ev_cmp_w2_v[i])
            o = _nsa_attention(q_t, gate_t, k_cmp, v_cmp_t, k_sel, v_sel_t, k_win, v_win_t)
            x = _proj_residual_ln(a.reshape(b * s, -1), o.reshape(b * s, -1), x.reshape(b * s, d), ev_w_out[i], ln_mix_g[layer], ln_mix_b[layer]).reshape(b, s, d)
        else:
            x = _odd_mixer_sublayer(x, od_w_in[i], od_conv_w[i], od_w_out[i], ln_mix_g[layer], ln_mix_b[layer])
        x, route, cnt = _xattn_router_sublayer(x, mem_k, mem_v, xa_wq[layer], xa_wo[layer], ln_xa_g[layer], ln_xa_b[layer],
                                               moe_wg[layer], moe_bg[layer], moe_we[layer], moe_be[layer])
        x = _moe_sublayer(x, route, cnt, moe_w1, moe_w3, moe_w2, layer, ln_ffn_g[layer], ln_ffn_b[layer])
    return x
```

```python
import functools

import numpy as np
import jax
import jax.numpy as jnp
from jax import lax
from jax.experimental import pallas as pl
from jax.experimental.pallas import tpu as pltpu

D_MODEL = 1024
DEPTH = 2
CONV_CH = D_MODEL // 2
CONV_WIDTH = 31
NSA_HEADS = 8
NSA_KV_HEADS = 2
HEAD_DIM = (D_MODEL // 2) // NSA_HEADS
CMP_BLOCK = 32
CMP_STRIDE = 16
SEL_BLOCK = 64
SEL_TOP_N = 16
WINDOW = 512
Q_BLOCK = 256
FORCE_SCORE = 1e4
SHORT_CONV_WIDTH = 3
XA_HEADS = 4
XA_HEAD_DIM = D_MODEL // XA_HEADS
N_GROUPS = 4
EXPERTS_PER_GROUP = 8
N_EXPERTS = N_GROUPS * EXPERTS_PER_GROUP
EXPERT_TOP_K = 2
DN_ALPHA = (2 * DEPTH) ** 0.25
LN_EPS = 1e-5
NEG_INF = -1e30
KV_COLS = NSA_KV_HEADS * HEAD_DIM
QCOLS = NSA_HEADS * HEAD_DIM
GATE_ROWS = 3 * NSA_HEADS

LANES = 128
SUBLANES = 8
HEADS_PER_KV = NSA_HEADS // NSA_KV_HEADS
QL = Q_BLOCK * HEADS_PER_KV
SEL_TILE = 1024
WIN_SPAN = WINDOW + Q_BLOCK
BLOCKS_PER_TILE = SEL_TILE // SEL_BLOCK
SEQ_TILE = 1024
HALO = 32
ODD_HALO = 8
ROUTE_TILE = 512
EXPERT_TILE = 512
ROUTE_ROWS = 8
VMEM_LIMIT = 56 * 1024 * 1024

_NT = (((1,), (1,)), ((), ()))


def _ln_rows(y, g, b):
    mu = jnp.mean(y, axis=-1, keepdims=True)
    yc = y - mu
    var = jnp.mean(yc * yc, axis=-1, keepdims=True)
    return yc * lax.rsqrt(var + LN_EPS) * g + b


def _even_in_kernel(x_ref, wa_ref, wkv_ref, wk2_ref, wqT_ref, wvT_ref, wgT_ref,
                    a_ref, kv_ref, ks_ref, kw_ref, qT_ref, vsT_ref, vwT_ref, gT_ref):
    f32 = jnp.float32
    bf16 = jnp.bfloat16
    xb = x_ref[0].astype(bf16)
    av = jnp.dot(xb, wa_ref[...], preferred_element_type=f32)
    a_ref[0] = av[:, :CONV_CH] * jax.nn.sigmoid(av[:, CONV_CH:])
    kv_ref[0] = jnp.dot(xb, wkv_ref[...], preferred_element_type=f32)
    k2 = jnp.dot(xb, wk2_ref[...], preferred_element_type=f32)
    ks_ref[0] = k2[:, :KV_COLS].astype(bf16)
    kw_ref[0] = k2[:, KV_COLS:].astype(bf16)
    qT_ref[0] = lax.dot_general(wqT_ref[...], xb, _NT, preferred_element_type=f32).astype(bf16)
    vT = lax.dot_general(wvT_ref[...], xb, _NT, preferred_element_type=f32).astype(bf16)
    vsT_ref[0, 0] = vT[:KV_COLS]
    for j in range(SEQ_TILE // Q_BLOCK):
        vwT_ref[0, j] = vT[KV_COLS:, j * Q_BLOCK:(j + 1) * Q_BLOCK]
    gT_ref[0] = lax.dot_general(wgT_ref[...], xb, _NT, preferred_element_type=f32)


def _even_in_proj(x, w_in):
    b, s, d = x.shape
    bf16 = jnp.bfloat16
    c = np.cumsum((0, CONV_CH, CONV_CH, QCOLS, KV_COLS, KV_COLS, KV_COLS, KV_COLS, KV_COLS, KV_COLS, GATE_ROWS))
    col = lambda i, j: w_in[:, c[i]:c[j]]
    wa = col(0, 2).astype(bf16)
    wq_t = col(2, 3).T.astype(bf16)
    wkv = col(3, 5).astype(bf16)
    wk2 = jnp.concatenate([col(5, 6), col(7, 8)], axis=1).astype(bf16)
    wv_t = jnp.concatenate([col(6, 7), col(8, 9)], axis=1).T.astype(bf16)
    wg_t = col(9, 10).T.astype(bf16)
    ts = SEQ_TILE
    per_sel = SEL_TILE // ts
    full = lambda shape: pl.BlockSpec(shape, lambda i, j: (0,) * len(shape))
    return pl.pallas_call(
        _even_in_kernel,
        name="even_in_proj",
        grid=(b, s // ts),
        in_specs=[pl.BlockSpec((1, ts, d), lambda i, j: (i, j, 0)),
                  full(wa.shape), full(wkv.shape), full(wk2.shape), full(wq_t.shape), full(wv_t.shape), full(wg_t.shape)],
        out_specs=[pl.BlockSpec((1, ts, CONV_CH), lambda i, j: (i, j, 0)),
                   pl.BlockSpec((1, ts, 2 * KV_COLS), lambda i, j: (i, j, 0)),
                   pl.BlockSpec((1, ts, KV_COLS), lambda i, j: (i, j, 0)),
                   pl.BlockSpec((1, ts, KV_COLS), lambda i, j: (i, j, 0)),
                   pl.BlockSpec((1, QCOLS, ts), lambda i, j: (i, 0, j)),
                   pl.BlockSpec((1, 1, KV_COLS, ts), lambda i, j: (i, j // per_sel, 0, j % per_sel)),
                   pl.BlockSpec((1, ts // Q_BLOCK, KV_COLS, Q_BLOCK), lambda i, j: (i, j, 0, 0)),
                   pl.BlockSpec((1, GATE_ROWS, ts), lambda i, j: (i, 0, j))],
        out_shape=[jax.ShapeDtypeStruct((b, s, CONV_CH), jnp.float32),
                   jax.ShapeDtypeStruct((b, s, 2 * KV_COLS), jnp.float32),
                   jax.ShapeDtypeStruct((b, s, KV_COLS), bf16),
                   jax.ShapeDtypeStruct((b, s, KV_COLS), bf16),
                   jax.ShapeDtypeStruct((b, QCOLS, s), bf16),
                   jax.ShapeDtypeStruct((b, s // SEL_TILE, KV_COLS, SEL_TILE), bf16),
                   jax.ShapeDtypeStruct((b, s // Q_BLOCK, KV_COLS, Q_BLOCK), bf16),
                   jax.ShapeDtypeStruct((b, GATE_ROWS, s), jnp.float32)],
        compiler_params=pltpu.CompilerParams(dimension_semantics=("arbitrary", "arbitrary"), vmem_limit_bytes=VMEM_LIMIT),
    )(x, wa, wkv, wk2, wq_t, wv_t, wg_t)


def _conv_kernel(cur_ref, halo_ref, w_ref, cb_ref, g_ref, b_ref, o_ref, ext_ref, win_ref):
    j = pl.program_id(1)
    ts = cur_ref.shape[1]
    halo = halo_ref[0]
    ext_ref[0:HALO, :] = jnp.where(j > 0, halo, jnp.zeros_like(halo))
    ext_ref[HALO:HALO + ts, :] = cur_ref[0]
    first = HALO - (CONV_WIDTH - 1)
    acc = jnp.zeros((ts, CONV_CH), jnp.float32)
    for p in range(SUBLANES):
        n_a = len(range(p, CONV_WIDTH, SUBLANES))
        rows = ts + SUBLANES * (n_a - 1)
        win_ref[0:rows, :] = ext_ref[first + p:first + p + rows, :]
        for a in range(n_a):
            k = SUBLANES * a + p
            acc = acc + w_ref[k:k + 1, :] * win_ref[SUBLANES * a:SUBLANES * a + ts, :]
    y = _ln_rows(acc + cb_ref[...], g_ref[...], b_ref[...])
    o_ref[0] = (y * jax.nn.sigmoid(y)).astype(o_ref.dtype)


def _conformer_conv(a, conv_w, conv_b, cn_g, cn_b):
    b, s, c = a.shape
    ts = SEQ_TILE
    per = ts // HALO
    row = lambda v: v.reshape(1, c)
    return pl.pallas_call(
        _conv_kernel,
        name="conformer_conv",
        grid=(b, s // ts),
        in_specs=[pl.BlockSpec((1, ts, c), lambda i, j: (i, j, 0)),
                  pl.BlockSpec((1, HALO, c), lambda i, j: (i, jnp.maximum(j * per - 1, 0), 0)),
                  pl.BlockSpec((CONV_WIDTH, c), lambda i, j: (0, 0)),
                  pl.BlockSpec((1, c), lambda i, j: (0, 0)),
                  pl.BlockSpec((1, c), lambda i, j: (0, 0)),
                  pl.BlockSpec((1, c), lambda i, j: (0, 0))],
        out_specs=pl.BlockSpec((1, ts, c), lambda i, j: (i, j, 0)),
        out_shape=jax.ShapeDtypeStruct((b, s, c), jnp.bfloat16),
        scratch_shapes=[pltpu.VMEM((HALO + ts, c), jnp.float32), pltpu.VMEM((HALO + ts, c), jnp.float32)],
        compiler_params=pltpu.CompilerParams(dimension_semantics=("arbitrary", "arbitrary")),
    )(a, a, conv_w, row(conv_b), row(cn_g), row(cn_b))


def _compress_kernel(r_ref, pe_ref, w1_ref, w2_ref, w2T_ref, o_ref, oT_ref):
    f32 = jnp.float32
    bf16 = jnp.bfloat16
    nch = o_ref.shape[1]
    hw = w1_ref.shape[3]
    first = jnp.zeros((nch, hw), f32)
    second = jnp.zeros((nch, hw), f32)
    for l in range(CMP_STRIDE):
        tok = r_ref[pl.ds(l, nch, stride=CMP_STRIDE), :]
        first = first + jnp.dot((tok + pe_ref[0, l:l + 1, :]).astype(bf16), w1_ref[0, l], preferred_element_type=f32)
        second = second + jnp.dot((tok + pe_ref[0, CMP_STRIDE + l:CMP_STRIDE + l + 1, :]).astype(bf16),
                                  w1_ref[0, CMP_STRIDE + l], preferred_element_type=f32)
    second_next = jnp.concatenate([second[1:], jnp.zeros_like(second[0:1])], axis=0)
    hid = jax.nn.gelu(first + second_next).astype(bf16)
    o_ref[0] = jnp.dot(hid, w2_ref[0], preferred_element_type=f32).astype(bf16)
    oT_ref[0] = lax.dot_general(w2T_ref[0], hid, _NT, preferred_element_type=f32).astype(bf16)


def _compress_kv(kv_in, pe_k, w1_k, w2_k, pe_v, w1_v, w2_v):
    b, s, _ = kv_in.shape
    bf16 = jnp.bfloat16
    nch = s // CMP_STRIDE
    g = NSA_KV_HEADS
    hidden = w1_k.shape[1]

    def expand(pe, w1, w2):
        same = jnp.eye(g, dtype=bool)
        w1r = w1.reshape(CMP_BLOCK, HEAD_DIM, hidden)
        w1e = jnp.where(same[None, :, None, :, None], w1r[:, None, :, None, :], 0.0)
        w2e = jnp.where(same[:, None, :, None], w2[None, :, None, :], 0.0)
        return jnp.tile(pe, (1, g)), w1e.reshape(CMP_BLOCK, KV_COLS, g * hidden), w2e.reshape(g * hidden, KV_COLS)

    pk, w1k, w2k = expand(pe_k, w1_k, w2_k)
    pv, w1v, w2v = expand(pe_v, w1_v, w2_v)
    pe = jnp.stack([pk, pv])
    w1 = jnp.stack([w1k, w1v]).astype(bf16)
    w2 = jnp.stack([w2k, w2v]).astype(bf16)
    w2t = jnp.swapaxes(w2, 1, 2)
    o, o_t = pl.pallas_call(
        _compress_kernel,
        name="compress_kv",
        grid=(b, 2),
        in_specs=[pl.BlockSpec((s, KV_COLS), lambda i, j: (i, j)),
                  pl.BlockSpec((1,) + pe.shape[1:], lambda i, j: (j, 0, 0)),
                  pl.BlockSpec((1,) + w1.shape[1:], lambda i, j: (j, 0, 0, 0)),
                  pl.BlockSpec((1,) + w2.shape[1:], lambda i, j: (j, 0, 0)),
                  pl.BlockSpec((1,) + w2t.shape[1:], lambda i, j: (j, 0, 0))],
        out_specs=[pl.BlockSpec((1, nch, KV_COLS), lambda i, j: (2 * i + j, 0, 0)),
                   pl.BlockSpec((1, KV_COLS, nch), lambda i, j: (2 * i + j, 0, 0))],
        out_shape=[jax.ShapeDtypeStruct((b * 2, nch, KV_COLS), bf16), jax.ShapeDtypeStruct((b * 2, KV_COLS, nch), bf16)],
        compiler_params=pltpu.CompilerParams(dimension_semantics=("arbitrary", "arbitrary"), vmem_limit_bytes=VMEM_LIMIT),
    )(kv_in.reshape(b * s, 2 * KV_COLS), pe, w1, w2, w2t)
    return o.reshape(b, 2, nch, KV_COLS)[:, 0], o_t.reshape(b, 2, KV_COLS, nch)[:, 1]


def _proj_ln_kernel(a_ref, o_ref, x_ref, w_ref, g_ref, b_ref, y_ref):
    lhs = jnp.concatenate([a_ref[...], o_ref[...]], axis=1)
    mix = jnp.dot(lhs, w_ref[...], preferred_element_type=jnp.float32)
    y_ref[...] = _ln_rows(DN_ALPHA * x_ref[...] + mix, g_ref[...], b_ref[...])


def _proj_residual_ln(a, o, x, w, ln_g, ln_b):
    t, d = x.shape
    tm = SEQ_TILE
    return pl.pallas_call(
        _proj_ln_kernel,
        name="proj_residual_ln",
        grid=(t // tm,),
        in_specs=[pl.BlockSpec((tm, a.shape[1]), lambda i: (i, 0)),
                  pl.BlockSpec((tm, o.shape[1]), lambda i: (i, 0)),
                  pl.BlockSpec((tm, d), lambda i: (i, 0)),
                  pl.BlockSpec(w.shape, lambda i: (0, 0)),
                  pl.BlockSpec((1, d), lambda i: (0, 0)),
                  pl.BlockSpec((1, d), lambda i: (0, 0))],
        out_specs=pl.BlockSpec((tm, d), lambda i: (i, 0)),
        out_shape=jax.ShapeDtypeStruct((t, d), jnp.float32),
        compiler_params=pltpu.CompilerParams(dimension_semantics=("arbitrary",), vmem_limit_bytes=VMEM_LIMIT),
    )(a, o, x, w.astype(jnp.bfloat16), ln_g.reshape(1, d), ln_b.reshape(1, d))


def _mem_kv_kernel(m_ref, wk_ref, wv_ref, k_ref, v_ref):
    mb = m_ref[...].astype(jnp.bfloat16)
    k_ref[...] = jnp.dot(mb, wk_ref[...], preferred_element_type=jnp.float32).astype(jnp.bfloat16)
    v_ref[...] = jnp.dot(mb, wv_ref[...], preferred_element_type=jnp.float32).astype(jnp.bfloat16)


def _mem_kv(mem, wk, wv):
    b, m, d = mem.shape
    bf16 = jnp.bfloat16
    k, v = pl.pallas_call(
        _mem_kv_kernel,
        name="mem_kv",
        grid=(b,),
        in_specs=[pl.BlockSpec((m, d), lambda i: (i, 0)), pl.BlockSpec((d, d), lambda i: (0, 0)), pl.BlockSpec((d, d), lambda i: (0, 0))],
        out_specs=[pl.BlockSpec((m, d), lambda i: (i, 0)), pl.BlockSpec((m, d), lambda i: (i, 0))],
        out_shape=[jax.ShapeDtypeStruct((b * m, d), bf16), jax.ShapeDtypeStruct((b * m, d), bf16)],
        compiler_params=pltpu.CompilerParams(dimension_semantics=("arbitrary",), vmem_limit_bytes=VMEM_LIMIT),
    )(mem.reshape(b * m, d), wk.astype(bf16), wv.astype(bf16))
    return k.reshape(b, m, d), v.reshape(b, m, d)


def _xattn_kernel(x_ref, k_ref, v_ref, wq_ref, wo_ref, g_ref, b_ref, rw_ref, rb_ref, tri_ref,
                  y_ref, route_ref, cnt_ref, run_ref):
    f32 = jnp.float32
    bf16 = jnp.bfloat16
    x = x_ref[0]
    q = (jnp.dot(x.astype(bf16), wq_ref[...], preferred_element_type=f32).astype(bf16)
         * jnp.asarray(XA_HEAD_DIM ** -0.5, bf16))
    heads = []
    for h in range(XA_HEADS):
        cols = slice(h * XA_HEAD_DIM, (h + 1) * XA_HEAD_DIM)
        s = lax.dot_general(q[:, cols], k_ref[0, :, cols], _NT, preferred_element_type=f32)
        m = jnp.max(s, axis=1, keepdims=True)
        p = jnp.exp(s - m)
        p = p / jnp.sum(p, axis=1, keepdims=True)
        heads.append(jnp.dot(p.astype(bf16), v_ref[0, :, cols], preferred_element_type=f32).astype(bf16))
    att = jnp.concatenate(heads, axis=1)
    out = jnp.dot(att, wo_ref[...], preferred_element_type=f32)
    y = _ln_rows(DN_ALPHA * x + out, g_ref[...], b_ref[...])
    y_ref[0] = y
    first_step = (pl.program_id(0) == 0) & (pl.program_id(1) == 0)
    _route_tile(y.astype(bf16), first_step, rw_ref, rb_ref, tri_ref, route_ref, cnt_ref, run_ref)


def _xattn_router_sublayer(x, mem_k, mem_v, wq, wo, ln_g, ln_b, wg, bg, we, be):
    b, s, d = x.shape
    m = mem_k.shape[1]
    ts = SEQ_TILE
    per_seq = s // ts
    bf16 = jnp.bfloat16
    rw, rb, tri = _router_operands(wg, bg, we, be, ts)
    const = lambda shape: pl.BlockSpec(shape, lambda i, j: (0,) * len(shape))
    return pl.pallas_call(
        _xattn_kernel,
        name="xattn_router_sublayer",
        grid=(b, per_seq),
        in_specs=[pl.BlockSpec((1, ts, d), lambda i, j: (i, j, 0)),
                  pl.BlockSpec((1, m, d), lambda i, j: (i, 0, 0)),
                  pl.BlockSpec((1, m, d), lambda i, j: (i, 0, 0)),
                  const((d, d)), const((d, d)), const((1, d)), const((1, d)),
                  const(rw.shape), const(rb.shape), const(tri.shape)],
        out_specs=[pl.BlockSpec((1, ts, d), lambda i, j: (i, j, 0)),
                   pl.BlockSpec((ROUTE_ROWS, ts), lambda i, j: (0, i * per_seq + j)),
                   const((LANES, 1))],
        out_shape=[jax.ShapeDtypeStruct((b, s, d), jnp.float32),
                   jax.ShapeDtypeStruct((ROUTE_ROWS, b * s), jnp.float32),
                   jax.ShapeDtypeStruct((LANES, 1), jnp.float32)],
        scratch_shapes=[pltpu.VMEM((LANES, 1), jnp.float32)],
        compiler_params=pltpu.CompilerParams(dimension_semantics=("arbitrary", "arbitrary"), vmem_limit_bytes=VMEM_LIMIT),
    )(x, mem_k, mem_v, wq.astype(bf16), wo.astype(bf16), ln_g.reshape(1, d), ln_b.reshape(1, d), rw, rb, tri)


def _odd_kernel(x_ref, halo_ref, wb_ref, wc_ref, wh_ref, cw_ref, wo_ref, g_ref, b_ref, y_ref):
    f32 = jnp.float32
    bf16 = jnp.bfloat16
    j = pl.program_id(1)
    ts = x_ref.shape[1]
    x = x_ref[0]
    xe = jnp.concatenate([halo_ref[0], x], axis=0).astype(bf16)
    u = (jnp.dot(xe, wc_ref[...], preferred_element_type=f32) * jnp.dot(xe, wh_ref[...], preferred_element_type=f32))
    row = lax.broadcasted_iota(jnp.int32, (ODD_HALO + ts, 1), 0)
    u = jnp.where((row >= ODD_HALO) | (j > 0), u, 0.0)
    conv = jnp.zeros((ts, u.shape[1]), f32)
    for k in range(SHORT_CONV_WIDTH):
        off = ODD_HALO - (SHORT_CONV_WIDTH - 1) + k
        conv = conv + cw_ref[k:k + 1, :] * u[off:off + ts, :]
    gate_b = jnp.dot(xe[ODD_HALO:], wb_ref[...], preferred_element_type=f32)
    mix = jnp.dot((gate_b * conv).astype(bf16), wo_ref[...], preferred_element_type=f32)
    y_ref[0] = _ln_rows(DN_ALPHA * x + mix, g_ref[...], b_ref[...])


def _odd_mixer_sublayer(x, w_in, conv_w, w_out, ln_g, ln_b):
    b, s, d = x.shape
    ts = SEQ_TILE
    per = ts // ODD_HALO
    bf16 = jnp.bfloat16
    wb, wc, wh = (w_in[:, i * d:(i + 1) * d].astype(bf16) for i in range(3))
    full = lambda shape: pl.BlockSpec(shape, lambda i, j: (0,) * len(shape))
    return pl.pallas_call(
        _odd_kernel,
        name="odd_mixer_sublayer",
        grid=(b, s // ts),
        in_specs=[pl.BlockSpec((1, ts, d), lambda i, j: (i, j, 0)),
                  pl.BlockSpec((1, ODD_HALO, d), lambda i, j: (i, jnp.maximum(j * per - 1, 0), 0)),
                  full((d, d)), full((d, d)), full((d, d)), full(conv_w.shape), full((d, d)), full((1, d)), full((1, d))],
        out_specs=pl.BlockSpec((1, ts, d), lambda i, j: (i, j, 0)),
        out_shape=jax.ShapeDtypeStruct((b, s, d), jnp.float32),
        compiler_params=pltpu.CompilerParams(dimension_semantics=("arbitrary", "arbitrary"), vmem_limit_bytes=VMEM_LIMIT),
    )(x, x, wb, wc, wh, conv_w, w_out.astype(bf16), ln_g.reshape(1, d), ln_b.reshape(1, d))


def _cmp_to_sel_matrix(n_cmp, n_sel):
    c0 = np.arange(n_cmp) * CMP_STRIDE
    s0 = np.arange(n_sel) * SEL_BLOCK
    ov = np.minimum(c0[:, None] + CMP_BLOCK, s0[None, :] + SEL_BLOCK) - np.maximum(c0[:, None], s0[None, :])
    return (np.clip(ov, 0, None) / CMP_BLOCK).astype(np.float32)


def _nsa_kernel(qT_ref, gT_ref, kc_ref, vcT_ref, mselT_ref, ks_ref, vsT_ref, kw_ref, vwT_ref, o_ref,
                sel_ref, *, n_sel):
    c = pl.program_id(1)
    q0 = c * Q_BLOCK
    f32 = jnp.float32
    bf16 = jnp.bfloat16
    n_cmp_pad = kc_ref.shape[1]

    lane_q = lax.broadcasted_iota(jnp.int32, (1, QL), 1) % Q_BLOCK
    t_row = q0 + lane_q
    t_row_q = q0 + lax.broadcasted_iota(jnp.int32, (1, Q_BLOCK), 1)
    cur_q = t_row_q // SEL_BLOCK

    top_n = min(SEL_TOP_N, n_sel)
    m_iota = lax.broadcasted_iota(jnp.int32, (n_sel, Q_BLOCK), 0)
    forced = (m_iota == 0) | (m_iota == cur_q) | (m_iota == cur_q - 1)
    valid = m_iota <= cur_q

    q_g, o_c = [], []
    for g in range(NSA_KV_HEADS):
        pieces = []
        for hg in range(HEADS_PER_KV):
            h = g * HEADS_PER_KV + hg
            qh = qT_ref[0, h * HEAD_DIM:(h + 1) * HEAD_DIM, :] * jnp.asarray(HEAD_DIM ** -0.5, bf16)
            z = jnp.zeros_like(qh)
            pieces.append(jnp.concatenate([qh, z] if g == 0 else [z, qh], axis=0))
        qTp = jnp.concatenate(pieces, axis=1)
        q_g.append(qTp)
        rows = slice(g * HEAD_DIM, (g + 1) * HEAD_DIM)

        s_c = jnp.dot(kc_ref[0], qTp, preferred_element_type=f32)
        n_iota = lax.broadcasted_iota(jnp.int32, (n_cmp_pad, QL), 0)
        mask_c = (n_iota * CMP_STRIDE + (CMP_BLOCK - 1)) <= t_row
        s_c = jnp.where(mask_c, s_c, NEG_INF)
        m_c = jnp.max(s_c, axis=0, keepdims=True)
        p_c = jnp.where(mask_c, jnp.exp(s_c - m_c), 0.0)
        l_c = jnp.sum(p_c, axis=0, keepdims=True)
        p_c = p_c * jnp.where(l_c > 0.0, 1.0 / l_c, 0.0)
        p_cb = p_c.astype(bf16)
        o_c.append(jnp.dot(vcT_ref[0], p_cb, preferred_element_type=f32)[rows])
        imp4 = jnp.dot(mselT_ref[...], p_cb, preferred_element_type=f32)
        imp = imp4[:, 0:Q_BLOCK]
        for hg in range(1, HEADS_PER_KV):
            imp = imp + imp4[:, hg * Q_BLOCK:(hg + 1) * Q_BLOCK]

        remaining = jnp.where(valid, jnp.where(forced, FORCE_SCORE, imp), NEG_INF)
        chosen = jnp.zeros((n_sel, Q_BLOCK), jnp.bool_)
        for _ in range(top_n):
            best = jnp.max(remaining, axis=0, keepdims=True)
            first = jnp.min(jnp.where(remaining == best, m_iota, n_sel), axis=0, keepdims=True)
            hit = m_iota == first
            chosen = chosen | hit
            remaining = jnp.where(hit, -jnp.inf, remaining)
        sel_ref[g] = jnp.where(chosen & valid, 0.0, NEG_INF)

    def sel_tile(j, carry, causal, size=SEL_TILE):
        k_tile = ks_ref[0, j, 0:size, :]
        v_tile = vsT_ref[0, j, :, 0:size]
        blk0 = pl.multiple_of(j * BLOCKS_PER_TILE, BLOCKS_PER_TILE)
        out = []
        for g in range(NSA_KV_HEADS):
            m_i, l_i, acc = carry[g]
            s = jnp.dot(k_tile, q_g[g], preferred_element_type=f32)
            selrows = sel_ref[g, pl.ds(blk0, BLOCKS_PER_TILE), :]
            bias = jnp.concatenate(
                [jnp.broadcast_to(selrows[r:r + 1, :], (SEL_BLOCK, Q_BLOCK)) for r in range(size // SEL_BLOCK)], axis=0)
            if causal:
                key = j * SEL_TILE + lax.broadcasted_iota(jnp.int32, (size, Q_BLOCK), 0)
                bias = jnp.where(key <= t_row_q, bias, NEG_INF)
            s = s + jnp.concatenate([bias] * HEADS_PER_KV, axis=1)
            m_new = jnp.maximum(m_i, jnp.max(s, axis=0, keepdims=True))
            alpha = jnp.exp(m_i - m_new)
            p = jnp.exp(s - m_new)
            l_new = alpha * l_i + jnp.sum(p, axis=0, keepdims=True)
            pv = jnp.dot(v_tile, p.astype(bf16), preferred_element_type=f32)
            out.append((m_new, l_new, alpha * acc + pv))
        return tuple(out)

    n_full = q0 // SEL_TILE
    init = tuple((jnp.full((1, QL), NEG_INF, f32), jnp.zeros((1, QL), f32), jnp.zeros((KV_COLS, QL), f32))
                 for _ in range(NSA_KV_HEADS))
    carry = lax.fori_loop(0, n_full, functools.partial(sel_tile, causal=False), init)
    carry = lax.cond(
        q0 % SEL_TILE + Q_BLOCK <= SEL_TILE // 2,
        functools.partial(sel_tile, n_full, causal=True, size=SEL_TILE // 2),
        functools.partial(sel_tile, n_full, causal=True, size=SEL_TILE),
        carry)

    start = pl.multiple_of(jnp.maximum(q0 - WINDOW, 0), Q_BLOCK)
    j0 = start // Q_BLOCK
    k_win = kw_ref[0, pl.ds(start, WIN_SPAN), :]
    key_w = start + lax.broadcasted_iota(jnp.int32, (WIN_SPAN, Q_BLOCK), 0)
    bias_w = jnp.where(key_w <= t_row_q, 0.0, NEG_INF)
    bias_w = jnp.where(key_w > t_row_q - WINDOW, bias_w, NEG_INF)
    bias_w = jnp.concatenate([bias_w] * HEADS_PER_KV, axis=1)
    for g in range(NSA_KV_HEADS):
        rows = slice(g * HEAD_DIM, (g + 1) * HEAD_DIM)
        m_s, l_s, acc_s = carry[g]
        o_s = acc_s[rows] * (1.0 / l_s)
        s_w = jnp.dot(k_win, q_g[g], preferred_element_type=f32) + bias_w
        m_w = jnp.max(s_w, axis=0, keepdims=True)
        p_w = jnp.exp(s_w - m_w)
        l_w = jnp.sum(p_w, axis=0, keepdims=True)
        p_wb = p_w.astype(bf16)
        acc_w = jnp.zeros((KV_COLS, QL), f32)
        for i in range(WIN_SPAN // Q_BLOCK):
            acc_w = acc_w + jnp.dot(vwT_ref[0, j0 + i], p_wb[i * Q_BLOCK:(i + 1) * Q_BLOCK, :], preferred_element_type=f32)
        o_w = acc_w[rows] * (1.0 / l_w)

        for pair in range(HEADS_PER_KV // 2):
            halves = []
            for hg in (2 * pair, 2 * pair + 1):
                h = g * HEADS_PER_KV + hg
                lanes = slice(hg * Q_BLOCK, (hg + 1) * Q_BLOCK)
                gate = jax.nn.sigmoid(gT_ref[0, 3 * h:3 * h + 3, :])
                halves.append(gate[0:1] * o_c[g][:, lanes] + gate[1:2] * o_s[:, lanes] + gate[2:3] * o_w[:, lanes])
            both = jnp.concatenate(halves, axis=0)
            col0 = (g * HEADS_PER_KV + 2 * pair) * HEAD_DIM
            o_ref[0, :, col0:col0 + 2 * HEAD_DIM] = both.T.astype(o_ref.dtype)


def _nsa_attention(qT, gT, kc, vcT, ks, vsT, kw, vwT):
    b, hd, s = qT.shape
    n_sel = s // SEL_BLOCK
    nc = kc.shape[1]
    n_cmp = (s - CMP_BLOCK) // CMP_STRIDE + 1
    mselT = jnp.asarray(np.pad(_cmp_to_sel_matrix(n_cmp, n_sel).T, ((0, 0), (0, nc - n_cmp))), jnp.bfloat16)
    ks4 = ks.reshape(b, s // SEL_TILE, SEL_TILE, KV_COLS)
    return pl.pallas_call(
        functools.partial(_nsa_kernel, n_sel=n_sel),
        name="nsa_attention",
        grid=(b, s // Q_BLOCK),
        in_specs=[
            pl.BlockSpec((1, hd, Q_BLOCK), lambda i, c: (i, 0, c)),
            pl.BlockSpec((1, 3 * NSA_HEADS, Q_BLOCK), lambda i, c: (i, 0, c)),
            pl.BlockSpec((1, nc, KV_COLS), lambda i, c: (i, 0, 0)),
            pl.BlockSpec((1, KV_COLS, nc), lambda i, c: (i, 0, 0)),
            pl.BlockSpec((n_sel, nc), lambda i, c: (0, 0)),
            pl.BlockSpec((1, s // SEL_TILE, SEL_TILE, KV_COLS), lambda i, c: (i, 0, 0, 0)),
            pl.BlockSpec((1, s // SEL_TILE, KV_COLS, SEL_TILE), lambda i, c: (i, 0, 0, 0)),
            pl.BlockSpec((1, s, KV_COLS), lambda i, c: (i, 0, 0)),
            pl.BlockSpec((1, s // Q_BLOCK, KV_COLS, Q_BLOCK), lambda i, c: (i, 0, 0, 0)),
        ],
        out_specs=pl.BlockSpec((1, Q_BLOCK, hd), lambda i, c: (i, c, 0)),
        out_shape=jax.ShapeDtypeStruct((b, s, hd), jnp.bfloat16),
        scratch_shapes=[pltpu.VMEM((NSA_KV_HEADS, n_sel, Q_BLOCK), jnp.float32)],
        compiler_params=pltpu.CompilerParams(dimension_semantics=("arbitrary", "arbitrary")),
    )(qT, gT, kc, vcT, mselT, ks4, vsT, kw, vwT)


def _route_tile(xb, first_step, wT_ref, b_ref, tri_ref, route_ref, cnt_ref, run_ref):
    f32 = jnp.float32

    @pl.when(first_step)
    def _():
        run_ref[...] = jnp.zeros_like(run_ref)

    tm = xb.shape[0]
    logits = lax.dot_general(wT_ref[...], xb, _NT, preferred_element_type=f32) + b_ref[...]
    sub = lax.broadcasted_iota(jnp.int32, (LANES, tm), 0)
    is_g = sub < N_GROUPS
    gl = jnp.where(is_g, logits, NEG_INF)
    g_max = jnp.max(gl, axis=0, keepdims=True)
    g_star = jnp.min(jnp.where(gl == g_max, sub, LANES), axis=0, keepdims=True)
    p_group = 1.0 / jnp.sum(jnp.where(is_g, jnp.exp(gl - g_max), 0.0), axis=0, keepdims=True)
    lo = N_GROUPS + g_star * EXPERTS_PER_GROUP
    in_grp = (sub >= lo) & (sub < lo + EXPERTS_PER_GROUP)
    el = jnp.where(in_grp, logits, NEG_INF)
    v1 = jnp.max(el, axis=0, keepdims=True)
    i1 = jnp.min(jnp.where(el == v1, sub, LANES), axis=0, keepdims=True)
    el2 = jnp.where(sub == i1, NEG_INF, el)
    v2 = jnp.max(el2, axis=0, keepdims=True)
    i2 = jnp.min(jnp.where(el2 == v2, sub, LANES), axis=0, keepdims=True)
    e21 = jnp.exp(v2 - v1)
    gate1 = p_group * (1.0 / (1.0 + e21))
    gate2 = p_group * (e21 / (1.0 + e21))
    oh1 = (sub == i1).astype(f32)
    oh2 = (sub == i2).astype(f32)
    both = oh1 + oh2
    before = jnp.dot(both.astype(jnp.bfloat16), tri_ref[...], preferred_element_type=f32) + run_ref[...]
    rank1 = jnp.sum(oh1 * before, axis=0, keepdims=True)
    rank2 = jnp.sum(oh2 * before, axis=0, keepdims=True)
    run_ref[...] = run_ref[...] + jnp.sum(both, axis=1, keepdims=True)
    cnt_ref[...] = run_ref[...]
    zero = jnp.zeros_like(gate1)
    route_ref[...] = jnp.concatenate(
        [gate1, gate2, (i1 - N_GROUPS).astype(f32), (i2 - N_GROUPS).astype(f32), rank1, rank2, zero, zero], axis=0)


def _router_operands(wg, bg, we, be, tile):
    d = wg.shape[0]
    pad = LANES - N_GROUPS - N_EXPERTS
    w_t = jnp.concatenate([wg, we.reshape(d, N_EXPERTS), jnp.zeros((d, pad), wg.dtype)], axis=1).T
    bias = jnp.concatenate([bg, be.reshape(N_EXPERTS), jnp.zeros((pad,), bg.dtype)])[:, None]
    tri = jnp.asarray(np.triu(np.ones((tile, tile), np.float32), 1), jnp.bfloat16)
    return w_t.astype(jnp.bfloat16), bias, tri


def _dispatch_kernel(zt_ref, dest_ref, x_ref, xd_hbm, zbuf, ring, sem, zsem):
    tm = x_ref.shape[0]

    @pl.when(pl.program_id(0) == 0)
    def _():
        zbuf[...] = jnp.zeros_like(zbuf)

        def zero_copy(k):
            start = pl.multiple_of(jnp.maximum(zt_ref[k], 0) * EXPERT_TILE, EXPERT_TILE)
            return pltpu.make_async_copy(zbuf, xd_hbm.at[pl.ds(start, EXPERT_TILE)], zsem)

        def start_body(k, c):
            @pl.when(zt_ref[k] >= 0)
            def _():
                zero_copy(k).start()
            return c

        def wait_body(k, c):
            @pl.when(zt_ref[k] >= 0)
            def _():
                zero_copy(k).wait()
            return c

        lax.fori_loop(0, zt_ref.shape[0], start_body, 0)
        lax.fori_loop(0, zt_ref.shape[0], wait_body, 0)

    i = pl.program_id(0)
    n = pl.num_programs(0)
    slot = i % 2

    def wait_tile(of_slot):
        for k in range(EXPERT_TOP_K):
            pltpu.make_async_copy(ring.at[of_slot], xd_hbm.at[pl.ds(0, tm)], sem.at[of_slot]).wait()

    @pl.when(i >= 2)
    def _():
        wait_tile(slot)

    ring[slot] = x_ref[...].reshape(ring.shape[1:])

    def body(r, c):
        for k in range(EXPERT_TOP_K):
            pltpu.make_async_copy(
                ring.at[slot, r], xd_hbm.at[dest_ref[0, 0, k * tm + r]], sem.at[slot]).start(priority=k)
        return c

    lax.fori_loop(0, tm, body, 0, unroll=8)

    @pl.when(i == n - 1)
    def _():
        wait_tile(slot)

        @pl.when(n > 1)
        def _():
            wait_tile(1 - slot)


def _moe_dispatch(xt, dest2, zero_tiles, n_rows):
    t, d = xt.shape
    row = (SUBLANES, d // SUBLANES)
    grid_spec = pltpu.PrefetchScalarGridSpec(
        num_scalar_prefetch=1,
        grid=(t // ROUTE_TILE,),
        in_specs=[pl.BlockSpec((1, 1, EXPERT_TOP_K * ROUTE_TILE), lambda i, zt: (i, 0, 0), memory_space=pltpu.SMEM),
                  pl.BlockSpec((ROUTE_TILE, d), lambda i, zt: (i, 0))],
        out_specs=pl.BlockSpec(memory_space=pl.ANY),
        scratch_shapes=[pltpu.VMEM((EXPERT_TILE,) + row, xt.dtype), pltpu.VMEM((2, ROUTE_TILE) + row, xt.dtype),
                        pltpu.SemaphoreType.DMA((2,)), pltpu.SemaphoreType.DMA(())],
    )
    return pl.pallas_call(
        _dispatch_kernel,
        name="moe_dispatch",
        grid_spec=grid_spec,
        out_shape=jax.ShapeDtypeStruct((n_rows,) + row, xt.dtype),
        compiler_params=pltpu.CompilerParams(dimension_semantics=("arbitrary",), has_side_effects=True),
    )(zero_tiles, dest2, xt)


def _expert_kernel(te_ref, nu_ref, xd_ref, w1_ref, w3_ref, w2_ref, y_ref, w1b, w3b, w2b):
    i = pl.program_id(0)
    used = i < nu_ref[0]
    new_expert = (i == 0) | (te_ref[i] != te_ref[jnp.maximum(i - 1, 0)])

    @pl.when(used & new_expert)
    def _():
        w1b[...] = w1_ref[0, 0].astype(jnp.bfloat16)
        w3b[...] = w3_ref[0, 0].astype(jnp.bfloat16)
        w2b[...] = w2_ref[0, 0].astype(jnp.bfloat16)

    @pl.when(used)
    def _():
        xb = xd_ref[...].reshape(xd_ref.shape[0], w1b.shape[0]).astype(jnp.bfloat16)
        h1 = jnp.dot(xb, w1b[...], preferred_element_type=jnp.float32)
        h3 = jnp.dot(xb, w3b[...], preferred_element_type=jnp.float32)
        a = (h1 * jax.nn.sigmoid(h1) * h3).astype(jnp.bfloat16)
        y_ref[...] = jnp.dot(a, w2b[...], preferred_element_type=jnp.float32).reshape(y_ref.shape)

    @pl.when(jnp.logical_not(used))
    def _():
        y_ref[...] = jnp.zeros_like(y_ref)


def _moe_experts(x_disp, tile_expert, n_used, w1, w3, w2, layer):
    n_rows = x_disp.shape[0]
    row = x_disp.shape[1:]
    d = row[0] * row[1]
    n_tiles = n_rows // EXPERT_TILE
    hid = w1.shape[3]

    def row_map(i, te, nu):
        return (i, 0, 0)

    def w_map(i, te, nu):
        return (layer, te[i], 0, 0)

    grid_spec = pltpu.PrefetchScalarGridSpec(
        num_scalar_prefetch=2,
        grid=(n_tiles,),
        in_specs=[pl.BlockSpec((EXPERT_TILE,) + row, row_map),
                  pl.BlockSpec((1, 1, d, hid), w_map),
                  pl.BlockSpec((1, 1, d, hid), w_map),
                  pl.BlockSpec((1, 1, hid, d), w_map)],
        out_specs=pl.BlockSpec((EXPERT_TILE,) + row, row_map),
        scratch_shapes=[pltpu.VMEM((d, hid), jnp.bfloat16), pltpu.VMEM((d, hid), jnp.bfloat16),
                        pltpu.VMEM((hid, d), jnp.bfloat16)],
    )
    return pl.pallas_call(
        _expert_kernel,
        name="moe_experts",
        grid_spec=grid_spec,
        out_shape=jax.ShapeDtypeStruct((n_rows,) + row, jnp.float32),
        compiler_params=pltpu.CompilerParams(dimension_semantics=("arbitrary",), vmem_limit_bytes=VMEM_LIMIT),
    )(tile_expert, n_used, x_disp, w1, w3, w2)


def _combine_kernel(dest_ref, dest_next_ref, x_ref, route_ref, g_ref, b_ref, yd_hbm, o_ref, ybuf, sem):
    i = pl.program_id(0)
    n = pl.num_programs(0)
    tm, d = x_ref.shape
    rows = EXPERT_TOP_K * tm

    def start_tile(dref, slot):
        def body(r, c):
            for k in range(EXPERT_TOP_K):
                row = k * tm + r
                pltpu.make_async_copy(yd_hbm.at[dref[0, 0, row]], ybuf.at[slot, row], sem.at[slot]).start(priority=k)
            return c
        lax.fori_loop(0, tm, body, 0, unroll=8)

    slot = i % 2

    @pl.when(i == 0)
    def _():
        start_tile(dest_ref, 0)

    @pl.when(i + 1 < n)
    def _():
        start_tile(dest_next_ref, 1 - slot)

    pltpu.make_async_copy(yd_hbm.at[pl.ds(0, rows)], ybuf.at[slot], sem.at[slot]).wait()
    y1 = ybuf[slot, 0:tm].reshape(tm, d)
    y2 = ybuf[slot, tm:rows].reshape(tm, d)
    route = route_ref[...].T
    y = DN_ALPHA * x_ref[...] + (y1 * route[:, 0:1] + y2 * route[:, 1:2])
    o_ref[...] = _ln_rows(y, g_ref[...], b_ref[...])


def _moe_combine_ln(xt, y_disp, dest2, route, ln_g, ln_b):
    t, d = xt.shape
    n = t // ROUTE_TILE
    rows = EXPERT_TOP_K * ROUTE_TILE
    return pl.pallas_call(
        _combine_kernel,
        name="moe_combine_ln",
        grid=(n,),
        in_specs=[pl.BlockSpec((1, 1, rows), lambda i: (i, 0, 0), memory_space=pltpu.SMEM),
                  pl.BlockSpec((1, 1, rows), lambda i: (jnp.minimum(i + 1, n - 1), 0, 0), memory_space=pltpu.SMEM),
                  pl.BlockSpec((ROUTE_TILE, d), lambda i: (i, 0)),
                  pl.BlockSpec((ROUTE_ROWS, ROUTE_TILE), lambda i: (0, i)),
                  pl.BlockSpec((1, d), lambda i: (0, 0)),
                  pl.BlockSpec((1, d), lambda i: (0, 0)),
                  pl.BlockSpec(memory_space=pl.ANY)],
        out_specs=pl.BlockSpec((ROUTE_TILE, d), lambda i: (i, 0)),
        out_shape=jax.ShapeDtypeStruct((t, d), jnp.float32),
        scratch_shapes=[pltpu.VMEM((2, rows) + y_disp.shape[1:], jnp.float32), pltpu.SemaphoreType.DMA((2,))],
        compiler_params=pltpu.CompilerParams(dimension_semantics=("arbitrary",), vmem_limit_bytes=VMEM_LIMIT),
    )(dest2, dest2, xt, route, ln_g.reshape(1, d), ln_b.reshape(1, d), y_disp)


def _moe_sublayer(x, route, cnt, w1, w3, w2, layer, ln_g, ln_b):
    b, s, d = x.shape
    t = b * s
    xt = x.reshape(t, d)
    counts = cnt[N_GROUPS:N_GROUPS + N_EXPERTS, 0].astype(jnp.int32)
    n_tiles = (t * EXPERT_TOP_K) // EXPERT_TILE + N_EXPERTS
    tiles_per = (counts + EXPERT_TILE - 1) // EXPERT_TILE
    tile_end = jnp.cumsum(tiles_per)
    pad_start = (tile_end - tiles_per) * EXPERT_TILE
    n_used = tile_end[-1:].astype(jnp.int32)
    tile_ids = jnp.minimum(jnp.arange(n_tiles), n_used[0] - 1)
    tile_expert = jnp.sum(tile_ids[:, None] >= tile_end[None, :], axis=1).astype(jnp.int32)
    experts = route[2:4].astype(jnp.int32)
    first_row = jnp.sum(jnp.where(experts[..., None] == jnp.arange(N_EXPERTS), pad_start, 0), axis=-1)
    dest = first_row + route[4:6].astype(jnp.int32)
    dest2 = jnp.swapaxes(dest.reshape(EXPERT_TOP_K, t // ROUTE_TILE, ROUTE_TILE), 0, 1)
    dest2 = dest2.reshape(t // ROUTE_TILE, 1, EXPERT_TOP_K * ROUTE_TILE)
    last_tiles = jnp.where(tiles_per > 0, tile_end - 1, -1)
    tail_tiles = n_used[0] + jnp.arange(N_EXPERTS)
    tail_tiles = jnp.where(tail_tiles < n_tiles, tail_tiles, -1)
    zero_tiles = jnp.concatenate([last_tiles, tail_tiles]).astype(jnp.int32)
    x_disp = _moe_dispatch(xt, dest2, zero_tiles, n_tiles * EXPERT_TILE)
    y_disp = _moe_experts(x_disp, tile_expert, n_used, w1, w3, w2, layer)
    return _moe_combine_ln(xt, y_disp, dest2, route, ln_g, ln_b).reshape(b, s, d)


def kernel(x, mem, mem_wk, mem_wv, ev_w_in, ev_conv_w, ev_conv_b, ev_cnorm_g, ev_cnorm_b, ev_cmp_pe_k, ev_cmp_w1_k, ev_cmp_w2_k, ev_cmp_pe_v, ev_cmp_w1_v, ev_cmp_w2_v, ev_w_out, od_w_in, od_conv_w, od_w_out, ln_mix_g, ln_mix_b, xa_wq, xa_wo, ln_xa_g, ln_xa_b, moe_wg, moe_bg, moe_we, moe_be, moe_w1, moe_w3, moe_w2, ln_ffn_g, ln_ffn_b):
    b, s, d = x.shape
    mem_k, mem_v = _mem_kv(mem, mem_wk, mem_wv)
    for layer in range(DEPTH):
        i = layer // 2
        if layer % 2 == 0:
            a, kv_in, k_sel, k_win, q_t, v_sel_t, v_win_t, gate_t = _even_in_proj(x, ev_w_in[i])
            a = _conformer_conv(a, ev_conv_w[i], ev_conv_b[i], ev_cnorm_g[i], ev_cnorm_b[i])
            k_cmp, v_cmp_t = _compress_kv(kv_in, ev_cmp_pe_k[i], ev_cmp_w1_k[i], ev_cmp_w2_k[i], ev_cmp_pe_v[i], ev_cmp_w1_v[i], ev_cmp_w2_v[i])
            o = _nsa_attention(q_t, gate_t, k_cmp, v_cmp_t, k_sel, v_sel_t, k_win, v_win_t)
            x = _proj_residual_ln(a.reshape(b * s, -1), o.reshape(b * s, -1), x.reshape(b * s, d), ev_w_out[i], ln_mix_g[layer], ln_mix_b[layer]).reshape(b, s, d)
        else:
            x = _odd_mixer_sublayer(x, od_w_in[i], od_conv_w[i], od_w_out[i], ln_mix_g[layer], ln_mix_b[layer])
        x, route, cnt = _xattn_router_sublayer(x, mem_k, mem_v, xa_wq[layer], xa_wo[layer], ln_xa_g[layer], ln_xa_b[layer],
                                               moe_wg[layer], moe_bg[layer], moe_we[layer], moe_be[layer])
        x = _moe_sublayer(x, route, cnt, moe_w1, moe_w3, moe_w2, layer, ln_ffn_g[layer], ln_ffn_b[layer])
    return x
```

```python
import functools

import numpy as np
import jax
import jax.numpy as jnp
from jax import lax
from jax.experimental import pallas as pl
from jax.experimental.pallas import tpu as pltpu

D_MODEL = 1024
DEPTH = 2
CONV_CH = D_MODEL // 2
CONV_WIDTH = 31
NSA_HEADS = 8
NSA_KV_HEADS = 2
HEAD_DIM = (D_MODEL // 2) // NSA_HEADS
CMP_BLOCK = 32
CMP_STRIDE = 16
SEL_BLOCK = 64
SEL_TOP_N = 16
WINDOW = 512
Q_BLOCK = 256
FORCE_SCORE = 1e4
SHORT_CONV_WIDTH = 3
XA_HEADS = 4
XA_HEAD_DIM = D_MODEL // XA_HEADS
N_GROUPS = 4
EXPERTS_PER_GROUP = 8
N_EXPERTS = N_GROUPS * EXPERTS_PER_GROUP
EXPERT_TOP_K = 2
DN_ALPHA = (2 * DEPTH) ** 0.25
LN_EPS = 1e-5
NEG_INF = -1e30
KV_COLS = NSA_KV_HEADS * HEAD_DIM
QCOLS = NSA_HEADS * HEAD_DIM
GATE_ROWS = 3 * NSA_HEADS

LANES = 128
SUBLANES = 8
HEADS_PER_KV = NSA_HEADS // NSA_KV_HEADS
QL = Q_BLOCK * HEADS_PER_KV
SEL_TILE = 1024
WIN_SPAN = WINDOW + Q_BLOCK
BLOCKS_PER_TILE = SEL_TILE // SEL_BLOCK
SEQ_TILE = 1024
HALO = 32
ODD_HALO = 8
ROUTE_TILE = 512
EXPERT_TILE = 512
ROUTE_ROWS = 8
VMEM_LIMIT = 56 * 1024 * 1024

_NT = (((1,), (1,)), ((), ()))


def _ln_rows(y, g, b):
    mu = jnp.mean(y, axis=-1, keepdims=True)
    yc = y - mu
    var = jnp.mean(yc * yc, axis=-1, keepdims=True)
    return yc * lax.rsqrt(var + LN_EPS) * g + b


def _even_in_kernel(x_ref, wa_ref, wkv_ref, wk2_ref, wqT_ref, wvT_ref, wgT_ref,
                    a_ref, kv_ref, ks_ref, kw_ref, qT_ref, vsT_ref, vwT_ref, gT_ref):
    f32 = jnp.float32
    bf16 = jnp.bfloat16
    xb = x_ref[0].astype(bf16)
    av = jnp.dot(xb, wa_ref[...], preferred_element_type=f32)
    a_ref[0] = av[:, :CONV_CH] * jax.nn.sigmoid(av[:, CONV_CH:])
    kv_ref[0] = jnp.dot(xb, wkv_ref[...], preferred_element_type=f32)
    k2 = jnp.dot(xb, wk2_ref[...], preferred_element_type=f32)
    ks_ref[0] = k2[:, :KV_COLS].astype(bf16)
    kw_ref[0] = k2[:, KV_COLS:].astype(bf16)
    qT_ref[0] = lax.dot_general(wqT_ref[...], xb, _NT, preferred_element_type=f32).astype(bf16)
    vT = lax.dot_general(wvT_ref[...], xb, _NT, preferred_element_type=f32).astype(bf16)
    vsT_ref[0, 0] = vT[:KV_COLS]
    for j in range(SEQ_TILE // Q_BLOCK):
        vwT_ref[0, j] = vT[KV_COLS:, j * Q_BLOCK:(j + 1) * Q_BLOCK]
    gT_ref[0] = lax.dot_general(wgT_ref[...], xb, _NT, preferred_element_type=f32)


def _even_in_proj(x, w_in):
    b, s, d = x.shape
    bf16 = jnp.bfloat16
    c = np.cumsum((0, CONV_CH, CONV_CH, QCOLS, KV_COLS, KV_COLS, KV_COLS, KV_COLS, KV_COLS, KV_COLS, GATE_ROWS))
    col = lambda i, j: w_in[:, c[i]:c[j]]
    wa = col(0, 2).astype(bf16)
    wq_t = col(2, 3).T.astype(bf16)
    wkv = col(3, 5).astype(bf16)
    wk2 = jnp.concatenate([col(5, 6), col(7, 8)], axis=1).astype(bf16)
    wv_t = jnp.concatenate([col(6, 7), col(8, 9)], axis=1).T.astype(bf16)
    wg_t = col(9, 10).T.astype(bf16)
    ts = SEQ_TILE
    per_sel = SEL_TILE // ts
    full = lambda shape: pl.BlockSpec(shape, lambda i, j: (0,) * len(shape))
    return pl.pallas_call(
        _even_in_kernel,
        name="even_in_proj",
        grid=(b, s // ts),
        in_specs=[pl.BlockSpec((1, ts, d), lambda i, j: (i, j, 0)),
                  full(wa.shape), full(wkv.shape), full(wk2.shape), full(wq_t.shape), full(wv_t.shape), full(wg_t.shape)],
        out_specs=[pl.BlockSpec((1, ts, CONV_CH), lambda i, j: (i, j, 0)),
                   pl.BlockSpec((1, ts, 2 * KV_COLS), lambda i, j: (i, j, 0)),
                   pl.BlockSpec((1, ts, KV_COLS), lambda i, j: (i, j, 0)),
                   pl.BlockSpec((1, ts, KV_COLS), lambda i, j: (i, j, 0)),
                   pl.BlockSpec((1, QCOLS, ts), lambda i, j: (i, 0, j)),
                   pl.BlockSpec((1, 1, KV_COLS, ts), lambda i, j: (i, j // per_sel, 0, j % per_sel)),
                   pl.BlockSpec((1, ts // Q_BLOCK, KV_COLS, Q_BLOCK), lambda i, j: (i, j, 0, 0)),
                   pl.BlockSpec((1, GATE_ROWS, ts), lambda i, j: (i, 0, j))],
        out_shape=[jax.ShapeDtypeStruct((b, s, CONV_CH), jnp.float32),
                   jax.ShapeDtypeStruct((b, s, 2 * KV_COLS), jnp.float32),
                   jax.ShapeDtypeStruct((b, s, KV_COLS), bf16),
                   jax.ShapeDtypeStruct((b, s, KV_COLS), bf16),
                   jax.ShapeDtypeStruct((b, QCOLS, s), bf16),
                   jax.ShapeDtypeStruct((b, s // SEL_TILE, KV_COLS, SEL_TILE), bf16),
                   jax.ShapeDtypeStruct((b, s // Q_BLOCK, KV_COLS, Q_BLOCK), bf16),
                   jax.ShapeDtypeStruct((b, GATE_ROWS, s), jnp.float32)],
        compiler_params=pltpu.CompilerParams(dimension_semantics=("arbitrary", "arbitrary"), vmem_limit_bytes=VMEM_LIMIT),
    )(x, wa, wkv, wk2, wq_t, wv_t, wg_t)


def _conv_kernel(cur_ref, halo_ref, w_ref, cb_ref, g_ref, b_ref, o_ref, ext_ref, win_ref):
    j = pl.program_id(1)
    ts = cur_ref.shape[1]
    halo = halo_ref[0]
    ext_ref[0:HALO, :] = jnp.where(j > 0, halo, jnp.zeros_like(halo))
    ext_ref[HALO:HALO + ts, :] = cur_ref[0]
    first = HALO - (CONV_WIDTH - 1)
    acc = jnp.zeros((ts, CONV_CH), jnp.float32)
    for p in range(SUBLANES):
        n_a = len(range(p, CONV_WIDTH, SUBLANES))
        rows = ts + SUBLANES * (n_a - 1)
        win_ref[0:rows, :] = ext_ref[first + p:first + p + rows, :]
        for a in range(n_a):
            k = SUBLANES * a + p
            acc = acc + w_ref[k:k + 1, :] * win_ref[SUBLANES * a:SUBLANES * a + ts, :]
    y = _ln_rows(acc + cb_ref[...], g_ref[...], b_ref[...])
    o_ref[0] = (y * jax.nn.sigmoid(y)).astype(o_ref.dtype)


def _conformer_conv(a, conv_w, conv_b, cn_g, cn_b):
    b, s, c = a.shape
    ts = SEQ_TILE
    per = ts // HALO
    row = lambda v: v.reshape(1, c)
    return pl.pallas_call(
        _conv_kernel,
        name="conformer_conv",
        grid=(b, s // ts),
        in_specs=[pl.BlockSpec((1, ts, c), lambda i, j: (i, j, 0)),
                  pl.BlockSpec((1, HALO, c), lambda i, j: (i, jnp.maximum(j * per - 1, 0), 0)),
                  pl.BlockSpec((CONV_WIDTH, c), lambda i, j: (0, 0)),
                  pl.BlockSpec((1, c), lambda i, j: (0, 0)),
                  pl.BlockSpec((1, c), lambda i, j: (0, 0)),
                  pl.BlockSpec((1, c), lambda i, j: (0, 0))],
        out_specs=pl.BlockSpec((1, ts, c), lambda i, j: (i, j, 0)),
        out_shape=jax.ShapeDtypeStruct((b, s, c), jnp.bfloat16),
        scratch_shapes=[pltpu.VMEM((HALO + ts, c), jnp.float32), pltpu.VMEM((HALO + ts, c), jnp.float32)],
        compiler_params=pltpu.CompilerParams(dimension_semantics=("arbitrary", "arbitrary")),
    )(a, a, conv_w, row(conv_b), row(cn_g), row(cn_b))


def _compress_kernel(r_ref, pe_ref, w1_ref, w2_ref, w2T_ref, o_ref, oT_ref):
    f32 = jnp.float32
    bf16 = jnp.bfloat16
    nch = o_ref.shape[1]
    hw = w1_ref.shape[3]
    first = jnp.zeros((nch, hw), f32)
    second = jnp.zeros((nch, hw), f32)
    for l in range(CMP_STRIDE):
        tok = r_ref[pl.ds(l, nch, stride=CMP_STRIDE), :]
        first = first + jnp.dot((tok + pe_ref[0, l:l + 1, :]).astype(bf16), w1_ref[0, l], preferred_element_type=f32)
        second = second + jnp.dot((tok + pe_ref[0, CMP_STRIDE + l:CMP_STRIDE + l + 1, :]).astype(bf16),
                                  w1_ref[0, CMP_STRIDE + l], preferred_element_type=f32)
    second_next = jnp.concatenate([second[1:], jnp.zeros_like(second[0:1])], axis=0)
    hid = jax.nn.gelu(first + second_next).astype(bf16)
    o_ref[0] = jnp.dot(hid, w2_ref[0], preferred_element_type=f32).astype(bf16)
    oT_ref[0] = lax.dot_general(w2T_ref[0], hid, _NT, preferred_element_type=f32).astype(bf16)


def _compress_kv(kv_in, pe_k, w1_k, w2_k, pe_v, w1_v, w2_v):
    b, s, _ = kv_in.shape
    bf16 = jnp.bfloat16
    nch = s // CMP_STRIDE
    g = NSA_KV_HEADS
    hidden = w1_k.shape[1]

    def expand(pe, w1, w2):
        same = jnp.eye(g, dtype=bool)
        w1r = w1.reshape(CMP_BLOCK, HEAD_DIM, hidden)
        w1e = jnp.where(same[None, :, None, :, None], w1r[:, None, :, None, :], 0.0)
        w2e = jnp.where(same[:, None, :, None], w2[None, :, None, :], 0.0)
        return jnp.tile(pe, (1, g)), w1e.reshape(CMP_BLOCK, KV_COLS, g * hidden), w2e.reshape(g * hidden, KV_COLS)

    pk, w1k, w2k = expand(pe_k, w1_k, w2_k)
    pv, w1v, w2v = expand(pe_v, w1_v, w2_v)
    pe = jnp.stack([pk, pv])
    w1 = jnp.stack([w1k, w1v]).astype(bf16)
    w2 = jnp.stack([w2k, w2v]).astype(bf16)
    w2t = jnp.swapaxes(w2, 1, 2)
    o, o_t = pl.pallas_call(
        _compress_kernel,
        name="compress_kv",
        grid=(b, 2),
        in_specs=[pl.BlockSpec((s, KV_COLS), lambda i, j: (i, j)),
                  pl.BlockSpec((1,) + pe.shape[1:], lambda i, j: (j, 0, 0)),
                  pl.BlockSpec((1,) + w1.shape[1:], lambda i, j: (j, 0, 0, 0)),
                  pl.BlockSpec((1,) + w2.shape[1:], lambda i, j: (j, 0, 0)),
                  pl.BlockSpec((1,) + w2t.shape[1:], lambda i, j: (j, 0, 0))],
        out_specs=[pl.BlockSpec((1, nch, KV_COLS), lambda i, j: (2 * i + j, 0, 0)),
                   pl.BlockSpec((1, KV_COLS, nch), lambda i, j: (2 * i + j, 0, 0))],
        out_shape=[jax.ShapeDtypeStruct((b * 2, nch, KV_COLS), bf16), jax.ShapeDtypeStruct((b * 2, KV_COLS, nch), bf16)],
        compiler_params=pltpu.CompilerParams(dimension_semantics=("arbitrary", "arbitrary"), vmem_limit_bytes=VMEM_LIMIT),
    )(kv_in.reshape(b * s, 2 * KV_COLS), pe, w1, w2, w2t)
    return o.reshape(b, 2, nch, KV_COLS)[:, 0], o_t.reshape(b, 2, KV_COLS, nch)[:, 1]


def _proj_ln_kernel(a_ref, o_ref, x_ref, w_ref, g_ref, b_ref, y_ref):
    lhs = jnp.concatenate([a_ref[...], o_ref[...]], axis=1)
    mix = jnp.dot(lhs, w_ref[...], preferred_element_type=jnp.float32)
    y_ref[...] = _ln_rows(DN_ALPHA * x_ref[...] + mix, g_ref[...], b_ref[...])


def _proj_residual_ln(a, o, x, w, ln_g, ln_b):
    t, d = x.shape
    tm = SEQ_TILE
    return pl.pallas_call(
        _proj_ln_kernel,
        name="proj_residual_ln",
        grid=(t // tm,),
        in_specs=[pl.BlockSpec((tm, a.shape[1]), lambda i: (i, 0)),
                  pl.BlockSpec((tm, o.shape[1]), lambda i: (i, 0)),
                  pl.BlockSpec((tm, d), lambda i: (i, 0)),
                  pl.BlockSpec(w.shape, lambda i: (0, 0)),
                  pl.BlockSpec((1, d), lambda i: (0, 0)),
                  pl.BlockSpec((1, d), lambda i: (0, 0))],
        out_specs=pl.BlockSpec((tm, d), lambda i: (i, 0)),
        out_shape=jax.ShapeDtypeStruct((t, d), jnp.float32),
        compiler_params=pltpu.CompilerParams(dimension_semantics=("arbitrary",), vmem_limit_bytes=VMEM_LIMIT),
    )(a, o, x, w.astype(jnp.bfloat16), ln_g.reshape(1, d), ln_b.reshape(1, d))


def _mem_kv_kernel(m_ref, wk_ref, wv_ref, k_ref, v_ref):
    mb = m_ref[...].astype(jnp.bfloat16)
    k_ref[...] = jnp.dot(mb, wk_ref[...], preferred_element_type=jnp.float32).astype(jnp.bfloat16)
    v_ref[...] = jnp.dot(mb, wv_ref[...], preferred_element_type=jnp.float32).astype(jnp.bfloat16)


def _mem_kv(mem, wk, wv):
    b, m, d = mem.shape
    bf16 = jnp.bfloat16
    k, v = pl.pallas_call(
        _mem_kv_kernel,
        name="mem_kv",
        grid=(b,),
        in_specs=[pl.BlockSpec((m, d), lambda i: (i, 0)), pl.BlockSpec((d, d), lambda i: (0, 0)), pl.BlockSpec((d, d), lambda i: (0, 0))],
        out_specs=[pl.BlockSpec((m, d), lambda i: (i, 0)), pl.BlockSpec((m, d), lambda i: (i, 0))],
        out_shape=[jax.ShapeDtypeStruct((b * m, d), bf16), jax.ShapeDtypeStruct((b * m, d), bf16)],
        compiler_params=pltpu.CompilerParams(dimension_semantics=("arbitrary",), vmem_limit_bytes=VMEM_LIMIT),
    )(mem.reshape(b * m, d), wk.astype(bf16), wv.astype(bf16))
    return k.reshape(b, m, d), v.reshape(b, m, d)


def _xattn_kernel(x_ref, k_ref, v_ref, wq_ref, wo_ref, g_ref, b_ref, rw_ref, rb_ref, tri_ref,
                  y_ref, route_ref, cnt_ref, run_ref):
    f32 = jnp.float32
    bf16 = jnp.bfloat16
    x = x_ref[0]
    q = (jnp.dot(x.astype(bf16), wq_ref[...], preferred_element_type=f32).astype(bf16)
         * jnp.asarray(XA_HEAD_DIM ** -0.5, bf16))
    heads = []
    for h in range(XA_HEADS):
        cols = slice(h * XA_HEAD_DIM, (h + 1) * XA_HEAD_DIM)
        s = lax.dot_general(q[:, cols], k_ref[0, :, cols], _NT, preferred_element_type=f32)
        m = jnp.max(s, axis=1, keepdims=True)
        p = jnp.exp(s - m)
        p = p / jnp.sum(p, axis=1, keepdims=True)
        heads.append(jnp.dot(p.astype(bf16), v_ref[0, :, cols], preferred_element_type=f32).astype(bf16))
    att = jnp.concatenate(heads, axis=1)
    out = jnp.dot(att, wo_ref[...], preferred_element_type=f32)
    y = _ln_rows(DN_ALPHA * x + out, g_ref[...], b_ref[...])
    y_ref[0] = y
    first_step = (pl.program_id(0) == 0) & (pl.program_id(1) == 0)
    _route_tile(y.astype(bf16), first_step, rw_ref, rb_ref, tri_ref, route_ref, cnt_ref, run_ref)


def _xattn_router_sublayer(x, mem_k, mem_v, wq, wo, ln_g, ln_b, wg, bg, we, be):
    b, s, d = x.shape
    m = mem_k.shape[1]
    ts = SEQ_TILE
    per_seq = s // ts
    bf16 = jnp.bfloat16
    rw, rb, tri = _router_operands(wg, bg, we, be, ts)
    const = lambda shape: pl.BlockSpec(shape, lambda i, j: (0,) * len(shape))
    return pl.pallas_call(
        _xattn_kernel,
        name="xattn_router_sublayer",
        grid=(b, per_seq),
        in_specs=[pl.BlockSpec((1, ts, d), lambda i, j: (i, j, 0)),
                  pl.BlockSpec((1, m, d), lambda i, j: (i, 0, 0)),
                  pl.BlockSpec((1, m, d), lambda i, j: (i, 0, 0)),
                  const((d, d)), const((d, d)), const((1, d)), const((1, d)),
                  const(rw.shape), const(rb.shape), const(tri.shape)],
        out_specs=[pl.BlockSpec((1, ts, d), lambda i, j: (i, j, 0)),
                   pl.BlockSpec((ROUTE_ROWS, ts), lambda i, j: (0, i * per_seq + j)),
                   const((LANES, 1))],
        out_shape=[jax.ShapeDtypeStruct((b, s, d), jnp.float32),
                   jax.ShapeDtypeStruct((ROUTE_ROWS, b * s), jnp.float32),
                   jax.ShapeDtypeStruct((LANES, 1), jnp.float32)],
        scratch_shapes=[pltpu.VMEM((LANES, 1), jnp.float32)],
        compiler_params=pltpu.CompilerParams(dimension_semantics=("arbitrary", "arbitrary"), vmem_limit_bytes=VMEM_LIMIT),
    )(x, mem_k, mem_v, wq.astype(bf16), wo.astype(bf16), ln_g.reshape(1, d), ln_b.reshape(1, d), rw, rb, tri)


def _odd_kernel(x_ref, halo_ref, wb_ref, wc_ref, wh_ref, cw_ref, wo_ref, g_ref, b_ref, y_ref):
    f32 = jnp.float32
    bf16 = jnp.bfloat16
    j = pl.program_id(1)
    ts = x_ref.shape[1]
    x = x_ref[0]
    xe = jnp.concatenate([halo_ref[0], x], axis=0).astype(bf16)
    u = (jnp.dot(xe, wc_ref[...], preferred_element_type=f32) * jnp.dot(xe, wh_ref[...], preferred_element_type=f32))
    row = lax.broadcasted_iota(jnp.int32, (ODD_HALO + ts, 1), 0)
    u = jnp.where((row >= ODD_HALO) | (j > 0), u, 0.0)
    conv = jnp.zeros((ts, u.shape[1]), f32)
    for k in range(SHORT_CONV_WIDTH):
        off = ODD_HALO - (SHORT_CONV_WIDTH - 1) + k
        conv = conv + cw_ref[k:k + 1, :] * u[off:off + ts, :]
    gate_b = jnp.dot(xe[ODD_HALO:], wb_ref[...], preferred_element_type=f32)
    mix = jnp.dot((gate_b * conv).astype(bf16), wo_ref[...], preferred_element_type=f32)
    y_ref[0] = _ln_rows(DN_ALPHA * x + mix, g_ref[...], b_ref[...])


def _odd_mixer_sublayer(x, w_in, conv_w, w_out, ln_g, ln_b):
    b, s, d = x.shape
    ts = SEQ_TILE
    per = ts // ODD_HALO
    bf16 = jnp.bfloat16
    wb, wc, wh = (w_in[:, i * d:(i + 1) * d].astype(bf16) for i in range(3))
    full = lambda shape: pl.BlockSpec(shape, lambda i, j: (0,) * len(shape))
    return pl.pallas_call(
        _odd_kernel,
        name="odd_mixer_sublayer",
        grid=(b, s // ts),
        in_specs=[pl.BlockSpec((1, ts, d), lambda i, j: (i, j, 0)),
                  pl.BlockSpec((1, ODD_HALO, d), lambda i, j: (i, jnp.maximum(j * per - 1, 0), 0)),
                  full((d, d)), full((d, d)), full((d, d)), full(conv_w.shape), full((d, d)), full((1, d)), full((1, d))],
        out_specs=pl.BlockSpec((1, ts, d), lambda i, j: (i, j, 0)),
        out_shape=jax.ShapeDtypeStruct((b, s, d), jnp.float32),
        compiler_params=pltpu.CompilerParams(dimension_semantics=("arbitrary", "arbitrary"), vmem_limit_bytes=VMEM_LIMIT),
    )(x, x, wb, wc, wh, conv_w, w_out.astype(bf16), ln_g.reshape(1, d), ln_b.reshape(1, d))


def _cmp_to_sel_matrix(n_cmp, n_sel):
    c0 = np.arange(n_cmp) * CMP_STRIDE
    s0 = np.arange(n_sel) * SEL_BLOCK
    ov = np.minimum(c0[:, None] + CMP_BLOCK, s0[None, :] + SEL_BLOCK) - np.maximum(c0[:, None], s0[None, :])
    return (np.clip(ov, 0, None) / CMP_BLOCK).astype(np.float32)


def _nsa_kernel(qT_ref, gT_ref, kc_ref, vcT_ref, mselT_ref, ks_ref, vsT_ref, kw_ref, vwT_ref, o_ref,
                sel_ref, *, n_sel):
    c = pl.program_id(1)
    q0 = c * Q_BLOCK
    f32 = jnp.float32
    bf16 = jnp.bfloat16
    n_cmp_pad = kc_ref.shape[1]

    lane_q = lax.broadcasted_iota(jnp.int32, (1, QL), 1) % Q_BLOCK
    t_row = q0 + lane_q
    t_row_q = q0 + lax.broadcasted_iota(jnp.int32, (1, Q_BLOCK), 1)
    cur_q = t_row_q // SEL_BLOCK

    top_n = min(SEL_TOP_N, n_sel)
    m_iota = lax.broadcasted_iota(jnp.int32, (n_sel, Q_BLOCK), 0)
    forced = (m_iota == 0) | (m_iota == cur_q) | (m_iota == cur_q - 1)
    valid = m_iota <= cur_q

    q_g, o_c = [], []
    for g in range(NSA_KV_HEADS):
        pieces = []
        for hg in range(HEADS_PER_KV):
            h = g * HEADS_PER_KV + hg
            qh = qT_ref[0, h * HEAD_DIM:(h + 1) * HEAD_DIM, :] * jnp.asarray(HEAD_DIM ** -0.5, bf16)
            z = jnp.zeros_like(qh)
            pieces.append(jnp.concatenate([qh, z] if g == 0 else [z, qh], axis=0))
        qTp = jnp.concatenate(pieces, axis=1)
        q_g.append(qTp)
        rows = slice(g * HEAD_DIM, (g + 1) * HEAD_DIM)

        s_c = jnp.dot(kc_ref[0], qTp, preferred_element_type=f32)
        n_iota = lax.broadcasted_iota(jnp.int32, (n_cmp_pad, QL), 0)
        mask_c = (n_iota * CMP_STRIDE + (CMP_BLOCK - 1)) <= t_row
        s_c = jnp.where(mask_c, s_c, NEG_INF)
        m_c = jnp.max(s_c, axis=0, keepdims=True)
        p_c = jnp.where(mask_c, jnp.exp(s_c - m_c), 0.0)
        l_c = jnp.sum(p_c, axis=0, keepdims=True)
        p_c = p_c * jnp.where(l_c > 0.0, 1.0 / l_c, 0.0)
        p_cb = p_c.astype(bf16)
        o_c.append(jnp.dot(vcT_ref[0], p_cb, preferred_element_type=f32)[rows])
        imp4 = jnp.dot(mselT_ref[...], p_cb, preferred_element_type=f32)
        imp = imp4[:, 0:Q_BLOCK]
        for hg in range(1, HEADS_PER_KV):
            imp = imp + imp4[:, hg * Q_BLOCK:(hg + 1) * Q_BLOCK]

        remaining = jnp.where(valid, jnp.where(forced, FORCE_SCORE, imp), NEG_INF)
        chosen = jnp.zeros((n_sel, Q_BLOCK), jnp.bool_)
        for _ in range(top_n):
            best = jnp.max(remaining, axis=0, keepdims=True)
            first = jnp.min(jnp.where(remaining == best, m_iota, n_sel), axis=0, keepdims=True)
            hit = m_iota == first
            chosen = chosen | hit
            remaining = jnp.where(hit, -jnp.inf, remaining)
        sel_ref[g] = jnp.where(chosen & valid, 0.0, NEG_INF)

    def sel_tile(j, carry, causal, size=SEL_TILE):
        k_tile = ks_ref[0, j, 0:size, :]
        v_tile = vsT_ref[0, j, :, 0:size]
        blk0 = pl.multiple_of(j * BLOCKS_PER_TILE, BLOCKS_PER_TILE)
        out = []
        for g in range(NSA_KV_HEADS):
            m_i, l_i, acc = carry[g]
            s = jnp.dot(k_tile, q_g[g], preferred_element_type=f32)
            selrows = sel_ref[g, pl.ds(blk0, BLOCKS_PER_TILE), :]
            bias = jnp.concatenate(
                [jnp.broadcast_to(selrows[r:r + 1, :], (SEL_BLOCK, Q_BLOCK)) for r in range(size // SEL_BLOCK)], axis=0)
            if causal:
                key = j * SEL_TILE + lax.broadcasted_iota(jnp.int32, (size, Q_BLOCK), 0)
                bias = jnp.where(key <= t_row_q, bias, NEG_INF)
            s = s + jnp.concatenate([bias] * HEADS_PER_KV, axis=1)
            m_new = jnp.maximum(m_i, jnp.max(s, axis=0, keepdims=True))
            alpha = jnp.exp(m_i - m_new)
            p = jnp.exp(s - m_new)
            l_new = alpha * l_i + jnp.sum(p, axis=0, keepdims=True)
            pv = jnp.dot(v_tile, p.astype(bf16), preferred_element_type=f32)
            out.append((m_new, l_new, alpha * acc + pv))
        return tuple(out)

    n_full = q0 // SEL_TILE
    init = tuple((jnp.full((1, QL), NEG_INF, f32), jnp.zeros((1, QL), f32), jnp.zeros((KV_COLS, QL), f32))
                 for _ in range(NSA_KV_HEADS))
    carry = lax.fori_loop(0, n_full, functools.partial(sel_tile, causal=False), init)
    carry = lax.switch(
        (q0 % SEL_TILE) // Q_BLOCK,
        [functools.partial(sel_tile, n_full, causal=True, size=(pos + 1) * Q_BLOCK) for pos in range(SEL_TILE // Q_BLOCK)],
        carry)

    start = pl.multiple_of(jnp.maximum(q0 - WINDOW, 0), Q_BLOCK)
    j0 = start // Q_BLOCK
    k_win = kw_ref[0, pl.ds(start, WIN_SPAN), :]
    key_w = start + lax.broadcasted_iota(jnp.int32, (WIN_SPAN, Q_BLOCK), 0)
    bias_w = jnp.where(key_w <= t_row_q, 0.0, NEG_INF)
    bias_w = jnp.where(key_w > t_row_q - WINDOW, bias_w, NEG_INF)
    bias_w = jnp.concatenate([bias_w] * HEADS_PER_KV, axis=1)
    for g in range(NSA_KV_HEADS):
        rows = slice(g * HEAD_DIM, (g + 1) * HEAD_DIM)
        m_s, l_s, acc_s = carry[g]
        o_s = acc_s[rows] * (1.0 / l_s)
        s_w = jnp.dot(k_win, q_g[g], preferred_element_type=f32) + bias_w
        m_w = jnp.max(s_w, axis=0, keepdims=True)
        p_w = jnp.exp(s_w - m_w)
        l_w = jnp.sum(p_w, axis=0, keepdims=True)
        p_wb = p_w.astype(bf16)
        acc_w = jnp.zeros((KV_COLS, QL), f32)
        for i in range(WIN_SPAN // Q_BLOCK):
            acc_w = acc_w + jnp.dot(vwT_ref[0, j0 + i], p_wb[i * Q_BLOCK:(i + 1) * Q_BLOCK, :], preferred_element_type=f32)
        o_w = acc_w[rows] * (1.0 / l_w)

        for pair in range(HEADS_PER_KV // 2):
            halves = []
            for hg in (2 * pair, 2 * pair + 1):
                h = g * HEADS_PER_KV + hg
                lanes = slice(hg * Q_BLOCK, (hg + 1) * Q_BLOCK)
                gate = jax.nn.sigmoid(gT_ref[0, 3 * h:3 * h + 3, :])
                halves.append(gate[0:1] * o_c[g][:, lanes] + gate[1:2] * o_s[:, lanes] + gate[2:3] * o_w[:, lanes])
            both = jnp.concatenate(halves, axis=0)
            col0 = (g * HEADS_PER_KV + 2 * pair) * HEAD_DIM
            o_ref[0, :, col0:col0 + 2 * HEAD_DIM] = both.T.astype(o_ref.dtype)


def _nsa_attention(qT, gT, kc, vcT, ks, vsT, kw, vwT):
    b, hd, s = qT.shape
    n_sel = s // SEL_BLOCK
    nc = kc.shape[1]
    n_cmp = (s - CMP_BLOCK) // CMP_STRIDE + 1
    mselT = jnp.asarray(np.pad(_cmp_to_sel_matrix(n_cmp, n_sel).T, ((0, 0), (0, nc - n_cmp))), jnp.bfloat16)
    ks4 = ks.reshape(b, s // SEL_TILE, SEL_TILE, KV_COLS)
    return pl.pallas_call(
        functools.partial(_nsa_kernel, n_sel=n_sel),
        name="nsa_attention",
        grid=(b, s // Q_BLOCK),
        in_specs=[
            pl.BlockSpec((1, hd, Q_BLOCK), lambda i, c: (i, 0, c)),
            pl.BlockSpec((1, 3 * NSA_HEADS, Q_BLOCK), lambda i, c: (i, 0, c)),
            pl.BlockSpec((1, nc, KV_COLS), lambda i, c: (i, 0, 0)),
            pl.BlockSpec((1, KV_COLS, nc), lambda i, c: (i, 0, 0)),
            pl.BlockSpec((n_sel, nc), lambda i, c: (0, 0)),
            pl.BlockSpec((1, s // SEL_TILE, SEL_TILE, KV_COLS), lambda i, c: (i, 0, 0, 0)),
            pl.BlockSpec((1, s // SEL_TILE, KV_COLS, SEL_TILE), lambda i, c: (i, 0, 0, 0)),
            pl.BlockSpec((1, s, KV_COLS), lambda i, c: (i, 0, 0)),
            pl.BlockSpec((1, s // Q_BLOCK, KV_COLS, Q_BLOCK), lambda i, c: (i, 0, 0, 0)),
        ],
        out_specs=pl.BlockSpec((1, Q_BLOCK, hd), lambda i, c: (i, c, 0)),
        out_shape=jax.ShapeDtypeStruct((b, s, hd), jnp.bfloat16),
        scratch_shapes=[pltpu.VMEM((NSA_KV_HEADS, n_sel, Q_BLOCK), jnp.float32)],
        compiler_params=pltpu.CompilerParams(dimension_semantics=("arbitrary", "arbitrary")),
    )(qT, gT, kc, vcT, mselT, ks4, vsT, kw, vwT)


def _route_tile(xb, first_step, wT_ref, b_ref, tri_ref, route_ref, cnt_ref, run_ref):
    f32 = jnp.float32

    @pl.when(first_step)
    def _():
        run_ref[...] = jnp.zeros_like(run_ref)

    tm = xb.shape[0]
    logits = lax.dot_general(wT_ref[...], xb, _NT, preferred_element_type=f32) + b_ref[...]
    sub = lax.broadcasted_iota(jnp.int32, (LANES, tm), 0)
    is_g = sub < N_GROUPS
    gl = jnp.where(is_g, logits, NEG_INF)
    g_max = jnp.max(gl, axis=0, keepdims=True)
    g_star = jnp.min(jnp.where(gl == g_max, sub, LANES), axis=0, keepdims=True)
    p_group = 1.0 / jnp.sum(jnp.where(is_g, jnp.exp(gl - g_max), 0.0), axis=0, keepdims=True)
    lo = N_GROUPS + g_star * EXPERTS_PER_GROUP
    in_grp = (sub >= lo) & (sub < lo + EXPERTS_PER_GROUP)
    el = jnp.where(in_grp, logits, NEG_INF)
    v1 = jnp.max(el, axis=0, keepdims=True)
    i1 = jnp.min(jnp.where(el == v1, sub, LANES), axis=0, keepdims=True)
    el2 = jnp.where(sub == i1, NEG_INF, el)
    v2 = jnp.max(el2, axis=0, keepdims=True)
    i2 = jnp.min(jnp.where(el2 == v2, sub, LANES), axis=0, keepdims=True)
    e21 = jnp.exp(v2 - v1)
    gate1 = p_group * (1.0 / (1.0 + e21))
    gate2 = p_group * (e21 / (1.0 + e21))
    oh1 = (sub == i1).astype(f32)
    oh2 = (sub == i2).astype(f32)
    both = oh1 + oh2
    before = jnp.dot(both.astype(jnp.bfloat16), tri_ref[...], preferred_element_type=f32) + run_ref[...]
    rank1 = jnp.sum(oh1 * before, axis=0, keepdims=True)
    rank2 = jnp.sum(oh2 * before, axis=0, keepdims=True)
    run_ref[...] = run_ref[...] + jnp.sum(both, axis=1, keepdims=True)
    cnt_ref[...] = run_ref[...]
    zero = jnp.zeros_like(gate1)
    route_ref[...] = jnp.concatenate(
        [gate1, gate2, (i1 - N_GROUPS).astype(f32), (i2 - N_GROUPS).astype(f32), rank1, rank2, zero, zero], axis=0)


def _router_operands(wg, bg, we, be, tile):
    d = wg.shape[0]
    pad = LANES - N_GROUPS - N_EXPERTS
    w_t = jnp.concatenate([wg, we.reshape(d, N_EXPERTS), jnp.zeros((d, pad), wg.dtype)], axis=1).T
    bias = jnp.concatenate([bg, be.reshape(N_EXPERTS), jnp.zeros((pad,), bg.dtype)])[:, None]
    tri = jnp.asarray(np.triu(np.ones((tile, tile), np.float32), 1), jnp.bfloat16)
    return w_t.astype(jnp.bfloat16), bias, tri


def _dispatch_kernel(zt_ref, dest_ref, x_ref, xd_hbm, zbuf, ring, sem, zsem):
    tm = x_ref.shape[0]

    @pl.when(pl.program_id(0) == 0)
    def _():
        zbuf[...] = jnp.zeros_like(zbuf)

        def zero_copy(k):
            start = pl.multiple_of(jnp.maximum(zt_ref[k], 0) * EXPERT_TILE, EXPERT_TILE)
            return pltpu.make_async_copy(zbuf, xd_hbm.at[pl.ds(start, EXPERT_TILE)], zsem)

        def start_body(k, c):
            @pl.when(zt_ref[k] >= 0)
            def _():
                zero_copy(k).start()
            return c

        def wait_body(k, c):
            @pl.when(zt_ref[k] >= 0)
            def _():
                zero_copy(k).wait()
            return c

        lax.fori_loop(0, zt_ref.shape[0], start_body, 0)
        lax.fori_loop(0, zt_ref.shape[0], wait_body, 0)

    i = pl.program_id(0)
    n = pl.num_programs(0)
    slot = i % 2

    def wait_tile(of_slot):
        for k in range(EXPERT_TOP_K):
            pltpu.make_async_copy(ring.at[of_slot], xd_hbm.at[pl.ds(0, tm)], sem.at[of_slot]).wait()

    @pl.when(i >= 2)
    def _():
        wait_tile(slot)

    ring[slot] = x_ref[...].reshape(ring.shape[1:])

    def body(r, c):
        for k in range(EXPERT_TOP_K):
            pltpu.make_async_copy(
                ring.at[slot, r], xd_hbm.at[dest_ref[0, 0, k * tm + r]], sem.at[slot]).start(priority=k)
        return c

    lax.fori_loop(0, tm, body, 0, unroll=8)

    @pl.when(i == n - 1)
    def _():
        wait_tile(slot)

        @pl.when(n > 1)
        def _():
            wait_tile(1 - slot)


def _moe_dispatch(xt, dest2, zero_tiles, n_rows):
    t, d = xt.shape
    row = (SUBLANES, d // SUBLANES)
    grid_spec = pltpu.PrefetchScalarGridSpec(
        num_scalar_prefetch=1,
        grid=(t // ROUTE_TILE,),
        in_specs=[pl.BlockSpec((1, 1, EXPERT_TOP_K * ROUTE_TILE), lambda i, zt: (i, 0, 0), memory_space=pltpu.SMEM),
                  pl.BlockSpec((ROUTE_TILE, d), lambda i, zt: (i, 0))],
        out_specs=pl.BlockSpec(memory_space=pl.ANY),
        scratch_shapes=[pltpu.VMEM((EXPERT_TILE,) + row, xt.dtype), pltpu.VMEM((2, ROUTE_TILE) + row, xt.dtype),
                        pltpu.SemaphoreType.DMA((2,)), pltpu.SemaphoreType.DMA(())],
    )
    return pl.pallas_call(
        _dispatch_kernel,
        name="moe_dispatch",
        grid_spec=grid_spec,
        out_shape=jax.ShapeDtypeStruct((n_rows,) + row, xt.dtype),
        compiler_params=pltpu.CompilerParams(dimension_semantics=("arbitrary",), has_side_effects=True),
    )(zero_tiles, dest2, xt)


def _expert_kernel(te_ref, nu_ref, xd_ref, w1_ref, w3_ref, w2_ref, y_ref, w1b, w3b, w2b):
    i = pl.program_id(0)
    used = i < nu_ref[0]
    new_expert = (i == 0) | (te_ref[i] != te_ref[jnp.maximum(i - 1, 0)])

    @pl.when(used & new_expert)
    def _():
        w1b[...] = w1_ref[0, 0].astype(jnp.bfloat16)
        w3b[...] = w3_ref[0, 0].astype(jnp.bfloat16)
        w2b[...] = w2_ref[0, 0].astype(jnp.bfloat16)

    @pl.when(used)
    def _():
        xb = xd_ref[...].reshape(xd_ref.shape[0], w1b.shape[0]).astype(jnp.bfloat16)
        h1 = jnp.dot(xb, w1b[...], preferred_element_type=jnp.float32)
        h3 = jnp.dot(xb, w3b[...], preferred_element_type=jnp.float32)
        a = (h1 * jax.nn.sigmoid(h1) * h3).astype(jnp.bfloat16)
        y_ref[...] = jnp.dot(a, w2b[...], preferred_element_type=jnp.float32).reshape(y_ref.shape)

    @pl.when(jnp.logical_not(used))
    def _():
        y_ref[...] = jnp.zeros_like(y_ref)


def _moe_experts(x_disp, tile_expert, n_used, w1, w3, w2, layer):
    n_rows = x_disp.shape[0]
    row = x_disp.shape[1:]
    d = row[0] * row[1]
    n_tiles = n_rows // EXPERT_TILE
    hid = w1.shape[3]

    def row_map(i, te, nu):
        return (i, 0, 0)

    def w_map(i, te, nu):
        return (layer, te[i], 0, 0)

    grid_spec = pltpu.PrefetchScalarGridSpec(
        num_scalar_prefetch=2,
        grid=(n_tiles,),
        in_specs=[pl.BlockSpec((EXPERT_TILE,) + row, row_map),
                  pl.BlockSpec((1, 1, d, hid), w_map),
                  pl.BlockSpec((1, 1, d, hid), w_map),
                  pl.BlockSpec((1, 1, hid, d), w_map)],
        out_specs=pl.BlockSpec((EXPERT_TILE,) + row, row_map),
        scratch_shapes=[pltpu.VMEM((d, hid), jnp.bfloat16), pltpu.VMEM((d, hid), jnp.bfloat16),
                        pltpu.VMEM((hid, d), jnp.bfloat16)],
    )
    return pl.pallas_call(
        _expert_kernel,
        name="moe_experts",
        grid_spec=grid_spec,
        out_shape=jax.ShapeDtypeStruct((n_rows,) + row, jnp.float32),
        compiler_params=pltpu.CompilerParams(dimension_semantics=("arbitrary",), vmem_limit_bytes=VMEM_LIMIT),
    )(tile_expert, n_used, x_disp, w1, w3, w2)


def _combine_kernel(dest_ref, dest_next_ref, x_ref, route_ref, g_ref, b_ref, yd_hbm, o_ref, ybuf, sem):
    i = pl.program_id(0)
    n = pl.num_programs(0)
    tm, d = x_ref.shape
    rows = EXPERT_TOP_K * tm

    def start_tile(dref, slot):
        def body(r, c):
            for k in range(EXPERT_TOP_K):
                row = k * tm + r
                pltpu.make_async_copy(yd_hbm.at[dref[0, 0, row]], ybuf.at[slot, row], sem.at[slot]).start(priority=k)
            return c
        lax.fori_loop(0, tm, body, 0, unroll=8)

    slot = i % 2

    @pl.when(i == 0)
    def _():
        start_tile(dest_ref, 0)

    @pl.when(i + 1 < n)
    def _():
        start_tile(dest_next_ref, 1 - slot)

    pltpu.make_async_copy(yd_hbm.at[pl.ds(0, rows)], ybuf.at[slot], sem.at[slot]).wait()
    y1 = ybuf[slot, 0:tm].reshape(tm, d)
    y2 = ybuf[slot, tm:rows].reshape(tm, d)
    route = route_ref[...].T
    y = DN_ALPHA * x_ref[...] + (y1 * route[:, 0:1] + y2 * route[:, 1:2])
    o_ref[...] = _ln_rows(y, g_ref[...], b_ref[...])


def _moe_combine_ln(xt, y_disp, dest2, route, ln_g, ln_b):
    t, d = xt.shape
    n = t // ROUTE_TILE
    rows = EXPERT_TOP_K * ROUTE_TILE
    return pl.pallas_call(
        _combine_kernel,
        name="moe_combine_ln",
        grid=(n,),
        in_specs=[pl.BlockSpec((1, 1, rows), lambda i: (i, 0, 0), memory_space=pltpu.SMEM),
                  pl.BlockSpec((1, 1, rows), lambda i: (jnp.minimum(i + 1, n - 1), 0, 0), memory_space=pltpu.SMEM),
                  pl.BlockSpec((ROUTE_TILE, d), lambda i: (i, 0)),
                  pl.BlockSpec((ROUTE_ROWS, ROUTE_TILE), lambda i: (0, i)),
                  pl.BlockSpec((1, d), lambda i: (0, 0)),
                  pl.BlockSpec((1, d), lambda i: (0, 0)),
                  pl.BlockSpec(memory_space=pl.ANY)],
        out_specs=pl.BlockSpec((ROUTE_TILE, d), lambda i: (i, 0)),
        out_shape=jax.ShapeDtypeStruct((t, d), jnp.float32),
        scratch_shapes=[pltpu.VMEM((2, rows) + y_disp.shape[1:], jnp.float32), pltpu.SemaphoreType.DMA((2,))],
        compiler_params=pltpu.CompilerParams(dimension_semantics=("arbitrary",), vmem_limit_bytes=VMEM_LIMIT),
    )(dest2, dest2, xt, route, ln_g.reshape(1, d), ln_b.reshape(1, d), y_disp)


def _moe_sublayer(x, route, cnt, w1, w3, w2, layer, ln_g, ln_b):
    b, s, d = x.shape
    t = b * s
    xt = x.reshape(t, d)
    counts = cnt[N_GROUPS:N_GROUPS + N_EXPERTS, 0].astype(jnp.int32)
    n_tiles = (t * EXPERT_TOP_K) // EXPERT_TILE + N_EXPERTS
    tiles_per = (counts + EXPERT_TILE - 1) // EXPERT_TILE
    tile_end = jnp.cumsum(tiles_per)
    pad_start = (tile_end - tiles_per) * EXPERT_TILE
    n_used = tile_end[-1:].astype(jnp.int32)
    tile_ids = jnp.minimum(jnp.arange(n_tiles), n_used[0] - 1)
    tile_expert = jnp.sum(tile_ids[:, None] >= tile_end[None, :], axis=1).astype(jnp.int32)
    experts = route[2:4].astype(jnp.int32)
    first_row = jnp.sum(jnp.where(experts[..., None] == jnp.arange(N_EXPERTS), pad_start, 0), axis=-1)
    dest = first_row + route[4:6].astype(jnp.int32)
    dest2 = jnp.swapaxes(dest.reshape(EXPERT_TOP_K, t // ROUTE_TILE, ROUTE_TILE), 0, 1)
    dest2 = dest2.reshape(t // ROUTE_TILE, 1, EXPERT_TOP_K * ROUTE_TILE)
    last_tiles = jnp.where(tiles_per > 0, tile_end - 1, -1)
    tail_tiles = n_used[0] + jnp.arange(N_EXPERTS)
    tail_tiles = jnp.where(tail_tiles < n_tiles, tail_tiles, -1)
    zero_tiles = jnp.concatenate([last_tiles, tail_tiles]).astype(jnp.int32)
    x_disp = _moe_dispatch(xt, dest2, zero_tiles, n_tiles * EXPERT_TILE)
    y_disp = _moe_experts(x_disp, tile_expert, n_used, w1, w3, w2, layer)
    return _moe_combine_ln(xt, y_disp, dest2, route, ln_g, ln_b).reshape(b, s, d)


def kernel(x, mem, mem_wk, mem_wv, ev_w_in, ev_conv_w, ev_conv_b, ev_cnorm_g, ev_cnorm_b, ev_cmp_pe_k, ev_cmp_w1_k, ev_cmp_w2_k, ev_cmp_pe_v, ev_cmp_w1_v, ev_cmp_w2_v, ev_w_out, od_w_in, od_conv_w, od_w_out, ln_mix_g, ln_mix_b, xa_wq, xa_wo, ln_xa_g, ln_xa_b, moe_wg, moe_bg, moe_we, moe_be, moe_w1, moe_w3, moe_w2, ln_ffn_g, ln_ffn_b):
    b, s, d = x.shape
    mem_k, mem_v = _mem_kv(mem, mem_wk, mem_wv)
    for layer in range(DEPTH):
        i = layer // 2
        if layer % 2 == 0:
            a, kv_in, k_sel, k_win, q_t, v_sel_t, v_win_t, gate_t = _even_in_proj(x, ev_w_in[i])
            a = _conformer_conv(a, ev_conv_w[i], ev_conv_b[i], ev_cnorm_g[i], ev_cnorm_b[i])
            k_cmp, v_cmp_t = _compress_kv(kv_in, ev_cmp_pe_k[i], ev_cmp_w1_k[i], ev_cmp_w2_k[i], ev_cmp_pe_v[i], ev_cmp_w1_v[i], ev_cmp_w2_v[i])
            o = _nsa_attention(q_t, gate_t, k_cmp, v_cmp_t, k_sel, v_sel_t, k_win, v_win_t)
            x = _proj_residual_ln(a.reshape(b * s, -1), o.reshape(b * s, -1), x.reshape(b * s, d), ev_w_out[i], ln_mix_g[layer], ln_mix_b[layer]).reshape(b, s, d)
        else:
            x = _odd_mixer_sublayer(x, od_w_in[i], od_conv_w[i], od_w_out[i], ln_mix_g[layer], ln_mix_b[layer])
        x, route, cnt = _xattn_router_sublayer(x, mem_k, mem_v, xa_wq[layer], xa_wo[layer], ln_xa_g[layer], ln_xa_b[layer],
                                               moe_wg[layer], moe_bg[layer], moe_we[layer], moe_be[layer])
        x = _moe_sublayer(x, route, cnt, moe_w1, moe_w3, moe_w2, layer, ln_ffn_g[layer], ln_ffn_b[layer])
    return x
```

```python
import functools

import numpy as np
import jax
import jax.numpy as jnp
from jax import lax
from jax.experimental import pallas as pl
from jax.experimental.pallas import tpu as pltpu

D_MODEL = 1024
DEPTH = 2
CONV_CH = D_MODEL // 2
CONV_WIDTH = 31
NSA_HEADS = 8
NSA_KV_HEADS = 2
HEAD_DIM = (D_MODEL // 2) // NSA_HEADS
CMP_BLOCK = 32
CMP_STRIDE = 16
SEL_BLOCK = 64
SEL_TOP_N = 16
WINDOW = 512
Q_BLOCK = 256
FORCE_SCORE = 1e4
SHORT_CONV_WIDTH = 3
XA_HEADS = 4
XA_HEAD_DIM = D_MODEL // XA_HEADS
N_GROUPS = 4
EXPERTS_PER_GROUP = 8
N_EXPERTS = N_GROUPS * EXPERTS_PER_GROUP
EXPERT_TOP_K = 2
DN_ALPHA = (2 * DEPTH) ** 0.25
LN_EPS = 1e-5
NEG_INF = -1e30
KV_COLS = NSA_KV_HEADS * HEAD_DIM
QCOLS = NSA_HEADS * HEAD_DIM
GATE_ROWS = 3 * NSA_HEADS

LANES = 128
SUBLANES = 8
HEADS_PER_KV = NSA_HEADS // NSA_KV_HEADS
QL = Q_BLOCK * HEADS_PER_KV
SEL_TILE = 1024
WIN_SPAN = WINDOW + Q_BLOCK
BLOCKS_PER_TILE = SEL_TILE // SEL_BLOCK
SEQ_TILE = 1024
HALO = 32
ODD_HALO = 8
ROUTE_TILE = 512
EXPERT_TILE = 512
ROUTE_ROWS = 8
VMEM_LIMIT = 56 * 1024 * 1024

_NT = (((1,), (1,)), ((), ()))


def _ln_rows(y, g, b):
    mu = jnp.mean(y, axis=-1, keepdims=True)
    yc = y - mu
    var = jnp.mean(yc * yc, axis=-1, keepdims=True)
    return yc * lax.rsqrt(var + LN_EPS) * g + b


def _even_in_kernel(x_ref, wa_ref, wkv_ref, wk2_ref, wqT_ref, wvT_ref, wgT_ref,
                    a_ref, kv_ref, ks_ref, kw_ref, qT_ref, vsT_ref, vwT_ref, gT_ref):
    f32 = jnp.float32
    bf16 = jnp.bfloat16
    xb = x_ref[0].astype(bf16)
    av = jnp.dot(xb, wa_ref[...], preferred_element_type=f32)
    a_ref[0] = av[:, :CONV_CH] * jax.nn.sigmoid(av[:, CONV_CH:])
    kv_ref[0] = jnp.dot(xb, wkv_ref[...], preferred_element_type=f32)
    k2 = jnp.dot(xb, wk2_ref[...], preferred_element_type=f32)
    ks_ref[0] = k2[:, :KV_COLS].astype(bf16)
    kw_ref[0] = k2[:, KV_COLS:].astype(bf16)
    qT_ref[0] = lax.dot_general(wqT_ref[...], xb, _NT, preferred_element_type=f32).astype(bf16)
    vT = lax.dot_general(wvT_ref[...], xb, _NT, preferred_element_type=f32).astype(bf16)
    vsT_ref[0, 0] = vT[:KV_COLS]
    for j in range(SEQ_TILE // Q_BLOCK):
        vwT_ref[0, j] = vT[KV_COLS:, j * Q_BLOCK:(j + 1) * Q_BLOCK]
    gT_ref[0] = lax.dot_general(wgT_ref[...], xb, _NT, preferred_element_type=f32)


def _even_in_proj(x, w_in):
    b, s, d = x.shape
    bf16 = jnp.bfloat16
    c = np.cumsum((0, CONV_CH, CONV_CH, QCOLS, KV_COLS, KV_COLS, KV_COLS, KV_COLS, KV_COLS, KV_COLS, GATE_ROWS))
    col = lambda i, j: w_in[:, c[i]:c[j]]
    wa = col(0, 2).astype(bf16)
    wq_t = col(2, 3).T.astype(bf16)
    wkv = col(3, 5).astype(bf16)
    wk2 = jnp.concatenate([col(5, 6), col(7, 8)], axis=1).astype(bf16)
    wv_t = jnp.concatenate([col(6, 7), col(8, 9)], axis=1).T.astype(bf16)
    wg_t = col(9, 10).T.astype(bf16)
    ts = SEQ_TILE
    per_sel = SEL_TILE // ts
    full = lambda shape: pl.BlockSpec(shape, lambda i, j: (0,) * len(shape))
    return pl.pallas_call(
        _even_in_kernel,
        name="even_in_proj",
        grid=(b, s // ts),
        in_specs=[pl.BlockSpec((1, ts, d), lambda i, j: (i, j, 0)),
                  full(wa.shape), full(wkv.shape), full(wk2.shape), full(wq_t.shape), full(wv_t.shape), full(wg_t.shape)],
        out_specs=[pl.BlockSpec((1, ts, CONV_CH), lambda i, j: (i, j, 0)),
                   pl.BlockSpec((1, ts, 2 * KV_COLS), lambda i, j: (i, j, 0)),
                   pl.BlockSpec((1, ts, KV_COLS), lambda i, j: (i, j, 0)),
                   pl.BlockSpec((1, ts, KV_COLS), lambda i, j: (i, j, 0)),
                   pl.BlockSpec((1, QCOLS, ts), lambda i, j: (i, 0, j)),
                   pl.BlockSpec((1, 1, KV_COLS, ts), lambda i, j: (i, j // per_sel, 0, j % per_sel)),
                   pl.BlockSpec((1, ts // Q_BLOCK, KV_COLS, Q_BLOCK), lambda i, j: (i, j, 0, 0)),
                   pl.BlockSpec((1, GATE_ROWS, ts), lambda i, j: (i, 0, j))],
        out_shape=[jax.ShapeDtypeStruct((b, s, CONV_CH), jnp.float32),
                   jax.ShapeDtypeStruct((b, s, 2 * KV_COLS), jnp.float32),
                   jax.ShapeDtypeStruct((b, s, KV_COLS), bf16),
                   jax.ShapeDtypeStruct((b, s, KV_COLS), bf16),
                   jax.ShapeDtypeStruct((b, QCOLS, s), bf16),
                   jax.ShapeDtypeStruct((b, s // SEL_TILE, KV_COLS, SEL_TILE), bf16),
                   jax.ShapeDtypeStruct((b, s // Q_BLOCK, KV_COLS, Q_BLOCK), bf16),
                   jax.ShapeDtypeStruct((b, GATE_ROWS, s), jnp.float32)],
        compiler_params=pltpu.CompilerParams(dimension_semantics=("arbitrary", "arbitrary"), vmem_limit_bytes=VMEM_LIMIT),
    )(x, wa, wkv, wk2, wq_t, wv_t, wg_t)


def _conv_kernel(cur_ref, halo_ref, w_ref, cb_ref, g_ref, b_ref, o_ref, ext_ref, win_ref):
    j = pl.program_id(1)
    ts = cur_ref.shape[1]
    halo = halo_ref[0]
    ext_ref[0:HALO, :] = jnp.where(j > 0, halo, jnp.zeros_like(halo))
    ext_ref[HALO:HALO + ts, :] = cur_ref[0]
    first = HALO - (CONV_WIDTH - 1)
    acc = jnp.zeros((ts, CONV_CH), jnp.float32)
    for p in range(SUBLANES):
        n_a = len(range(p, CONV_WIDTH, SUBLANES))
        rows = ts + SUBLANES * (n_a - 1)
        win_ref[0:rows, :] = ext_ref[first + p:first + p + rows, :]
        for a in range(n_a):
            k = SUBLANES * a + p
            acc = acc + w_ref[k:k + 1, :] * win_ref[SUBLANES * a:SUBLANES * a + ts, :]
    y = _ln_rows(acc + cb_ref[...], g_ref[...], b_ref[...])
    o_ref[0] = (y * jax.nn.sigmoid(y)).astype(o_ref.dtype)


def _conformer_conv(a, conv_w, conv_b, cn_g, cn_b):
    b, s, c = a.shape
    ts = SEQ_TILE
    per = ts // HALO
    row = lambda v: v.reshape(1, c)
    return pl.pallas_call(
        _conv_kernel,
        name="conformer_conv",
        grid=(b, s // ts),
        in_specs=[pl.BlockSpec((1, ts, c), lambda i, j: (i, j, 0)),
                  pl.BlockSpec((1, HALO, c), lambda i, j: (i, jnp.maximum(j * per - 1, 0), 0)),
                  pl.BlockSpec((CONV_WIDTH, c), lambda i, j: (0, 0)),
                  pl.BlockSpec((1, c), lambda i, j: (0, 0)),
                  pl.BlockSpec((1, c), lambda i, j: (0, 0)),
                  pl.BlockSpec((1, c), lambda i, j: (0, 0))],
        out_specs=pl.BlockSpec((1, ts, c), lambda i, j: (i, j, 0)),
        out_shape=jax.ShapeDtypeStruct((b, s, c), jnp.bfloat16),
        scratch_shapes=[pltpu.VMEM((HALO + ts, c), jnp.float32), pltpu.VMEM((HALO + ts, c), jnp.float32)],
        compiler_params=pltpu.CompilerParams(dimension_semantics=("arbitrary", "arbitrary")),
    )(a, a, conv_w, row(conv_b), row(cn_g), row(cn_b))


def _compress_kernel(r_ref, pe_ref, w1_ref, w2_ref, w2T_ref, o_ref, oT_ref):
    f32 = jnp.float32
    bf16 = jnp.bfloat16
    nch = o_ref.shape[1]
    hw = w1_ref.shape[3]
    first = jnp.zeros((nch, hw), f32)
    second = jnp.zeros((nch, hw), f32)
    for l in range(CMP_STRIDE):
        tok = r_ref[pl.ds(l, nch, stride=CMP_STRIDE), :]
        first = first + jnp.dot((tok + pe_ref[0, l:l + 1, :]).astype(bf16), w1_ref[0, l], preferred_element_type=f32)
        second = second + jnp.dot((tok + pe_ref[0, CMP_STRIDE + l:CMP_STRIDE + l + 1, :]).astype(bf16),
                                  w1_ref[0, CMP_STRIDE + l], preferred_element_type=f32)
    second_next = jnp.concatenate([second[1:], jnp.zeros_like(second[0:1])], axis=0)
    hid = jax.nn.gelu(first + second_next).astype(bf16)
    o_ref[0] = jnp.dot(hid, w2_ref[0], preferred_element_type=f32).astype(bf16)
    oT_ref[0] = lax.dot_general(w2T_ref[0], hid, _NT, preferred_element_type=f32).astype(bf16)


def _compress_kv(kv_in, pe_k, w1_k, w2_k, pe_v, w1_v, w2_v):
    b, s, _ = kv_in.shape
    bf16 = jnp.bfloat16
    nch = s // CMP_STRIDE
    g = NSA_KV_HEADS
    hidden = w1_k.shape[1]

    def expand(pe, w1, w2):
        same = jnp.eye(g, dtype=bool)
        w1r = w1.reshape(CMP_BLOCK, HEAD_DIM, hidden)
        w1e = jnp.where(same[None, :, None, :, None], w1r[:, None, :, None, :], 0.0)
        w2e = jnp.where(same[:, None, :, None], w2[None, :, None, :], 0.0)
        return jnp.tile(pe, (1, g)), w1e.reshape(CMP_BLOCK, KV_COLS, g * hidden), w2e.reshape(g * hidden, KV_COLS)

    pk, w1k, w2k = expand(pe_k, w1_k, w2_k)
    pv, w1v, w2v = expand(pe_v, w1_v, w2_v)
    pe = jnp.stack([pk, pv])
    w1 = jnp.stack([w1k, w1v]).astype(bf16)
    w2 = jnp.stack([w2k, w2v]).astype(bf16)
    w2t = jnp.swapaxes(w2, 1, 2)
    o, o_t = pl.pallas_call(
        _compress_kernel,
        name="compress_kv",
        grid=(b, 2),
        in_specs=[pl.BlockSpec((s, KV_COLS), lambda i, j: (i, j)),
                  pl.BlockSpec((1,) + pe.shape[1:], lambda i, j: (j, 0, 0)),
                  pl.BlockSpec((1,) + w1.shape[1:], lambda i, j: (j, 0, 0, 0)),
                  pl.BlockSpec((1,) + w2.shape[1:], lambda i, j: (j, 0, 0)),
                  pl.BlockSpec((1,) + w2t.shape[1:], lambda i, j: (j, 0, 0))],
        out_specs=[pl.BlockSpec((1, nch, KV_COLS), lambda i, j: (2 * i + j, 0, 0)),
                   pl.BlockSpec((1, KV_COLS, nch), lambda i, j: (2 * i + j, 0, 0))],
        out_shape=[jax.ShapeDtypeStruct((b * 2, nch, KV_COLS), bf16), jax.ShapeDtypeStruct((b * 2, KV_COLS, nch), bf16)],
        compiler_params=pltpu.CompilerParams(dimension_semantics=("arbitrary", "arbitrary"), vmem_limit_bytes=VMEM_LIMIT),
    )(kv_in.reshape(b * s, 2 * KV_COLS), pe, w1, w2, w2t)
    return o.reshape(b, 2, nch, KV_COLS)[:, 0], o_t.reshape(b, 2, KV_COLS, nch)[:, 1]


def _proj_ln_kernel(a_ref, o_ref, x_ref, w_ref, g_ref, b_ref, y_ref):
    lhs = jnp.concatenate([a_ref[...], o_ref[...]], axis=1)
    mix = jnp.dot(lhs, w_ref[...], preferred_element_type=jnp.float32)
    y_ref[...] = _ln_rows(DN_ALPHA * x_ref[...] + mix, g_ref[...], b_ref[...])


def _proj_residual_ln(a, o, x, w, ln_g, ln_b):
    t, d = x.shape
    tm = SEQ_TILE
    return pl.pallas_call(
        _proj_ln_kernel,
        name="proj_residual_ln",
        grid=(t // tm,),
        in_specs=[pl.BlockSpec((tm, a.shape[1]), lambda i: (i, 0)),
                  pl.BlockSpec((tm, o.shape[1]), lambda i: (i, 0)),
                  pl.BlockSpec((tm, d), lambda i: (i, 0)),
                  pl.BlockSpec(w.shape, lambda i: (0, 0)),
                  pl.BlockSpec((1, d), lambda i: (0, 0)),
                  pl.BlockSpec((1, d), lambda i: (0, 0))],
        out_specs=pl.BlockSpec((tm, d), lambda i: (i, 0)),
        out_shape=jax.ShapeDtypeStruct((t, d), jnp.float32),
        compiler_params=pltpu.CompilerParams(dimension_semantics=("arbitrary",), vmem_limit_bytes=VMEM_LIMIT),
    )(a, o, x, w.astype(jnp.bfloat16), ln_g.reshape(1, d), ln_b.reshape(1, d))


def _mem_kv_kernel(m_ref, wk_ref, wv_ref, k_ref, v_ref):
    mb = m_ref[...].astype(jnp.bfloat16)
    k_ref[...] = jnp.dot(mb, wk_ref[...], preferred_element_type=jnp.float32).astype(jnp.bfloat16)
    v_ref[...] = jnp.dot(mb, wv_ref[...], preferred_element_type=jnp.float32).astype(jnp.bfloat16)


def _mem_kv(mem, wk, wv):
    b, m, d = mem.shape
    bf16 = jnp.bfloat16
    k, v = pl.pallas_call(
        _mem_kv_kernel,
        name="mem_kv",
        grid=(b,),
        in_specs=[pl.BlockSpec((m, d), lambda i: (i, 0)), pl.BlockSpec((d, d), lambda i: (0, 0)), pl.BlockSpec((d, d), lambda i: (0, 0))],
        out_specs=[pl.BlockSpec((m, d), lambda i: (i, 0)), pl.BlockSpec((m, d), lambda i: (i, 0))],
        out_shape=[jax.ShapeDtypeStruct((b * m, d), bf16), jax.ShapeDtypeStruct((b * m, d), bf16)],
        compiler_params=pltpu.CompilerParams(dimension_semantics=("arbitrary",), vmem_limit_bytes=VMEM_LIMIT),
    )(mem.reshape(b * m, d), wk.astype(bf16), wv.astype(bf16))
    return k.reshape(b, m, d), v.reshape(b, m, d)


def _xattn_kernel(x_ref, k_ref, v_ref, wq_ref, wo_ref, g_ref, b_ref, rw_ref, rb_ref, tri_ref,
                  y_ref, route_ref, cnt_ref, run_ref):
    f32 = jnp.float32
    bf16 = jnp.bfloat16
    x = x_ref[0]
    q = (jnp.dot(x.astype(bf16), wq_ref[...], preferred_element_type=f32).astype(bf16)
         * jnp.asarray(XA_HEAD_DIM ** -0.5, bf16))
    heads = []
    for h in range(XA_HEADS):
        cols = slice(h * XA_HEAD_DIM, (h + 1) * XA_HEAD_DIM)
        s = lax.dot_general(q[:, cols], k_ref[0, :, cols], _NT, preferred_element_type=f32)
        m = jnp.max(s, axis=1, keepdims=True)
        p = jnp.exp(s - m)
        p = p / jnp.sum(p, axis=1, keepdims=True)
        heads.append(jnp.dot(p.astype(bf16), v_ref[0, :, cols], preferred_element_type=f32).astype(bf16))
    att = jnp.concatenate(heads, axis=1)
    out = jnp.dot(att, wo_ref[...], preferred_element_type=f32)
    y = _ln_rows(DN_ALPHA * x + out, g_ref[...], b_ref[...])
    y_ref[0] = y
    first_step = (pl.program_id(0) == 0) & (pl.program_id(1) == 0)
    _route_tile(y.astype(bf16), first_step, rw_ref, rb_ref, tri_ref, route_ref, cnt_ref, run_ref)


def _xattn_router_sublayer(x, mem_k, mem_v, wq, wo, ln_g, ln_b, wg, bg, we, be):
    b, s, d = x.shape
    m = mem_k.shape[1]
    ts = SEQ_TILE
    per_seq = s // ts
    bf16 = jnp.bfloat16
    rw, rb, tri = _router_operands(wg, bg, we, be, ts)
    const = lambda shape: pl.BlockSpec(shape, lambda i, j: (0,) * len(shape))
    return pl.pallas_call(
        _xattn_kernel,
        name="xattn_router_sublayer",
        grid=(b, per_seq),
        in_specs=[pl.BlockSpec((1, ts, d), lambda i, j: (i, j, 0)),
                  pl.BlockSpec((1, m, d), lambda i, j: (i, 0, 0)),
                  pl.BlockSpec((1, m, d), lambda i, j: (i, 0, 0)),
                  const((d, d)), const((d, d)), const((1, d)), const((1, d)),
                  const(rw.shape), const(rb.shape), const(tri.shape)],
        out_specs=[pl.BlockSpec((1, ts, d), lambda i, j: (i, j, 0)),
                   pl.BlockSpec((ROUTE_ROWS, ts), lambda i, j: (0, i * per_seq + j)),
                   const((LANES, 1))],
        out_shape=[jax.ShapeDtypeStruct((b, s, d), jnp.float32),
                   jax.ShapeDtypeStruct((ROUTE_ROWS, b * s), jnp.float32),
                   jax.ShapeDtypeStruct((LANES, 1), jnp.float32)],
        scratch_shapes=[pltpu.VMEM((LANES, 1), jnp.float32)],
        compiler_params=pltpu.CompilerParams(dimension_semantics=("arbitrary", "arbitrary"), vmem_limit_bytes=VMEM_LIMIT),
    )(x, mem_k, mem_v, wq.astype(bf16), wo.astype(bf16), ln_g.reshape(1, d), ln_b.reshape(1, d), rw, rb, tri)


def _odd_kernel(x_ref, halo_ref, wb_ref, wc_ref, wh_ref, cw_ref, wo_ref, g_ref, b_ref, y_ref):
    f32 = jnp.float32
    bf16 = jnp.bfloat16
    j = pl.program_id(1)
    ts = x_ref.shape[1]
    x = x_ref[0]
    xe = jnp.concatenate([halo_ref[0], x], axis=0).astype(bf16)
    u = (jnp.dot(xe, wc_ref[...], preferred_element_type=f32) * jnp.dot(xe, wh_ref[...], preferred_element_type=f32))
    row = lax.broadcasted_iota(jnp.int32, (ODD_HALO + ts, 1), 0)
    u = jnp.where((row >= ODD_HALO) | (j > 0), u, 0.0)
    conv = jnp.zeros((ts, u.shape[1]), f32)
    for k in range(SHORT_CONV_WIDTH):
        off = ODD_HALO - (SHORT_CONV_WIDTH - 1) + k
        conv = conv + cw_ref[k:k + 1, :] * u[off:off + ts, :]
    gate_b = jnp.dot(xe[ODD_HALO:], wb_ref[...], preferred_element_type=f32)
    mix = jnp.dot((gate_b * conv).astype(bf16), wo_ref[...], preferred_element_type=f32)
    y_ref[0] = _ln_rows(DN_ALPHA * x + mix, g_ref[...], b_ref[...])


def _odd_mixer_sublayer(x, w_in, conv_w, w_out, ln_g, ln_b):
    b, s, d = x.shape
    ts = SEQ_TILE
    per = ts // ODD_HALO
    bf16 = jnp.bfloat16
    wb, wc, wh = (w_in[:, i * d:(i + 1) * d].astype(bf16) for i in range(3))
    full = lambda shape: pl.BlockSpec(shape, lambda i, j: (0,) * len(shape))
    return pl.pallas_call(
        _odd_kernel,
        name="odd_mixer_sublayer",
        grid=(b, s // ts),
        in_specs=[pl.BlockSpec((1, ts, d), lambda i, j: (i, j, 0)),
                  pl.BlockSpec((1, ODD_HALO, d), lambda i, j: (i, jnp.maximum(j * per - 1, 0), 0)),
                  full((d, d)), full((d, d)), full((d, d)), full(conv_w.shape), full((d, d)), full((1, d)), full((1, d))],
        out_specs=pl.BlockSpec((1, ts, d), lambda i, j: (i, j, 0)),
        out_shape=jax.ShapeDtypeStruct((b, s, d), jnp.float32),
        compiler_params=pltpu.CompilerParams(dimension_semantics=("arbitrary", "arbitrary"), vmem_limit_bytes=VMEM_LIMIT),
    )(x, x, wb, wc, wh, conv_w, w_out.astype(bf16), ln_g.reshape(1, d), ln_b.reshape(1, d))


def _cmp_to_sel_matrix(n_cmp, n_sel):
    c0 = np.arange(n_cmp) * CMP_STRIDE
    s0 = np.arange(n_sel) * SEL_BLOCK
    ov = np.minimum(c0[:, None] + CMP_BLOCK, s0[None, :] + SEL_BLOCK) - np.maximum(c0[:, None], s0[None, :])
    return (np.clip(ov, 0, None) / CMP_BLOCK).astype(np.float32)


def _nsa_kernel(qT_ref, gT_ref, kc_ref, vcT_ref, mselT_ref, ks_ref, vsT_ref, kw_ref, vwT_ref, o_ref,
                sel_ref, *, n_sel):
    c = pl.program_id(1)
    q0 = c * Q_BLOCK
    f32 = jnp.float32
    bf16 = jnp.bfloat16
    n_cmp_pad = kc_ref.shape[1]

    lane_q = lax.broadcasted_iota(jnp.int32, (1, QL), 1) % Q_BLOCK
    t_row = q0 + lane_q
    t_row_q = q0 + lax.broadcasted_iota(jnp.int32, (1, Q_BLOCK), 1)
    cur_q = t_row_q // SEL_BLOCK

    top_n = min(SEL_TOP_N, n_sel)
    m_iota = lax.broadcasted_iota(jnp.int32, (n_sel, Q_BLOCK), 0)
    forced = (m_iota == 0) | (m_iota == cur_q) | (m_iota == cur_q - 1)
    valid = m_iota <= cur_q

    q_g, o_c = [], []
    for g in range(NSA_KV_HEADS):
        pieces = []
        for hg in range(HEADS_PER_KV):
            h = g * HEADS_PER_KV + hg
            qh = qT_ref[0, h * HEAD_DIM:(h + 1) * HEAD_DIM, :] * jnp.asarray(HEAD_DIM ** -0.5, bf16)
            z = jnp.zeros_like(qh)
            pieces.append(jnp.concatenate([qh, z] if g == 0 else [z, qh], axis=0))
        qTp = jnp.concatenate(pieces, axis=1)
        q_g.append(qTp)
        rows = slice(g * HEAD_DIM, (g + 1) * HEAD_DIM)

        s_c = jnp.dot(kc_ref[0], qTp, preferred_element_type=f32)
        n_iota = lax.broadcasted_iota(jnp.int32, (n_cmp_pad, QL), 0)
        mask_c = (n_iota * CMP_STRIDE + (CMP_BLOCK - 1)) <= t_row
        s_c = jnp.where(mask_c, s_c, NEG_INF)
        m_c = jnp.max(s_c, axis=0, keepdims=True)
        p_c = jnp.where(mask_c, jnp.exp(s_c - m_c), 0.0)
        l_c = jnp.sum(p_c, axis=0, keepdims=True)
        p_c = p_c * jnp.where(l_c > 0.0, 1.0 / l_c, 0.0)
        p_cb = p_c.astype(bf16)
        o_c.append(jnp.dot(vcT_ref[0], p_cb, preferred_element_type=f32)[rows])
        imp4 = jnp.dot(mselT_ref[...], p_cb, preferred_element_type=f32)
        imp = imp4[:, 0:Q_BLOCK]
        for hg in range(1, HEADS_PER_KV):
            imp = imp + imp4[:, hg * Q_BLOCK:(hg + 1) * Q_BLOCK]

        remaining = jnp.where(valid, jnp.where(forced, FORCE_SCORE, imp), NEG_INF)
        chosen = jnp.zeros((n_sel, Q_BLOCK), jnp.bool_)
        for _ in range(top_n):
            best = jnp.max(remaining, axis=0, keepdims=True)
            first = jnp.min(jnp.where(remaining == best, m_iota, n_sel), axis=0, keepdims=True)
            hit = m_iota == first
            chosen = chosen | hit
            remaining = jnp.where(hit, -jnp.inf, remaining)
        sel_ref[g] = jnp.where(chosen & valid, 0.0, NEG_INF)

    def sel_tile(j, carry, causal, size=SEL_TILE):
        k_tile = ks_ref[0, j, 0:size, :]
        v_tile = vsT_ref[0, j, :, 0:size]
        blk0 = pl.multiple_of(j * BLOCKS_PER_TILE, BLOCKS_PER_TILE)
        out = []
        for g in range(NSA_KV_HEADS):
            m_i, l_i, acc = carry[g]
            s = jnp.dot(k_tile, q_g[g], preferred_element_type=f32)
            selrows = sel_ref[g, pl.ds(blk0, BLOCKS_PER_TILE), :]
            bias = jnp.concatenate(
                [jnp.broadcast_to(selrows[r:r + 1, :], (SEL_BLOCK, Q_BLOCK)) for r in range(size // SEL_BLOCK)], axis=0)
            if causal:
                key = j * SEL_TILE + lax.broadcasted_iota(jnp.int32, (size, Q_BLOCK), 0)
                bias = jnp.where(key <= t_row_q, bias, NEG_INF)
            s = s + jnp.concatenate([bias] * HEADS_PER_KV, axis=1)
            m_new = jnp.maximum(m_i, jnp.max(s, axis=0, keepdims=True))
            alpha = jnp.exp(m_i - m_new)
            p = jnp.exp(s - m_new)
            l_new = alpha * l_i + jnp.sum(p, axis=0, keepdims=True)
            pv = jnp.dot(v_tile, p.astype(bf16), preferred_element_type=f32)
            out.append((m_new, l_new, alpha * acc + pv))
        return tuple(out)

    n_full = q0 // SEL_TILE
    init = tuple((jnp.full((1, QL), NEG_INF, f32), jnp.zeros((1, QL), f32), jnp.zeros((KV_COLS, QL), f32))
                 for _ in range(NSA_KV_HEADS))
    carry = lax.fori_loop(0, n_full, functools.partial(sel_tile, causal=False), init)
    carry = lax.switch(
        (q0 % SEL_TILE) // Q_BLOCK,
        [functools.partial(sel_tile, n_full, causal=True, size=(pos + 1) * Q_BLOCK) for pos in range(SEL_TILE // Q_BLOCK)],
        carry)

    start = pl.multiple_of(jnp.maximum(q0 - WINDOW, 0), Q_BLOCK)
    j0 = start // Q_BLOCK
    k_win = kw_ref[0, pl.ds(start, WIN_SPAN), :]
    key_w = start + lax.broadcasted_iota(jnp.int32, (WIN_SPAN, Q_BLOCK), 0)
    bias_w = jnp.where(key_w <= t_row_q, 0.0, NEG_INF)
    bias_w = jnp.where(key_w > t_row_q - WINDOW, bias_w, NEG_INF)
    bias_w = jnp.concatenate([bias_w] * HEADS_PER_KV, axis=1)
    for g in range(NSA_KV_HEADS):
        rows = slice(g * HEAD_DIM, (g + 1) * HEAD_DIM)
        m_s, l_s, acc_s = carry[g]
        o_s = acc_s[rows] * (1.0 / l_s)
        s_w = jnp.dot(k_win, q_g[g], preferred_element_type=f32) + bias_w
        m_w = jnp.max(s_w, axis=0, keepdims=True)
        p_w = jnp.exp(s_w - m_w)
        l_w = jnp.sum(p_w, axis=0, keepdims=True)
        p_wb = p_w.astype(bf16)
        acc_w = jnp.zeros((KV_COLS, QL), f32)
        for i in range(WIN_SPAN // Q_BLOCK):
            acc_w = acc_w + jnp.dot(vwT_ref[0, j0 + i], p_wb[i * Q_BLOCK:(i + 1) * Q_BLOCK, :], preferred_element_type=f32)
        o_w = acc_w[rows] * (1.0 / l_w)

        for pair in range(HEADS_PER_KV // 2):
            halves = []
            for hg in (2 * pair, 2 * pair + 1):
                h = g * HEADS_PER_KV + hg
                lanes = slice(hg * Q_BLOCK, (hg + 1) * Q_BLOCK)
                gate = jax.nn.sigmoid(gT_ref[0, 3 * h:3 * h + 3, :])
                halves.append(gate[0:1] * o_c[g][:, lanes] + gate[1:2] * o_s[:, lanes] + gate[2:3] * o_w[:, lanes])
            both = jnp.concatenate(halves, axis=0)
            col0 = (g * HEADS_PER_KV + 2 * pair) * HEAD_DIM
            o_ref[0, :, col0:col0 + 2 * HEAD_DIM] = both.T.astype(o_ref.dtype)


def _nsa_attention(qT, gT, kc, vcT, ks, vsT, kw, vwT):
    b, hd, s = qT.shape
    n_sel = s // SEL_BLOCK
    nc = kc.shape[1]
    n_cmp = (s - CMP_BLOCK) // CMP_STRIDE + 1
    mselT = jnp.asarray(np.pad(_cmp_to_sel_matrix(n_cmp, n_sel).T, ((0, 0), (0, nc - n_cmp))), jnp.bfloat16)
    ks4 = ks.reshape(b, s // SEL_TILE, SEL_TILE, KV_COLS)
    return pl.pallas_call(
        functools.partial(_nsa_kernel, n_sel=n_sel),
        name="nsa_attention",
        grid=(b, s // Q_BLOCK),
        in_specs=[
            pl.BlockSpec((1, hd, Q_BLOCK), lambda i, c: (i, 0, c)),
            pl.BlockSpec((1, 3 * NSA_HEADS, Q_BLOCK), lambda i, c: (i, 0, c)),
            pl.BlockSpec((1, nc, KV_COLS), lambda i, c: (i, 0, 0)),
            pl.BlockSpec((1, KV_COLS, nc), lambda i, c: (i, 0, 0)),
            pl.BlockSpec((n_sel, nc), lambda i, c: (0, 0)),
            pl.BlockSpec((1, s // SEL_TILE, SEL_TILE, KV_COLS), lambda i, c: (i, 0, 0, 0)),
            pl.BlockSpec((1, s // SEL_TILE, KV_COLS, SEL_TILE), lambda i, c: (i, 0, 0, 0)),
            pl.BlockSpec((1, s, KV_COLS), lambda i, c: (i, 0, 0)),
            pl.BlockSpec((1, s // Q_BLOCK, KV_COLS, Q_BLOCK), lambda i, c: (i, 0, 0, 0)),
        ],
        out_specs=pl.BlockSpec((1, Q_BLOCK, hd), lambda i, c: (i, c, 0)),
        out_shape=jax.ShapeDtypeStruct((b, s, hd), jnp.bfloat16),
        scratch_shapes=[pltpu.VMEM((NSA_KV_HEADS, n_sel, Q_BLOCK), jnp.float32)],
        compiler_params=pltpu.CompilerParams(dimension_semantics=("arbitrary", "arbitrary")),
    )(qT, gT, kc, vcT, mselT, ks4, vsT, kw, vwT)


def _route_tile(xb, first_step, wT_ref, b_ref, tri_ref, route_ref, cnt_ref, run_ref):
    f32 = jnp.float32

    @pl.when(first_step)
    def _():
        run_ref[...] = jnp.zeros_like(run_ref)

    tm = xb.shape[0]
    logits = lax.dot_general(wT_ref[...], xb, _NT, preferred_element_type=f32) + b_ref[...]
    sub = lax.broadcasted_iota(jnp.int32, (LANES, tm), 0)
    is_g = sub < N_GROUPS
    gl = jnp.where(is_g, logits, NEG_INF)
    g_max = jnp.max(gl, axis=0, keepdims=True)
    g_star = jnp.min(jnp.where(gl == g_max, sub, LANES), axis=0, keepdims=True)
    p_group = 1.0 / jnp.sum(jnp.where(is_g, jnp.exp(gl - g_max), 0.0), axis=0, keepdims=True)
    lo = N_GROUPS + g_star * EXPERTS_PER_GROUP
    in_grp = (sub >= lo) & (sub < lo + EXPERTS_PER_GROUP)
    el = jnp.where(in_grp, logits, NEG_INF)
    v1 = jnp.max(el, axis=0, keepdims=True)
    i1 = jnp.min(jnp.where(el == v1, sub, LANES), axis=0, keepdims=True)
    el2 = jnp.where(sub == i1, NEG_INF, el)
    v2 = jnp.max(el2, axis=0, keepdims=True)
    i2 = jnp.min(jnp.where(el2 == v2, sub, LANES), axis=0, keepdims=True)
    e21 = jnp.exp(v2 - v1)
    gate1 = p_group * (1.0 / (1.0 + e21))
    gate2 = p_group * (e21 / (1.0 + e21))
    oh1 = (sub == i1).astype(f32)
    oh2 = (sub == i2).astype(f32)
    both = oh1 + oh2
    before = jnp.dot(both.astype(jnp.bfloat16), tri_ref[...], preferred_element_type=f32) + run_ref[...]
    rank1 = jnp.sum(oh1 * before, axis=0, keepdims=True)
    rank2 = jnp.sum(oh2 * before, axis=0, keepdims=True)
    run_ref[...] = run_ref[...] + jnp.sum(both, axis=1, keepdims=True)
    cnt_ref[...] = run_ref[...]
    zero = jnp.zeros_like(gate1)
    route_ref[...] = jnp.concatenate(
        [gate1, gate2, (i1 - N_GROUPS).astype(f32), (i2 - N_GROUPS).astype(f32), rank1, rank2, zero, zero], axis=0)


def _router_operands(wg, bg, we, be, tile):
    d = wg.shape[0]
    pad = LANES - N_GROUPS - N_EXPERTS
    w_t = jnp.concatenate([wg, we.reshape(d, N_EXPERTS), jnp.zeros((d, pad), wg.dtype)], axis=1).T
    bias = jnp.concatenate([bg, be.reshape(N_EXPERTS), jnp.zeros((pad,), bg.dtype)])[:, None]
    tri = jnp.asarray(np.triu(np.ones((tile, tile), np.float32), 1), jnp.bfloat16)
    return w_t.astype(jnp.bfloat16), bias, tri


def _dispatch_kernel(zt_ref, dest_ref, x_ref, xd_hbm, zbuf, ring, sem, zsem):
    tm = x_ref.shape[0]

    @pl.when(pl.program_id(0) == 0)
    def _():
        zbuf[...] = jnp.zeros_like(zbuf)

        def zero_copy(k):
            start = pl.multiple_of(jnp.maximum(zt_ref[k], 0) * EXPERT_TILE, EXPERT_TILE)
            return pltpu.make_async_copy(zbuf, xd_hbm.at[pl.ds(start, EXPERT_TILE)], zsem)

        def start_body(k, c):
            @pl.when(zt_ref[k] >= 0)
            def _():
                zero_copy(k).start()
            return c

        def wait_body(k, c):
            @pl.when(zt_ref[k] >= 0)
            def _():
                zero_copy(k).wait()
            return c

        lax.fori_loop(0, zt_ref.shape[0], start_body, 0)
        lax.fori_loop(0, zt_ref.shape[0], wait_body, 0)

    i = pl.program_id(0)
    n = pl.num_programs(0)
    slot = i % 2

    def wait_tile(of_slot):
        for k in range(EXPERT_TOP_K):
            pltpu.make_async_copy(ring.at[of_slot], xd_hbm.at[pl.ds(0, tm)], sem.at[of_slot]).wait()

    @pl.when(i >= 2)
    def _():
        wait_tile(slot)

    ring[slot] = x_ref[...].reshape(ring.shape[1:])

    def body(r, c):
        for k in range(EXPERT_TOP_K):
            pltpu.make_async_copy(
                ring.at[slot, r], xd_hbm.at[dest_ref[0, 0, k * tm + r]], sem.at[slot]).start(priority=k)
        return c

    lax.fori_loop(0, tm, body, 0, unroll=8)

    @pl.when(i == n - 1)
    def _():
        wait_tile(slot)

        @pl.when(n > 1)
        def _():
            wait_tile(1 - slot)


def _moe_dispatch(xt, dest2, zero_tiles, n_rows):
    t, d = xt.shape
    row = (SUBLANES, d // SUBLANES)
    grid_spec = pltpu.PrefetchScalarGridSpec(
        num_scalar_prefetch=1,
        grid=(t // ROUTE_TILE,),
        in_specs=[pl.BlockSpec((1, 1, EXPERT_TOP_K * ROUTE_TILE), lambda i, zt: (i, 0, 0), memory_space=pltpu.SMEM),
                  pl.BlockSpec((ROUTE_TILE, d), lambda i, zt: (i, 0))],
        out_specs=pl.BlockSpec(memory_space=pl.ANY),
        scratch_shapes=[pltpu.VMEM((EXPERT_TILE,) + row, xt.dtype), pltpu.VMEM((2, ROUTE_TILE) + row, xt.dtype),
                        pltpu.SemaphoreType.DMA((2,)), pltpu.SemaphoreType.DMA(())],
    )
    return pl.pallas_call(
        _dispatch_kernel,
        name="moe_dispatch",
        grid_spec=grid_spec,
        out_shape=jax.ShapeDtypeStruct((n_rows,) + row, xt.dtype),
        compiler_params=pltpu.CompilerParams(dimension_semantics=("arbitrary",), has_side_effects=True),
    )(zero_tiles, dest2, xt)


def _expert_kernel(te_ref, nu_ref, xd_ref, w1_ref, w3_ref, w2_ref, y_ref, w1b, w3b, w2b):
    i = pl.program_id(0)
    used = i < nu_ref[0]
    new_expert = (i == 0) | (te_ref[i] != te_ref[jnp.maximum(i - 1, 0)])

    @pl.when(used & new_expert)
    def _():
        w1b[...] = w1_ref[0, 0].astype(jnp.bfloat16)
        w3b[...] = w3_ref[0, 0].astype(jnp.bfloat16)
        w2b[...] = w2_ref[0, 0].astype(jnp.bfloat16)

    @pl.when(used)
    def _():
        xb = xd_ref[...].reshape(xd_ref.shape[0], w1b.shape[0]).astype(jnp.bfloat16)
        h1 = jnp.dot(xb, w1b[...], preferred_element_type=jnp.float32)
        h3 = jnp.dot(xb, w3b[...], preferred_element_type=jnp.float32)
        a = (h1 * jax.nn.sigmoid(h1) * h3).astype(jnp.bfloat16)
        y_ref[...] = jnp.dot(a, w2b[...], preferred_element_type=jnp.float32).reshape(y_ref.shape)

    @pl.when(jnp.logical_not(used))
    def _():
        y_ref[...] = jnp.zeros_like(y_ref)


def _moe_experts(x_disp, tile_expert, n_used, w1, w3, w2, layer):
    n_rows = x_disp.shape[0]
    row = x_disp.shape[1:]
    d = row[0] * row[1]
    n_tiles = n_rows // EXPERT_TILE
    hid = w1.shape[3]

    def row_map(i, te, nu):
        return (i, 0, 0)

    def w_map(i, te, nu):
        return (layer, te[i], 0, 0)

    grid_spec = pltpu.PrefetchScalarGridSpec(
        num_scalar_prefetch=2,
        grid=(n_tiles,),
        in_specs=[pl.BlockSpec((EXPERT_TILE,) + row, row_map),
                  pl.BlockSpec((1, 1, d, hid), w_map),
                  pl.BlockSpec((1, 1, d, hid), w_map),
                  pl.BlockSpec((1, 1, hid, d), w_map)],
        out_specs=pl.BlockSpec((EXPERT_TILE,) + row, row_map),
        scratch_shapes=[pltpu.VMEM((d, hid), jnp.bfloat16), pltpu.VMEM((d, hid), jnp.bfloat16),
                        pltpu.VMEM((hid, d), jnp.bfloat16)],
    )
    return pl.pallas_call(
        _expert_kernel,
        name="moe_experts",
        grid_spec=grid_spec,
        out_shape=jax.ShapeDtypeStruct((n_rows,) + row, jnp.float32),
        compiler_params=pltpu.CompilerParams(dimension_semantics=("arbitrary",), vmem_limit_bytes=VMEM_LIMIT),
    )(tile_expert, n_used, x_disp, w1, w3, w2)


def _combine_kernel(dest_ref, dest_next_ref, x_ref, route_ref, g_ref, b_ref, yd_hbm, o_ref, ybuf, sem):
    i = pl.program_id(0)
    n = pl.num_programs(0)
    tm, d = x_ref.shape
    rows = EXPERT_TOP_K * tm

    def start_tile(dref, slot):
        def body(r, c):
            for k in range(EXPERT_TOP_K):
                row = k * tm + r
                pltpu.make_async_copy(yd_hbm.at[dref[0, 0, row]], ybuf.at[slot, row], sem.at[slot]).start(priority=k)
            return c
        lax.fori_loop(0, tm, body, 0, unroll=8)

    slot = i % 2

    @pl.when(i == 0)
    def _():
        start_tile(dest_ref, 0)

    @pl.when(i + 1 < n)
    def _():
        start_tile(dest_next_ref, 1 - slot)

    pltpu.make_async_copy(yd_hbm.at[pl.ds(0, rows)], ybuf.at[slot], sem.at[slot]).wait()
    y1 = ybuf[slot, 0:tm].reshape(tm, d)
    y2 = ybuf[slot, tm:rows].reshape(tm, d)
    route = route_ref[...].T
    y = DN_ALPHA * x_ref[...] + (y1 * route[:, 0:1] + y2 * route[:, 1:2])
    o_ref[...] = _ln_rows(y, g_ref[...], b_ref[...])


def _moe_combine_ln(xt, y_disp, dest2, route, ln_g, ln_b):
    t, d = xt.shape
    n = t // ROUTE_TILE
    rows = EXPERT_TOP_K * ROUTE_TILE
    return pl.pallas_call(
        _combine_kernel,
        name="moe_combine_ln",
        grid=(n,),
        in_specs=[pl.BlockSpec((1, 1, rows), lambda i: (i, 0, 0), memory_space=pltpu.SMEM),
                  pl.BlockSpec((1, 1, rows), lambda i: (jnp.minimum(i + 1, n - 1), 0, 0), memory_space=pltpu.SMEM),
                  pl.BlockSpec((ROUTE_TILE, d), lambda i: (i, 0)),
                  pl.BlockSpec((ROUTE_ROWS, ROUTE_TILE), lambda i: (0, i)),
                  pl.BlockSpec((1, d), lambda i: (0, 0)),
                  pl.BlockSpec((1, d), lambda i: (0, 0)),
                  pl.BlockSpec(memory_space=pl.ANY)],
        out_specs=pl.BlockSpec((ROUTE_TILE, d), lambda i: (i, 0)),
        out_shape=jax.ShapeDtypeStruct((t, d), jnp.float32),
        scratch_shapes=[pltpu.VMEM((2, rows) + y_disp.shape[1:], jnp.float32), pltpu.SemaphoreType.DMA((2,))],
        compiler_params=pltpu.CompilerParams(dimension_semantics=("arbitrary",), vmem_limit_bytes=VMEM_LIMIT),
    )(dest2, dest2, xt, route, ln_g.reshape(1, d), ln_b.reshape(1, d), y_disp)


def _moe_sublayer(x, route, cnt, w1, w3, w2, layer, ln_g, ln_b):
    b, s, d = x.shape
    t = b * s
    xt = x.reshape(t, d)
    counts = cnt[N_GROUPS:N_GROUPS + N_EXPERTS, 0].astype(jnp.int32)
    n_tiles = (t * EXPERT_TOP_K) // EXPERT_TILE + N_EXPERTS
    tiles_per = (counts + EXPERT_TILE - 1) // EXPERT_TILE
    tile_end = jnp.cumsum(tiles_per)
    pad_start = (tile_end - tiles_per) * EXPERT_TILE
    n_used = tile_end[-1:].astype(jnp.int32)
    tile_ids = jnp.minimum(jnp.arange(n_tiles), n_used[0] - 1)
    tile_expert = jnp.sum(tile_ids[:, None] >= tile_end[None, :], axis=1).astype(jnp.int32)
    experts = route[2:4].astype(jnp.int32)
    first_row = jnp.sum(jnp.where(experts[..., None] == jnp.arange(N_EXPERTS), pad_start, 0), axis=-1)
    dest = first_row + route[4:6].astype(jnp.int32)
    dest2 = jnp.swapaxes(dest.reshape(EXPERT_TOP_K, t // ROUTE_TILE, ROUTE_TILE), 0, 1)
    dest2 = dest2.reshape(t // ROUTE_TILE, 1, EXPERT_TOP_K * ROUTE_TILE)
    last_tiles = jnp.where(tiles_per > 0, tile_end - 1, -1)
    tail_tiles = n_used[0] + jnp.arange(N_EXPERTS)
    tail_tiles = jnp.where(tail_tiles < n_tiles, tail_tiles, -1)
    zero_tiles = jnp.concatenate([last_tiles, tail_tiles]).astype(jnp.int32)
    x_disp = _moe_dispatch(xt, dest2, zero_tiles, n_tiles * EXPERT_TILE)
    y_disp = _moe_experts(x_disp, tile_expert, n_used, w1, w3, w2, layer)
    return _moe_combine_ln(xt, y_disp, dest2, route, ln_g, ln_b).reshape(b, s, d)


def kernel(x, mem, mem_wk, mem_wv, ev_w_in, ev_conv_w, ev_conv_b, ev_cnorm_g, ev_cnorm_b, ev_cmp_pe_k, ev_cmp_w1_k, ev_cmp_w2_k, ev_cmp_pe_v, ev_cmp_w1_v, ev_cmp_w2_v, ev_w_out, od_w_in, od_conv_w, od_w_out, ln_mix_g, ln_mix_b, xa_wq, xa_wo, ln_xa_g, ln_xa_b, moe_wg, moe_bg, moe_we, moe_be, moe_w1, moe_w3, moe_w2, ln_ffn_g, ln_ffn_b):
    b, s, d = x.shape
    assert d == D_MODEL and d == SUBLANES * LANES, "one (8, 128) tile per MoE dispatch row"
    assert s % SEL_TILE == 0 and s % SEQ_TILE == 0 and SEL_TILE % SEQ_TILE == 0 and SEL_TILE % Q_BLOCK == 0
    assert (b * s) % ROUTE_TILE == 0 and (b * s * EXPERT_TOP_K) % EXPERT_TILE == 0
    mem_k, mem_v = _mem_kv(mem, mem_wk, mem_wv)
    for layer in range(DEPTH):
        i = layer // 2
        if layer % 2 == 0:
            a, kv_in, k_sel, k_win, q_t, v_sel_t, v_win_t, gate_t = _even_in_proj(x, ev_w_in[i])
            a = _conformer_conv(a, ev_conv_w[i], ev_conv_b[i], ev_cnorm_g[i], ev_cnorm_b[i])
            k_cmp, v_cmp_t = _compress_kv(kv_in, ev_cmp_pe_k[i], ev_cmp_w1_k[i], ev_cmp_w2_k[i], ev_cmp_pe_v[i], ev_cmp_w1_v[i], ev_cmp_w2_v[i])
            o = _nsa_attention(q_t, gate_t, k_cmp, v_cmp_t, k_sel, v_sel_t, k_win, v_win_t)
            x = _proj_residual_ln(a.reshape(b * s, -1), o.reshape(b * s, -1), x.reshape(b * s, d), ev_w_out[i], ln_mix_g[layer], ln_mix_b[layer]).reshape(b, s, d)
        else:
            x = _odd_mixer_sublayer(x, od_w_in[i], od_conv_w[i], od_w_out[i], ln_mix_g[layer], ln_mix_b[layer])
        x, route, cnt = _xattn_router_sublayer(x, mem_k, mem_v, xa_wq[layer], xa_wo[layer], ln_xa_g[layer], ln_xa_b[layer],
                                               moe_wg[layer], moe_bg[layer], moe_we[layer], moe_be[layer])
        x = _moe_sublayer(x, route, cnt, moe_w1, moe_w3, moe_w2, layer, ln_ffn_g[layer], ln_ffn_b[layer])
    return x
```

```python
import functools

import numpy as np
import jax
import jax.numpy as jnp
from jax import lax
from jax.experimental import pallas as pl
from jax.experimental.pallas import tpu as pltpu

D_MODEL = 1024
DEPTH = 2
CONV_CH = D_MODEL // 2
CONV_WIDTH = 31
NSA_HEADS = 8
NSA_KV_HEADS = 2
HEAD_DIM = (D_MODEL // 2) // NSA_HEADS
CMP_BLOCK = 32
CMP_STRIDE = 16
SEL_BLOCK = 64
SEL_TOP_N = 16
WINDOW = 512
Q_BLOCK = 256
FORCE_SCORE = 1e4
SHORT_CONV_WIDTH = 3
XA_HEADS = 4
XA_HEAD_DIM = D_MODEL // XA_HEADS
N_GROUPS = 4
EXPERTS_PER_GROUP = 8
N_EXPERTS = N_GROUPS * EXPERTS_PER_GROUP
EXPERT_TOP_K = 2
DN_ALPHA = (2 * DEPTH) ** 0.25
LN_EPS = 1e-5
NEG_INF = -1e30
KV_COLS = NSA_KV_HEADS * HEAD_DIM
QCOLS = NSA_HEADS * HEAD_DIM
GATE_ROWS = 3 * NSA_HEADS

LANES = 128
SUBLANES = 8
HEADS_PER_KV = NSA_HEADS // NSA_KV_HEADS
QL = Q_BLOCK * HEADS_PER_KV
SEL_TILE = 1024
WIN_SPAN = WINDOW + Q_BLOCK
BLOCKS_PER_TILE = SEL_TILE // SEL_BLOCK
SEQ_TILE = 1024
HALO = 32
ODD_HALO = 8
ROUTE_TILE = 512
EXPERT_TILE = 512
ROUTE_ROWS = 8
VMEM_LIMIT = 56 * 1024 * 1024

_NT = (((1,), (1,)), ((), ()))


def _ln_rows(y, g, b):
    mu = jnp.mean(y, axis=-1, keepdims=True)
    yc = y - mu
    var = jnp.mean(yc * yc, axis=-1, keepdims=True)
    return yc * lax.rsqrt(var + LN_EPS) * g + b


def _even_in_kernel(x_ref, halo_ref, wa_ref, wkv_ref, wk2_ref, wqT_ref, wvT_ref, wgT_ref, cw_ref, cb_ref, cg_ref, cbeta_ref,
                    a_ref, kv_ref, ks_ref, kw_ref, qT_ref, vsT_ref, vwT_ref, gT_ref, ext_ref, win_ref):
    f32 = jnp.float32
    bf16 = jnp.bfloat16
    j = pl.program_id(1)
    ts = x_ref.shape[1]
    xb = x_ref[0].astype(bf16)

    def glu(rows_bf16):
        av = jnp.dot(rows_bf16, wa_ref[...], preferred_element_type=f32)
        return av[:, :CONV_CH] * jax.nn.sigmoid(av[:, CONV_CH:])

    a_halo = glu(halo_ref[0].astype(bf16))
    ext_ref[0:HALO, :] = jnp.where(j > 0, a_halo, jnp.zeros_like(a_halo))
    ext_ref[HALO:HALO + ts, :] = glu(xb)
    first = HALO - (CONV_WIDTH - 1)
    acc = jnp.zeros((ts, CONV_CH), f32)
    for p in range(SUBLANES):
        n_a = len(range(p, CONV_WIDTH, SUBLANES))
        rows = ts + SUBLANES * (n_a - 1)
        win_ref[0:rows, :] = ext_ref[first + p:first + p + rows, :]
        for a in range(n_a):
            k = SUBLANES * a + p
            acc = acc + cw_ref[k:k + 1, :] * win_ref[SUBLANES * a:SUBLANES * a + ts, :]
    y = _ln_rows(acc + cb_ref[...], cg_ref[...], cbeta_ref[...])
    a_ref[0] = (y * jax.nn.sigmoid(y)).astype(a_ref.dtype)

    kv_ref[0] = jnp.dot(xb, wkv_ref[...], preferred_element_type=f32)
    k2 = jnp.dot(xb, wk2_ref[...], preferred_element_type=f32)
    ks_ref[0] = k2[:, :KV_COLS].astype(bf16)
    kw_ref[0] = k2[:, KV_COLS:].astype(bf16)
    qT_ref[0] = lax.dot_general(wqT_ref[...], xb, _NT, preferred_element_type=f32).astype(bf16)
    vT = lax.dot_general(wvT_ref[...], xb, _NT, preferred_element_type=f32).astype(bf16)
    vsT_ref[0, 0] = vT[:KV_COLS]
    for j in range(SEQ_TILE // Q_BLOCK):
        vwT_ref[0, j] = vT[KV_COLS:, j * Q_BLOCK:(j + 1) * Q_BLOCK]
    gT_ref[0] = lax.dot_general(wgT_ref[...], xb, _NT, preferred_element_type=f32)


def _even_in_proj(x, w_in, conv_w, conv_b, cn_g, cn_b):
    b, s, d = x.shape
    bf16 = jnp.bfloat16
    c = np.cumsum((0, CONV_CH, CONV_CH, QCOLS, KV_COLS, KV_COLS, KV_COLS, KV_COLS, KV_COLS, KV_COLS, GATE_ROWS))
    col = lambda i, j: w_in[:, c[i]:c[j]]
    wa = col(0, 2).astype(bf16)
    wq_t = col(2, 3).T.astype(bf16)
    wkv = col(3, 5).astype(bf16)
    wk2 = jnp.concatenate([col(5, 6), col(7, 8)], axis=1).astype(bf16)
    wv_t = jnp.concatenate([col(6, 7), col(8, 9)], axis=1).T.astype(bf16)
    wg_t = col(9, 10).T.astype(bf16)
    ts = SEQ_TILE
    per_sel = SEL_TILE // ts
    per_halo = ts // HALO
    full = lambda shape: pl.BlockSpec(shape, lambda i, j: (0,) * len(shape))
    chan = lambda v: v.reshape(1, CONV_CH)
    return pl.pallas_call(
        _even_in_kernel,
        name="even_in_proj_conv",
        grid=(b, s // ts),
        in_specs=[pl.BlockSpec((1, ts, d), lambda i, j: (i, j, 0)),
                  pl.BlockSpec((1, HALO, d), lambda i, j: (i, jnp.maximum(j * per_halo - 1, 0), 0)),
                  full(wa.shape), full(wkv.shape), full(wk2.shape), full(wq_t.shape), full(wv_t.shape), full(wg_t.shape),
                  full(conv_w.shape), full((1, CONV_CH)), full((1, CONV_CH)), full((1, CONV_CH))],
        out_specs=[pl.BlockSpec((1, ts, CONV_CH), lambda i, j: (i, j, 0)),
                   pl.BlockSpec((1, ts, 2 * KV_COLS), lambda i, j: (i, j, 0)),
                   pl.BlockSpec((1, ts, KV_COLS), lambda i, j: (i, j, 0)),
                   pl.BlockSpec((1, ts, KV_COLS), lambda i, j: (i, j, 0)),
                   pl.BlockSpec((1, QCOLS, ts), lambda i, j: (i, 0, j)),
                   pl.BlockSpec((1, 1, KV_COLS, ts), lambda i, j: (i, j // per_sel, 0, j % per_sel)),
                   pl.BlockSpec((1, ts // Q_BLOCK, KV_COLS, Q_BLOCK), lambda i, j: (i, j, 0, 0)),
                   pl.BlockSpec((1, GATE_ROWS, ts), lambda i, j: (i, 0, j))],
        out_shape=[jax.ShapeDtypeStruct((b, s, CONV_CH), bf16),
                   jax.ShapeDtypeStruct((b, s, 2 * KV_COLS), jnp.float32),
                   jax.ShapeDtypeStruct((b, s, KV_COLS), bf16),
                   jax.ShapeDtypeStruct((b, s, KV_COLS), bf16),
                   jax.ShapeDtypeStruct((b, QCOLS, s), bf16),
                   jax.ShapeDtypeStruct((b, s // SEL_TILE, KV_COLS, SEL_TILE), bf16),
                   jax.ShapeDtypeStruct((b, s // Q_BLOCK, KV_COLS, Q_BLOCK), bf16),
                   jax.ShapeDtypeStruct((b, GATE_ROWS, s), jnp.float32)],
        scratch_shapes=[pltpu.VMEM((HALO + ts, CONV_CH), jnp.float32), pltpu.VMEM((HALO + ts, CONV_CH), jnp.float32)],
        compiler_params=pltpu.CompilerParams(dimension_semantics=("arbitrary", "arbitrary"), vmem_limit_bytes=VMEM_LIMIT),
    )(x, x, wa, wkv, wk2, wq_t, wv_t, wg_t, conv_w, chan(conv_b), chan(cn_g), chan(cn_b))


def _compress_kernel(r_ref, pe_ref, w1_ref, w2_ref, w2T_ref, o_ref, oT_ref):
    f32 = jnp.float32
    bf16 = jnp.bfloat16
    nch = o_ref.shape[1]
    hw = w1_ref.shape[3]
    first = jnp.zeros((nch, hw), f32)
    second = jnp.zeros((nch, hw), f32)
    for l in range(CMP_STRIDE):
        tok = r_ref[pl.ds(l, nch, stride=CMP_STRIDE), :]
        first = first + jnp.dot((tok + pe_ref[0, l:l + 1, :]).astype(bf16), w1_ref[0, l], preferred_element_type=f32)
        second = second + jnp.dot((tok + pe_ref[0, CMP_STRIDE + l:CMP_STRIDE + l + 1, :]).astype(bf16),
                                  w1_ref[0, CMP_STRIDE + l], preferred_element_type=f32)
    second_next = jnp.concatenate([second[1:], jnp.zeros_like(second[0:1])], axis=0)
    hid = jax.nn.gelu(first + second_next).astype(bf16)
    o_ref[0] = jnp.dot(hid, w2_ref[0], preferred_element_type=f32).astype(bf16)
    oT_ref[0] = lax.dot_general(w2T_ref[0], hid, _NT, preferred_element_type=f32).astype(bf16)


def _compress_kv(kv_in, pe_k, w1_k, w2_k, pe_v, w1_v, w2_v):
    b, s, _ = kv_in.shape
    bf16 = jnp.bfloat16
    nch = s // CMP_STRIDE
    g = NSA_KV_HEADS
    hidden = w1_k.shape[1]

    def expand(pe, w1, w2):
        same = jnp.eye(g, dtype=bool)
        w1r = w1.reshape(CMP_BLOCK, HEAD_DIM, hidden)
        w1e = jnp.where(same[None, :, None, :, None], w1r[:, None, :, None, :], 0.0)
        w2e = jnp.where(same[:, None, :, None], w2[None, :, None, :], 0.0)
        return jnp.tile(pe, (1, g)), w1e.reshape(CMP_BLOCK, KV_COLS, g * hidden), w2e.reshape(g * hidden, KV_COLS)

    pk, w1k, w2k = expand(pe_k, w1_k, w2_k)
    pv, w1v, w2v = expand(pe_v, w1_v, w2_v)
    pe = jnp.stack([pk, pv])
    w1 = jnp.stack([w1k, w1v]).astype(bf16)
    w2 = jnp.stack([w2k, w2v]).astype(bf16)
    w2t = jnp.swapaxes(w2, 1, 2)
    o, o_t = pl.pallas_call(
        _compress_kernel,
        name="compress_kv",
        grid=(b, 2),
        in_specs=[pl.BlockSpec((s, KV_COLS), lambda i, j: (i, j)),
                  pl.BlockSpec((1,) + pe.shape[1:], lambda i, j: (j, 0, 0)),
                  pl.BlockSpec((1,) + w1.shape[1:], lambda i, j: (j, 0, 0, 0)),
                  pl.BlockSpec((1,) + w2.shape[1:], lambda i, j: (j, 0, 0)),
                  pl.BlockSpec((1,) + w2t.shape[1:], lambda i, j: (j, 0, 0))],
        out_specs=[pl.BlockSpec((1, nch, KV_COLS), lambda i, j: (2 * i + j, 0, 0)),
                   pl.BlockSpec((1, KV_COLS, nch), lambda i, j: (2 * i + j, 0, 0))],
        out_shape=[jax.ShapeDtypeStruct((b * 2, nch, KV_COLS), bf16), jax.ShapeDtypeStruct((b * 2, KV_COLS, nch), bf16)],
        compiler_params=pltpu.CompilerParams(dimension_semantics=("arbitrary", "arbitrary"), vmem_limit_bytes=VMEM_LIMIT),
    )(kv_in.reshape(b * s, 2 * KV_COLS), pe, w1, w2, w2t)
    return o.reshape(b, 2, nch, KV_COLS)[:, 0], o_t.reshape(b, 2, KV_COLS, nch)[:, 1]


def _proj_ln_kernel(a_ref, o_ref, x_ref, w_ref, g_ref, b_ref, y_ref):
    lhs = jnp.concatenate([a_ref[...], o_ref[...]], axis=1)
    mix = jnp.dot(lhs, w_ref[...], preferred_element_type=jnp.float32)
    y_ref[...] = _ln_rows(DN_ALPHA * x_ref[...] + mix, g_ref[...], b_ref[...])


def _proj_residual_ln(a, o, x, w, ln_g, ln_b):
    t, d = x.shape
    tm = SEQ_TILE
    return pl.pallas_call(
        _proj_ln_kernel,
        name="proj_residual_ln",
        grid=(t // tm,),
        in_specs=[pl.BlockSpec((tm, a.shape[1]), lambda i: (i, 0)),
                  pl.BlockSpec((tm, o.shape[1]), lambda i: (i, 0)),
                  pl.BlockSpec((tm, d), lambda i: (i, 0)),
                  pl.BlockSpec(w.shape, lambda i: (0, 0)),
                  pl.BlockSpec((1, d), lambda i: (0, 0)),
                  pl.BlockSpec((1, d), lambda i: (0, 0))],
        out_specs=pl.BlockSpec((tm, d), lambda i: (i, 0)),
        out_shape=jax.ShapeDtypeStruct((t, d), jnp.float32),
        compiler_params=pltpu.CompilerParams(dimension_semantics=("arbitrary",), vmem_limit_bytes=VMEM_LIMIT),
    )(a, o, x, w.astype(jnp.bfloat16), ln_g.reshape(1, d), ln_b.reshape(1, d))


def _mem_kv_kernel(m_ref, wk_ref, wv_ref, k_ref, v_ref):
    mb = m_ref[...].astype(jnp.bfloat16)
    k_ref[...] = jnp.dot(mb, wk_ref[...], preferred_element_type=jnp.float32).astype(jnp.bfloat16)
    v_ref[...] = jnp.dot(mb, wv_ref[...], preferred_element_type=jnp.float32).astype(jnp.bfloat16)


def _mem_kv(mem, wk, wv):
    b, m, d = mem.shape
    bf16 = jnp.bfloat16
    k, v = pl.pallas_call(
        _mem_kv_kernel,
        name="mem_kv",
        grid=(b,),
        in_specs=[pl.BlockSpec((m, d), lambda i: (i, 0)), pl.BlockSpec((d, d), lambda i: (0, 0)), pl.BlockSpec((d, d), lambda i: (0, 0))],
        out_specs=[pl.BlockSpec((m, d), lambda i: (i, 0)), pl.BlockSpec((m, d), lambda i: (i, 0))],
        out_shape=[jax.ShapeDtypeStruct((b * m, d), bf16), jax.ShapeDtypeStruct((b * m, d), bf16)],
        compiler_params=pltpu.CompilerParams(dimension_semantics=("arbitrary",), vmem_limit_bytes=VMEM_LIMIT),
    )(mem.reshape(b * m, d), wk.astype(bf16), wv.astype(bf16))
    return k.reshape(b, m, d), v.reshape(b, m, d)


def _xattn_kernel(x_ref, k_ref, v_ref, wq_ref, wo_ref, g_ref, b_ref, rw_ref, rb_ref, tri_ref,
                  y_ref, route_ref, cnt_ref, run_ref):
    f32 = jnp.float32
    bf16 = jnp.bfloat16
    x = x_ref[0]
    q = (jnp.dot(x.astype(bf16), wq_ref[...], preferred_element_type=f32).astype(bf16)
         * jnp.asarray(XA_HEAD_DIM ** -0.5, bf16))
    heads = []
    for h in range(XA_HEADS):
        cols = slice(h * XA_HEAD_DIM, (h + 1) * XA_HEAD_DIM)
        s = lax.dot_general(q[:, cols], k_ref[0, :, cols], _NT, preferred_element_type=f32)
        m = jnp.max(s, axis=1, keepdims=True)
        p = jnp.exp(s - m)
        p = p / jnp.sum(p, axis=1, keepdims=True)
        heads.append(jnp.dot(p.astype(bf16), v_ref[0, :, cols], preferred_element_type=f32).astype(bf16))
    att = jnp.concatenate(heads, axis=1)
    out = jnp.dot(att, wo_ref[...], preferred_element_type=f32)
    y = _ln_rows(DN_ALPHA * x + out, g_ref[...], b_ref[...])
    y_ref[0] = y
    first_step = (pl.program_id(0) == 0) & (pl.program_id(1) == 0)
    _route_tile(y.astype(bf16), first_step, rw_ref, rb_ref, tri_ref, route_ref, cnt_ref, run_ref)


def _xattn_router_sublayer(x, mem_k, mem_v, wq, wo, ln_g, ln_b, wg, bg, we, be):
    b, s, d = x.shape
    m = mem_k.shape[1]
    ts = SEQ_TILE
    per_seq = s // ts
    bf16 = jnp.bfloat16
    rw, rb, tri = _router_operands(wg, bg, we, be, ts)
    const = lambda shape: pl.BlockSpec(shape, lambda i, j: (0,) * len(shape))
    return pl.pallas_call(
        _xattn_kernel,
        name="xattn_router_sublayer",
        grid=(b, per_seq),
        in_specs=[pl.BlockSpec((1, ts, d), lambda i, j: (i, j, 0)),
                  pl.BlockSpec((1, m, d), lambda i, j: (i, 0, 0)),
                  pl.BlockSpec((1, m, d), lambda i, j: (i, 0, 0)),
                  const((d, d)), const((d, d)), const((1, d)), const((1, d)),
                  const(rw.shape), const(rb.shape), const(tri.shape)],
        out_specs=[pl.BlockSpec((1, ts, d), lambda i, j: (i, j, 0)),
                   pl.BlockSpec((ROUTE_ROWS, ts), lambda i, j: (0, i * per_seq + j)),
                   const((LANES, 1))],
        out_shape=[jax.ShapeDtypeStruct((b, s, d), jnp.float32),
                   jax.ShapeDtypeStruct((ROUTE_ROWS, b * s), jnp.float32),
                   jax.ShapeDtypeStruct((LANES, 1), jnp.float32)],
        scratch_shapes=[pltpu.VMEM((LANES, 1), jnp.float32)],
        compiler_params=pltpu.CompilerParams(dimension_semantics=("arbitrary", "arbitrary"), vmem_limit_bytes=VMEM_LIMIT),
    )(x, mem_k, mem_v, wq.astype(bf16), wo.astype(bf16), ln_g.reshape(1, d), ln_b.reshape(1, d), rw, rb, tri)


def _odd_kernel(x_ref, halo_ref, wb_ref, wc_ref, wh_ref, cw_ref, wo_ref, g_ref, b_ref, y_ref):
    f32 = jnp.float32
    bf16 = jnp.bfloat16
    j = pl.program_id(1)
    ts = x_ref.shape[1]
    x = x_ref[0]
    xe = jnp.concatenate([halo_ref[0], x], axis=0).astype(bf16)
    u = (jnp.dot(xe, wc_ref[...], preferred_element_type=f32) * jnp.dot(xe, wh_ref[...], preferred_element_type=f32))
    row = lax.broadcasted_iota(jnp.int32, (ODD_HALO + ts, 1), 0)
    u = jnp.where((row >= ODD_HALO) | (j > 0), u, 0.0)
    conv = jnp.zeros((ts, u.shape[1]), f32)
    for k in range(SHORT_CONV_WIDTH):
        off = ODD_HALO - (SHORT_CONV_WIDTH - 1) + k
        conv = conv + cw_ref[k:k + 1, :] * u[off:off + ts, :]
    gate_b = jnp.dot(xe[ODD_HALO:], wb_ref[...], preferred_element_type=f32)
    mix = jnp.dot((gate_b * conv).astype(bf16), wo_ref[...], preferred_element_type=f32)
    y_ref[0] = _ln_rows(DN_ALPHA * x + mix, g_ref[...], b_ref[...])


def _odd_mixer_sublayer(x, w_in, conv_w, w_out, ln_g, ln_b):
    b, s, d = x.shape
    ts = SEQ_TILE
    per = ts // ODD_HALO
    bf16 = jnp.bfloat16
    wb, wc, wh = (w_in[:, i * d:(i + 1) * d].astype(bf16) for i in range(3))
    full = lambda shape: pl.BlockSpec(shape, lambda i, j: (0,) * len(shape))
    return pl.pallas_call(
        _odd_kernel,
        name="odd_mixer_sublayer",
        grid=(b, s // ts),
        in_specs=[pl.BlockSpec((1, ts, d), lambda i, j: (i, j, 0)),
                  pl.BlockSpec((1, ODD_HALO, d), lambda i, j: (i, jnp.maximum(j * per - 1, 0), 0)),
                  full((d, d)), full((d, d)), full((d, d)), full(conv_w.shape), full((d, d)), full((1, d)), full((1, d))],
        out_specs=pl.BlockSpec((1, ts, d), lambda i, j: (i, j, 0)),
        out_shape=jax.ShapeDtypeStruct((b, s, d), jnp.float32),
        compiler_params=pltpu.CompilerParams(dimension_semantics=("arbitrary", "arbitrary"), vmem_limit_bytes=VMEM_LIMIT),
    )(x, x, wb, wc, wh, conv_w, w_out.astype(bf16), ln_g.reshape(1, d), ln_b.reshape(1, d))


def _cmp_to_sel_matrix(n_cmp, n_sel):
    c0 = np.arange(n_cmp) * CMP_STRIDE
    s0 = np.arange(n_sel) * SEL_BLOCK
    ov = np.minimum(c0[:, None] + CMP_BLOCK, s0[None, :] + SEL_BLOCK) - np.maximum(c0[:, None], s0[None, :])
    return (np.clip(ov, 0, None) / CMP_BLOCK).astype(np.float32)


def _nsa_kernel(qT_ref, gT_ref, kc_ref, vcT_ref, mselT_ref, ks_ref, vsT_ref, kw_ref, vwT_ref, o_ref,
                sel_ref, *, n_sel):
    c = pl.program_id(1)
    q0 = c * Q_BLOCK
    f32 = jnp.float32
    bf16 = jnp.bfloat16
    n_cmp_pad = kc_ref.shape[1]

    lane_q = lax.broadcasted_iota(jnp.int32, (1, QL), 1) % Q_BLOCK
    t_row = q0 + lane_q
    t_row_q = q0 + lax.broadcasted_iota(jnp.int32, (1, Q_BLOCK), 1)
    cur_q = t_row_q // SEL_BLOCK

    top_n = min(SEL_TOP_N, n_sel)
    m_iota = lax.broadcasted_iota(jnp.int32, (n_sel, Q_BLOCK), 0)
    forced = (m_iota == 0) | (m_iota == cur_q) | (m_iota == cur_q - 1)
    valid = m_iota <= cur_q

    q_g, o_c = [], []
    for g in range(NSA_KV_HEADS):
        pieces = []
        for hg in range(HEADS_PER_KV):
            h = g * HEADS_PER_KV + hg
            qh = qT_ref[0, h * HEAD_DIM:(h + 1) * HEAD_DIM, :] * jnp.asarray(HEAD_DIM ** -0.5, bf16)
            z = jnp.zeros_like(qh)
            pieces.append(jnp.concatenate([qh, z] if g == 0 else [z, qh], axis=0))
        qTp = jnp.concatenate(pieces, axis=1)
        q_g.append(qTp)
        rows = slice(g * HEAD_DIM, (g + 1) * HEAD_DIM)

        s_c = jnp.dot(kc_ref[0], qTp, preferred_element_type=f32)
        n_iota = lax.broadcasted_iota(jnp.int32, (n_cmp_pad, QL), 0)
        mask_c = (n_iota * CMP_STRIDE + (CMP_BLOCK - 1)) <= t_row
        s_c = jnp.where(mask_c, s_c, NEG_INF)
        m_c = jnp.max(s_c, axis=0, keepdims=True)
        p_c = jnp.where(mask_c, jnp.exp(s_c - m_c), 0.0)
        l_c = jnp.sum(p_c, axis=0, keepdims=True)
        p_c = p_c * jnp.where(l_c > 0.0, 1.0 / l_c, 0.0)
        p_cb = p_c.astype(bf16)
        o_c.append(jnp.dot(vcT_ref[0], p_cb, preferred_element_type=f32)[rows])
        imp4 = jnp.dot(mselT_ref[...], p_cb, preferred_element_type=f32)
        imp = imp4[:, 0:Q_BLOCK]
        for hg in range(1, HEADS_PER_KV):
            imp = imp + imp4[:, hg * Q_BLOCK:(hg + 1) * Q_BLOCK]

        remaining = jnp.where(valid, jnp.where(forced, FORCE_SCORE, imp), NEG_INF)
        chosen = jnp.zeros((n_sel, Q_BLOCK), jnp.bool_)
        for _ in range(top_n):
            best = jnp.max(remaining, axis=0, keepdims=True)
            first = jnp.min(jnp.where(remaining == best, m_iota, n_sel), axis=0, keepdims=True)
            hit = m_iota == first
            chosen = chosen | hit
            remaining = jnp.where(hit, -jnp.inf, remaining)
        sel_ref[g] = jnp.where(chosen & valid, 0.0, NEG_INF)

    def sel_tile(j, carry, causal, size=SEL_TILE):
        k_tile = ks_ref[0, j, 0:size, :]
        v_tile = vsT_ref[0, j, :, 0:size]
        blk0 = pl.multiple_of(j * BLOCKS_PER_TILE, BLOCKS_PER_TILE)
        out = []
        for g in range(NSA_KV_HEADS):
            m_i, l_i, acc = carry[g]
            s = jnp.dot(k_tile, q_g[g], preferred_element_type=f32)
            selrows = sel_ref[g, pl.ds(blk0, BLOCKS_PER_TILE), :]
            bias = jnp.concatenate(
                [jnp.broadcast_to(selrows[r:r + 1, :], (SEL_BLOCK, Q_BLOCK)) for r in range(size // SEL_BLOCK)], axis=0)
            if causal:
                key = j * SEL_TILE + lax.broadcasted_iota(jnp.int32, (size, Q_BLOCK), 0)
                bias = jnp.where(key <= t_row_q, bias, NEG_INF)
            s = s + jnp.concatenate([bias] * HEADS_PER_KV, axis=1)
            m_new = jnp.maximum(m_i, jnp.max(s, axis=0, keepdims=True))
            alpha = jnp.exp(m_i - m_new)
            p = jnp.exp(s - m_new)
            l_new = alpha * l_i + jnp.sum(p, axis=0, keepdims=True)
            pv = jnp.dot(v_tile, p.astype(bf16), preferred_element_type=f32)
            out.append((m_new, l_new, alpha * acc + pv))
        return tuple(out)

    n_full = q0 // SEL_TILE
    init = tuple((jnp.full((1, QL), NEG_INF, f32), jnp.zeros((1, QL), f32), jnp.zeros((KV_COLS, QL), f32))
                 for _ in range(NSA_KV_HEADS))
    carry = lax.fori_loop(0, n_full, functools.partial(sel_tile, causal=False), init)
    carry = lax.switch(
        (q0 % SEL_TILE) // Q_BLOCK,
        [functools.partial(sel_tile, n_full, causal=True, size=(pos + 1) * Q_BLOCK) for pos in range(SEL_TILE // Q_BLOCK)],
        carry)

    start = pl.multiple_of(jnp.maximum(q0 - WINDOW, 0), Q_BLOCK)
    j0 = start // Q_BLOCK
    k_win = kw_ref[0, pl.ds(start, WIN_SPAN), :]
    key_w = start + lax.broadcasted_iota(jnp.int32, (WIN_SPAN, Q_BLOCK), 0)
    bias_w = jnp.where(key_w <= t_row_q, 0.0, NEG_INF)
    bias_w = jnp.where(key_w > t_row_q - WINDOW, bias_w, NEG_INF)
    bias_w = jnp.concatenate([bias_w] * HEADS_PER_KV, axis=1)
    for g in range(NSA_KV_HEADS):
        rows = slice(g * HEAD_DIM, (g + 1) * HEAD_DIM)
        m_s, l_s, acc_s = carry[g]
        o_s = acc_s[rows] * (1.0 / l_s)
        s_w = jnp.dot(k_win, q_g[g], preferred_element_type=f32) + bias_w
        m_w = jnp.max(s_w, axis=0, keepdims=True)
        p_w = jnp.exp(s_w - m_w)
        l_w = jnp.sum(p_w, axis=0, keepdims=True)
        p_wb = p_w.astype(bf16)
        acc_w = jnp.zeros((KV_COLS, QL), f32)
        for i in range(WIN_SPAN // Q_BLOCK):
            acc_w = acc_w + jnp.dot(vwT_ref[0, j0 + i], p_wb[i * Q_BLOCK:(i + 1) * Q_BLOCK, :], preferred_element_type=f32)
        o_w = acc_w[rows] * (1.0 / l_w)

        for pair in range(HEADS_PER_KV // 2):
            halves = []
            for hg in (2 * pair, 2 * pair + 1):
                h = g * HEADS_PER_KV + hg
                lanes = slice(hg * Q_BLOCK, (hg + 1) * Q_BLOCK)
                gate = jax.nn.sigmoid(gT_ref[0, 3 * h:3 * h + 3, :])
                halves.append(gate[0:1] * o_c[g][:, lanes] + gate[1:2] * o_s[:, lanes] + gate[2:3] * o_w[:, lanes])
            both = jnp.concatenate(halves, axis=0)
            col0 = (g * HEADS_PER_KV + 2 * pair) * HEAD_DIM
            o_ref[0, :, col0:col0 + 2 * HEAD_DIM] = both.T.astype(o_ref.dtype)


def _nsa_attention(qT, gT, kc, vcT, ks, vsT, kw, vwT):
    b, hd, s = qT.shape
    n_sel = s // SEL_BLOCK
    nc = kc.shape[1]
    n_cmp = (s - CMP_BLOCK) // CMP_STRIDE + 1
    mselT = jnp.asarray(np.pad(_cmp_to_sel_matrix(n_cmp, n_sel).T, ((0, 0), (0, nc - n_cmp))), jnp.bfloat16)
    ks4 = ks.reshape(b, s // SEL_TILE, SEL_TILE, KV_COLS)
    return pl.pallas_call(
        functools.partial(_nsa_kernel, n_sel=n_sel),
        name="nsa_attention",
        grid=(b, s // Q_BLOCK),
        in_specs=[
            pl.BlockSpec((1, hd, Q_BLOCK), lambda i, c: (i, 0, c)),
            pl.BlockSpec((1, 3 * NSA_HEADS, Q_BLOCK), lambda i, c: (i, 0, c)),
            pl.BlockSpec((1, nc, KV_COLS), lambda i, c: (i, 0, 0)),
            pl.BlockSpec((1, KV_COLS, nc), lambda i, c: (i, 0, 0)),
            pl.BlockSpec((n_sel, nc), lambda i, c: (0, 0)),
            pl.BlockSpec((1, s // SEL_TILE, SEL_TILE, KV_COLS), lambda i, c: (i, 0, 0, 0)),
            pl.BlockSpec((1, s // SEL_TILE, KV_COLS, SEL_TILE), lambda i, c: (i, 0, 0, 0)),
            pl.BlockSpec((1, s, KV_COLS), lambda i, c: (i, 0, 0)),
            pl.BlockSpec((1, s // Q_BLOCK, KV_COLS, Q_BLOCK), lambda i, c: (i, 0, 0, 0)),
        ],
        out_specs=pl.BlockSpec((1, Q_BLOCK, hd), lambda i, c: (i, c, 0)),
        out_shape=jax.ShapeDtypeStruct((b, s, hd), jnp.bfloat16),
        scratch_shapes=[pltpu.VMEM((NSA_KV_HEADS, n_sel, Q_BLOCK), jnp.float32)],
        compiler_params=pltpu.CompilerParams(dimension_semantics=("arbitrary", "arbitrary")),
    )(qT, gT, kc, vcT, mselT, ks4, vsT, kw, vwT)


def _route_tile(xb, first_step, wT_ref, b_ref, tri_ref, route_ref, cnt_ref, run_ref):
    f32 = jnp.float32

    @pl.when(first_step)
    def _():
        run_ref[...] = jnp.zeros_like(run_ref)

    tm = xb.shape[0]
    logits = lax.dot_general(wT_ref[...], xb, _NT, preferred_element_type=f32) + b_ref[...]
    sub = lax.broadcasted_iota(jnp.int32, (LANES, tm), 0)
    is_g = sub < N_GROUPS
    gl = jnp.where(is_g, logits, NEG_INF)
    g_max = jnp.max(gl, axis=0, keepdims=True)
    g_star = jnp.min(jnp.where(gl == g_max, sub, LANES), axis=0, keepdims=True)
    p_group = 1.0 / jnp.sum(jnp.where(is_g, jnp.exp(gl - g_max), 0.0), axis=0, keepdims=True)
    lo = N_GROUPS + g_star * EXPERTS_PER_GROUP
    in_grp = (sub >= lo) & (sub < lo + EXPERTS_PER_GROUP)
    el = jnp.where(in_grp, logits, NEG_INF)
    v1 = jnp.max(el, axis=0, keepdims=True)
    i1 = jnp.min(jnp.where(el == v1, sub, LANES), axis=0, keepdims=True)
    el2 = jnp.where(sub == i1, NEG_INF, el)
    v2 = jnp.max(el2, axis=0, keepdims=True)
    i2 = jnp.min(jnp.where(el2 == v2, sub, LANES), axis=0, keepdims=True)
    e21 = jnp.exp(v2 - v1)
    gate1 = p_group * (1.0 / (1.0 + e21))
    gate2 = p_group * (e21 / (1.0 + e21))
    oh1 = (sub == i1).astype(f32)
    oh2 = (sub == i2).astype(f32)
    both = oh1 + oh2
    before = jnp.dot(both.astype(jnp.bfloat16), tri_ref[...], preferred_element_type=f32) + run_ref[...]
    rank1 = jnp.sum(oh1 * before, axis=0, keepdims=True)
    rank2 = jnp.sum(oh2 * before, axis=0, keepdims=True)
    run_ref[...] = run_ref[...] + jnp.sum(both, axis=1, keepdims=True)
    cnt_ref[...] = run_ref[...]
    zero = jnp.zeros_like(gate1)
    route_ref[...] = jnp.concatenate(
        [gate1, gate2, (i1 - N_GROUPS).astype(f32), (i2 - N_GROUPS).astype(f32), rank1, rank2, zero, zero], axis=0)


def _router_operands(wg, bg, we, be, tile):
    d = wg.shape[0]
    pad = LANES - N_GROUPS - N_EXPERTS
    w_t = jnp.concatenate([wg, we.reshape(d, N_EXPERTS), jnp.zeros((d, pad), wg.dtype)], axis=1).T
    bias = jnp.concatenate([bg, be.reshape(N_EXPERTS), jnp.zeros((pad,), bg.dtype)])[:, None]
    tri = jnp.asarray(np.triu(np.ones((tile, tile), np.float32), 1), jnp.bfloat16)
    return w_t.astype(jnp.bfloat16), bias, tri


def _dispatch_kernel(zt_ref, dest_ref, x_ref, xd_hbm, zbuf, ring, sem, zsem):
    tm = x_ref.shape[0]

    @pl.when(pl.program_id(0) == 0)
    def _():
        zbuf[...] = jnp.zeros_like(zbuf)

        def zero_copy(k):
            start = pl.multiple_of(jnp.maximum(zt_ref[k], 0) * EXPERT_TILE, EXPERT_TILE)
            return pltpu.make_async_copy(zbuf, xd_hbm.at[pl.ds(start, EXPERT_TILE)], zsem)

        def start_body(k, c):
            @pl.when(zt_ref[k] >= 0)
            def _():
                zero_copy(k).start()
            return c

        def wait_body(k, c):
            @pl.when(zt_ref[k] >= 0)
            def _():
                zero_copy(k).wait()
            return c

        lax.fori_loop(0, zt_ref.shape[0], start_body, 0)
        lax.fori_loop(0, zt_ref.shape[0], wait_body, 0)

    i = pl.program_id(0)
    n = pl.num_programs(0)
    slot = i % 2

    def wait_tile(of_slot):
        for k in range(EXPERT_TOP_K):
            pltpu.make_async_copy(ring.at[of_slot], xd_hbm.at[pl.ds(0, tm)], sem.at[of_slot]).wait()

    @pl.when(i >= 2)
    def _():
        wait_tile(slot)

    ring[slot] = x_ref[...].reshape(ring.shape[1:])

    def body(r, c):
        for k in range(EXPERT_TOP_K):
            pltpu.make_async_copy(
                ring.at[slot, r], xd_hbm.at[dest_ref[0, 0, k * tm + r]], sem.at[slot]).start(priority=k)
        return c

    lax.fori_loop(0, tm, body, 0, unroll=8)

    @pl.when(i == n - 1)
    def _():
        wait_tile(slot)

        @pl.when(n > 1)
        def _():
            wait_tile(1 - slot)


def _moe_dispatch(xt, dest2, zero_tiles, n_rows):
    t, d = xt.shape
    row = (SUBLANES, d // SUBLANES)
    grid_spec = pltpu.PrefetchScalarGridSpec(
        num_scalar_prefetch=1,
        grid=(t // ROUTE_TILE,),
        in_specs=[pl.BlockSpec((1, 1, EXPERT_TOP_K * ROUTE_TILE), lambda i, zt: (i, 0, 0), memory_space=pltpu.SMEM),
                  pl.BlockSpec((ROUTE_TILE, d), lambda i, zt: (i, 0))],
        out_specs=pl.BlockSpec(memory_space=pl.ANY),
        scratch_shapes=[pltpu.VMEM((EXPERT_TILE,) + row, xt.dtype), pltpu.VMEM((2, ROUTE_TILE) + row, xt.dtype),
                        pltpu.SemaphoreType.DMA((2,)), pltpu.SemaphoreType.DMA(())],
    )
    return pl.pallas_call(
        _dispatch_kernel,
        name="moe_dispatch",
        grid_spec=grid_spec,
        out_shape=jax.ShapeDtypeStruct((n_rows,) + row, xt.dtype),
        compiler_params=pltpu.CompilerParams(dimension_semantics=("arbitrary",), has_side_effects=True),
    )(zero_tiles, dest2, xt)


def _expert_kernel(te_ref, nu_ref, xd_ref, w1_ref, w3_ref, w2_ref, y_ref, w1b, w3b, w2b):
    i = pl.program_id(0)
    used = i < nu_ref[0]
    new_expert = (i == 0) | (te_ref[i] != te_ref[jnp.maximum(i - 1, 0)])

    @pl.when(used & new_expert)
    def _():
        w1b[...] = w1_ref[0, 0].astype(jnp.bfloat16)
        w3b[...] = w3_ref[0, 0].astype(jnp.bfloat16)
        w2b[...] = w2_ref[0, 0].astype(jnp.bfloat16)

    @pl.when(used)
    def _():
        xb = xd_ref[...].reshape(xd_ref.shape[0], w1b.shape[0]).astype(jnp.bfloat16)
        h1 = jnp.dot(xb, w1b[...], preferred_element_type=jnp.float32)
        h3 = jnp.dot(xb, w3b[...], preferred_element_type=jnp.float32)
        a = (h1 * jax.nn.sigmoid(h1) * h3).astype(jnp.bfloat16)
        y_ref[...] = jnp.dot(a, w2b[...], preferred_element_type=jnp.float32).reshape(y_ref.shape)

    @pl.when(jnp.logical_not(used))
    def _():
        y_ref[...] = jnp.zeros_like(y_ref)


def _moe_experts(x_disp, tile_expert, n_used, w1, w3, w2, layer):
    n_rows = x_disp.shape[0]
    row = x_disp.shape[1:]
    d = row[0] * row[1]
    n_tiles = n_rows // EXPERT_TILE
    hid = w1.shape[3]

    def row_map(i, te, nu):
        return (i, 0, 0)

    def w_map(i, te, nu):
        return (layer, te[i], 0, 0)

    grid_spec = pltpu.PrefetchScalarGridSpec(
        num_scalar_prefetch=2,
        grid=(n_tiles,),
        in_specs=[pl.BlockSpec((EXPERT_TILE,) + row, row_map),
                  pl.BlockSpec((1, 1, d, hid), w_map),
                  pl.BlockSpec((1, 1, d, hid), w_map),
                  pl.BlockSpec((1, 1, hid, d), w_map)],
        out_specs=pl.BlockSpec((EXPERT_TILE,) + row, row_map),
        scratch_shapes=[pltpu.VMEM((d, hid), jnp.bfloat16), pltpu.VMEM((d, hid), jnp.bfloat16),
                        pltpu.VMEM((hid, d), jnp.bfloat16)],
    )
    return pl.pallas_call(
        _expert_kernel,
        name="moe_experts",
        grid_spec=grid_spec,
        out_shape=jax.ShapeDtypeStruct((n_rows,) + row, jnp.float32),
        compiler_params=pltpu.CompilerParams(dimension_semantics=("arbitrary",), vmem_limit_bytes=VMEM_LIMIT),
    )(tile_expert, n_used, x_disp, w1, w3, w2)


def _combine_kernel(dest_ref, dest_next_ref, x_ref, route_ref, g_ref, b_ref, yd_hbm, o_ref, ybuf, sem):
    i = pl.program_id(0)
    n = pl.num_programs(0)
    tm, d = x_ref.shape
    rows = EXPERT_TOP_K * tm

    def start_tile(dref, slot):
        def body(r, c):
            for k in range(EXPERT_TOP_K):
                row = k * tm + r
                pltpu.make_async_copy(yd_hbm.at[dref[0, 0, row]], ybuf.at[slot, row], sem.at[slot]).start(priority=k)
            return c
        lax.fori_loop(0, tm, body, 0, unroll=8)

    slot = i % 2

    @pl.when(i == 0)
    def _():
        start_tile(dest_ref, 0)

    @pl.when(i + 1 < n)
    def _():
        start_tile(dest_next_ref, 1 - slot)

    pltpu.make_async_copy(yd_hbm.at[pl.ds(0, rows)], ybuf.at[slot], sem.at[slot]).wait()
    y1 = ybuf[slot, 0:tm].reshape(tm, d)
    y2 = ybuf[slot, tm:rows].reshape(tm, d)
    route = route_ref[...].T
    y = DN_ALPHA * x_ref[...] + (y1 * route[:, 0:1] + y2 * route[:, 1:2])
    o_ref[...] = _ln_rows(y, g_ref[...], b_ref[...])


def _moe_combine_ln(xt, y_disp, dest2, route, ln_g, ln_b):
    t, d = xt.shape
    n = t // ROUTE_TILE
    rows = EXPERT_TOP_K * ROUTE_TILE
    return pl.pallas_call(
        _combine_kernel,
        name="moe_combine_ln",
        grid=(n,),
        in_specs=[pl.BlockSpec((1, 1, rows), lambda i: (i, 0, 0), memory_space=pltpu.SMEM),
                  pl.BlockSpec((1, 1, rows), lambda i: (jnp.minimum(i + 1, n - 1), 0, 0), memory_space=pltpu.SMEM),
                  pl.BlockSpec((ROUTE_TILE, d), lambda i: (i, 0)),
                  pl.BlockSpec((ROUTE_ROWS, ROUTE_TILE), lambda i: (0, i)),
                  pl.BlockSpec((1, d), lambda i: (0, 0)),
                  pl.BlockSpec((1, d), lambda i: (0, 0)),
                  pl.BlockSpec(memory_space=pl.ANY)],
        out_specs=pl.BlockSpec((ROUTE_TILE, d), lambda i: (i, 0)),
        out_shape=jax.ShapeDtypeStruct((t, d), jnp.float32),
        scratch_shapes=[pltpu.VMEM((2, rows) + y_disp.shape[1:], jnp.float32), pltpu.SemaphoreType.DMA((2,))],
        compiler_params=pltpu.CompilerParams(dimension_semantics=("arbitrary",), vmem_limit_bytes=VMEM_LIMIT),
    )(dest2, dest2, xt, route, ln_g.reshape(1, d), ln_b.reshape(1, d), y_disp)


def _moe_sublayer(x, route, cnt, w1, w3, w2, layer, ln_g, ln_b):
    b, s, d = x.shape
    t = b * s
    xt = x.reshape(t, d)
    counts = cnt[N_GROUPS:N_GROUPS + N_EXPERTS, 0].astype(jnp.int32)
    n_tiles = (t * EXPERT_TOP_K) // EXPERT_TILE + N_EXPERTS
    tiles_per = (counts + EXPERT_TILE - 1) // EXPERT_TILE
    tile_end = jnp.cumsum(tiles_per)
    pad_start = (tile_end - tiles_per) * EXPERT_TILE
    n_used = tile_end[-1:].astype(jnp.int32)
    tile_ids = jnp.minimum(jnp.arange(n_tiles), n_used[0] - 1)
    tile_expert = jnp.sum(tile_ids[:, None] >= tile_end[None, :], axis=1).astype(jnp.int32)
    experts = route[2:4].astype(jnp.int32)
    first_row = jnp.sum(jnp.where(experts[..., None] == jnp.arange(N_EXPERTS), pad_start, 0), axis=-1)
    dest = first_row + route[4:6].astype(jnp.int32)
    dest2 = jnp.swapaxes(dest.reshape(EXPERT_TOP_K, t // ROUTE_TILE, ROUTE_TILE), 0, 1)
    dest2 = dest2.reshape(t // ROUTE_TILE, 1, EXPERT_TOP_K * ROUTE_TILE)
    last_tiles = jnp.where(tiles_per > 0, tile_end - 1, -1)
    tail_tiles = n_used[0] + jnp.arange(N_EXPERTS)
    tail_tiles = jnp.where(tail_tiles < n_tiles, tail_tiles, -1)
    zero_tiles = jnp.concatenate([last_tiles, tail_tiles]).astype(jnp.int32)
    x_disp = _moe_dispatch(xt, dest2, zero_tiles, n_tiles * EXPERT_TILE)
    y_disp = _moe_experts(x_disp, tile_expert, n_used, w1, w3, w2, layer)
    return _moe_combine_ln(xt, y_disp, dest2, route, ln_g, ln_b).reshape(b, s, d)


def kernel(x, mem, mem_wk, mem_wv, ev_w_in, ev_conv_w, ev_conv_b, ev_cnorm_g, ev_cnorm_b, ev_cmp_pe_k, ev_cmp_w1_k, ev_cmp_w2_k, ev_cmp_pe_v, ev_cmp_w1_v, ev_cmp_w2_v, ev_w_out, od_w_in, od_conv_w, od_w_out, ln_mix_g, ln_mix_b, xa_wq, xa_wo, ln_xa_g, ln_xa_b, moe_wg, moe_bg, moe_we, moe_be, moe_w1, moe_w3, moe_w2, ln_ffn_g, ln_ffn_b):
    b, s, d = x.shape
    assert d == D_MODEL and d == SUBLANES * LANES, "one (8, 128) tile per MoE dispatch row"
    assert s % SEL_TILE == 0 and s % SEQ_TILE == 0 and SEL_TILE % SEQ_TILE == 0 and SEL_TILE % Q_BLOCK == 0
    assert (b * s) % ROUTE_TILE == 0 and (b * s * EXPERT_TOP_K) % EXPERT_TILE == 0
    mem_k, mem_v = _mem_kv(mem, mem_wk, mem_wv)
    for layer in range(DEPTH):
        i = layer // 2
        if layer % 2 == 0:
            a, kv_in, k_sel, k_win, q_t, v_sel_t, v_win_t, gate_t = _even_in_proj(
                x, ev_w_in[i], ev_conv_w[i], ev_conv_b[i], ev_cnorm_g[i], ev_cnorm_b[i])
            k_cmp, v_cmp_t = _compress_kv(kv_in, ev_cmp_pe_k[i], ev_cmp_w1_k[i], ev_cmp_w2_k[i], ev_cmp_pe_v[i], ev_cmp_w1_v[i], ev_cmp_w2_v[i])
            o = _nsa_attention(q_t, gate_t, k_cmp, v_cmp_t, k_sel, v_sel_t, k_win, v_win_t)
            x = _proj_residual_ln(a.reshape(b * s, -1), o.reshape(b * s, -1), x.reshape(b * s, d), ev_w_out[i], ln_mix_g[layer], ln_mix_b[layer]).reshape(b, s, d)
        else:
            x = _odd_mixer_sublayer(x, od_w_in[i], od_conv_w[i], od_w_out[i], ln_mix_g[layer], ln_mix_b[layer])
        x, route, cnt = _xattn_router_sublayer(x, mem_k, mem_v, xa_wq[layer], xa_wo[layer], ln_xa_g[layer], ln_xa_b[layer],
                                               moe_wg[layer], moe_bg[layer], moe_we[layer], moe_be[layer])
        x = _moe_sublayer(x, route, cnt, moe_w1, moe_w3, moe_w2, layer, ln_ffn_g[layer], ln_ffn_b[layer])
    return x
```
